```python
import math
import jax, jax.numpy as jnp
from jax import lax
import numpy as np

D_MODEL = 1024
BATCH = 1
SEQ = 16384
DEPTH = 2
DEC_BATCH = 8
DEC_SEQ = 64
PAST_LEN = 1024

CHUNK = 64
HEAD_DIM = 128
FOX_HEADS = 4
GDN_HEADS = 4
FOX_WIDTH = FOX_HEADS * HEAD_DIM
GDN_WIDTH = GDN_HEADS * HEAD_DIM
CONV_WIDTH = 4
Q_BLOCK = 128
FGATE_BIAS = 3.0
POOL_WINDOWS = (2, 4, 8, 16)
POOL_GROUPS = 4
POOL_GROUP_DIM = D_MODEL // POOL_GROUPS
POOL_STATE = 15
N_EXPERTS = 32
TOP_K = 4
D_EXPERT = D_MODEL
SWIGLU_LIMIT = 7.0
SWIGLU_ALPHA = 1.702
MOE_BLOCK = 128
PLE_DIM = 256
N_AB = (DEPTH + 1) // 2
N_C = DEPTH // 2
DN_ALPHA = (2 * DEPTH) ** 0.25
DN_BETA = (8 * DEPTH) ** -0.25
LN_EPS = 1e-5
NORM_EPS = 1e-6
NEG_INF = -1e30

_FQ = 0
_FK = _FQ + FOX_WIDTH
_FV = _FK + FOX_WIDTH
_FF = _FV + FOX_WIDTH
_GQ = _FF + FOX_HEADS
_GK = _GQ + GDN_WIDTH
_GV = _GK + GDN_WIDTH
_GZ = _GV + GDN_WIDTH
_GA = _GZ + GDN_WIDTH
_GB = _GA + GDN_HEADS
IN_AB = _GB + GDN_HEADS

kernel_name = "hybrid_fox_gdn_pool_moe_stream_step"

F32 = jnp.float32


def _layer_norm(x, g, b):
    xf = x.astype(F32)
    xc = xf - xf.mean(-1, keepdims=True)
    var = (xc * xc).mean(-1, keepdims=True)
    return (xc * lax.rsqrt(var + LN_EPS) * g.astype(F32) + b.astype(F32)).astype(x.dtype)


def _l2norm(x):
    return x * lax.rsqrt(jnp.sum(x * x, -1, keepdims=True) + NORM_EPS)


def _fox_attend(q, k, v, c_q, c_k, q_pos, k_pos):
    s = jnp.einsum('bqhd,bkhd->bhqk', q.astype(F32), k.astype(F32)) * (HEAD_DIM ** -0.5)
    s = s + jnp.swapaxes(c_q, 1, 2)[..., :, None] - jnp.swapaxes(c_k, 1, 2)[..., None, :]
    s = jnp.where(k_pos[None, :] <= q_pos[:, None], s, NEG_INF)
    p = jax.nn.softmax(s, axis=-1)
    return jnp.einsum('bhqk,bkhd->bqhd', p, v.astype(F32))


def _fox_mixer(q, k, v, logf, past_len):
    B, T, H, d = q.shape
    L = k.shape[1]
    c = jnp.cumsum(logf, axis=1)
    c_q = c[:, past_len:]
    k_pos = jnp.arange(L)
    q_pos = past_len + jnp.arange(T)
    qb = min(Q_BLOCK, T)
    nb = T // qb
    blocks = (jnp.swapaxes(q.reshape(B, nb, qb, H, d), 0, 1),
              jnp.swapaxes(c_q.reshape(B, nb, qb, H), 0, 1),
              q_pos.reshape(nb, qb))
    out = lax.map(lambda blk: _fox_attend(blk[0], k, v, blk[1], c, blk[2], k_pos), blocks)
    return jnp.swapaxes(out, 0, 1).reshape(B, T, H, d)


def _gdn_chunked(q, k, v, g, beta, s0):
    B, T, H, dk = q.shape
    dv = v.shape[-1]
    n = -(-T // CHUNK)
    pad = n * CHUNK - T

    def blk(a):
        a = jnp.moveaxis(a, 1, 2)
        a = jnp.pad(a, [(0, 0), (0, 0), (0, pad)] + [(0, 0)] * (a.ndim - 3))
        return a.reshape(a.shape[:2] + (n, CHUNK) + a.shape[3:])

    q, k, v, g, beta = blk(q), blk(k), blk(v), blk(g), blk(beta)
    gc = jnp.cumsum(g, -1)
    incl = jnp.tril(jnp.ones((CHUNK, CHUNK), bool))
    strict = jnp.tril(jnp.ones((CHUNK, CHUNK), bool), -1)
    diff = gc[..., :, None] - gc[..., None, :]
    decay = jnp.where(incl, jnp.exp(jnp.where(incl, diff, 0.0)), 0.0)
    kb = k * beta[..., None]
    vb = v * beta[..., None]
    lower = jnp.where(strict, jnp.einsum('bhncd,bhnsd->bhncs', kb, k) * decay, 0.0)
    a_mat = lower + jnp.eye(CHUNK, dtype=lower.dtype)
    u = lax.linalg.triangular_solve(a_mat, vb, left_side=True, lower=True, unit_diagonal=True)
    w = lax.linalg.triangular_solve(a_mat, kb * jnp.exp(gc)[..., None], left_side=True, lower=True,
                                    unit_diagonal=True)
    intra = jnp.where(incl, jnp.einsum('bhncd,bhnsd->bhncs', q, k) * decay, 0.0)
    qd = q * jnp.exp(gc)[..., None]
    kd = k * jnp.exp(gc[..., -1:] - gc)[..., None]
    glast = jnp.exp(gc[..., -1])

    def step(s, xs):
        qd_i, kd_i, u_i, w_i, a_i, gl_i = xs
        v_new = u_i - jnp.einsum('bhcd,bhde->bhce', w_i, s)
        o = jnp.einsum('bhcd,bhde->bhce', qd_i, s) + jnp.einsum('bhcs,bhse->bhce', a_i, v_new)
        s = s * gl_i[..., None, None] + jnp.einsum('bhcd,bhce->bhde', kd_i, v_new)
        return s, o

    xs = (jnp.moveaxis(qd, 2, 0), jnp.moveaxis(kd, 2, 0), jnp.moveaxis(u, 2, 0),
          jnp.moveaxis(w, 2, 0), jnp.moveaxis(intra, 2, 0), jnp.moveaxis(glast, 2, 0))
    s, o = lax.scan(step, s0, xs)
    o = jnp.moveaxis(o, 0, 2).reshape(B, H, n * CHUNK, dv)[:, :, :T]
    return jnp.moveaxis(o, 1, 2), s


def _ab_mixer(x, k_past, v_past, logf_past, s_past, conv_past,
              w_in, b_f, a_log, dt_bias, conv_w, norm_g, w_out):
    B, T, _ = x.shape
    P = k_past.shape[1]
    u = x @ w_in
    fq = u[..., _FQ:_FK].reshape(B, T, FOX_HEADS, HEAD_DIM)
    fk = u[..., _FK:_FV].reshape(B, T, FOX_HEADS, HEAD_DIM)
    fv = u[..., _FV:_FF].reshape(B, T, FOX_HEADS, HEAD_DIM)
    logf = jax.nn.log_sigmoid(u[..., _FF:_GQ].astype(F32) + b_f.astype(F32))
    k_all = jnp.concatenate([k_past.astype(fk.dtype), fk], 1)
    v_all = jnp.concatenate([v_past.astype(fv.dtype), fv], 1)
    lf_all = jnp.concatenate([logf_past.astype(F32), logf], 1)
    o_fox = _fox_mixer(fq, k_all, v_all, lf_all, P)
    pre = u[..., _GQ:_GZ]
    xc = jnp.concatenate([conv_past.astype(pre.dtype), pre], 1)
    conv = xc[:, 0:T] * conv_w[0]
    for j in range(1, CONV_WIDTH):
        conv = conv + xc[:, j:j + T] * conv_w[j]
    act = jax.nn.silu(conv.astype(F32))
    gq = _l2norm(act[..., :GDN_WIDTH].reshape(B, T, GDN_HEADS, HEAD_DIM)) * (HEAD_DIM ** -0.5)
    gk = _l2norm(act[..., GDN_WIDTH:2 * GDN_WIDTH].reshape(B, T, GDN_HEADS, HEAD_DIM))
    gv = act[..., 2 * GDN_WIDTH:].reshape(B, T, GDN_HEADS, HEAD_DIM)
    beta = jax.nn.sigmoid(u[..., _GB:IN_AB].astype(F32))
    g = -jnp.exp(a_log.astype(F32)) * jax.nn.softplus(u[..., _GA:_GB].astype(F32) + dt_bias.astype(F32))
    o_gdn, s_new = _gdn_chunked(gq, gk, gv, g, beta, s_past.astype(F32))
    z = u[..., _GZ:_GA].reshape(B, T, GDN_HEADS, HEAD_DIM).astype(F32)
    o_gdn = (o_gdn * lax.rsqrt(jnp.mean(o_gdn * o_gdn, -1, keepdims=True) + NORM_EPS)
             * norm_g.astype(F32) * jax.nn.silu(z))
    o = jnp.concatenate([o_fox.reshape(B, T, FOX_WIDTH), o_gdn.reshape(B, T, GDN_WIDTH)], -1).astype(x.dtype)
    y = o @ w_out
    return y, (fk, fv, logf.astype(x.dtype), s_new.astype(x.dtype), xc[:, -(CONV_WIDTH - 1):])


def _pool_mixer(x, pool_past, pos0, pool_w, pool_scale, w_out):
    B, T, D = x.shape
    xf = jnp.concatenate([pool_past.astype(x.dtype), x], 1)
    cs = jnp.pad(jnp.cumsum(xf.astype(F32), 1), ((0, 0), (1, 0), (0, 0)))
    end = cs[:, POOL_STATE + 1:]
    pos = pos0 + jnp.arange(T)
    groups = []
    for gi, w in enumerate(POOL_WINDOWS):
        lo, hi = gi * POOL_GROUP_DIM, (gi + 1) * POOL_GROUP_DIM
        st = POOL_STATE + 1 - w
        s = end[..., lo:hi] - cs[:, st:st + T, lo:hi]
        cnt = jnp.minimum(pos + 1, w).astype(F32)
        groups.append(s / cnt[None, :, None] - x[..., lo:hi].astype(F32))
    zg = jnp.stack(groups, 2)
    zg = jnp.einsum('btgc,gce->btge', zg, pool_w.astype(F32)).reshape(B, T, D) * pool_scale.astype(F32)
    y = zg.astype(x.dtype) @ w_out
    return y, xf[:, -POOL_STATE:]


def _moe(h, w_r, b_r, w_up, b_up, w_dn, b_dn):
    B, T, D = h.shape
    n_tok = B * T
    n_asg = n_tok * TOP_K
    xt = h.reshape(n_tok, D)
    logits = (xt @ w_r).astype(F32) + b_r.astype(F32)
    top_val, top_idx = lax.top_k(logits, TOP_K)
    gate = jax.nn.softmax(top_val, axis=-1)
    flat_e = top_idx.reshape(-1)
    order = jnp.argsort(flat_e)
    sorted_e = flat_e[order]
    counts = jnp.zeros((N_EXPERTS,), jnp.int32).at[flat_e].add(1)
    padded = (counts + MOE_BLOCK - 1) // MOE_BLOCK * MOE_BLOCK
    start = jnp.cumsum(counts) - counts
    pend = jnp.cumsum(padded)
    pstart = pend - padded
    slot_sorted = pstart[sorted_e] + jnp.arange(n_asg, dtype=jnp.int32) - start[sorted_e]
    slot = jnp.zeros((n_asg,), jnp.int32).at[order].set(slot_sorted)
    n_blocks = (n_asg + N_EXPERTS * (MOE_BLOCK - 1) + MOE_BLOCK - 1) // MOE_BLOCK
    n_slots = n_blocks * MOE_BLOCK
    slot_tok = jnp.zeros((n_slots,), jnp.int32).at[slot].set(jnp.arange(n_asg, dtype=jnp.int32) // TOP_K)
    block_e = jnp.minimum(jnp.searchsorted(pend, jnp.arange(n_blocks, dtype=jnp.int32) * MOE_BLOCK,
                                           side='right'), N_EXPERTS - 1)
    xb = xt[slot_tok].reshape(n_blocks, MOE_BLOCK, D)

    def expert(args):
        xe, e = args
        hu = xe @ w_up[e] + b_up[e]
        glu = jnp.minimum(hu[..., :D_EXPERT], SWIGLU_LIMIT)
        lin = jnp.clip(hu[..., D_EXPERT:], -SWIGLU_LIMIT, SWIGLU_LIMIT)
        a = glu * jax.nn.sigmoid(SWIGLU_ALPHA * glu) * (lin + 1.0)
        return a @ w_dn[e] + b_dn[e]

    yb = lax.map(expert, (xb, block_e)).reshape(n_slots, D)
    y = jnp.sum(yb[slot].reshape(n_tok, TOP_K, D).astype(F32) * gate[..., None], axis=1)
    return y.astype(h.dtype).reshape(B, T, D)


def _layer_tail(x, mix, p_i, g1, b1, g2, b2, w_r, b_r, w_up, b_up, w_dn, b_dn, w_pg, w_pp):
    h = _layer_norm(DN_ALPHA * x + mix, g1, b1)
    h = _layer_norm(DN_ALPHA * h + _moe(h, w_r, b_r, w_up, b_up, w_dn, b_dn), g2, b2)
    return h + jax.nn.sigmoid(h @ w_pg) * (p_i.astype(h.dtype) @ w_pp)


def setup_inputs(seed: int = 0) -> dict:
    key = jax.random.key(seed)
    ks = iter(jax.random.split(key, 48))

    def nrm(shape, scale):
        return jax.random.normal(next(ks), shape, F32) * scale

    D = D_MODEL
    G = POOL_GROUP_DIM
    dt = jnp.exp(jax.random.uniform(next(ks), (N_AB, GDN_HEADS), F32, math.log(1e-3), math.log(1e-1)))
    return {
        "x_prompt": nrm((BATCH, SEQ, D), 1.0),
        "x_sample": nrm((DEC_BATCH, DEC_SEQ, D), 1.0),
        "cache_fox_k": nrm((N_AB, DEC_BATCH, PAST_LEN, FOX_HEADS, HEAD_DIM), 1.0),
        "cache_fox_v": nrm((N_AB, DEC_BATCH, PAST_LEN, FOX_HEADS, HEAD_DIM), 1.0),
        "cache_fox_logf": jax.nn.log_sigmoid(FGATE_BIAS + nrm((N_AB, DEC_BATCH, PAST_LEN, FOX_HEADS), 1.0)),
        "state_gdn": nrm((N_AB, DEC_BATCH, GDN_HEADS, HEAD_DIM, HEAD_DIM), 0.1),
        "state_gdn_conv": nrm((N_AB, DEC_BATCH, CONV_WIDTH - 1, 3 * GDN_WIDTH), 1.0),
        "cache_pool": nrm((N_C, DEC_BATCH, POOL_STATE, D), 1.0),
        "p_prompt": nrm((DEPTH, BATCH, SEQ, PLE_DIM), 1.0),
        "p_sample": nrm((DEPTH, DEC_BATCH, DEC_SEQ, PLE_DIM), 1.0),
        "w_in_ab": nrm((N_AB, D, IN_AB), D ** -0.5),
        "b_fgate": FGATE_BIAS + nrm((N_AB, FOX_HEADS), 0.5),
        "gdn_a_log": jnp.log(jax.random.uniform(next(ks), (N_AB, GDN_HEADS), F32, 1.0, 16.0)),
        "gdn_dt_bias": dt + jnp.log(-jnp.expm1(-dt)),
        "gdn_conv_w": nrm((N_AB, CONV_WIDTH, 3 * GDN_WIDTH), CONV_WIDTH ** -0.5),
        "gdn_norm_g": 1.0 + nrm((N_AB, HEAD_DIM), 0.02),
        "w_out_ab": nrm((N_AB, D, D), D ** -0.5 * DN_BETA),
        "pool_w": nrm((N_C, POOL_GROUPS, G, G), G ** -0.5),
        "pool_scale": 1.0 + nrm((N_C, D), 0.02),
        "w_out_pool": nrm((N_C, D, D), D ** -0.5 * DN_BETA),
        "ln1_g": 1.0 + nrm((DEPTH, D), 0.02),
        "ln1_b": nrm((DEPTH, D), 0.02),
        "ln2_g": 1.0 + nrm((DEPTH, D), 0.02),
        "ln2_b": nrm((DEPTH, D), 0.02),
        "w_router": nrm((DEPTH, D, N_EXPERTS), D ** -0.5),
        "b_router": nrm((DEPTH, N_EXPERTS), 0.01),
        "w_expert_up": nrm((DEPTH, N_EXPERTS, D, 2 * D_EXPERT), D ** -0.5),
        "b_expert_up": nrm((DEPTH, N_EXPERTS, 2 * D_EXPERT), 0.02),
        "w_expert_down": nrm((DEPTH, N_EXPERTS, D_EXPERT, D), D_EXPERT ** -0.5 * DN_BETA),
        "b_expert_down": nrm((DEPTH, N_EXPERTS, D), 0.02),
        "w_ple_gate": nrm((DEPTH, D, D), D ** -0.5),
        "w_ple_proj": nrm((DEPTH, PLE_DIM, D), PLE_DIM ** -0.5),
    }


def reference(x_prompt, x_sample, cache_fox_k, cache_fox_v, cache_fox_logf, state_gdn, state_gdn_conv,
              cache_pool, p_prompt, p_sample, w_in_ab, b_fgate, gdn_a_log, gdn_dt_bias, gdn_conv_w,
              gdn_norm_g, w_out_ab, pool_w, pool_scale, w_out_pool, ln1_g, ln1_b, ln2_g, ln2_b,
              w_router, b_router, w_expert_up, b_expert_up, w_expert_down, b_expert_down,
              w_ple_gate, w_ple_proj):
    xp, xs = x_prompt, x_sample
    Bp = xp.shape[0]
    dt = xp.dtype
    fk_p, fv_p, lf_p, sg_p, cv_p, pl_p = [], [], [], [], [], []
    fk_s, fv_s, lf_s, sg_s, cv_s, pl_s = [], [], [], [], [], []
    for i in range(DEPTH):
        j = i // 2
        if i % 2 == 0:
            prm = (w_in_ab[j], b_fgate[j], gdn_a_log[j], gdn_dt_bias[j], gdn_conv_w[j], gdn_norm_g[j],
                   w_out_ab[j])
            mix_p, st_p = _ab_mixer(
                xp,
                jnp.zeros((Bp, 0, FOX_HEADS, HEAD_DIM), dt),
                jnp.zeros((Bp, 0, FOX_HEADS, HEAD_DIM), dt),
                jnp.zeros((Bp, 0, FOX_HEADS), F32),
                jnp.zeros((Bp, GDN_HEADS, HEAD_DIM, HEAD_DIM), F32),
                jnp.zeros((Bp, CONV_WIDTH - 1, 3 * GDN_WIDTH), dt),
                *prm)
            mix_s, st_s = _ab_mixer(xs, cache_fox_k[j], cache_fox_v[j], cache_fox_logf[j], state_gdn[j],
                                    state_gdn_conv[j], *prm)
            fk_p.append(st_p[0]); fv_p.append(st_p[1]); lf_p.append(st_p[2])
            sg_p.append(st_p[3]); cv_p.append(st_p[4])
            fk_s.append(st_s[0]); fv_s.append(st_s[1]); lf_s.append(st_s[2])
            sg_s.append(st_s[3]); cv_s.append(st_s[4])
        else:
            mix_p, ps_p = _pool_mixer(xp, jnp.zeros((Bp, POOL_STATE, D_MODEL), dt), 0,
                                      pool_w[j], pool_scale[j], w_out_pool[j])
            mix_s, ps_s = _pool_mixer(xs, cache_pool[j], PAST_LEN, pool_w[j], pool_scale[j], w_out_pool[j])
            pl_p.append(ps_p)
            pl_s.append(ps_s)
        tail = (ln1_g[i], ln1_b[i], ln2_g[i], ln2_b[i], w_router[i], b_router[i], w_expert_up[i],
                b_expert_up[i], w_expert_down[i], b_expert_down[i], w_ple_gate[i], w_ple_proj[i])
        xp = _layer_tail(xp, mix_p, p_prompt[i], *tail)
        xs = _layer_tail(xs, mix_s, p_sample[i], *tail)
    return (xp, xs,
            jnp.stack(fk_p), jnp.stack(fv_p), jnp.stack(lf_p), jnp.stack(sg_p), jnp.stack(cv_p), jnp.stack(pl_p),
            jnp.stack(fk_s), jnp.stack(fv_s), jnp.stack(lf_s), jnp.stack(sg_s), jnp.stack(cv_s), jnp.stack(pl_s))
```

```python
import functools

import numpy as np
import jax
import jax.numpy as jnp
from jax import lax
from jax.experimental import pallas as pl
from jax.experimental.pallas import tpu as pltpu

F32 = jnp.float32
BF16 = jnp.bfloat16
I32 = jnp.int32
HIGHEST = lax.Precision.HIGHEST

LANES = 128
SUBLANES = 8
VMEM_LIMIT = 56 * 1024 * 1024

HEAD_DIM = 128
N_HEADS = 4
WIDTH = N_HEADS * HEAD_DIM
CHUNK = 64
CONV_WIDTH = 4
POOL_WINDOWS = (2, 4, 8, 16)
POOL_HALO = 16
POOL_STATE = 15
N_EXPERTS = 32
TOP_K = 4
SWIGLU_LIMIT = 7.0
SWIGLU_ALPHA = 1.702
DEPTH = 2
DN_ALPHA = (2 * DEPTH) ** 0.25
LN_EPS = 1e-5
NORM_EPS = 1e-6
NEG_INF = -1e30

COL_Q, COL_K, COL_V = 0, WIDTH, 2 * WIDTH
COL_G = 3 * WIDTH
COL_Z = 6 * WIDTH
COL_S = 7 * WIDTH
U_COLS = COL_S + LANES
LANE_F, LANE_A, LANE_B = 0, N_HEADS, 2 * N_HEADS

TM_PROJ = 256
TM_TOK = 512
TM_ROW = 256
MOE_BLOCK = 256
TQ = 512
TK = 512


def _cparams(sem):
    return pltpu.CompilerParams(dimension_semantics=sem, vmem_limit_bytes=VMEM_LIMIT)


def _softplus(x):
    return jnp.maximum(x, 0.0) + jnp.log1p(jnp.exp(-jnp.abs(x)))


def _sigmoid(x):
    return 1.0 / (1.0 + jnp.exp(-x))


def _silu(x):
    return x * _sigmoid(x)


def _layer_norm(y, g, b):
    mu = jnp.mean(y, axis=-1, keepdims=True)
    yc = y - mu
    var = jnp.mean(yc * yc, axis=-1, keepdims=True)
    return yc * lax.rsqrt(var + LN_EPS) * g + b


def _dot(a, b):
    return jnp.dot(a.astype(BF16), b.astype(BF16), preferred_element_type=F32)


def _dot_nt(a, b):
    return lax.dot_general(a.astype(BF16), b.astype(BF16), (((1,), (1,)), ((), ())),
                           preferred_element_type=F32)


def _dot_tn(a, b):
    return lax.dot_general(a.astype(BF16), b.astype(BF16), (((0,), (0,)), ((), ())),
                           preferred_element_type=F32)


def _lanes_to_rows(x, lane0):
    r = lax.broadcasted_iota(I32, (SUBLANES, LANES), 0)
    c = lax.broadcasted_iota(I32, (SUBLANES, LANES), 1)
    sel = (c == r + lane0).astype(F32)
    return lax.dot_general(sel, x, (((1,), (1,)), ((), ())), precision=HIGHEST,
                           preferred_element_type=F32)


def _proj_kernel(x_ref, w_ref, bf_ref, u_ref):
    u = jnp.dot(x_ref[...].astype(BF16), w_ref[...], preferred_element_type=F32)
    u_ref[...] = u
    small = u[:, COL_S:]
    lane = lax.broadcasted_iota(I32, small.shape, 1)
    logf = -_softplus(-(small + bf_ref[...]))
    u_ref[:, COL_S:] = jnp.where(lane < LANE_A, logf, small)


def _proj(x, w_bf16, bf_row):
    n, d = x.shape
    m = w_bf16.shape[1]
    return pl.pallas_call(
        _proj_kernel,
        grid=(n // TM_PROJ,),
        in_specs=[pl.BlockSpec((TM_PROJ, d), lambda i: (i, 0)),
                  pl.BlockSpec((d, m), lambda i: (0, 0)),
                  pl.BlockSpec((1, LANES), lambda i: (0, 0))],
        out_specs=pl.BlockSpec((TM_PROJ, m), lambda i: (i, 0)),
        out_shape=jax.ShapeDtypeStruct((n, m), F32),
        compiler_params=_cparams(("parallel",)),
        name="in_proj",
    )(x, w_bf16, bf_row)


def _cumsum_kernel(lf_ref, c_ref, carry_ref):
    @pl.when(pl.program_id(1) == 0)
    def _():
        carry_ref[...] = jnp.zeros_like(carry_ref)

    lf = lf_ref[0]
    t = lf.shape[0]
    r = lax.broadcasted_iota(I32, (t, t), 0)
    c = lax.broadcasted_iota(I32, (t, t), 1)
    tril = (c <= r).astype(F32)
    cs = jnp.dot(tril, lf, precision=HIGHEST, preferred_element_type=F32) + carry_ref[0:1, :]
    c_ref[0] = cs
    carry_ref[...] = jnp.broadcast_to(cs[t - 1:t, :], carry_ref.shape)


def _cumsum(arr, n_batch, length, tl, col_block):
    return pl.pallas_call(
        _cumsum_kernel,
        grid=(n_batch, length // tl),
        in_specs=[pl.BlockSpec((1, tl, LANES), lambda b, j: (b, j, col_block))],
        out_specs=pl.BlockSpec((1, tl, LANES), lambda b, j: (b, j, 0)),
        out_shape=jax.ShapeDtypeStruct((n_batch, length, LANES), F32),
        scratch_shapes=[pltpu.VMEM((SUBLANES, LANES), F32)],
        compiler_params=_cparams(("parallel", "arbitrary")),
        name="logf_cumsum",
    )(arr)


def _fox_kernel(qi_ref, kj_ref, last_ref, q_ref, k_ref, v_ref, cq_ref, ck_ref, o_ref,
                m_ref, l_ref, acc_ref, *, tq, tk, past):
    s_idx = pl.program_id(1)
    qi = qi_ref[s_idx]
    kj = kj_ref[s_idx]

    @pl.when(kj == 0)
    def _():
        m_ref[...] = jnp.full_like(m_ref, NEG_INF)
        l_ref[...] = jnp.zeros_like(l_ref)
        acc_ref[...] = jnp.zeros_like(acc_ref)

    ck_rows = _lanes_to_rows(ck_ref[0], LANE_F)
    q_pos = past + qi * tq + lax.broadcasted_iota(I32, (tq, tk), 0)
    k_pos = kj * tk + lax.broadcasted_iota(I32, (tq, tk), 1)
    visible = k_pos <= q_pos
    scale = HEAD_DIM ** -0.5
    for h in range(N_HEADS):
        cols = slice(h * HEAD_DIM, (h + 1) * HEAD_DIM)
        s = _dot_nt(q_ref[0, :, cols], k_ref[0, :, cols]) * scale
        s = s + (cq_ref[0, :, LANE_F + h:LANE_F + h + 1] - ck_rows[h:h + 1, :])
        s = jnp.where(visible, s, NEG_INF)
        m_prev = m_ref[h]
        m_new = jnp.maximum(m_prev, jnp.max(s, axis=-1, keepdims=True))
        alpha = jnp.exp(m_prev - m_new)
        p = jnp.exp(s - m_new)
        l_ref[h] = alpha * l_ref[h] + jnp.sum(p, axis=-1, keepdims=True)
        acc_ref[:, cols] = alpha * acc_ref[:, cols] + _dot(p, v_ref[0, :, cols])
        m_ref[h] = m_new

    @pl.when(last_ref[s_idx] == 1)
    def _():
        for h in range(N_HEADS):
            cols = slice(h * HEAD_DIM, (h + 1) * HEAD_DIM)
            o_ref[0, :, cols] = acc_ref[:, cols] / l_ref[h]


def _fox_schedule(n_q, tq, tk, past):
    qi, kj, last = [], [], []
    for i in range(n_q):
        hi = (past + (i + 1) * tq - 1) // tk
        for j in range(hi + 1):
            qi.append(i)
            kj.append(j)
            last.append(1 if j == hi else 0)
    return (jnp.asarray(np.array(qi, np.int32)), jnp.asarray(np.array(kj, np.int32)),
            jnp.asarray(np.array(last, np.int32)))


def _fox(q_arr, q_map, k_arr, k_map, v_arr, v_map, cq_arr, cq_map, ck_arr, ck_map,
         n_batch, n_q, tq, tk, past):
    qi, kj, last = _fox_schedule(n_q, tq, tk, past)
    n_steps = int(qi.shape[0])
    spec = lambda shape, fn, tab: pl.BlockSpec(shape, lambda b, s, qi_r, kj_r, la_r: fn(b, (qi_r if tab == 'q' else kj_r)[s]))
    return pl.pallas_call(
        functools.partial(_fox_kernel, tq=tq, tk=tk, past=past),
        grid_spec=pltpu.PrefetchScalarGridSpec(
            num_scalar_prefetch=3,
            grid=(n_batch, n_steps),
            in_specs=[spec((1, tq, WIDTH), q_map, 'q'),
                      spec((1, tk, WIDTH), k_map, 'k'),
                      spec((1, tk, WIDTH), v_map, 'k'),
                      spec((1, tq, LANES), cq_map, 'q'),
                      spec((1, tk, LANES), ck_map, 'k')],
            out_specs=spec((1, tq, WIDTH), lambda b, i: (b, i, 0), 'q'),
            scratch_shapes=[pltpu.VMEM((N_HEADS, tq, 1), F32),
                            pltpu.VMEM((N_HEADS, tq, 1), F32),
                            pltpu.VMEM((tq, WIDTH), F32)]),
        out_shape=jax.ShapeDtypeStruct((n_batch, n_q * tq, WIDTH), F32),
        compiler_params=_cparams(("parallel", "arbitrary")),
        name="fox_attention",
    )(qi, kj, last, q_arr, k_arr, v_arr, cq_arr, ck_arr)


def _gdn_kernel(first_ref, last_ref, seq_ref,
                pre_ref, z_ref, sm_ref, cpast_ref, convw_ref, s0_ref, alog_ref, dtb_ref, ng_ref,
                o_ref, sout_ref, stage_ref, s_ref):
    step = pl.program_id(0)
    halo = SUBLANES

    @pl.when(first_ref[step] == 1)
    def _():
        stage_ref[0:halo, :] = cpast_ref[0]
        s_ref[...] = s0_ref[0]

    stage_ref[halo:halo + CHUNK, :] = pre_ref[0]
    conv = stage_ref[halo:halo + CHUNK, :] * convw_ref[CONV_WIDTH - 1:CONV_WIDTH, :]
    for j in range(1, CONV_WIDTH):
        conv = conv + (stage_ref[halo - j:halo - j + CHUNK, :]
                       * convw_ref[CONV_WIDTH - 1 - j:CONV_WIDTH - j, :])
    stage_ref[0:halo, :] = stage_ref[CHUNK:CHUNK + halo, :]
    act = _silu(conv)

    small = sm_ref[0]
    beta_all = _sigmoid(small)
    g_all = -jnp.exp(alog_ref[...]) * _softplus(small + dtb_ref[...])
    r = lax.broadcasted_iota(I32, (CHUNK, CHUNK), 0)
    c = lax.broadcasted_iota(I32, (CHUNK, CHUNK), 1)
    incl = c <= r
    strict = c < r
    eye = (c == r).astype(F32)
    gc_all = jnp.dot(incl.astype(F32), g_all, precision=HIGHEST, preferred_element_type=F32)
    gc_rows = _lanes_to_rows(gc_all, LANE_A)

    for h in range(N_HEADS):
        cols = slice(h * HEAD_DIM, (h + 1) * HEAD_DIM)
        q = act[:, h * HEAD_DIM:(h + 1) * HEAD_DIM]
        k = act[:, WIDTH + h * HEAD_DIM:WIDTH + (h + 1) * HEAD_DIM]
        v = act[:, 2 * WIDTH + h * HEAD_DIM:2 * WIDTH + (h + 1) * HEAD_DIM]
        q = q * lax.rsqrt(jnp.sum(q * q, axis=-1, keepdims=True) + NORM_EPS) * (HEAD_DIM ** -0.5)
        k = k * lax.rsqrt(jnp.sum(k * k, axis=-1, keepdims=True) + NORM_EPS)
        beta = beta_all[:, LANE_B + h:LANE_B + h + 1]
        gc = gc_all[:, LANE_A + h:LANE_A + h + 1]
        diff = gc - gc_rows[h:h + 1, :]
        decay = jnp.where(incl, jnp.exp(jnp.where(incl, diff, 0.0)), 0.0)
        kb = k * beta
        vb = v * beta
        low = jnp.where(strict, _dot_nt(kb, k) * decay, 0.0)
        inv = eye - low
        pw = _dot(low, low)
        n_sq = CHUNK.bit_length() - 2
        for it in range(n_sq):
            inv = inv + _dot(inv, pw)
            if it + 1 < n_sq:
                pw = _dot(pw, pw)
        egc = jnp.exp(gc)
        u = _dot(inv, vb)
        w = _dot(inv, kb * egc)
        intra = jnp.where(incl, _dot_nt(q, k) * decay, 0.0)
        qd = q * egc
        g_last = gc[CHUNK - 1:CHUNK, :]
        kd = k * jnp.exp(g_last - gc)
        state = s_ref[h]
        v_new = u - _dot(w, state)
        o = _dot(qd, state) + _dot(intra, v_new)
        s_ref[h] = state * jnp.exp(g_last) + _dot_tn(kd, v_new)
        o = (o * lax.rsqrt(jnp.mean(o * o, axis=-1, keepdims=True) + NORM_EPS)
             * ng_ref[...] * _silu(z_ref[0, :, cols]))
        o_ref[0, :, cols] = o

    @pl.when(last_ref[step] == 1)
    def _():
        sout_ref[0] = s_ref[...]


def _gdn(u64, first, last, seq, conv_past, conv_w, s0, alog_row, dtb_row, ng_row):
    n_steps = u64.shape[0]
    n_seq = s0.shape[0]
    gw = 3 * WIDTH
    return pl.pallas_call(
        _gdn_kernel,
        grid_spec=pltpu.PrefetchScalarGridSpec(
            num_scalar_prefetch=3,
            grid=(n_steps,),
            in_specs=[pl.BlockSpec((1, CHUNK, gw), lambda s, f, l, q: (s, 0, COL_G // gw)),
                      pl.BlockSpec((1, CHUNK, WIDTH), lambda s, f, l, q: (s, 0, COL_Z // WIDTH)),
                      pl.BlockSpec((1, CHUNK, LANES), lambda s, f, l, q: (s, 0, COL_S // LANES)),
                      pl.BlockSpec((1, SUBLANES, gw), lambda s, f, l, q: (q[s], 0, 0)),
                      pl.BlockSpec((SUBLANES, gw), lambda s, f, l, q: (0, 0)),
                      pl.BlockSpec((1, N_HEADS, HEAD_DIM, HEAD_DIM), lambda s, f, l, q: (q[s], 0, 0, 0)),
                      pl.BlockSpec((1, LANES), lambda s, f, l, q: (0, 0)),
                      pl.BlockSpec((1, LANES), lambda s, f, l, q: (0, 0)),
                      pl.BlockSpec((1, LANES), lambda s, f, l, q: (0, 0))],
            out_specs=[pl.BlockSpec((1, CHUNK, WIDTH), lambda s, f, l, q: (s, 0, 0)),
                       pl.BlockSpec((1, N_HEADS, HEAD_DIM, HEAD_DIM), lambda s, f, l, q: (q[s], 0, 0, 0))],
            scratch_shapes=[pltpu.VMEM((CHUNK + SUBLANES, gw), F32),
                            pltpu.VMEM((N_HEADS, HEAD_DIM, HEAD_DIM), F32)]),
        out_shape=[jax.ShapeDtypeStruct((n_steps, CHUNK, WIDTH), F32),
                   jax.ShapeDtypeStruct((n_seq, N_HEADS, HEAD_DIM, HEAD_DIM), F32)],
        compiler_params=_cparams(("arbitrary",)),
        name="gated_deltanet",
    )(first, last, seq, u64, u64, u64, conv_past, conv_w, s0, alog_row, dtb_row, ng_row)


def _outproj_ln_kernel(x_ref, of_ref, og_ref, w_ref, g_ref, b_ref, h_ref):
    mix = _dot(of_ref[...], w_ref[0:WIDTH, :]) + _dot(og_ref[...], w_ref[WIDTH:2 * WIDTH, :])
    h_ref[...] = _layer_norm(DN_ALPHA * x_ref[...] + mix, g_ref[...], b_ref[...])


def _outproj_ln(x, o_fox, o_gdn, w_bf16, g_row, b_row):
    n, d = x.shape
    row = lambda i: (i, 0)
    fixed = lambda i: (0, 0)
    return pl.pallas_call(
        _outproj_ln_kernel,
        grid=(n // TM_TOK,),
        in_specs=[pl.BlockSpec((TM_TOK, d), row), pl.BlockSpec((TM_TOK, WIDTH), row),
                  pl.BlockSpec((TM_TOK, WIDTH), row), pl.BlockSpec((2 * WIDTH, d), fixed),
                  pl.BlockSpec((1, d), fixed), pl.BlockSpec((1, d), fixed)],
        out_specs=pl.BlockSpec((TM_TOK, d), row),
        out_shape=jax.ShapeDtypeStruct((n, d), F32),
        compiler_params=_cparams(("parallel",)),
        name="out_proj_ln",
    )(x, o_fox, o_gdn, w_bf16, g_row, b_row)


def _pool_ln_kernel(x_ref, halo_ref, pw_ref, ps_ref, w_ref, g_ref, b_ref, h_ref, stage_ref,
                    *, tm, pos0, zero_first_halo):
    i = pl.program_id(0)
    stage_ref[0:POOL_HALO, :] = halo_ref[0]
    if zero_first_halo:
        @pl.when(i == 0)
        def _():
            stage_ref[0:POOL_HALO, :] = jnp.zeros((POOL_HALO, stage_ref.shape[1]), F32)
    x = x_ref[...]
    stage_ref[POOL_HALO:POOL_HALO + tm, :] = x
    gdim = x.shape[1] // len(POOL_WINDOWS)
    pos = pos0 + lax.broadcasted_iota(I32, (tm, 1), 0)
    if zero_first_halo:
        pos = pos + i * tm
    parts = []
    for gi, win in enumerate(POOL_WINDOWS):
        cols = slice(gi * gdim, (gi + 1) * gdim)
        s = stage_ref[POOL_HALO:POOL_HALO + tm, cols]
        for j in range(1, win):
            s = s + stage_ref[POOL_HALO - j:POOL_HALO - j + tm, cols]
        cnt = jnp.minimum(pos + 1, win).astype(F32)
        zg = s / cnt - x[:, cols]
        parts.append(_dot(zg, pw_ref[gi]))
    zg = jnp.concatenate(parts, axis=-1) * ps_ref[...]
    mix = _dot(zg, w_ref[...])
    h_ref[...] = _layer_norm(DN_ALPHA * x + mix, g_ref[...], b_ref[...])


def _pool_ln(x, x_map, halo_arr, halo_map, n_tiles, tm, pos0, zero_first_halo,
             pw_bf16, ps_row, w_bf16, g_row, b_row):
    d = x.shape[1]
    gdim = d // len(POOL_WINDOWS)
    fixed = lambda i: (0, 0)
    return pl.pallas_call(
        functools.partial(_pool_ln_kernel, tm=tm, pos0=pos0, zero_first_halo=zero_first_halo),
        grid=(n_tiles,),
        in_specs=[pl.BlockSpec((tm, d), x_map),
                  pl.BlockSpec((1, POOL_HALO, d), halo_map),
                  pl.BlockSpec((len(POOL_WINDOWS), gdim, gdim), lambda i: (0, 0, 0)),
                  pl.BlockSpec((1, d), fixed), pl.BlockSpec((d, d), fixed),
                  pl.BlockSpec((1, d), fixed), pl.BlockSpec((1, d), fixed)],
        out_specs=pl.BlockSpec((tm, d), lambda i: (i, 0)),
        out_shape=jax.ShapeDtypeStruct((n_tiles * tm, d), F32),
        scratch_shapes=[pltpu.VMEM((POOL_HALO + tm, d), F32)],
        compiler_params=_cparams(("arbitrary",)),
        name="pool_mixer_ln",
    )(x, halo_arr, pw_bf16, ps_row, w_bf16, g_row, b_row)


def _router_kernel(h_ref, wr_ref, br_ref, idx_ref, gate_ref, rank_ref, cnt_ref, carry_ref):
    @pl.when(pl.program_id(0) == 0)
    def _():
        carry_ref[...] = jnp.zeros_like(carry_ref)

    tm = h_ref.shape[0]
    lane = lax.broadcasted_iota(I32, (tm, LANES), 1).astype(F32)
    logits = jnp.dot(h_ref[...], wr_ref[...], precision=HIGHEST, preferred_element_type=F32)
    work = jnp.where(lane < N_EXPERTS, logits + br_ref[...], -jnp.inf)
    vals, ids = [], []
    for _ in range(TOP_K):
        m = jnp.max(work, axis=-1, keepdims=True)
        ik = jnp.min(jnp.where(work == m, lane, float(LANES)), axis=-1, keepdims=True)
        vals.append(m)
        ids.append(ik)
        work = jnp.where(lane == ik, -jnp.inf, work)
    exps = [jnp.exp(v - vals[0]) for v in vals]
    denom = exps[0]
    for e in exps[1:]:
        denom = denom + e
    multihot = jnp.zeros((tm, LANES), F32)
    idx_out = jnp.zeros((tm, LANES), F32)
    gate_out = jnp.zeros((tm, LANES), F32)
    for k in range(TOP_K):
        multihot = multihot + (lane == ids[k]).astype(F32)
        idx_out = jnp.where(lane == k, ids[k], idx_out)
        gate_out = jnp.where(lane == k, exps[k] / denom, gate_out)
    r = lax.broadcasted_iota(I32, (tm, tm), 0)
    c = lax.broadcasted_iota(I32, (tm, tm), 1)
    before = _dot((c < r).astype(F32), multihot) + carry_ref[0:1, :]
    rank_out = jnp.zeros((tm, LANES), F32)
    for k in range(TOP_K):
        rk = jnp.sum(jnp.where(lane == ids[k], before, 0.0), axis=-1, keepdims=True)
        rank_out = jnp.where(lane == k, rk, rank_out)
    idx_ref[...] = idx_out.astype(I32)
    gate_ref[...] = gate_out
    rank_ref[...] = rank_out.astype(I32)
    total = carry_ref[0:1, :] + jnp.sum(multihot, axis=0, keepdims=True)
    carry_ref[...] = jnp.broadcast_to(total, carry_ref.shape)
    cnt_ref[...] = jnp.broadcast_to(total, cnt_ref.shape).astype(I32)


def _router(h, wr_pad, br_row):
    n, d = h.shape
    row = lambda i: (i, 0)
    fixed = lambda i: (0, 0)
    return pl.pallas_call(
        _router_kernel,
        grid=(n // TM_TOK,),
        in_specs=[pl.BlockSpec((TM_TOK, d), row), pl.BlockSpec((d, LANES), fixed),
                  pl.BlockSpec((1, LANES), fixed)],
        out_specs=[pl.BlockSpec((TM_TOK, LANES), row), pl.BlockSpec((TM_TOK, LANES), row),
                   pl.BlockSpec((TM_TOK, LANES), row), pl.BlockSpec((SUBLANES, LANES), fixed)],
        out_shape=[jax.ShapeDtypeStruct((n, LANES), I32), jax.ShapeDtypeStruct((n, LANES), F32),
                   jax.ShapeDtypeStruct((n, LANES), I32), jax.ShapeDtypeStruct((SUBLANES, LANES), I32)],
        scratch_shapes=[pltpu.VMEM((SUBLANES, LANES), F32)],
        compiler_params=_cparams(("arbitrary",)),
        name="moe_router",
    )(h, wr_pad, br_row)


def _row_copy(src_ref, src_row, dst_ref, dst_row, sem):
    return pltpu.make_async_copy(src_ref.at[pl.ds(src_row, 1)], dst_ref.at[pl.ds(dst_row, 1)], sem)


def _dispatch_kernel(slot_ref, h_ref, xb_in_ref, xb_ref, sem):
    del xb_in_ref
    tm = h_ref.shape[0]
    base = pl.program_id(0) * tm * TOP_K

    def issue(r, carry):
        for k in range(TOP_K):
            _row_copy(h_ref, r, xb_ref, slot_ref[base + r * TOP_K + k], sem).start()
        return carry

    def drain(r, carry):
        for k in range(TOP_K):
            _row_copy(h_ref, r, xb_ref, slot_ref[base + r * TOP_K + k], sem).wait()
        return carry

    lax.fori_loop(0, tm, issue, 0)
    lax.fori_loop(0, tm, drain, 0)


def _dispatch(h, slot_flat, xb_init):
    n, d = h.shape
    return pl.pallas_call(
        _dispatch_kernel,
        grid_spec=pltpu.PrefetchScalarGridSpec(
            num_scalar_prefetch=1,
            grid=(n // TM_ROW,),
            in_specs=[pl.BlockSpec((TM_ROW, d), lambda i, s: (i, 0)),
                      pl.BlockSpec(memory_space=pl.ANY)],
            out_specs=pl.BlockSpec(memory_space=pl.ANY),
            scratch_shapes=[pltpu.SemaphoreType.DMA(())]),
        out_shape=jax.ShapeDtypeStruct(xb_init.shape, F32),
        input_output_aliases={2: 0},
        compiler_params=_cparams(("arbitrary",)),
        name="moe_dispatch",
    )(slot_flat, h, xb_init)


def _expert_kernel(be_ref, nu_ref, xb_ref, wup_ref, bup_ref, wdn_ref, bdn_ref, yb_ref,
                   wup_bf_ref, wdn_bf_ref):
    b = pl.program_id(0)

    @pl.when(b < nu_ref[0])
    def _():
        prev = be_ref[jnp.maximum(b - 1, 0)]

        @pl.when((b == 0) | (be_ref[b] != prev))
        def _():
            wup_bf_ref[...] = wup_ref[0].astype(BF16)
            wdn_bf_ref[...] = wdn_ref[0].astype(BF16)

        d_exp = wdn_ref.shape[1]
        hu = jnp.dot(xb_ref[...].astype(BF16), wup_bf_ref[...], preferred_element_type=F32) + bup_ref[0]
        glu = jnp.minimum(hu[:, :d_exp], SWIGLU_LIMIT)
        lin = jnp.clip(hu[:, d_exp:], -SWIGLU_LIMIT, SWIGLU_LIMIT)
        a = glu * _sigmoid(SWIGLU_ALPHA * glu) * (lin + 1.0)
        yb_ref[...] = jnp.dot(a.astype(BF16), wdn_bf_ref[...], preferred_element_type=F32) + bdn_ref[0]

    @pl.when(b >= nu_ref[0])
    def _():
        yb_ref[...] = jnp.zeros_like(yb_ref)


def _experts(xb, block_e, n_used, w_up, b_up, w_dn, b_dn):
    n_slots, d = xb.shape
    n_blocks = n_slots // MOE_BLOCK
    d_up = w_up.shape[2]
    d_exp = w_dn.shape[1]
    blk = lambda b, be, nu: (jnp.minimum(b, nu[0] - 1), 0)
    wsel = lambda b, be, nu: (be[b], 0, 0)
    return pl.pallas_call(
        _expert_kernel,
        grid_spec=pltpu.PrefetchScalarGridSpec(
            num_scalar_prefetch=2,
            grid=(n_blocks,),
            in_specs=[pl.BlockSpec((MOE_BLOCK, d), blk),
                      pl.BlockSpec((1, d, d_up), wsel),
                      pl.BlockSpec((1, 1, d_up), wsel),
                      pl.BlockSpec((1, d_exp, d), wsel),
                      pl.BlockSpec((1, 1, d), wsel)],
            out_specs=pl.BlockSpec((MOE_BLOCK, d), lambda b, be, nu: (b, 0)),
            scratch_shapes=[pltpu.VMEM((d, d_up), BF16), pltpu.VMEM((d_exp, d), BF16)]),
        out_shape=jax.ShapeDtypeStruct((n_slots, d), F32),
        compiler_params=_cparams(("arbitrary",)),
        name="moe_experts",
    )(block_e, n_used, xb, w_up, b_up, w_dn, b_dn)


def _combine_kernel(slot_ref, h_ref, gate_ref, p_ref, wpg_ref, wpp_ref, g_ref, b_ref, yb_ref,
                    out_ref, buf_ref, sem):
    tm = h_ref.shape[0]
    base = pl.program_id(0) * tm * TOP_K

    def issue(r, carry):
        for k in range(TOP_K):
            _row_copy(yb_ref, slot_ref[base + r * TOP_K + k], buf_ref.at[k], r, sem).start()
        return carry

    def drain(r, carry):
        for k in range(TOP_K):
            _row_copy(yb_ref, slot_ref[base + r * TOP_K + k], buf_ref.at[k], r, sem).wait()
        return carry

    lax.fori_loop(0, tm, issue, 0)
    lax.fori_loop(0, tm, drain, 0)
    gate = gate_ref[...]
    moe = buf_ref[0] * gate[:, 0:1]
    for k in range(1, TOP_K):
        moe = moe + buf_ref[k] * gate[:, k:k + 1]
    h2 = _layer_norm(DN_ALPHA * h_ref[...] + moe, g_ref[...], b_ref[...])
    out_ref[...] = h2 + _sigmoid(_dot(h2, wpg_ref[...])) * _dot(p_ref[...], wpp_ref[...])


def _combine(h, gate, p, wpg_bf16, wpp_bf16, g_row, b_row, yb, slot_flat):
    n, d = h.shape
    row = lambda i, s: (i, 0)
    fixed = lambda i, s: (0, 0)
    return pl.pallas_call(
        _combine_kernel,
        grid_spec=pltpu.PrefetchScalarGridSpec(
            num_scalar_prefetch=1,
            grid=(n // TM_ROW,),
            in_specs=[pl.BlockSpec((TM_ROW, d), row), pl.BlockSpec((TM_ROW, LANES), row),
                      pl.BlockSpec((TM_ROW, p.shape[1]), row), pl.BlockSpec((d, d), fixed),
                      pl.BlockSpec((p.shape[1], d), fixed), pl.BlockSpec((1, d), fixed),
                      pl.BlockSpec((1, d), fixed), pl.BlockSpec(memory_space=pl.ANY)],
            out_specs=pl.BlockSpec((TM_ROW, d), row),
            scratch_shapes=[pltpu.VMEM((TOP_K, TM_ROW, d), F32), pltpu.SemaphoreType.DMA(())]),
        out_shape=jax.ShapeDtypeStruct((n, d), F32),
        compiler_params=_cparams(("arbitrary",)),
        name="moe_combine_ln_embed",
    )(slot_flat, h, gate, p, wpg_bf16, wpp_bf16, g_row, b_row, yb)


def _layer_tail(h, p, g2, b2, w_r, b_r, w_up, b_up, w_dn, b_dn, w_pg, w_pp):
    n, d = h.shape
    wr_pad = jnp.pad(w_r, ((0, 0), (0, LANES - N_EXPERTS)))
    br_row = jnp.pad(b_r, (0, LANES - N_EXPERTS))[None]
    idx, gate, rank, cnt = _router(h, wr_pad, br_row)
    counts = cnt[0, :N_EXPERTS]
    padded = (counts + MOE_BLOCK - 1) // MOE_BLOCK * MOE_BLOCK
    pend = jnp.cumsum(padded)
    pstart = pend - padded
    slot = (pstart[idx[:, :TOP_K]] + rank[:, :TOP_K]).reshape(-1).astype(I32)
    n_blocks = (n * TOP_K + N_EXPERTS * (MOE_BLOCK - 1) + MOE_BLOCK - 1) // MOE_BLOCK
    n_used = (pend[-1] // MOE_BLOCK).astype(I32)
    blk = jnp.minimum(jnp.arange(n_blocks, dtype=I32), n_used - 1) * MOE_BLOCK
    block_e = jnp.minimum(jnp.searchsorted(pend, blk, side='right'), N_EXPERTS - 1).astype(I32)
    xb = _dispatch(h, slot, jnp.zeros((n_blocks * MOE_BLOCK, d), F32))
    yb = _experts(xb, block_e, n_used[None], w_up, b_up[:, None, :], w_dn, b_dn[:, None, :])
    return _combine(h, gate, p, w_pg.astype(BF16), w_pp.astype(BF16), g2[None], b2[None], yb, slot)


def _lane_row(v, lane0):
    return jnp.zeros((1, LANES), F32).at[0, lane0:lane0 + v.shape[0]].set(v.astype(F32))


def kernel(x_prompt, x_sample, cache_fox_k, cache_fox_v, cache_fox_logf, state_gdn, state_gdn_conv,
           cache_pool, p_prompt, p_sample, w_in_ab, b_fgate, gdn_a_log, gdn_dt_bias, gdn_conv_w,
           gdn_norm_g, w_out_ab, pool_w, pool_scale, w_out_pool, ln1_g, ln1_b, ln2_g, ln2_b,
           w_router, b_router, w_expert_up, b_expert_up, w_expert_down, b_expert_down,
           w_ple_gate, w_ple_proj):
    n_pb, seq, d = x_prompt.shape
    n_sb, dseq, _ = x_sample.shape
    past = cache_fox_k.shape[2]
    assert n_pb == 1 and dseq == CHUNK and past % dseq == 0 and seq % TQ == 0
    n_p = n_pb * seq
    n_s = n_sb * dseq
    n = n_p + n_s
    x = jnp.concatenate([x_prompt.reshape(n_p, d), x_sample.reshape(n_s, d)], axis=0)

    def tail(h, i):
        p = jnp.concatenate([p_prompt[i].reshape(n_p, -1), p_sample[i].reshape(n_s, -1)], axis=0)
        return _layer_tail(h, p, ln2_g[i], ln2_b[i], w_router[i], b_router[i], w_expert_up[i],
                           b_expert_up[i], w_expert_down[i], b_expert_down[i], w_ple_gate[i],
                           w_ple_proj[i])

    w_in = w_in_ab[0]
    n_small = 3 * N_HEADS
    ff0 = 3 * WIDTH
    gq0 = ff0 + N_HEADS
    ga0 = gq0 + 4 * WIDTH
    w_small = jnp.concatenate([w_in[:, ff0:gq0], w_in[:, ga0:ga0 + 2 * N_HEADS],
                               jnp.zeros((d, LANES - n_small), F32)], axis=1)
    w_all = jnp.concatenate([w_in[:, :ff0], w_in[:, gq0:ga0], w_small], axis=1).astype(BF16)
    u = _proj(x, w_all, _lane_row(b_fgate[0], LANE_F))

    u3 = u[None]
    u64 = u.reshape(n // CHUNK, CHUNK, U_COLS)
    c_p = _cumsum(u3, 1, n_p, TK, COL_S // LANES)
    lf_s = jnp.concatenate(
        [jnp.pad(cache_fox_logf[0].astype(F32), ((0, 0), (0, 0), (0, LANES - N_HEADS))),
         u[n_p:, COL_S:].reshape(n_sb, dseq, LANES)], axis=1)
    c_s = _cumsum(lf_s, n_sb, past + dseq, past + dseq, 0)

    of_p = _fox(u3, lambda b, i: (0, i, COL_Q // WIDTH), u3, lambda b, j: (0, j, COL_K // WIDTH),
                u3, lambda b, j: (0, j, COL_V // WIDTH), c_p, lambda b, i: (0, i, 0),
                c_p, lambda b, j: (0, j, 0), 1, n_p // TQ, TQ, TK, 0)
    k_new = u[n_p:, COL_K:COL_K + WIDTH].reshape(n_sb, dseq, WIDTH)
    v_new = u[n_p:, COL_V:COL_V + WIDTH].reshape(n_sb, dseq, WIDTH)
    k_all = jnp.concatenate([cache_fox_k[0].reshape(n_sb, past, WIDTH), k_new], axis=1)
    v_all = jnp.concatenate([cache_fox_v[0].reshape(n_sb, past, WIDTH), v_new], axis=1)
    of_s = _fox(u64, lambda b, i: (n_p // CHUNK + b, 0, COL_Q // WIDTH), k_all, lambda b, j: (b, 0, 0),
                v_all, lambda b, j: (b, 0, 0), c_s, lambda b, i: (b, past // dseq, 0),
                c_s, lambda b, j: (b, 0, 0), n_sb, 1, dseq, past + dseq, past)
    o_fox = jnp.concatenate([of_p.reshape(n_p, WIDTH), of_s.reshape(n_s, WIDTH)], axis=0)

    n_pc = n_p // CHUNK
    seq_id = np.concatenate([np.zeros(n_pc, np.int32), 1 + np.arange(n_sb, dtype=np.int32)])
    first = np.concatenate([[1], (seq_id[1:] != seq_id[:-1])]).astype(np.int32)
    last = np.concatenate([(seq_id[1:] != seq_id[:-1]), [1]]).astype(np.int32)
    gw = 3 * WIDTH
    conv_past = jnp.concatenate(
        [jnp.zeros((1, SUBLANES, gw), F32),
         jnp.pad(state_gdn_conv[0].astype(F32), ((0, 0), (SUBLANES - (CONV_WIDTH - 1), 0), (0, 0)))], axis=0)
    conv_w = jnp.pad(gdn_conv_w[0], ((0, SUBLANES - CONV_WIDTH), (0, 0)))
    s0 = jnp.concatenate([jnp.zeros((1, N_HEADS, HEAD_DIM, HEAD_DIM), F32), state_gdn[0].astype(F32)], axis=0)
    o_gdn, s_out = _gdn(u64, jnp.asarray(first), jnp.asarray(last), jnp.asarray(seq_id), conv_past, conv_w, s0,
                        _lane_row(gdn_a_log[0], LANE_A), _lane_row(gdn_dt_bias[0], LANE_A),
                        gdn_norm_g[0][None])
    h = _outproj_ln(x, o_fox, o_gdn.reshape(n, WIDTH), w_out_ab[0].astype(BF16), ln1_g[0][None], ln1_b[0][None])
    x1 = tail(h, 0)

    pool_args = (pool_w[0].astype(BF16), pool_scale[0][None], w_out_pool[0].astype(BF16),
                 ln1_g[1][None], ln1_b[1][None])
    ratio = TM_TOK // POOL_HALO
    x1_halo = x1.reshape(n // POOL_HALO, POOL_HALO, d)
    h_p = _pool_ln(x1, lambda i: (i, 0), x1_halo, lambda i: (jnp.maximum(i * ratio - 1, 0), 0, 0),
                   n_p // TM_TOK, TM_TOK, 0, True, *pool_args)
    cache16 = jnp.pad(cache_pool[0].astype(F32), ((0, 0), (POOL_HALO - POOL_STATE, 0), (0, 0)))
    h_s = _pool_ln(x1, lambda i: (n_p // dseq + i, 0), cache16, lambda i: (i, 0, 0),
                   n_sb, dseq, past, False, *pool_args)
    x2 = tail(jnp.concatenate([h_p, h_s], axis=0), 1)

    up = u[:n_p]
    us = u[n_p:]
    return (x2[:n_p].reshape(n_pb, seq, d), x2[n_p:].reshape(n_sb, dseq, d),
            up[:, COL_K:COL_K + WIDTH].reshape(1, n_pb, seq, N_HEADS, HEAD_DIM),
            up[:, COL_V:COL_V + WIDTH].reshape(1, n_pb, seq, N_HEADS, HEAD_DIM),
            up[:, COL_S:COL_S + N_HEADS].reshape(1, n_pb, seq, N_HEADS),
            s_out[0:1].reshape(1, n_pb, N_HEADS, HEAD_DIM, HEAD_DIM),
            up[seq - (CONV_WIDTH - 1):, COL_G:COL_G + gw].reshape(1, n_pb, CONV_WIDTH - 1, gw),
            x1[n_p - POOL_STATE:n_p].reshape(1, n_pb, POOL_STATE, d),
            us[:, COL_K:COL_K + WIDTH].reshape(1, n_sb, dseq, N_HEADS, HEAD_DIM),
            us[:, COL_V:COL_V + WIDTH].reshape(1, n_sb, dseq, N_HEADS, HEAD_DIM),
            us[:, COL_S:COL_S + N_HEADS].reshape(1, n_sb, dseq, N_HEADS),
            s_out[1:].reshape(1, n_sb, N_HEADS, HEAD_DIM, HEAD_DIM),
            us[:, COL_G:COL_G + gw].reshape(n_sb, dseq, gw)[:, dseq - (CONV_WIDTH - 1):].reshape(
                1, n_sb, CONV_WIDTH - 1, gw),
            x1[n_p:].reshape(n_sb, dseq, d)[:, dseq - POOL_STATE:].reshape(1, n_sb, POOL_STATE, d))
```

```python
import functools

import numpy as np
import jax
import jax.numpy as jnp
from jax import lax
from jax.experimental import pallas as pl
from jax.experimental.pallas import tpu as pltpu

F32 = jnp.float32
BF16 = jnp.bfloat16
I32 = jnp.int32
HIGHEST = lax.Precision.HIGHEST

LANES = 128
SUBLANES = 8
VMEM_LIMIT = 56 * 1024 * 1024

HEAD_DIM = 128
N_HEADS = 4
WIDTH = N_HEADS * HEAD_DIM
CHUNK = 64
CONV_WIDTH = 4
POOL_WINDOWS = (2, 4, 8, 16)
POOL_HALO = 16
POOL_STATE = 15
N_EXPERTS = 32
TOP_K = 4
SWIGLU_LIMIT = 7.0
SWIGLU_ALPHA = 1.702
DEPTH = 2
DN_ALPHA = (2 * DEPTH) ** 0.25
LN_EPS = 1e-5
NORM_EPS = 1e-6
NEG_INF = -1e30
LOG2E = 1.4426950408889634

COL_Q, COL_K, COL_V = 0, WIDTH, 2 * WIDTH
COL_G = 3 * WIDTH
COL_Z = 6 * WIDTH
COL_S = 7 * WIDTH
U_COLS = COL_S + LANES
LANE_F, LANE_A, LANE_B = 0, N_HEADS, 2 * N_HEADS

TM_PROJ = 256
TM_TOK = 512
TM_ROW = 256
MOE_BLOCK = 256
TQ = 512
TK = 512
GDN_ROWS = 256


def _cparams(sem):
    return pltpu.CompilerParams(dimension_semantics=sem, vmem_limit_bytes=VMEM_LIMIT)


def _softplus(x):
    return jnp.maximum(x, 0.0) + jnp.log1p(jnp.exp(-jnp.abs(x)))


def _sigmoid(x):
    return 1.0 / (1.0 + jnp.exp(-x))


def _silu(x):
    return x * _sigmoid(x)


def _layer_norm(y, g, b):
    mu = jnp.mean(y, axis=-1, keepdims=True)
    yc = y - mu
    var = jnp.mean(yc * yc, axis=-1, keepdims=True)
    return yc * lax.rsqrt(var + LN_EPS) * g + b


def _dot(a, b):
    return jnp.dot(a.astype(BF16), b.astype(BF16), preferred_element_type=F32)


def _dot_nt(a, b):
    return lax.dot_general(a.astype(BF16), b.astype(BF16), (((1,), (1,)), ((), ())),
                           preferred_element_type=F32)


def _dot_tn(a, b):
    return lax.dot_general(a.astype(BF16), b.astype(BF16), (((0,), (0,)), ((), ())),
                           preferred_element_type=F32)


def _spread_lanes(x, width):
    if width % LANES == 0:
        return jnp.concatenate([x] * (width // LANES), axis=1)
    return jnp.broadcast_to(x[:, 0:1], (x.shape[0], width))


def _lanes_to_rows(x, lane0):
    r = lax.broadcasted_iota(I32, (SUBLANES, LANES), 0)
    c = lax.broadcasted_iota(I32, (SUBLANES, LANES), 1)
    sel = (c == r + lane0).astype(F32)
    return lax.dot_general(sel, x, (((1,), (1,)), ((), ())), precision=HIGHEST,
                           preferred_element_type=F32)


def _proj_kernel(x_ref, w_ref, bf_ref, u_ref):
    u = jnp.dot(x_ref[...].astype(BF16), w_ref[...], preferred_element_type=F32)
    u_ref[...] = u
    small = u[:, COL_S:]
    lane = lax.broadcasted_iota(I32, small.shape, 1)
    logf = -_softplus(-(small + bf_ref[...]))
    u_ref[:, COL_S:] = jnp.where(lane < LANE_A, logf, small)


def _proj(x, w_bf16, bf_row):
    n, d = x.shape
    m = w_bf16.shape[1]
    return pl.pallas_call(
        _proj_kernel,
        grid=(n // TM_PROJ,),
        in_specs=[pl.BlockSpec((TM_PROJ, d), lambda i: (i, 0)),
                  pl.BlockSpec((d, m), lambda i: (0, 0)),
                  pl.BlockSpec((1, LANES), lambda i: (0, 0))],
        out_specs=pl.BlockSpec((TM_PROJ, m), lambda i: (i, 0)),
        out_shape=jax.ShapeDtypeStruct((n, m), F32),
        compiler_params=_cparams(("parallel",)),
        name="in_proj",
    )(x, w_bf16, bf_row)


def _cumsum_kernel(lf_ref, crep_ref, crow_ref, carry_ref):
    @pl.when(pl.program_id(1) == 0)
    def _():
        carry_ref[...] = jnp.zeros_like(carry_ref)

    lf = lf_ref[0]
    t = lf.shape[0]
    r = lax.broadcasted_iota(I32, (t, t), 0)
    c = lax.broadcasted_iota(I32, (t, t), 1)
    tril = (c <= r).astype(F32)
    cs = jnp.dot(tril, lf, precision=HIGHEST, preferred_element_type=F32) + carry_ref[0:1, :]
    carry_ref[...] = jnp.broadcast_to(cs[t - 1:t, :], carry_ref.shape)
    c2 = cs * LOG2E
    crow_ref[0] = _lanes_to_rows(c2, LANE_F)
    for h in range(N_HEADS):
        crep_ref[0, :, h * HEAD_DIM:(h + 1) * HEAD_DIM] = jnp.broadcast_to(
            c2[:, LANE_F + h:LANE_F + h + 1], (t, HEAD_DIM))


def _cumsum(arr, n_batch, length, tl, col_block):
    return pl.pallas_call(
        _cumsum_kernel,
        grid=(n_batch, length // tl),
        in_specs=[pl.BlockSpec((1, tl, LANES), lambda b, j: (b, j, col_block))],
        out_specs=[pl.BlockSpec((1, tl, WIDTH), lambda b, j: (b, j, 0)),
                   pl.BlockSpec((1, SUBLANES, tl), lambda b, j: (b, 0, j))],
        out_shape=[jax.ShapeDtypeStruct((n_batch, length, WIDTH), F32),
                   jax.ShapeDtypeStruct((n_batch, SUBLANES, length), F32)],
        scratch_shapes=[pltpu.VMEM((SUBLANES, LANES), F32)],
        compiler_params=_cparams(("parallel", "arbitrary")),
        name="logf_cumsum",
    )(arr)


def _fox_kernel(qi_ref, kj_ref, last_ref, q_ref, k_ref, v_ref, cq_ref, ck_ref, o_ref,
                m_ref, l_ref, acc_ref, *, tq, tk, past):
    s_idx = pl.program_id(1)
    qi = qi_ref[s_idx]
    kj = kj_ref[s_idx]

    @pl.when(kj == 0)
    def _():
        m_ref[...] = jnp.full_like(m_ref, NEG_INF)
        l_ref[...] = jnp.zeros_like(l_ref)
        acc_ref[...] = jnp.zeros_like(acc_ref)

    def update(masked):
        if masked:
            q_pos = past + qi * tq + lax.broadcasted_iota(I32, (tq, tk), 0)
            k_pos = kj * tk + lax.broadcasted_iota(I32, (tq, tk), 1)
            visible = k_pos <= q_pos
        for h in range(N_HEADS):
            cols = slice(h * HEAD_DIM, (h + 1) * HEAD_DIM)
            q = q_ref[0, :, cols] * (HEAD_DIM ** -0.5 * LOG2E)
            t = _dot_nt(q, k_ref[0, :, cols]) - ck_ref[0, h:h + 1, :]
            if masked:
                t = jnp.where(visible, t, NEG_INF)
            cq = cq_ref[0, :, cols]
            m_prev = m_ref[h]
            m_new = jnp.maximum(m_prev, jnp.max(t, axis=-1, keepdims=True) + cq)
            p = jnp.exp2(t - _spread_lanes(m_new - cq, tk))
            alpha = jnp.exp2(m_prev - m_new)
            l_ref[h] = alpha * l_ref[h] + jnp.sum(p, axis=-1, keepdims=True)
            acc_ref[:, cols] = alpha * acc_ref[:, cols] + _dot(p, v_ref[0, :, cols])
            m_ref[h] = m_new

    crosses_diagonal = kj * tk + (tk - 1) > past + qi * tq
    pl.when(crosses_diagonal)(functools.partial(update, True))
    pl.when(jnp.logical_not(crosses_diagonal))(functools.partial(update, False))

    @pl.when(last_ref[s_idx] == 1)
    def _():
        for h in range(N_HEADS):
            cols = slice(h * HEAD_DIM, (h + 1) * HEAD_DIM)
            o_ref[0, :, cols] = acc_ref[:, cols] / l_ref[h]


def _fox_schedule(n_q, tq, tk, past):
    qi, kj, last = [], [], []
    for i in range(n_q):
        hi = (past + (i + 1) * tq - 1) // tk
        for j in range(hi + 1):
            qi.append(i)
            kj.append(j)
            last.append(1 if j == hi else 0)
    return (jnp.asarray(np.array(qi, np.int32)), jnp.asarray(np.array(kj, np.int32)),
            jnp.asarray(np.array(last, np.int32)))


def _fox(q_arr, q_map, k_arr, k_map, v_arr, v_map, cq_arr, cq_map, ck_arr, ck_map,
         n_batch, n_q, tq, tk, past):
    qi, kj, last = _fox_schedule(n_q, tq, tk, past)
    n_steps = int(qi.shape[0])
    spec = lambda shape, fn, tab: pl.BlockSpec(shape, lambda b, s, qi_r, kj_r, la_r: fn(b, (qi_r if tab == 'q' else kj_r)[s]))
    return pl.pallas_call(
        functools.partial(_fox_kernel, tq=tq, tk=tk, past=past),
        grid_spec=pltpu.PrefetchScalarGridSpec(
            num_scalar_prefetch=3,
            grid=(n_batch, n_steps),
            in_specs=[spec((1, tq, WIDTH), q_map, 'q'),
                      spec((1, tk, WIDTH), k_map, 'k'),
                      spec((1, tk, WIDTH), v_map, 'k'),
                      spec((1, tq, WIDTH), cq_map, 'q'),
                      spec((1, SUBLANES, tk), ck_map, 'k')],
            out_specs=spec((1, tq, WIDTH), lambda b, i: (b, i, 0), 'q'),
            scratch_shapes=[pltpu.VMEM((N_HEADS, tq, HEAD_DIM), F32),
                            pltpu.VMEM((N_HEADS, tq, HEAD_DIM), F32),
                            pltpu.VMEM((tq, WIDTH), F32)]),
        out_shape=jax.ShapeDtypeStruct((n_batch, n_q * tq, WIDTH), F32),
        compiler_params=_cparams(("parallel", "arbitrary")),
        name="fox_attention",
    )(qi, kj, last, q_arr, k_arr, v_arr, cq_arr, ck_arr)


def _gdn_kernel(first_ref, last_ref, seq_ref,
                pre_ref, z_ref, sm_ref, cpast_ref, convw_ref, s0_ref, alog_ref, dtb_ref, ng_ref,
                o_ref, sout_ref, stage_ref, s_ref, *, rows):
    step = pl.program_id(0)
    halo = SUBLANES
    n_chunks = rows // CHUNK

    @pl.when(first_ref[step] == 1)
    def _():
        stage_ref[0:halo, :] = cpast_ref[0]
        s_ref[...] = s0_ref[0]

    stage_ref[halo:halo + rows, :] = pre_ref[0]
    conv = stage_ref[halo:halo + rows, :] * convw_ref[CONV_WIDTH - 1:CONV_WIDTH, :]
    for j in range(1, CONV_WIDTH):
        conv = conv + (stage_ref[halo - j:halo - j + rows, :]
                       * convw_ref[CONV_WIDTH - 1 - j:CONV_WIDTH - j, :])
    stage_ref[0:halo, :] = stage_ref[rows:rows + halo, :]
    act = _silu(conv)

    small = sm_ref[0]
    beta_all = _sigmoid(small)
    g_all = -jnp.exp(alog_ref[...]) * _softplus(small + dtb_ref[...])
    r = lax.broadcasted_iota(I32, (rows, rows), 0)
    c = lax.broadcasted_iota(I32, (rows, rows), 1)
    same_chunk = (r // CHUNK) == (c // CHUNK)
    incl = same_chunk & (c <= r)
    strict = same_chunk & (c < r)
    eye = (c == r).astype(F32)
    gc_all = jnp.dot(incl.astype(F32), g_all, precision=HIGHEST, preferred_element_type=F32)
    gc_rows = _lanes_to_rows(gc_all, LANE_A)

    for h in range(N_HEADS):
        cols = slice(h * HEAD_DIM, (h + 1) * HEAD_DIM)
        q = act[:, h * HEAD_DIM:(h + 1) * HEAD_DIM]
        k = act[:, WIDTH + h * HEAD_DIM:WIDTH + (h + 1) * HEAD_DIM]
        v = act[:, 2 * WIDTH + h * HEAD_DIM:2 * WIDTH + (h + 1) * HEAD_DIM]
        q = q * lax.rsqrt(jnp.sum(q * q, axis=-1, keepdims=True) + NORM_EPS) * (HEAD_DIM ** -0.5)
        k = k * lax.rsqrt(jnp.sum(k * k, axis=-1, keepdims=True) + NORM_EPS)
        beta = beta_all[:, LANE_B + h:LANE_B + h + 1]
        gc = gc_all[:, LANE_A + h:LANE_A + h + 1]
        diff = gc - gc_rows[h:h + 1, :]
        decay = jnp.where(incl, jnp.exp(jnp.where(incl, diff, 0.0)), 0.0)
        kb = k * beta
        vb = v * beta
        low = jnp.where(strict, _dot_nt(kb, k) * decay, 0.0)
        inv = eye - low
        pw = _dot(low, low)
        n_sq = CHUNK.bit_length() - 2
        for it in range(n_sq):
            inv = inv + _dot(inv, pw)
            if it + 1 < n_sq:
                pw = _dot(pw, pw)
        egc = jnp.exp(gc)
        uw = _dot(inv, jnp.concatenate([vb, kb * egc], axis=1))
        intra = jnp.where(incl, _dot_nt(q, k) * decay, 0.0)
        qd = q * egc
        g_last = [gc[(g + 1) * CHUNK - 1:(g + 1) * CHUNK, :] for g in range(n_chunks)]
        kd = k * jnp.exp(jnp.concatenate([jnp.broadcast_to(gl, (CHUNK, 1)) for gl in g_last], axis=0)
                         - gc)
        state = s_ref[h]
        v_new = []
        for g in range(n_chunks):
            rs = slice(g * CHUNK, (g + 1) * CHUNK)
            v_new.append(uw[rs, :HEAD_DIM] - _dot(uw[rs, HEAD_DIM:], state))
            v_rows = jnp.concatenate(
                v_new + [jnp.zeros((rows - (g + 1) * CHUNK, HEAD_DIM), F32)] * (g + 1 < n_chunks), axis=0)
            o = _dot(qd[rs, :], state) + _dot(intra[rs, :], v_rows)
            state = state * jnp.exp(g_last[g]) + _dot_tn(kd[rs, :], v_new[g])
            o = (o * lax.rsqrt(jnp.mean(o * o, axis=-1, keepdims=True) + NORM_EPS)
                 * ng_ref[...] * _silu(z_ref[0, rs, cols]))
            o_ref[0, rs, cols] = o
        s_ref[h] = state

    @pl.when(last_ref[step] == 1)
    def _():
        sout_ref[0] = s_ref[...]


def _gdn(u_view, blk0, first, last, seq, conv_past, conv_w, s0, alog_row, dtb_row, ng_row):
    rows = u_view.shape[1]
    n_steps = int(first.shape[0])
    n_seq = s0.shape[0]
    gw = 3 * WIDTH
    return pl.pallas_call(
        functools.partial(_gdn_kernel, rows=rows),
        grid_spec=pltpu.PrefetchScalarGridSpec(
            num_scalar_prefetch=3,
            grid=(n_steps,),
            in_specs=[pl.BlockSpec((1, rows, gw), lambda s, f, l, q: (blk0 + s, 0, COL_G // gw)),
                      pl.BlockSpec((1, rows, WIDTH), lambda s, f, l, q: (blk0 + s, 0, COL_Z // WIDTH)),
                      pl.BlockSpec((1, rows, LANES), lambda s, f, l, q: (blk0 + s, 0, COL_S // LANES)),
                      pl.BlockSpec((1, SUBLANES, gw), lambda s, f, l, q: (q[s], 0, 0)),
                      pl.BlockSpec((SUBLANES, gw), lambda s, f, l, q: (0, 0)),
                      pl.BlockSpec((1, N_HEADS, HEAD_DIM, HEAD_DIM), lambda s, f, l, q: (q[s], 0, 0, 0)),
                      pl.BlockSpec((1, LANES), lambda s, f, l, q: (0, 0)),
                      pl.BlockSpec((1, LANES), lambda s, f, l, q: (0, 0)),
                      pl.BlockSpec((1, LANES), lambda s, f, l, q: (0, 0))],
            out_specs=[pl.BlockSpec((1, rows, WIDTH), lambda s, f, l, q: (s, 0, 0)),
                       pl.BlockSpec((1, N_HEADS, HEAD_DIM, HEAD_DIM), lambda s, f, l, q: (q[s], 0, 0, 0))],
            scratch_shapes=[pltpu.VMEM((rows + SUBLANES, gw), F32),
                            pltpu.VMEM((N_HEADS, HEAD_DIM, HEAD_DIM), F32)]),
        out_shape=[jax.ShapeDtypeStruct((n_steps, rows, WIDTH), F32),
                   jax.ShapeDtypeStruct((n_seq, N_HEADS, HEAD_DIM, HEAD_DIM), F32)],
        compiler_params=_cparams(("arbitrary",)),
        name="gated_deltanet",
    )(first, last, seq, u_view, u_view, u_view, conv_past, conv_w, s0, alog_row, dtb_row, ng_row)


def _outproj_ln_kernel(x_ref, of_ref, og_ref, w_ref, g_ref, b_ref, h_ref):
    mix = _dot(of_ref[...], w_ref[0:WIDTH, :]) + _dot(og_ref[...], w_ref[WIDTH:2 * WIDTH, :])
    h_ref[...] = _layer_norm(DN_ALPHA * x_ref[...] + mix, g_ref[...], b_ref[...])


def _outproj_ln(x, o_fox, o_gdn, w_bf16, g_row, b_row):
    n, d = x.shape
    row = lambda i: (i, 0)
    fixed = lambda i: (0, 0)
    return pl.pallas_call(
        _outproj_ln_kernel,
        grid=(n // TM_TOK,),
        in_specs=[pl.BlockSpec((TM_TOK, d), row), pl.BlockSpec((TM_TOK, WIDTH), row),
                  pl.BlockSpec((TM_TOK, WIDTH), row), pl.BlockSpec((2 * WIDTH, d), fixed),
                  pl.BlockSpec((1, d), fixed), pl.BlockSpec((1, d), fixed)],
        out_specs=pl.BlockSpec((TM_TOK, d), row),
        out_shape=jax.ShapeDtypeStruct((n, d), F32),
        compiler_params=_cparams(("parallel",)),
        name="out_proj_ln",
    )(x, o_fox, o_gdn, w_bf16, g_row, b_row)


def _pool_ln_kernel(x_ref, halo_ref, pw_ref, ps_ref, w_ref, g_ref, b_ref, h_ref, stage_ref,
                    *, tm, pos0, zero_first_halo):
    i = pl.program_id(0)
    stage_ref[0:POOL_HALO, :] = halo_ref[0]
    if zero_first_halo:
        @pl.when(i == 0)
        def _():
            stage_ref[0:POOL_HALO, :] = jnp.zeros((POOL_HALO, stage_ref.shape[1]), F32)
    x = x_ref[...]
    stage_ref[POOL_HALO:POOL_HALO + tm, :] = x
    gdim = x.shape[1] // len(POOL_WINDOWS)
    pos = pos0 + lax.broadcasted_iota(I32, (tm, 1), 0)
    if zero_first_halo:
        pos = pos + i * tm
    parts = []
    for gi, win in enumerate(POOL_WINDOWS):
        cols = slice(gi * gdim, (gi + 1) * gdim)
        s = stage_ref[POOL_HALO:POOL_HALO + tm, cols]
        for j in range(1, win):
            s = s + stage_ref[POOL_HALO - j:POOL_HALO - j + tm, cols]
        cnt = jnp.minimum(pos + 1, win).astype(F32)
        zg = s / cnt - x[:, cols]
        parts.append(_dot(zg, pw_ref[gi]))
    zg = jnp.concatenate(parts, axis=-1) * ps_ref[...]
    mix = _dot(zg, w_ref[...])
    h_ref[...] = _layer_norm(DN_ALPHA * x + mix, g_ref[...], b_ref[...])


def _pool_ln(x, x_map, halo_arr, halo_map, n_tiles, tm, pos0, zero_first_halo,
             pw_bf16, ps_row, w_bf16, g_row, b_row):
    d = x.shape[1]
    gdim = d // len(POOL_WINDOWS)
    fixed = lambda i: (0, 0)
    return pl.pallas_call(
        functools.partial(_pool_ln_kernel, tm=tm, pos0=pos0, zero_first_halo=zero_first_halo),
        grid=(n_tiles,),
        in_specs=[pl.BlockSpec((tm, d), x_map),
                  pl.BlockSpec((1, POOL_HALO, d), halo_map),
                  pl.BlockSpec((len(POOL_WINDOWS), gdim, gdim), lambda i: (0, 0, 0)),
                  pl.BlockSpec((1, d), fixed), pl.BlockSpec((d, d), fixed),
                  pl.BlockSpec((1, d), fixed), pl.BlockSpec((1, d), fixed)],
        out_specs=pl.BlockSpec((tm, d), lambda i: (i, 0)),
        out_shape=jax.ShapeDtypeStruct((n_tiles * tm, d), F32),
        scratch_shapes=[pltpu.VMEM((POOL_HALO + tm, d), F32)],
        compiler_params=_cparams(("arbitrary",)),
        name="pool_mixer_ln",
    )(x, halo_arr, pw_bf16, ps_row, w_bf16, g_row, b_row)


def _router_kernel(h_ref, wr_ref, br_ref, idx_ref, gate_ref, rank_ref, cnt_ref, carry_ref):
    @pl.when(pl.program_id(0) == 0)
    def _():
        carry_ref[...] = jnp.zeros_like(carry_ref)

    tm = h_ref.shape[0]
    lane = lax.broadcasted_iota(I32, (tm, LANES), 1).astype(F32)
    logits = jnp.dot(h_ref[...], wr_ref[...], precision=HIGHEST, preferred_element_type=F32)
    work = jnp.where(lane < N_EXPERTS, logits + br_ref[...], -jnp.inf)
    vals, ids = [], []
    for _ in range(TOP_K):
        m = jnp.max(work, axis=-1, keepdims=True)
        ik = jnp.min(jnp.where(work == m, lane, float(LANES)), axis=-1, keepdims=True)
        vals.append(m)
        ids.append(ik)
        work = jnp.where(lane == ik, -jnp.inf, work)
    exps = [jnp.exp(v - vals[0]) for v in vals]
    denom = exps[0]
    for e in exps[1:]:
        denom = denom + e
    multihot = jnp.zeros((tm, LANES), F32)
    idx_out = jnp.zeros((tm, LANES), F32)
    gate_out = jnp.zeros((tm, LANES), F32)
    for k in range(TOP_K):
        multihot = multihot + (lane == ids[k]).astype(F32)
        idx_out = jnp.where(lane == k, ids[k], idx_out)
        gate_out = jnp.where(lane == k, exps[k] / denom, gate_out)
    r = lax.broadcasted_iota(I32, (tm, tm), 0)
    c = lax.broadcasted_iota(I32, (tm, tm), 1)
    before = _dot((c < r).astype(F32), multihot) + carry_ref[0:1, :]
    rank_out = jnp.zeros((tm, LANES), F32)
    for k in range(TOP_K):
        rk = jnp.sum(jnp.where(lane == ids[k], before, 0.0), axis=-1, keepdims=True)
        rank_out = jnp.where(lane == k, rk, rank_out)
    idx_ref[...] = idx_out.astype(I32)
    gate_ref[...] = gate_out
    rank_ref[...] = rank_out.astype(I32)
    total = carry_ref[0:1, :] + jnp.sum(multihot, axis=0, keepdims=True)
    carry_ref[...] = jnp.broadcast_to(total, carry_ref.shape)
    cnt_ref[...] = jnp.broadcast_to(total, cnt_ref.shape).astype(I32)


def _router(h, wr_pad, br_row):
    n, d = h.shape
    row = lambda i: (i, 0)
    fixed = lambda i: (0, 0)
    return pl.pallas_call(
        _router_kernel,
        grid=(n // TM_TOK,),
        in_specs=[pl.BlockSpec((TM_TOK, d), row), pl.BlockSpec((d, LANES), fixed),
                  pl.BlockSpec((1, LANES), fixed)],
        out_specs=[pl.BlockSpec((TM_TOK, LANES), row), pl.BlockSpec((TM_TOK, LANES), row),
                   pl.BlockSpec((TM_TOK, LANES), row), pl.BlockSpec((SUBLANES, LANES), fixed)],
        out_shape=[jax.ShapeDtypeStruct((n, LANES), I32), jax.ShapeDtypeStruct((n, LANES), F32),
                   jax.ShapeDtypeStruct((n, LANES), I32), jax.ShapeDtypeStruct((SUBLANES, LANES), I32)],
        scratch_shapes=[pltpu.VMEM((SUBLANES, LANES), F32)],
        compiler_params=_cparams(("arbitrary",)),
        name="moe_router",
    )(h, wr_pad, br_row)


def _row_copy(src_ref, src_row, dst_ref, dst_row, sem):
    return pltpu.make_async_copy(src_ref.at[pl.ds(src_row, 1)], dst_ref.at[pl.ds(dst_row, 1)], sem)


def _block_copy(src_ref, dst_ref, dst_block, sem):
    start = pl.multiple_of(dst_block * MOE_BLOCK, MOE_BLOCK)
    return pltpu.make_async_copy(src_ref, dst_ref.at[pl.ds(start, MOE_BLOCK)], sem)


def _dispatch_kernel(slot_ref, pend_ref, h_ref, xb_ref, zero_ref, sem, zsem):
    tm = h_ref.shape[0]
    base = pl.program_id(0) * tm * TOP_K
    n_blocks = xb_ref.shape[0] // MOE_BLOCK

    @pl.when(pl.program_id(0) == 0)
    def _():
        zero_ref[...] = jnp.zeros_like(zero_ref)
        n_used = pend_ref[N_EXPERTS - 1] // MOE_BLOCK
        group_last = lambda e: jnp.maximum(pend_ref[e] // MOE_BLOCK - 1, 0)
        for e in range(N_EXPERTS):
            _block_copy(zero_ref, xb_ref, group_last(e), zsem).start()
        lax.fori_loop(n_used, n_blocks, lambda b, c: (_block_copy(zero_ref, xb_ref, b, zsem).start(), c)[1], 0)
        for e in range(N_EXPERTS):
            _block_copy(zero_ref, xb_ref, group_last(e), zsem).wait()
        lax.fori_loop(n_used, n_blocks, lambda b, c: (_block_copy(zero_ref, xb_ref, b, zsem).wait(), c)[1], 0)

    def issue(r, carry):
        for k in range(TOP_K):
            _row_copy(h_ref, r, xb_ref, slot_ref[base + r * TOP_K + k], sem).start()
        return carry

    def drain(r, carry):
        for k in range(TOP_K):
            _row_copy(h_ref, r, xb_ref, slot_ref[base + r * TOP_K + k], sem).wait()
        return carry

    lax.fori_loop(0, tm, issue, 0)
    lax.fori_loop(0, tm, drain, 0)


def _dispatch(h, slot_flat, pend, n_slots):
    n, d = h.shape
    return pl.pallas_call(
        _dispatch_kernel,
        grid_spec=pltpu.PrefetchScalarGridSpec(
            num_scalar_prefetch=2,
            grid=(n // TM_ROW,),
            in_specs=[pl.BlockSpec((TM_ROW, d), lambda i, s, p: (i, 0))],
            out_specs=pl.BlockSpec(memory_space=pl.ANY),
            scratch_shapes=[pltpu.VMEM((MOE_BLOCK, d), F32), pltpu.SemaphoreType.DMA(()),
                            pltpu.SemaphoreType.DMA(())]),
        out_shape=jax.ShapeDtypeStruct((n_slots, d), F32),
        compiler_params=_cparams(("arbitrary",)),
        name="moe_dispatch",
    )(slot_flat, pend, h)


def _expert_kernel(be_ref, nu_ref, xb_ref, wup_ref, bup_ref, wdn_ref, bdn_ref, yb_ref,
                   wup_bf_ref, wdn_bf_ref):
    b = pl.program_id(0)

    @pl.when(b < nu_ref[0])
    def _():
        prev = be_ref[jnp.maximum(b - 1, 0)]

        @pl.when((b == 0) | (be_ref[b] != prev))
        def _():
            wup_bf_ref[...] = wup_ref[0, 0].astype(BF16)
            wdn_bf_ref[...] = wdn_ref[0, 0].astype(BF16)

        d_exp = wdn_ref.shape[2]
        hu = jnp.dot(xb_ref[...].astype(BF16), wup_bf_ref[...], preferred_element_type=F32) + bup_ref[0, 0]
        glu = jnp.minimum(hu[:, :d_exp], SWIGLU_LIMIT)
        lin = jnp.clip(hu[:, d_exp:], -SWIGLU_LIMIT, SWIGLU_LIMIT)
        a = glu * _sigmoid(SWIGLU_ALPHA * glu) * (lin + 1.0)
        yb_ref[...] = jnp.dot(a.astype(BF16), wdn_bf_ref[...], preferred_element_type=F32) + bdn_ref[0, 0]

    @pl.when(b >= nu_ref[0])
    def _():
        yb_ref[...] = jnp.zeros_like(yb_ref)


def _experts(xb, block_e, n_used, layer, w_up, b_up, w_dn, b_dn):
    n_slots, d = xb.shape
    n_blocks = n_slots // MOE_BLOCK
    d_up = w_up.shape[3]
    d_exp = w_dn.shape[2]
    blk = lambda b, be, nu: (jnp.minimum(b, nu[0] - 1), 0)
    wsel = lambda b, be, nu: (layer, be[b], 0, 0)
    return pl.pallas_call(
        _expert_kernel,
        grid_spec=pltpu.PrefetchScalarGridSpec(
            num_scalar_prefetch=2,
            grid=(n_blocks,),
            in_specs=[pl.BlockSpec((MOE_BLOCK, d), blk),
                      pl.BlockSpec((1, 1, d, d_up), wsel),
                      pl.BlockSpec((1, 1, 1, d_up), wsel),
                      pl.BlockSpec((1, 1, d_exp, d), wsel),
                      pl.BlockSpec((1, 1, 1, d), wsel)],
            out_specs=pl.BlockSpec((MOE_BLOCK, d), lambda b, be, nu: (b, 0)),
            scratch_shapes=[pltpu.VMEM((d, d_up), BF16), pltpu.VMEM((d_exp, d), BF16)]),
        out_shape=jax.ShapeDtypeStruct((n_slots, d), F32),
        compiler_params=_cparams(("arbitrary",)),
        name="moe_experts",
    )(block_e, n_used, xb, w_up, b_up, w_dn, b_dn)


def _combine_kernel(slot_ref, h_ref, gate_ref, p_ref, wpg_ref, wpp_ref, g_ref, b_ref, yb_ref,
                    out_ref, buf_ref, sem):
    tm = h_ref.shape[0]
    base = pl.program_id(0) * tm * TOP_K

    def issue(r, carry):
        for k in range(TOP_K):
            _row_copy(yb_ref, slot_ref[base + r * TOP_K + k], buf_ref.at[k], r, sem).start()
        return carry

    def drain(r, carry):
        for k in range(TOP_K):
            _row_copy(yb_ref, slot_ref[base + r * TOP_K + k], buf_ref.at[k], r, sem).wait()
        return carry

    lax.fori_loop(0, tm, issue, 0)
    lax.fori_loop(0, tm, drain, 0)
    gate = gate_ref[...]
    moe = buf_ref[0] * gate[:, 0:1]
    for k in range(1, TOP_K):
        moe = moe + buf_ref[k] * gate[:, k:k + 1]
    h2 = _layer_norm(DN_ALPHA * h_ref[...] + moe, g_ref[...], b_ref[...])
    out_ref[...] = h2 + _sigmoid(_dot(h2, wpg_ref[...])) * _dot(p_ref[...], wpp_ref[...])


def _combine(h, gate, p, wpg_bf16, wpp_bf16, g_row, b_row, yb, slot_flat):
    n, d = h.shape
    row = lambda i, s: (i, 0)
    fixed = lambda i, s: (0, 0)
    return pl.pallas_call(
        _combine_kernel,
        grid_spec=pltpu.PrefetchScalarGridSpec(
            num_scalar_prefetch=1,
            grid=(n // TM_ROW,),
            in_specs=[pl.BlockSpec((TM_ROW, d), row), pl.BlockSpec((TM_ROW, LANES), row),
                      pl.BlockSpec((TM_ROW, p.shape[1]), row), pl.BlockSpec((d, d), fixed),
                      pl.BlockSpec((p.shape[1], d), fixed), pl.BlockSpec((1, d), fixed),
                      pl.BlockSpec((1, d), fixed), pl.BlockSpec(memory_space=pl.ANY)],
            out_specs=pl.BlockSpec((TM_ROW, d), row),
            scratch_shapes=[pltpu.VMEM((TOP_K, TM_ROW, d), F32), pltpu.SemaphoreType.DMA(())]),
        out_shape=jax.ShapeDtypeStruct((n, d), F32),
        compiler_params=_cparams(("arbitrary",)),
        name="moe_combine_ln_embed",
    )(slot_flat, h, gate, p, wpg_bf16, wpp_bf16, g_row, b_row, yb)


def _layer_tail(h, p, layer, g2, b2, w_r, b_r, w_up, b_up, w_dn, b_dn, w_pg, w_pp):
    n, d = h.shape
    wr_pad = jnp.pad(w_r, ((0, 0), (0, LANES - N_EXPERTS)))
    br_row = jnp.pad(b_r, (0, LANES - N_EXPERTS))[None]
    idx, gate, rank, cnt = _router(h, wr_pad, br_row)
    counts = cnt[0, :N_EXPERTS]
    padded = (counts + MOE_BLOCK - 1) // MOE_BLOCK * MOE_BLOCK
    pend = jnp.cumsum(padded).astype(I32)
    pstart = pend - padded
    slot = (pstart[idx[:, :TOP_K]] + rank[:, :TOP_K]).reshape(-1).astype(I32)
    n_blocks = (n * TOP_K + N_EXPERTS * (MOE_BLOCK - 1) + MOE_BLOCK - 1) // MOE_BLOCK
    n_used = pend[-1] // MOE_BLOCK
    blk = jnp.minimum(jnp.arange(n_blocks, dtype=I32), n_used - 1) * MOE_BLOCK
    block_e = jnp.minimum(jnp.sum((pend[None, :] <= blk[:, None]).astype(I32), axis=1), N_EXPERTS - 1)
    xb = _dispatch(h, slot, pend, n_blocks * MOE_BLOCK)
    yb = _experts(xb, block_e, n_used[None], layer, w_up, b_up[:, :, None, :], w_dn, b_dn[:, :, None, :])
    return _combine(h, gate, p, w_pg.astype(BF16), w_pp.astype(BF16), g2[None], b2[None], yb, slot)


def _lane_row(v, lane0):
    return jnp.zeros((1, LANES), F32).at[0, lane0:lane0 + v.shape[0]].set(v.astype(F32))


def kernel(x_prompt, x_sample, cache_fox_k, cache_fox_v, cache_fox_logf, state_gdn, state_gdn_conv,
           cache_pool, p_prompt, p_sample, w_in_ab, b_fgate, gdn_a_log, gdn_dt_bias, gdn_conv_w,
           gdn_norm_g, w_out_ab, pool_w, pool_scale, w_out_pool, ln1_g, ln1_b, ln2_g, ln2_b,
           w_router, b_router, w_expert_up, b_expert_up, w_expert_down, b_expert_down,
           w_ple_gate, w_ple_proj):
    n_pb, seq, d = x_prompt.shape
    n_sb, dseq, _ = x_sample.shape
    past = cache_fox_k.shape[2]
    assert n_pb == 1 and dseq == CHUNK and past % dseq == 0 and seq % TQ == 0
    n_p = n_pb * seq
    n_s = n_sb * dseq
    n = n_p + n_s
    x = jnp.concatenate([x_prompt.reshape(n_p, d), x_sample.reshape(n_s, d)], axis=0)

    def tail(h, i):
        p = jnp.concatenate([p_prompt[i].reshape(n_p, -1), p_sample[i].reshape(n_s, -1)], axis=0)
        return _layer_tail(h, p, i, ln2_g[i], ln2_b[i], w_router[i], b_router[i], w_expert_up,
                           b_expert_up, w_expert_down, b_expert_down, w_ple_gate[i], w_ple_proj[i])

    w_in = w_in_ab[0]
    n_small = 3 * N_HEADS
    ff0 = 3 * WIDTH
    gq0 = ff0 + N_HEADS
    ga0 = gq0 + 4 * WIDTH
    w_small = jnp.concatenate([w_in[:, ff0:gq0], w_in[:, ga0:ga0 + 2 * N_HEADS],
                               jnp.zeros((d, LANES - n_small), F32)], axis=1)
    w_all = jnp.concatenate([w_in[:, :ff0], w_in[:, gq0:ga0], w_small], axis=1).astype(BF16)
    u = _proj(x, w_all, _lane_row(b_fgate[0], LANE_F))

    u3 = u[None]
    u64 = u.reshape(n // CHUNK, CHUNK, U_COLS)
    cq_p, ck_p = _cumsum(u3, 1, n_p, TK, COL_S // LANES)
    lf_s = jnp.concatenate(
        [jnp.pad(cache_fox_logf[0].astype(F32), ((0, 0), (0, 0), (0, LANES - N_HEADS))),
         u[n_p:, COL_S:].reshape(n_sb, dseq, LANES)], axis=1)
    cq_s, ck_s = _cumsum(lf_s, n_sb, past + dseq, past + dseq, 0)

    of_p = _fox(u3, lambda b, i: (0, i, COL_Q // WIDTH), u3, lambda b, j: (0, j, COL_K // WIDTH),
                u3, lambda b, j: (0, j, COL_V // WIDTH), cq_p, lambda b, i: (0, i, 0),
                ck_p, lambda b, j: (0, 0, j), 1, n_p // TQ, TQ, TK, 0)
    k_new = u[n_p:, COL_K:COL_K + WIDTH].reshape(n_sb, dseq, WIDTH)
    v_new = u[n_p:, COL_V:COL_V + WIDTH].reshape(n_sb, dseq, WIDTH)
    k_all = jnp.concatenate([cache_fox_k[0].reshape(n_sb, past, WIDTH), k_new], axis=1)
    v_all = jnp.concatenate([cache_fox_v[0].reshape(n_sb, past, WIDTH), v_new], axis=1)
    of_s = _fox(u64, lambda b, i: (n_p // CHUNK + b, 0, COL_Q // WIDTH), k_all, lambda b, j: (b, 0, 0),
                v_all, lambda b, j: (b, 0, 0), cq_s, lambda b, i: (b, past // dseq, 0),
                ck_s, lambda b, j: (b, 0, 0), n_sb, 1, dseq, past + dseq, past)
    o_fox = jnp.concatenate([of_p.reshape(n_p, WIDTH), of_s.reshape(n_s, WIDTH)], axis=0)

    gw = 3 * WIDTH
    conv_w = jnp.pad(gdn_conv_w[0], ((0, SUBLANES - CONV_WIDTH), (0, 0)))
    gdn_args = (conv_w, _lane_row(gdn_a_log[0], LANE_A), _lane_row(gdn_dt_bias[0], LANE_A),
                gdn_norm_g[0][None])
    n_pstep = n_p // GDN_ROWS
    ends = lambda k: (jnp.asarray((np.arange(k) == 0).astype(np.int32)),
                      jnp.asarray((np.arange(k) == k - 1).astype(np.int32)))
    og_p, st_p = _gdn(u.reshape(n // GDN_ROWS, GDN_ROWS, U_COLS), 0, *ends(n_pstep),
                      jnp.zeros((n_pstep,), I32), jnp.zeros((1, SUBLANES, gw), F32), gdn_args[0],
                      jnp.zeros((1, N_HEADS, HEAD_DIM, HEAD_DIM), F32), *gdn_args[1:])
    conv_past = jnp.pad(state_gdn_conv[0].astype(F32), ((0, 0), (SUBLANES - (CONV_WIDTH - 1), 0), (0, 0)))
    ones = jnp.ones((n_sb,), I32)
    og_s, st_s = _gdn(u64, n_p // CHUNK, ones, ones, jnp.arange(n_sb, dtype=I32), conv_past, gdn_args[0],
                      state_gdn[0].astype(F32), *gdn_args[1:])
    o_gdn = jnp.concatenate([og_p.reshape(n_p, WIDTH), og_s.reshape(n_s, WIDTH)], axis=0)
    h = _outproj_ln(x, o_fox, o_gdn, w_out_ab[0].astype(BF16), ln1_g[0][None], ln1_b[0][None])
    x1 = tail(h, 0)

    pool_args = (pool_w[0].astype(BF16), pool_scale[0][None], w_out_pool[0].astype(BF16),
                 ln1_g[1][None], ln1_b[1][None])
    ratio = TM_TOK // POOL_HALO
    x1_halo = x1.reshape(n // POOL_HALO, POOL_HALO, d)
    h_p = _pool_ln(x1, lambda i: (i, 0), x1_halo, lambda i: (jnp.maximum(i * ratio - 1, 0), 0, 0),
                   n_p // TM_TOK, TM_TOK, 0, True, *pool_args)
    cache16 = jnp.pad(cache_pool[0].astype(F32), ((0, 0), (POOL_HALO - POOL_STATE, 0), (0, 0)))
    h_s = _pool_ln(x1, lambda i: (n_p // dseq + i, 0), cache16, lambda i: (i, 0, 0),
                   n_sb, dseq, past, False, *pool_args)
    x2 = tail(jnp.concatenate([h_p, h_s], axis=0), 1)

    up = u[:n_p]
    us = u[n_p:]
    return (x2[:n_p].reshape(n_pb, seq, d), x2[n_p:].reshape(n_sb, dseq, d),
            up[:, COL_K:COL_K + WIDTH].reshape(1, n_pb, seq, N_HEADS, HEAD_DIM),
            up[:, COL_V:COL_V + WIDTH].reshape(1, n_pb, seq, N_HEADS, HEAD_DIM),
            up[:, COL_S:COL_S + N_HEADS].reshape(1, n_pb, seq, N_HEADS),
            st_p.reshape(1, n_pb, N_HEADS, HEAD_DIM, HEAD_DIM),
            up[seq - (CONV_WIDTH - 1):, COL_G:COL_G + gw].reshape(1, n_pb, CONV_WIDTH - 1, gw),
            x1[n_p - POOL_STATE:n_p].reshape(1, n_pb, POOL_STATE, d),
            us[:, COL_K:COL_K + WIDTH].reshape(1, n_sb, dseq, N_HEADS, HEAD_DIM),
            us[:, COL_V:COL_V + WIDTH].reshape(1, n_sb, dseq, N_HEADS, HEAD_DIM),
            us[:, COL_S:COL_S + N_HEADS].reshape(1, n_sb, dseq, N_HEADS),
            st_s.reshape(1, n_sb, N_HEADS, HEAD_DIM, HEAD_DIM),
            us[:, COL_G:COL_G + gw].reshape(n_sb, dseq, gw)[:, dseq - (CONV_WIDTH - 1):].reshape(
                1, n_sb, CONV_WIDTH - 1, gw),
            x1[n_p:].reshape(n_sb, dseq, d)[:, dseq - POOL_STATE:].reshape(1, n_sb, POOL_STATE, d))
```

```python
import functools

import numpy as np
import jax
import jax.numpy as jnp
from jax import lax
from jax.experimental import pallas as pl
from jax.experimental.pallas import tpu as pltpu

F32 = jnp.float32
BF16 = jnp.bfloat16
I32 = jnp.int32
HIGHEST = lax.Precision.HIGHEST

LANES = 128
SUBLANES = 8
VMEM_LIMIT = 56 * 1024 * 1024

HEAD_DIM = 128
N_HEADS = 4
WIDTH = N_HEADS * HEAD_DIM
CHUNK = 64
CONV_WIDTH = 4
POOL_WINDOWS = (2, 4, 8, 16)
POOL_HALO = 16
POOL_STATE = 15
N_EXPERTS = 32
TOP_K = 4
SWIGLU_LIMIT = 7.0
SWIGLU_ALPHA = 1.702
DEPTH = 2
DN_ALPHA = (2 * DEPTH) ** 0.25
LN_EPS = 1e-5
NORM_EPS = 1e-6
NEG_INF = -1e30
LOG2E = 1.4426950408889634

COL_Q, COL_K, COL_V = 0, WIDTH, 2 * WIDTH
COL_G = 3 * WIDTH
COL_Z = 6 * WIDTH
COL_S = 7 * WIDTH
U_COLS = COL_S + LANES
LANE_F, LANE_A, LANE_B = 0, N_HEADS, 2 * N_HEADS

TM_PROJ = 256
TM_TOK = 512
TM_ROW = 256
MOE_BLOCK = 256
TQ = 512
TK = 512
GDN_ROWS = 256


def _cparams(sem):
    return pltpu.CompilerParams(dimension_semantics=sem, vmem_limit_bytes=VMEM_LIMIT)


def _softplus(x):
    return jnp.maximum(x, 0.0) + jnp.log1p(jnp.exp(-jnp.abs(x)))


def _sigmoid(x):
    return 1.0 / (1.0 + jnp.exp(-x))


def _silu(x):
    return x * _sigmoid(x)


def _layer_norm(y, g, b):
    mu = jnp.mean(y, axis=-1, keepdims=True)
    yc = y - mu
    var = jnp.mean(yc * yc, axis=-1, keepdims=True)
    return yc * lax.rsqrt(var + LN_EPS) * g + b


def _dot(a, b):
    return jnp.dot(a.astype(BF16), b.astype(BF16), preferred_element_type=F32)


def _dot_nt(a, b):
    return lax.dot_general(a.astype(BF16), b.astype(BF16), (((1,), (1,)), ((), ())),
                           preferred_element_type=F32)


def _dot_tn(a, b):
    return lax.dot_general(a.astype(BF16), b.astype(BF16), (((0,), (0,)), ((), ())),
                           preferred_element_type=F32)


def _spread_lanes(x, width):
    if width % LANES == 0:
        return jnp.concatenate([x] * (width // LANES), axis=1)
    return jnp.broadcast_to(x[:, 0:1], (x.shape[0], width))


def _lanes_to_rows(x, lane0):
    r = lax.broadcasted_iota(I32, (SUBLANES, LANES), 0)
    c = lax.broadcasted_iota(I32, (SUBLANES, LANES), 1)
    sel = (c == r + lane0).astype(F32)
    return lax.dot_general(sel, x, (((1,), (1,)), ((), ())), precision=HIGHEST,
                           preferred_element_type=F32)


def _proj_kernel(x_ref, w_ref, bf_ref, u_ref):
    u = jnp.dot(x_ref[...].astype(BF16), w_ref[...], preferred_element_type=F32)
    u_ref[...] = u
    small = u[:, COL_S:]
    lane = lax.broadcasted_iota(I32, small.shape, 1)
    logf = -_softplus(-(small + bf_ref[...]))
    u_ref[:, COL_S:] = jnp.where(lane < LANE_A, logf, small)


def _proj(x, w_bf16, bf_row):
    n, d = x.shape
    m = w_bf16.shape[1]
    return pl.pallas_call(
        _proj_kernel,
        grid=(n // TM_PROJ,),
        in_specs=[pl.BlockSpec((TM_PROJ, d), lambda i: (i, 0)),
                  pl.BlockSpec((d, m), lambda i: (0, 0)),
                  pl.BlockSpec((1, LANES), lambda i: (0, 0))],
        out_specs=pl.BlockSpec((TM_PROJ, m), lambda i: (i, 0)),
        out_shape=jax.ShapeDtypeStruct((n, m), F32),
        compiler_params=_cparams(("parallel",)),
        name="in_proj",
    )(x, w_bf16, bf_row)


def _cumsum_kernel(lf_ref, crep_ref, crow_ref, carry_ref):
    @pl.when(pl.program_id(1) == 0)
    def _():
        carry_ref[...] = jnp.zeros_like(carry_ref)

    lf = lf_ref[0]
    t = lf.shape[0]
    r = lax.broadcasted_iota(I32, (t, t), 0)
    c = lax.broadcasted_iota(I32, (t, t), 1)
    tril = (c <= r).astype(F32)
    cs = jnp.dot(tril, lf, precision=HIGHEST, preferred_element_type=F32) + carry_ref[0:1, :]
    carry_ref[...] = jnp.broadcast_to(cs[t - 1:t, :], carry_ref.shape)
    c2 = cs * LOG2E
    crow_ref[0] = _lanes_to_rows(c2, LANE_F)
    for h in range(N_HEADS):
        crep_ref[0, :, h * HEAD_DIM:(h + 1) * HEAD_DIM] = jnp.broadcast_to(
            c2[:, LANE_F + h:LANE_F + h + 1], (t, HEAD_DIM))


def _cumsum(arr, n_batch, length, tl, col_block):
    return pl.pallas_call(
        _cumsum_kernel,
        grid=(n_batch, length // tl),
        in_specs=[pl.BlockSpec((1, tl, LANES), lambda b, j: (b, j, col_block))],
        out_specs=[pl.BlockSpec((1, tl, WIDTH), lambda b, j: (b, j, 0)),
                   pl.BlockSpec((1, SUBLANES, tl), lambda b, j: (b, 0, j))],
        out_shape=[jax.ShapeDtypeStruct((n_batch, length, WIDTH), F32),
                   jax.ShapeDtypeStruct((n_batch, SUBLANES, length), F32)],
        scratch_shapes=[pltpu.VMEM((SUBLANES, LANES), F32)],
        compiler_params=_cparams(("parallel", "arbitrary")),
        name="logf_cumsum",
    )(arr)


def _fox_kernel(qi_ref, kj_ref, last_ref, q_ref, k_ref, v_ref, cq_ref, ck_ref, o_ref,
                m_ref, l_ref, acc_ref, *, tq, tk, past):
    s_idx = pl.program_id(1)
    qi = qi_ref[s_idx]
    kj = kj_ref[s_idx]

    @pl.when(kj == 0)
    def _():
        m_ref[...] = jnp.full_like(m_ref, NEG_INF)
        l_ref[...] = jnp.zeros_like(l_ref)
        acc_ref[...] = jnp.zeros_like(acc_ref)

    def update(masked):
        if masked:
            q_pos = past + qi * tq + lax.broadcasted_iota(I32, (tq, tk), 0)
            k_pos = kj * tk + lax.broadcasted_iota(I32, (tq, tk), 1)
            visible = k_pos <= q_pos
        for h in range(N_HEADS):
            cols = slice(h * HEAD_DIM, (h + 1) * HEAD_DIM)
            q = q_ref[0, :, cols] * (HEAD_DIM ** -0.5 * LOG2E)
            t = _dot_nt(q, k_ref[0, :, cols]) - ck_ref[0, h:h + 1, :]
            if masked:
                t = jnp.where(visible, t, NEG_INF)
            cq = cq_ref[0, :, cols]
            m_prev = m_ref[h]
            m_new = jnp.maximum(m_prev, jnp.max(t, axis=-1, keepdims=True) + cq)
            p = jnp.exp2(t - _spread_lanes(m_new - cq, tk))
            alpha = jnp.exp2(m_prev - m_new)
            l_ref[h] = alpha * l_ref[h] + jnp.sum(p, axis=-1, keepdims=True)
            acc_ref[:, cols] = alpha * acc_ref[:, cols] + _dot(p, v_ref[0, :, cols])
            m_ref[h] = m_new

    crosses_diagonal = kj * tk + (tk - 1) > past + qi * tq
    pl.when(crosses_diagonal)(functools.partial(update, True))
    pl.when(jnp.logical_not(crosses_diagonal))(functools.partial(update, False))

    @pl.when(last_ref[s_idx] == 1)
    def _():
        for h in range(N_HEADS):
            cols = slice(h * HEAD_DIM, (h + 1) * HEAD_DIM)
            o_ref[0, :, cols] = acc_ref[:, cols] / l_ref[h]


def _fox_schedule(n_q, tq, tk, past):
    qi, kj, last = [], [], []
    for i in range(n_q):
        hi = (past + (i + 1) * tq - 1) // tk
        for j in range(hi + 1):
            qi.append(i)
            kj.append(j)
            last.append(1 if j == hi else 0)
    return (jnp.asarray(np.array(qi, np.int32)), jnp.asarray(np.array(kj, np.int32)),
            jnp.asarray(np.array(last, np.int32)))


def _fox(q_arr, q_map, k_arr, k_map, v_arr, v_map, cq_arr, cq_map, ck_arr, ck_map,
         n_batch, n_q, tq, tk, past):
    qi, kj, last = _fox_schedule(n_q, tq, tk, past)
    n_steps = int(qi.shape[0])
    spec = lambda shape, fn, tab: pl.BlockSpec(shape, lambda b, s, qi_r, kj_r, la_r: fn(b, (qi_r if tab == 'q' else kj_r)[s]))
    return pl.pallas_call(
        functools.partial(_fox_kernel, tq=tq, tk=tk, past=past),
        grid_spec=pltpu.PrefetchScalarGridSpec(
            num_scalar_prefetch=3,
            grid=(n_batch, n_steps),
            in_specs=[spec((1, tq, WIDTH), q_map, 'q'),
                      spec((1, tk, WIDTH), k_map, 'k'),
                      spec((1, tk, WIDTH), v_map, 'k'),
                      spec((1, tq, WIDTH), cq_map, 'q'),
                      spec((1, SUBLANES, tk), ck_map, 'k')],
            out_specs=spec((1, tq, WIDTH), lambda b, i: (b, i, 0), 'q'),
            scratch_shapes=[pltpu.VMEM((N_HEADS, tq, HEAD_DIM), F32),
                            pltpu.VMEM((N_HEADS, tq, HEAD_DIM), F32),
                            pltpu.VMEM((tq, WIDTH), F32)]),
        out_shape=jax.ShapeDtypeStruct((n_batch, n_q * tq, WIDTH), F32),
        compiler_params=_cparams(("parallel", "arbitrary")),
        name="fox_attention",
    )(qi, kj, last, q_arr, k_arr, v_arr, cq_arr, ck_arr)


def _gdn_kernel(first_ref, last_ref, seq_ref,
                pre_ref, z_ref, sm_ref, cpast_ref, convw_ref, s0_ref, alog_ref, dtb_ref, ng_ref,
                o_ref, sout_ref, stage_ref, s_ref, *, rows):
    step = pl.program_id(0)
    halo = SUBLANES
    n_chunks = rows // CHUNK

    @pl.when(first_ref[step] == 1)
    def _():
        stage_ref[0:halo, :] = cpast_ref[0]
        s_ref[...] = s0_ref[0]

    stage_ref[halo:halo + rows, :] = pre_ref[0]
    conv = stage_ref[halo:halo + rows, :] * convw_ref[CONV_WIDTH - 1:CONV_WIDTH, :]
    for j in range(1, CONV_WIDTH):
        conv = conv + (stage_ref[halo - j:halo - j + rows, :]
                       * convw_ref[CONV_WIDTH - 1 - j:CONV_WIDTH - j, :])
    stage_ref[0:halo, :] = stage_ref[rows:rows + halo, :]
    act = _silu(conv)

    small = sm_ref[0]
    beta_all = _sigmoid(small)
    g_all = -jnp.exp(alog_ref[...]) * _softplus(small + dtb_ref[...])
    r = lax.broadcasted_iota(I32, (rows, rows), 0)
    c = lax.broadcasted_iota(I32, (rows, rows), 1)
    same_chunk = (r // CHUNK) == (c // CHUNK)
    incl = same_chunk & (c <= r)
    strict = same_chunk & (c < r)
    eye = (c == r).astype(F32)
    gc_all = jnp.dot(incl.astype(F32), g_all, precision=HIGHEST, preferred_element_type=F32)
    gc_rows = _lanes_to_rows(gc_all, LANE_A)

    for h in range(N_HEADS):
        cols = slice(h * HEAD_DIM, (h + 1) * HEAD_DIM)
        q = act[:, h * HEAD_DIM:(h + 1) * HEAD_DIM]
        k = act[:, WIDTH + h * HEAD_DIM:WIDTH + (h + 1) * HEAD_DIM]
        v = act[:, 2 * WIDTH + h * HEAD_DIM:2 * WIDTH + (h + 1) * HEAD_DIM]
        q = q * lax.rsqrt(jnp.sum(q * q, axis=-1, keepdims=True) + NORM_EPS) * (HEAD_DIM ** -0.5)
        k = k * lax.rsqrt(jnp.sum(k * k, axis=-1, keepdims=True) + NORM_EPS)
        beta = beta_all[:, LANE_B + h:LANE_B + h + 1]
        gc = gc_all[:, LANE_A + h:LANE_A + h + 1]
        diff = gc - gc_rows[h:h + 1, :]
        decay = jnp.where(incl, jnp.exp(jnp.where(incl, diff, 0.0)), 0.0)
        kb = k * beta
        vb = v * beta
        low = jnp.where(strict, _dot_nt(kb, k) * decay, 0.0)
        inv = eye - low
        pw = _dot(low, low)
        n_sq = CHUNK.bit_length() - 2
        for it in range(n_sq):
            inv = inv + _dot(inv, pw)
            if it + 1 < n_sq:
                pw = _dot(pw, pw)
        egc = jnp.exp(gc)
        uw = _dot(inv, jnp.concatenate([vb, kb * egc], axis=1))
        intra = jnp.where(incl, _dot_nt(q, k) * decay, 0.0)
        qd = q * egc
        g_last = [gc[(g + 1) * CHUNK - 1:(g + 1) * CHUNK, :] for g in range(n_chunks)]
        kd = k * jnp.exp(jnp.concatenate([jnp.broadcast_to(gl, (CHUNK, 1)) for gl in g_last], axis=0)
                         - gc)
        state = s_ref[h]
        v_new = []
        for g in range(n_chunks):
            rs = slice(g * CHUNK, (g + 1) * CHUNK)
            v_new.append(uw[rs, :HEAD_DIM] - _dot(uw[rs, HEAD_DIM:], state))
            v_rows = jnp.concatenate(
                v_new + [jnp.zeros((rows - (g + 1) * CHUNK, HEAD_DIM), F32)] * (g + 1 < n_chunks), axis=0)
            o = _dot(qd[rs, :], state) + _dot(intra[rs, :], v_rows)
            state = state * jnp.exp(g_last[g]) + _dot_tn(kd[rs, :], v_new[g])
            o = (o * lax.rsqrt(jnp.mean(o * o, axis=-1, keepdims=True) + NORM_EPS)
                 * ng_ref[...] * _silu(z_ref[0, rs, cols]))
            o_ref[0, rs, cols] = o
        s_ref[h] = state

    @pl.when(last_ref[step] == 1)
    def _():
        sout_ref[0] = s_ref[...]


def _gdn(u_view, blk0, first, last, seq, conv_past, conv_w, s0, alog_row, dtb_row, ng_row):
    rows = u_view.shape[1]
    n_steps = int(first.shape[0])
    n_seq = s0.shape[0]
    gw = 3 * WIDTH
    return pl.pallas_call(
        functools.partial(_gdn_kernel, rows=rows),
        grid_spec=pltpu.PrefetchScalarGridSpec(
            num_scalar_prefetch=3,
            grid=(n_steps,),
            in_specs=[pl.BlockSpec((1, rows, gw), lambda s, f, l, q: (blk0 + s, 0, COL_G // gw)),
                      pl.BlockSpec((1, rows, WIDTH), lambda s, f, l, q: (blk0 + s, 0, COL_Z // WIDTH)),
                      pl.BlockSpec((1, rows, LANES), lambda s, f, l, q: (blk0 + s, 0, COL_S // LANES)),
                      pl.BlockSpec((1, SUBLANES, gw), lambda s, f, l, q: (q[s], 0, 0)),
                      pl.BlockSpec((SUBLANES, gw), lambda s, f, l, q: (0, 0)),
                      pl.BlockSpec((1, N_HEADS, HEAD_DIM, HEAD_DIM), lambda s, f, l, q: (q[s], 0, 0, 0)),
                      pl.BlockSpec((1, LANES), lambda s, f, l, q: (0, 0)),
                      pl.BlockSpec((1, LANES), lambda s, f, l, q: (0, 0)),
                      pl.BlockSpec((1, LANES), lambda s, f, l, q: (0, 0))],
            out_specs=[pl.BlockSpec((1, rows, WIDTH), lambda s, f, l, q: (s, 0, 0)),
                       pl.BlockSpec((1, N_HEADS, HEAD_DIM, HEAD_DIM), lambda s, f, l, q: (q[s], 0, 0, 0))],
            scratch_shapes=[pltpu.VMEM((rows + SUBLANES, gw), F32),
                            pltpu.VMEM((N_HEADS, HEAD_DIM, HEAD_DIM), F32)]),
        out_shape=[jax.ShapeDtypeStruct((n_steps, rows, WIDTH), F32),
                   jax.ShapeDtypeStruct((n_seq, N_HEADS, HEAD_DIM, HEAD_DIM), F32)],
        compiler_params=_cparams(("arbitrary",)),
        name="gated_deltanet",
    )(first, last, seq, u_view, u_view, u_view, conv_past, conv_w, s0, alog_row, dtb_row, ng_row)


def _outproj_ln_kernel(x_ref, of_ref, og_ref, w_ref, g_ref, b_ref, h_ref):
    mix = _dot(of_ref[...], w_ref[0:WIDTH, :]) + _dot(og_ref[...], w_ref[WIDTH:2 * WIDTH, :])
    h_ref[...] = _layer_norm(DN_ALPHA * x_ref[...] + mix, g_ref[...], b_ref[...])


def _outproj_ln(x, o_fox, o_gdn, w_bf16, g_row, b_row):
    n, d = x.shape
    row = lambda i: (i, 0)
    fixed = lambda i: (0, 0)
    return pl.pallas_call(
        _outproj_ln_kernel,
        grid=(n // TM_TOK,),
        in_specs=[pl.BlockSpec((TM_TOK, d), row), pl.BlockSpec((TM_TOK, WIDTH), row),
                  pl.BlockSpec((TM_TOK, WIDTH), row), pl.BlockSpec((2 * WIDTH, d), fixed),
                  pl.BlockSpec((1, d), fixed), pl.BlockSpec((1, d), fixed)],
        out_specs=pl.BlockSpec((TM_TOK, d), row),
        out_shape=jax.ShapeDtypeStruct((n, d), F32),
        compiler_params=_cparams(("parallel",)),
        name="out_proj_ln",
    )(x, o_fox, o_gdn, w_bf16, g_row, b_row)


def _pool_ln_kernel(x_ref, halo_ref, pw_ref, ps_ref, w_ref, g_ref, b_ref, h_ref, stage_ref,
                    *, tm, pos0, zero_first_halo):
    i = pl.program_id(0)
    stage_ref[0:POOL_HALO, :] = halo_ref[0]
    if zero_first_halo:
        @pl.when(i == 0)
        def _():
            stage_ref[0:POOL_HALO, :] = jnp.zeros((POOL_HALO, stage_ref.shape[1]), F32)
    x = x_ref[...]
    stage_ref[POOL_HALO:POOL_HALO + tm, :] = x
    gdim = x.shape[1] // len(POOL_WINDOWS)
    pos = pos0 + lax.broadcasted_iota(I32, (tm, 1), 0)
    if zero_first_halo:
        pos = pos + i * tm
    parts = []
    for gi, win in enumerate(POOL_WINDOWS):
        cols = slice(gi * gdim, (gi + 1) * gdim)
        s = stage_ref[POOL_HALO:POOL_HALO + tm, cols]
        for j in range(1, win):
            s = s + stage_ref[POOL_HALO - j:POOL_HALO - j + tm, cols]
        cnt = jnp.minimum(pos + 1, win).astype(F32)
        zg = s / cnt - x[:, cols]
        parts.append(_dot(zg, pw_ref[gi]))
    zg = jnp.concatenate(parts, axis=-1) * ps_ref[...]
    mix = _dot(zg, w_ref[...])
    h_ref[...] = _layer_norm(DN_ALPHA * x + mix, g_ref[...], b_ref[...])


def _pool_ln(x, x_map, halo_arr, halo_map, n_tiles, tm, pos0, zero_first_halo,
             pw_bf16, ps_row, w_bf16, g_row, b_row):
    d = x.shape[1]
    gdim = d // len(POOL_WINDOWS)
    fixed = lambda i: (0, 0)
    return pl.pallas_call(
        functools.partial(_pool_ln_kernel, tm=tm, pos0=pos0, zero_first_halo=zero_first_halo),
        grid=(n_tiles,),
        in_specs=[pl.BlockSpec((tm, d), x_map),
                  pl.BlockSpec((1, POOL_HALO, d), halo_map),
                  pl.BlockSpec((len(POOL_WINDOWS), gdim, gdim), lambda i: (0, 0, 0)),
                  pl.BlockSpec((1, d), fixed), pl.BlockSpec((d, d), fixed),
                  pl.BlockSpec((1, d), fixed), pl.BlockSpec((1, d), fixed)],
        out_specs=pl.BlockSpec((tm, d), lambda i: (i, 0)),
        out_shape=jax.ShapeDtypeStruct((n_tiles * tm, d), F32),
        scratch_shapes=[pltpu.VMEM((POOL_HALO + tm, d), F32)],
        compiler_params=_cparams(("arbitrary",)),
        name="pool_mixer_ln",
    )(x, halo_arr, pw_bf16, ps_row, w_bf16, g_row, b_row)


def _router_kernel(h_ref, wr_ref, br_ref, idx_ref, gate_ref, rank_ref, cnt_ref, carry_ref):
    @pl.when(pl.program_id(0) == 0)
    def _():
        carry_ref[...] = jnp.zeros_like(carry_ref)

    tm = h_ref.shape[0]
    lane = lax.broadcasted_iota(I32, (tm, LANES), 1).astype(F32)
    logits = jnp.dot(h_ref[...], wr_ref[...], precision=HIGHEST, preferred_element_type=F32)
    work = jnp.where(lane < N_EXPERTS, logits + br_ref[...], -jnp.inf)
    vals, ids = [], []
    for _ in range(TOP_K):
        m = jnp.max(work, axis=-1, keepdims=True)
        ik = jnp.min(jnp.where(work == m, lane, float(LANES)), axis=-1, keepdims=True)
        vals.append(m)
        ids.append(ik)
        work = jnp.where(lane == ik, -jnp.inf, work)
    exps = [jnp.exp(v - vals[0]) for v in vals]
    denom = exps[0]
    for e in exps[1:]:
        denom = denom + e
    multihot = jnp.zeros((tm, LANES), F32)
    idx_out = jnp.zeros((tm, LANES), F32)
    gate_out = jnp.zeros((tm, LANES), F32)
    for k in range(TOP_K):
        multihot = multihot + (lane == ids[k]).astype(F32)
        idx_out = jnp.where(lane == k, ids[k], idx_out)
        gate_out = jnp.where(lane == k, exps[k] / denom, gate_out)
    r = lax.broadcasted_iota(I32, (tm, tm), 0)
    c = lax.broadcasted_iota(I32, (tm, tm), 1)
    before = _dot((c < r).astype(F32), multihot) + carry_ref[0:1, :]
    rank_out = jnp.zeros((tm, LANES), F32)
    for k in range(TOP_K):
        rk = jnp.sum(jnp.where(lane == ids[k], before, 0.0), axis=-1, keepdims=True)
        rank_out = jnp.where(lane == k, rk, rank_out)
    idx_ref[...] = idx_out.astype(I32)
    gate_ref[...] = gate_out
    rank_ref[...] = rank_out.astype(I32)
    total = carry_ref[0:1, :] + jnp.sum(multihot, axis=0, keepdims=True)
    carry_ref[...] = jnp.broadcast_to(total, carry_ref.shape)
    cnt_ref[...] = jnp.broadcast_to(total, cnt_ref.shape).astype(I32)


def _router(h, wr_pad, br_row):
    n, d = h.shape
    row = lambda i: (i, 0)
    fixed = lambda i: (0, 0)
    return pl.pallas_call(
        _router_kernel,
        grid=(n // TM_TOK,),
        in_specs=[pl.BlockSpec((TM_TOK, d), row), pl.BlockSpec((d, LANES), fixed),
                  pl.BlockSpec((1, LANES), fixed)],
        out_specs=[pl.BlockSpec((TM_TOK, LANES), row), pl.BlockSpec((TM_TOK, LANES), row),
                   pl.BlockSpec((TM_TOK, LANES), row), pl.BlockSpec((SUBLANES, LANES), fixed)],
        out_shape=[jax.ShapeDtypeStruct((n, LANES), I32), jax.ShapeDtypeStruct((n, LANES), F32),
                   jax.ShapeDtypeStruct((n, LANES), I32), jax.ShapeDtypeStruct((SUBLANES, LANES), I32)],
        scratch_shapes=[pltpu.VMEM((SUBLANES, LANES), F32)],
        compiler_params=_cparams(("arbitrary",)),
        name="moe_router",
    )(h, wr_pad, br_row)


def _row_gather(h_ref, tok, x_ref, r, sem):
    return pltpu.make_async_copy(h_ref.at[pl.ds(tok, 1)], x_ref.at[pl.ds(r, 1)], sem)


def _row_scatter(y_ref, r, out_ref, dst, sem):
    return pltpu.make_async_copy(y_ref.at[pl.ds(r, 1)], out_ref.at[pl.ds(dst, 1)], sem)


def _all_rows(buf_ref, sem):
    return pltpu.make_async_copy(buf_ref, buf_ref, sem)


def _expert_kernel(be_ref, tok_cur_ref, tok_next_ref, dst_prev_ref, dst_cur_ref,
                   h_ref, wup_ref, bup_ref, wdn_ref, bdn_ref, y_ref,
                   xbuf_ref, ybuf_ref, wup_bf_ref, wdn_bf_ref, gsem, ssem):
    b = pl.program_id(0)
    n_b = pl.num_programs(0)

    @pl.when(b == 0)
    def _():
        ybuf_ref[1] = jnp.zeros(ybuf_ref.shape[1:], F32)

        def first(r, c):
            _row_gather(h_ref, tok_cur_ref[0, 0, r], xbuf_ref.at[0], r, gsem.at[0]).start()
            return c
        lax.fori_loop(0, MOE_BLOCK, first, 0)

    @pl.when((b == 0) | (be_ref[b] != be_ref[jnp.maximum(b - 1, 0)]))
    def _():
        wup_bf_ref[...] = wup_ref[0, 0].astype(BF16)
        wdn_bf_ref[...] = wdn_ref[0, 0].astype(BF16)

    def step(p):
        q = 1 - p
        _all_rows(xbuf_ref.at[p], gsem.at[p]).wait()
        d_exp = wdn_ref.shape[2]
        hu = jnp.dot(xbuf_ref[p].astype(BF16), wup_bf_ref[...], preferred_element_type=F32) + bup_ref[0, 0]
        for r in range(MOE_BLOCK):
            _row_gather(h_ref, tok_next_ref[0, 0, r], xbuf_ref.at[q], r, gsem.at[q]).start()
            _row_scatter(ybuf_ref.at[q], r, y_ref, dst_prev_ref[0, 0, r], ssem.at[q]).start()
        glu = jnp.minimum(hu[:, :d_exp], SWIGLU_LIMIT)
        lin = jnp.clip(hu[:, d_exp:], -SWIGLU_LIMIT, SWIGLU_LIMIT)
        a = glu * _sigmoid(SWIGLU_ALPHA * glu) * (lin + 1.0)
        ybuf_ref[p] = jnp.dot(a.astype(BF16), wdn_bf_ref[...], preferred_element_type=F32) + bdn_ref[0, 0]
        _all_rows(ybuf_ref.at[q], ssem.at[q]).wait()

    pl.when(b % 2 == 0)(functools.partial(step, 0))
    pl.when(b % 2 == 1)(functools.partial(step, 1))

    @pl.when(b == n_b - 1)
    def _():
        for p in range(2):
            @pl.when(b % 2 == p)
            def _():
                def last(r, c):
                    _row_scatter(ybuf_ref.at[p], r, y_ref, dst_cur_ref[0, 0, r], ssem.at[p]).start()
                    return c
                lax.fori_loop(0, MOE_BLOCK, last, 0)
                _all_rows(ybuf_ref.at[p], ssem.at[p]).wait()
                _all_rows(xbuf_ref.at[1 - p], gsem.at[1 - p]).wait()


def _experts(h, block_e, tok_of, dst_of, layer, w_up, b_up, w_dn, b_dn):
    n, d = h.shape
    n_blocks = tok_of.shape[0]
    d_up = w_up.shape[3]
    d_exp = w_dn.shape[2]
    wsel = lambda b, be: (layer, be[b], 0, 0)
    table = lambda fn: pl.BlockSpec((1, 1, MOE_BLOCK), lambda b, be: (fn(b), 0, 0), memory_space=pltpu.SMEM)
    return pl.pallas_call(
        _expert_kernel,
        grid_spec=pltpu.PrefetchScalarGridSpec(
            num_scalar_prefetch=1,
            grid=(n_blocks,),
            in_specs=[table(lambda b: b),
                      table(lambda b: jnp.minimum(b + 1, n_blocks - 1)),
                      table(lambda b: jnp.maximum(b - 1, 0)),
                      table(lambda b: b),
                      pl.BlockSpec(memory_space=pl.ANY),
                      pl.BlockSpec((1, 1, d, d_up), wsel),
                      pl.BlockSpec((1, 1, 1, d_up), wsel),
                      pl.BlockSpec((1, 1, d_exp, d), wsel),
                      pl.BlockSpec((1, 1, 1, d), wsel)],
            out_specs=pl.BlockSpec(memory_space=pl.ANY),
            scratch_shapes=[pltpu.VMEM((2, MOE_BLOCK, d), F32), pltpu.VMEM((2, MOE_BLOCK, d), F32),
                            pltpu.VMEM((d, d_up), BF16), pltpu.VMEM((d_exp, d), BF16),
                            pltpu.SemaphoreType.DMA((2,)), pltpu.SemaphoreType.DMA((2,))]),
        out_shape=jax.ShapeDtypeStruct((n_blocks * MOE_BLOCK, d), F32),
        compiler_params=_cparams(("arbitrary",)),
        name="moe_experts",
    )(block_e, tok_of, tok_of, dst_of, dst_of, h, w_up, b_up, w_dn, b_dn)


def _combine_kernel(h_ref, y0_ref, y1_ref, y2_ref, y3_ref, gate_ref, p_ref, wpg_ref, wpp_ref,
                    g_ref, b_ref, out_ref):
    gate = gate_ref[...]
    moe = y0_ref[...] * gate[:, 0:1]
    for k, y_ref in enumerate((y1_ref, y2_ref, y3_ref), start=1):
        moe = moe + y_ref[...] * gate[:, k:k + 1]
    h2 = _layer_norm(DN_ALPHA * h_ref[...] + moe, g_ref[...], b_ref[...])
    out_ref[...] = h2 + _sigmoid(_dot(h2, wpg_ref[...])) * _dot(p_ref[...], wpp_ref[...])


def _combine(h, y, gate, p, wpg_bf16, wpp_bf16, g_row, b_row):
    n, d = h.shape
    n_tiles = n // TM_TOK
    row = lambda i: (i, 0)
    fixed = lambda i: (0, 0)
    choice = lambda k: pl.BlockSpec((TM_TOK, d), lambda i: (k * n_tiles + i, 0))
    return pl.pallas_call(
        _combine_kernel,
        grid=(n_tiles,),
        in_specs=[pl.BlockSpec((TM_TOK, d), row)] + [choice(k) for k in range(TOP_K)]
                 + [pl.BlockSpec((TM_TOK, LANES), row), pl.BlockSpec((TM_TOK, p.shape[1]), row),
                    pl.BlockSpec((d, d), fixed), pl.BlockSpec((p.shape[1], d), fixed),
                    pl.BlockSpec((1, d), fixed), pl.BlockSpec((1, d), fixed)],
        out_specs=pl.BlockSpec((TM_TOK, d), row),
        out_shape=jax.ShapeDtypeStruct((n, d), F32),
        compiler_params=_cparams(("parallel",)),
        name="moe_combine_ln_embed",
    )(h, y, y, y, y, gate, p, wpg_bf16, wpp_bf16, g_row, b_row)


def _layer_tail(h, p, layer, g2, b2, w_r, b_r, w_up, b_up, w_dn, b_dn, w_pg, w_pp):
    n, d = h.shape
    wr_pad = jnp.pad(w_r, ((0, 0), (0, LANES - N_EXPERTS)))
    br_row = jnp.pad(b_r, (0, LANES - N_EXPERTS))[None]
    idx, gate, rank, cnt = _router(h, wr_pad, br_row)
    counts = cnt[0, :N_EXPERTS]
    padded = (counts + MOE_BLOCK - 1) // MOE_BLOCK * MOE_BLOCK
    pend = jnp.cumsum(padded).astype(I32)
    pstart = pend - padded
    slot = (pstart[idx[:, :TOP_K]] + rank[:, :TOP_K]).reshape(-1).astype(I32)
    n_asg = n * TOP_K
    n_blocks = n_asg // MOE_BLOCK + N_EXPERTS
    assert n_asg % MOE_BLOCK == 0 and n % TM_TOK == 0
    n_slots = n_blocks * MOE_BLOCK
    n_used = pend[-1] // MOE_BLOCK
    blk = jnp.minimum(jnp.arange(n_blocks, dtype=I32), n_used - 1) * MOE_BLOCK
    block_e = jnp.minimum(jnp.sum((pend[None, :] <= blk[:, None]).astype(I32), axis=1), N_EXPERTS - 1)
    asg = jnp.full((n_slots,), -1, I32).at[slot].set(jnp.arange(n_asg, dtype=I32))
    is_pad = asg < 0
    tok_of = jnp.where(is_pad, 0, asg // TOP_K)
    dst_of = jnp.where(is_pad, n_asg - 1 + jnp.cumsum(is_pad.astype(I32)), (asg % TOP_K) * n + asg // TOP_K)
    y = _experts(h, block_e, tok_of.reshape(n_blocks, 1, MOE_BLOCK), dst_of.reshape(n_blocks, 1, MOE_BLOCK),
                 layer, w_up, b_up[:, :, None, :], w_dn, b_dn[:, :, None, :])
    return _combine(h, y, gate, p, w_pg.astype(BF16), w_pp.astype(BF16), g2[None], b2[None])


def _lane_row(v, lane0):
    return jnp.zeros((1, LANES), F32).at[0, lane0:lane0 + v.shape[0]].set(v.astype(F32))


def kernel(x_prompt, x_sample, cache_fox_k, cache_fox_v, cache_fox_logf, state_gdn, state_gdn_conv,
           cache_pool, p_prompt, p_sample, w_in_ab, b_fgate, gdn_a_log, gdn_dt_bias, gdn_conv_w,
           gdn_norm_g, w_out_ab, pool_w, pool_scale, w_out_pool, ln1_g, ln1_b, ln2_g, ln2_b,
           w_router, b_router, w_expert_up, b_expert_up, w_expert_down, b_expert_down,
           w_ple_gate, w_ple_proj):
    n_pb, seq, d = x_prompt.shape
    n_sb, dseq, _ = x_sample.shape
    past = cache_fox_k.shape[2]
    assert n_pb == 1 and dseq == CHUNK and past % dseq == 0 and seq % TQ == 0
    n_p = n_pb * seq
    n_s = n_sb * dseq
    n = n_p + n_s
    x = jnp.concatenate([x_prompt.reshape(n_p, d), x_sample.reshape(n_s, d)], axis=0)

    def tail(h, i):
        p = jnp.concatenate([p_prompt[i].reshape(n_p, -1), p_sample[i].reshape(n_s, -1)], axis=0)
        return _layer_tail(h, p, i, ln2_g[i], ln2_b[i], w_router[i], b_router[i], w_expert_up,
                           b_expert_up, w_expert_down, b_expert_down, w_ple_gate[i], w_ple_proj[i])

    w_in = w_in_ab[0]
    n_small = 3 * N_HEADS
    ff0 = 3 * WIDTH
    gq0 = ff0 + N_HEADS
    ga0 = gq0 + 4 * WIDTH
    w_small = jnp.concatenate([w_in[:, ff0:gq0], w_in[:, ga0:ga0 + 2 * N_HEADS],
                               jnp.zeros((d, LANES - n_small), F32)], axis=1)
    w_all = jnp.concatenate([w_in[:, :ff0], w_in[:, gq0:ga0], w_small], axis=1).astype(BF16)
    u = _proj(x, w_all, _lane_row(b_fgate[0], LANE_F))

    u3 = u[None]
    u64 = u.reshape(n // CHUNK, CHUNK, U_COLS)
    cq_p, ck_p = _cumsum(u3, 1, n_p, TK, COL_S // LANES)
    lf_s = jnp.concatenate(
        [jnp.pad(cache_fox_logf[0].astype(F32), ((0, 0), (0, 0), (0, LANES - N_HEADS))),
         u[n_p:, COL_S:].reshape(n_sb, dseq, LANES)], axis=1)
    cq_s, ck_s = _cumsum(lf_s, n_sb, past + dseq, past + dseq, 0)

    of_p = _fox(u3, lambda b, i: (0, i, COL_Q // WIDTH), u3, lambda b, j: (0, j, COL_K // WIDTH),
                u3, lambda b, j: (0, j, COL_V // WIDTH), cq_p, lambda b, i: (0, i, 0),
                ck_p, lambda b, j: (0, 0, j), 1, n_p // TQ, TQ, TK, 0)
    k_new = u[n_p:, COL_K:COL_K + WIDTH].reshape(n_sb, dseq, WIDTH)
    v_new = u[n_p:, COL_V:COL_V + WIDTH].reshape(n_sb, dseq, WIDTH)
    k_all = jnp.concatenate([cache_fox_k[0].reshape(n_sb, past, WIDTH), k_new], axis=1)
    v_all = jnp.concatenate([cache_fox_v[0].reshape(n_sb, past, WIDTH), v_new], axis=1)
    of_s = _fox(u64, lambda b, i: (n_p // CHUNK + b, 0, COL_Q // WIDTH), k_all, lambda b, j: (b, 0, 0),
                v_all, lambda b, j: (b, 0, 0), cq_s, lambda b, i: (b, past // dseq, 0),
                ck_s, lambda b, j: (b, 0, 0), n_sb, 1, dseq, past + dseq, past)
    o_fox = jnp.concatenate([of_p.reshape(n_p, WIDTH), of_s.reshape(n_s, WIDTH)], axis=0)

    gw = 3 * WIDTH
    conv_w = jnp.pad(gdn_conv_w[0], ((0, SUBLANES - CONV_WIDTH), (0, 0)))
    gdn_args = (conv_w, _lane_row(gdn_a_log[0], LANE_A), _lane_row(gdn_dt_bias[0], LANE_A),
                gdn_norm_g[0][None])
    n_pstep = n_p // GDN_ROWS
    ends = lambda k: (jnp.asarray((np.arange(k) == 0).astype(np.int32)),
                      jnp.asarray((np.arange(k) == k - 1).astype(np.int32)))
    og_p, st_p = _gdn(u.reshape(n // GDN_ROWS, GDN_ROWS, U_COLS), 0, *ends(n_pstep),
                      jnp.zeros((n_pstep,), I32), jnp.zeros((1, SUBLANES, gw), F32), gdn_args[0],
                      jnp.zeros((1, N_HEADS, HEAD_DIM, HEAD_DIM), F32), *gdn_args[1:])
    conv_past = jnp.pad(state_gdn_conv[0].astype(F32), ((0, 0), (SUBLANES - (CONV_WIDTH - 1), 0), (0, 0)))
    ones = jnp.ones((n_sb,), I32)
    og_s, st_s = _gdn(u64, n_p // CHUNK, ones, ones, jnp.arange(n_sb, dtype=I32), conv_past, gdn_args[0],
                      state_gdn[0].astype(F32), *gdn_args[1:])
    o_gdn = jnp.concatenate([og_p.reshape(n_p, WIDTH), og_s.reshape(n_s, WIDTH)], axis=0)
    h = _outproj_ln(x, o_fox, o_gdn, w_out_ab[0].astype(BF16), ln1_g[0][None], ln1_b[0][None])
    x1 = tail(h, 0)

    pool_args = (pool_w[0].astype(BF16), pool_scale[0][None], w_out_pool[0].astype(BF16),
                 ln1_g[1][None], ln1_b[1][None])
    ratio = TM_TOK // POOL_HALO
    x1_halo = x1.reshape(n // POOL_HALO, POOL_HALO, d)
    h_p = _pool_ln(x1, lambda i: (i, 0), x1_halo, lambda i: (jnp.maximum(i * ratio - 1, 0), 0, 0),
                   n_p // TM_TOK, TM_TOK, 0, True, *pool_args)
    cache16 = jnp.pad(cache_pool[0].astype(F32), ((0, 0), (POOL_HALO - POOL_STATE, 0), (0, 0)))
    h_s = _pool_ln(x1, lambda i: (n_p // dseq + i, 0), cache16, lambda i: (i, 0, 0),
                   n_sb, dseq, past, False, *pool_args)
    x2 = tail(jnp.concatenate([h_p, h_s], axis=0), 1)

    up = u[:n_p]
    us = u[n_p:]
    return (x2[:n_p].reshape(n_pb, seq, d), x2[n_p:].reshape(n_sb, dseq, d),
            up[:, COL_K:COL_K + WIDTH].reshape(1, n_pb, seq, N_HEADS, HEAD_DIM),
            up[:, COL_V:COL_V + WIDTH].reshape(1, n_pb, seq, N_HEADS, HEAD_DIM),
            up[:, COL_S:COL_S + N_HEADS].reshape(1, n_pb, seq, N_HEADS),
            st_p.reshape(1, n_pb, N_HEADS, HEAD_DIM, HEAD_DIM),
            up[seq - (CONV_WIDTH - 1):, COL_G:COL_G + gw].reshape(1, n_pb, CONV_WIDTH - 1, gw),
            x1[n_p - POOL_STATE:n_p].reshape(1, n_pb, POOL_STATE, d),
            us[:, COL_K:COL_K + WIDTH].reshape(1, n_sb, dseq, N_HEADS, HEAD_DIM),
            us[:, COL_V:COL_V + WIDTH].reshape(1, n_sb, dseq, N_HEADS, HEAD_DIM),
            us[:, COL_S:COL_S + N_HEADS].reshape(1, n_sb, dseq, N_HEADS),
            st_s.reshape(1, n_sb, N_HEADS, HEAD_DIM, HEAD_DIM),
            us[:, COL_G:COL_G + gw].reshape(n_sb, dseq, gw)[:, dseq - (CONV_WIDTH - 1):].reshape(
                1, n_sb, CONV_WIDTH - 1, gw),
            x1[n_p:].reshape(n_sb, dseq, d)[:, dseq - POOL_STATE:].reshape(1, n_sb, POOL_STATE, d))
```

```python
import functools

import numpy as np
import jax
import jax.numpy as jnp
from jax import lax
from jax.experimental import pallas as pl
from jax.experimental.pallas import tpu as pltpu

F32 = jnp.float32
BF16 = jnp.bfloat16
I32 = jnp.int32
HIGHEST = lax.Precision.HIGHEST

LANES = 128
SUBLANES = 8
VMEM_LIMIT = 56 * 1024 * 1024

HEAD_DIM = 128
N_HEADS = 4
WIDTH = N_HEADS * HEAD_DIM
CHUNK = 64
CONV_WIDTH = 4
POOL_WINDOWS = (2, 4, 8, 16)
POOL_HALO = 16
POOL_STATE = 15
N_EXPERTS = 32
TOP_K = 4
SWIGLU_LIMIT = 7.0
SWIGLU_ALPHA = 1.702
DEPTH = 2
DN_ALPHA = (2 * DEPTH) ** 0.25
LN_EPS = 1e-5
NORM_EPS = 1e-6
NEG_INF = -1e30
LOG2E = 1.4426950408889634

COL_Q, COL_K, COL_V = 0, WIDTH, 2 * WIDTH
COL_G = 3 * WIDTH
COL_Z = 6 * WIDTH
COL_S = 7 * WIDTH
U_COLS = COL_S + LANES
LANE_F, LANE_A, LANE_B = 0, N_HEADS, 2 * N_HEADS

TM_PROJ = 256
TM_TOK = 512
MOE_BLOCK = 256
TQ = 512
TK = 512
GDN_ROWS = 256


def _cparams(sem):
    return pltpu.CompilerParams(dimension_semantics=sem, vmem_limit_bytes=VMEM_LIMIT)


def _softplus(x):
    return jnp.maximum(x, 0.0) + jnp.log1p(jnp.exp(-jnp.abs(x)))


def _sigmoid(x):
    return 1.0 / (1.0 + jnp.exp(-x))


def _silu(x):
    return x * _sigmoid(x)


def _layer_norm(y, g, b):
    mu = jnp.mean(y, axis=-1, keepdims=True)
    yc = y - mu
    var = jnp.mean(yc * yc, axis=-1, keepdims=True)
    return yc * lax.rsqrt(var + LN_EPS) * g + b


def _dot(a, b):
    return jnp.dot(a.astype(BF16), b.astype(BF16), preferred_element_type=F32)


def _dot_nt(a, b):
    return lax.dot_general(a.astype(BF16), b.astype(BF16), (((1,), (1,)), ((), ())),
                           preferred_element_type=F32)


def _dot_tn(a, b):
    return lax.dot_general(a.astype(BF16), b.astype(BF16), (((0,), (0,)), ((), ())),
                           preferred_element_type=F32)


def _spread_lanes(x, width):
    if width % LANES == 0:
        return jnp.concatenate([x] * (width // LANES), axis=1)
    return jnp.broadcast_to(x[:, 0:1], (x.shape[0], width))


def _load_token_tiles(ref, lead, n_tok):
    return jnp.concatenate([ref[(*lead, pl.ds(j, n_tok, stride=SUBLANES), slice(None))]
                            for j in range(SUBLANES)], axis=1)


def _store_token_tiles(ref, lead, x):
    for j in range(SUBLANES):
        ref[(*lead, pl.ds(j, x.shape[0], stride=SUBLANES), slice(None))] = x[:, j * LANES:(j + 1) * LANES]


def _lanes_to_rows(x, lane0):
    r = lax.broadcasted_iota(I32, (SUBLANES, LANES), 0)
    c = lax.broadcasted_iota(I32, (SUBLANES, LANES), 1)
    sel = (c == r + lane0).astype(F32)
    return lax.dot_general(sel, x, (((1,), (1,)), ((), ())), precision=HIGHEST,
                           preferred_element_type=F32)


def _proj_kernel(x_ref, w_ref, bf_ref, u_ref):
    u = jnp.dot(x_ref[...].astype(BF16), w_ref[...], preferred_element_type=F32)
    u_ref[...] = u
    small = u[:, COL_S:]
    lane = lax.broadcasted_iota(I32, small.shape, 1)
    logf = -_softplus(-(small + bf_ref[...]))
    u_ref[:, COL_S:] = jnp.where(lane < LANE_A, logf, small)


def _proj(x, w_bf16, bf_row):
    n, d = x.shape
    m = w_bf16.shape[1]
    return pl.pallas_call(
        _proj_kernel,
        grid=(n // TM_PROJ,),
        in_specs=[pl.BlockSpec((TM_PROJ, d), lambda i: (i, 0)),
                  pl.BlockSpec((d, m), lambda i: (0, 0)),
                  pl.BlockSpec((1, LANES), lambda i: (0, 0))],
        out_specs=pl.BlockSpec((TM_PROJ, m), lambda i: (i, 0)),
        out_shape=jax.ShapeDtypeStruct((n, m), F32),
        compiler_params=_cparams(("parallel",)),
        name="in_proj",
    )(x, w_bf16, bf_row)


def _cumsum_kernel(lf_ref, crep_ref, crow_ref, carry_ref):
    @pl.when(pl.program_id(1) == 0)
    def _():
        carry_ref[...] = jnp.zeros_like(carry_ref)

    lf = lf_ref[0]
    t = lf.shape[0]
    r = lax.broadcasted_iota(I32, (t, t), 0)
    c = lax.broadcasted_iota(I32, (t, t), 1)
    tril = (c <= r).astype(F32)
    cs = jnp.dot(tril, lf, precision=HIGHEST, preferred_element_type=F32) + carry_ref[0:1, :]
    carry_ref[...] = jnp.broadcast_to(cs[t - 1:t, :], carry_ref.shape)
    c2 = cs * LOG2E
    crow_ref[0] = _lanes_to_rows(c2, LANE_F)
    for h in range(N_HEADS):
        crep_ref[0, :, h * HEAD_DIM:(h + 1) * HEAD_DIM] = jnp.broadcast_to(
            c2[:, LANE_F + h:LANE_F + h + 1], (t, HEAD_DIM))


def _cumsum(arr, n_batch, length, tl, col_block):
    return pl.pallas_call(
        _cumsum_kernel,
        grid=(n_batch, length // tl),
        in_specs=[pl.BlockSpec((1, tl, LANES), lambda b, j: (b, j, col_block))],
        out_specs=[pl.BlockSpec((1, tl, WIDTH), lambda b, j: (b, j, 0)),
                   pl.BlockSpec((1, SUBLANES, tl), lambda b, j: (b, 0, j))],
        out_shape=[jax.ShapeDtypeStruct((n_batch, length, WIDTH), F32),
                   jax.ShapeDtypeStruct((n_batch, SUBLANES, length), F32)],
        scratch_shapes=[pltpu.VMEM((SUBLANES, LANES), F32)],
        compiler_params=_cparams(("parallel", "arbitrary")),
        name="logf_cumsum",
    )(arr)


def _fox_kernel(qi_ref, kj_ref, last_ref, q_ref, k_ref, v_ref, cq_ref, ck_ref, o_ref,
                m_ref, l_ref, acc_ref, *, tq, tk, past):
    s_idx = pl.program_id(1)
    qi = qi_ref[s_idx]
    kj = kj_ref[s_idx]

    @pl.when(kj == 0)
    def _():
        m_ref[...] = jnp.full_like(m_ref, NEG_INF)
        l_ref[...] = jnp.zeros_like(l_ref)
        acc_ref[...] = jnp.zeros_like(acc_ref)

    def update(masked):
        if masked:
            q_pos = past + qi * tq + lax.broadcasted_iota(I32, (tq, tk), 0)
            k_pos = kj * tk + lax.broadcasted_iota(I32, (tq, tk), 1)
            visible = k_pos <= q_pos
        for h in range(N_HEADS):
            cols = slice(h * HEAD_DIM, (h + 1) * HEAD_DIM)
            q = q_ref[0, :, cols] * (HEAD_DIM ** -0.5 * LOG2E)
            t = _dot_nt(q, k_ref[0, :, cols]) - ck_ref[0, h:h + 1, :]
            if masked:
                t = jnp.where(visible, t, NEG_INF)
            cq = cq_ref[0, :, cols]
            m_prev = m_ref[h]
            m_new = jnp.maximum(m_prev, jnp.max(t, axis=-1, keepdims=True) + cq)
            p = jnp.exp2(t - _spread_lanes(m_new - cq, tk))
            alpha = jnp.exp2(m_prev - m_new)
            l_ref[h] = alpha * l_ref[h] + jnp.sum(p, axis=-1, keepdims=True)
            acc_ref[:, cols] = alpha * acc_ref[:, cols] + _dot(p, v_ref[0, :, cols])
            m_ref[h] = m_new

    crosses_diagonal = kj * tk + (tk - 1) > past + qi * tq
    pl.when(crosses_diagonal)(functools.partial(update, True))
    pl.when(jnp.logical_not(crosses_diagonal))(functools.partial(update, False))

    @pl.when(last_ref[s_idx] == 1)
    def _():
        for h in range(N_HEADS):
            cols = slice(h * HEAD_DIM, (h + 1) * HEAD_DIM)
            o_ref[0, :, cols] = acc_ref[:, cols] / l_ref[h]


def _fox_schedule(n_q, tq, tk, past):
    qi, kj, last = [], [], []
    for i in range(n_q):
        hi = (past + (i + 1) * tq - 1) // tk
        for j in range(hi + 1):
            qi.append(i)
            kj.append(j)
            last.append(1 if j == hi else 0)
    return (jnp.asarray(np.array(qi, np.int32)), jnp.asarray(np.array(kj, np.int32)),
            jnp.asarray(np.array(last, np.int32)))


def _fox(q_arr, q_map, k_arr, k_map, v_arr, v_map, cq_arr, cq_map, ck_arr, ck_map,
         n_batch, n_q, tq, tk, past):
    qi, kj, last = _fox_schedule(n_q, tq, tk, past)
    n_steps = int(qi.shape[0])
    spec = lambda shape, fn, tab: pl.BlockSpec(shape, lambda b, s, qi_r, kj_r, la_r: fn(b, (qi_r if tab == 'q' else kj_r)[s]))
    return pl.pallas_call(
        functools.partial(_fox_kernel, tq=tq, tk=tk, past=past),
        grid_spec=pltpu.PrefetchScalarGridSpec(
            num_scalar_prefetch=3,
            grid=(n_batch, n_steps),
            in_specs=[spec((1, tq, WIDTH), q_map, 'q'),
                      spec((1, tk, WIDTH), k_map, 'k'),
                      spec((1, tk, WIDTH), v_map, 'k'),
                      spec((1, tq, WIDTH), cq_map, 'q'),
                      spec((1, SUBLANES, tk), ck_map, 'k')],
            out_specs=spec((1, tq, WIDTH), lambda b, i: (b, i, 0), 'q'),
            scratch_shapes=[pltpu.VMEM((N_HEADS, tq, HEAD_DIM), F32),
                            pltpu.VMEM((N_HEADS, tq, HEAD_DIM), F32),
                            pltpu.VMEM((tq, WIDTH), F32)]),
        out_shape=jax.ShapeDtypeStruct((n_batch, n_q * tq, WIDTH), F32),
        compiler_params=_cparams(("parallel", "arbitrary")),
        name="fox_attention",
    )(qi, kj, last, q_arr, k_arr, v_arr, cq_arr, ck_arr)


def _gdn_kernel(first_ref, last_ref, seq_ref,
                pre_ref, z_ref, sm_ref, cpast_ref, convw_ref, s0_ref, alog_ref, dtb_ref, ng_ref,
                o_ref, sout_ref, stage_ref, s_ref, *, rows):
    step = pl.program_id(0)
    halo = SUBLANES
    n_chunks = rows // CHUNK

    @pl.when(first_ref[step] == 1)
    def _():
        stage_ref[0:halo, :] = cpast_ref[0]
        s_ref[...] = s0_ref[0]

    stage_ref[halo:halo + rows, :] = pre_ref[0]
    conv = stage_ref[halo:halo + rows, :] * convw_ref[CONV_WIDTH - 1:CONV_WIDTH, :]
    for j in range(1, CONV_WIDTH):
        conv = conv + (stage_ref[halo - j:halo - j + rows, :]
                       * convw_ref[CONV_WIDTH - 1 - j:CONV_WIDTH - j, :])
    stage_ref[0:halo, :] = stage_ref[rows:rows + halo, :]
    act = _silu(conv)

    small = sm_ref[0]
    beta_all = _sigmoid(small)
    g_all = -jnp.exp(alog_ref[...]) * _softplus(small + dtb_ref[...])
    r = lax.broadcasted_iota(I32, (rows, rows), 0)
    c = lax.broadcasted_iota(I32, (rows, rows), 1)
    same_chunk = (r // CHUNK) == (c // CHUNK)
    incl = same_chunk & (c <= r)
    strict = same_chunk & (c < r)
    eye = (c == r).astype(F32)
    gc_all = jnp.dot(incl.astype(F32), g_all, precision=HIGHEST, preferred_element_type=F32)
    gc_rows = _lanes_to_rows(gc_all, LANE_A)

    for h in range(N_HEADS):
        cols = slice(h * HEAD_DIM, (h + 1) * HEAD_DIM)
        q = act[:, h * HEAD_DIM:(h + 1) * HEAD_DIM]
        k = act[:, WIDTH + h * HEAD_DIM:WIDTH + (h + 1) * HEAD_DIM]
        v = act[:, 2 * WIDTH + h * HEAD_DIM:2 * WIDTH + (h + 1) * HEAD_DIM]
        q = q * lax.rsqrt(jnp.sum(q * q, axis=-1, keepdims=True) + NORM_EPS) * (HEAD_DIM ** -0.5)
        k = k * lax.rsqrt(jnp.sum(k * k, axis=-1, keepdims=True) + NORM_EPS)
        beta = beta_all[:, LANE_B + h:LANE_B + h + 1]
        gc = gc_all[:, LANE_A + h:LANE_A + h + 1]
        diff = gc - gc_rows[h:h + 1, :]
        decay = jnp.where(incl, jnp.exp(jnp.where(incl, diff, 0.0)), 0.0)
        kb = k * beta
        vb = v * beta
        low = jnp.where(strict, _dot_nt(kb, k) * decay, 0.0)
        inv = eye - low
        pw = _dot(low, low)
        n_sq = CHUNK.bit_length() - 2
        for it in range(n_sq):
            inv = inv + _dot(inv, pw)
            if it + 1 < n_sq:
                pw = _dot(pw, pw)
        egc = jnp.exp(gc)
        uw = _dot(inv, jnp.concatenate([vb, kb * egc], axis=1))
        intra = jnp.where(incl, _dot_nt(q, k) * decay, 0.0)
        qd = q * egc
        g_last = [gc[(g + 1) * CHUNK - 1:(g + 1) * CHUNK, :] for g in range(n_chunks)]
        kd = k * jnp.exp(jnp.concatenate([jnp.broadcast_to(gl, (CHUNK, 1)) for gl in g_last], axis=0)
                         - gc)
        state = s_ref[h]
        v_new = []
        for g in range(n_chunks):
            rs = slice(g * CHUNK, (g + 1) * CHUNK)
            v_new.append(uw[rs, :HEAD_DIM] - _dot(uw[rs, HEAD_DIM:], state))
            v_rows = jnp.concatenate(
                v_new + [jnp.zeros((rows - (g + 1) * CHUNK, HEAD_DIM), F32)] * (g + 1 < n_chunks), axis=0)
            o = _dot(qd[rs, :], state) + _dot(intra[rs, :], v_rows)
            state = state * jnp.exp(g_last[g]) + _dot_tn(kd[rs, :], v_new[g])
            o = (o * lax.rsqrt(jnp.mean(o * o, axis=-1, keepdims=True) + NORM_EPS)
                 * ng_ref[...] * _silu(z_ref[0, rs, cols]))
            o_ref[0, rs, cols] = o
        s_ref[h] = state

    @pl.when(last_ref[step] == 1)
    def _():
        sout_ref[0] = s_ref[...]


def _gdn(u_view, blk0, first, last, seq, conv_past, conv_w, s0, alog_row, dtb_row, ng_row):
    rows = u_view.shape[1]
    n_steps = int(first.shape[0])
    n_seq = s0.shape[0]
    gw = 3 * WIDTH
    return pl.pallas_call(
        functools.partial(_gdn_kernel, rows=rows),
        grid_spec=pltpu.PrefetchScalarGridSpec(
            num_scalar_prefetch=3,
            grid=(n_steps,),
            in_specs=[pl.BlockSpec((1, rows, gw), lambda s, f, l, q: (blk0 + s, 0, COL_G // gw)),
                      pl.BlockSpec((1, rows, WIDTH), lambda s, f, l, q: (blk0 + s, 0, COL_Z // WIDTH)),
                      pl.BlockSpec((1, rows, LANES), lambda s, f, l, q: (blk0 + s, 0, COL_S // LANES)),
                      pl.BlockSpec((1, SUBLANES, gw), lambda s, f, l, q: (q[s], 0, 0)),
                      pl.BlockSpec((SUBLANES, gw), lambda s, f, l, q: (0, 0)),
                      pl.BlockSpec((1, N_HEADS, HEAD_DIM, HEAD_DIM), lambda s, f, l, q: (q[s], 0, 0, 0)),
                      pl.BlockSpec((1, LANES), lambda s, f, l, q: (0, 0)),
                      pl.BlockSpec((1, LANES), lambda s, f, l, q: (0, 0)),
                      pl.BlockSpec((1, LANES), lambda s, f, l, q: (0, 0))],
            out_specs=[pl.BlockSpec((1, rows, WIDTH), lambda s, f, l, q: (s, 0, 0)),
                       pl.BlockSpec((1, N_HEADS, HEAD_DIM, HEAD_DIM), lambda s, f, l, q: (q[s], 0, 0, 0))],
            scratch_shapes=[pltpu.VMEM((rows + SUBLANES, gw), F32),
                            pltpu.VMEM((N_HEADS, HEAD_DIM, HEAD_DIM), F32)]),
        out_shape=[jax.ShapeDtypeStruct((n_steps, rows, WIDTH), F32),
                   jax.ShapeDtypeStruct((n_seq, N_HEADS, HEAD_DIM, HEAD_DIM), F32)],
        compiler_params=_cparams(("arbitrary",)),
        name="gated_deltanet",
    )(first, last, seq, u_view, u_view, u_view, conv_past, conv_w, s0, alog_row, dtb_row, ng_row)


def _outproj_ln_kernel(x_ref, of_ref, og_ref, w_ref, g_ref, b_ref, h_ref, ht_ref):
    mix = _dot(of_ref[...], w_ref[0:WIDTH, :]) + _dot(og_ref[...], w_ref[WIDTH:2 * WIDTH, :])
    h = _layer_norm(DN_ALPHA * x_ref[...] + mix, g_ref[...], b_ref[...])
    h_ref[...] = h
    _store_token_tiles(ht_ref, (), h)


def _outproj_ln(x, o_fox, o_gdn, w_bf16, g_row, b_row):
    n, d = x.shape
    row = lambda i: (i, 0)
    fixed = lambda i: (0, 0)
    return pl.pallas_call(
        _outproj_ln_kernel,
        grid=(n // TM_TOK,),
        in_specs=[pl.BlockSpec((TM_TOK, d), row), pl.BlockSpec((TM_TOK, WIDTH), row),
                  pl.BlockSpec((TM_TOK, WIDTH), row), pl.BlockSpec((2 * WIDTH, d), fixed),
                  pl.BlockSpec((1, d), fixed), pl.BlockSpec((1, d), fixed)],
        out_specs=[pl.BlockSpec((TM_TOK, d), row), pl.BlockSpec((TM_TOK * SUBLANES, LANES), row)],
        out_shape=[jax.ShapeDtypeStruct((n, d), F32), jax.ShapeDtypeStruct((n * SUBLANES, LANES), F32)],
        compiler_params=_cparams(("parallel",)),
        name="out_proj_ln",
    )(x, o_fox, o_gdn, w_bf16, g_row, b_row)


def _pool_ln_kernel(x_ref, halo_ref, pw_ref, ps_ref, w_ref, g_ref, b_ref, h_ref, ht_ref, stage_ref,
                    *, tm, pos0, zero_first_halo):
    i = pl.program_id(0)
    stage_ref[0:POOL_HALO, :] = halo_ref[0]
    if zero_first_halo:
        @pl.when(i == 0)
        def _():
            stage_ref[0:POOL_HALO, :] = jnp.zeros((POOL_HALO, stage_ref.shape[1]), F32)
    x = x_ref[...]
    stage_ref[POOL_HALO:POOL_HALO + tm, :] = x
    gdim = x.shape[1] // len(POOL_WINDOWS)
    pos = pos0 + lax.broadcasted_iota(I32, (tm, 1), 0)
    if zero_first_halo:
        pos = pos + i * tm
    parts = []
    for gi, win in enumerate(POOL_WINDOWS):
        cols = slice(gi * gdim, (gi + 1) * gdim)
        s = stage_ref[POOL_HALO:POOL_HALO + tm, cols]
        for j in range(1, win):
            s = s + stage_ref[POOL_HALO - j:POOL_HALO - j + tm, cols]
        cnt = jnp.minimum(pos + 1, win).astype(F32)
        zg = s / cnt - x[:, cols]
        parts.append(_dot(zg, pw_ref[gi]))
    zg = jnp.concatenate(parts, axis=-1) * ps_ref[...]
    mix = _dot(zg, w_ref[...])
    h = _layer_norm(DN_ALPHA * x + mix, g_ref[...], b_ref[...])
    h_ref[...] = h
    _store_token_tiles(ht_ref, (), h)


def _pool_ln(x, x_map, halo_arr, halo_map, n_tiles, tm, pos0, zero_first_halo,
             pw_bf16, ps_row, w_bf16, g_row, b_row):
    d = x.shape[1]
    gdim = d // len(POOL_WINDOWS)
    fixed = lambda i: (0, 0)
    return pl.pallas_call(
        functools.partial(_pool_ln_kernel, tm=tm, pos0=pos0, zero_first_halo=zero_first_halo),
        grid=(n_tiles,),
        in_specs=[pl.BlockSpec((tm, d), x_map),
                  pl.BlockSpec((1, POOL_HALO, d), halo_map),
                  pl.BlockSpec((len(POOL_WINDOWS), gdim, gdim), lambda i: (0, 0, 0)),
                  pl.BlockSpec((1, d), fixed), pl.BlockSpec((d, d), fixed),
                  pl.BlockSpec((1, d), fixed), pl.BlockSpec((1, d), fixed)],
        out_specs=[pl.BlockSpec((tm, d), lambda i: (i, 0)),
                   pl.BlockSpec((tm * SUBLANES, LANES), lambda i: (i, 0))],
        out_shape=[jax.ShapeDtypeStruct((n_tiles * tm, d), F32),
                   jax.ShapeDtypeStruct((n_tiles * tm * SUBLANES, LANES), F32)],
        scratch_shapes=[pltpu.VMEM((POOL_HALO + tm, d), F32)],
        compiler_params=_cparams(("arbitrary",)),
        name="pool_mixer_ln",
    )(x, halo_arr, pw_bf16, ps_row, w_bf16, g_row, b_row)


def _router_kernel(h_ref, wr_ref, br_ref, idx_ref, gate_ref, rank_ref, cnt_ref, carry_ref):
    @pl.when(pl.program_id(0) == 0)
    def _():
        carry_ref[...] = jnp.zeros_like(carry_ref)

    tm = h_ref.shape[0]
    lane = lax.broadcasted_iota(I32, (tm, LANES), 1).astype(F32)
    logits = jnp.dot(h_ref[...], wr_ref[...], precision=HIGHEST, preferred_element_type=F32)
    work = jnp.where(lane < N_EXPERTS, logits + br_ref[...], -jnp.inf)
    vals, ids = [], []
    for _ in range(TOP_K):
        m = jnp.max(work, axis=-1, keepdims=True)
        ik = jnp.min(jnp.where(work == m, lane, float(LANES)), axis=-1, keepdims=True)
        vals.append(m)
        ids.append(ik)
        work = jnp.where(lane == ik, -jnp.inf, work)
    exps = [jnp.exp(v - vals[0]) for v in vals]
    denom = exps[0]
    for e in exps[1:]:
        denom = denom + e
    multihot = jnp.zeros((tm, LANES), F32)
    idx_out = jnp.zeros((tm, LANES), F32)
    gate_out = jnp.zeros((tm, LANES), F32)
    for k in range(TOP_K):
        multihot = multihot + (lane == ids[k]).astype(F32)
        idx_out = jnp.where(lane == k, ids[k], idx_out)
        gate_out = jnp.where(lane == k, exps[k] / denom, gate_out)
    r = lax.broadcasted_iota(I32, (tm, tm), 0)
    c = lax.broadcasted_iota(I32, (tm, tm), 1)
    before = _dot((c < r).astype(F32), multihot) + carry_ref[0:1, :]
    rank_out = jnp.zeros((tm, LANES), F32)
    for k in range(TOP_K):
        rk = jnp.sum(jnp.where(lane == ids[k], before, 0.0), axis=-1, keepdims=True)
        rank_out = jnp.where(lane == k, rk, rank_out)
    idx_ref[...] = idx_out.astype(I32)
    gate_ref[...] = gate_out
    rank_ref[...] = rank_out.astype(I32)
    total = carry_ref[0:1, :] + jnp.sum(multihot, axis=0, keepdims=True)
    carry_ref[...] = jnp.broadcast_to(total, carry_ref.shape)
    cnt_ref[...] = jnp.broadcast_to(total, cnt_ref.shape).astype(I32)


def _router(h, wr_pad, br_row):
    n, d = h.shape
    row = lambda i: (i, 0)
    fixed = lambda i: (0, 0)
    return pl.pallas_call(
        _router_kernel,
        grid=(n // TM_TOK,),
        in_specs=[pl.BlockSpec((TM_TOK, d), row), pl.BlockSpec((d, LANES), fixed),
                  pl.BlockSpec((1, LANES), fixed)],
        out_specs=[pl.BlockSpec((TM_TOK, LANES), row), pl.BlockSpec((TM_TOK, LANES), row),
                   pl.BlockSpec((TM_TOK, LANES), row), pl.BlockSpec((SUBLANES, LANES), fixed)],
        out_shape=[jax.ShapeDtypeStruct((n, LANES), I32), jax.ShapeDtypeStruct((n, LANES), F32),
                   jax.ShapeDtypeStruct((n, LANES), I32), jax.ShapeDtypeStruct((SUBLANES, LANES), I32)],
        scratch_shapes=[pltpu.VMEM((SUBLANES, LANES), F32)],
        compiler_params=_cparams(("arbitrary",)),
        name="moe_router",
    )(h, wr_pad, br_row)


def _tile_copy(src_ref, src_row, dst_ref, dst_row, sem):
    rows = lambda r: pl.ds(pl.multiple_of(r, SUBLANES), SUBLANES)
    return pltpu.make_async_copy(src_ref.at[rows(src_row)], dst_ref.at[rows(dst_row)], sem)


def _all_rows(buf_ref, sem):
    return pltpu.make_async_copy(buf_ref, buf_ref, sem)


def _expert_kernel(be_ref, tok_cur_ref, tok_next_ref, dst_prev_ref, dst_cur_ref,
                   h_ref, wup_ref, bup_ref, wdn_ref, bdn_ref, y_ref,
                   xbuf_ref, ybuf_ref, wup_bf_ref, wdn_bf_ref, gsem, ssem):
    b = pl.program_id(0)
    n_b = pl.num_programs(0)

    @pl.when(b == 0)
    def _():
        ybuf_ref[1] = jnp.zeros(ybuf_ref.shape[1:], F32)

        def first(r, c):
            _tile_copy(h_ref, tok_cur_ref[0, 0, r], xbuf_ref.at[0], r * SUBLANES, gsem.at[0]).start()
            return c
        lax.fori_loop(0, MOE_BLOCK, first, 0)

    @pl.when((b == 0) | (be_ref[b] != be_ref[jnp.maximum(b - 1, 0)]))
    def _():
        wup_bf_ref[...] = wup_ref[0, 0].astype(BF16)
        wdn_bf_ref[...] = wdn_ref[0, 0].astype(BF16)

    def step(p):
        q = 1 - p
        _all_rows(xbuf_ref.at[p], gsem.at[p]).wait()
        d_exp = wdn_ref.shape[2]
        x = _load_token_tiles(xbuf_ref, (p,), MOE_BLOCK)
        hu = jnp.dot(x.astype(BF16), wup_bf_ref[...], preferred_element_type=F32) + bup_ref[0, 0]
        for r in range(MOE_BLOCK):
            _tile_copy(h_ref, tok_next_ref[0, 0, r], xbuf_ref.at[q], r * SUBLANES, gsem.at[q]).start()
            _tile_copy(ybuf_ref.at[q], r * SUBLANES, y_ref, dst_prev_ref[0, 0, r], ssem.at[q]).start()
        glu = jnp.minimum(hu[:, :d_exp], SWIGLU_LIMIT)
        lin = jnp.clip(hu[:, d_exp:], -SWIGLU_LIMIT, SWIGLU_LIMIT)
        a = glu * _sigmoid(SWIGLU_ALPHA * glu) * (lin + 1.0)
        y = jnp.dot(a.astype(BF16), wdn_bf_ref[...], preferred_element_type=F32) + bdn_ref[0, 0]
        _store_token_tiles(ybuf_ref, (p,), y)
        _all_rows(ybuf_ref.at[q], ssem.at[q]).wait()

    pl.when(b % 2 == 0)(functools.partial(step, 0))
    pl.when(b % 2 == 1)(functools.partial(step, 1))

    @pl.when(b == n_b - 1)
    def _():
        for p in range(2):
            @pl.when(b % 2 == p)
            def _():
                def last(r, c):
                    _tile_copy(ybuf_ref.at[p], r * SUBLANES, y_ref, dst_cur_ref[0, 0, r], ssem.at[p]).start()
                    return c
                lax.fori_loop(0, MOE_BLOCK, last, 0)
                _all_rows(ybuf_ref.at[p], ssem.at[p]).wait()
                _all_rows(xbuf_ref.at[1 - p], gsem.at[1 - p]).wait()


def _experts(ht, block_e, tok_of, dst_of, layer, w_up, b_up, w_dn, b_dn):
    d = SUBLANES * LANES
    n_blocks = tok_of.shape[0]
    d_up = w_up.shape[3]
    d_exp = w_dn.shape[2]
    wsel = lambda b, be: (layer, be[b], 0, 0)
    table = lambda fn: pl.BlockSpec((1, 1, MOE_BLOCK), lambda b, be: (fn(b), 0, 0), memory_space=pltpu.SMEM)
    return pl.pallas_call(
        _expert_kernel,
        grid_spec=pltpu.PrefetchScalarGridSpec(
            num_scalar_prefetch=1,
            grid=(n_blocks,),
            in_specs=[table(lambda b: b),
                      table(lambda b: jnp.minimum(b + 1, n_blocks - 1)),
                      table(lambda b: jnp.maximum(b - 1, 0)),
                      table(lambda b: b),
                      pl.BlockSpec(memory_space=pl.ANY),
                      pl.BlockSpec((1, 1, d, d_up), wsel),
                      pl.BlockSpec((1, 1, 1, d_up), wsel),
                      pl.BlockSpec((1, 1, d_exp, d), wsel),
                      pl.BlockSpec((1, 1, 1, d), wsel)],
            out_specs=pl.BlockSpec(memory_space=pl.ANY),
            scratch_shapes=[pltpu.VMEM((2, MOE_BLOCK * SUBLANES, LANES), F32),
                            pltpu.VMEM((2, MOE_BLOCK * SUBLANES, LANES), F32),
                            pltpu.VMEM((d, d_up), BF16), pltpu.VMEM((d_exp, d), BF16),
                            pltpu.SemaphoreType.DMA((2,)), pltpu.SemaphoreType.DMA((2,))]),
        out_shape=jax.ShapeDtypeStruct((n_blocks * MOE_BLOCK * SUBLANES, LANES), F32),
        compiler_params=_cparams(("arbitrary",)),
        name="moe_experts",
    )(block_e, tok_of, tok_of, dst_of, dst_of, ht, w_up, b_up, w_dn, b_dn)


def _combine_kernel(h_ref, y0_ref, y1_ref, y2_ref, y3_ref, gate_ref, p_ref, wpg_ref, wpp_ref,
                    g_ref, b_ref, out_ref):
    gate = gate_ref[...]
    tm = h_ref.shape[0]
    moe = _load_token_tiles(y0_ref, (), tm) * gate[:, 0:1]
    for k, y_ref in enumerate((y1_ref, y2_ref, y3_ref), start=1):
        moe = moe + _load_token_tiles(y_ref, (), tm) * gate[:, k:k + 1]
    h2 = _layer_norm(DN_ALPHA * h_ref[...] + moe, g_ref[...], b_ref[...])
    out_ref[...] = h2 + _sigmoid(_dot(h2, wpg_ref[...])) * _dot(p_ref[...], wpp_ref[...])


def _combine(h, y, gate, p, wpg_bf16, wpp_bf16, g_row, b_row):
    n, d = h.shape
    n_tiles = n // TM_TOK
    row = lambda i: (i, 0)
    fixed = lambda i: (0, 0)
    choice = lambda k: pl.BlockSpec((TM_TOK * SUBLANES, LANES), lambda i: (k * n_tiles + i, 0))
    return pl.pallas_call(
        _combine_kernel,
        grid=(n_tiles,),
        in_specs=[pl.BlockSpec((TM_TOK, d), row)] + [choice(k) for k in range(TOP_K)]
                 + [pl.BlockSpec((TM_TOK, LANES), row), pl.BlockSpec((TM_TOK, p.shape[1]), row),
                    pl.BlockSpec((d, d), fixed), pl.BlockSpec((p.shape[1], d), fixed),
                    pl.BlockSpec((1, d), fixed), pl.BlockSpec((1, d), fixed)],
        out_specs=pl.BlockSpec((TM_TOK, d), row),
        out_shape=jax.ShapeDtypeStruct((n, d), F32),
        compiler_params=_cparams(("parallel",)),
        name="moe_combine_ln_embed",
    )(h, y, y, y, y, gate, p, wpg_bf16, wpp_bf16, g_row, b_row)


def _layer_tail(h, ht, p, layer, g2, b2, w_r, b_r, w_up, b_up, w_dn, b_dn, w_pg, w_pp):
    n, d = h.shape
    wr_pad = jnp.pad(w_r, ((0, 0), (0, LANES - N_EXPERTS)))
    br_row = jnp.pad(b_r, (0, LANES - N_EXPERTS))[None]
    idx, gate, rank, cnt = _router(h, wr_pad, br_row)
    counts = cnt[0, :N_EXPERTS]
    padded = (counts + MOE_BLOCK - 1) // MOE_BLOCK * MOE_BLOCK
    pend = jnp.cumsum(padded).astype(I32)
    pstart = pend - padded
    slot = (pstart[idx[:, :TOP_K]] + rank[:, :TOP_K]).reshape(-1).astype(I32)
    n_asg = n * TOP_K
    n_blocks = n_asg // MOE_BLOCK + N_EXPERTS
    assert n_asg % MOE_BLOCK == 0 and n % TM_TOK == 0
    n_slots = n_blocks * MOE_BLOCK
    n_used = pend[-1] // MOE_BLOCK
    blk = jnp.minimum(jnp.arange(n_blocks, dtype=I32), n_used - 1) * MOE_BLOCK
    block_e = jnp.minimum(jnp.sum((pend[None, :] <= blk[:, None]).astype(I32), axis=1), N_EXPERTS - 1)
    asg = jnp.full((n_slots,), -1, I32).at[slot].set(jnp.arange(n_asg, dtype=I32))
    is_pad = asg < 0
    tok_of = jnp.where(is_pad, 0, asg // TOP_K) * SUBLANES
    dst_of = jnp.where(is_pad, n_asg - 1 + jnp.cumsum(is_pad.astype(I32)),
                       (asg % TOP_K) * n + asg // TOP_K) * SUBLANES
    y = _experts(ht, block_e, tok_of.reshape(n_blocks, 1, MOE_BLOCK), dst_of.reshape(n_blocks, 1, MOE_BLOCK),
                 layer, w_up, b_up[:, :, None, :], w_dn, b_dn[:, :, None, :])
    return _combine(h, y, gate, p, w_pg.astype(BF16), w_pp.astype(BF16), g2[None], b2[None])


def _lane_row(v, lane0):
    return jnp.zeros((1, LANES), F32).at[0, lane0:lane0 + v.shape[0]].set(v.astype(F32))


def kernel(x_prompt, x_sample, cache_fox_k, cache_fox_v, cache_fox_logf, state_gdn, state_gdn_conv,
           cache_pool, p_prompt, p_sample, w_in_ab, b_fgate, gdn_a_log, gdn_dt_bias, gdn_conv_w,
           gdn_norm_g, w_out_ab, pool_w, pool_scale, w_out_pool, ln1_g, ln1_b, ln2_g, ln2_b,
           w_router, b_router, w_expert_up, b_expert_up, w_expert_down, b_expert_down,
           w_ple_gate, w_ple_proj):
    n_pb, seq, d = x_prompt.shape
    n_sb, dseq, _ = x_sample.shape
    past = cache_fox_k.shape[2]
    assert n_pb == 1 and dseq == CHUNK and past % dseq == 0 and seq % TQ == 0
    assert d == SUBLANES * LANES
    n_p = n_pb * seq
    n_s = n_sb * dseq
    n = n_p + n_s
    x = jnp.concatenate([x_prompt.reshape(n_p, d), x_sample.reshape(n_s, d)], axis=0)

    def tail(h, ht, i):
        p = jnp.concatenate([p_prompt[i].reshape(n_p, -1), p_sample[i].reshape(n_s, -1)], axis=0)
        return _layer_tail(h, ht, p, i, ln2_g[i], ln2_b[i], w_router[i], b_router[i], w_expert_up,
                           b_expert_up, w_expert_down, b_expert_down, w_ple_gate[i], w_ple_proj[i])

    w_in = w_in_ab[0]
    n_small = 3 * N_HEADS
    ff0 = 3 * WIDTH
    gq0 = ff0 + N_HEADS
    ga0 = gq0 + 4 * WIDTH
    w_small = jnp.concatenate([w_in[:, ff0:gq0], w_in[:, ga0:ga0 + 2 * N_HEADS],
                               jnp.zeros((d, LANES - n_small), F32)], axis=1)
    w_all = jnp.concatenate([w_in[:, :ff0], w_in[:, gq0:ga0], w_small], axis=1).astype(BF16)
    u = _proj(x, w_all, _lane_row(b_fgate[0], LANE_F))

    u3 = u[None]
    u64 = u.reshape(n // CHUNK, CHUNK, U_COLS)
    cq_p, ck_p = _cumsum(u3, 1, n_p, TK, COL_S // LANES)
    lf_s = jnp.concatenate(
        [jnp.pad(cache_fox_logf[0].astype(F32), ((0, 0), (0, 0), (0, LANES - N_HEADS))),
         u[n_p:, COL_S:].reshape(n_sb, dseq, LANES)], axis=1)
    cq_s, ck_s = _cumsum(lf_s, n_sb, past + dseq, past + dseq, 0)

    of_p = _fox(u3, lambda b, i: (0, i, COL_Q // WIDTH), u3, lambda b, j: (0, j, COL_K // WIDTH),
                u3, lambda b, j: (0, j, COL_V // WIDTH), cq_p, lambda b, i: (0, i, 0),
                ck_p, lambda b, j: (0, 0, j), 1, n_p // TQ, TQ, TK, 0)
    k_new = u[n_p:, COL_K:COL_K + WIDTH].reshape(n_sb, dseq, WIDTH)
    v_new = u[n_p:, COL_V:COL_V + WIDTH].reshape(n_sb, dseq, WIDTH)
    k_all = jnp.concatenate([cache_fox_k[0].reshape(n_sb, past, WIDTH), k_new], axis=1)
    v_all = jnp.concatenate([cache_fox_v[0].reshape(n_sb, past, WIDTH), v_new], axis=1)
    of_s = _fox(u64, lambda b, i: (n_p // CHUNK + b, 0, COL_Q // WIDTH), k_all, lambda b, j: (b, 0, 0),
                v_all, lambda b, j: (b, 0, 0), cq_s, lambda b, i: (b, past // dseq, 0),
                ck_s, lambda b, j: (b, 0, 0), n_sb, 1, dseq, past + dseq, past)
    o_fox = jnp.concatenate([of_p.reshape(n_p, WIDTH), of_s.reshape(n_s, WIDTH)], axis=0)

    gw = 3 * WIDTH
    conv_w = jnp.pad(gdn_conv_w[0], ((0, SUBLANES - CONV_WIDTH), (0, 0)))
    gdn_args = (conv_w, _lane_row(gdn_a_log[0], LANE_A), _lane_row(gdn_dt_bias[0], LANE_A),
                gdn_norm_g[0][None])
    n_pstep = n_p // GDN_ROWS
    ends = lambda k: (jnp.asarray((np.arange(k) == 0).astype(np.int32)),
                      jnp.asarray((np.arange(k) == k - 1).astype(np.int32)))
    og_p, st_p = _gdn(u.reshape(n // GDN_ROWS, GDN_ROWS, U_COLS), 0, *ends(n_pstep),
                      jnp.zeros((n_pstep,), I32), jnp.zeros((1, SUBLANES, gw), F32), gdn_args[0],
                      jnp.zeros((1, N_HEADS, HEAD_DIM, HEAD_DIM), F32), *gdn_args[1:])
    conv_past = jnp.pad(state_gdn_conv[0].astype(F32), ((0, 0), (SUBLANES - (CONV_WIDTH - 1), 0), (0, 0)))
    ones = jnp.ones((n_sb,), I32)
    og_s, st_s = _gdn(u64, n_p // CHUNK, ones, ones, jnp.arange(n_sb, dtype=I32), conv_past, gdn_args[0],
                      state_gdn[0].astype(F32), *gdn_args[1:])
    o_gdn = jnp.concatenate([og_p.reshape(n_p, WIDTH), og_s.reshape(n_s, WIDTH)], axis=0)
    h, ht = _outproj_ln(x, o_fox, o_gdn, w_out_ab[0].astype(BF16), ln1_g[0][None], ln1_b[0][None])
    x1 = tail(h, ht, 0)

    pool_args = (pool_w[0].astype(BF16), pool_scale[0][None], w_out_pool[0].astype(BF16),
                 ln1_g[1][None], ln1_b[1][None])
    ratio = TM_TOK // POOL_HALO
    x1_halo = x1.reshape(n // POOL_HALO, POOL_HALO, d)
    h_p, ht_p = _pool_ln(x1, lambda i: (i, 0), x1_halo, lambda i: (jnp.maximum(i * ratio - 1, 0), 0, 0),
                         n_p // TM_TOK, TM_TOK, 0, True, *pool_args)
    cache16 = jnp.pad(cache_pool[0].astype(F32), ((0, 0), (POOL_HALO - POOL_STATE, 0), (0, 0)))
    h_s, ht_s = _pool_ln(x1, lambda i: (n_p // dseq + i, 0), cache16, lambda i: (i, 0, 0),
                         n_sb, dseq, past, False, *pool_args)
    x2 = tail(jnp.concatenate([h_p, h_s], axis=0), jnp.concatenate([ht_p, ht_s], axis=0), 1)

    up = u[:n_p]
    us = u[n_p:]
    return (x2[:n_p].reshape(n_pb, seq, d), x2[n_p:].reshape(n_sb, dseq, d),
            up[:, COL_K:COL_K + WIDTH].reshape(1, n_pb, seq, N_HEADS, HEAD_DIM),
            up[:, COL_V:COL_V + WIDTH].reshape(1, n_pb, seq, N_HEADS, HEAD_DIM),
            up[:, COL_S:COL_S + N_HEADS].reshape(1, n_pb, seq, N_HEADS),
            st_p.reshape(1, n_pb, N_HEADS, HEAD_DIM, HEAD_DIM),
            up[seq - (CONV_WIDTH - 1):, COL_G:COL_G + gw].reshape(1, n_pb, CONV_WIDTH - 1, gw),
            x1[n_p - POOL_STATE:n_p].reshape(1, n_pb, POOL_STATE, d),
            us[:, COL_K:COL_K + WIDTH].reshape(1, n_sb, dseq, N_HEADS, HEAD_DIM),
            us[:, COL_V:COL_V + WIDTH].reshape(1, n_sb, dseq, N_HEADS, HEAD_DIM),
            us[:, COL_S:COL_S + N_HEADS].reshape(1, n_sb, dseq, N_HEADS),
            st_s.reshape(1, n_sb, N_HEADS, HEAD_DIM, HEAD_DIM),
            us[:, COL_G:COL_G + gw].reshape(n_sb, dseq, gw)[:, dseq - (CONV_WIDTH - 1):].reshape(
                1, n_sb, CONV_WIDTH - 1, gw),
            x1[n_p:].reshape(n_sb, dseq, d)[:, dseq - POOL_STATE:].reshape(1, n_sb, POOL_STATE, d))
```

```python
import functools

import numpy as np
import jax
import jax.numpy as jnp
from jax import lax
from jax.experimental import pallas as pl
from jax.experimental.pallas import tpu as pltpu

F32 = jnp.float32
BF16 = jnp.bfloat16
I32 = jnp.int32
HIGHEST = lax.Precision.HIGHEST

LANES = 128
SUBLANES = 8
VMEM_LIMIT = 56 * 1024 * 1024

HEAD_DIM = 128
N_HEADS = 4
WIDTH = N_HEADS * HEAD_DIM
CHUNK = 64
CONV_WIDTH = 4
POOL_WINDOWS = (2, 4, 8, 16)
POOL_HALO = 16
POOL_STATE = 15
N_EXPERTS = 32
TOP_K = 4
SWIGLU_LIMIT = 7.0
SWIGLU_ALPHA = 1.702
DEPTH = 2
DN_ALPHA = (2 * DEPTH) ** 0.25
LN_EPS = 1e-5
NORM_EPS = 1e-6
NEG_INF = -1e30
LOG2E = 1.4426950408889634

COL_Q, COL_K, COL_V = 0, WIDTH, 2 * WIDTH
COL_G = 3 * WIDTH
COL_Z = 6 * WIDTH
COL_S = 7 * WIDTH
U_COLS = COL_S + LANES
LANE_F, LANE_A, LANE_B = 0, N_HEADS, 2 * N_HEADS

TM_PROJ = 256
TM_TOK = 512
TM_ROW = 256
MOE_BLOCK = 256
TQ = 512
TK = 512
GDN_ROWS = 256


def _cparams(sem):
    return pltpu.CompilerParams(dimension_semantics=sem, vmem_limit_bytes=VMEM_LIMIT)


def _softplus(x):
    return jnp.maximum(x, 0.0) + jnp.log1p(jnp.exp(-jnp.abs(x)))


def _sigmoid(x):
    return 1.0 / (1.0 + jnp.exp(-x))


def _silu(x):
    return x * _sigmoid(x)


def _layer_norm(y, g, b):
    mu = jnp.mean(y, axis=-1, keepdims=True)
    yc = y - mu
    var = jnp.mean(yc * yc, axis=-1, keepdims=True)
    return yc * lax.rsqrt(var + LN_EPS) * g + b


def _dot(a, b):
    return jnp.dot(a.astype(BF16), b.astype(BF16), preferred_element_type=F32)


def _dot_nt(a, b):
    return lax.dot_general(a.astype(BF16), b.astype(BF16), (((1,), (1,)), ((), ())),
                           preferred_element_type=F32)


def _dot_tn(a, b):
    return lax.dot_general(a.astype(BF16), b.astype(BF16), (((0,), (0,)), ((), ())),
                           preferred_element_type=F32)


def _spread_lanes(x, width):
    if width % LANES == 0:
        return jnp.concatenate([x] * (width // LANES), axis=1)
    return jnp.broadcast_to(x[:, 0:1], (x.shape[0], width))


def _load_token_tiles(ref, lead, n_tok):
    return jnp.concatenate([ref[(*lead, pl.ds(j, n_tok, stride=SUBLANES), slice(None))]
                            for j in range(SUBLANES)], axis=1)


def _store_token_tiles(ref, lead, x):
    for j in range(SUBLANES):
        ref[(*lead, pl.ds(j, x.shape[0], stride=SUBLANES), slice(None))] = x[:, j * LANES:(j + 1) * LANES]


def _lanes_to_rows(x, lane0):
    r = lax.broadcasted_iota(I32, (SUBLANES, LANES), 0)
    c = lax.broadcasted_iota(I32, (SUBLANES, LANES), 1)
    sel = (c == r + lane0).astype(F32)
    return lax.dot_general(sel, x, (((1,), (1,)), ((), ())), precision=HIGHEST,
                           preferred_element_type=F32)


def _proj_kernel(x_ref, w_ref, bf_ref, u_ref):
    u = jnp.dot(x_ref[...].astype(BF16), w_ref[...], preferred_element_type=F32)
    u_ref[...] = u
    small = u[:, COL_S:]
    lane = lax.broadcasted_iota(I32, small.shape, 1)
    logf = -_softplus(-(small + bf_ref[...]))
    u_ref[:, COL_S:] = jnp.where(lane < LANE_A, logf, small)


def _proj(x, w_bf16, bf_row):
    n, d = x.shape
    m = w_bf16.shape[1]
    return pl.pallas_call(
        _proj_kernel,
        grid=(n // TM_PROJ,),
        in_specs=[pl.BlockSpec((TM_PROJ, d), lambda i: (i, 0)),
                  pl.BlockSpec((d, m), lambda i: (0, 0)),
                  pl.BlockSpec((1, LANES), lambda i: (0, 0))],
        out_specs=pl.BlockSpec((TM_PROJ, m), lambda i: (i, 0)),
        out_shape=jax.ShapeDtypeStruct((n, m), F32),
        compiler_params=_cparams(("parallel",)),
        name="in_proj",
    )(x, w_bf16, bf_row)


def _cumsum_kernel(lf_ref, crep_ref, crow_ref, carry_ref):
    @pl.when(pl.program_id(1) == 0)
    def _():
        carry_ref[...] = jnp.zeros_like(carry_ref)

    lf = lf_ref[0]
    t = lf.shape[0]
    r = lax.broadcasted_iota(I32, (t, t), 0)
    c = lax.broadcasted_iota(I32, (t, t), 1)
    tril = (c <= r).astype(F32)
    cs = jnp.dot(tril, lf, precision=HIGHEST, preferred_element_type=F32) + carry_ref[0:1, :]
    carry_ref[...] = jnp.broadcast_to(cs[t - 1:t, :], carry_ref.shape)
    c2 = cs * LOG2E
    crow_ref[0] = _lanes_to_rows(c2, LANE_F)
    for h in range(N_HEADS):
        crep_ref[0, :, h * HEAD_DIM:(h + 1) * HEAD_DIM] = jnp.broadcast_to(
            c2[:, LANE_F + h:LANE_F + h + 1], (t, HEAD_DIM))


def _cumsum(arr, n_batch, length, tl, col_block):
    return pl.pallas_call(
        _cumsum_kernel,
        grid=(n_batch, length // tl),
        in_specs=[pl.BlockSpec((1, tl, LANES), lambda b, j: (b, j, col_block))],
        out_specs=[pl.BlockSpec((1, tl, WIDTH), lambda b, j: (b, j, 0)),
                   pl.BlockSpec((1, SUBLANES, tl), lambda b, j: (b, 0, j))],
        out_shape=[jax.ShapeDtypeStruct((n_batch, length, WIDTH), F32),
                   jax.ShapeDtypeStruct((n_batch, SUBLANES, length), F32)],
        scratch_shapes=[pltpu.VMEM((SUBLANES, LANES), F32)],
        compiler_params=_cparams(("parallel", "arbitrary")),
        name="logf_cumsum",
    )(arr)


def _fox_kernel(qi_ref, kj_ref, last_ref, q_ref, k_ref, v_ref, cq_ref, ck_ref, o_ref,
                m_ref, l_ref, acc_ref, *, tq, tk, past):
    s_idx = pl.program_id(1)
    qi = qi_ref[s_idx]
    kj = kj_ref[s_idx]

    @pl.when(kj == 0)
    def _():
        m_ref[...] = jnp.full_like(m_ref, NEG_INF)
        l_ref[...] = jnp.zeros_like(l_ref)
        acc_ref[...] = jnp.zeros_like(acc_ref)

    def update(masked):
        if masked:
            q_pos = past + qi * tq + lax.broadcasted_iota(I32, (tq, tk), 0)
            k_pos = kj * tk + lax.broadcasted_iota(I32, (tq, tk), 1)
            visible = k_pos <= q_pos
        for h in range(N_HEADS):
            cols = slice(h * HEAD_DIM, (h + 1) * HEAD_DIM)
            q = q_ref[0, :, cols] * (HEAD_DIM ** -0.5 * LOG2E)
            t = _dot_nt(q, k_ref[0, :, cols]) - ck_ref[0, h:h + 1, :]
            if masked:
                t = jnp.where(visible, t, NEG_INF)
            cq = cq_ref[0, :, cols]
            m_prev = m_ref[h]
            m_new = jnp.maximum(m_prev, jnp.max(t, axis=-1, keepdims=True) + cq)
            p = jnp.exp2(t - _spread_lanes(m_new - cq, tk))
            alpha = jnp.exp2(m_prev - m_new)
            l_ref[h] = alpha * l_ref[h] + jnp.sum(p, axis=-1, keepdims=True)
            acc_ref[:, cols] = alpha * acc_ref[:, cols] + _dot(p, v_ref[0, :, cols])
            m_ref[h] = m_new

    crosses_diagonal = kj * tk + (tk - 1) > past + qi * tq
    pl.when(crosses_diagonal)(functools.partial(update, True))
    pl.when(jnp.logical_not(crosses_diagonal))(functools.partial(update, False))

    @pl.when(last_ref[s_idx] == 1)
    def _():
        for h in range(N_HEADS):
            cols = slice(h * HEAD_DIM, (h + 1) * HEAD_DIM)
            o_ref[0, :, cols] = acc_ref[:, cols] / l_ref[h]


def _fox_schedule(n_q, tq, tk, past):
    qi, kj, last = [], [], []
    for i in range(n_q):
        hi = (past + (i + 1) * tq - 1) // tk
        for j in range(hi + 1):
            qi.append(i)
            kj.append(j)
            last.append(1 if j == hi else 0)
    return (jnp.asarray(np.array(qi, np.int32)), jnp.asarray(np.array(kj, np.int32)),
            jnp.asarray(np.array(last, np.int32)))


def _fox(q_arr, q_map, k_arr, k_map, v_arr, v_map, cq_arr, cq_map, ck_arr, ck_map,
         n_batch, n_q, tq, tk, past):
    qi, kj, last = _fox_schedule(n_q, tq, tk, past)
    n_steps = int(qi.shape[0])
    spec = lambda shape, fn, tab: pl.BlockSpec(shape, lambda b, s, qi_r, kj_r, la_r: fn(b, (qi_r if tab == 'q' else kj_r)[s]))
    return pl.pallas_call(
        functools.partial(_fox_kernel, tq=tq, tk=tk, past=past),
        grid_spec=pltpu.PrefetchScalarGridSpec(
            num_scalar_prefetch=3,
            grid=(n_batch, n_steps),
            in_specs=[spec((1, tq, WIDTH), q_map, 'q'),
                      spec((1, tk, WIDTH), k_map, 'k'),
                      spec((1, tk, WIDTH), v_map, 'k'),
                      spec((1, tq, WIDTH), cq_map, 'q'),
                      spec((1, SUBLANES, tk), ck_map, 'k')],
            out_specs=spec((1, tq, WIDTH), lambda b, i: (b, i, 0), 'q'),
            scratch_shapes=[pltpu.VMEM((N_HEADS, tq, HEAD_DIM), F32),
                            pltpu.VMEM((N_HEADS, tq, HEAD_DIM), F32),
                            pltpu.VMEM((tq, WIDTH), F32)]),
        out_shape=jax.ShapeDtypeStruct((n_batch, n_q * tq, WIDTH), F32),
        compiler_params=_cparams(("parallel", "arbitrary")),
        name="fox_attention",
    )(qi, kj, last, q_arr, k_arr, v_arr, cq_arr, ck_arr)


def _gdn_kernel(first_ref, last_ref, seq_ref,
                pre_ref, z_ref, sm_ref, cpast_ref, convw_ref, s0_ref, alog_ref, dtb_ref, ng_ref,
                o_ref, sout_ref, stage_ref, s_ref, *, rows):
    step = pl.program_id(0)
    halo = SUBLANES
    n_chunks = rows // CHUNK

    @pl.when(first_ref[step] == 1)
    def _():
        stage_ref[0:halo, :] = cpast_ref[0]
        s_ref[...] = s0_ref[0]

    stage_ref[halo:halo + rows, :] = pre_ref[0]
    conv = stage_ref[halo:halo + rows, :] * convw_ref[CONV_WIDTH - 1:CONV_WIDTH, :]
    for j in range(1, CONV_WIDTH):
        conv = conv + (stage_ref[halo - j:halo - j + rows, :]
                       * convw_ref[CONV_WIDTH - 1 - j:CONV_WIDTH - j, :])
    stage_ref[0:halo, :] = stage_ref[rows:rows + halo, :]
    act = _silu(conv)

    small = sm_ref[0]
    beta_all = _sigmoid(small)
    g_all = -jnp.exp(alog_ref[...]) * _softplus(small + dtb_ref[...])
    r = lax.broadcasted_iota(I32, (rows, rows), 0)
    c = lax.broadcasted_iota(I32, (rows, rows), 1)
    same_chunk = (r // CHUNK) == (c // CHUNK)
    incl = same_chunk & (c <= r)
    strict = same_chunk & (c < r)
    eye = (c == r).astype(F32)
    gc_all = jnp.dot(incl.astype(F32), g_all, precision=HIGHEST, preferred_element_type=F32)
    gc_rows = _lanes_to_rows(gc_all, LANE_A)

    for h in range(N_HEADS):
        cols = slice(h * HEAD_DIM, (h + 1) * HEAD_DIM)
        q = act[:, h * HEAD_DIM:(h + 1) * HEAD_DIM]
        k = act[:, WIDTH + h * HEAD_DIM:WIDTH + (h + 1) * HEAD_DIM]
        v = act[:, 2 * WIDTH + h * HEAD_DIM:2 * WIDTH + (h + 1) * HEAD_DIM]
        q = q * lax.rsqrt(jnp.sum(q * q, axis=-1, keepdims=True) + NORM_EPS) * (HEAD_DIM ** -0.5)
        k = k * lax.rsqrt(jnp.sum(k * k, axis=-1, keepdims=True) + NORM_EPS)
        beta = beta_all[:, LANE_B + h:LANE_B + h + 1]
        gc = gc_all[:, LANE_A + h:LANE_A + h + 1]
        diff = gc - gc_rows[h:h + 1, :]
        decay = jnp.where(incl, jnp.exp(jnp.where(incl, diff, 0.0)), 0.0)
        kb = k * beta
        vb = v * beta
        low = jnp.where(strict, _dot_nt(kb, k) * decay, 0.0)
        inv = eye - low
        pw = _dot(low, low)
        n_sq = CHUNK.bit_length() - 2
        for it in range(n_sq):
            inv = inv + _dot(inv, pw)
            if it + 1 < n_sq:
                pw = _dot(pw, pw)
        egc = jnp.exp(gc)
        uw = _dot(inv, jnp.concatenate([vb, kb * egc], axis=1))
        intra = jnp.where(incl, _dot_nt(q, k) * decay, 0.0)
        qd = q * egc
        g_last = [gc[(g + 1) * CHUNK - 1:(g + 1) * CHUNK, :] for g in range(n_chunks)]
        kd = k * jnp.exp(jnp.concatenate([jnp.broadcast_to(gl, (CHUNK, 1)) for gl in g_last], axis=0)
                         - gc)
        state = s_ref[h]
        v_new = []
        for g in range(n_chunks):
            rs = slice(g * CHUNK, (g + 1) * CHUNK)
            v_new.append(uw[rs, :HEAD_DIM] - _dot(uw[rs, HEAD_DIM:], state))
            v_rows = jnp.concatenate(
                v_new + [jnp.zeros((rows - (g + 1) * CHUNK, HEAD_DIM), F32)] * (g + 1 < n_chunks), axis=0)
            o = _dot(qd[rs, :], state) + _dot(intra[rs, :], v_rows)
            state = state * jnp.exp(g_last[g]) + _dot_tn(kd[rs, :], v_new[g])
            o = (o * lax.rsqrt(jnp.mean(o * o, axis=-1, keepdims=True) + NORM_EPS)
                 * ng_ref[...] * _silu(z_ref[0, rs, cols]))
            o_ref[0, rs, cols] = o
        s_ref[h] = state

    @pl.when(last_ref[step] == 1)
    def _():
        sout_ref[0] = s_ref[...]


def _gdn(u_view, blk0, first, last, seq, conv_past, conv_w, s0, alog_row, dtb_row, ng_row):
    rows = u_view.shape[1]
    n_steps = int(first.shape[0])
    n_seq = s0.shape[0]
    gw = 3 * WIDTH
    return pl.pallas_call(
        functools.partial(_gdn_kernel, rows=rows),
        grid_spec=pltpu.PrefetchScalarGridSpec(
            num_scalar_prefetch=3,
            grid=(n_steps,),
            in_specs=[pl.BlockSpec((1, rows, gw), lambda s, f, l, q: (blk0 + s, 0, COL_G // gw)),
                      pl.BlockSpec((1, rows, WIDTH), lambda s, f, l, q: (blk0 + s, 0, COL_Z // WIDTH)),
                      pl.BlockSpec((1, rows, LANES), lambda s, f, l, q: (blk0 + s, 0, COL_S // LANES)),
                      pl.BlockSpec((1, SUBLANES, gw), lambda s, f, l, q: (q[s], 0, 0)),
                      pl.BlockSpec((SUBLANES, gw), lambda s, f, l, q: (0, 0)),
                      pl.BlockSpec((1, N_HEADS, HEAD_DIM, HEAD_DIM), lambda s, f, l, q: (q[s], 0, 0, 0)),
                      pl.BlockSpec((1, LANES), lambda s, f, l, q: (0, 0)),
                      pl.BlockSpec((1, LANES), lambda s, f, l, q: (0, 0)),
                      pl.BlockSpec((1, LANES), lambda s, f, l, q: (0, 0))],
            out_specs=[pl.BlockSpec((1, rows, WIDTH), lambda s, f, l, q: (s, 0, 0)),
                       pl.BlockSpec((1, N_HEADS, HEAD_DIM, HEAD_DIM), lambda s, f, l, q: (q[s], 0, 0, 0))],
            scratch_shapes=[pltpu.VMEM((rows + SUBLANES, gw), F32),
                            pltpu.VMEM((N_HEADS, HEAD_DIM, HEAD_DIM), F32)]),
        out_shape=[jax.ShapeDtypeStruct((n_steps, rows, WIDTH), F32),
                   jax.ShapeDtypeStruct((n_seq, N_HEADS, HEAD_DIM, HEAD_DIM), F32)],
        compiler_params=_cparams(("arbitrary",)),
        name="gated_deltanet",
    )(first, last, seq, u_view, u_view, u_view, conv_past, conv_w, s0, alog_row, dtb_row, ng_row)


def _outproj_ln_kernel(x_ref, of_ref, og_ref, w_ref, g_ref, b_ref, h_ref, ht_ref):
    mix = _dot(of_ref[...], w_ref[0:WIDTH, :]) + _dot(og_ref[...], w_ref[WIDTH:2 * WIDTH, :])
    h = _layer_norm(DN_ALPHA * x_ref[...] + mix, g_ref[...], b_ref[...])
    h_ref[...] = h
    _store_token_tiles(ht_ref, (), h)


def _outproj_ln(x, o_fox, o_gdn, w_bf16, g_row, b_row):
    n, d = x.shape
    row = lambda i: (i, 0)
    fixed = lambda i: (0, 0)
    return pl.pallas_call(
        _outproj_ln_kernel,
        grid=(n // TM_TOK,),
        in_specs=[pl.BlockSpec((TM_TOK, d), row), pl.BlockSpec((TM_TOK, WIDTH), row),
                  pl.BlockSpec((TM_TOK, WIDTH), row), pl.BlockSpec((2 * WIDTH, d), fixed),
                  pl.BlockSpec((1, d), fixed), pl.BlockSpec((1, d), fixed)],
        out_specs=[pl.BlockSpec((TM_TOK, d), row), pl.BlockSpec((TM_TOK * SUBLANES, LANES), row)],
        out_shape=[jax.ShapeDtypeStruct((n, d), F32), jax.ShapeDtypeStruct((n * SUBLANES, LANES), F32)],
        compiler_params=_cparams(("parallel",)),
        name="out_proj_ln",
    )(x, o_fox, o_gdn, w_bf16, g_row, b_row)


def _pool_ln_kernel(x_ref, halo_ref, pw_ref, ps_ref, w_ref, g_ref, b_ref, h_ref, ht_ref, stage_ref,
                    *, tm, pos0, zero_first_halo):
    i = pl.program_id(0)
    stage_ref[0:POOL_HALO, :] = halo_ref[0]
    if zero_first_halo:
        @pl.when(i == 0)
        def _():
            stage_ref[0:POOL_HALO, :] = jnp.zeros((POOL_HALO, stage_ref.shape[1]), F32)
    x = x_ref[...]
    stage_ref[POOL_HALO:POOL_HALO + tm, :] = x
    gdim = x.shape[1] // len(POOL_WINDOWS)
    pos = pos0 + lax.broadcasted_iota(I32, (tm, 1), 0)
    if zero_first_halo:
        pos = pos + i * tm
    parts = []
    for gi, win in enumerate(POOL_WINDOWS):
        cols = slice(gi * gdim, (gi + 1) * gdim)
        s = stage_ref[POOL_HALO:POOL_HALO + tm, cols]
        for j in range(1, win):
            s = s + stage_ref[POOL_HALO - j:POOL_HALO - j + tm, cols]
        cnt = jnp.minimum(pos + 1, win).astype(F32)
        zg = s / cnt - x[:, cols]
        parts.append(_dot(zg, pw_ref[gi]))
    zg = jnp.concatenate(parts, axis=-1) * ps_ref[...]
    mix = _dot(zg, w_ref[...])
    h = _layer_norm(DN_ALPHA * x + mix, g_ref[...], b_ref[...])
    h_ref[...] = h
    _store_token_tiles(ht_ref, (), h)


def _pool_ln(x, x_map, halo_arr, halo_map, n_tiles, tm, pos0, zero_first_halo,
             pw_bf16, ps_row, w_bf16, g_row, b_row):
    d = x.shape[1]
    gdim = d // len(POOL_WINDOWS)
    fixed = lambda i: (0, 0)
    return pl.pallas_call(
        functools.partial(_pool_ln_kernel, tm=tm, pos0=pos0, zero_first_halo=zero_first_halo),
        grid=(n_tiles,),
        in_specs=[pl.BlockSpec((tm, d), x_map),
                  pl.BlockSpec((1, POOL_HALO, d), halo_map),
                  pl.BlockSpec((len(POOL_WINDOWS), gdim, gdim), lambda i: (0, 0, 0)),
                  pl.BlockSpec((1, d), fixed), pl.BlockSpec((d, d), fixed),
                  pl.BlockSpec((1, d), fixed), pl.BlockSpec((1, d), fixed)],
        out_specs=[pl.BlockSpec((tm, d), lambda i: (i, 0)),
                   pl.BlockSpec((tm * SUBLANES, LANES), lambda i: (i, 0))],
        out_shape=[jax.ShapeDtypeStruct((n_tiles * tm, d), F32),
                   jax.ShapeDtypeStruct((n_tiles * tm * SUBLANES, LANES), F32)],
        scratch_shapes=[pltpu.VMEM((POOL_HALO + tm, d), F32)],
        compiler_params=_cparams(("arbitrary",)),
        name="pool_mixer_ln",
    )(x, halo_arr, pw_bf16, ps_row, w_bf16, g_row, b_row)


def _router_kernel(h_ref, wr_ref, br_ref, idx_ref, gate_ref, rank_ref, cnt_ref, carry_ref):
    @pl.when(pl.program_id(0) == 0)
    def _():
        carry_ref[...] = jnp.zeros_like(carry_ref)

    tm = h_ref.shape[0]
    lane = lax.broadcasted_iota(I32, (tm, LANES), 1).astype(F32)
    logits = jnp.dot(h_ref[...], wr_ref[...], precision=HIGHEST, preferred_element_type=F32)
    work = jnp.where(lane < N_EXPERTS, logits + br_ref[...], -jnp.inf)
    vals, ids = [], []
    for _ in range(TOP_K):
        m = jnp.max(work, axis=-1, keepdims=True)
        ik = jnp.min(jnp.where(work == m, lane, float(LANES)), axis=-1, keepdims=True)
        vals.append(m)
        ids.append(ik)
        work = jnp.where(lane == ik, -jnp.inf, work)
    exps = [jnp.exp(v - vals[0]) for v in vals]
    denom = exps[0]
    for e in exps[1:]:
        denom = denom + e
    multihot = jnp.zeros((tm, LANES), F32)
    idx_out = jnp.zeros((tm, LANES), F32)
    gate_out = jnp.zeros((tm, LANES), F32)
    for k in range(TOP_K):
        multihot = multihot + (lane == ids[k]).astype(F32)
        idx_out = jnp.where(lane == k, ids[k], idx_out)
        gate_out = jnp.where(lane == k, exps[k] / denom, gate_out)
    r = lax.broadcasted_iota(I32, (tm, tm), 0)
    c = lax.broadcasted_iota(I32, (tm, tm), 1)
    before = _dot((c < r).astype(F32), multihot) + carry_ref[0:1, :]
    rank_out = jnp.zeros((tm, LANES), F32)
    for k in range(TOP_K):
        rk = jnp.sum(jnp.where(lane == ids[k], before, 0.0), axis=-1, keepdims=True)
        rank_out = jnp.where(lane == k, rk, rank_out)
    idx_ref[...] = idx_out.astype(I32)
    gate_ref[...] = gate_out
    rank_ref[...] = rank_out.astype(I32)
    total = carry_ref[0:1, :] + jnp.sum(multihot, axis=0, keepdims=True)
    carry_ref[...] = jnp.broadcast_to(total, carry_ref.shape)
    cnt_ref[...] = jnp.broadcast_to(total, cnt_ref.shape).astype(I32)


def _router(h, wr_pad, br_row):
    n, d = h.shape
    row = lambda i: (i, 0)
    fixed = lambda i: (0, 0)
    return pl.pallas_call(
        _router_kernel,
        grid=(n // TM_TOK,),
        in_specs=[pl.BlockSpec((TM_TOK, d), row), pl.BlockSpec((d, LANES), fixed),
                  pl.BlockSpec((1, LANES), fixed)],
        out_specs=[pl.BlockSpec((TM_TOK, LANES), row), pl.BlockSpec((TM_TOK, LANES), row),
                   pl.BlockSpec((TM_TOK, LANES), row), pl.BlockSpec((SUBLANES, LANES), fixed)],
        out_shape=[jax.ShapeDtypeStruct((n, LANES), I32), jax.ShapeDtypeStruct((n, LANES), F32),
                   jax.ShapeDtypeStruct((n, LANES), I32), jax.ShapeDtypeStruct((SUBLANES, LANES), I32)],
        scratch_shapes=[pltpu.VMEM((SUBLANES, LANES), F32)],
        compiler_params=_cparams(("arbitrary",)),
        name="moe_router",
    )(h, wr_pad, br_row)


def _tile_copy(src_ref, src_row, dst_ref, dst_row, sem):
    rows = lambda r: pl.ds(pl.multiple_of(r, SUBLANES), SUBLANES)
    return pltpu.make_async_copy(src_ref.at[rows(src_row)], dst_ref.at[rows(dst_row)], sem)


def _tiles_of(ref, n_tok, sem):
    span = ref.at[pl.ds(0, n_tok * SUBLANES)]
    return pltpu.make_async_copy(span, span, sem)


def _dispatch_kernel(slot_ref, pend_ref, ht_ref, xb_ref, zero_ref, sem, zsem):
    tm = ht_ref.shape[0] // SUBLANES
    base = pl.program_id(0) * tm * TOP_K
    n_blocks = xb_ref.shape[0] // (MOE_BLOCK * SUBLANES)

    @pl.when(pl.program_id(0) == 0)
    def _():
        zero_ref[...] = jnp.zeros_like(zero_ref)
        block = lambda b: pltpu.make_async_copy(
            zero_ref, xb_ref.at[pl.ds(pl.multiple_of(b * (MOE_BLOCK * SUBLANES), SUBLANES),
                                      MOE_BLOCK * SUBLANES)], zsem)
        n_used = pend_ref[N_EXPERTS - 1] // MOE_BLOCK
        group_last = lambda e: jnp.maximum(pend_ref[e] // MOE_BLOCK - 1, 0)
        for e in range(N_EXPERTS):
            block(group_last(e)).start()
        lax.fori_loop(n_used, n_blocks, lambda b, c: (block(b).start(), c)[1], 0)
        for e in range(N_EXPERTS):
            block(group_last(e)).wait()
        lax.fori_loop(n_used, n_blocks, lambda b, c: (block(b).wait(), c)[1], 0)

    for r in range(tm):
        for k in range(TOP_K):
            _tile_copy(ht_ref, r * SUBLANES, xb_ref, slot_ref[base + r * TOP_K + k] * SUBLANES, sem).start()
    _tiles_of(xb_ref, tm * TOP_K, sem).wait()


def _dispatch(ht, slot_flat, pend, n_slots):
    n = ht.shape[0] // SUBLANES
    return pl.pallas_call(
        _dispatch_kernel,
        grid_spec=pltpu.PrefetchScalarGridSpec(
            num_scalar_prefetch=2,
            grid=(n // TM_ROW,),
            in_specs=[pl.BlockSpec((TM_ROW * SUBLANES, LANES), lambda i, s, p: (i, 0))],
            out_specs=pl.BlockSpec(memory_space=pl.ANY),
            scratch_shapes=[pltpu.VMEM((MOE_BLOCK * SUBLANES, LANES), F32), pltpu.SemaphoreType.DMA(()),
                            pltpu.SemaphoreType.DMA(())]),
        out_shape=jax.ShapeDtypeStruct((n_slots * SUBLANES, LANES), F32),
        compiler_params=_cparams(("arbitrary",)),
        name="moe_dispatch",
    )(slot_flat, pend, ht)


def _expert_kernel(be_ref, nu_ref, xb_ref, wup_ref, bup_ref, wdn_ref, bdn_ref, yb_ref,
                   wup_bf_ref, wdn_bf_ref):
    b = pl.program_id(0)

    @pl.when(b < nu_ref[0])
    def _():
        @pl.when((b == 0) | (be_ref[b] != be_ref[jnp.maximum(b - 1, 0)]))
        def _():
            wup_bf_ref[...] = wup_ref[0, 0].astype(BF16)
            wdn_bf_ref[...] = wdn_ref[0, 0].astype(BF16)

        d_exp = wdn_ref.shape[2]
        x = _load_token_tiles(xb_ref, (), MOE_BLOCK)
        hu = jnp.dot(x.astype(BF16), wup_bf_ref[...], preferred_element_type=F32) + bup_ref[0, 0]
        glu = jnp.minimum(hu[:, :d_exp], SWIGLU_LIMIT)
        lin = jnp.clip(hu[:, d_exp:], -SWIGLU_LIMIT, SWIGLU_LIMIT)
        a = glu * _sigmoid(SWIGLU_ALPHA * glu) * (lin + 1.0)
        y = jnp.dot(a.astype(BF16), wdn_bf_ref[...], preferred_element_type=F32) + bdn_ref[0, 0]
        _store_token_tiles(yb_ref, (), y)

    @pl.when(b >= nu_ref[0])
    def _():
        yb_ref[...] = jnp.zeros_like(yb_ref)


def _experts(xb, block_e, n_used, layer, w_up, b_up, w_dn, b_dn):
    d = SUBLANES * LANES
    rows = MOE_BLOCK * SUBLANES
    n_blocks = xb.shape[0] // rows
    d_up = w_up.shape[3]
    d_exp = w_dn.shape[2]
    wsel = lambda b, be, nu: (layer, be[b], 0, 0)
    return pl.pallas_call(
        _expert_kernel,
        grid_spec=pltpu.PrefetchScalarGridSpec(
            num_scalar_prefetch=2,
            grid=(n_blocks,),
            in_specs=[pl.BlockSpec((rows, LANES), lambda b, be, nu: (jnp.minimum(b, nu[0] - 1), 0)),
                      pl.BlockSpec((1, 1, d, d_up), wsel),
                      pl.BlockSpec((1, 1, 1, d_up), wsel),
                      pl.BlockSpec((1, 1, d_exp, d), wsel),
                      pl.BlockSpec((1, 1, 1, d), wsel)],
            out_specs=pl.BlockSpec((rows, LANES), lambda b, be, nu: (b, 0)),
            scratch_shapes=[pltpu.VMEM((d, d_up), BF16), pltpu.VMEM((d_exp, d), BF16)]),
        out_shape=jax.ShapeDtypeStruct(xb.shape, F32),
        compiler_params=_cparams(("arbitrary",)),
        name="moe_experts",
    )(block_e, n_used, xb, w_up, b_up, w_dn, b_dn)


def _combine_kernel(slot_ref, h_ref, gate_ref, p_ref, wpg_ref, wpp_ref, g_ref, b_ref, yb_ref,
                    out_ref, buf_ref, sem):
    tm = h_ref.shape[0]
    base = pl.program_id(0) * tm * TOP_K
    for r in range(tm):
        for k in range(TOP_K):
            _tile_copy(yb_ref, slot_ref[base + r * TOP_K + k] * SUBLANES, buf_ref.at[k], r * SUBLANES,
                       sem).start()
    for k in range(TOP_K):
        _tiles_of(buf_ref.at[k], tm, sem).wait()
    gate = gate_ref[...]
    moe = _load_token_tiles(buf_ref, (0,), tm) * gate[:, 0:1]
    for k in range(1, TOP_K):
        moe = moe + _load_token_tiles(buf_ref, (k,), tm) * gate[:, k:k + 1]
    h2 = _layer_norm(DN_ALPHA * h_ref[...] + moe, g_ref[...], b_ref[...])
    out_ref[...] = h2 + _sigmoid(_dot(h2, wpg_ref[...])) * _dot(p_ref[...], wpp_ref[...])


def _combine(h, gate, p, wpg_bf16, wpp_bf16, g_row, b_row, yb, slot_flat):
    n, d = h.shape
    row = lambda i, s: (i, 0)
    fixed = lambda i, s: (0, 0)
    return pl.pallas_call(
        _combine_kernel,
        grid_spec=pltpu.PrefetchScalarGridSpec(
            num_scalar_prefetch=1,
            grid=(n // TM_ROW,),
            in_specs=[pl.BlockSpec((TM_ROW, d), row), pl.BlockSpec((TM_ROW, LANES), row),
                      pl.BlockSpec((TM_ROW, p.shape[1]), row), pl.BlockSpec((d, d), fixed),
                      pl.BlockSpec((p.shape[1], d), fixed), pl.BlockSpec((1, d), fixed),
                      pl.BlockSpec((1, d), fixed), pl.BlockSpec(memory_space=pl.ANY)],
            out_specs=pl.BlockSpec((TM_ROW, d), row),
            scratch_shapes=[pltpu.VMEM((TOP_K, TM_ROW * SUBLANES, LANES), F32),
                            pltpu.SemaphoreType.DMA(())]),
        out_shape=jax.ShapeDtypeStruct((n, d), F32),
        compiler_params=_cparams(("arbitrary",)),
        name="moe_combine_ln_embed",
    )(slot_flat, h, gate, p, wpg_bf16, wpp_bf16, g_row, b_row, yb)


def _layer_tail(h, ht, p, layer, g2, b2, w_r, b_r, w_up, b_up, w_dn, b_dn, w_pg, w_pp):
    n, d = h.shape
    wr_pad = jnp.pad(w_r, ((0, 0), (0, LANES - N_EXPERTS)))
    br_row = jnp.pad(b_r, (0, LANES - N_EXPERTS))[None]
    idx, gate, rank, cnt = _router(h, wr_pad, br_row)
    counts = cnt[0, :N_EXPERTS]
    padded = (counts + MOE_BLOCK - 1) // MOE_BLOCK * MOE_BLOCK
    pend = jnp.cumsum(padded).astype(I32)
    pstart = pend - padded
    slot = (pstart[idx[:, :TOP_K]] + rank[:, :TOP_K]).reshape(-1).astype(I32)
    n_asg = n * TOP_K
    n_blocks = n_asg // MOE_BLOCK + N_EXPERTS
    assert n_asg % MOE_BLOCK == 0 and n % TM_ROW == 0
    n_slots = n_blocks * MOE_BLOCK
    n_used = pend[-1] // MOE_BLOCK
    blk = jnp.minimum(jnp.arange(n_blocks, dtype=I32), n_used - 1) * MOE_BLOCK
    block_e = jnp.minimum(jnp.sum((pend[None, :] <= blk[:, None]).astype(I32), axis=1), N_EXPERTS - 1)
    xb = _dispatch(ht, slot, pend, n_slots)
    yb = _experts(xb, block_e, n_used[None], layer, w_up, b_up[:, :, None, :], w_dn, b_dn[:, :, None, :])
    return _combine(h, gate, p, w_pg.astype(BF16), w_pp.astype(BF16), g2[None], b2[None], yb, slot)


def _lane_row(v, lane0):
    return jnp.zeros((1, LANES), F32).at[0, lane0:lane0 + v.shape[0]].set(v.astype(F32))


def kernel(x_prompt, x_sample, cache_fox_k, cache_fox_v, cache_fox_logf, state_gdn, state_gdn_conv,
           cache_pool, p_prompt, p_sample, w_in_ab, b_fgate, gdn_a_log, gdn_dt_bias, gdn_conv_w,
           gdn_norm_g, w_out_ab, pool_w, pool_scale, w_out_pool, ln1_g, ln1_b, ln2_g, ln2_b,
           w_router, b_router, w_expert_up, b_expert_up, w_expert_down, b_expert_down,
           w_ple_gate, w_ple_proj):
    n_pb, seq, d = x_prompt.shape
    n_sb, dseq, _ = x_sample.shape
    past = cache_fox_k.shape[2]
    assert n_pb == 1 and dseq == CHUNK and past % dseq == 0 and seq % TQ == 0
    assert d == SUBLANES * LANES
    n_p = n_pb * seq
    n_s = n_sb * dseq
    n = n_p + n_s
    x = jnp.concatenate([x_prompt.reshape(n_p, d), x_sample.reshape(n_s, d)], axis=0)

    def tail(h, ht, i):
        p = jnp.concatenate([p_prompt[i].reshape(n_p, -1), p_sample[i].reshape(n_s, -1)], axis=0)
        return _layer_tail(h, ht, p, i, ln2_g[i], ln2_b[i], w_router[i], b_router[i], w_expert_up,
                           b_expert_up, w_expert_down, b_expert_down, w_ple_gate[i], w_ple_proj[i])

    w_in = w_in_ab[0]
    n_small = 3 * N_HEADS
    ff0 = 3 * WIDTH
    gq0 = ff0 + N_HEADS
    ga0 = gq0 + 4 * WIDTH
    w_small = jnp.concatenate([w_in[:, ff0:gq0], w_in[:, ga0:ga0 + 2 * N_HEADS],
                               jnp.zeros((d, LANES - n_small), F32)], axis=1)
    w_all = jnp.concatenate([w_in[:, :ff0], w_in[:, gq0:ga0], w_small], axis=1).astype(BF16)
    u = _proj(x, w_all, _lane_row(b_fgate[0], LANE_F))

    u3 = u[None]
    u64 = u.reshape(n // CHUNK, CHUNK, U_COLS)
    cq_p, ck_p = _cumsum(u3, 1, n_p, TK, COL_S // LANES)
    lf_s = jnp.concatenate(
        [jnp.pad(cache_fox_logf[0].astype(F32), ((0, 0), (0, 0), (0, LANES - N_HEADS))),
         u[n_p:, COL_S:].reshape(n_sb, dseq, LANES)], axis=1)
    cq_s, ck_s = _cumsum(lf_s, n_sb, past + dseq, past + dseq, 0)

    of_p = _fox(u3, lambda b, i: (0, i, COL_Q // WIDTH), u3, lambda b, j: (0, j, COL_K // WIDTH),
                u3, lambda b, j: (0, j, COL_V // WIDTH), cq_p, lambda b, i: (0, i, 0),
                ck_p, lambda b, j: (0, 0, j), 1, n_p // TQ, TQ, TK, 0)
    k_new = u[n_p:, COL_K:COL_K + WIDTH].reshape(n_sb, dseq, WIDTH)
    v_new = u[n_p:, COL_V:COL_V + WIDTH].reshape(n_sb, dseq, WIDTH)
    k_all = jnp.concatenate([cache_fox_k[0].reshape(n_sb, past, WIDTH), k_new], axis=1)
    v_all = jnp.concatenate([cache_fox_v[0].reshape(n_sb, past, WIDTH), v_new], axis=1)
    of_s = _fox(u64, lambda b, i: (n_p // CHUNK + b, 0, COL_Q // WIDTH), k_all, lambda b, j: (b, 0, 0),
                v_all, lambda b, j: (b, 0, 0), cq_s, lambda b, i: (b, past // dseq, 0),
                ck_s, lambda b, j: (b, 0, 0), n_sb, 1, dseq, past + dseq, past)
    o_fox = jnp.concatenate([of_p.reshape(n_p, WIDTH), of_s.reshape(n_s, WIDTH)], axis=0)

    gw = 3 * WIDTH
    conv_w = jnp.pad(gdn_conv_w[0], ((0, SUBLANES - CONV_WIDTH), (0, 0)))
    gdn_args = (conv_w, _lane_row(gdn_a_log[0], LANE_A), _lane_row(gdn_dt_bias[0], LANE_A),
                gdn_norm_g[0][None])
    n_pstep = n_p // GDN_ROWS
    ends = lambda k: (jnp.asarray((np.arange(k) == 0).astype(np.int32)),
                      jnp.asarray((np.arange(k) == k - 1).astype(np.int32)))
    og_p, st_p = _gdn(u.reshape(n // GDN_ROWS, GDN_ROWS, U_COLS), 0, *ends(n_pstep),
                      jnp.zeros((n_pstep,), I32), jnp.zeros((1, SUBLANES, gw), F32), gdn_args[0],
                      jnp.zeros((1, N_HEADS, HEAD_DIM, HEAD_DIM), F32), *gdn_args[1:])
    conv_past = jnp.pad(state_gdn_conv[0].astype(F32), ((0, 0), (SUBLANES - (CONV_WIDTH - 1), 0), (0, 0)))
    ones = jnp.ones((n_sb,), I32)
    og_s, st_s = _gdn(u64, n_p // CHUNK, ones, ones, jnp.arange(n_sb, dtype=I32), conv_past, gdn_args[0],
                      state_gdn[0].astype(F32), *gdn_args[1:])
    o_gdn = jnp.concatenate([og_p.reshape(n_p, WIDTH), og_s.reshape(n_s, WIDTH)], axis=0)
    h, ht = _outproj_ln(x, o_fox, o_gdn, w_out_ab[0].astype(BF16), ln1_g[0][None], ln1_b[0][None])
    x1 = tail(h, ht, 0)

    pool_args = (pool_w[0].astype(BF16), pool_scale[0][None], w_out_pool[0].astype(BF16),
                 ln1_g[1][None], ln1_b[1][None])
    ratio = TM_TOK // POOL_HALO
    x1_halo = x1.reshape(n // POOL_HALO, POOL_HALO, d)
    h_p, ht_p = _pool_ln(x1, lambda i: (i, 0), x1_halo, lambda i: (jnp.maximum(i * ratio - 1, 0), 0, 0),
                         n_p // TM_TOK, TM_TOK, 0, True, *pool_args)
    cache16 = jnp.pad(cache_pool[0].astype(F32), ((0, 0), (POOL_HALO - POOL_STATE, 0), (0, 0)))
    h_s, ht_s = _pool_ln(x1, lambda i: (n_p // dseq + i, 0), cache16, lambda i: (i, 0, 0),
                         n_sb, dseq, past, False, *pool_args)
    x2 = tail(jnp.concatenate([h_p, h_s], axis=0), jnp.concatenate([ht_p, ht_s], axis=0), 1)

    up = u[:n_p]
    us = u[n_p:]
    return (x2[:n_p].reshape(n_pb, seq, d), x2[n_p:].reshape(n_sb, dseq, d),
            up[:, COL_K:COL_K + WIDTH].reshape(1, n_pb, seq, N_HEADS, HEAD_DIM),
            up[:, COL_V:COL_V + WIDTH].reshape(1, n_pb, seq, N_HEADS, HEAD_DIM),
            up[:, COL_S:COL_S + N_HEADS].reshape(1, n_pb, seq, N_HEADS),
            st_p.reshape(1, n_pb, N_HEADS, HEAD_DIM, HEAD_DIM),
            up[seq - (CONV_WIDTH - 1):, COL_G:COL_G + gw].reshape(1, n_pb, CONV_WIDTH - 1, gw),
            x1[n_p - POOL_STATE:n_p].reshape(1, n_pb, POOL_STATE, d),
            us[:, COL_K:COL_K + WIDTH].reshape(1, n_sb, dseq, N_HEADS, HEAD_DIM),
            us[:, COL_V:COL_V + WIDTH].reshape(1, n_sb, dseq, N_HEADS, HEAD_DIM),
            us[:, COL_S:COL_S + N_HEADS].reshape(1, n_sb, dseq, N_HEADS),
            st_s.reshape(1, n_sb, N_HEADS, HEAD_DIM, HEAD_DIM),
            us[:, COL_G:COL_G + gw].reshape(n_sb, dseq, gw)[:, dseq - (CONV_WIDTH - 1):].reshape(
                1, n_sb, CONV_WIDTH - 1, gw),
            x1[n_p:].reshape(n_sb, dseq, d)[:, dseq - POOL_STATE:].reshape(1, n_sb, POOL_STATE, d))
```

```python
import functools

import numpy as np
import jax
import jax.numpy as jnp
from jax import lax
from jax.experimental import pallas as pl
from jax.experimental.pallas import tpu as pltpu

F32 = jnp.float32
BF16 = jnp.bfloat16
I32 = jnp.int32
HIGHEST = lax.Precision.HIGHEST

LANES = 128
SUBLANES = 8
VMEM_LIMIT = 56 * 1024 * 1024

HEAD_DIM = 128
N_HEADS = 4
WIDTH = N_HEADS * HEAD_DIM
CHUNK = 64
CONV_WIDTH = 4
POOL_WINDOWS = (2, 4, 8, 16)
POOL_HALO = 16
POOL_STATE = 15
N_EXPERTS = 32
TOP_K = 4
SWIGLU_LIMIT = 7.0
SWIGLU_ALPHA = 1.702
DEPTH = 2
DN_ALPHA = (2 * DEPTH) ** 0.25
LN_EPS = 1e-5
NORM_EPS = 1e-6
NEG_INF = -1e30
LOG2E = 1.4426950408889634

COL_Q, COL_K, COL_V = 0, WIDTH, 2 * WIDTH
COL_G = 3 * WIDTH
COL_Z = 6 * WIDTH
COL_S = 7 * WIDTH
U_COLS = COL_S + LANES
LANE_F, LANE_A, LANE_B = 0, N_HEADS, 2 * N_HEADS

TM_PROJ = 256
TM_TOK = 512
TM_ROW = 256
MOE_BLOCK = 512
TQ = 1024
TK = 512
GDN_ROWS = 256


def _cparams(sem):
    return pltpu.CompilerParams(dimension_semantics=sem, vmem_limit_bytes=VMEM_LIMIT)


def _softplus(x):
    return jnp.maximum(x, 0.0) + jnp.log1p(jnp.exp(-jnp.abs(x)))


def _sigmoid(x):
    return 1.0 / (1.0 + jnp.exp(-x))


def _silu(x):
    return x * _sigmoid(x)


def _layer_norm(y, g, b):
    mu = jnp.mean(y, axis=-1, keepdims=True)
    yc = y - mu
    var = jnp.mean(yc * yc, axis=-1, keepdims=True)
    return yc * lax.rsqrt(var + LN_EPS) * g + b


def _dot_general(a, b, dims, hi):
    if hi:
        return lax.dot_general(a.astype(F32), b.astype(F32), (dims, ((), ())), precision=HIGHEST,
                               preferred_element_type=F32)
    return lax.dot_general(a.astype(BF16), b.astype(BF16), (dims, ((), ())), preferred_element_type=F32)


def _dot(a, b, hi=False):
    return _dot_general(a, b, ((1,), (0,)), hi)


def _dot_nt(a, b, hi=False):
    return _dot_general(a, b, ((1,), (1,)), hi)


def _dot_tn(a, b, hi=False):
    return _dot_general(a, b, ((0,), (0,)), hi)


def _spread_lanes(x, width):
    if width % LANES == 0:
        return jnp.concatenate([x] * (width // LANES), axis=1)
    return jnp.broadcast_to(x[:, 0:1], (x.shape[0], width))


def _load_token_tiles(ref, lead, n_tok):
    return jnp.concatenate([ref[(*lead, pl.ds(j, n_tok, stride=SUBLANES), slice(None))]
                            for j in range(SUBLANES)], axis=1)


def _store_token_tiles(ref, lead, x):
    for j in range(SUBLANES):
        ref[(*lead, pl.ds(j, x.shape[0], stride=SUBLANES), slice(None))] = x[:, j * LANES:(j + 1) * LANES]


def _lanes_to_rows(x, lane0):
    r = lax.broadcasted_iota(I32, (SUBLANES, LANES), 0)
    c = lax.broadcasted_iota(I32, (SUBLANES, LANES), 1)
    sel = (c == r + lane0).astype(F32)
    return lax.dot_general(sel, x, (((1,), (1,)), ((), ())), precision=HIGHEST,
                           preferred_element_type=F32)


def _proj_kernel(x_ref, w_ref, bf_ref, u_ref, *, hi):
    u = _dot(x_ref[...], w_ref[...], hi)
    u_ref[...] = u

    @pl.when(pl.program_id(1) == pl.num_programs(1) - 1)
    def _():
        small = u[:, u.shape[1] - LANES:]
        lane = lax.broadcasted_iota(I32, small.shape, 1)
        logf = -_softplus(-(small + bf_ref[...]))
        u_ref[:, u.shape[1] - LANES:] = jnp.where(lane < LANE_A, logf, small)


def _proj(x, w, bf_row, tn, hi):
    n, d = x.shape
    m = w.shape[1]
    return pl.pallas_call(
        functools.partial(_proj_kernel, hi=hi),
        grid=(n // TM_PROJ, m // tn),
        in_specs=[pl.BlockSpec((TM_PROJ, d), lambda i, j: (i, 0)),
                  pl.BlockSpec((d, tn), lambda i, j: (0, j)),
                  pl.BlockSpec((1, LANES), lambda i, j: (0, 0))],
        out_specs=pl.BlockSpec((TM_PROJ, tn), lambda i, j: (i, j)),
        out_shape=jax.ShapeDtypeStruct((n, m), F32),
        compiler_params=_cparams(("parallel", "parallel")),
        name="in_proj",
    )(x, w, bf_row)


def _cumsum_kernel(lf_ref, crep_ref, crow_ref, carry_ref):
    @pl.when(pl.program_id(1) == 0)
    def _():
        carry_ref[...] = jnp.zeros_like(carry_ref)

    lf = lf_ref[0]
    t = lf.shape[0]
    r = lax.broadcasted_iota(I32, (t, t), 0)
    c = lax.broadcasted_iota(I32, (t, t), 1)
    tril = (c <= r).astype(F32)
    cs = jnp.dot(tril, lf, precision=HIGHEST, preferred_element_type=F32) + carry_ref[0:1, :]
    carry_ref[...] = jnp.broadcast_to(cs[t - 1:t, :], carry_ref.shape)
    c2 = cs * LOG2E
    crow_ref[0] = _lanes_to_rows(c2, LANE_F)
    for h in range(N_HEADS):
        crep_ref[0, :, h * HEAD_DIM:(h + 1) * HEAD_DIM] = jnp.broadcast_to(
            c2[:, LANE_F + h:LANE_F + h + 1], (t, HEAD_DIM))


def _cumsum(arr, n_batch, length, tl, col_block):
    return pl.pallas_call(
        _cumsum_kernel,
        grid=(n_batch, length // tl),
        in_specs=[pl.BlockSpec((1, tl, LANES), lambda b, j: (b, j, col_block))],
        out_specs=[pl.BlockSpec((1, tl, WIDTH), lambda b, j: (b, j, 0)),
                   pl.BlockSpec((1, SUBLANES, tl), lambda b, j: (b, 0, j))],
        out_shape=[jax.ShapeDtypeStruct((n_batch, length, WIDTH), F32),
                   jax.ShapeDtypeStruct((n_batch, SUBLANES, length), F32)],
        scratch_shapes=[pltpu.VMEM((SUBLANES, LANES), F32)],
        compiler_params=_cparams(("parallel", "arbitrary")),
        name="logf_cumsum",
    )(arr)


def _fox_kernel(qi_ref, kj_ref, last_ref, q_ref, k_ref, v_ref, cq_ref, ck_ref, o_ref,
                m_ref, l_ref, acc_ref, *, tq, tk, past, hi):
    s_idx = pl.program_id(1)
    qi = qi_ref[s_idx]
    kj = kj_ref[s_idx]

    @pl.when(kj == 0)
    def _():
        m_ref[...] = jnp.full_like(m_ref, NEG_INF)
        l_ref[...] = jnp.zeros_like(l_ref)
        acc_ref[...] = jnp.zeros_like(acc_ref)

    def update(masked):
        if masked:
            q_pos = past + qi * tq + lax.broadcasted_iota(I32, (tq, tk), 0)
            k_pos = kj * tk + lax.broadcasted_iota(I32, (tq, tk), 1)
            visible = k_pos <= q_pos
        for h in range(N_HEADS):
            cols = slice(h * HEAD_DIM, (h + 1) * HEAD_DIM)
            q = q_ref[0, :, cols] * (HEAD_DIM ** -0.5 * LOG2E)
            t = _dot_nt(q, k_ref[0, :, cols], hi) - ck_ref[0, h:h + 1, :]
            if masked:
                t = jnp.where(visible, t, NEG_INF)
            cq = cq_ref[0, :, cols]
            m_prev = m_ref[h]
            m_new = jnp.maximum(m_prev, jnp.max(t, axis=-1, keepdims=True) + cq)
            p = jnp.exp2(t - _spread_lanes(m_new - cq, tk))
            alpha = jnp.exp2(m_prev - m_new)
            l_ref[h] = alpha * l_ref[h] + jnp.sum(p, axis=-1, keepdims=True)
            acc_ref[:, cols] = alpha * acc_ref[:, cols] + _dot(p, v_ref[0, :, cols], hi)
            m_ref[h] = m_new

    crosses_diagonal = kj * tk + (tk - 1) > past + qi * tq
    pl.when(crosses_diagonal)(functools.partial(update, True))
    pl.when(jnp.logical_not(crosses_diagonal))(functools.partial(update, False))

    @pl.when(last_ref[s_idx] == 1)
    def _():
        for h in range(N_HEADS):
            cols = slice(h * HEAD_DIM, (h + 1) * HEAD_DIM)
            o_ref[0, :, cols] = acc_ref[:, cols] / l_ref[h]


def _fox_schedule(n_q, tq, tk, past):
    qi, kj, last = [], [], []
    for i in range(n_q):
        hi = (past + (i + 1) * tq - 1) // tk
        for j in range(hi + 1):
            qi.append(i)
            kj.append(j)
            last.append(1 if j == hi else 0)
    return (jnp.asarray(np.array(qi, np.int32)), jnp.asarray(np.array(kj, np.int32)),
            jnp.asarray(np.array(last, np.int32)))


def _fox(q_arr, q_map, k_arr, k_map, v_arr, v_map, cq_arr, cq_map, ck_arr, ck_map,
         n_batch, n_q, tq, tk, past, hi=False):
    qi, kj, last = _fox_schedule(n_q, tq, tk, past)
    n_steps = int(qi.shape[0])
    spec = lambda shape, fn, tab: pl.BlockSpec(shape, lambda b, s, qi_r, kj_r, la_r: fn(b, (qi_r if tab == 'q' else kj_r)[s]))
    return pl.pallas_call(
        functools.partial(_fox_kernel, tq=tq, tk=tk, past=past, hi=hi),
        grid_spec=pltpu.PrefetchScalarGridSpec(
            num_scalar_prefetch=3,
            grid=(n_batch, n_steps),
            in_specs=[spec((1, tq, WIDTH), q_map, 'q'),
                      spec((1, tk, WIDTH), k_map, 'k'),
                      spec((1, tk, WIDTH), v_map, 'k'),
                      spec((1, tq, WIDTH), cq_map, 'q'),
                      spec((1, SUBLANES, tk), ck_map, 'k')],
            out_specs=spec((1, tq, WIDTH), lambda b, i: (b, i, 0), 'q'),
            scratch_shapes=[pltpu.VMEM((N_HEADS, tq, HEAD_DIM), F32),
                            pltpu.VMEM((N_HEADS, tq, HEAD_DIM), F32),
                            pltpu.VMEM((tq, WIDTH), F32)]),
        out_shape=jax.ShapeDtypeStruct((n_batch, n_q * tq, WIDTH), F32),
        compiler_params=_cparams(("parallel", "arbitrary")),
        name="fox_attention",
    )(qi, kj, last, q_arr, k_arr, v_arr, cq_arr, ck_arr)


def _gdn_kernel(first_ref, last_ref, seq_ref,
                pre_ref, z_ref, sm_ref, cpast_ref, convw_ref, s0_ref, alog_ref, dtb_ref, ng_ref,
                o_ref, sout_ref, stage_ref, s_ref, *, rows, hi):
    dot, dot_nt, dot_tn = (functools.partial(f, hi=hi) for f in (_dot, _dot_nt, _dot_tn))
    step = pl.program_id(0)
    halo = SUBLANES
    n_chunks = rows // CHUNK

    @pl.when(first_ref[step] == 1)
    def _():
        stage_ref[0:halo, :] = cpast_ref[0]
        s_ref[...] = s0_ref[0]

    stage_ref[halo:halo + rows, :] = pre_ref[0]
    conv = stage_ref[halo:halo + rows, :] * convw_ref[CONV_WIDTH - 1:CONV_WIDTH, :]
    for j in range(1, CONV_WIDTH):
        conv = conv + (stage_ref[halo - j:halo - j + rows, :]
                       * convw_ref[CONV_WIDTH - 1 - j:CONV_WIDTH - j, :])
    stage_ref[0:halo, :] = stage_ref[rows:rows + halo, :]
    act = _silu(conv)

    small = sm_ref[0]
    beta_all = _sigmoid(small)
    g_all = -jnp.exp(alog_ref[...]) * _softplus(small + dtb_ref[...])
    r = lax.broadcasted_iota(I32, (rows, rows), 0)
    c = lax.broadcasted_iota(I32, (rows, rows), 1)
    same_chunk = (r // CHUNK) == (c // CHUNK)
    incl = same_chunk & (c <= r)
    strict = same_chunk & (c < r)
    eye = (c == r).astype(F32)
    gc_all = jnp.dot(incl.astype(F32), g_all, precision=HIGHEST, preferred_element_type=F32)
    gc_rows = _lanes_to_rows(gc_all, LANE_A)

    for h in range(N_HEADS):
        cols = slice(h * HEAD_DIM, (h + 1) * HEAD_DIM)
        q = act[:, h * HEAD_DIM:(h + 1) * HEAD_DIM]
        k = act[:, WIDTH + h * HEAD_DIM:WIDTH + (h + 1) * HEAD_DIM]
        v = act[:, 2 * WIDTH + h * HEAD_DIM:2 * WIDTH + (h + 1) * HEAD_DIM]
        q = q * lax.rsqrt(jnp.sum(q * q, axis=-1, keepdims=True) + NORM_EPS) * (HEAD_DIM ** -0.5)
        k = k * lax.rsqrt(jnp.sum(k * k, axis=-1, keepdims=True) + NORM_EPS)
        beta = beta_all[:, LANE_B + h:LANE_B + h + 1]
        gc = gc_all[:, LANE_A + h:LANE_A + h + 1]
        diff = gc - gc_rows[h:h + 1, :]
        decay = jnp.where(incl, jnp.exp(jnp.where(incl, diff, 0.0)), 0.0)
        kb = k * beta
        vb = v * beta
        low = jnp.where(strict, dot_nt(kb, k) * decay, 0.0)
        inv = eye - low
        pw = dot(low, low)
        n_sq = CHUNK.bit_length() - 2
        for it in range(n_sq):
            inv = inv + dot(inv, pw)
            if it + 1 < n_sq:
                pw = dot(pw, pw)
        egc = jnp.exp(gc)
        uw = dot(inv, jnp.concatenate([vb, kb * egc], axis=1))
        intra = jnp.where(incl, dot_nt(q, k) * decay, 0.0)
        qd = q * egc
        g_last = [gc[(g + 1) * CHUNK - 1:(g + 1) * CHUNK, :] for g in range(n_chunks)]
        kd = k * jnp.exp(jnp.concatenate([jnp.broadcast_to(gl, (CHUNK, 1)) for gl in g_last], axis=0)
                         - gc)
        state = s_ref[h]
        v_new = []
        for g in range(n_chunks):
            rs = slice(g * CHUNK, (g + 1) * CHUNK)
            v_new.append(uw[rs, :HEAD_DIM] - dot(uw[rs, HEAD_DIM:], state))
            v_rows = jnp.concatenate(
                v_new + [jnp.zeros((rows - (g + 1) * CHUNK, HEAD_DIM), F32)] * (g + 1 < n_chunks), axis=0)
            o = dot(qd[rs, :], state) + dot(intra[rs, :], v_rows)
            state = state * jnp.exp(g_last[g]) + dot_tn(kd[rs, :], v_new[g])
            o = (o * lax.rsqrt(jnp.mean(o * o, axis=-1, keepdims=True) + NORM_EPS)
                 * ng_ref[...] * _silu(z_ref[0, rs, cols]))
            o_ref[0, rs, cols] = o
        s_ref[h] = state

    @pl.when(last_ref[step] == 1)
    def _():
        sout_ref[0] = s_ref[...]


def _gdn(u_view, hi, first, last, seq, conv_past, conv_w, s0, alog_row, dtb_row, ng_row):
    n_steps, rows, _ = u_view.shape
    n_seq = s0.shape[0]
    gw = 3 * WIDTH
    return pl.pallas_call(
        functools.partial(_gdn_kernel, rows=rows, hi=hi),
        grid_spec=pltpu.PrefetchScalarGridSpec(
            num_scalar_prefetch=3,
            grid=(n_steps,),
            in_specs=[pl.BlockSpec((1, rows, gw), lambda s, f, l, q: (s, 0, COL_G // gw)),
                      pl.BlockSpec((1, rows, WIDTH), lambda s, f, l, q: (s, 0, COL_Z // WIDTH)),
                      pl.BlockSpec((1, rows, LANES), lambda s, f, l, q: (s, 0, COL_S // LANES)),
                      pl.BlockSpec((1, SUBLANES, gw), lambda s, f, l, q: (q[s], 0, 0)),
                      pl.BlockSpec((SUBLANES, gw), lambda s, f, l, q: (0, 0)),
                      pl.BlockSpec((1, N_HEADS, HEAD_DIM, HEAD_DIM), lambda s, f, l, q: (q[s], 0, 0, 0)),
                      pl.BlockSpec((1, LANES), lambda s, f, l, q: (0, 0)),
                      pl.BlockSpec((1, LANES), lambda s, f, l, q: (0, 0)),
                      pl.BlockSpec((1, LANES), lambda s, f, l, q: (0, 0))],
            out_specs=[pl.BlockSpec((1, rows, WIDTH), lambda s, f, l, q: (s, 0, 0)),
                       pl.BlockSpec((1, N_HEADS, HEAD_DIM, HEAD_DIM), lambda s, f, l, q: (q[s], 0, 0, 0))],
            scratch_shapes=[pltpu.VMEM((rows + SUBLANES, gw), F32),
                            pltpu.VMEM((N_HEADS, HEAD_DIM, HEAD_DIM), F32)]),
        out_shape=[jax.ShapeDtypeStruct((n_steps, rows, WIDTH), F32),
                   jax.ShapeDtypeStruct((n_seq, N_HEADS, HEAD_DIM, HEAD_DIM), F32)],
        compiler_params=_cparams(("arbitrary",)),
        name="gated_deltanet",
    )(first, last, seq, u_view, u_view, u_view, conv_past, conv_w, s0, alog_row, dtb_row, ng_row)


def _outproj_ln_kernel(x_ref, of_ref, og_ref, w_ref, wf_ref, g_ref, b_ref, h_ref, ht_ref, *, hi_from):
    def run(hi):
        w = wf_ref if hi else w_ref
        mix = _dot(of_ref[...], w[0:WIDTH, :], hi) + _dot(og_ref[...], w[WIDTH:2 * WIDTH, :], hi)
        h = _layer_norm(DN_ALPHA * x_ref[...] + mix, g_ref[...], b_ref[...])
        h_ref[...] = h
        _store_token_tiles(ht_ref, (), h)

    pl.when(pl.program_id(0) < hi_from)(functools.partial(run, False))
    pl.when(pl.program_id(0) >= hi_from)(functools.partial(run, True))


def _outproj_ln(x, o_fox, o_gdn, w, hi_from, g_row, b_row):
    n, d = x.shape
    row = lambda i: (i, 0)
    fixed = lambda i: (0, 0)
    return pl.pallas_call(
        functools.partial(_outproj_ln_kernel, hi_from=hi_from),
        grid=(n // TM_TOK,),
        in_specs=[pl.BlockSpec((TM_TOK, d), row), pl.BlockSpec((TM_TOK, WIDTH), row),
                  pl.BlockSpec((TM_TOK, WIDTH), row), pl.BlockSpec((2 * WIDTH, d), fixed),
                  pl.BlockSpec((2 * WIDTH, d), fixed),
                  pl.BlockSpec((1, d), fixed), pl.BlockSpec((1, d), fixed)],
        out_specs=[pl.BlockSpec((TM_TOK, d), row), pl.BlockSpec((TM_TOK * SUBLANES, LANES), row)],
        out_shape=[jax.ShapeDtypeStruct((n, d), F32), jax.ShapeDtypeStruct((n * SUBLANES, LANES), F32)],
        compiler_params=_cparams(("parallel",)),
        name="out_proj_ln",
    )(x, o_fox, o_gdn, w.astype(BF16), w, g_row, b_row)


def _pool_ln_kernel(x_ref, halo_ref, pw_ref, ps_ref, w_ref, g_ref, b_ref, h_ref, ht_ref, stage_ref,
                    *, tm, pos0, zero_first_halo):
    i = pl.program_id(0)
    stage_ref[0:POOL_HALO, :] = halo_ref[0]
    if zero_first_halo:
        @pl.when(i == 0)
        def _():
            stage_ref[0:POOL_HALO, :] = jnp.zeros((POOL_HALO, stage_ref.shape[1]), F32)
    x = x_ref[...]
    stage_ref[POOL_HALO:POOL_HALO + tm, :] = x
    gdim = x.shape[1] // len(POOL_WINDOWS)
    pos = pos0 + lax.broadcasted_iota(I32, (tm, 1), 0)
    if zero_first_halo:
        pos = pos + i * tm
    parts = []
    for gi, win in enumerate(POOL_WINDOWS):
        cols = slice(gi * gdim, (gi + 1) * gdim)
        s = stage_ref[POOL_HALO:POOL_HALO + tm, cols]
        for j in range(1, win):
            s = s + stage_ref[POOL_HALO - j:POOL_HALO - j + tm, cols]
        cnt = jnp.minimum(pos + 1, win).astype(F32)
        zg = s / cnt - x[:, cols]
        parts.append(_dot(zg, pw_ref[gi]))
    zg = jnp.concatenate(parts, axis=-1) * ps_ref[...]
    mix = _dot(zg, w_ref[...])
    h = _layer_norm(DN_ALPHA * x + mix, g_ref[...], b_ref[...])
    h_ref[...] = h
    _store_token_tiles(ht_ref, (), h)


def _pool_ln(x, x_map, halo_arr, halo_map, n_tiles, tm, pos0, zero_first_halo,
             pw_bf16, ps_row, w_bf16, g_row, b_row):
    d = x.shape[1]
    gdim = d // len(POOL_WINDOWS)
    fixed = lambda i: (0, 0)
    return pl.pallas_call(
        functools.partial(_pool_ln_kernel, tm=tm, pos0=pos0, zero_first_halo=zero_first_halo),
        grid=(n_tiles,),
        in_specs=[pl.BlockSpec((tm, d), x_map),
                  pl.BlockSpec((1, POOL_HALO, d), halo_map),
                  pl.BlockSpec((len(POOL_WINDOWS), gdim, gdim), lambda i: (0, 0, 0)),
                  pl.BlockSpec((1, d), fixed), pl.BlockSpec((d, d), fixed),
                  pl.BlockSpec((1, d), fixed), pl.BlockSpec((1, d), fixed)],
        out_specs=[pl.BlockSpec((tm, d), lambda i: (i, 0)),
                   pl.BlockSpec((tm * SUBLANES, LANES), lambda i: (i, 0))],
        out_shape=[jax.ShapeDtypeStruct((n_tiles * tm, d), F32),
                   jax.ShapeDtypeStruct((n_tiles * tm * SUBLANES, LANES), F32)],
        scratch_shapes=[pltpu.VMEM((POOL_HALO + tm, d), F32)],
        compiler_params=_cparams(("arbitrary",)),
        name="pool_mixer_ln",
    )(x, halo_arr, pw_bf16, ps_row, w_bf16, g_row, b_row)


def _router_kernel(h_ref, wr_ref, br_ref, idx_ref, gate_ref, rank_ref, cnt_ref, carry_ref):
    @pl.when(pl.program_id(0) == 0)
    def _():
        carry_ref[...] = jnp.zeros_like(carry_ref)

    tm = h_ref.shape[0]
    lane = lax.broadcasted_iota(I32, (tm, LANES), 1).astype(F32)
    logits = jnp.dot(h_ref[...], wr_ref[...], precision=HIGHEST, preferred_element_type=F32)
    work = jnp.where(lane < N_EXPERTS, logits + br_ref[...], -jnp.inf)
    vals, ids = [], []
    for _ in range(TOP_K):
        m = jnp.max(work, axis=-1, keepdims=True)
        ik = jnp.min(jnp.where(work == m, lane, float(LANES)), axis=-1, keepdims=True)
        vals.append(m)
        ids.append(ik)
        work = jnp.where(lane == ik, -jnp.inf, work)
    exps = [jnp.exp(v - vals[0]) for v in vals]
    denom = exps[0]
    for e in exps[1:]:
        denom = denom + e
    multihot = jnp.zeros((tm, LANES), F32)
    idx_out = jnp.zeros((tm, LANES), F32)
    gate_out = jnp.zeros((tm, LANES), F32)
    for k in range(TOP_K):
        multihot = multihot + (lane == ids[k]).astype(F32)
        idx_out = jnp.where(lane == k, ids[k], idx_out)
        gate_out = jnp.where(lane == k, exps[k] / denom, gate_out)
    r = lax.broadcasted_iota(I32, (tm, tm), 0)
    c = lax.broadcasted_iota(I32, (tm, tm), 1)
    before = _dot((c < r).astype(F32), multihot) + carry_ref[0:1, :]
    rank_out = jnp.zeros((tm, LANES), F32)
    for k in range(TOP_K):
        rk = jnp.sum(jnp.where(lane == ids[k], before, 0.0), axis=-1, keepdims=True)
        rank_out = jnp.where(lane == k, rk, rank_out)
    idx_ref[...] = idx_out.astype(I32)
    gate_ref[...] = gate_out
    rank_ref[...] = rank_out.astype(I32)
    total = carry_ref[0:1, :] + jnp.sum(multihot, axis=0, keepdims=True)
    carry_ref[...] = jnp.broadcast_to(total, carry_ref.shape)
    cnt_ref[...] = jnp.broadcast_to(total, cnt_ref.shape).astype(I32)


def _router(h, wr_pad, br_row):
    n, d = h.shape
    row = lambda i: (i, 0)
    fixed = lambda i: (0, 0)
    return pl.pallas_call(
        _router_kernel,
        grid=(n // TM_TOK,),
        in_specs=[pl.BlockSpec((TM_TOK, d), row), pl.BlockSpec((d, LANES), fixed),
                  pl.BlockSpec((1, LANES), fixed)],
        out_specs=[pl.BlockSpec((TM_TOK, LANES), row), pl.BlockSpec((TM_TOK, LANES), row),
                   pl.BlockSpec((TM_TOK, LANES), row), pl.BlockSpec((SUBLANES, LANES), fixed)],
        out_shape=[jax.ShapeDtypeStruct((n, LANES), I32), jax.ShapeDtypeStruct((n, LANES), F32),
                   jax.ShapeDtypeStruct((n, LANES), I32), jax.ShapeDtypeStruct((SUBLANES, LANES), I32)],
        scratch_shapes=[pltpu.VMEM((SUBLANES, LANES), F32)],
        compiler_params=_cparams(("arbitrary",)),
        name="moe_router",
    )(h, wr_pad, br_row)


def _tile_copy(src_ref, src_row, dst_ref, dst_row, sem):
    rows = lambda r: pl.ds(pl.multiple_of(r, SUBLANES), SUBLANES)
    return pltpu.make_async_copy(src_ref.at[rows(src_row)], dst_ref.at[rows(dst_row)], sem)


def _tiles_of(ref, n_tok, sem):
    span = ref.at[pl.ds(0, n_tok * SUBLANES)]
    return pltpu.make_async_copy(span, span, sem)


def _dispatch_kernel(slot_ref, pend_ref, ht_ref, xb_ref, zero_ref, sem, zsem):
    tm = ht_ref.shape[0] // SUBLANES
    base = pl.program_id(0) * tm * TOP_K
    n_blocks = xb_ref.shape[0] // (MOE_BLOCK * SUBLANES)

    @pl.when(pl.program_id(0) == 0)
    def _():
        zero_ref[...] = jnp.zeros_like(zero_ref)
        block = lambda b: pltpu.make_async_copy(
            zero_ref, xb_ref.at[pl.ds(pl.multiple_of(b * (MOE_BLOCK * SUBLANES), SUBLANES),
                                      MOE_BLOCK * SUBLANES)], zsem)
        n_used = pend_ref[N_EXPERTS - 1] // MOE_BLOCK
        group_last = lambda e: jnp.maximum(pend_ref[e] // MOE_BLOCK - 1, 0)
        for e in range(N_EXPERTS):
            block(group_last(e)).start()
        lax.fori_loop(n_used, n_blocks, lambda b, c: (block(b).start(), c)[1], 0)
        for e in range(N_EXPERTS):
            block(group_last(e)).wait()
        lax.fori_loop(n_used, n_blocks, lambda b, c: (block(b).wait(), c)[1], 0)

    for r in range(tm):
        for k in range(TOP_K):
            _tile_copy(ht_ref, r * SUBLANES, xb_ref, slot_ref[base + r * TOP_K + k] * SUBLANES, sem).start()
    _tiles_of(xb_ref, tm * TOP_K, sem).wait()


def _dispatch(ht, slot_flat, pend, n_slots):
    n = ht.shape[0] // SUBLANES
    return pl.pallas_call(
        _dispatch_kernel,
        grid_spec=pltpu.PrefetchScalarGridSpec(
            num_scalar_prefetch=2,
            grid=(n // TM_ROW,),
            in_specs=[pl.BlockSpec((TM_ROW * SUBLANES, LANES), lambda i, s, p: (i, 0))],
            out_specs=pl.BlockSpec(memory_space=pl.ANY),
            scratch_shapes=[pltpu.VMEM((MOE_BLOCK * SUBLANES, LANES), F32), pltpu.SemaphoreType.DMA(()),
                            pltpu.SemaphoreType.DMA(())]),
        out_shape=jax.ShapeDtypeStruct((n_slots * SUBLANES, LANES), F32),
        compiler_params=_cparams(("arbitrary",)),
        name="moe_dispatch",
    )(slot_flat, pend, ht)


def _expert_kernel(be_ref, nu_ref, xb_ref, wup_ref, bup_ref, wdn_ref, bdn_ref, yb_ref,
                   wup_bf_ref, wdn_bf_ref):
    b = pl.program_id(0)

    @pl.when(b < nu_ref[0])
    def _():
        @pl.when((b == 0) | (be_ref[b] != be_ref[jnp.maximum(b - 1, 0)]))
        def _():
            wup_bf_ref[...] = wup_ref[0, 0].astype(BF16)
            wdn_bf_ref[...] = wdn_ref[0, 0].astype(BF16)

        d_exp = wdn_ref.shape[2]
        x = _load_token_tiles(xb_ref, (), MOE_BLOCK)
        hu = jnp.dot(x.astype(BF16), wup_bf_ref[...], preferred_element_type=F32) + bup_ref[0, 0]
        glu = jnp.minimum(hu[:, :d_exp], SWIGLU_LIMIT)
        lin = jnp.clip(hu[:, d_exp:], -SWIGLU_LIMIT, SWIGLU_LIMIT)
        a = glu * _sigmoid(SWIGLU_ALPHA * glu) * (lin + 1.0)
        y = jnp.dot(a.astype(BF16), wdn_bf_ref[...], preferred_element_type=F32) + bdn_ref[0, 0]
        _store_token_tiles(yb_ref, (), y)

    @pl.when(b >= nu_ref[0])
    def _():
        yb_ref[...] = jnp.zeros_like(yb_ref)


def _experts(xb, block_e, n_used, layer, w_up, b_up, w_dn, b_dn):
    d = SUBLANES * LANES
    rows = MOE_BLOCK * SUBLANES
    n_blocks = xb.shape[0] // rows
    d_up = w_up.shape[3]
    d_exp = w_dn.shape[2]
    wsel = lambda b, be, nu: (layer, be[b], 0, 0)
    return pl.pallas_call(
        _expert_kernel,
        grid_spec=pltpu.PrefetchScalarGridSpec(
            num_scalar_prefetch=2,
            grid=(n_blocks,),
            in_specs=[pl.BlockSpec((rows, LANES), lambda b, be, nu: (jnp.minimum(b, nu[0] - 1), 0)),
                      pl.BlockSpec((1, 1, d, d_up), wsel),
                      pl.BlockSpec((1, 1, 1, d_up), wsel),
                      pl.BlockSpec((1, 1, d_exp, d), wsel),
                      pl.BlockSpec((1, 1, 1, d), wsel)],
            out_specs=pl.BlockSpec((rows, LANES), lambda b, be, nu: (b, 0)),
            scratch_shapes=[pltpu.VMEM((d, d_up), BF16), pltpu.VMEM((d_exp, d), BF16)]),
        out_shape=jax.ShapeDtypeStruct(xb.shape, F32),
        compiler_params=_cparams(("arbitrary",)),
        name="moe_experts",
    )(block_e, n_used, xb, w_up, b_up, w_dn, b_dn)


def _combine_kernel(slot_ref, h_ref, gate_ref, p_ref, wpg_ref, wpp_ref, g_ref, b_ref, yb_ref,
                    out_ref, buf_ref, sem):
    tm = h_ref.shape[0]
    base = pl.program_id(0) * tm * TOP_K
    for r in range(tm):
        for k in range(TOP_K):
            _tile_copy(yb_ref, slot_ref[base + r * TOP_K + k] * SUBLANES, buf_ref.at[k], r * SUBLANES,
                       sem).start()
    for k in range(TOP_K):
        _tiles_of(buf_ref.at[k], tm, sem).wait()
    gate = gate_ref[...]
    moe = _load_token_tiles(buf_ref, (0,), tm) * gate[:, 0:1]
    for k in range(1, TOP_K):
        moe = moe + _load_token_tiles(buf_ref, (k,), tm) * gate[:, k:k + 1]
    h2 = _layer_norm(DN_ALPHA * h_ref[...] + moe, g_ref[...], b_ref[...])
    out_ref[...] = h2 + _sigmoid(_dot(h2, wpg_ref[...])) * _dot(p_ref[...], wpp_ref[...])


def _combine(h, gate, p, wpg_bf16, wpp_bf16, g_row, b_row, yb, slot_flat):
    n, d = h.shape
    row = lambda i, s: (i, 0)
    fixed = lambda i, s: (0, 0)
    return pl.pallas_call(
        _combine_kernel,
        grid_spec=pltpu.PrefetchScalarGridSpec(
            num_scalar_prefetch=1,
            grid=(n // TM_ROW,),
            in_specs=[pl.BlockSpec((TM_ROW, d), row), pl.BlockSpec((TM_ROW, LANES), row),
                      pl.BlockSpec((TM_ROW, p.shape[1]), row), pl.BlockSpec((d, d), fixed),
                      pl.BlockSpec((p.shape[1], d), fixed), pl.BlockSpec((1, d), fixed),
                      pl.BlockSpec((1, d), fixed), pl.BlockSpec(memory_space=pl.ANY)],
            out_specs=pl.BlockSpec((TM_ROW, d), row),
            scratch_shapes=[pltpu.VMEM((TOP_K, TM_ROW * SUBLANES, LANES), F32),
                            pltpu.SemaphoreType.DMA(())]),
        out_shape=jax.ShapeDtypeStruct((n, d), F32),
        compiler_params=_cparams(("arbitrary",)),
        name="moe_combine_ln_embed",
    )(slot_flat, h, gate, p, wpg_bf16, wpp_bf16, g_row, b_row, yb)


def _layer_tail(h, ht, p, layer, g2, b2, w_r, b_r, w_up, b_up, w_dn, b_dn, w_pg, w_pp):
    n, d = h.shape
    wr_pad = jnp.pad(w_r, ((0, 0), (0, LANES - N_EXPERTS)))
    br_row = jnp.pad(b_r, (0, LANES - N_EXPERTS))[None]
    idx, gate, rank, cnt = _router(h, wr_pad, br_row)
    counts = cnt[0, :N_EXPERTS]
    padded = (counts + MOE_BLOCK - 1) // MOE_BLOCK * MOE_BLOCK
    pend = jnp.cumsum(padded).astype(I32)
    pstart = pend - padded
    slot = (pstart[idx[:, :TOP_K]] + rank[:, :TOP_K]).reshape(-1).astype(I32)
    n_asg = n * TOP_K
    n_blocks = n_asg // MOE_BLOCK + N_EXPERTS
    assert n_asg % MOE_BLOCK == 0 and n % TM_ROW == 0
    n_slots = n_blocks * MOE_BLOCK
    n_used = pend[-1] // MOE_BLOCK
    blk = jnp.minimum(jnp.arange(n_blocks, dtype=I32), n_used - 1) * MOE_BLOCK
    block_e = jnp.minimum(jnp.sum((pend[None, :] <= blk[:, None]).astype(I32), axis=1), N_EXPERTS - 1)
    xb = _dispatch(ht, slot, pend, n_slots)
    yb = _experts(xb, block_e, n_used[None], layer, w_up, b_up[:, :, None, :], w_dn, b_dn[:, :, None, :])
    return _combine(h, gate, p, w_pg.astype(BF16), w_pp.astype(BF16), g2[None], b2[None], yb, slot)


def _lane_row(v, lane0):
    return jnp.zeros((1, LANES), F32).at[0, lane0:lane0 + v.shape[0]].set(v.astype(F32))


def kernel(x_prompt, x_sample, cache_fox_k, cache_fox_v, cache_fox_logf, state_gdn, state_gdn_conv,
           cache_pool, p_prompt, p_sample, w_in_ab, b_fgate, gdn_a_log, gdn_dt_bias, gdn_conv_w,
           gdn_norm_g, w_out_ab, pool_w, pool_scale, w_out_pool, ln1_g, ln1_b, ln2_g, ln2_b,
           w_router, b_router, w_expert_up, b_expert_up, w_expert_down, b_expert_down,
           w_ple_gate, w_ple_proj):
    n_pb, seq, d = x_prompt.shape
    n_sb, dseq, _ = x_sample.shape
    past = cache_fox_k.shape[2]
    assert n_pb == 1 and dseq == CHUNK and past % dseq == 0 and seq % TQ == 0
    assert d == SUBLANES * LANES
    n_p = n_pb * seq
    n_s = n_sb * dseq
    n = n_p + n_s
    x = jnp.concatenate([x_prompt.reshape(n_p, d), x_sample.reshape(n_s, d)], axis=0)

    def tail(h, ht, i):
        p = jnp.concatenate([p_prompt[i].reshape(n_p, -1), p_sample[i].reshape(n_s, -1)], axis=0)
        return _layer_tail(h, ht, p, i, ln2_g[i], ln2_b[i], w_router[i], b_router[i], w_expert_up,
                           b_expert_up, w_expert_down, b_expert_down, w_ple_gate[i], w_ple_proj[i])

    w_in = w_in_ab[0]
    n_small = 3 * N_HEADS
    ff0 = 3 * WIDTH
    gq0 = ff0 + N_HEADS
    ga0 = gq0 + 4 * WIDTH
    w_small = jnp.concatenate([w_in[:, ff0:gq0], w_in[:, ga0:ga0 + 2 * N_HEADS],
                               jnp.zeros((d, LANES - n_small), F32)], axis=1)
    w_all = jnp.concatenate([w_in[:, :ff0], w_in[:, gq0:ga0], w_small], axis=1)
    bf_row = _lane_row(b_fgate[0], LANE_F)
    up = _proj(x_prompt.reshape(n_p, d), w_all.astype(BF16), bf_row, U_COLS, False)
    us = _proj(x_sample.reshape(n_s, d), w_all, bf_row, LANES, True)

    u3 = up[None]
    us3 = us.reshape(n_sb, dseq, U_COLS)
    cq_p, ck_p = _cumsum(u3, 1, n_p, TK, COL_S // LANES)
    lf_s = jnp.concatenate(
        [jnp.pad(cache_fox_logf[0].astype(F32), ((0, 0), (0, 0), (0, LANES - N_HEADS))),
         us3[:, :, COL_S:]], axis=1)
    cq_s, ck_s = _cumsum(lf_s, n_sb, past + dseq, past + dseq, 0)

    of_p = _fox(u3, lambda b, i: (0, i, COL_Q // WIDTH), u3, lambda b, j: (0, j, COL_K // WIDTH),
                u3, lambda b, j: (0, j, COL_V // WIDTH), cq_p, lambda b, i: (0, i, 0),
                ck_p, lambda b, j: (0, 0, j), 1, n_p // TQ, TQ, TK, 0)
    k_all = jnp.concatenate([cache_fox_k[0].reshape(n_sb, past, WIDTH), us3[:, :, COL_K:COL_K + WIDTH]], axis=1)
    v_all = jnp.concatenate([cache_fox_v[0].reshape(n_sb, past, WIDTH), us3[:, :, COL_V:COL_V + WIDTH]], axis=1)
    of_s = _fox(us3, lambda b, i: (b, 0, COL_Q // WIDTH), k_all, lambda b, j: (b, 0, 0),
                v_all, lambda b, j: (b, 0, 0), cq_s, lambda b, i: (b, past // dseq, 0),
                ck_s, lambda b, j: (b, 0, 0), n_sb, 1, dseq, past + dseq, past, hi=True)
    o_fox = jnp.concatenate([of_p.reshape(n_p, WIDTH), of_s.reshape(n_s, WIDTH)], axis=0)

    gw = 3 * WIDTH
    conv_w = jnp.pad(gdn_conv_w[0], ((0, SUBLANES - CONV_WIDTH), (0, 0)))
    gdn_args = (conv_w, _lane_row(gdn_a_log[0], LANE_A), _lane_row(gdn_dt_bias[0], LANE_A),
                gdn_norm_g[0][None])
    n_pstep = n_p // GDN_ROWS
    ends = lambda k: (jnp.asarray((np.arange(k) == 0).astype(np.int32)),
                      jnp.asarray((np.arange(k) == k - 1).astype(np.int32)))
    og_p, st_p = _gdn(up.reshape(n_pstep, GDN_ROWS, U_COLS), False, *ends(n_pstep),
                      jnp.zeros((n_pstep,), I32), jnp.zeros((1, SUBLANES, gw), F32), gdn_args[0],
                      jnp.zeros((1, N_HEADS, HEAD_DIM, HEAD_DIM), F32), *gdn_args[1:])
    conv_past = jnp.pad(state_gdn_conv[0].astype(F32), ((0, 0), (SUBLANES - (CONV_WIDTH - 1), 0), (0, 0)))
    ones = jnp.ones((n_sb,), I32)
    og_s, st_s = _gdn(us3, True, ones, ones, jnp.arange(n_sb, dtype=I32), conv_past, gdn_args[0],
                      state_gdn[0].astype(F32), *gdn_args[1:])
    o_gdn = jnp.concatenate([og_p.reshape(n_p, WIDTH), og_s.reshape(n_s, WIDTH)], axis=0)
    assert n_p % TM_TOK == 0
    h, ht = _outproj_ln(x, o_fox, o_gdn, w_out_ab[0], n_p // TM_TOK, ln1_g[0][None], ln1_b[0][None])
    x1 = tail(h, ht, 0)

    pool_args = (pool_w[0].astype(BF16), pool_scale[0][None], w_out_pool[0].astype(BF16),
                 ln1_g[1][None], ln1_b[1][None])
    ratio = TM_TOK // POOL_HALO
    x1_halo = x1.reshape(n // POOL_HALO, POOL_HALO, d)
    h_p, ht_p = _pool_ln(x1, lambda i: (i, 0), x1_halo, lambda i: (jnp.maximum(i * ratio - 1, 0), 0, 0),
                         n_p // TM_TOK, TM_TOK, 0, True, *pool_args)
    cache16 = jnp.pad(cache_pool[0].astype(F32), ((0, 0), (POOL_HALO - POOL_STATE, 0), (0, 0)))
    h_s, ht_s = _pool_ln(x1, lambda i: (n_p // dseq + i, 0), cache16, lambda i: (i, 0, 0),
                         n_sb, dseq, past, False, *pool_args)
    x2 = tail(jnp.concatenate([h_p, h_s], axis=0), jnp.concatenate([ht_p, ht_s], axis=0), 1)

    return (x2[:n_p].reshape(n_pb, seq, d), x2[n_p:].reshape(n_sb, dseq, d),
            up[:, COL_K:COL_K + WIDTH].reshape(1, n_pb, seq, N_HEADS, HEAD_DIM),
            up[:, COL_V:COL_V + WIDTH].reshape(1, n_pb, seq, N_HEADS, HEAD_DIM),
            up[:, COL_S:COL_S + N_HEADS].reshape(1, n_pb, seq, N_HEADS),
            st_p.reshape(1, n_pb, N_HEADS, HEAD_DIM, HEAD_DIM),
            up[seq - (CONV_WIDTH - 1):, COL_G:COL_G + gw].reshape(1, n_pb, CONV_WIDTH - 1, gw),
            x1[n_p - POOL_STATE:n_p].reshape(1, n_pb, POOL_STATE, d),
            us[:, COL_K:COL_K + WIDTH].reshape(1, n_sb, dseq, N_HEADS, HEAD_DIM),
            us[:, COL_V:COL_V + WIDTH].reshape(1, n_sb, dseq, N_HEADS, HEAD_DIM),
            us[:, COL_S:COL_S + N_HEADS].reshape(1, n_sb, dseq, N_HEADS),
            st_s.reshape(1, n_sb, N_HEADS, HEAD_DIM, HEAD_DIM),
            us[:, COL_G:COL_G + gw].reshape(n_sb, dseq, gw)[:, dseq - (CONV_WIDTH - 1):].reshape(
                1, n_sb, CONV_WIDTH - 1, gw),
            x1[n_p:].reshape(n_sb, dseq, d)[:, dseq - POOL_STATE:].reshape(1, n_sb, POOL_STATE, d))
```

```python
import functools

import numpy as np
import jax
import jax.numpy as jnp
from jax import lax
from jax.experimental import pallas as pl
from jax.experimental.pallas import tpu as pltpu

F32 = jnp.float32
BF16 = jnp.bfloat16
I32 = jnp.int32
HIGHEST = lax.Precision.HIGHEST

LANES = 128
SUBLANES = 8
VMEM_LIMIT = 56 * 1024 * 1024

HEAD_DIM = 128
N_HEADS = 4
WIDTH = N_HEADS * HEAD_DIM
CHUNK = 64
CONV_WIDTH = 4
POOL_WINDOWS = (2, 4, 8, 16)
POOL_HALO = 16
POOL_STATE = 15
N_EXPERTS = 32
TOP_K = 4
SWIGLU_LIMIT = 7.0
SWIGLU_ALPHA = 1.702
DEPTH = 2
DN_ALPHA = (2 * DEPTH) ** 0.25
LN_EPS = 1e-5
NORM_EPS = 1e-6
NEG_INF = -1e30
LOG2E = 1.4426950408889634

COL_Q, COL_K, COL_V = 0, WIDTH, 2 * WIDTH
COL_G = 3 * WIDTH
COL_Z = 6 * WIDTH
COL_S = 7 * WIDTH
U_COLS = COL_S + LANES
LANE_F, LANE_A, LANE_B = 0, N_HEADS, 2 * N_HEADS

TM_PROJ = 256
TM_TOK = 512
TM_ROW = 256
MOE_BLOCK = 512
TQ = 1024
TK = 512
GDN_ROWS = 256


def _cparams(sem):
    return pltpu.CompilerParams(dimension_semantics=sem, vmem_limit_bytes=VMEM_LIMIT)


def _softplus(x):
    return jnp.maximum(x, 0.0) + jnp.log1p(jnp.exp(-jnp.abs(x)))


def _sigmoid(x):
    return 1.0 / (1.0 + jnp.exp(-x))


def _silu(x):
    return x * _sigmoid(x)


def _layer_norm(y, g, b):
    mu = jnp.mean(y, axis=-1, keepdims=True)
    yc = y - mu
    var = jnp.mean(yc * yc, axis=-1, keepdims=True)
    return yc * lax.rsqrt(var + LN_EPS) * g + b


def _dot_general(a, b, dims, hi):
    if hi:
        return lax.dot_general(a.astype(F32), b.astype(F32), (dims, ((), ())), precision=HIGHEST,
                               preferred_element_type=F32)
    return lax.dot_general(a.astype(BF16), b.astype(BF16), (dims, ((), ())), preferred_element_type=F32)


def _dot(a, b, hi=False):
    return _dot_general(a, b, ((1,), (0,)), hi)


def _dot_nt(a, b, hi=False):
    return _dot_general(a, b, ((1,), (1,)), hi)


def _dot_tn(a, b, hi=False):
    return _dot_general(a, b, ((0,), (0,)), hi)


def _spread_lanes(x, width):
    if width % LANES == 0:
        return jnp.concatenate([x] * (width // LANES), axis=1)
    return jnp.broadcast_to(x[:, 0:1], (x.shape[0], width))


def _load_token_tiles(ref, lead, n_tok):
    return jnp.concatenate([ref[(*lead, pl.ds(j, n_tok, stride=SUBLANES), slice(None))]
                            for j in range(SUBLANES)], axis=1)


def _store_token_tiles(ref, lead, x):
    for j in range(SUBLANES):
        ref[(*lead, pl.ds(j, x.shape[0], stride=SUBLANES), slice(None))] = x[:, j * LANES:(j + 1) * LANES]


def _lanes_to_rows(x, lane0):
    r = lax.broadcasted_iota(I32, (SUBLANES, LANES), 0)
    c = lax.broadcasted_iota(I32, (SUBLANES, LANES), 1)
    sel = (c == r + lane0).astype(F32)
    return lax.dot_general(sel, x, (((1,), (1,)), ((), ())), precision=HIGHEST,
                           preferred_element_type=F32)


def _proj_kernel(x_ref, w_ref, bf_ref, u_ref, *, hi):
    u = _dot(x_ref[...], w_ref[...], hi)
    u_ref[...] = u

    @pl.when(pl.program_id(1) == pl.num_programs(1) - 1)
    def _():
        small = u[:, u.shape[1] - LANES:]
        lane = lax.broadcasted_iota(I32, small.shape, 1)
        logf = -_softplus(-(small + bf_ref[...]))
        u_ref[:, u.shape[1] - LANES:] = jnp.where(lane < LANE_A, logf, small)


def _proj(x, w, bf_row, tn, hi):
    n, d = x.shape
    m = w.shape[1]
    return pl.pallas_call(
        functools.partial(_proj_kernel, hi=hi),
        grid=(n // TM_PROJ, m // tn),
        in_specs=[pl.BlockSpec((TM_PROJ, d), lambda i, j: (i, 0)),
                  pl.BlockSpec((d, tn), lambda i, j: (0, j)),
                  pl.BlockSpec((1, LANES), lambda i, j: (0, 0))],
        out_specs=pl.BlockSpec((TM_PROJ, tn), lambda i, j: (i, j)),
        out_shape=jax.ShapeDtypeStruct((n, m), F32),
        compiler_params=_cparams(("parallel", "parallel")),
        name="in_proj",
    )(x, w, bf_row)


def _cumsum_kernel(lf_ref, crep_ref, crow_ref, carry_ref):
    @pl.when(pl.program_id(1) == 0)
    def _():
        carry_ref[...] = jnp.zeros_like(carry_ref)

    lf = lf_ref[0]
    t = lf.shape[0]
    r = lax.broadcasted_iota(I32, (t, t), 0)
    c = lax.broadcasted_iota(I32, (t, t), 1)
    tril = (c <= r).astype(F32)
    cs = jnp.dot(tril, lf, precision=HIGHEST, preferred_element_type=F32) + carry_ref[0:1, :]
    carry_ref[...] = jnp.broadcast_to(cs[t - 1:t, :], carry_ref.shape)
    c2 = cs * LOG2E
    crow_ref[0] = _lanes_to_rows(c2, LANE_F)
    for h in range(N_HEADS):
        crep_ref[0, :, h * HEAD_DIM:(h + 1) * HEAD_DIM] = jnp.broadcast_to(
            c2[:, LANE_F + h:LANE_F + h + 1], (t, HEAD_DIM))


def _cumsum(arr, n_batch, length, tl, col_block):
    return pl.pallas_call(
        _cumsum_kernel,
        grid=(n_batch, length // tl),
        in_specs=[pl.BlockSpec((1, tl, LANES), lambda b, j: (b, j, col_block))],
        out_specs=[pl.BlockSpec((1, tl, WIDTH), lambda b, j: (b, j, 0)),
                   pl.BlockSpec((1, SUBLANES, tl), lambda b, j: (b, 0, j))],
        out_shape=[jax.ShapeDtypeStruct((n_batch, length, WIDTH), F32),
                   jax.ShapeDtypeStruct((n_batch, SUBLANES, length), F32)],
        scratch_shapes=[pltpu.VMEM((SUBLANES, LANES), F32)],
        compiler_params=_cparams(("parallel", "arbitrary")),
        name="logf_cumsum",
    )(arr)


def _fox_kernel(qi_ref, kj_ref, last_ref, q_ref, k_ref, v_ref, cq_ref, ck_ref, o_ref,
                m_ref, l_ref, acc_ref, *, tq, tk, past, hi):
    s_idx = pl.program_id(1)
    qi = qi_ref[s_idx]
    kj = kj_ref[s_idx]

    @pl.when(kj == 0)
    def _():
        m_ref[...] = jnp.full_like(m_ref, NEG_INF)
        l_ref[...] = jnp.zeros_like(l_ref)
        acc_ref[...] = jnp.zeros_like(acc_ref)

    def update(masked):
        if masked:
            q_pos = past + qi * tq + lax.broadcasted_iota(I32, (tq, tk), 0)
            k_pos = kj * tk + lax.broadcasted_iota(I32, (tq, tk), 1)
            visible = k_pos <= q_pos
        for h in range(N_HEADS):
            cols = slice(h * HEAD_DIM, (h + 1) * HEAD_DIM)
            q = q_ref[0, :, cols] * (HEAD_DIM ** -0.5 * LOG2E)
            t = _dot_nt(q, k_ref[0, :, cols], hi) - ck_ref[0, h:h + 1, :]
            if masked:
                t = jnp.where(visible, t, NEG_INF)
            cq = cq_ref[0, :, cols]
            m_prev = m_ref[h]
            m_new = jnp.maximum(m_prev, jnp.max(t, axis=-1, keepdims=True) + cq)
            p = jnp.exp2(t - _spread_lanes(m_new - cq, tk))
            alpha = jnp.exp2(m_prev - m_new)
            l_ref[h] = alpha * l_ref[h] + jnp.sum(p, axis=-1, keepdims=True)
            acc_ref[:, cols] = alpha * acc_ref[:, cols] + _dot(p, v_ref[0, :, cols], hi)
            m_ref[h] = m_new

    crosses_diagonal = kj * tk + (tk - 1) > past + qi * tq
    pl.when(crosses_diagonal)(functools.partial(update, True))
    pl.when(jnp.logical_not(crosses_diagonal))(functools.partial(update, False))

    @pl.when(last_ref[s_idx] == 1)
    def _():
        for h in range(N_HEADS):
            cols = slice(h * HEAD_DIM, (h + 1) * HEAD_DIM)
            o_ref[0, :, cols] = acc_ref[:, cols] / l_ref[h]


def _fox_schedule(n_q, tq, tk, past):
    qi, kj, last = [], [], []
    for i in range(n_q):
        hi = (past + (i + 1) * tq - 1) // tk
        for j in range(hi + 1):
            qi.append(i)
            kj.append(j)
            last.append(1 if j == hi else 0)
    return (jnp.asarray(np.array(qi, np.int32)), jnp.asarray(np.array(kj, np.int32)),
            jnp.asarray(np.array(last, np.int32)))


def _fox(q_arr, q_map, k_arr, k_map, v_arr, v_map, cq_arr, cq_map, ck_arr, ck_map,
         n_batch, n_q, tq, tk, past, hi=False):
    qi, kj, last = _fox_schedule(n_q, tq, tk, past)
    n_steps = int(qi.shape[0])
    spec = lambda shape, fn, tab: pl.BlockSpec(shape, lambda b, s, qi_r, kj_r, la_r: fn(b, (qi_r if tab == 'q' else kj_r)[s]))
    return pl.pallas_call(
        functools.partial(_fox_kernel, tq=tq, tk=tk, past=past, hi=hi),
        grid_spec=pltpu.PrefetchScalarGridSpec(
            num_scalar_prefetch=3,
            grid=(n_batch, n_steps),
            in_specs=[spec((1, tq, WIDTH), q_map, 'q'),
                      spec((1, tk, WIDTH), k_map, 'k'),
                      spec((1, tk, WIDTH), v_map, 'k'),
                      spec((1, tq, WIDTH), cq_map, 'q'),
                      spec((1, SUBLANES, tk), ck_map, 'k')],
            out_specs=spec((1, tq, WIDTH), lambda b, i: (b, i, 0), 'q'),
            scratch_shapes=[pltpu.VMEM((N_HEADS, tq, HEAD_DIM), F32),
                            pltpu.VMEM((N_HEADS, tq, HEAD_DIM), F32),
                            pltpu.VMEM((tq, WIDTH), F32)]),
        out_shape=jax.ShapeDtypeStruct((n_batch, n_q * tq, WIDTH), F32),
        compiler_params=_cparams(("parallel", "arbitrary")),
        name="fox_attention",
    )(qi, kj, last, q_arr, k_arr, v_arr, cq_arr, ck_arr)


def _gdn_kernel(first_ref, last_ref, seq_ref,
                pre_ref, z_ref, sm_ref, cpast_ref, convw_ref, s0_ref, alog_ref, dtb_ref, ng_ref,
                o_ref, sout_ref, stage_ref, s_ref, *, rows, hi):
    dot, dot_nt, dot_tn = (functools.partial(f, hi=hi) for f in (_dot, _dot_nt, _dot_tn))
    step = pl.program_id(0)
    halo = SUBLANES
    n_chunks = rows // CHUNK

    @pl.when(first_ref[step] == 1)
    def _():
        stage_ref[0:halo, :] = cpast_ref[0]
        s_ref[...] = s0_ref[0]

    stage_ref[halo:halo + rows, :] = pre_ref[0]
    conv = stage_ref[halo:halo + rows, :] * convw_ref[CONV_WIDTH - 1:CONV_WIDTH, :]
    for j in range(1, CONV_WIDTH):
        conv = conv + (stage_ref[halo - j:halo - j + rows, :]
                       * convw_ref[CONV_WIDTH - 1 - j:CONV_WIDTH - j, :])
    stage_ref[0:halo, :] = stage_ref[rows:rows + halo, :]
    act = _silu(conv)

    small = sm_ref[0]
    beta_all = _sigmoid(small)
    g_all = -jnp.exp(alog_ref[...]) * _softplus(small + dtb_ref[...])
    r = lax.broadcasted_iota(I32, (rows, rows), 0)
    c = lax.broadcasted_iota(I32, (rows, rows), 1)
    same_chunk = (r // CHUNK) == (c // CHUNK)
    incl = same_chunk & (c <= r)
    strict = same_chunk & (c < r)
    eye = (c == r).astype(F32)
    gc_all = jnp.dot(incl.astype(F32), g_all, precision=HIGHEST, preferred_element_type=F32)
    gc_rows = _lanes_to_rows(gc_all, LANE_A)

    for h in range(N_HEADS):
        cols = slice(h * HEAD_DIM, (h + 1) * HEAD_DIM)
        q = act[:, h * HEAD_DIM:(h + 1) * HEAD_DIM]
        k = act[:, WIDTH + h * HEAD_DIM:WIDTH + (h + 1) * HEAD_DIM]
        v = act[:, 2 * WIDTH + h * HEAD_DIM:2 * WIDTH + (h + 1) * HEAD_DIM]
        q = q * lax.rsqrt(jnp.sum(q * q, axis=-1, keepdims=True) + NORM_EPS) * (HEAD_DIM ** -0.5)
        k = k * lax.rsqrt(jnp.sum(k * k, axis=-1, keepdims=True) + NORM_EPS)
        beta = beta_all[:, LANE_B + h:LANE_B + h + 1]
        gc = gc_all[:, LANE_A + h:LANE_A + h + 1]
        diff = gc - gc_rows[h:h + 1, :]
        decay = jnp.where(incl, jnp.exp(jnp.where(incl, diff, 0.0)), 0.0)
        kb = k * beta
        vb = v * beta
        low = jnp.where(strict, dot_nt(kb, k) * decay, 0.0)
        inv = eye - low
        pw = dot(low, low)
        n_sq = CHUNK.bit_length() - 2
        for it in range(n_sq):
            inv = inv + dot(inv, pw)
            if it + 1 < n_sq:
                pw = dot(pw, pw)
        egc = jnp.exp(gc)
        uw = dot(inv, jnp.concatenate([vb, kb * egc], axis=1))
        intra = jnp.where(incl, dot_nt(q, k) * decay, 0.0)
        qd = q * egc
        g_last = [gc[(g + 1) * CHUNK - 1:(g + 1) * CHUNK, :] for g in range(n_chunks)]
        kd = k * jnp.exp(jnp.concatenate([jnp.broadcast_to(gl, (CHUNK, 1)) for gl in g_last], axis=0)
                         - gc)
        state = s_ref[h]
        v_new = []
        for g in range(n_chunks):
            rs = slice(g * CHUNK, (g + 1) * CHUNK)
            v_new.append(uw[rs, :HEAD_DIM] - dot(uw[rs, HEAD_DIM:], state))
            v_rows = jnp.concatenate(
                v_new + [jnp.zeros((rows - (g + 1) * CHUNK, HEAD_DIM), F32)] * (g + 1 < n_chunks), axis=0)
            o = dot(qd[rs, :], state) + dot(intra[rs, :], v_rows)
            state = state * jnp.exp(g_last[g]) + dot_tn(kd[rs, :], v_new[g])
            o = (o * lax.rsqrt(jnp.mean(o * o, axis=-1, keepdims=True) + NORM_EPS)
                 * ng_ref[...] * _silu(z_ref[0, rs, cols]))
            o_ref[0, rs, cols] = o
        s_ref[h] = state

    @pl.when(last_ref[step] == 1)
    def _():
        sout_ref[0] = s_ref[...]


def _gdn(u_view, hi, first, last, seq, conv_past, conv_w, s0, alog_row, dtb_row, ng_row):
    n_steps, rows, _ = u_view.shape
    n_seq = s0.shape[0]
    gw = 3 * WIDTH
    return pl.pallas_call(
        functools.partial(_gdn_kernel, rows=rows, hi=hi),
        grid_spec=pltpu.PrefetchScalarGridSpec(
            num_scalar_prefetch=3,
            grid=(n_steps,),
            in_specs=[pl.BlockSpec((1, rows, gw), lambda s, f, l, q: (s, 0, COL_G // gw)),
                      pl.BlockSpec((1, rows, WIDTH), lambda s, f, l, q: (s, 0, COL_Z // WIDTH)),
                      pl.BlockSpec((1, rows, LANES), lambda s, f, l, q: (s, 0, COL_S // LANES)),
                      pl.BlockSpec((1, SUBLANES, gw), lambda s, f, l, q: (q[s], 0, 0)),
                      pl.BlockSpec((SUBLANES, gw), lambda s, f, l, q: (0, 0)),
                      pl.BlockSpec((1, N_HEADS, HEAD_DIM, HEAD_DIM), lambda s, f, l, q: (q[s], 0, 0, 0)),
                      pl.BlockSpec((1, LANES), lambda s, f, l, q: (0, 0)),
                      pl.BlockSpec((1, LANES), lambda s, f, l, q: (0, 0)),
                      pl.BlockSpec((1, LANES), lambda s, f, l, q: (0, 0))],
            out_specs=[pl.BlockSpec((1, rows, WIDTH), lambda s, f, l, q: (s, 0, 0)),
                       pl.BlockSpec((1, N_HEADS, HEAD_DIM, HEAD_DIM), lambda s, f, l, q: (q[s], 0, 0, 0))],
            scratch_shapes=[pltpu.VMEM((rows + SUBLANES, gw), F32),
                            pltpu.VMEM((N_HEADS, HEAD_DIM, HEAD_DIM), F32)]),
        out_shape=[jax.ShapeDtypeStruct((n_steps, rows, WIDTH), F32),
                   jax.ShapeDtypeStruct((n_seq, N_HEADS, HEAD_DIM, HEAD_DIM), F32)],
        compiler_params=_cparams(("arbitrary",)),
        name="gated_deltanet",
    )(first, last, seq, u_view, u_view, u_view, conv_past, conv_w, s0, alog_row, dtb_row, ng_row)


def _outproj_ln_kernel(xp_ref, ofp_ref, ogp_ref, xs_ref, ofs_ref, ogs_ref, w_ref, wf_ref, g_ref, b_ref,
                       h_ref, ht_ref, *, hi_from):
    def run(hi, x_ref, of_ref, og_ref):
        w = wf_ref if hi else w_ref
        mix = _dot(of_ref[...], w[0:WIDTH, :], hi) + _dot(og_ref[...], w[WIDTH:2 * WIDTH, :], hi)
        h = _layer_norm(DN_ALPHA * x_ref[...] + mix, g_ref[...], b_ref[...])
        h_ref[...] = h
        _store_token_tiles(ht_ref, (), h)

    pl.when(pl.program_id(0) < hi_from)(functools.partial(run, False, xp_ref, ofp_ref, ogp_ref))
    pl.when(pl.program_id(0) >= hi_from)(functools.partial(run, True, xs_ref, ofs_ref, ogs_ref))


def _outproj_ln(prompt, sample, w, g_row, b_row):
    n_p, d = prompt[0].shape
    n = n_p + sample[0].shape[0]
    hi_from = n_p // TM_TOK
    row = lambda i: (i, 0)
    row_p = lambda i: (jnp.minimum(i, hi_from - 1), 0)
    row_s = lambda i: (jnp.maximum(i - hi_from, 0), 0)
    fixed = lambda i: (0, 0)
    group = lambda rows: [pl.BlockSpec((TM_TOK, d), rows), pl.BlockSpec((TM_TOK, WIDTH), rows),
                          pl.BlockSpec((TM_TOK, WIDTH), rows)]
    return pl.pallas_call(
        functools.partial(_outproj_ln_kernel, hi_from=hi_from),
        grid=(n // TM_TOK,),
        in_specs=group(row_p) + group(row_s) + [pl.BlockSpec((2 * WIDTH, d), fixed),
                                                pl.BlockSpec((2 * WIDTH, d), fixed),
                                                pl.BlockSpec((1, d), fixed), pl.BlockSpec((1, d), fixed)],
        out_specs=[pl.BlockSpec((TM_TOK, d), row), pl.BlockSpec((TM_TOK * SUBLANES, LANES), row)],
        out_shape=[jax.ShapeDtypeStruct((n, d), F32), jax.ShapeDtypeStruct((n * SUBLANES, LANES), F32)],
        compiler_params=_cparams(("parallel",)),
        name="out_proj_ln",
    )(*prompt, *sample, w.astype(BF16), w, g_row, b_row)


def _pool_ln_kernel(x_ref, halo_ref, pw_ref, ps_ref, w_ref, g_ref, b_ref, h_ref, ht_ref, stage_ref,
                    *, tm, pos0, zero_first_halo):
    i = pl.program_id(0)
    stage_ref[0:POOL_HALO, :] = halo_ref[0]
    if zero_first_halo:
        @pl.when(i == 0)
        def _():
            stage_ref[0:POOL_HALO, :] = jnp.zeros((POOL_HALO, stage_ref.shape[1]), F32)
    x = x_ref[...]
    stage_ref[POOL_HALO:POOL_HALO + tm, :] = x
    gdim = x.shape[1] // len(POOL_WINDOWS)
    pos = pos0 + lax.broadcasted_iota(I32, (tm, 1), 0)
    if zero_first_halo:
        pos = pos + i * tm
    parts = []
    for gi, win in enumerate(POOL_WINDOWS):
        cols = slice(gi * gdim, (gi + 1) * gdim)
        s = stage_ref[POOL_HALO:POOL_HALO + tm, cols]
        for j in range(1, win):
            s = s + stage_ref[POOL_HALO - j:POOL_HALO - j + tm, cols]
        cnt = jnp.minimum(pos + 1, win).astype(F32)
        zg = s / cnt - x[:, cols]
        parts.append(_dot(zg, pw_ref[gi]))
    zg = jnp.concatenate(parts, axis=-1) * ps_ref[...]
    mix = _dot(zg, w_ref[...])
    h = _layer_norm(DN_ALPHA * x + mix, g_ref[...], b_ref[...])
    h_ref[...] = h
    _store_token_tiles(ht_ref, (), h)


def _pool_ln(x, x_map, halo_arr, halo_map, n_tiles, tm, pos0, zero_first_halo,
             pw_bf16, ps_row, w_bf16, g_row, b_row):
    d = x.shape[1]
    gdim = d // len(POOL_WINDOWS)
    fixed = lambda i: (0, 0)
    return pl.pallas_call(
        functools.partial(_pool_ln_kernel, tm=tm, pos0=pos0, zero_first_halo=zero_first_halo),
        grid=(n_tiles,),
        in_specs=[pl.BlockSpec((tm, d), x_map),
                  pl.BlockSpec((1, POOL_HALO, d), halo_map),
                  pl.BlockSpec((len(POOL_WINDOWS), gdim, gdim), lambda i: (0, 0, 0)),
                  pl.BlockSpec((1, d), fixed), pl.BlockSpec((d, d), fixed),
                  pl.BlockSpec((1, d), fixed), pl.BlockSpec((1, d), fixed)],
        out_specs=[pl.BlockSpec((tm, d), lambda i: (i, 0)),
                   pl.BlockSpec((tm * SUBLANES, LANES), lambda i: (i, 0))],
        out_shape=[jax.ShapeDtypeStruct((n_tiles * tm, d), F32),
                   jax.ShapeDtypeStruct((n_tiles * tm * SUBLANES, LANES), F32)],
        scratch_shapes=[pltpu.VMEM((POOL_HALO + tm, d), F32)],
        compiler_params=_cparams(("arbitrary",)),
        name="pool_mixer_ln",
    )(x, halo_arr, pw_bf16, ps_row, w_bf16, g_row, b_row)


def _router_kernel(h_ref, wr_ref, br_ref, idx_ref, gate_ref, rank_ref, cnt_ref, carry_ref):
    @pl.when(pl.program_id(0) == 0)
    def _():
        carry_ref[...] = jnp.zeros_like(carry_ref)

    tm = h_ref.shape[0]
    lane = lax.broadcasted_iota(I32, (tm, LANES), 1).astype(F32)
    logits = jnp.dot(h_ref[...], wr_ref[...], precision=HIGHEST, preferred_element_type=F32)
    work = jnp.where(lane < N_EXPERTS, logits + br_ref[...], -jnp.inf)
    vals, ids = [], []
    for _ in range(TOP_K):
        m = jnp.max(work, axis=-1, keepdims=True)
        ik = jnp.min(jnp.where(work == m, lane, float(LANES)), axis=-1, keepdims=True)
        vals.append(m)
        ids.append(ik)
        work = jnp.where(lane == ik, -jnp.inf, work)
    exps = [jnp.exp(v - vals[0]) for v in vals]
    denom = exps[0]
    for e in exps[1:]:
        denom = denom + e
    multihot = jnp.zeros((tm, LANES), F32)
    idx_out = jnp.zeros((tm, LANES), F32)
    gate_out = jnp.zeros((tm, LANES), F32)
    for k in range(TOP_K):
        multihot = multihot + (lane == ids[k]).astype(F32)
        idx_out = jnp.where(lane == k, ids[k], idx_out)
        gate_out = jnp.where(lane == k, exps[k] / denom, gate_out)
    r = lax.broadcasted_iota(I32, (tm, tm), 0)
    c = lax.broadcasted_iota(I32, (tm, tm), 1)
    before = _dot((c < r).astype(F32), multihot) + carry_ref[0:1, :]
    rank_out = jnp.zeros((tm, LANES), F32)
    for k in range(TOP_K):
        rk = jnp.sum(jnp.where(lane == ids[k], before, 0.0), axis=-1, keepdims=True)
        rank_out = jnp.where(lane == k, rk, rank_out)
    idx_ref[...] = idx_out.astype(I32)
    gate_ref[...] = gate_out
    rank_ref[...] = rank_out.astype(I32)
    total = carry_ref[0:1, :] + jnp.sum(multihot, axis=0, keepdims=True)
    carry_ref[...] = jnp.broadcast_to(total, carry_ref.shape)
    cnt_ref[...] = jnp.broadcast_to(total, cnt_ref.shape).astype(I32)


def _router(h, wr_pad, br_row):
    n, d = h.shape
    row = lambda i: (i, 0)
    fixed = lambda i: (0, 0)
    return pl.pallas_call(
        _router_kernel,
        grid=(n // TM_TOK,),
        in_specs=[pl.BlockSpec((TM_TOK, d), row), pl.BlockSpec((d, LANES), fixed),
                  pl.BlockSpec((1, LANES), fixed)],
        out_specs=[pl.BlockSpec((TM_TOK, LANES), row), pl.BlockSpec((TM_TOK, LANES), row),
                   pl.BlockSpec((TM_TOK, LANES), row), pl.BlockSpec((SUBLANES, LANES), fixed)],
        out_shape=[jax.ShapeDtypeStruct((n, LANES), I32), jax.ShapeDtypeStruct((n, LANES), F32),
                   jax.ShapeDtypeStruct((n, LANES), I32), jax.ShapeDtypeStruct((SUBLANES, LANES), I32)],
        scratch_shapes=[pltpu.VMEM((SUBLANES, LANES), F32)],
        compiler_params=_cparams(("arbitrary",)),
        name="moe_router",
    )(h, wr_pad, br_row)


def _tile_copy(src_ref, src_row, dst_ref, dst_row, sem):
    rows = lambda r: pl.ds(pl.multiple_of(r, SUBLANES), SUBLANES)
    return pltpu.make_async_copy(src_ref.at[rows(src_row)], dst_ref.at[rows(dst_row)], sem)


def _tiles_of(ref, n_tok, sem):
    span = ref.at[pl.ds(0, n_tok * SUBLANES)]
    return pltpu.make_async_copy(span, span, sem)


def _dispatch_kernel(slot_ref, pend_ref, ht_ref, xb_ref, zero_ref, sem, zsem):
    tm = ht_ref.shape[0] // SUBLANES
    base = pl.program_id(0) * tm * TOP_K
    n_blocks = xb_ref.shape[0] // (MOE_BLOCK * SUBLANES)

    @pl.when(pl.program_id(0) == 0)
    def _():
        zero_ref[...] = jnp.zeros_like(zero_ref)
        block = lambda b: pltpu.make_async_copy(
            zero_ref, xb_ref.at[pl.ds(pl.multiple_of(b * (MOE_BLOCK * SUBLANES), SUBLANES),
                                      MOE_BLOCK * SUBLANES)], zsem)
        n_used = pend_ref[N_EXPERTS - 1] // MOE_BLOCK
        group_last = lambda e: jnp.maximum(pend_ref[e] // MOE_BLOCK - 1, 0)
        for e in range(N_EXPERTS):
            block(group_last(e)).start()
        lax.fori_loop(n_used, n_blocks, lambda b, c: (block(b).start(), c)[1], 0)
        for e in range(N_EXPERTS):
            block(group_last(e)).wait()
        lax.fori_loop(n_used, n_blocks, lambda b, c: (block(b).wait(), c)[1], 0)

    for r in range(tm):
        for k in range(TOP_K):
            _tile_copy(ht_ref, r * SUBLANES, xb_ref, slot_ref[base + r * TOP_K + k] * SUBLANES, sem).start()
    _tiles_of(xb_ref, tm * TOP_K, sem).wait()


def _dispatch(ht, slot_flat, pend, n_slots):
    n = ht.shape[0] // SUBLANES
    return pl.pallas_call(
        _dispatch_kernel,
        grid_spec=pltpu.PrefetchScalarGridSpec(
            num_scalar_prefetch=2,
            grid=(n // TM_ROW,),
            in_specs=[pl.BlockSpec((TM_ROW * SUBLANES, LANES), lambda i, s, p: (i, 0))],
            out_specs=pl.BlockSpec(memory_space=pl.ANY),
            scratch_shapes=[pltpu.VMEM((MOE_BLOCK * SUBLANES, LANES), F32), pltpu.SemaphoreType.DMA(()),
                            pltpu.SemaphoreType.DMA(())]),
        out_shape=jax.ShapeDtypeStruct((n_slots * SUBLANES, LANES), F32),
        compiler_params=_cparams(("arbitrary",)),
        name="moe_dispatch",
    )(slot_flat, pend, ht)


def _expert_kernel(be_ref, nu_ref, xb_ref, wup_ref, bup_ref, wdn_ref, bdn_ref, yb_ref,
                   wup_bf_ref, wdn_bf_ref):
    b = pl.program_id(0)

    @pl.when(b < nu_ref[0])
    def _():
        @pl.when((b == 0) | (be_ref[b] != be_ref[jnp.maximum(b - 1, 0)]))
        def _():
            wup_bf_ref[...] = wup_ref[0, 0].astype(BF16)
            wdn_bf_ref[...] = wdn_ref[0, 0].astype(BF16)

        d_exp = wdn_ref.shape[2]
        x = _load_token_tiles(xb_ref, (), MOE_BLOCK)
        hu = jnp.dot(x.astype(BF16), wup_bf_ref[...], preferred_element_type=F32) + bup_ref[0, 0]
        glu = jnp.minimum(hu[:, :d_exp], SWIGLU_LIMIT)
        lin = jnp.clip(hu[:, d_exp:], -SWIGLU_LIMIT, SWIGLU_LIMIT)
        a = glu * _sigmoid(SWIGLU_ALPHA * glu) * (lin + 1.0)
        y = jnp.dot(a.astype(BF16), wdn_bf_ref[...], preferred_element_type=F32) + bdn_ref[0, 0]
        _store_token_tiles(yb_ref, (), y)

    @pl.when(b >= nu_ref[0])
    def _():
        yb_ref[...] = jnp.zeros_like(yb_ref)


def _experts(xb, block_e, n_used, layer, w_up, b_up, w_dn, b_dn):
    d = SUBLANES * LANES
    rows = MOE_BLOCK * SUBLANES
    n_blocks = xb.shape[0] // rows
    d_up = w_up.shape[3]
    d_exp = w_dn.shape[2]
    wsel = lambda b, be, nu: (layer, be[b], 0, 0)
    return pl.pallas_call(
        _expert_kernel,
        grid_spec=pltpu.PrefetchScalarGridSpec(
            num_scalar_prefetch=2,
            grid=(n_blocks,),
            in_specs=[pl.BlockSpec((rows, LANES), lambda b, be, nu: (jnp.minimum(b, nu[0] - 1), 0)),
                      pl.BlockSpec((1, 1, d, d_up), wsel),
                      pl.BlockSpec((1, 1, 1, d_up), wsel),
                      pl.BlockSpec((1, 1, d_exp, d), wsel),
                      pl.BlockSpec((1, 1, 1, d), wsel)],
            out_specs=pl.BlockSpec((rows, LANES), lambda b, be, nu: (b, 0)),
            scratch_shapes=[pltpu.VMEM((d, d_up), BF16), pltpu.VMEM((d_exp, d), BF16)]),
        out_shape=jax.ShapeDtypeStruct(xb.shape, F32),
        compiler_params=_cparams(("arbitrary",)),
        name="moe_experts",
    )(block_e, n_used, xb, w_up, b_up, w_dn, b_dn)


def _combine_kernel(slot_ref, h_ref, gate_ref, pp_ref, ps_ref, wpg_ref, wpp_ref, g_ref, b_ref, yb_ref,
                    *refs, n_ptiles, split):
    outp_ref, outs_ref = (refs[0], refs[1]) if split else (refs[0], refs[0])
    buf_ref, sem = refs[-2:]
    tm = h_ref.shape[0]
    base = pl.program_id(0) * tm * TOP_K
    for r in range(tm):
        for k in range(TOP_K):
            _tile_copy(yb_ref, slot_ref[base + r * TOP_K + k] * SUBLANES, buf_ref.at[k], r * SUBLANES,
                       sem).start()
    for k in range(TOP_K):
        _tiles_of(buf_ref.at[k], tm, sem).wait()
    gate = gate_ref[...]
    moe = _load_token_tiles(buf_ref, (0,), tm) * gate[:, 0:1]
    for k in range(1, TOP_K):
        moe = moe + _load_token_tiles(buf_ref, (k,), tm) * gate[:, k:k + 1]
    h2 = _layer_norm(DN_ALPHA * h_ref[...] + moe, g_ref[...], b_ref[...])
    embed_gate = _sigmoid(_dot(h2, wpg_ref[...]))

    def finish(p_ref, out_ref):
        out_ref[...] = h2 + embed_gate * _dot(p_ref[...], wpp_ref[...])

    is_prompt = pl.program_id(0) < n_ptiles
    pl.when(is_prompt)(functools.partial(finish, pp_ref, outp_ref))
    pl.when(jnp.logical_not(is_prompt))(functools.partial(finish, ps_ref, outs_ref))


def _combine(h, gate, p_prompt, p_sample, layer, n_p, split, wpg_bf16, wpp_bf16, g_row, b_row, yb,
             slot_flat):
    n, d = h.shape
    e = p_prompt.shape[1]
    n_ptiles = n_p // TM_ROW
    n_stiles = (n - n_p) // TM_ROW
    row = lambda i, s: (i, 0)
    row_p = lambda i, s: (jnp.minimum(i, n_ptiles - 1), 0)
    row_s = lambda i, s: (jnp.maximum(i - n_ptiles, 0), 0)
    fixed = lambda i, s: (0, 0)
    if split:
        out_specs = [pl.BlockSpec((TM_ROW, d), row_p), pl.BlockSpec((TM_ROW, d), row_s)]
        out_shape = [jax.ShapeDtypeStruct((n_p, d), F32), jax.ShapeDtypeStruct((n - n_p, d), F32)]
    else:
        out_specs = pl.BlockSpec((TM_ROW, d), row)
        out_shape = jax.ShapeDtypeStruct((n, d), F32)
    return pl.pallas_call(
        functools.partial(_combine_kernel, n_ptiles=n_ptiles, split=split),
        grid_spec=pltpu.PrefetchScalarGridSpec(
            num_scalar_prefetch=1,
            grid=(n // TM_ROW,),
            in_specs=[pl.BlockSpec((TM_ROW, d), row), pl.BlockSpec((TM_ROW, LANES), row),
                      pl.BlockSpec((TM_ROW, e), lambda i, s: (layer * n_ptiles + row_p(i, s)[0], 0)),
                      pl.BlockSpec((TM_ROW, e), lambda i, s: (layer * n_stiles + row_s(i, s)[0], 0)),
                      pl.BlockSpec((d, d), fixed), pl.BlockSpec((e, d), fixed), pl.BlockSpec((1, d), fixed),
                      pl.BlockSpec((1, d), fixed), pl.BlockSpec(memory_space=pl.ANY)],
            out_specs=out_specs,
            scratch_shapes=[pltpu.VMEM((TOP_K, TM_ROW * SUBLANES, LANES), F32),
                            pltpu.SemaphoreType.DMA(())]),
        out_shape=out_shape,
        compiler_params=_cparams(("arbitrary",)),
        name="moe_combine_ln_embed",
    )(slot_flat, h, gate, p_prompt, p_sample, wpg_bf16, wpp_bf16, g_row, b_row, yb)


def _layer_tail(h, ht, p_prompt, p_sample, layer, n_p, split, g2, b2, w_r, b_r, w_up, b_up, w_dn, b_dn,
                w_pg, w_pp):
    n, d = h.shape
    wr_pad = jnp.pad(w_r, ((0, 0), (0, LANES - N_EXPERTS)))
    br_row = jnp.pad(b_r, (0, LANES - N_EXPERTS))[None]
    idx, gate, rank, cnt = _router(h, wr_pad, br_row)
    counts = cnt[0, :N_EXPERTS]
    padded = (counts + MOE_BLOCK - 1) // MOE_BLOCK * MOE_BLOCK
    pend = jnp.cumsum(padded).astype(I32)
    pstart = pend - padded
    slot = (pstart[idx[:, :TOP_K]] + rank[:, :TOP_K]).reshape(-1).astype(I32)
    n_asg = n * TOP_K
    n_blocks = n_asg // MOE_BLOCK + N_EXPERTS
    assert n_asg % MOE_BLOCK == 0 and n % TM_ROW == 0
    n_slots = n_blocks * MOE_BLOCK
    n_used = pend[-1] // MOE_BLOCK
    blk = jnp.minimum(jnp.arange(n_blocks, dtype=I32), n_used - 1) * MOE_BLOCK
    block_e = jnp.minimum(jnp.sum((pend[None, :] <= blk[:, None]).astype(I32), axis=1), N_EXPERTS - 1)
    xb = _dispatch(ht, slot, pend, n_slots)
    yb = _experts(xb, block_e, n_used[None], layer, w_up, b_up[:, :, None, :], w_dn, b_dn[:, :, None, :])
    return _combine(h, gate, p_prompt, p_sample, layer, n_p, split, w_pg.astype(BF16), w_pp.astype(BF16),
                    g2[None], b2[None], yb, slot)


def _lane_row(v, lane0):
    return jnp.zeros((1, LANES), F32).at[0, lane0:lane0 + v.shape[0]].set(v.astype(F32))


def kernel(x_prompt, x_sample, cache_fox_k, cache_fox_v, cache_fox_logf, state_gdn, state_gdn_conv,
           cache_pool, p_prompt, p_sample, w_in_ab, b_fgate, gdn_a_log, gdn_dt_bias, gdn_conv_w,
           gdn_norm_g, w_out_ab, pool_w, pool_scale, w_out_pool, ln1_g, ln1_b, ln2_g, ln2_b,
           w_router, b_router, w_expert_up, b_expert_up, w_expert_down, b_expert_down,
           w_ple_gate, w_ple_proj):
    n_pb, seq, d = x_prompt.shape
    n_sb, dseq, _ = x_sample.shape
    past = cache_fox_k.shape[2]
    assert n_pb == 1 and dseq == CHUNK and past % dseq == 0 and seq % TQ == 0
    assert d == SUBLANES * LANES
    n_p = n_pb * seq
    n_s = n_sb * dseq
    n = n_p + n_s
    assert n_p % TM_TOK == 0 and n_s % TM_TOK == 0
    n_layers = p_prompt.shape[0]
    pp_all = p_prompt.reshape(n_layers * n_p, -1)
    ps_all = p_sample.reshape(n_layers * n_s, -1)

    def tail(h, ht, i, split):
        return _layer_tail(h, ht, pp_all, ps_all, i, n_p, split, ln2_g[i], ln2_b[i], w_router[i], b_router[i],
                           w_expert_up, b_expert_up, w_expert_down, b_expert_down, w_ple_gate[i],
                           w_ple_proj[i])

    w_in = w_in_ab[0]
    n_small = 3 * N_HEADS
    ff0 = 3 * WIDTH
    gq0 = ff0 + N_HEADS
    ga0 = gq0 + 4 * WIDTH
    w_small = jnp.concatenate([w_in[:, ff0:gq0], w_in[:, ga0:ga0 + 2 * N_HEADS],
                               jnp.zeros((d, LANES - n_small), F32)], axis=1)
    w_all = jnp.concatenate([w_in[:, :ff0], w_in[:, gq0:ga0], w_small], axis=1)
    bf_row = _lane_row(b_fgate[0], LANE_F)
    up = _proj(x_prompt.reshape(n_p, d), w_all.astype(BF16), bf_row, U_COLS, False)
    us = _proj(x_sample.reshape(n_s, d), w_all, bf_row, LANES, True)

    u3 = up[None]
    us3 = us.reshape(n_sb, dseq, U_COLS)
    cq_p, ck_p = _cumsum(u3, 1, n_p, TK, COL_S // LANES)
    lf_s = jnp.concatenate(
        [jnp.pad(cache_fox_logf[0].astype(F32), ((0, 0), (0, 0), (0, LANES - N_HEADS))),
         us3[:, :, COL_S:]], axis=1)
    cq_s, ck_s = _cumsum(lf_s, n_sb, past + dseq, past + dseq, 0)

    of_p = _fox(u3, lambda b, i: (0, i, COL_Q // WIDTH), u3, lambda b, j: (0, j, COL_K // WIDTH),
                u3, lambda b, j: (0, j, COL_V // WIDTH), cq_p, lambda b, i: (0, i, 0),
                ck_p, lambda b, j: (0, 0, j), 1, n_p // TQ, TQ, TK, 0)
    k_all = jnp.concatenate([cache_fox_k[0].reshape(n_sb, past, WIDTH), us3[:, :, COL_K:COL_K + WIDTH]], axis=1)
    v_all = jnp.concatenate([cache_fox_v[0].reshape(n_sb, past, WIDTH), us3[:, :, COL_V:COL_V + WIDTH]], axis=1)
    of_s = _fox(us3, lambda b, i: (b, 0, COL_Q // WIDTH), k_all, lambda b, j: (b, 0, 0),
                v_all, lambda b, j: (b, 0, 0), cq_s, lambda b, i: (b, past // dseq, 0),
                ck_s, lambda b, j: (b, 0, 0), n_sb, 1, dseq, past + dseq, past, hi=True)

    gw = 3 * WIDTH
    conv_w = jnp.pad(gdn_conv_w[0], ((0, SUBLANES - CONV_WIDTH), (0, 0)))
    gdn_args = (conv_w, _lane_row(gdn_a_log[0], LANE_A), _lane_row(gdn_dt_bias[0], LANE_A),
                gdn_norm_g[0][None])
    n_pstep = n_p // GDN_ROWS
    ends = lambda k: (jnp.asarray((np.arange(k) == 0).astype(np.int32)),
                      jnp.asarray((np.arange(k) == k - 1).astype(np.int32)))
    og_p, st_p = _gdn(up.reshape(n_pstep, GDN_ROWS, U_COLS), False, *ends(n_pstep),
                      jnp.zeros((n_pstep,), I32), jnp.zeros((1, SUBLANES, gw), F32), gdn_args[0],
                      jnp.zeros((1, N_HEADS, HEAD_DIM, HEAD_DIM), F32), *gdn_args[1:])
    conv_past = jnp.pad(state_gdn_conv[0].astype(F32), ((0, 0), (SUBLANES - (CONV_WIDTH - 1), 0), (0, 0)))
    ones = jnp.ones((n_sb,), I32)
    og_s, st_s = _gdn(us3, True, ones, ones, jnp.arange(n_sb, dtype=I32), conv_past, gdn_args[0],
                      state_gdn[0].astype(F32), *gdn_args[1:])
    h, ht = _outproj_ln((x_prompt.reshape(n_p, d), of_p.reshape(n_p, WIDTH), og_p.reshape(n_p, WIDTH)),
                        (x_sample.reshape(n_s, d), of_s.reshape(n_s, WIDTH), og_s.reshape(n_s, WIDTH)),
                        w_out_ab[0], ln1_g[0][None], ln1_b[0][None])
    x1 = tail(h, ht, 0, False)

    pool_args = (pool_w[0].astype(BF16), pool_scale[0][None], w_out_pool[0].astype(BF16),
                 ln1_g[1][None], ln1_b[1][None])
    ratio = TM_TOK // POOL_HALO
    x1_halo = x1.reshape(n // POOL_HALO, POOL_HALO, d)
    h_p, ht_p = _pool_ln(x1, lambda i: (i, 0), x1_halo, lambda i: (jnp.maximum(i * ratio - 1, 0), 0, 0),
                         n_p // TM_TOK, TM_TOK, 0, True, *pool_args)
    cache16 = jnp.pad(cache_pool[0].astype(F32), ((0, 0), (POOL_HALO - POOL_STATE, 0), (0, 0)))
    h_s, ht_s = _pool_ln(x1, lambda i: (n_p // dseq + i, 0), cache16, lambda i: (i, 0, 0),
                         n_sb, dseq, past, False, *pool_args)
    x2_p, x2_s = tail(jnp.concatenate([h_p, h_s], axis=0), jnp.concatenate([ht_p, ht_s], axis=0), 1, True)

    return (x2_p.reshape(n_pb, seq, d), x2_s.reshape(n_sb, dseq, d),
            up[:, COL_K:COL_K + WIDTH].reshape(1, n_pb, seq, N_HEADS, HEAD_DIM),
            up[:, COL_V:COL_V + WIDTH].reshape(1, n_pb, seq, N_HEADS, HEAD_DIM),
            up[:, COL_S:COL_S + N_HEADS].reshape(1, n_pb, seq, N_HEADS),
            st_p.reshape(1, n_pb, N_HEADS, HEAD_DIM, HEAD_DIM),
            up[seq - (CONV_WIDTH - 1):, COL_G:COL_G + gw].reshape(1, n_pb, CONV_WIDTH - 1, gw),
            x1[n_p - POOL_STATE:n_p].reshape(1, n_pb, POOL_STATE, d),
            us[:, COL_K:COL_K + WIDTH].reshape(1, n_sb, dseq, N_HEADS, HEAD_DIM),
            us[:, COL_V:COL_V + WIDTH].reshape(1, n_sb, dseq, N_HEADS, HEAD_DIM),
            us[:, COL_S:COL_S + N_HEADS].reshape(1, n_sb, dseq, N_HEADS),
            st_s.reshape(1, n_sb, N_HEADS, HEAD_DIM, HEAD_DIM),
            us[:, COL_G:COL_G + gw].reshape(n_sb, dseq, gw)[:, dseq - (CONV_WIDTH - 1):].reshape(
                1, n_sb, CONV_WIDTH - 1, gw),
            x1[n_p:].reshape(n_sb, dseq, d)[:, dseq - POOL_STATE:].reshape(1, n_sb, POOL_STATE, d))
```

```python
import functools

import numpy as np
import jax
import jax.numpy as jnp
from jax import lax
from jax.experimental import pallas as pl
from jax.experimental.pallas import tpu as pltpu
from jax.experimental.pallas import tpu_sc as plsc

F32 = jnp.float32
BF16 = jnp.bfloat16
I32 = jnp.int32
HIGHEST = lax.Precision.HIGHEST

LANES = 128
SUBLANES = 8
VMEM_LIMIT = 56 * 1024 * 1024

HEAD_DIM = 128
N_HEADS = 4
WIDTH = N_HEADS * HEAD_DIM
CHUNK = 64
CONV_WIDTH = 4
POOL_WINDOWS = (2, 4, 8, 16)
POOL_HALO = 16
POOL_STATE = 15
N_EXPERTS = 32
TOP_K = 4
SWIGLU_LIMIT = 7.0
SWIGLU_ALPHA = 1.702
DEPTH = 2
DN_ALPHA = (2 * DEPTH) ** 0.25
LN_EPS = 1e-5
NORM_EPS = 1e-6
NEG_INF = -1e30
LOG2E = 1.4426950408889634

COL_Q, COL_K, COL_V = 0, WIDTH, 2 * WIDTH
COL_G = 3 * WIDTH
COL_Z = 6 * WIDTH
COL_S = 7 * WIDTH
U_COLS = COL_S + LANES
LANE_F, LANE_A, LANE_B = 0, N_HEADS, 2 * N_HEADS

TM_PROJ = 256
TM_TOK = 512
TM_ROW = 256
SC_WINDOW = 32
MOE_BLOCK = 512
TQ = 1024
TK = 512
GDN_ROWS = 256


def _cparams(sem):
    return pltpu.CompilerParams(dimension_semantics=sem, vmem_limit_bytes=VMEM_LIMIT)


def _softplus(x):
    return jnp.maximum(x, 0.0) + jnp.log1p(jnp.exp(-jnp.abs(x)))


def _sigmoid(x):
    return 1.0 / (1.0 + jnp.exp(-x))


def _silu(x):
    return x * _sigmoid(x)


def _layer_norm(y, g, b):
    mu = jnp.mean(y, axis=-1, keepdims=True)
    yc = y - mu
    var = jnp.mean(yc * yc, axis=-1, keepdims=True)
    return yc * lax.rsqrt(var + LN_EPS) * g + b


def _dot_general(a, b, dims, hi):
    if hi:
        return lax.dot_general(a.astype(F32), b.astype(F32), (dims, ((), ())), precision=HIGHEST,
                               preferred_element_type=F32)
    return lax.dot_general(a.astype(BF16), b.astype(BF16), (dims, ((), ())), preferred_element_type=F32)


def _dot(a, b, hi=False):
    return _dot_general(a, b, ((1,), (0,)), hi)


def _dot_nt(a, b, hi=False):
    return _dot_general(a, b, ((1,), (1,)), hi)


def _dot_tn(a, b, hi=False):
    return _dot_general(a, b, ((0,), (0,)), hi)


def _spread_lanes(x, width):
    if width % LANES == 0:
        return jnp.concatenate([x] * (width // LANES), axis=1)
    return jnp.broadcast_to(x[:, 0:1], (x.shape[0], width))


def _load_token_tiles(ref, lead, n_tok):
    return jnp.concatenate([ref[(*lead, pl.ds(j, n_tok, stride=SUBLANES), slice(None))]
                            for j in range(SUBLANES)], axis=1)


def _store_token_tiles(ref, lead, x):
    for j in range(SUBLANES):
        ref[(*lead, pl.ds(j, x.shape[0], stride=SUBLANES), slice(None))] = x[:, j * LANES:(j + 1) * LANES]


def _lanes_to_rows(x, lane0):
    r = lax.broadcasted_iota(I32, (SUBLANES, LANES), 0)
    c = lax.broadcasted_iota(I32, (SUBLANES, LANES), 1)
    sel = (c == r + lane0).astype(F32)
    return lax.dot_general(sel, x, (((1,), (1,)), ((), ())), precision=HIGHEST,
                           preferred_element_type=F32)


def _proj_kernel(x_ref, w_ref, bf_ref, u_ref, *, hi):
    u = _dot(x_ref[...], w_ref[...], hi)
    u_ref[...] = u

    @pl.when(pl.program_id(1) == pl.num_programs(1) - 1)
    def _():
        small = u[:, u.shape[1] - LANES:]
        lane = lax.broadcasted_iota(I32, small.shape, 1)
        logf = -_softplus(-(small + bf_ref[...]))
        u_ref[:, u.shape[1] - LANES:] = jnp.where(lane < LANE_A, logf, small)


def _proj(x, w, bf_row, tn, hi):
    n, d = x.shape
    m = w.shape[1]
    return pl.pallas_call(
        functools.partial(_proj_kernel, hi=hi),
        grid=(n // TM_PROJ, m // tn),
        in_specs=[pl.BlockSpec((TM_PROJ, d), lambda i, j: (i, 0)),
                  pl.BlockSpec((d, tn), lambda i, j: (0, j)),
                  pl.BlockSpec((1, LANES), lambda i, j: (0, 0))],
        out_specs=pl.BlockSpec((TM_PROJ, tn), lambda i, j: (i, j)),
        out_shape=jax.ShapeDtypeStruct((n, m), F32),
        compiler_params=_cparams(("parallel", "parallel")),
        name="in_proj",
    )(x, w, bf_row)


def _cumsum_kernel(lf_ref, crep_ref, crow_ref, carry_ref):
    @pl.when(pl.program_id(1) == 0)
    def _():
        carry_ref[...] = jnp.zeros_like(carry_ref)

    lf = lf_ref[0]
    t = lf.shape[0]
    r = lax.broadcasted_iota(I32, (t, t), 0)
    c = lax.broadcasted_iota(I32, (t, t), 1)
    tril = (c <= r).astype(F32)
    cs = jnp.dot(tril, lf, precision=HIGHEST, preferred_element_type=F32) + carry_ref[0:1, :]
    carry_ref[...] = jnp.broadcast_to(cs[t - 1:t, :], carry_ref.shape)
    c2 = cs * LOG2E
    crow_ref[0] = _lanes_to_rows(c2, LANE_F)
    for h in range(N_HEADS):
        crep_ref[0, :, h * HEAD_DIM:(h + 1) * HEAD_DIM] = jnp.broadcast_to(
            c2[:, LANE_F + h:LANE_F + h + 1], (t, HEAD_DIM))


def _cumsum(arr, n_batch, length, tl, col_block):
    return pl.pallas_call(
        _cumsum_kernel,
        grid=(n_batch, length // tl),
        in_specs=[pl.BlockSpec((1, tl, LANES), lambda b, j: (b, j, col_block))],
        out_specs=[pl.BlockSpec((1, tl, WIDTH), lambda b, j: (b, j, 0)),
                   pl.BlockSpec((1, SUBLANES, tl), lambda b, j: (b, 0, j))],
        out_shape=[jax.ShapeDtypeStruct((n_batch, length, WIDTH), F32),
                   jax.ShapeDtypeStruct((n_batch, SUBLANES, length), F32)],
        scratch_shapes=[pltpu.VMEM((SUBLANES, LANES), F32)],
        compiler_params=_cparams(("parallel", "arbitrary")),
        name="logf_cumsum",
    )(arr)


def _fox_kernel(qi_ref, kj_ref, last_ref, q_ref, k_ref, v_ref, cq_ref, ck_ref, o_ref,
                m_ref, l_ref, acc_ref, *, tq, tk, past, hi):
    s_idx = pl.program_id(1)
    qi = qi_ref[s_idx]
    kj = kj_ref[s_idx]

    @pl.when(kj == 0)
    def _():
        m_ref[...] = jnp.full_like(m_ref, NEG_INF)
        l_ref[...] = jnp.zeros_like(l_ref)
        acc_ref[...] = jnp.zeros_like(acc_ref)

    def update(masked):
        if masked:
            q_pos = past + qi * tq + lax.broadcasted_iota(I32, (tq, tk), 0)
            k_pos = kj * tk + lax.broadcasted_iota(I32, (tq, tk), 1)
            visible = k_pos <= q_pos
        for h in range(N_HEADS):
            cols = slice(h * HEAD_DIM, (h + 1) * HEAD_DIM)
            q = q_ref[0, :, cols] * (HEAD_DIM ** -0.5 * LOG2E)
            t = _dot_nt(q, k_ref[0, :, cols], hi) - ck_ref[0, h:h + 1, :]
            if masked:
                t = jnp.where(visible, t, NEG_INF)
            cq = cq_ref[0, :, cols]
            m_prev = m_ref[h]
            m_new = jnp.maximum(m_prev, jnp.max(t, axis=-1, keepdims=True) + cq)
            p = jnp.exp2(t - _spread_lanes(m_new - cq, tk))
            alpha = jnp.exp2(m_prev - m_new)
            l_ref[h] = alpha * l_ref[h] + jnp.sum(p, axis=-1, keepdims=True)
            acc_ref[:, cols] = alpha * acc_ref[:, cols] + _dot(p, v_ref[0, :, cols], hi)
            m_ref[h] = m_new

    crosses_diagonal = kj * tk + (tk - 1) > past + qi * tq
    pl.when(crosses_diagonal)(functools.partial(update, True))
    pl.when(jnp.logical_not(crosses_diagonal))(functools.partial(update, False))

    @pl.when(last_ref[s_idx] == 1)
    def _():
        for h in range(N_HEADS):
            cols = slice(h * HEAD_DIM, (h + 1) * HEAD_DIM)
            o_ref[0, :, cols] = acc_ref[:, cols] / l_ref[h]


def _fox_schedule(n_q, tq, tk, past):
    qi, kj, last = [], [], []
    for i in range(n_q):
        hi = (past + (i + 1) * tq - 1) // tk
        for j in range(hi + 1):
            qi.append(i)
            kj.append(j)
            last.append(1 if j == hi else 0)
    return (jnp.asarray(np.array(qi, np.int32)), jnp.asarray(np.array(kj, np.int32)),
            jnp.asarray(np.array(last, np.int32)))


def _fox(q_arr, q_map, k_arr, k_map, v_arr, v_map, cq_arr, cq_map, ck_arr, ck_map,
         n_batch, n_q, tq, tk, past, hi=False):
    qi, kj, last = _fox_schedule(n_q, tq, tk, past)
    n_steps = int(qi.shape[0])
    spec = lambda shape, fn, tab: pl.BlockSpec(shape, lambda b, s, qi_r, kj_r, la_r: fn(b, (qi_r if tab == 'q' else kj_r)[s]))
    return pl.pallas_call(
        functools.partial(_fox_kernel, tq=tq, tk=tk, past=past, hi=hi),
        grid_spec=pltpu.PrefetchScalarGridSpec(
            num_scalar_prefetch=3,
            grid=(n_batch, n_steps),
            in_specs=[spec((1, tq, WIDTH), q_map, 'q'),
                      spec((1, tk, WIDTH), k_map, 'k'),
                      spec((1, tk, WIDTH), v_map, 'k'),
                      spec((1, tq, WIDTH), cq_map, 'q'),
                      spec((1, SUBLANES, tk), ck_map, 'k')],
            out_specs=spec((1, tq, WIDTH), lambda b, i: (b, i, 0), 'q'),
            scratch_shapes=[pltpu.VMEM((N_HEADS, tq, HEAD_DIM), F32),
                            pltpu.VMEM((N_HEADS, tq, HEAD_DIM), F32),
                            pltpu.VMEM((tq, WIDTH), F32)]),
        out_shape=jax.ShapeDtypeStruct((n_batch, n_q * tq, WIDTH), F32),
        compiler_params=_cparams(("parallel", "arbitrary")),
        name="fox_attention",
    )(qi, kj, last, q_arr, k_arr, v_arr, cq_arr, ck_arr)


def _gdn_kernel(first_ref, last_ref, seq_ref,
                pre_ref, z_ref, sm_ref, cpast_ref, convw_ref, s0_ref, alog_ref, dtb_ref, ng_ref,
                o_ref, sout_ref, stage_ref, s_ref, *, rows, hi):
    dot, dot_nt, dot_tn = (functools.partial(f, hi=hi) for f in (_dot, _dot_nt, _dot_tn))
    step = pl.program_id(0)
    halo = SUBLANES
    n_chunks = rows // CHUNK

    @pl.when(first_ref[step] == 1)
    def _():
        stage_ref[0:halo, :] = cpast_ref[0]
        s_ref[...] = s0_ref[0]

    stage_ref[halo:halo + rows, :] = pre_ref[0]
    conv = stage_ref[halo:halo + rows, :] * convw_ref[CONV_WIDTH - 1:CONV_WIDTH, :]
    for j in range(1, CONV_WIDTH):
        conv = conv + (stage_ref[halo - j:halo - j + rows, :]
                       * convw_ref[CONV_WIDTH - 1 - j:CONV_WIDTH - j, :])
    stage_ref[0:halo, :] = stage_ref[rows:rows + halo, :]
    act = _silu(conv)

    small = sm_ref[0]
    beta_all = _sigmoid(small)
    g_all = -jnp.exp(alog_ref[...]) * _softplus(small + dtb_ref[...])
    r = lax.broadcasted_iota(I32, (rows, rows), 0)
    c = lax.broadcasted_iota(I32, (rows, rows), 1)
    same_chunk = (r // CHUNK) == (c // CHUNK)
    incl = same_chunk & (c <= r)
    strict = same_chunk & (c < r)
    eye = (c == r).astype(F32)
    gc_all = jnp.dot(incl.astype(F32), g_all, precision=HIGHEST, preferred_element_type=F32)
    gc_rows = _lanes_to_rows(gc_all, LANE_A)

    for h in range(N_HEADS):
        cols = slice(h * HEAD_DIM, (h + 1) * HEAD_DIM)
        q = act[:, h * HEAD_DIM:(h + 1) * HEAD_DIM]
        k = act[:, WIDTH + h * HEAD_DIM:WIDTH + (h + 1) * HEAD_DIM]
        v = act[:, 2 * WIDTH + h * HEAD_DIM:2 * WIDTH + (h + 1) * HEAD_DIM]
        q = q * lax.rsqrt(jnp.sum(q * q, axis=-1, keepdims=True) + NORM_EPS) * (HEAD_DIM ** -0.5)
        k = k * lax.rsqrt(jnp.sum(k * k, axis=-1, keepdims=True) + NORM_EPS)
        beta = beta_all[:, LANE_B + h:LANE_B + h + 1]
        gc = gc_all[:, LANE_A + h:LANE_A + h + 1]
        diff = gc - gc_rows[h:h + 1, :]
        decay = jnp.where(incl, jnp.exp(jnp.where(incl, diff, 0.0)), 0.0)
        kb = k * beta
        vb = v * beta
        low = jnp.where(strict, dot_nt(kb, k) * decay, 0.0)
        inv = eye - low
        pw = dot(low, low)
        n_sq = CHUNK.bit_length() - 2
        for it in range(n_sq):
            inv = inv + dot(inv, pw)
            if it + 1 < n_sq:
                pw = dot(pw, pw)
        egc = jnp.exp(gc)
        uw = dot(inv, jnp.concatenate([vb, kb * egc], axis=1))
        intra = jnp.where(incl, dot_nt(q, k) * decay, 0.0)
        qd = q * egc
        g_last = [gc[(g + 1) * CHUNK - 1:(g + 1) * CHUNK, :] for g in range(n_chunks)]
        kd = k * jnp.exp(jnp.concatenate([jnp.broadcast_to(gl, (CHUNK, 1)) for gl in g_last], axis=0)
                         - gc)
        state = s_ref[h]
        v_new = []
        for g in range(n_chunks):
            rs = slice(g * CHUNK, (g + 1) * CHUNK)
            v_new.append(uw[rs, :HEAD_DIM] - dot(uw[rs, HEAD_DIM:], state))
            v_rows = jnp.concatenate(
                v_new + [jnp.zeros((rows - (g + 1) * CHUNK, HEAD_DIM), F32)] * (g + 1 < n_chunks), axis=0)
            o = dot(qd[rs, :], state) + dot(intra[rs, :], v_rows)
            state = state * jnp.exp(g_last[g]) + dot_tn(kd[rs, :], v_new[g])
            o = (o * lax.rsqrt(jnp.mean(o * o, axis=-1, keepdims=True) + NORM_EPS)
                 * ng_ref[...] * _silu(z_ref[0, rs, cols]))
            o_ref[0, rs, cols] = o
        s_ref[h] = state

    @pl.when(last_ref[step] == 1)
    def _():
        sout_ref[0] = s_ref[...]


def _gdn(u_view, hi, first, last, seq, conv_past, conv_w, s0, alog_row, dtb_row, ng_row):
    n_steps, rows, _ = u_view.shape
    n_seq = s0.shape[0]
    gw = 3 * WIDTH
    return pl.pallas_call(
        functools.partial(_gdn_kernel, rows=rows, hi=hi),
        grid_spec=pltpu.PrefetchScalarGridSpec(
            num_scalar_prefetch=3,
            grid=(n_steps,),
            in_specs=[pl.BlockSpec((1, rows, gw), lambda s, f, l, q: (s, 0, COL_G // gw)),
                      pl.BlockSpec((1, rows, WIDTH), lambda s, f, l, q: (s, 0, COL_Z // WIDTH)),
                      pl.BlockSpec((1, rows, LANES), lambda s, f, l, q: (s, 0, COL_S // LANES)),
                      pl.BlockSpec((1, SUBLANES, gw), lambda s, f, l, q: (q[s], 0, 0)),
                      pl.BlockSpec((SUBLANES, gw), lambda s, f, l, q: (0, 0)),
                      pl.BlockSpec((1, N_HEADS, HEAD_DIM, HEAD_DIM), lambda s, f, l, q: (q[s], 0, 0, 0)),
                      pl.BlockSpec((1, LANES), lambda s, f, l, q: (0, 0)),
                      pl.BlockSpec((1, LANES), lambda s, f, l, q: (0, 0)),
                      pl.BlockSpec((1, LANES), lambda s, f, l, q: (0, 0))],
            out_specs=[pl.BlockSpec((1, rows, WIDTH), lambda s, f, l, q: (s, 0, 0)),
                       pl.BlockSpec((1, N_HEADS, HEAD_DIM, HEAD_DIM), lambda s, f, l, q: (q[s], 0, 0, 0))],
            scratch_shapes=[pltpu.VMEM((rows + SUBLANES, gw), F32),
                            pltpu.VMEM((N_HEADS, HEAD_DIM, HEAD_DIM), F32)]),
        out_shape=[jax.ShapeDtypeStruct((n_steps, rows, WIDTH), F32),
                   jax.ShapeDtypeStruct((n_seq, N_HEADS, HEAD_DIM, HEAD_DIM), F32)],
        compiler_params=_cparams(("arbitrary",)),
        name="gated_deltanet",
    )(first, last, seq, u_view, u_view, u_view, conv_past, conv_w, s0, alog_row, dtb_row, ng_row)


def _outproj_ln_kernel(xp_ref, ofp_ref, ogp_ref, xs_ref, ofs_ref, ogs_ref, w_ref, wf_ref, g_ref, b_ref,
                       h_ref, ht_ref, *, hi_from):
    def run(hi, x_ref, of_ref, og_ref):
        w = wf_ref if hi else w_ref
        mix = _dot(of_ref[...], w[0:WIDTH, :], hi) + _dot(og_ref[...], w[WIDTH:2 * WIDTH, :], hi)
        h = _layer_norm(DN_ALPHA * x_ref[...] + mix, g_ref[...], b_ref[...])
        h_ref[...] = h
        _store_token_tiles(ht_ref, (), h)

    pl.when(pl.program_id(0) < hi_from)(functools.partial(run, False, xp_ref, ofp_ref, ogp_ref))
    pl.when(pl.program_id(0) >= hi_from)(functools.partial(run, True, xs_ref, ofs_ref, ogs_ref))


def _outproj_ln(prompt, sample, w, g_row, b_row):
    n_p, d = prompt[0].shape
    n = n_p + sample[0].shape[0]
    hi_from = n_p // TM_TOK
    row = lambda i: (i, 0)
    row_p = lambda i: (jnp.minimum(i, hi_from - 1), 0)
    row_s = lambda i: (jnp.maximum(i - hi_from, 0), 0)
    fixed = lambda i: (0, 0)
    group = lambda rows: [pl.BlockSpec((TM_TOK, d), rows), pl.BlockSpec((TM_TOK, WIDTH), rows),
                          pl.BlockSpec((TM_TOK, WIDTH), rows)]
    return pl.pallas_call(
        functools.partial(_outproj_ln_kernel, hi_from=hi_from),
        grid=(n // TM_TOK,),
        in_specs=group(row_p) + group(row_s) + [pl.BlockSpec((2 * WIDTH, d), fixed),
                                                pl.BlockSpec((2 * WIDTH, d), fixed),
                                                pl.BlockSpec((1, d), fixed), pl.BlockSpec((1, d), fixed)],
        out_specs=[pl.BlockSpec((TM_TOK, d), row), pl.BlockSpec((TM_TOK * SUBLANES, LANES), row)],
        out_shape=[jax.ShapeDtypeStruct((n, d), F32), jax.ShapeDtypeStruct((n * SUBLANES, LANES), F32)],
        compiler_params=_cparams(("parallel",)),
        name="out_proj_ln",
    )(*prompt, *sample, w.astype(BF16), w, g_row, b_row)


def _pool_ln_kernel(x_ref, halo_ref, pw_ref, ps_ref, w_ref, g_ref, b_ref, h_ref, ht_ref, stage_ref,
                    *, tm, pos0, zero_first_halo):
    i = pl.program_id(0)
    stage_ref[0:POOL_HALO, :] = halo_ref[0]
    if zero_first_halo:
        @pl.when(i == 0)
        def _():
            stage_ref[0:POOL_HALO, :] = jnp.zeros((POOL_HALO, stage_ref.shape[1]), F32)
    x = x_ref[...]
    stage_ref[POOL_HALO:POOL_HALO + tm, :] = x
    gdim = x.shape[1] // len(POOL_WINDOWS)
    pos = pos0 + lax.broadcasted_iota(I32, (tm, 1), 0)
    if zero_first_halo:
        pos = pos + i * tm
    parts = []
    for gi, win in enumerate(POOL_WINDOWS):
        cols = slice(gi * gdim, (gi + 1) * gdim)
        s = stage_ref[POOL_HALO:POOL_HALO + tm, cols]
        for j in range(1, win):
            s = s + stage_ref[POOL_HALO - j:POOL_HALO - j + tm, cols]
        cnt = jnp.minimum(pos + 1, win).astype(F32)
        zg = s / cnt - x[:, cols]
        parts.append(_dot(zg, pw_ref[gi]))
    zg = jnp.concatenate(parts, axis=-1) * ps_ref[...]
    mix = _dot(zg, w_ref[...])
    h = _layer_norm(DN_ALPHA * x + mix, g_ref[...], b_ref[...])
    h_ref[...] = h
    _store_token_tiles(ht_ref, (), h)


def _pool_ln(x, x_map, halo_arr, halo_map, n_tiles, tm, pos0, zero_first_halo,
             pw_bf16, ps_row, w_bf16, g_row, b_row):
    d = x.shape[1]
    gdim = d // len(POOL_WINDOWS)
    fixed = lambda i: (0, 0)
    return pl.pallas_call(
        functools.partial(_pool_ln_kernel, tm=tm, pos0=pos0, zero_first_halo=zero_first_halo),
        grid=(n_tiles,),
        in_specs=[pl.BlockSpec((tm, d), x_map),
                  pl.BlockSpec((1, POOL_HALO, d), halo_map),
                  pl.BlockSpec((len(POOL_WINDOWS), gdim, gdim), lambda i: (0, 0, 0)),
                  pl.BlockSpec((1, d), fixed), pl.BlockSpec((d, d), fixed),
                  pl.BlockSpec((1, d), fixed), pl.BlockSpec((1, d), fixed)],
        out_specs=[pl.BlockSpec((tm, d), lambda i: (i, 0)),
                   pl.BlockSpec((tm * SUBLANES, LANES), lambda i: (i, 0))],
        out_shape=[jax.ShapeDtypeStruct((n_tiles * tm, d), F32),
                   jax.ShapeDtypeStruct((n_tiles * tm * SUBLANES, LANES), F32)],
        scratch_shapes=[pltpu.VMEM((POOL_HALO + tm, d), F32)],
        compiler_params=_cparams(("arbitrary",)),
        name="pool_mixer_ln",
    )(x, halo_arr, pw_bf16, ps_row, w_bf16, g_row, b_row)


def _router_kernel(h_ref, wr_ref, br_ref, idx_ref, gate_ref, rank_ref, cnt_ref, carry_ref):
    @pl.when(pl.program_id(0) == 0)
    def _():
        carry_ref[...] = jnp.zeros_like(carry_ref)

    tm = h_ref.shape[0]
    lane = lax.broadcasted_iota(I32, (tm, LANES), 1).astype(F32)
    logits = jnp.dot(h_ref[...], wr_ref[...], precision=HIGHEST, preferred_element_type=F32)
    work = jnp.where(lane < N_EXPERTS, logits + br_ref[...], -jnp.inf)
    vals, ids = [], []
    for _ in range(TOP_K):
        m = jnp.max(work, axis=-1, keepdims=True)
        ik = jnp.min(jnp.where(work == m, lane, float(LANES)), axis=-1, keepdims=True)
        vals.append(m)
        ids.append(ik)
        work = jnp.where(lane == ik, -jnp.inf, work)
    exps = [jnp.exp(v - vals[0]) for v in vals]
    denom = exps[0]
    for e in exps[1:]:
        denom = denom + e
    multihot = jnp.zeros((tm, LANES), F32)
    idx_out = jnp.zeros((tm, LANES), F32)
    gate_out = jnp.zeros((tm, LANES), F32)
    for k in range(TOP_K):
        multihot = multihot + (lane == ids[k]).astype(F32)
        idx_out = jnp.where(lane == k, ids[k], idx_out)
        gate_out = jnp.where(lane == k, exps[k] / denom, gate_out)
    r = lax.broadcasted_iota(I32, (tm, tm), 0)
    c = lax.broadcasted_iota(I32, (tm, tm), 1)
    before = _dot((c < r).astype(F32), multihot) + carry_ref[0:1, :]
    rank_out = jnp.zeros((tm, LANES), F32)
    for k in range(TOP_K):
        rk = jnp.sum(jnp.where(lane == ids[k], before, 0.0), axis=-1, keepdims=True)
        rank_out = jnp.where(lane == k, rk, rank_out)
    idx_ref[...] = idx_out.astype(I32)
    gate_ref[...] = gate_out
    rank_ref[...] = rank_out.astype(I32)
    total = carry_ref[0:1, :] + jnp.sum(multihot, axis=0, keepdims=True)
    carry_ref[...] = jnp.broadcast_to(total, carry_ref.shape)
    cnt_ref[...] = jnp.broadcast_to(total, cnt_ref.shape).astype(I32)


def _router(h, wr_pad, br_row):
    n, d = h.shape
    row = lambda i: (i, 0)
    fixed = lambda i: (0, 0)
    return pl.pallas_call(
        _router_kernel,
        grid=(n // TM_TOK,),
        in_specs=[pl.BlockSpec((TM_TOK, d), row), pl.BlockSpec((d, LANES), fixed),
                  pl.BlockSpec((1, LANES), fixed)],
        out_specs=[pl.BlockSpec((TM_TOK, LANES), row), pl.BlockSpec((TM_TOK, LANES), row),
                   pl.BlockSpec((TM_TOK, LANES), row), pl.BlockSpec((SUBLANES, LANES), fixed)],
        out_shape=[jax.ShapeDtypeStruct((n, LANES), I32), jax.ShapeDtypeStruct((n, LANES), F32),
                   jax.ShapeDtypeStruct((n, LANES), I32), jax.ShapeDtypeStruct((SUBLANES, LANES), I32)],
        scratch_shapes=[pltpu.VMEM((SUBLANES, LANES), F32)],
        compiler_params=_cparams(("arbitrary",)),
        name="moe_router",
    )(h, wr_pad, br_row)


def _tile_copy(src_ref, src_row, dst_ref, dst_row, sem):
    rows = lambda r: pl.ds(pl.multiple_of(r, SUBLANES), SUBLANES)
    return pltpu.make_async_copy(src_ref.at[rows(src_row)], dst_ref.at[rows(dst_row)], sem)


def _tiles_of(ref, n_tok, sem):
    span = ref.at[pl.ds(0, n_tok * SUBLANES)]
    return pltpu.make_async_copy(span, span, sem)


def _dispatch_kernel(slot_ref, pend_ref, ht_ref, xb_ref, zero_ref, sem, zsem):
    tm = ht_ref.shape[0] // SUBLANES
    base = pl.program_id(0) * tm * TOP_K
    n_blocks = xb_ref.shape[0] // (MOE_BLOCK * SUBLANES)

    @pl.when(pl.program_id(0) == 0)
    def _():
        zero_ref[...] = jnp.zeros_like(zero_ref)
        block = lambda b: pltpu.make_async_copy(
            zero_ref, xb_ref.at[pl.ds(pl.multiple_of(b * (MOE_BLOCK * SUBLANES), SUBLANES),
                                      MOE_BLOCK * SUBLANES)], zsem)
        n_used = pend_ref[N_EXPERTS - 1] // MOE_BLOCK
        group_last = lambda e: jnp.maximum(pend_ref[e] // MOE_BLOCK - 1, 0)
        for e in range(N_EXPERTS):
            block(group_last(e)).start()
        lax.fori_loop(n_used, n_blocks, lambda b, c: (block(b).start(), c)[1], 0)
        for e in range(N_EXPERTS):
            block(group_last(e)).wait()
        lax.fori_loop(n_used, n_blocks, lambda b, c: (block(b).wait(), c)[1], 0)

    for r in range(tm):
        for k in range(TOP_K):
            _tile_copy(ht_ref, r * SUBLANES, xb_ref, slot_ref[base + r * TOP_K + k] * SUBLANES, sem).start()
    _tiles_of(xb_ref, tm * TOP_K, sem).wait()


def _dispatch(ht, slot_flat, pend, n_slots):
    n = ht.shape[0] // SUBLANES
    return pl.pallas_call(
        _dispatch_kernel,
        grid_spec=pltpu.PrefetchScalarGridSpec(
            num_scalar_prefetch=2,
            grid=(n // TM_ROW,),
            in_specs=[pl.BlockSpec((TM_ROW * SUBLANES, LANES), lambda i, s, p: (i, 0))],
            out_specs=pl.BlockSpec(memory_space=pl.ANY),
            scratch_shapes=[pltpu.VMEM((MOE_BLOCK * SUBLANES, LANES), F32), pltpu.SemaphoreType.DMA(()),
                            pltpu.SemaphoreType.DMA(())]),
        out_shape=jax.ShapeDtypeStruct((n_slots * SUBLANES, LANES), F32),
        compiler_params=_cparams(("arbitrary",)),
        name="moe_dispatch",
    )(slot_flat, pend, ht)


def _expert_kernel(be_ref, nu_ref, xb_ref, wup_ref, bup_ref, wdn_ref, bdn_ref, yb_ref,
                   wup_bf_ref, wdn_bf_ref):
    b = pl.program_id(0)

    @pl.when(b < nu_ref[0])
    def _():
        @pl.when((b == 0) | (be_ref[b] != be_ref[jnp.maximum(b - 1, 0)]))
        def _():
            wup_bf_ref[...] = wup_ref[0, 0].astype(BF16)
            wdn_bf_ref[...] = wdn_ref[0, 0].astype(BF16)

        d_exp = wdn_ref.shape[2]
        x = _load_token_tiles(xb_ref, (), MOE_BLOCK)
        hu = jnp.dot(x.astype(BF16), wup_bf_ref[...], preferred_element_type=F32) + bup_ref[0, 0]
        glu = jnp.minimum(hu[:, :d_exp], SWIGLU_LIMIT)
        lin = jnp.clip(hu[:, d_exp:], -SWIGLU_LIMIT, SWIGLU_LIMIT)
        a = glu * _sigmoid(SWIGLU_ALPHA * glu) * (lin + 1.0)
        y = jnp.dot(a.astype(BF16), wdn_bf_ref[...], preferred_element_type=F32) + bdn_ref[0, 0]
        _store_token_tiles(yb_ref, (), y)

    @pl.when(b >= nu_ref[0])
    def _():
        yb_ref[...] = jnp.zeros_like(yb_ref)


def _experts(xb, block_e, n_used, layer, w_up, b_up, w_dn, b_dn):
    d = SUBLANES * LANES
    rows = MOE_BLOCK * SUBLANES
    n_blocks = xb.shape[0] // rows
    d_up = w_up.shape[3]
    d_exp = w_dn.shape[2]
    wsel = lambda b, be, nu: (layer, be[b], 0, 0)
    return pl.pallas_call(
        _expert_kernel,
        grid_spec=pltpu.PrefetchScalarGridSpec(
            num_scalar_prefetch=2,
            grid=(n_blocks,),
            in_specs=[pl.BlockSpec((rows, LANES), lambda b, be, nu: (jnp.minimum(b, nu[0] - 1), 0)),
                      pl.BlockSpec((1, 1, d, d_up), wsel),
                      pl.BlockSpec((1, 1, 1, d_up), wsel),
                      pl.BlockSpec((1, 1, d_exp, d), wsel),
                      pl.BlockSpec((1, 1, 1, d), wsel)],
            out_specs=pl.BlockSpec((rows, LANES), lambda b, be, nu: (b, 0)),
            scratch_shapes=[pltpu.VMEM((d, d_up), BF16), pltpu.VMEM((d_exp, d), BF16)]),
        out_shape=jax.ShapeDtypeStruct(xb.shape, F32),
        compiler_params=_cparams(("arbitrary",)),
        name="moe_experts",
    )(block_e, n_used, xb, w_up, b_up, w_dn, b_dn)


def _sc_gather_tiles(table, idx):
    m = idx.shape[0]
    mesh = plsc.VectorSubcoreMesh(core_axis_name="core", subcore_axis_name="subcore")

    @functools.partial(pl.kernel, out_type=jax.ShapeDtypeStruct((m, SUBLANES, LANES), table.dtype), mesh=mesh)
    def gather(table_hbm, idx_hbm, out_hbm):
        def window(idx_vmem, out_vmem):
            pltpu.sync_copy(table_hbm.at[idx_vmem.at[0, pl.ds(0, SC_WINDOW)]], out_vmem)

        pltpu.emit_pipeline(
            window,
            grid=(m // SC_WINDOW,),
            in_specs=[pl.BlockSpec((1, LANES), lambda i: (i, 0))],
            out_specs=[pl.BlockSpec((SC_WINDOW, SUBLANES, LANES), lambda i: (i, 0, 0))],
            core_axis_name=("core", "subcore"),
            dimension_semantics=(pltpu.PARALLEL,),
        )(idx_hbm, out_hbm)

    idx_rows = jnp.pad(idx.reshape(m // SC_WINDOW, SC_WINDOW), ((0, 0), (0, LANES - SC_WINDOW)))
    return gather(table, idx_rows)


def _combine_kernel(h_ref, y0_ref, y1_ref, y2_ref, y3_ref, gate_ref, pp_ref, ps_ref, wpg_ref, wpp_ref,
                    g_ref, b_ref, *outs, n_ptiles):
    outp_ref, outs_ref = outs[0], outs[-1]
    tm = h_ref.shape[0]
    gate = gate_ref[...]
    moe = _load_token_tiles(y0_ref, (), tm) * gate[:, 0:1]
    for k, y_ref in enumerate((y1_ref, y2_ref, y3_ref), start=1):
        moe = moe + _load_token_tiles(y_ref, (), tm) * gate[:, k:k + 1]
    h2 = _layer_norm(DN_ALPHA * h_ref[...] + moe, g_ref[...], b_ref[...])
    embed_gate = _sigmoid(_dot(h2, wpg_ref[...]))

    def finish(p_ref, out_ref):
        out_ref[...] = h2 + embed_gate * _dot(p_ref[...], wpp_ref[...])

    is_prompt = pl.program_id(0) < n_ptiles
    pl.when(is_prompt)(functools.partial(finish, pp_ref, outp_ref))
    pl.when(jnp.logical_not(is_prompt))(functools.partial(finish, ps_ref, outs_ref))


def _combine(h, y, gate, p_prompt, p_sample, layer, n_p, split, wpg_bf16, wpp_bf16, g_row, b_row):
    n, d = h.shape
    e = p_prompt.shape[1]
    n_tiles = n // TM_TOK
    n_ptiles = n_p // TM_TOK
    n_stiles = (n - n_p) // TM_TOK
    row = lambda i: (i, 0)
    row_p = lambda i: (jnp.minimum(i, n_ptiles - 1), 0)
    row_s = lambda i: (jnp.maximum(i - n_ptiles, 0), 0)
    fixed = lambda i: (0, 0)
    choice = lambda k: pl.BlockSpec((TM_TOK * SUBLANES, LANES), lambda i: (k * n_tiles + i, 0))
    if split:
        out_specs = [pl.BlockSpec((TM_TOK, d), row_p), pl.BlockSpec((TM_TOK, d), row_s)]
        out_shape = [jax.ShapeDtypeStruct((n_p, d), F32), jax.ShapeDtypeStruct((n - n_p, d), F32)]
    else:
        out_specs = pl.BlockSpec((TM_TOK, d), row)
        out_shape = jax.ShapeDtypeStruct((n, d), F32)
    return pl.pallas_call(
        functools.partial(_combine_kernel, n_ptiles=n_ptiles),
        grid=(n_tiles,),
        in_specs=[pl.BlockSpec((TM_TOK, d), row)] + [choice(k) for k in range(TOP_K)]
                 + [pl.BlockSpec((TM_TOK, LANES), row),
                    pl.BlockSpec((TM_TOK, e), lambda i: (layer * n_ptiles + row_p(i)[0], 0)),
                    pl.BlockSpec((TM_TOK, e), lambda i: (layer * n_stiles + row_s(i)[0], 0)),
                    pl.BlockSpec((d, d), fixed), pl.BlockSpec((e, d), fixed), pl.BlockSpec((1, d), fixed),
                    pl.BlockSpec((1, d), fixed)],
        out_specs=out_specs,
        out_shape=out_shape,
        compiler_params=_cparams(("arbitrary",)),
        name="moe_combine_ln_embed",
    )(h, y, y, y, y, gate, p_prompt, p_sample, wpg_bf16, wpp_bf16, g_row, b_row)


def _layer_tail(h, ht, p_prompt, p_sample, layer, n_p, split, g2, b2, w_r, b_r, w_up, b_up, w_dn, b_dn,
                w_pg, w_pp):
    n, d = h.shape
    wr_pad = jnp.pad(w_r, ((0, 0), (0, LANES - N_EXPERTS)))
    br_row = jnp.pad(b_r, (0, LANES - N_EXPERTS))[None]
    idx, gate, rank, cnt = _router(h, wr_pad, br_row)
    counts = cnt[0, :N_EXPERTS]
    padded = (counts + MOE_BLOCK - 1) // MOE_BLOCK * MOE_BLOCK
    pend = jnp.cumsum(padded).astype(I32)
    pstart = pend - padded
    slot = (pstart[idx[:, :TOP_K]] + rank[:, :TOP_K]).reshape(-1).astype(I32)
    n_asg = n * TOP_K
    n_blocks = n_asg // MOE_BLOCK + N_EXPERTS
    assert n_asg % MOE_BLOCK == 0 and n % TM_ROW == 0
    n_slots = n_blocks * MOE_BLOCK
    n_used = pend[-1] // MOE_BLOCK
    blk = jnp.minimum(jnp.arange(n_blocks, dtype=I32), n_used - 1) * MOE_BLOCK
    block_e = jnp.minimum(jnp.sum((pend[None, :] <= blk[:, None]).astype(I32), axis=1), N_EXPERTS - 1)
    xb = _dispatch(ht, slot, pend, n_slots)
    yb = _experts(xb, block_e, n_used[None], layer, w_up, b_up[:, :, None, :], w_dn, b_dn[:, :, None, :])
    slot_by_choice = slot.reshape(n, TOP_K).T.reshape(-1)
    y = _sc_gather_tiles(yb.reshape(n_slots, SUBLANES, LANES), slot_by_choice)
    return _combine(h, y.reshape(n_asg * SUBLANES, LANES), gate, p_prompt, p_sample, layer, n_p, split,
                    w_pg.astype(BF16), w_pp.astype(BF16), g2[None], b2[None])


def _lane_row(v, lane0):
    return jnp.zeros((1, LANES), F32).at[0, lane0:lane0 + v.shape[0]].set(v.astype(F32))


def kernel(x_prompt, x_sample, cache_fox_k, cache_fox_v, cache_fox_logf, state_gdn, state_gdn_conv,
           cache_pool, p_prompt, p_sample, w_in_ab, b_fgate, gdn_a_log, gdn_dt_bias, gdn_conv_w,
           gdn_norm_g, w_out_ab, pool_w, pool_scale, w_out_pool, ln1_g, ln1_b, ln2_g, ln2_b,
           w_router, b_router, w_expert_up, b_expert_up, w_expert_down, b_expert_down,
           w_ple_gate, w_ple_proj):
    n_pb, seq, d = x_prompt.shape
    n_sb, dseq, _ = x_sample.shape
    past = cache_fox_k.shape[2]
    assert n_pb == 1 and dseq == CHUNK and past % dseq == 0 and seq % TQ == 0
    assert d == SUBLANES * LANES
    n_p = n_pb * seq
    n_s = n_sb * dseq
    n = n_p + n_s
    assert n_p % TM_TOK == 0 and n_s % TM_TOK == 0
    n_layers = p_prompt.shape[0]
    pp_all = p_prompt.reshape(n_layers * n_p, -1)
    ps_all = p_sample.reshape(n_layers * n_s, -1)

    def tail(h, ht, i, split):
        return _layer_tail(h, ht, pp_all, ps_all, i, n_p, split, ln2_g[i], ln2_b[i], w_router[i], b_router[i],
                           w_expert_up, b_expert_up, w_expert_down, b_expert_down, w_ple_gate[i],
                           w_ple_proj[i])

    w_in = w_in_ab[0]
    n_small = 3 * N_HEADS
    ff0 = 3 * WIDTH
    gq0 = ff0 + N_HEADS
    ga0 = gq0 + 4 * WIDTH
    w_small = jnp.concatenate([w_in[:, ff0:gq0], w_in[:, ga0:ga0 + 2 * N_HEADS],
                               jnp.zeros((d, LANES - n_small), F32)], axis=1)
    w_all = jnp.concatenate([w_in[:, :ff0], w_in[:, gq0:ga0], w_small], axis=1)
    bf_row = _lane_row(b_fgate[0], LANE_F)
    up = _proj(x_prompt.reshape(n_p, d), w_all.astype(BF16), bf_row, U_COLS, False)
    us = _proj(x_sample.reshape(n_s, d), w_all, bf_row, LANES, True)

    u3 = up[None]
    us3 = us.reshape(n_sb, dseq, U_COLS)
    cq_p, ck_p = _cumsum(u3, 1, n_p, TK, COL_S // LANES)
    lf_s = jnp.concatenate(
        [jnp.pad(cache_fox_logf[0].astype(F32), ((0, 0), (0, 0), (0, LANES - N_HEADS))),
         us3[:, :, COL_S:]], axis=1)
    cq_s, ck_s = _cumsum(lf_s, n_sb, past + dseq, past + dseq, 0)

    of_p = _fox(u3, lambda b, i: (0, i, COL_Q // WIDTH), u3, lambda b, j: (0, j, COL_K // WIDTH),
                u3, lambda b, j: (0, j, COL_V // WIDTH), cq_p, lambda b, i: (0, i, 0),
                ck_p, lambda b, j: (0, 0, j), 1, n_p // TQ, TQ, TK, 0)
    k_all = jnp.concatenate([cache_fox_k[0].reshape(n_sb, past, WIDTH), us3[:, :, COL_K:COL_K + WIDTH]], axis=1)
    v_all = jnp.concatenate([cache_fox_v[0].reshape(n_sb, past, WIDTH), us3[:, :, COL_V:COL_V + WIDTH]], axis=1)
    of_s = _fox(us3, lambda b, i: (b, 0, COL_Q // WIDTH), k_all, lambda b, j: (b, 0, 0),
                v_all, lambda b, j: (b, 0, 0), cq_s, lambda b, i: (b, past // dseq, 0),
                ck_s, lambda b, j: (b, 0, 0), n_sb, 1, dseq, past + dseq, past, hi=True)

    gw = 3 * WIDTH
    conv_w = jnp.pad(gdn_conv_w[0], ((0, SUBLANES - CONV_WIDTH), (0, 0)))
    gdn_args = (conv_w, _lane_row(gdn_a_log[0], LANE_A), _lane_row(gdn_dt_bias[0], LANE_A),
                gdn_norm_g[0][None])
    n_pstep = n_p // GDN_ROWS
    ends = lambda k: (jnp.asarray((np.arange(k) == 0).astype(np.int32)),
                      jnp.asarray((np.arange(k) == k - 1).astype(np.int32)))
    og_p, st_p = _gdn(up.reshape(n_pstep, GDN_ROWS, U_COLS), False, *ends(n_pstep),
                      jnp.zeros((n_pstep,), I32), jnp.zeros((1, SUBLANES, gw), F32), gdn_args[0],
                      jnp.zeros((1, N_HEADS, HEAD_DIM, HEAD_DIM), F32), *gdn_args[1:])
    conv_past = jnp.pad(state_gdn_conv[0].astype(F32), ((0, 0), (SUBLANES - (CONV_WIDTH - 1), 0), (0, 0)))
    ones = jnp.ones((n_sb,), I32)
    og_s, st_s = _gdn(us3, True, ones, ones, jnp.arange(n_sb, dtype=I32), conv_past, gdn_args[0],
                      state_gdn[0].astype(F32), *gdn_args[1:])
    h, ht = _outproj_ln((x_prompt.reshape(n_p, d), of_p.reshape(n_p, WIDTH), og_p.reshape(n_p, WIDTH)),
                        (x_sample.reshape(n_s, d), of_s.reshape(n_s, WIDTH), og_s.reshape(n_s, WIDTH)),
                        w_out_ab[0], ln1_g[0][None], ln1_b[0][None])
    x1 = tail(h, ht, 0, False)

    pool_args = (pool_w[0].astype(BF16), pool_scale[0][None], w_out_pool[0].astype(BF16),
                 ln1_g[1][None], ln1_b[1][None])
    ratio = TM_TOK // POOL_HALO
    x1_halo = x1.reshape(n // POOL_HALO, POOL_HALO, d)
    h_p, ht_p = _pool_ln(x1, lambda i: (i, 0), x1_halo, lambda i: (jnp.maximum(i * ratio - 1, 0), 0, 0),
                         n_p // TM_TOK, TM_TOK, 0, True, *pool_args)
    cache16 = jnp.pad(cache_pool[0].astype(F32), ((0, 0), (POOL_HALO - POOL_STATE, 0), (0, 0)))
    h_s, ht_s = _pool_ln(x1, lambda i: (n_p // dseq + i, 0), cache16, lambda i: (i, 0, 0),
                         n_sb, dseq, past, False, *pool_args)
    x2_p, x2_s = tail(jnp.concatenate([h_p, h_s], axis=0), jnp.concatenate([ht_p, ht_s], axis=0), 1, True)

    return (x2_p.reshape(n_pb, seq, d), x2_s.reshape(n_sb, dseq, d),
            up[:, COL_K:COL_K + WIDTH].reshape(1, n_pb, seq, N_HEADS, HEAD_DIM),
            up[:, COL_V:COL_V + WIDTH].reshape(1, n_pb, seq, N_HEADS, HEAD_DIM),
            up[:, COL_S:COL_S + N_HEADS].reshape(1, n_pb, seq, N_HEADS),
            st_p.reshape(1, n_pb, N_HEADS, HEAD_DIM, HEAD_DIM),
            up[seq - (CONV_WIDTH - 1):, COL_G:COL_G + gw].reshape(1, n_pb, CONV_WIDTH - 1, gw),
            x1[n_p - POOL_STATE:n_p].reshape(1, n_pb, POOL_STATE, d),
            us[:, COL_K:COL_K + WIDTH].reshape(1, n_sb, dseq, N_HEADS, HEAD_DIM),
            us[:, COL_V:COL_V + WIDTH].reshape(1, n_sb, dseq, N_HEADS, HEAD_DIM),
            us[:, COL_S:COL_S + N_HEADS].reshape(1, n_sb, dseq, N_HEADS),
            st_s.reshape(1, n_sb, N_HEADS, HEAD_DIM, HEAD_DIM),
            us[:, COL_G:COL_G + gw].reshape(n_sb, dseq, gw)[:, dseq - (CONV_WIDTH - 1):].reshape(
                1, n_sb, CONV_WIDTH - 1, gw),
            x1[n_p:].reshape(n_sb, dseq, d)[:, dseq - POOL_STATE:].reshape(1, n_sb, POOL_STATE, d))
```

```python
import functools

import numpy as np
import jax
import jax.numpy as jnp
from jax import lax
from jax.experimental import pallas as pl
from jax.experimental.pallas import tpu as pltpu
from jax.experimental.pallas import tpu_sc as plsc

F32 = jnp.float32
BF16 = jnp.bfloat16
I32 = jnp.int32
HIGHEST = lax.Precision.HIGHEST

LANES = 128
SUBLANES = 8
VMEM_LIMIT = 56 * 1024 * 1024

HEAD_DIM = 128
N_HEADS = 4
WIDTH = N_HEADS * HEAD_DIM
CHUNK = 64
CONV_WIDTH = 4
POOL_WINDOWS = (2, 4, 8, 16)
POOL_HALO = 16
POOL_STATE = 15
N_EXPERTS = 32
TOP_K = 4
SWIGLU_LIMIT = 7.0
SWIGLU_ALPHA = 1.702
DEPTH = 2
DN_ALPHA = (2 * DEPTH) ** 0.25
LN_EPS = 1e-5
NORM_EPS = 1e-6
NEG_INF = -1e30
LOG2E = 1.4426950408889634

COL_Q, COL_K, COL_V = 0, WIDTH, 2 * WIDTH
COL_G = 3 * WIDTH
COL_Z = 6 * WIDTH
COL_S = 7 * WIDTH
U_COLS = COL_S + LANES
LANE_F, LANE_A, LANE_B = 0, N_HEADS, 2 * N_HEADS

TM_PROJ = 256
TM_TOK = 512
SC_WINDOW = 32
MOE_BLOCK = 512
TQ = 1024
TK = 512
GDN_ROWS = 256


def _cparams(sem):
    return pltpu.CompilerParams(dimension_semantics=sem, vmem_limit_bytes=VMEM_LIMIT)


def _softplus(x):
    return jnp.maximum(x, 0.0) + jnp.log1p(jnp.exp(-jnp.abs(x)))


def _sigmoid(x):
    return 1.0 / (1.0 + jnp.exp(-x))


def _silu(x):
    return x * _sigmoid(x)


def _layer_norm(y, g, b):
    mu = jnp.mean(y, axis=-1, keepdims=True)
    yc = y - mu
    var = jnp.mean(yc * yc, axis=-1, keepdims=True)
    return yc * lax.rsqrt(var + LN_EPS) * g + b


def _dot_general(a, b, dims, hi):
    if hi:
        return lax.dot_general(a.astype(F32), b.astype(F32), (dims, ((), ())), precision=HIGHEST,
                               preferred_element_type=F32)
    return lax.dot_general(a.astype(BF16), b.astype(BF16), (dims, ((), ())), preferred_element_type=F32)


def _dot(a, b, hi=False):
    return _dot_general(a, b, ((1,), (0,)), hi)


def _dot_nt(a, b, hi=False):
    return _dot_general(a, b, ((1,), (1,)), hi)


def _dot_tn(a, b, hi=False):
    return _dot_general(a, b, ((0,), (0,)), hi)


def _spread_lanes(x, width):
    if width % LANES == 0:
        return jnp.concatenate([x] * (width // LANES), axis=1)
    return jnp.broadcast_to(x[:, 0:1], (x.shape[0], width))


def _load_token_tiles(ref, lead, n_tok):
    return jnp.concatenate([ref[(*lead, pl.ds(j, n_tok, stride=SUBLANES), slice(None))]
                            for j in range(SUBLANES)], axis=1)


def _store_token_tiles(ref, lead, x):
    for j in range(SUBLANES):
        ref[(*lead, pl.ds(j, x.shape[0], stride=SUBLANES), slice(None))] = x[:, j * LANES:(j + 1) * LANES]


def _lanes_to_rows(x, lane0):
    r = lax.broadcasted_iota(I32, (SUBLANES, LANES), 0)
    c = lax.broadcasted_iota(I32, (SUBLANES, LANES), 1)
    sel = (c == r + lane0).astype(F32)
    return lax.dot_general(sel, x, (((1,), (1,)), ((), ())), precision=HIGHEST,
                           preferred_element_type=F32)


def _proj_kernel(x_ref, w_ref, bf_ref, u_ref, *, hi):
    u = _dot(x_ref[...], w_ref[...], hi)
    u_ref[...] = u

    @pl.when(pl.program_id(1) == pl.num_programs(1) - 1)
    def _():
        small = u[:, u.shape[1] - LANES:]
        lane = lax.broadcasted_iota(I32, small.shape, 1)
        logf = -_softplus(-(small + bf_ref[...]))
        u_ref[:, u.shape[1] - LANES:] = jnp.where(lane < LANE_A, logf, small)


def _proj(x, w, bf_row, tn, hi):
    n, d = x.shape
    m = w.shape[1]
    return pl.pallas_call(
        functools.partial(_proj_kernel, hi=hi),
        grid=(n // TM_PROJ, m // tn),
        in_specs=[pl.BlockSpec((TM_PROJ, d), lambda i, j: (i, 0)),
                  pl.BlockSpec((d, tn), lambda i, j: (0, j)),
                  pl.BlockSpec((1, LANES), lambda i, j: (0, 0))],
        out_specs=pl.BlockSpec((TM_PROJ, tn), lambda i, j: (i, j)),
        out_shape=jax.ShapeDtypeStruct((n, m), F32),
        compiler_params=_cparams(("parallel", "parallel")),
        name="in_proj",
    )(x, w, bf_row)


def _cumsum_kernel(lf_ref, crep_ref, crow_ref, carry_ref):
    @pl.when(pl.program_id(1) == 0)
    def _():
        carry_ref[...] = jnp.zeros_like(carry_ref)

    lf = lf_ref[0]
    t = lf.shape[0]
    r = lax.broadcasted_iota(I32, (t, t), 0)
    c = lax.broadcasted_iota(I32, (t, t), 1)
    tril = (c <= r).astype(F32)
    cs = jnp.dot(tril, lf, precision=HIGHEST, preferred_element_type=F32) + carry_ref[0:1, :]
    carry_ref[...] = jnp.broadcast_to(cs[t - 1:t, :], carry_ref.shape)
    c2 = cs * LOG2E
    crow_ref[0] = _lanes_to_rows(c2, LANE_F)
    for h in range(N_HEADS):
        crep_ref[0, :, h * HEAD_DIM:(h + 1) * HEAD_DIM] = jnp.broadcast_to(
            c2[:, LANE_F + h:LANE_F + h + 1], (t, HEAD_DIM))


def _cumsum(arr, n_batch, length, tl, col_block):
    return pl.pallas_call(
        _cumsum_kernel,
        grid=(n_batch, length // tl),
        in_specs=[pl.BlockSpec((1, tl, LANES), lambda b, j: (b, j, col_block))],
        out_specs=[pl.BlockSpec((1, tl, WIDTH), lambda b, j: (b, j, 0)),
                   pl.BlockSpec((1, SUBLANES, tl), lambda b, j: (b, 0, j))],
        out_shape=[jax.ShapeDtypeStruct((n_batch, length, WIDTH), F32),
                   jax.ShapeDtypeStruct((n_batch, SUBLANES, length), F32)],
        scratch_shapes=[pltpu.VMEM((SUBLANES, LANES), F32)],
        compiler_params=_cparams(("parallel", "arbitrary")),
        name="logf_cumsum",
    )(arr)


def _fox_kernel(qi_ref, kj_ref, last_ref, q_ref, k_ref, v_ref, cq_ref, ck_ref, o_ref,
                m_ref, l_ref, acc_ref, *, tq, tk, past, hi):
    s_idx = pl.program_id(1)
    qi = qi_ref[s_idx]
    kj = kj_ref[s_idx]

    @pl.when(kj == 0)
    def _():
        m_ref[...] = jnp.full_like(m_ref, NEG_INF)
        l_ref[...] = jnp.zeros_like(l_ref)
        acc_ref[...] = jnp.zeros_like(acc_ref)

    def update(masked):
        if masked:
            q_pos = past + qi * tq + lax.broadcasted_iota(I32, (tq, tk), 0)
            k_pos = kj * tk + lax.broadcasted_iota(I32, (tq, tk), 1)
            visible = k_pos <= q_pos
        for h in range(N_HEADS):
            cols = slice(h * HEAD_DIM, (h + 1) * HEAD_DIM)
            q = q_ref[0, :, cols] * (HEAD_DIM ** -0.5 * LOG2E)
            t = _dot_nt(q, k_ref[0, :, cols], hi) - ck_ref[0, h:h + 1, :]
            if masked:
                t = jnp.where(visible, t, NEG_INF)
            cq = cq_ref[0, :, cols]
            m_prev = m_ref[h]
            m_new = jnp.maximum(m_prev, jnp.max(t, axis=-1, keepdims=True) + cq)
            p = jnp.exp2(t - _spread_lanes(m_new - cq, tk))
            alpha = jnp.exp2(m_prev - m_new)
            l_ref[h] = alpha * l_ref[h] + jnp.sum(p, axis=-1, keepdims=True)
            acc_ref[:, cols] = alpha * acc_ref[:, cols] + _dot(p, v_ref[0, :, cols], hi)
            m_ref[h] = m_new

    crosses_diagonal = kj * tk + (tk - 1) > past + qi * tq
    pl.when(crosses_diagonal)(functools.partial(update, True))
    pl.when(jnp.logical_not(crosses_diagonal))(functools.partial(update, False))

    @pl.when(last_ref[s_idx] == 1)
    def _():
        for h in range(N_HEADS):
            cols = slice(h * HEAD_DIM, (h + 1) * HEAD_DIM)
            o_ref[0, :, cols] = acc_ref[:, cols] / l_ref[h]


def _fox_schedule(n_q, tq, tk, past):
    qi, kj, last = [], [], []
    for i in range(n_q):
        hi = (past + (i + 1) * tq - 1) // tk
        for j in range(hi + 1):
            qi.append(i)
            kj.append(j)
            last.append(1 if j == hi else 0)
    return (jnp.asarray(np.array(qi, np.int32)), jnp.asarray(np.array(kj, np.int32)),
            jnp.asarray(np.array(last, np.int32)))


def _fox(q_arr, q_map, k_arr, k_map, v_arr, v_map, cq_arr, cq_map, ck_arr, ck_map,
         n_batch, n_q, tq, tk, past, hi=False):
    qi, kj, last = _fox_schedule(n_q, tq, tk, past)
    n_steps = int(qi.shape[0])
    spec = lambda shape, fn, tab: pl.BlockSpec(shape, lambda b, s, qi_r, kj_r, la_r: fn(b, (qi_r if tab == 'q' else kj_r)[s]))
    return pl.pallas_call(
        functools.partial(_fox_kernel, tq=tq, tk=tk, past=past, hi=hi),
        grid_spec=pltpu.PrefetchScalarGridSpec(
            num_scalar_prefetch=3,
            grid=(n_batch, n_steps),
            in_specs=[spec((1, tq, WIDTH), q_map, 'q'),
                      spec((1, tk, WIDTH), k_map, 'k'),
                      spec((1, tk, WIDTH), v_map, 'k'),
                      spec((1, tq, WIDTH), cq_map, 'q'),
                      spec((1, SUBLANES, tk), ck_map, 'k')],
            out_specs=spec((1, tq, WIDTH), lambda b, i: (b, i, 0), 'q'),
            scratch_shapes=[pltpu.VMEM((N_HEADS, tq, HEAD_DIM), F32),
                            pltpu.VMEM((N_HEADS, tq, HEAD_DIM), F32),
                            pltpu.VMEM((tq, WIDTH), F32)]),
        out_shape=jax.ShapeDtypeStruct((n_batch, n_q * tq, WIDTH), F32),
        compiler_params=_cparams(("parallel", "arbitrary")),
        name="fox_attention",
    )(qi, kj, last, q_arr, k_arr, v_arr, cq_arr, ck_arr)


def _gdn_kernel(first_ref, last_ref, seq_ref,
                pre_ref, z_ref, sm_ref, cpast_ref, convw_ref, s0_ref, alog_ref, dtb_ref, ng_ref,
                o_ref, sout_ref, stage_ref, s_ref, *, rows, hi):
    dot, dot_nt, dot_tn = (functools.partial(f, hi=hi) for f in (_dot, _dot_nt, _dot_tn))
    step = pl.program_id(0)
    halo = SUBLANES
    n_chunks = rows // CHUNK

    @pl.when(first_ref[step] == 1)
    def _():
        stage_ref[0:halo, :] = cpast_ref[0]
        s_ref[...] = s0_ref[0]

    stage_ref[halo:halo + rows, :] = pre_ref[0]
    conv = stage_ref[halo:halo + rows, :] * convw_ref[CONV_WIDTH - 1:CONV_WIDTH, :]
    for j in range(1, CONV_WIDTH):
        conv = conv + (stage_ref[halo - j:halo - j + rows, :]
                       * convw_ref[CONV_WIDTH - 1 - j:CONV_WIDTH - j, :])
    stage_ref[0:halo, :] = stage_ref[rows:rows + halo, :]
    act = _silu(conv)

    small = sm_ref[0]
    beta_all = _sigmoid(small)
    g_all = -jnp.exp(alog_ref[...]) * _softplus(small + dtb_ref[...])
    r = lax.broadcasted_iota(I32, (rows, rows), 0)
    c = lax.broadcasted_iota(I32, (rows, rows), 1)
    same_chunk = (r // CHUNK) == (c // CHUNK)
    incl = same_chunk & (c <= r)
    strict = same_chunk & (c < r)
    eye = (c == r).astype(F32)
    gc_all = jnp.dot(incl.astype(F32), g_all, precision=HIGHEST, preferred_element_type=F32)
    gc_rows = _lanes_to_rows(gc_all, LANE_A)

    for h in range(N_HEADS):
        cols = slice(h * HEAD_DIM, (h + 1) * HEAD_DIM)
        q = act[:, h * HEAD_DIM:(h + 1) * HEAD_DIM]
        k = act[:, WIDTH + h * HEAD_DIM:WIDTH + (h + 1) * HEAD_DIM]
        v = act[:, 2 * WIDTH + h * HEAD_DIM:2 * WIDTH + (h + 1) * HEAD_DIM]
        q = q * lax.rsqrt(jnp.sum(q * q, axis=-1, keepdims=True) + NORM_EPS) * (HEAD_DIM ** -0.5)
        k = k * lax.rsqrt(jnp.sum(k * k, axis=-1, keepdims=True) + NORM_EPS)
        beta = beta_all[:, LANE_B + h:LANE_B + h + 1]
        gc = gc_all[:, LANE_A + h:LANE_A + h + 1]
        diff = gc - gc_rows[h:h + 1, :]
        decay = jnp.where(incl, jnp.exp(jnp.where(incl, diff, 0.0)), 0.0)
        kb = k * beta
        vb = v * beta
        low = jnp.where(strict, dot_nt(kb, k) * decay, 0.0)
        inv = eye - low
        pw = dot(low, low)
        n_sq = CHUNK.bit_length() - 2
        for it in range(n_sq):
            inv = inv + dot(inv, pw)
            if it + 1 < n_sq:
                pw = dot(pw, pw)
        egc = jnp.exp(gc)
        uw = dot(inv, jnp.concatenate([vb, kb * egc], axis=1))
        intra = jnp.where(incl, dot_nt(q, k) * decay, 0.0)
        qd = q * egc
        g_last = [gc[(g + 1) * CHUNK - 1:(g + 1) * CHUNK, :] for g in range(n_chunks)]
        kd = k * jnp.exp(jnp.concatenate([jnp.broadcast_to(gl, (CHUNK, 1)) for gl in g_last], axis=0)
                         - gc)
        state = s_ref[h]
        v_new = []
        for g in range(n_chunks):
            rs = slice(g * CHUNK, (g + 1) * CHUNK)
            v_new.append(uw[rs, :HEAD_DIM] - dot(uw[rs, HEAD_DIM:], state))
            v_rows = jnp.concatenate(
                v_new + [jnp.zeros((rows - (g + 1) * CHUNK, HEAD_DIM), F32)] * (g + 1 < n_chunks), axis=0)
            o = dot(qd[rs, :], state) + dot(intra[rs, :], v_rows)
            state = state * jnp.exp(g_last[g]) + dot_tn(kd[rs, :], v_new[g])
            o = (o * lax.rsqrt(jnp.mean(o * o, axis=-1, keepdims=True) + NORM_EPS)
                 * ng_ref[...] * _silu(z_ref[0, rs, cols]))
            o_ref[0, rs, cols] = o
        s_ref[h] = state

    @pl.when(last_ref[step] == 1)
    def _():
        sout_ref[0] = s_ref[...]


def _gdn(u_view, hi, first, last, seq, conv_past, conv_w, s0, alog_row, dtb_row, ng_row):
    n_steps, rows, _ = u_view.shape
    n_seq = s0.shape[0]
    gw = 3 * WIDTH
    return pl.pallas_call(
        functools.partial(_gdn_kernel, rows=rows, hi=hi),
        grid_spec=pltpu.PrefetchScalarGridSpec(
            num_scalar_prefetch=3,
            grid=(n_steps,),
            in_specs=[pl.BlockSpec((1, rows, gw), lambda s, f, l, q: (s, 0, COL_G // gw)),
                      pl.BlockSpec((1, rows, WIDTH), lambda s, f, l, q: (s, 0, COL_Z // WIDTH)),
                      pl.BlockSpec((1, rows, LANES), lambda s, f, l, q: (s, 0, COL_S // LANES)),
                      pl.BlockSpec((1, SUBLANES, gw), lambda s, f, l, q: (q[s], 0, 0)),
                      pl.BlockSpec((SUBLANES, gw), lambda s, f, l, q: (0, 0)),
                      pl.BlockSpec((1, N_HEADS, HEAD_DIM, HEAD_DIM), lambda s, f, l, q: (q[s], 0, 0, 0)),
                      pl.BlockSpec((1, LANES), lambda s, f, l, q: (0, 0)),
                      pl.BlockSpec((1, LANES), lambda s, f, l, q: (0, 0)),
                      pl.BlockSpec((1, LANES), lambda s, f, l, q: (0, 0))],
            out_specs=[pl.BlockSpec((1, rows, WIDTH), lambda s, f, l, q: (s, 0, 0)),
                       pl.BlockSpec((1, N_HEADS, HEAD_DIM, HEAD_DIM), lambda s, f, l, q: (q[s], 0, 0, 0))],
            scratch_shapes=[pltpu.VMEM((rows + SUBLANES, gw), F32),
                            pltpu.VMEM((N_HEADS, HEAD_DIM, HEAD_DIM), F32)]),
        out_shape=[jax.ShapeDtypeStruct((n_steps, rows, WIDTH), F32),
                   jax.ShapeDtypeStruct((n_seq, N_HEADS, HEAD_DIM, HEAD_DIM), F32)],
        compiler_params=_cparams(("arbitrary",)),
        name="gated_deltanet",
    )(first, last, seq, u_view, u_view, u_view, conv_past, conv_w, s0, alog_row, dtb_row, ng_row)


def _outproj_ln_kernel(xp_ref, ofp_ref, ogp_ref, xs_ref, ofs_ref, ogs_ref, w_ref, wf_ref, g_ref, b_ref,
                       h_ref, ht_ref, *, hi_from):
    def run(hi, x_ref, of_ref, og_ref):
        w = wf_ref if hi else w_ref
        mix = _dot(of_ref[...], w[0:WIDTH, :], hi) + _dot(og_ref[...], w[WIDTH:2 * WIDTH, :], hi)
        h = _layer_norm(DN_ALPHA * x_ref[...] + mix, g_ref[...], b_ref[...])
        h_ref[...] = h
        _store_token_tiles(ht_ref, (), h)

    pl.when(pl.program_id(0) < hi_from)(functools.partial(run, False, xp_ref, ofp_ref, ogp_ref))
    pl.when(pl.program_id(0) >= hi_from)(functools.partial(run, True, xs_ref, ofs_ref, ogs_ref))


def _outproj_ln(prompt, sample, w, g_row, b_row):
    n_p, d = prompt[0].shape
    n = n_p + sample[0].shape[0]
    hi_from = n_p // TM_TOK
    row = lambda i: (i, 0)
    row_p = lambda i: (jnp.minimum(i, hi_from - 1), 0)
    row_s = lambda i: (jnp.maximum(i - hi_from, 0), 0)
    fixed = lambda i: (0, 0)
    group = lambda rows: [pl.BlockSpec((TM_TOK, d), rows), pl.BlockSpec((TM_TOK, WIDTH), rows),
                          pl.BlockSpec((TM_TOK, WIDTH), rows)]
    return pl.pallas_call(
        functools.partial(_outproj_ln_kernel, hi_from=hi_from),
        grid=(n // TM_TOK,),
        in_specs=group(row_p) + group(row_s) + [pl.BlockSpec((2 * WIDTH, d), fixed),
                                                pl.BlockSpec((2 * WIDTH, d), fixed),
                                                pl.BlockSpec((1, d), fixed), pl.BlockSpec((1, d), fixed)],
        out_specs=[pl.BlockSpec((TM_TOK, d), row), pl.BlockSpec((TM_TOK * SUBLANES, LANES), row)],
        out_shape=[jax.ShapeDtypeStruct((n, d), F32), jax.ShapeDtypeStruct((n * SUBLANES, LANES), F32)],
        compiler_params=_cparams(("parallel",)),
        name="out_proj_ln",
    )(*prompt, *sample, w.astype(BF16), w, g_row, b_row)


def _pool_ln_kernel(x_ref, halo_ref, pw_ref, ps_ref, w_ref, g_ref, b_ref, h_ref, ht_ref, stage_ref,
                    *, tm, pos0, zero_first_halo):
    i = pl.program_id(0)
    stage_ref[0:POOL_HALO, :] = halo_ref[0]
    if zero_first_halo:
        @pl.when(i == 0)
        def _():
            stage_ref[0:POOL_HALO, :] = jnp.zeros((POOL_HALO, stage_ref.shape[1]), F32)
    x = x_ref[...]
    stage_ref[POOL_HALO:POOL_HALO + tm, :] = x
    gdim = x.shape[1] // len(POOL_WINDOWS)
    pos = pos0 + lax.broadcasted_iota(I32, (tm, 1), 0)
    if zero_first_halo:
        pos = pos + i * tm
    parts = []
    for gi, win in enumerate(POOL_WINDOWS):
        cols = slice(gi * gdim, (gi + 1) * gdim)
        s = stage_ref[POOL_HALO:POOL_HALO + tm, cols]
        for j in range(1, win):
            s = s + stage_ref[POOL_HALO - j:POOL_HALO - j + tm, cols]
        cnt = jnp.minimum(pos + 1, win).astype(F32)
        zg = s / cnt - x[:, cols]
        parts.append(_dot(zg, pw_ref[gi]))
    zg = jnp.concatenate(parts, axis=-1) * ps_ref[...]
    mix = _dot(zg, w_ref[...])
    h = _layer_norm(DN_ALPHA * x + mix, g_ref[...], b_ref[...])
    h_ref[...] = h
    _store_token_tiles(ht_ref, (), h)


def _pool_ln(x, x_map, halo_arr, halo_map, n_tiles, tm, pos0, zero_first_halo,
             pw_bf16, ps_row, w_bf16, g_row, b_row):
    d = x.shape[1]
    gdim = d // len(POOL_WINDOWS)
    fixed = lambda i: (0, 0)
    return pl.pallas_call(
        functools.partial(_pool_ln_kernel, tm=tm, pos0=pos0, zero_first_halo=zero_first_halo),
        grid=(n_tiles,),
        in_specs=[pl.BlockSpec((tm, d), x_map),
                  pl.BlockSpec((1, POOL_HALO, d), halo_map),
                  pl.BlockSpec((len(POOL_WINDOWS), gdim, gdim), lambda i: (0, 0, 0)),
                  pl.BlockSpec((1, d), fixed), pl.BlockSpec((d, d), fixed),
                  pl.BlockSpec((1, d), fixed), pl.BlockSpec((1, d), fixed)],
        out_specs=[pl.BlockSpec((tm, d), lambda i: (i, 0)),
                   pl.BlockSpec((tm * SUBLANES, LANES), lambda i: (i, 0))],
        out_shape=[jax.ShapeDtypeStruct((n_tiles * tm, d), F32),
                   jax.ShapeDtypeStruct((n_tiles * tm * SUBLANES, LANES), F32)],
        scratch_shapes=[pltpu.VMEM((POOL_HALO + tm, d), F32)],
        compiler_params=_cparams(("arbitrary",)),
        name="pool_mixer_ln",
    )(x, halo_arr, pw_bf16, ps_row, w_bf16, g_row, b_row)


def _router_kernel(h_ref, wr_ref, br_ref, idx_ref, gate_ref, rank_ref, cnt_ref, carry_ref):
    @pl.when(pl.program_id(0) == 0)
    def _():
        carry_ref[...] = jnp.zeros_like(carry_ref)

    tm = h_ref.shape[0]
    lane = lax.broadcasted_iota(I32, (tm, LANES), 1).astype(F32)
    logits = jnp.dot(h_ref[...], wr_ref[...], precision=HIGHEST, preferred_element_type=F32)
    work = jnp.where(lane < N_EXPERTS, logits + br_ref[...], -jnp.inf)
    vals, ids = [], []
    for _ in range(TOP_K):
        m = jnp.max(work, axis=-1, keepdims=True)
        ik = jnp.min(jnp.where(work == m, lane, float(LANES)), axis=-1, keepdims=True)
        vals.append(m)
        ids.append(ik)
        work = jnp.where(lane == ik, -jnp.inf, work)
    exps = [jnp.exp(v - vals[0]) for v in vals]
    denom = exps[0]
    for e in exps[1:]:
        denom = denom + e
    multihot = jnp.zeros((tm, LANES), F32)
    idx_out = jnp.zeros((tm, LANES), F32)
    gate_out = jnp.zeros((tm, LANES), F32)
    for k in range(TOP_K):
        multihot = multihot + (lane == ids[k]).astype(F32)
        idx_out = jnp.where(lane == k, ids[k], idx_out)
        gate_out = jnp.where(lane == k, exps[k] / denom, gate_out)
    r = lax.broadcasted_iota(I32, (tm, tm), 0)
    c = lax.broadcasted_iota(I32, (tm, tm), 1)
    before = _dot((c < r).astype(F32), multihot) + carry_ref[0:1, :]
    rank_out = jnp.zeros((tm, LANES), F32)
    for k in range(TOP_K):
        rk = jnp.sum(jnp.where(lane == ids[k], before, 0.0), axis=-1, keepdims=True)
        rank_out = jnp.where(lane == k, rk, rank_out)
    idx_ref[...] = idx_out.astype(I32)
    gate_ref[...] = gate_out
    rank_ref[...] = rank_out.astype(I32)
    total = carry_ref[0:1, :] + jnp.sum(multihot, axis=0, keepdims=True)
    carry_ref[...] = jnp.broadcast_to(total, carry_ref.shape)
    cnt_ref[...] = jnp.broadcast_to(total, cnt_ref.shape).astype(I32)


def _router(h, wr_pad, br_row):
    n, d = h.shape
    row = lambda i: (i, 0)
    fixed = lambda i: (0, 0)
    return pl.pallas_call(
        _router_kernel,
        grid=(n // TM_TOK,),
        in_specs=[pl.BlockSpec((TM_TOK, d), row), pl.BlockSpec((d, LANES), fixed),
                  pl.BlockSpec((1, LANES), fixed)],
        out_specs=[pl.BlockSpec((TM_TOK, LANES), row), pl.BlockSpec((TM_TOK, LANES), row),
                   pl.BlockSpec((TM_TOK, LANES), row), pl.BlockSpec((SUBLANES, LANES), fixed)],
        out_shape=[jax.ShapeDtypeStruct((n, LANES), I32), jax.ShapeDtypeStruct((n, LANES), F32),
                   jax.ShapeDtypeStruct((n, LANES), I32), jax.ShapeDtypeStruct((SUBLANES, LANES), I32)],
        scratch_shapes=[pltpu.VMEM((SUBLANES, LANES), F32)],
        compiler_params=_cparams(("arbitrary",)),
        name="moe_router",
    )(h, wr_pad, br_row)


def _expert_kernel(be_ref, nu_ref, valid_ref, xb_ref, wup_ref, bup_ref, wdn_ref, bdn_ref, yb_ref,
                   wup_bf_ref, wdn_bf_ref):
    b = pl.program_id(0)

    @pl.when(b < nu_ref[0])
    def _():
        @pl.when((b == 0) | (be_ref[b] != be_ref[jnp.maximum(b - 1, 0)]))
        def _():
            wup_bf_ref[...] = wup_ref[0, 0].astype(BF16)
            wdn_bf_ref[...] = wdn_ref[0, 0].astype(BF16)

        d_exp = wdn_ref.shape[2]
        x = _load_token_tiles(xb_ref, (), MOE_BLOCK)
        row = lax.broadcasted_iota(I32, (MOE_BLOCK, 1), 0)
        x = jnp.where(row < valid_ref[b], x, 0.0)
        hu = jnp.dot(x.astype(BF16), wup_bf_ref[...], preferred_element_type=F32) + bup_ref[0, 0]
        glu = jnp.minimum(hu[:, :d_exp], SWIGLU_LIMIT)
        lin = jnp.clip(hu[:, d_exp:], -SWIGLU_LIMIT, SWIGLU_LIMIT)
        a = glu * _sigmoid(SWIGLU_ALPHA * glu) * (lin + 1.0)
        y = jnp.dot(a.astype(BF16), wdn_bf_ref[...], preferred_element_type=F32) + bdn_ref[0, 0]
        _store_token_tiles(yb_ref, (), y)

    @pl.when(b >= nu_ref[0])
    def _():
        yb_ref[...] = jnp.zeros_like(yb_ref)


def _experts(xb, block_e, n_used, block_valid, layer, w_up, b_up, w_dn, b_dn):
    d = SUBLANES * LANES
    rows = MOE_BLOCK * SUBLANES
    n_blocks = xb.shape[0] // rows
    d_up = w_up.shape[3]
    d_exp = w_dn.shape[2]
    wsel = lambda b, be, nu, va: (layer, be[b], 0, 0)
    return pl.pallas_call(
        _expert_kernel,
        grid_spec=pltpu.PrefetchScalarGridSpec(
            num_scalar_prefetch=3,
            grid=(n_blocks,),
            in_specs=[pl.BlockSpec((rows, LANES), lambda b, be, nu, va: (jnp.minimum(b, nu[0] - 1), 0)),
                      pl.BlockSpec((1, 1, d, d_up), wsel),
                      pl.BlockSpec((1, 1, 1, d_up), wsel),
                      pl.BlockSpec((1, 1, d_exp, d), wsel),
                      pl.BlockSpec((1, 1, 1, d), wsel)],
            out_specs=pl.BlockSpec((rows, LANES), lambda b, be, nu, va: (b, 0)),
            scratch_shapes=[pltpu.VMEM((d, d_up), BF16), pltpu.VMEM((d_exp, d), BF16)]),
        out_shape=jax.ShapeDtypeStruct(xb.shape, F32),
        compiler_params=_cparams(("arbitrary",)),
        name="moe_experts",
    )(block_e, n_used, block_valid, xb, w_up, b_up, w_dn, b_dn)


def _sc_gather_tiles(table, idx):
    m = idx.shape[0]
    mesh = plsc.VectorSubcoreMesh(core_axis_name="core", subcore_axis_name="subcore")

    @functools.partial(pl.kernel, out_type=jax.ShapeDtypeStruct((m, SUBLANES, LANES), table.dtype), mesh=mesh)
    def gather(table_hbm, idx_hbm, out_hbm):
        def window(idx_vmem, out_vmem):
            pltpu.sync_copy(table_hbm.at[idx_vmem.at[0, pl.ds(0, SC_WINDOW)]], out_vmem)

        pltpu.emit_pipeline(
            window,
            grid=(m // SC_WINDOW,),
            in_specs=[pl.BlockSpec((1, LANES), lambda i: (i, 0))],
            out_specs=[pl.BlockSpec((SC_WINDOW, SUBLANES, LANES), lambda i: (i, 0, 0))],
            core_axis_name=("core", "subcore"),
            dimension_semantics=(pltpu.PARALLEL,),
        )(idx_hbm, out_hbm)

    idx_rows = jnp.pad(idx.reshape(m // SC_WINDOW, SC_WINDOW), ((0, 0), (0, LANES - SC_WINDOW)))
    return gather(table, idx_rows)


def _sc_scatter_tiles(tiles, idx_by_choice, n_out):
    n = tiles.shape[0]
    mesh = plsc.VectorSubcoreMesh(core_axis_name="core", subcore_axis_name="subcore")

    @functools.partial(pl.kernel, out_type=jax.ShapeDtypeStruct((n_out, SUBLANES, LANES), tiles.dtype),
                       mesh=mesh)
    def scatter(tiles_hbm, *refs):
        idx_hbm, out_hbm = refs[:TOP_K], refs[TOP_K]

        def window(tiles_vmem, *idx_vmem):
            for k in range(TOP_K):
                pltpu.sync_copy(tiles_vmem, out_hbm.at[idx_vmem[k].at[0, pl.ds(0, SC_WINDOW)]])

        pltpu.emit_pipeline(
            window,
            grid=(n // SC_WINDOW,),
            in_specs=[pl.BlockSpec((SC_WINDOW, SUBLANES, LANES), lambda i: (i, 0, 0))]
                     + [pl.BlockSpec((1, LANES), lambda i: (i, 0))] * TOP_K,
            out_specs=[],
            core_axis_name=("core", "subcore"),
            dimension_semantics=(pltpu.PARALLEL,),
        )(tiles_hbm, *idx_hbm)

    idx_rows = jnp.pad(idx_by_choice.reshape(TOP_K, n // SC_WINDOW, SC_WINDOW),
                       ((0, 0), (0, 0), (0, LANES - SC_WINDOW)))
    return scatter(tiles, *[idx_rows[k] for k in range(TOP_K)])


def _combine_kernel(h_ref, y0_ref, y1_ref, y2_ref, y3_ref, gate_ref, pp_ref, ps_ref, wpg_ref, wpp_ref,
                    g_ref, b_ref, *outs, n_ptiles):
    outp_ref, outs_ref = outs[0], outs[-1]
    tm = h_ref.shape[0]
    gate = gate_ref[...]
    moe = _load_token_tiles(y0_ref, (), tm) * gate[:, 0:1]
    for k, y_ref in enumerate((y1_ref, y2_ref, y3_ref), start=1):
        moe = moe + _load_token_tiles(y_ref, (), tm) * gate[:, k:k + 1]
    h2 = _layer_norm(DN_ALPHA * h_ref[...] + moe, g_ref[...], b_ref[...])
    embed_gate = _sigmoid(_dot(h2, wpg_ref[...]))

    def finish(p_ref, out_ref):
        out_ref[...] = h2 + embed_gate * _dot(p_ref[...], wpp_ref[...])

    is_prompt = pl.program_id(0) < n_ptiles
    pl.when(is_prompt)(functools.partial(finish, pp_ref, outp_ref))
    pl.when(jnp.logical_not(is_prompt))(functools.partial(finish, ps_ref, outs_ref))


def _combine(h, y, gate, p_prompt, p_sample, layer, n_p, split, wpg_bf16, wpp_bf16, g_row, b_row):
    n, d = h.shape
    e = p_prompt.shape[1]
    n_tiles = n // TM_TOK
    n_ptiles = n_p // TM_TOK
    n_stiles = (n - n_p) // TM_TOK
    row = lambda i: (i, 0)
    row_p = lambda i: (jnp.minimum(i, n_ptiles - 1), 0)
    row_s = lambda i: (jnp.maximum(i - n_ptiles, 0), 0)
    fixed = lambda i: (0, 0)
    choice = lambda k: pl.BlockSpec((TM_TOK * SUBLANES, LANES), lambda i: (k * n_tiles + i, 0))
    if split:
        out_specs = [pl.BlockSpec((TM_TOK, d), row_p), pl.BlockSpec((TM_TOK, d), row_s)]
        out_shape = [jax.ShapeDtypeStruct((n_p, d), F32), jax.ShapeDtypeStruct((n - n_p, d), F32)]
    else:
        out_specs = pl.BlockSpec((TM_TOK, d), row)
        out_shape = jax.ShapeDtypeStruct((n, d), F32)
    return pl.pallas_call(
        functools.partial(_combine_kernel, n_ptiles=n_ptiles),
        grid=(n_tiles,),
        in_specs=[pl.BlockSpec((TM_TOK, d), row)] + [choice(k) for k in range(TOP_K)]
                 + [pl.BlockSpec((TM_TOK, LANES), row),
                    pl.BlockSpec((TM_TOK, e), lambda i: (layer * n_ptiles + row_p(i)[0], 0)),
                    pl.BlockSpec((TM_TOK, e), lambda i: (layer * n_stiles + row_s(i)[0], 0)),
                    pl.BlockSpec((d, d), fixed), pl.BlockSpec((e, d), fixed), pl.BlockSpec((1, d), fixed),
                    pl.BlockSpec((1, d), fixed)],
        out_specs=out_specs,
        out_shape=out_shape,
        compiler_params=_cparams(("arbitrary",)),
        name="moe_combine_ln_embed",
    )(h, y, y, y, y, gate, p_prompt, p_sample, wpg_bf16, wpp_bf16, g_row, b_row)


def _layer_tail(h, ht, p_prompt, p_sample, layer, n_p, split, g2, b2, w_r, b_r, w_up, b_up, w_dn, b_dn,
                w_pg, w_pp):
    n, d = h.shape
    wr_pad = jnp.pad(w_r, ((0, 0), (0, LANES - N_EXPERTS)))
    br_row = jnp.pad(b_r, (0, LANES - N_EXPERTS))[None]
    idx, gate, rank, cnt = _router(h, wr_pad, br_row)
    counts = cnt[0, :N_EXPERTS]
    padded = (counts + MOE_BLOCK - 1) // MOE_BLOCK * MOE_BLOCK
    pend = jnp.cumsum(padded).astype(I32)
    pstart = pend - padded
    slot = (pstart[idx[:, :TOP_K]] + rank[:, :TOP_K]).reshape(-1).astype(I32)
    n_asg = n * TOP_K
    n_blocks = n_asg // MOE_BLOCK + N_EXPERTS
    assert n_asg % MOE_BLOCK == 0 and n % SC_WINDOW == 0
    n_slots = n_blocks * MOE_BLOCK
    n_used = pend[-1] // MOE_BLOCK
    blk = jnp.minimum(jnp.arange(n_blocks, dtype=I32), n_used - 1) * MOE_BLOCK
    block_e = jnp.minimum(jnp.sum((pend[None, :] <= blk[:, None]).astype(I32), axis=1), N_EXPERTS - 1)
    block_valid = jnp.clip(counts[block_e] - (blk - pstart[block_e]), 0, MOE_BLOCK).astype(I32)
    slot_by_choice = slot.reshape(n, TOP_K).T
    xb = _sc_scatter_tiles(ht.reshape(n, SUBLANES, LANES), slot_by_choice, n_slots)
    yb = _experts(xb.reshape(n_slots * SUBLANES, LANES), block_e, n_used[None], block_valid, layer,
                  w_up, b_up[:, :, None, :], w_dn, b_dn[:, :, None, :])
    y = _sc_gather_tiles(yb.reshape(n_slots, SUBLANES, LANES), slot_by_choice.reshape(-1))
    return _combine(h, y.reshape(n_asg * SUBLANES, LANES), gate, p_prompt, p_sample, layer, n_p, split,
                    w_pg.astype(BF16), w_pp.astype(BF16), g2[None], b2[None])


def _lane_row(v, lane0):
    return jnp.zeros((1, LANES), F32).at[0, lane0:lane0 + v.shape[0]].set(v.astype(F32))


def kernel(x_prompt, x_sample, cache_fox_k, cache_fox_v, cache_fox_logf, state_gdn, state_gdn_conv,
           cache_pool, p_prompt, p_sample, w_in_ab, b_fgate, gdn_a_log, gdn_dt_bias, gdn_conv_w,
           gdn_norm_g, w_out_ab, pool_w, pool_scale, w_out_pool, ln1_g, ln1_b, ln2_g, ln2_b,
           w_router, b_router, w_expert_up, b_expert_up, w_expert_down, b_expert_down,
           w_ple_gate, w_ple_proj):
    n_pb, seq, d = x_prompt.shape
    n_sb, dseq, _ = x_sample.shape
    past = cache_fox_k.shape[2]
    assert n_pb == 1 and dseq == CHUNK and past % dseq == 0 and seq % TQ == 0
    assert d == SUBLANES * LANES
    n_p = n_pb * seq
    n_s = n_sb * dseq
    n = n_p + n_s
    assert n_p % TM_TOK == 0 and n_s % TM_TOK == 0
    n_layers = p_prompt.shape[0]
    pp_all = p_prompt.reshape(n_layers * n_p, -1)
    ps_all = p_sample.reshape(n_layers * n_s, -1)

    def tail(h, ht, i, split):
        return _layer_tail(h, ht, pp_all, ps_all, i, n_p, split, ln2_g[i], ln2_b[i], w_router[i], b_router[i],
                           w_expert_up, b_expert_up, w_expert_down, b_expert_down, w_ple_gate[i],
                           w_ple_proj[i])

    w_in = w_in_ab[0]
    n_small = 3 * N_HEADS
    ff0 = 3 * WIDTH
    gq0 = ff0 + N_HEADS
    ga0 = gq0 + 4 * WIDTH
    w_small = jnp.concatenate([w_in[:, ff0:gq0], w_in[:, ga0:ga0 + 2 * N_HEADS],
                               jnp.zeros((d, LANES - n_small), F32)], axis=1)
    w_all = jnp.concatenate([w_in[:, :ff0], w_in[:, gq0:ga0], w_small], axis=1)
    bf_row = _lane_row(b_fgate[0], LANE_F)
    up = _proj(x_prompt.reshape(n_p, d), w_all.astype(BF16), bf_row, U_COLS, False)
    us = _proj(x_sample.reshape(n_s, d), w_all, bf_row, LANES, True)

    u3 = up[None]
    us3 = us.reshape(n_sb, dseq, U_COLS)
    cq_p, ck_p = _cumsum(u3, 1, n_p, TK, COL_S // LANES)
    lf_s = jnp.concatenate(
        [jnp.pad(cache_fox_logf[0].astype(F32), ((0, 0), (0, 0), (0, LANES - N_HEADS))),
         us3[:, :, COL_S:]], axis=1)
    cq_s, ck_s = _cumsum(lf_s, n_sb, past + dseq, past + dseq, 0)

    of_p = _fox(u3, lambda b, i: (0, i, COL_Q // WIDTH), u3, lambda b, j: (0, j, COL_K // WIDTH),
                u3, lambda b, j: (0, j, COL_V // WIDTH), cq_p, lambda b, i: (0, i, 0),
                ck_p, lambda b, j: (0, 0, j), 1, n_p // TQ, TQ, TK, 0)
    k_all = jnp.concatenate([cache_fox_k[0].reshape(n_sb, past, WIDTH), us3[:, :, COL_K:COL_K + WIDTH]], axis=1)
    v_all = jnp.concatenate([cache_fox_v[0].reshape(n_sb, past, WIDTH), us3[:, :, COL_V:COL_V + WIDTH]], axis=1)
    of_s = _fox(us3, lambda b, i: (b, 0, COL_Q // WIDTH), k_all, lambda b, j: (b, 0, 0),
                v_all, lambda b, j: (b, 0, 0), cq_s, lambda b, i: (b, past // dseq, 0),
                ck_s, lambda b, j: (b, 0, 0), n_sb, 1, dseq, past + dseq, past, hi=True)

    gw = 3 * WIDTH
    conv_w = jnp.pad(gdn_conv_w[0], ((0, SUBLANES - CONV_WIDTH), (0, 0)))
    gdn_args = (conv_w, _lane_row(gdn_a_log[0], LANE_A), _lane_row(gdn_dt_bias[0], LANE_A),
                gdn_norm_g[0][None])
    n_pstep = n_p // GDN_ROWS
    ends = lambda k: (jnp.asarray((np.arange(k) == 0).astype(np.int32)),
                      jnp.asarray((np.arange(k) == k - 1).astype(np.int32)))
    og_p, st_p = _gdn(up.reshape(n_pstep, GDN_ROWS, U_COLS), False, *ends(n_pstep),
                      jnp.zeros((n_pstep,), I32), jnp.zeros((1, SUBLANES, gw), F32), gdn_args[0],
                      jnp.zeros((1, N_HEADS, HEAD_DIM, HEAD_DIM), F32), *gdn_args[1:])
    conv_past = jnp.pad(state_gdn_conv[0].astype(F32), ((0, 0), (SUBLANES - (CONV_WIDTH - 1), 0), (0, 0)))
    ones = jnp.ones((n_sb,), I32)
    og_s, st_s = _gdn(us3, True, ones, ones, jnp.arange(n_sb, dtype=I32), conv_past, gdn_args[0],
                      state_gdn[0].astype(F32), *gdn_args[1:])
    h, ht = _outproj_ln((x_prompt.reshape(n_p, d), of_p.reshape(n_p, WIDTH), og_p.reshape(n_p, WIDTH)),
                        (x_sample.reshape(n_s, d), of_s.reshape(n_s, WIDTH), og_s.reshape(n_s, WIDTH)),
                        w_out_ab[0], ln1_g[0][None], ln1_b[0][None])
    x1 = tail(h, ht, 0, False)

    pool_args = (pool_w[0].astype(BF16), pool_scale[0][None], w_out_pool[0].astype(BF16),
                 ln1_g[1][None], ln1_b[1][None])
    ratio = TM_TOK // POOL_HALO
    x1_halo = x1.reshape(n // POOL_HALO, POOL_HALO, d)
    h_p, ht_p = _pool_ln(x1, lambda i: (i, 0), x1_halo, lambda i: (jnp.maximum(i * ratio - 1, 0), 0, 0),
                         n_p // TM_TOK, TM_TOK, 0, True, *pool_args)
    cache16 = jnp.pad(cache_pool[0].astype(F32), ((0, 0), (POOL_HALO - POOL_STATE, 0), (0, 0)))
    h_s, ht_s = _pool_ln(x1, lambda i: (n_p // dseq + i, 0), cache16, lambda i: (i, 0, 0),
                         n_sb, dseq, past, False, *pool_args)
    x2_p, x2_s = tail(jnp.concatenate([h_p, h_s], axis=0), jnp.concatenate([ht_p, ht_s], axis=0), 1, True)

    return (x2_p.reshape(n_pb, seq, d), x2_s.reshape(n_sb, dseq, d),
            up[:, COL_K:COL_K + WIDTH].reshape(1, n_pb, seq, N_HEADS, HEAD_DIM),
            up[:, COL_V:COL_V + WIDTH].reshape(1, n_pb, seq, N_HEADS, HEAD_DIM),
            up[:, COL_S:COL_S + N_HEADS].reshape(1, n_pb, seq, N_HEADS),
            st_p.reshape(1, n_pb, N_HEADS, HEAD_DIM, HEAD_DIM),
            up[seq - (CONV_WIDTH - 1):, COL_G:COL_G + gw].reshape(1, n_pb, CONV_WIDTH - 1, gw),
            x1[n_p - POOL_STATE:n_p].reshape(1, n_pb, POOL_STATE, d),
            us[:, COL_K:COL_K + WIDTH].reshape(1, n_sb, dseq, N_HEADS, HEAD_DIM),
            us[:, COL_V:COL_V + WIDTH].reshape(1, n_sb, dseq, N_HEADS, HEAD_DIM),
            us[:, COL_S:COL_S + N_HEADS].reshape(1, n_sb, dseq, N_HEADS),
            st_s.reshape(1, n_sb, N_HEADS, HEAD_DIM, HEAD_DIM),
            us[:, COL_G:COL_G + gw].reshape(n_sb, dseq, gw)[:, dseq - (CONV_WIDTH - 1):].reshape(
                1, n_sb, CONV_WIDTH - 1, gw),
            x1[n_p:].reshape(n_sb, dseq, d)[:, dseq - POOL_STATE:].reshape(1, n_sb, POOL_STATE, d))
```

```python
import functools

import numpy as np
import jax
import jax.numpy as jnp
from jax import lax
from jax.experimental import pallas as pl
from jax.experimental.pallas import tpu as pltpu
from jax.experimental.pallas import tpu_sc as plsc

F32 = jnp.float32
BF16 = jnp.bfloat16
I32 = jnp.int32
HIGHEST = lax.Precision.HIGHEST

LANES = 128
SUBLANES = 8
VMEM_LIMIT = 56 * 1024 * 1024

HEAD_DIM = 128
N_HEADS = 4
WIDTH = N_HEADS * HEAD_DIM
CHUNK = 64
CONV_WIDTH = 4
POOL_WINDOWS = (2, 4, 8, 16)
POOL_HALO = 16
POOL_STATE = 15
N_EXPERTS = 32
TOP_K = 4
SWIGLU_LIMIT = 7.0
SWIGLU_ALPHA = 1.702
DEPTH = 2
DN_ALPHA = (2 * DEPTH) ** 0.25
LN_EPS = 1e-5
NORM_EPS = 1e-6
NEG_INF = -1e30
LOG2E = 1.4426950408889634

COL_Q, COL_K, COL_V = 0, WIDTH, 2 * WIDTH
COL_G = 3 * WIDTH
COL_Z = 6 * WIDTH
COL_S = 7 * WIDTH
U_COLS = COL_S + LANES
LANE_F, LANE_A, LANE_B = 0, N_HEADS, 2 * N_HEADS

TM_PROJ = 256
TM_TOK = 512
SC_WINDOW = 32
MOE_BLOCK = 512
TQ = 1024
TK = 512
GDN_ROWS = 256
GDN_CHUNK = CHUNK


def _cparams(sem):
    return pltpu.CompilerParams(dimension_semantics=sem, vmem_limit_bytes=VMEM_LIMIT)


def _softplus(x):
    return jnp.maximum(x, 0.0) + jnp.log1p(jnp.exp(-jnp.abs(x)))


def _sigmoid(x):
    return 1.0 / (1.0 + jnp.exp(-x))


def _silu(x):
    return x * _sigmoid(x)


def _layer_norm(y, g, b):
    mu = jnp.mean(y, axis=-1, keepdims=True)
    yc = y - mu
    var = jnp.mean(yc * yc, axis=-1, keepdims=True)
    return yc * lax.rsqrt(var + LN_EPS) * g + b


def _dot_general(a, b, dims, hi):
    if hi:
        return lax.dot_general(a.astype(F32), b.astype(F32), (dims, ((), ())), precision=HIGHEST,
                               preferred_element_type=F32)
    return lax.dot_general(a.astype(BF16), b.astype(BF16), (dims, ((), ())), preferred_element_type=F32)


def _dot(a, b, hi=False):
    return _dot_general(a, b, ((1,), (0,)), hi)


def _dot_nt(a, b, hi=False):
    return _dot_general(a, b, ((1,), (1,)), hi)


def _dot_tn(a, b, hi=False):
    return _dot_general(a, b, ((0,), (0,)), hi)


def _spread_lanes(x, width):
    if width % LANES == 0:
        return jnp.concatenate([x] * (width // LANES), axis=1)
    return jnp.broadcast_to(x[:, 0:1], (x.shape[0], width))


def _load_token_tiles(ref, lead, n_tok):
    return jnp.concatenate([ref[(*lead, pl.ds(j, n_tok, stride=SUBLANES), slice(None))]
                            for j in range(SUBLANES)], axis=1)


def _store_token_tiles(ref, lead, x):
    for j in range(SUBLANES):
        ref[(*lead, pl.ds(j, x.shape[0], stride=SUBLANES), slice(None))] = x[:, j * LANES:(j + 1) * LANES]


def _lanes_to_rows(x, lane0):
    r = lax.broadcasted_iota(I32, (SUBLANES, LANES), 0)
    c = lax.broadcasted_iota(I32, (SUBLANES, LANES), 1)
    sel = (c == r + lane0).astype(F32)
    return lax.dot_general(sel, x, (((1,), (1,)), ((), ())), precision=HIGHEST,
                           preferred_element_type=F32)


def _proj_kernel(x_ref, w_ref, bf_ref, u_ref, *, hi):
    u = _dot(x_ref[...], w_ref[...], hi)
    u_ref[...] = u

    @pl.when(pl.program_id(1) == pl.num_programs(1) - 1)
    def _():
        small = u[:, u.shape[1] - LANES:]
        lane = lax.broadcasted_iota(I32, small.shape, 1)
        logf = -_softplus(-(small + bf_ref[...]))
        u_ref[:, u.shape[1] - LANES:] = jnp.where(lane < LANE_A, logf, small)


def _proj(x, w, bf_row, tn, hi):
    n, d = x.shape
    m = w.shape[1]
    return pl.pallas_call(
        functools.partial(_proj_kernel, hi=hi),
        grid=(n // TM_PROJ, m // tn),
        in_specs=[pl.BlockSpec((TM_PROJ, d), lambda i, j: (i, 0)),
                  pl.BlockSpec((d, tn), lambda i, j: (0, j)),
                  pl.BlockSpec((1, LANES), lambda i, j: (0, 0))],
        out_specs=pl.BlockSpec((TM_PROJ, tn), lambda i, j: (i, j)),
        out_shape=jax.ShapeDtypeStruct((n, m), F32),
        compiler_params=_cparams(("parallel", "parallel")),
        name="in_proj",
    )(x, w, bf_row)


def _cumsum_kernel(lf_ref, crep_ref, crow_ref, carry_ref):
    @pl.when(pl.program_id(1) == 0)
    def _():
        carry_ref[...] = jnp.zeros_like(carry_ref)

    lf = lf_ref[0]
    t = lf.shape[0]
    r = lax.broadcasted_iota(I32, (t, t), 0)
    c = lax.broadcasted_iota(I32, (t, t), 1)
    tril = (c <= r).astype(F32)
    cs = jnp.dot(tril, lf, precision=HIGHEST, preferred_element_type=F32) + carry_ref[0:1, :]
    carry_ref[...] = jnp.broadcast_to(cs[t - 1:t, :], carry_ref.shape)
    c2 = cs * LOG2E
    crow_ref[0] = _lanes_to_rows(c2, LANE_F)
    for h in range(N_HEADS):
        crep_ref[0, :, h * HEAD_DIM:(h + 1) * HEAD_DIM] = jnp.broadcast_to(
            c2[:, LANE_F + h:LANE_F + h + 1], (t, HEAD_DIM))


def _cumsum(arr, n_batch, length, tl, col_block):
    return pl.pallas_call(
        _cumsum_kernel,
        grid=(n_batch, length // tl),
        in_specs=[pl.BlockSpec((1, tl, LANES), lambda b, j: (b, j, col_block))],
        out_specs=[pl.BlockSpec((1, tl, WIDTH), lambda b, j: (b, j, 0)),
                   pl.BlockSpec((1, SUBLANES, tl), lambda b, j: (b, 0, j))],
        out_shape=[jax.ShapeDtypeStruct((n_batch, length, WIDTH), F32),
                   jax.ShapeDtypeStruct((n_batch, SUBLANES, length), F32)],
        scratch_shapes=[pltpu.VMEM((SUBLANES, LANES), F32)],
        compiler_params=_cparams(("parallel", "arbitrary")),
        name="logf_cumsum",
    )(arr)


def _fox_kernel(qi_ref, kj_ref, last_ref, q_ref, k_ref, v_ref, cq_ref, ck_ref, o_ref,
                m_ref, l_ref, acc_ref, *, tq, tk, past, hi):
    s_idx = pl.program_id(1)
    qi = qi_ref[s_idx]
    kj = kj_ref[s_idx]

    @pl.when(kj == 0)
    def _():
        m_ref[...] = jnp.full_like(m_ref, NEG_INF)
        l_ref[...] = jnp.zeros_like(l_ref)
        acc_ref[...] = jnp.zeros_like(acc_ref)

    def update(masked):
        if masked:
            q_pos = past + qi * tq + lax.broadcasted_iota(I32, (tq, tk), 0)
            k_pos = kj * tk + lax.broadcasted_iota(I32, (tq, tk), 1)
            visible = k_pos <= q_pos
        for h in range(N_HEADS):
            cols = slice(h * HEAD_DIM, (h + 1) * HEAD_DIM)
            q = q_ref[0, :, cols] * (HEAD_DIM ** -0.5 * LOG2E)
            t = _dot_nt(q, k_ref[0, :, cols], hi) - ck_ref[0, h:h + 1, :]
            if masked:
                t = jnp.where(visible, t, NEG_INF)
            cq = cq_ref[0, :, cols]
            m_prev = m_ref[h]
            m_new = jnp.maximum(m_prev, jnp.max(t, axis=-1, keepdims=True) + cq)
            p = jnp.exp2(t - _spread_lanes(m_new - cq, tk))
            alpha = jnp.exp2(m_prev - m_new)
            l_ref[h] = alpha * l_ref[h] + jnp.sum(p, axis=-1, keepdims=True)
            acc_ref[:, cols] = alpha * acc_ref[:, cols] + _dot(p, v_ref[0, :, cols], hi)
            m_ref[h] = m_new

    crosses_diagonal = kj * tk + (tk - 1) > past + qi * tq
    pl.when(crosses_diagonal)(functools.partial(update, True))
    pl.when(jnp.logical_not(crosses_diagonal))(functools.partial(update, False))

    @pl.when(last_ref[s_idx] == 1)
    def _():
        for h in range(N_HEADS):
            cols = slice(h * HEAD_DIM, (h + 1) * HEAD_DIM)
            o_ref[0, :, cols] = acc_ref[:, cols] / l_ref[h]


def _fox_schedule(n_q, tq, tk, past):
    qi, kj, last = [], [], []
    for i in range(n_q):
        hi = (past + (i + 1) * tq - 1) // tk
        for j in range(hi + 1):
            qi.append(i)
            kj.append(j)
            last.append(1 if j == hi else 0)
    return (jnp.asarray(np.array(qi, np.int32)), jnp.asarray(np.array(kj, np.int32)),
            jnp.asarray(np.array(last, np.int32)))


def _fox(q_arr, q_map, k_arr, k_map, v_arr, v_map, cq_arr, cq_map, ck_arr, ck_map,
         n_batch, n_q, tq, tk, past, hi=False):
    qi, kj, last = _fox_schedule(n_q, tq, tk, past)
    n_steps = int(qi.shape[0])
    spec = lambda shape, fn, tab: pl.BlockSpec(shape, lambda b, s, qi_r, kj_r, la_r: fn(b, (qi_r if tab == 'q' else kj_r)[s]))
    return pl.pallas_call(
        functools.partial(_fox_kernel, tq=tq, tk=tk, past=past, hi=hi),
        grid_spec=pltpu.PrefetchScalarGridSpec(
            num_scalar_prefetch=3,
            grid=(n_batch, n_steps),
            in_specs=[spec((1, tq, WIDTH), q_map, 'q'),
                      spec((1, tk, WIDTH), k_map, 'k'),
                      spec((1, tk, WIDTH), v_map, 'k'),
                      spec((1, tq, WIDTH), cq_map, 'q'),
                      spec((1, SUBLANES, tk), ck_map, 'k')],
            out_specs=spec((1, tq, WIDTH), lambda b, i: (b, i, 0), 'q'),
            scratch_shapes=[pltpu.VMEM((N_HEADS, tq, HEAD_DIM), F32),
                            pltpu.VMEM((N_HEADS, tq, HEAD_DIM), F32),
                            pltpu.VMEM((tq, WIDTH), F32)]),
        out_shape=jax.ShapeDtypeStruct((n_batch, n_q * tq, WIDTH), F32),
        compiler_params=_cparams(("parallel", "arbitrary")),
        name="fox_attention",
    )(qi, kj, last, q_arr, k_arr, v_arr, cq_arr, ck_arr)


def _gdn_kernel(first_ref, last_ref, seq_ref,
                pre_ref, z_ref, sm_ref, cpast_ref, convw_ref, s0_ref, alog_ref, dtb_ref, ng_ref,
                o_ref, sout_ref, stage_ref, s_ref, *, rows, chunk, hi):
    dot, dot_nt, dot_tn = (functools.partial(f, hi=hi) for f in (_dot, _dot_nt, _dot_tn))
    step = pl.program_id(0)
    halo = SUBLANES
    n_chunks = rows // chunk

    @pl.when(first_ref[step] == 1)
    def _():
        stage_ref[0:halo, :] = cpast_ref[0]
        s_ref[...] = s0_ref[0]

    stage_ref[halo:halo + rows, :] = pre_ref[0]
    conv = stage_ref[halo:halo + rows, :] * convw_ref[CONV_WIDTH - 1:CONV_WIDTH, :]
    for j in range(1, CONV_WIDTH):
        conv = conv + (stage_ref[halo - j:halo - j + rows, :]
                       * convw_ref[CONV_WIDTH - 1 - j:CONV_WIDTH - j, :])
    stage_ref[0:halo, :] = stage_ref[rows:rows + halo, :]
    act = _silu(conv)

    small = sm_ref[0]
    beta_all = _sigmoid(small)
    g_all = -jnp.exp(alog_ref[...]) * _softplus(small + dtb_ref[...])
    r = lax.broadcasted_iota(I32, (rows, rows), 0)
    c = lax.broadcasted_iota(I32, (rows, rows), 1)
    same_chunk = (r // chunk) == (c // chunk)
    incl = same_chunk & (c <= r)
    strict = same_chunk & (c < r)
    eye = (c == r).astype(F32)
    gc_all = jnp.dot(incl.astype(F32), g_all, precision=HIGHEST, preferred_element_type=F32)
    gc_rows = _lanes_to_rows(gc_all, LANE_A)

    heads = range(N_HEADS)
    head_cols = [slice(h * HEAD_DIM, (h + 1) * HEAD_DIM) for h in heads]
    q, k, gc, decay, kb, vb, low = [], [], [], [], [], [], []
    for h in heads:
        qh = act[:, h * HEAD_DIM:(h + 1) * HEAD_DIM]
        kh = act[:, WIDTH + h * HEAD_DIM:WIDTH + (h + 1) * HEAD_DIM]
        vh = act[:, 2 * WIDTH + h * HEAD_DIM:2 * WIDTH + (h + 1) * HEAD_DIM]
        q.append(qh * lax.rsqrt(jnp.sum(qh * qh, axis=-1, keepdims=True) + NORM_EPS) * (HEAD_DIM ** -0.5))
        k.append(kh * lax.rsqrt(jnp.sum(kh * kh, axis=-1, keepdims=True) + NORM_EPS))
        beta = beta_all[:, LANE_B + h:LANE_B + h + 1]
        gc.append(gc_all[:, LANE_A + h:LANE_A + h + 1])
        diff = gc[h] - gc_rows[h:h + 1, :]
        decay.append(jnp.where(incl, jnp.exp(jnp.where(incl, diff, 0.0)), 0.0))
        kb.append(k[h] * beta)
        vb.append(vh * beta)
    for h in heads:
        low.append(jnp.where(strict, dot_nt(kb[h], k[h]) * decay[h], 0.0))
    inv = [eye - low[h] for h in heads]
    pw = [dot(low[h], low[h]) for h in heads]
    n_sq = chunk.bit_length() - 2
    for it in range(n_sq):
        inv = [inv[h] + dot(inv[h], pw[h]) for h in heads]
        if it + 1 < n_sq:
            pw = [dot(pw[h], pw[h]) for h in heads]
    egc = [jnp.exp(gc[h]) for h in heads]
    uw = [dot(inv[h], jnp.concatenate([vb[h], kb[h] * egc[h]], axis=1)) for h in heads]
    intra = [jnp.where(incl, dot_nt(q[h], k[h]) * decay[h], 0.0) for h in heads]
    qd = [q[h] * egc[h] for h in heads]
    g_last = [[gc[h][(g + 1) * chunk - 1:(g + 1) * chunk, :] for g in range(n_chunks)] for h in heads]
    kd = [k[h] * jnp.exp(jnp.concatenate([jnp.broadcast_to(gl, (chunk, 1)) for gl in g_last[h]], axis=0)
                         - gc[h]) for h in heads]
    state = [s_ref[h] for h in heads]
    v_new = [[] for _ in heads]
    for g in range(n_chunks):
        rs = slice(g * chunk, (g + 1) * chunk)
        for h in heads:
            v_new[h].append(uw[h][rs, :HEAD_DIM] - dot(uw[h][rs, HEAD_DIM:], state[h]))
        for h in heads:
            v_rows = jnp.concatenate(
                v_new[h] + [jnp.zeros((rows - (g + 1) * chunk, HEAD_DIM), F32)] * (g + 1 < n_chunks), axis=0)
            o = dot(qd[h][rs, :], state[h]) + dot(intra[h][rs, :], v_rows)
            state[h] = state[h] * jnp.exp(g_last[h][g]) + dot_tn(kd[h][rs, :], v_new[h][g])
            o = (o * lax.rsqrt(jnp.mean(o * o, axis=-1, keepdims=True) + NORM_EPS)
                 * ng_ref[...] * _silu(z_ref[0, rs, head_cols[h]]))
            o_ref[0, rs, head_cols[h]] = o
    for h in heads:
        s_ref[h] = state[h]

    @pl.when(last_ref[step] == 1)
    def _():
        sout_ref[0] = s_ref[...]


def _gdn(u_view, hi, first, last, seq, conv_past, conv_w, s0, alog_row, dtb_row, ng_row):
    n_steps, rows, _ = u_view.shape
    n_seq = s0.shape[0]
    gw = 3 * WIDTH
    return pl.pallas_call(
        functools.partial(_gdn_kernel, rows=rows, chunk=min(rows, GDN_CHUNK), hi=hi),
        grid_spec=pltpu.PrefetchScalarGridSpec(
            num_scalar_prefetch=3,
            grid=(n_steps,),
            in_specs=[pl.BlockSpec((1, rows, gw), lambda s, f, l, q: (s, 0, COL_G // gw)),
                      pl.BlockSpec((1, rows, WIDTH), lambda s, f, l, q: (s, 0, COL_Z // WIDTH)),
                      pl.BlockSpec((1, rows, LANES), lambda s, f, l, q: (s, 0, COL_S // LANES)),
                      pl.BlockSpec((1, SUBLANES, gw), lambda s, f, l, q: (q[s], 0, 0)),
                      pl.BlockSpec((SUBLANES, gw), lambda s, f, l, q: (0, 0)),
                      pl.BlockSpec((1, N_HEADS, HEAD_DIM, HEAD_DIM), lambda s, f, l, q: (q[s], 0, 0, 0)),
                      pl.BlockSpec((1, LANES), lambda s, f, l, q: (0, 0)),
                      pl.BlockSpec((1, LANES), lambda s, f, l, q: (0, 0)),
                      pl.BlockSpec((1, LANES), lambda s, f, l, q: (0, 0))],
            out_specs=[pl.BlockSpec((1, rows, WIDTH), lambda s, f, l, q: (s, 0, 0)),
                       pl.BlockSpec((1, N_HEADS, HEAD_DIM, HEAD_DIM), lambda s, f, l, q: (q[s], 0, 0, 0))],
            scratch_shapes=[pltpu.VMEM((rows + SUBLANES, gw), F32),
                            pltpu.VMEM((N_HEADS, HEAD_DIM, HEAD_DIM), F32)]),
        out_shape=[jax.ShapeDtypeStruct((n_steps, rows, WIDTH), F32),
                   jax.ShapeDtypeStruct((n_seq, N_HEADS, HEAD_DIM, HEAD_DIM), F32)],
        compiler_params=_cparams(("arbitrary",)),
        name="gated_deltanet",
    )(first, last, seq, u_view, u_view, u_view, conv_past, conv_w, s0, alog_row, dtb_row, ng_row)


def _outproj_ln_kernel(xp_ref, ofp_ref, ogp_ref, xs_ref, ofs_ref, ogs_ref, w_ref, wf_ref, g_ref, b_ref,
                       h_ref, ht_ref, *, hi_from):
    def run(hi, x_ref, of_ref, og_ref):
        w = wf_ref if hi else w_ref
        mix = _dot(of_ref[...], w[0:WIDTH, :], hi) + _dot(og_ref[...], w[WIDTH:2 * WIDTH, :], hi)
        h = _layer_norm(DN_ALPHA * x_ref[...] + mix, g_ref[...], b_ref[...])
        h_ref[...] = h
        _store_token_tiles(ht_ref, (), h)

    pl.when(pl.program_id(0) < hi_from)(functools.partial(run, False, xp_ref, ofp_ref, ogp_ref))
    pl.when(pl.program_id(0) >= hi_from)(functools.partial(run, True, xs_ref, ofs_ref, ogs_ref))


def _outproj_ln(prompt, sample, w, g_row, b_row):
    n_p, d = prompt[0].shape
    n = n_p + sample[0].shape[0]
    hi_from = n_p // TM_TOK
    row = lambda i: (i, 0)
    row_p = lambda i: (jnp.minimum(i, hi_from - 1), 0)
    row_s = lambda i: (jnp.maximum(i - hi_from, 0), 0)
    fixed = lambda i: (0, 0)
    group = lambda rows: [pl.BlockSpec((TM_TOK, d), rows), pl.BlockSpec((TM_TOK, WIDTH), rows),
                          pl.BlockSpec((TM_TOK, WIDTH), rows)]
    return pl.pallas_call(
        functools.partial(_outproj_ln_kernel, hi_from=hi_from),
        grid=(n // TM_TOK,),
        in_specs=group(row_p) + group(row_s) + [pl.BlockSpec((2 * WIDTH, d), fixed),
                                                pl.BlockSpec((2 * WIDTH, d), fixed),
                                                pl.BlockSpec((1, d), fixed), pl.BlockSpec((1, d), fixed)],
        out_specs=[pl.BlockSpec((TM_TOK, d), row), pl.BlockSpec((TM_TOK * SUBLANES, LANES), row)],
        out_shape=[jax.ShapeDtypeStruct((n, d), F32), jax.ShapeDtypeStruct((n * SUBLANES, LANES), F32)],
        compiler_params=_cparams(("parallel",)),
        name="out_proj_ln",
    )(*prompt, *sample, w.astype(BF16), w, g_row, b_row)


def _pool_ln_kernel(x_ref, halo_ref, pw_ref, ps_ref, w_ref, g_ref, b_ref, h_ref, ht_ref, stage_ref,
                    *, tm, pos0, zero_first_halo):
    i = pl.program_id(0)
    stage_ref[0:POOL_HALO, :] = halo_ref[0]
    if zero_first_halo:
        @pl.when(i == 0)
        def _():
            stage_ref[0:POOL_HALO, :] = jnp.zeros((POOL_HALO, stage_ref.shape[1]), F32)
    x = x_ref[...]
    stage_ref[POOL_HALO:POOL_HALO + tm, :] = x
    gdim = x.shape[1] // len(POOL_WINDOWS)
    pos = pos0 + lax.broadcasted_iota(I32, (tm, 1), 0)
    if zero_first_halo:
        pos = pos + i * tm
    parts = []
    for gi, win in enumerate(POOL_WINDOWS):
        cols = slice(gi * gdim, (gi + 1) * gdim)
        s = stage_ref[POOL_HALO:POOL_HALO + tm, cols]
        for j in range(1, win):
            s = s + stage_ref[POOL_HALO - j:POOL_HALO - j + tm, cols]
        cnt = jnp.minimum(pos + 1, win).astype(F32)
        zg = s / cnt - x[:, cols]
        parts.append(_dot(zg, pw_ref[gi]))
    zg = jnp.concatenate(parts, axis=-1) * ps_ref[...]
    mix = _dot(zg, w_ref[...])
    h = _layer_norm(DN_ALPHA * x + mix, g_ref[...], b_ref[...])
    h_ref[...] = h
    _store_token_tiles(ht_ref, (), h)


def _pool_ln(x, x_map, halo_arr, halo_map, n_tiles, tm, pos0, zero_first_halo,
             pw_bf16, ps_row, w_bf16, g_row, b_row):
    d = x.shape[1]
    gdim = d // len(POOL_WINDOWS)
    fixed = lambda i: (0, 0)
    return pl.pallas_call(
        functools.partial(_pool_ln_kernel, tm=tm, pos0=pos0, zero_first_halo=zero_first_halo),
        grid=(n_tiles,),
        in_specs=[pl.BlockSpec((tm, d), x_map),
                  pl.BlockSpec((1, POOL_HALO, d), halo_map),
                  pl.BlockSpec((len(POOL_WINDOWS), gdim, gdim), lambda i: (0, 0, 0)),
                  pl.BlockSpec((1, d), fixed), pl.BlockSpec((d, d), fixed),
                  pl.BlockSpec((1, d), fixed), pl.BlockSpec((1, d), fixed)],
        out_specs=[pl.BlockSpec((tm, d), lambda i: (i, 0)),
                   pl.BlockSpec((tm * SUBLANES, LANES), lambda i: (i, 0))],
        out_shape=[jax.ShapeDtypeStruct((n_tiles * tm, d), F32),
                   jax.ShapeDtypeStruct((n_tiles * tm * SUBLANES, LANES), F32)],
        scratch_shapes=[pltpu.VMEM((POOL_HALO + tm, d), F32)],
        compiler_params=_cparams(("arbitrary",)),
        name="pool_mixer_ln",
    )(x, halo_arr, pw_bf16, ps_row, w_bf16, g_row, b_row)


def _router_kernel(h_ref, wr_ref, br_ref, idx_ref, gate_ref, rank_ref, cnt_ref, carry_ref):
    @pl.when(pl.program_id(0) == 0)
    def _():
        carry_ref[...] = jnp.zeros_like(carry_ref)

    tm = h_ref.shape[0]
    lane = lax.broadcasted_iota(I32, (tm, LANES), 1).astype(F32)
    logits = jnp.dot(h_ref[...], wr_ref[...], precision=HIGHEST, preferred_element_type=F32)
    work = jnp.where(lane < N_EXPERTS, logits + br_ref[...], -jnp.inf)
    vals, ids = [], []
    for _ in range(TOP_K):
        m = jnp.max(work, axis=-1, keepdims=True)
        ik = jnp.min(jnp.where(work == m, lane, float(LANES)), axis=-1, keepdims=True)
        vals.append(m)
        ids.append(ik)
        work = jnp.where(lane == ik, -jnp.inf, work)
    exps = [jnp.exp(v - vals[0]) for v in vals]
    denom = exps[0]
    for e in exps[1:]:
        denom = denom + e
    multihot = jnp.zeros((tm, LANES), F32)
    idx_out = jnp.zeros((tm, LANES), F32)
    gate_out = jnp.zeros((tm, LANES), F32)
    for k in range(TOP_K):
        multihot = multihot + (lane == ids[k]).astype(F32)
        idx_out = jnp.where(lane == k, ids[k], idx_out)
        gate_out = jnp.where(lane == k, exps[k] / denom, gate_out)
    r = lax.broadcasted_iota(I32, (tm, tm), 0)
    c = lax.broadcasted_iota(I32, (tm, tm), 1)
    before = _dot((c < r).astype(F32), multihot) + carry_ref[0:1, :]
    rank_out = jnp.zeros((tm, LANES), F32)
    for k in range(TOP_K):
        rk = jnp.sum(jnp.where(lane == ids[k], before, 0.0), axis=-1, keepdims=True)
        rank_out = jnp.where(lane == k, rk, rank_out)
    idx_ref[...] = idx_out.astype(I32)
    gate_ref[...] = gate_out
    rank_ref[...] = rank_out.astype(I32)
    total = carry_ref[0:1, :] + jnp.sum(multihot, axis=0, keepdims=True)
    carry_ref[...] = jnp.broadcast_to(total, carry_ref.shape)
    cnt_ref[...] = jnp.broadcast_to(total, cnt_ref.shape).astype(I32)


def _router(h, wr_pad, br_row):
    n, d = h.shape
    row = lambda i: (i, 0)
    fixed = lambda i: (0, 0)
    return pl.pallas_call(
        _router_kernel,
        grid=(n // TM_TOK,),
        in_specs=[pl.BlockSpec((TM_TOK, d), row), pl.BlockSpec((d, LANES), fixed),
                  pl.BlockSpec((1, LANES), fixed)],
        out_specs=[pl.BlockSpec((TM_TOK, LANES), row), pl.BlockSpec((TM_TOK, LANES), row),
                   pl.BlockSpec((TM_TOK, LANES), row), pl.BlockSpec((SUBLANES, LANES), fixed)],
        out_shape=[jax.ShapeDtypeStruct((n, LANES), I32), jax.ShapeDtypeStruct((n, LANES), F32),
                   jax.ShapeDtypeStruct((n, LANES), I32), jax.ShapeDtypeStruct((SUBLANES, LANES), I32)],
        scratch_shapes=[pltpu.VMEM((SUBLANES, LANES), F32)],
        compiler_params=_cparams(("arbitrary",)),
        name="moe_router",
    )(h, wr_pad, br_row)


def _expert_kernel(be_ref, nu_ref, valid_ref, xb_ref, wup_ref, bup_ref, wdn_ref, bdn_ref, yb_ref,
                   wup_bf_ref, wdn_bf_ref):
    b = pl.program_id(0)

    @pl.when(b < nu_ref[0])
    def _():
        @pl.when((b == 0) | (be_ref[b] != be_ref[jnp.maximum(b - 1, 0)]))
        def _():
            wup_bf_ref[...] = wup_ref[0, 0].astype(BF16)
            wdn_bf_ref[...] = wdn_ref[0, 0].astype(BF16)

        d_exp = wdn_ref.shape[2]
        x = _load_token_tiles(xb_ref, (), MOE_BLOCK)
        row = lax.broadcasted_iota(I32, (MOE_BLOCK, 1), 0)
        x = jnp.where(row < valid_ref[b], x, 0.0)
        hu = jnp.dot(x.astype(BF16), wup_bf_ref[...], preferred_element_type=F32) + bup_ref[0, 0]
        glu = jnp.minimum(hu[:, :d_exp], SWIGLU_LIMIT)
        lin = jnp.clip(hu[:, d_exp:], -SWIGLU_LIMIT, SWIGLU_LIMIT)
        a = glu * _sigmoid(SWIGLU_ALPHA * glu) * (lin + 1.0)
        y = jnp.dot(a.astype(BF16), wdn_bf_ref[...], preferred_element_type=F32) + bdn_ref[0, 0]
        _store_token_tiles(yb_ref, (), y)

    @pl.when(b >= nu_ref[0])
    def _():
        yb_ref[...] = jnp.zeros_like(yb_ref)


def _experts(xb, block_e, n_used, block_valid, layer, w_up, b_up, w_dn, b_dn):
    d = SUBLANES * LANES
    rows = MOE_BLOCK * SUBLANES
    n_blocks = xb.shape[0] // rows
    d_up = w_up.shape[3]
    d_exp = w_dn.shape[2]
    wsel = lambda b, be, nu, va: (layer, be[b], 0, 0)
    return pl.pallas_call(
        _expert_kernel,
        grid_spec=pltpu.PrefetchScalarGridSpec(
            num_scalar_prefetch=3,
            grid=(n_blocks,),
            in_specs=[pl.BlockSpec((rows, LANES), lambda b, be, nu, va: (jnp.minimum(b, nu[0] - 1), 0)),
                      pl.BlockSpec((1, 1, d, d_up), wsel),
                      pl.BlockSpec((1, 1, 1, d_up), wsel),
                      pl.BlockSpec((1, 1, d_exp, d), wsel),
                      pl.BlockSpec((1, 1, 1, d), wsel)],
            out_specs=pl.BlockSpec((rows, LANES), lambda b, be, nu, va: (b, 0)),
            scratch_shapes=[pltpu.VMEM((d, d_up), BF16), pltpu.VMEM((d_exp, d), BF16)]),
        out_shape=jax.ShapeDtypeStruct(xb.shape, F32),
        compiler_params=_cparams(("arbitrary",)),
        name="moe_experts",
    )(block_e, n_used, block_valid, xb, w_up, b_up, w_dn, b_dn)


def _sc_gather_tiles(table, idx):
    m = idx.shape[0]
    mesh = plsc.VectorSubcoreMesh(core_axis_name="core", subcore_axis_name="subcore")

    @functools.partial(pl.kernel, out_type=jax.ShapeDtypeStruct((m, SUBLANES, LANES), table.dtype), mesh=mesh)
    def gather(table_hbm, idx_hbm, out_hbm):
        def window(idx_vmem, out_vmem):
            pltpu.sync_copy(table_hbm.at[idx_vmem.at[0, pl.ds(0, SC_WINDOW)]], out_vmem)

        pltpu.emit_pipeline(
            window,
            grid=(m // SC_WINDOW,),
            in_specs=[pl.BlockSpec((1, LANES), lambda i: (i, 0))],
            out_specs=[pl.BlockSpec((SC_WINDOW, SUBLANES, LANES), lambda i: (i, 0, 0))],
            core_axis_name=("core", "subcore"),
            dimension_semantics=(pltpu.PARALLEL,),
        )(idx_hbm, out_hbm)

    idx_rows = jnp.pad(idx.reshape(m // SC_WINDOW, SC_WINDOW), ((0, 0), (0, LANES - SC_WINDOW)))
    return gather(table, idx_rows)


def _sc_scatter_tiles(tiles, idx_by_choice, n_out):
    n = tiles.shape[0]
    mesh = plsc.VectorSubcoreMesh(core_axis_name="core", subcore_axis_name="subcore")

    @functools.partial(pl.kernel, out_type=jax.ShapeDtypeStruct((n_out, SUBLANES, LANES), tiles.dtype),
                       mesh=mesh)
    def scatter(tiles_hbm, *refs):
        idx_hbm, out_hbm = refs[:TOP_K], refs[TOP_K]

        def window(tiles_vmem, *idx_vmem):
            for k in range(TOP_K):
                pltpu.sync_copy(tiles_vmem, out_hbm.at[idx_vmem[k].at[0, pl.ds(0, SC_WINDOW)]])

        pltpu.emit_pipeline(
            window,
            grid=(n // SC_WINDOW,),
            in_specs=[pl.BlockSpec((SC_WINDOW, SUBLANES, LANES), lambda i: (i, 0, 0))]
                     + [pl.BlockSpec((1, LANES), lambda i: (i, 0))] * TOP_K,
            out_specs=[],
            core_axis_name=("core", "subcore"),
            dimension_semantics=(pltpu.PARALLEL,),
        )(tiles_hbm, *idx_hbm)

    idx_rows = jnp.pad(idx_by_choice.reshape(TOP_K, n // SC_WINDOW, SC_WINDOW),
                       ((0, 0), (0, 0), (0, LANES - SC_WINDOW)))
    return scatter(tiles, *[idx_rows[k] for k in range(TOP_K)])


def _combine_kernel(h_ref, y0_ref, y1_ref, y2_ref, y3_ref, gate_ref, pp_ref, ps_ref, wpg_ref, wpp_ref,
                    g_ref, b_ref, *outs, n_ptiles):
    outp_ref, outs_ref = outs[0], outs[-1]
    tm = h_ref.shape[0]
    gate = gate_ref[...]
    moe = _load_token_tiles(y0_ref, (), tm) * gate[:, 0:1]
    for k, y_ref in enumerate((y1_ref, y2_ref, y3_ref), start=1):
        moe = moe + _load_token_tiles(y_ref, (), tm) * gate[:, k:k + 1]
    h2 = _layer_norm(DN_ALPHA * h_ref[...] + moe, g_ref[...], b_ref[...])
    embed_gate = _sigmoid(_dot(h2, wpg_ref[...]))

    def finish(p_ref, out_ref):
        out_ref[...] = h2 + embed_gate * _dot(p_ref[...], wpp_ref[...])

    is_prompt = pl.program_id(0) < n_ptiles
    pl.when(is_prompt)(functools.partial(finish, pp_ref, outp_ref))
    pl.when(jnp.logical_not(is_prompt))(functools.partial(finish, ps_ref, outs_ref))


def _combine(h, y, gate, p_prompt, p_sample, layer, n_p, split, wpg_bf16, wpp_bf16, g_row, b_row):
    n, d = h.shape
    e = p_prompt.shape[1]
    n_tiles = n // TM_TOK
    n_ptiles = n_p // TM_TOK
    n_stiles = (n - n_p) // TM_TOK
    row = lambda i: (i, 0)
    row_p = lambda i: (jnp.minimum(i, n_ptiles - 1), 0)
    row_s = lambda i: (jnp.maximum(i - n_ptiles, 0), 0)
    fixed = lambda i: (0, 0)
    choice = lambda k: pl.BlockSpec((TM_TOK * SUBLANES, LANES), lambda i: (k * n_tiles + i, 0))
    if split:
        out_specs = [pl.BlockSpec((TM_TOK, d), row_p), pl.BlockSpec((TM_TOK, d), row_s)]
        out_shape = [jax.ShapeDtypeStruct((n_p, d), F32), jax.ShapeDtypeStruct((n - n_p, d), F32)]
    else:
        out_specs = pl.BlockSpec((TM_TOK, d), row)
        out_shape = jax.ShapeDtypeStruct((n, d), F32)
    return pl.pallas_call(
        functools.partial(_combine_kernel, n_ptiles=n_ptiles),
        grid=(n_tiles,),
        in_specs=[pl.BlockSpec((TM_TOK, d), row)] + [choice(k) for k in range(TOP_K)]
                 + [pl.BlockSpec((TM_TOK, LANES), row),
                    pl.BlockSpec((TM_TOK, e), lambda i: (layer * n_ptiles + row_p(i)[0], 0)),
                    pl.BlockSpec((TM_TOK, e), lambda i: (layer * n_stiles + row_s(i)[0], 0)),
                    pl.BlockSpec((d, d), fixed), pl.BlockSpec((e, d), fixed), pl.BlockSpec((1, d), fixed),
                    pl.BlockSpec((1, d), fixed)],
        out_specs=out_specs,
        out_shape=out_shape,
        compiler_params=_cparams(("arbitrary",)),
        name="moe_combine_ln_embed",
    )(h, y, y, y, y, gate, p_prompt, p_sample, wpg_bf16, wpp_bf16, g_row, b_row)


def _layer_tail(h, ht, p_prompt, p_sample, layer, n_p, split, g2, b2, w_r, b_r, w_up, b_up, w_dn, b_dn,
                w_pg, w_pp):
    n, d = h.shape
    wr_pad = jnp.pad(w_r, ((0, 0), (0, LANES - N_EXPERTS)))
    br_row = jnp.pad(b_r, (0, LANES - N_EXPERTS))[None]
    idx, gate, rank, cnt = _router(h, wr_pad, br_row)
    counts = cnt[0, :N_EXPERTS]
    padded = (counts + MOE_BLOCK - 1) // MOE_BLOCK * MOE_BLOCK
    pend = jnp.cumsum(padded).astype(I32)
    pstart = pend - padded
    slot = (pstart[idx[:, :TOP_K]] + rank[:, :TOP_K]).reshape(-1).astype(I32)
    n_asg = n * TOP_K
    n_blocks = n_asg // MOE_BLOCK + N_EXPERTS
    assert n_asg % MOE_BLOCK == 0 and n % SC_WINDOW == 0
    n_slots = n_blocks * MOE_BLOCK
    n_used = pend[-1] // MOE_BLOCK
    blk = jnp.minimum(jnp.arange(n_blocks, dtype=I32), n_used - 1) * MOE_BLOCK
    block_e = jnp.minimum(jnp.sum((pend[None, :] <= blk[:, None]).astype(I32), axis=1), N_EXPERTS - 1)
    block_valid = jnp.clip(counts[block_e] - (blk - pstart[block_e]), 0, MOE_BLOCK).astype(I32)
    slot_by_choice = slot.reshape(n, TOP_K).T
    xb = _sc_scatter_tiles(ht.reshape(n, SUBLANES, LANES), slot_by_choice, n_slots)
    yb = _experts(xb.reshape(n_slots * SUBLANES, LANES), block_e, n_used[None], block_valid, layer,
                  w_up, b_up[:, :, None, :], w_dn, b_dn[:, :, None, :])
    y = _sc_gather_tiles(yb.reshape(n_slots, SUBLANES, LANES), slot_by_choice.reshape(-1))
    return _combine(h, y.reshape(n_asg * SUBLANES, LANES), gate, p_prompt, p_sample, layer, n_p, split,
                    w_pg.astype(BF16), w_pp.astype(BF16), g2[None], b2[None])


def _lane_row(v, lane0):
    return jnp.zeros((1, LANES), F32).at[0, lane0:lane0 + v.shape[0]].set(v.astype(F32))


def kernel(x_prompt, x_sample, cache_fox_k, cache_fox_v, cache_fox_logf, state_gdn, state_gdn_conv,
           cache_pool, p_prompt, p_sample, w_in_ab, b_fgate, gdn_a_log, gdn_dt_bias, gdn_conv_w,
           gdn_norm_g, w_out_ab, pool_w, pool_scale, w_out_pool, ln1_g, ln1_b, ln2_g, ln2_b,
           w_router, b_router, w_expert_up, b_expert_up, w_expert_down, b_expert_down,
           w_ple_gate, w_ple_proj):
    n_pb, seq, d = x_prompt.shape
    n_sb, dseq, _ = x_sample.shape
    past = cache_fox_k.shape[2]
    assert n_pb == 1 and dseq == CHUNK and past % dseq == 0 and seq % TQ == 0
    assert d == SUBLANES * LANES
    n_p = n_pb * seq
    n_s = n_sb * dseq
    n = n_p + n_s
    assert n_p % TM_TOK == 0 and n_s % TM_TOK == 0
    n_layers = p_prompt.shape[0]
    pp_all = p_prompt.reshape(n_layers * n_p, -1)
    ps_all = p_sample.reshape(n_layers * n_s, -1)

    def tail(h, ht, i, split):
        return _layer_tail(h, ht, pp_all, ps_all, i, n_p, split, ln2_g[i], ln2_b[i], w_router[i], b_router[i],
                           w_expert_up, b_expert_up, w_expert_down, b_expert_down, w_ple_gate[i],
                           w_ple_proj[i])

    w_in = w_in_ab[0]
    n_small = 3 * N_HEADS
    ff0 = 3 * WIDTH
    gq0 = ff0 + N_HEADS
    ga0 = gq0 + 4 * WIDTH
    w_small = jnp.concatenate([w_in[:, ff0:gq0], w_in[:, ga0:ga0 + 2 * N_HEADS],
                               jnp.zeros((d, LANES - n_small), F32)], axis=1)
    w_all = jnp.concatenate([w_in[:, :ff0], w_in[:, gq0:ga0], w_small], axis=1)
    bf_row = _lane_row(b_fgate[0], LANE_F)
    up = _proj(x_prompt.reshape(n_p, d), w_all.astype(BF16), bf_row, U_COLS, False)
    us = _proj(x_sample.reshape(n_s, d), w_all, bf_row, LANES, True)

    u3 = up[None]
    us3 = us.reshape(n_sb, dseq, U_COLS)
    cq_p, ck_p = _cumsum(u3, 1, n_p, TK, COL_S // LANES)
    lf_s = jnp.concatenate(
        [jnp.pad(cache_fox_logf[0].astype(F32), ((0, 0), (0, 0), (0, LANES - N_HEADS))),
         us3[:, :, COL_S:]], axis=1)
    cq_s, ck_s = _cumsum(lf_s, n_sb, past + dseq, past + dseq, 0)

    of_p = _fox(u3, lambda b, i: (0, i, COL_Q // WIDTH), u3, lambda b, j: (0, j, COL_K // WIDTH),
                u3, lambda b, j: (0, j, COL_V // WIDTH), cq_p, lambda b, i: (0, i, 0),
                ck_p, lambda b, j: (0, 0, j), 1, n_p // TQ, TQ, TK, 0)
    k_all = jnp.concatenate([cache_fox_k[0].reshape(n_sb, past, WIDTH), us3[:, :, COL_K:COL_K + WIDTH]], axis=1)
    v_all = jnp.concatenate([cache_fox_v[0].reshape(n_sb, past, WIDTH), us3[:, :, COL_V:COL_V + WIDTH]], axis=1)
    of_s = _fox(us3, lambda b, i: (b, 0, COL_Q // WIDTH), k_all, lambda b, j: (b, 0, 0),
                v_all, lambda b, j: (b, 0, 0), cq_s, lambda b, i: (b, past // dseq, 0),
                ck_s, lambda b, j: (b, 0, 0), n_sb, 1, dseq, past + dseq, past, hi=True)

    gw = 3 * WIDTH
    conv_w = jnp.pad(gdn_conv_w[0], ((0, SUBLANES - CONV_WIDTH), (0, 0)))
    gdn_args = (conv_w, _lane_row(gdn_a_log[0], LANE_A), _lane_row(gdn_dt_bias[0], LANE_A),
                gdn_norm_g[0][None])
    n_pstep = n_p // GDN_ROWS
    ends = lambda k: (jnp.asarray((np.arange(k) == 0).astype(np.int32)),
                      jnp.asarray((np.arange(k) == k - 1).astype(np.int32)))
    og_p, st_p = _gdn(up.reshape(n_pstep, GDN_ROWS, U_COLS), False, *ends(n_pstep),
                      jnp.zeros((n_pstep,), I32), jnp.zeros((1, SUBLANES, gw), F32), gdn_args[0],
                      jnp.zeros((1, N_HEADS, HEAD_DIM, HEAD_DIM), F32), *gdn_args[1:])
    conv_past = jnp.pad(state_gdn_conv[0].astype(F32), ((0, 0), (SUBLANES - (CONV_WIDTH - 1), 0), (0, 0)))
    ones = jnp.ones((n_sb,), I32)
    og_s, st_s = _gdn(us3, True, ones, ones, jnp.arange(n_sb, dtype=I32), conv_past, gdn_args[0],
                      state_gdn[0].astype(F32), *gdn_args[1:])
    h, ht = _outproj_ln((x_prompt.reshape(n_p, d), of_p.reshape(n_p, WIDTH), og_p.reshape(n_p, WIDTH)),
                        (x_sample.reshape(n_s, d), of_s.reshape(n_s, WIDTH), og_s.reshape(n_s, WIDTH)),
                        w_out_ab[0], ln1_g[0][None], ln1_b[0][None])
    x1 = tail(h, ht, 0, False)

    pool_args = (pool_w[0].astype(BF16), pool_scale[0][None], w_out_pool[0].astype(BF16),
                 ln1_g[1][None], ln1_b[1][None])
    ratio = TM_TOK // POOL_HALO
    x1_halo = x1.reshape(n // POOL_HALO, POOL_HALO, d)
    h_p, ht_p = _pool_ln(x1, lambda i: (i, 0), x1_halo, lambda i: (jnp.maximum(i * ratio - 1, 0), 0, 0),
                         n_p // TM_TOK, TM_TOK, 0, True, *pool_args)
    cache16 = jnp.pad(cache_pool[0].astype(F32), ((0, 0), (POOL_HALO - POOL_STATE, 0), (0, 0)))
    h_s, ht_s = _pool_ln(x1, lambda i: (n_p // dseq + i, 0), cache16, lambda i: (i, 0, 0),
                         n_sb, dseq, past, False, *pool_args)
    x2_p, x2_s = tail(jnp.concatenate([h_p, h_s], axis=0), jnp.concatenate([ht_p, ht_s], axis=0), 1, True)

    return (x2_p.reshape(n_pb, seq, d), x2_s.reshape(n_sb, dseq, d),
            up[:, COL_K:COL_K + WIDTH].reshape(1, n_pb, seq, N_HEADS, HEAD_DIM),
            up[:, COL_V:COL_V + WIDTH].reshape(1, n_pb, seq, N_HEADS, HEAD_DIM),
            up[:, COL_S:COL_S + N_HEADS].reshape(1, n_pb, seq, N_HEADS),
            st_p.reshape(1, n_pb, N_HEADS, HEAD_DIM, HEAD_DIM),
            up[seq - (CONV_WIDTH - 1):, COL_G:COL_G + gw].reshape(1, n_pb, CONV_WIDTH - 1, gw),
            x1[n_p - POOL_STATE:n_p].reshape(1, n_pb, POOL_STATE, d),
            us[:, COL_K:COL_K + WIDTH].reshape(1, n_sb, dseq, N_HEADS, HEAD_DIM),
            us[:, COL_V:COL_V + WIDTH].reshape(1, n_sb, dseq, N_HEADS, HEAD_DIM),
            us[:, COL_S:COL_S + N_HEADS].reshape(1, n_sb, dseq, N_HEADS),
            st_s.reshape(1, n_sb, N_HEADS, HEAD_DIM, HEAD_DIM),
            us[:, COL_G:COL_G + gw].reshape(n_sb, dseq, gw)[:, dseq - (CONV_WIDTH - 1):].reshape(
                1, n_sb, CONV_WIDTH - 1, gw),
            x1[n_p:].reshape(n_sb, dseq, d)[:, dseq - POOL_STATE:].reshape(1, n_sb, POOL_STATE, d))
```

```python
import functools

import numpy as np
import jax
import jax.numpy as jnp
from jax import lax
from jax.experimental import pallas as pl
from jax.experimental.pallas import tpu as pltpu
from jax.experimental.pallas import tpu_sc as plsc

F32 = jnp.float32
BF16 = jnp.bfloat16
I32 = jnp.int32
HIGHEST = lax.Precision.HIGHEST

LANES = 128
SUBLANES = 8
VMEM_LIMIT = 56 * 1024 * 1024

HEAD_DIM = 128
N_HEADS = 4
WIDTH = N_HEADS * HEAD_DIM
CHUNK = 64
CONV_WIDTH = 4
POOL_WINDOWS = (2, 4, 8, 16)
POOL_HALO = 16
POOL_STATE = 15
N_EXPERTS = 32
TOP_K = 4
SWIGLU_LIMIT = 7.0
SWIGLU_ALPHA = 1.702
DEPTH = 2
DN_ALPHA = (2 * DEPTH) ** 0.25
LN_EPS = 1e-5
NORM_EPS = 1e-6
NEG_INF = -1e30
LOG2E = 1.4426950408889634

COL_Q, COL_K, COL_V = 0, WIDTH, 2 * WIDTH
COL_G = 3 * WIDTH
COL_Z = 6 * WIDTH
COL_S = 7 * WIDTH
U_COLS = COL_S + LANES
LANE_F, LANE_A, LANE_B = 0, N_HEADS, 2 * N_HEADS

TM_PROJ = 256
TM_TOK = 512
SC_WINDOW = 32
MOE_BLOCK = 512
TQ = 1024
TK = 512
GDN_ROWS = 256
GDN_CHUNK = CHUNK


def _cparams(sem):
    return pltpu.CompilerParams(dimension_semantics=sem, vmem_limit_bytes=VMEM_LIMIT)


def _softplus(x):
    return jnp.maximum(x, 0.0) + jnp.log1p(jnp.exp(-jnp.abs(x)))


def _sigmoid(x):
    return 1.0 / (1.0 + jnp.exp(-x))


def _silu(x):
    return x * _sigmoid(x)


def _layer_norm(y, g, b):
    mu = jnp.mean(y, axis=-1, keepdims=True)
    yc = y - mu
    var = jnp.mean(yc * yc, axis=-1, keepdims=True)
    return yc * lax.rsqrt(var + LN_EPS) * g + b


def _dot_general(a, b, dims, hi):
    if hi:
        return lax.dot_general(a.astype(F32), b.astype(F32), (dims, ((), ())), precision=HIGHEST,
                               preferred_element_type=F32)
    return lax.dot_general(a.astype(BF16), b.astype(BF16), (dims, ((), ())), preferred_element_type=F32)


def _dot(a, b, hi=False):
    return _dot_general(a, b, ((1,), (0,)), hi)


def _dot_nt(a, b, hi=False):
    return _dot_general(a, b, ((1,), (1,)), hi)


def _dot_tn(a, b, hi=False):
    return _dot_general(a, b, ((0,), (0,)), hi)


def _spread_lanes(x, width):
    if width % LANES == 0:
        return jnp.concatenate([x] * (width // LANES), axis=1)
    return jnp.broadcast_to(x[:, 0:1], (x.shape[0], width))


def _load_token_tiles(ref, lead, n_tok):
    return jnp.concatenate([ref[(*lead, pl.ds(j, n_tok, stride=SUBLANES), slice(None))]
                            for j in range(SUBLANES)], axis=1)


def _store_token_tiles(ref, lead, x):
    for j in range(SUBLANES):
        ref[(*lead, pl.ds(j, x.shape[0], stride=SUBLANES), slice(None))] = x[:, j * LANES:(j + 1) * LANES]


def _lanes_to_rows(x, lane0):
    r = lax.broadcasted_iota(I32, (SUBLANES, LANES), 0)
    c = lax.broadcasted_iota(I32, (SUBLANES, LANES), 1)
    sel = (c == r + lane0).astype(F32)
    return lax.dot_general(sel, x, (((1,), (1,)), ((), ())), precision=HIGHEST,
                           preferred_element_type=F32)


def _proj_kernel(x_ref, w_ref, bf_ref, u_ref, *, hi):
    u = _dot(x_ref[...], w_ref[...], hi)
    u_ref[...] = u

    @pl.when(pl.program_id(1) == pl.num_programs(1) - 1)
    def _():
        small = u[:, u.shape[1] - LANES:]
        lane = lax.broadcasted_iota(I32, small.shape, 1)
        logf = -_softplus(-(small + bf_ref[...]))
        u_ref[:, u.shape[1] - LANES:] = jnp.where(lane < LANE_A, logf, small)


def _proj(x, w, bf_row, tn, hi):
    n, d = x.shape
    m = w.shape[1]
    return pl.pallas_call(
        functools.partial(_proj_kernel, hi=hi),
        grid=(n // TM_PROJ, m // tn),
        in_specs=[pl.BlockSpec((TM_PROJ, d), lambda i, j: (i, 0)),
                  pl.BlockSpec((d, tn), lambda i, j: (0, j)),
                  pl.BlockSpec((1, LANES), lambda i, j: (0, 0))],
        out_specs=pl.BlockSpec((TM_PROJ, tn), lambda i, j: (i, j)),
        out_shape=jax.ShapeDtypeStruct((n, m), F32),
        compiler_params=_cparams(("parallel", "parallel")),
        name="in_proj",
    )(x, w, bf_row)


def _cumsum_kernel(lf_ref, crep_ref, crow_ref, carry_ref):
    @pl.when(pl.program_id(1) == 0)
    def _():
        carry_ref[...] = jnp.zeros_like(carry_ref)

    lf = lf_ref[0]
    t = lf.shape[0]
    r = lax.broadcasted_iota(I32, (t, t), 0)
    c = lax.broadcasted_iota(I32, (t, t), 1)
    tril = (c <= r).astype(F32)
    cs = jnp.dot(tril, lf, precision=HIGHEST, preferred_element_type=F32) + carry_ref[0:1, :]
    carry_ref[...] = jnp.broadcast_to(cs[t - 1:t, :], carry_ref.shape)
    c2 = cs * LOG2E
    crow_ref[0] = _lanes_to_rows(c2, LANE_F)
    for h in range(N_HEADS):
        crep_ref[0, :, h * HEAD_DIM:(h + 1) * HEAD_DIM] = jnp.broadcast_to(
            c2[:, LANE_F + h:LANE_F + h + 1], (t, HEAD_DIM))


def _cumsum(arr, n_batch, length, tl, col_block):
    return pl.pallas_call(
        _cumsum_kernel,
        grid=(n_batch, length // tl),
        in_specs=[pl.BlockSpec((1, tl, LANES), lambda b, j: (b, j, col_block))],
        out_specs=[pl.BlockSpec((1, tl, WIDTH), lambda b, j: (b, j, 0)),
                   pl.BlockSpec((1, SUBLANES, tl), lambda b, j: (b, 0, j))],
        out_shape=[jax.ShapeDtypeStruct((n_batch, length, WIDTH), F32),
                   jax.ShapeDtypeStruct((n_batch, SUBLANES, length), F32)],
        scratch_shapes=[pltpu.VMEM((SUBLANES, LANES), F32)],
        compiler_params=_cparams(("parallel", "arbitrary")),
        name="logf_cumsum",
    )(arr)


def _fox_kernel(qi_ref, kj_ref, last_ref, q_ref, k_ref, v_ref, cq_ref, ck_ref, o_ref,
                m_ref, l_ref, acc_ref, *, tq, tk, past, hi):
    s_idx = pl.program_id(1)
    qi = qi_ref[s_idx]
    kj = kj_ref[s_idx]

    @pl.when(kj == 0)
    def _():
        m_ref[...] = jnp.full_like(m_ref, NEG_INF)
        l_ref[...] = jnp.zeros_like(l_ref)
        acc_ref[...] = jnp.zeros_like(acc_ref)

    def update(masked):
        if masked:
            q_pos = past + qi * tq + lax.broadcasted_iota(I32, (tq, tk), 0)
            k_pos = kj * tk + lax.broadcasted_iota(I32, (tq, tk), 1)
            visible = k_pos <= q_pos
        for h in range(N_HEADS):
            cols = slice(h * HEAD_DIM, (h + 1) * HEAD_DIM)
            q = q_ref[0, :, cols] * (HEAD_DIM ** -0.5 * LOG2E)
            t = _dot_nt(q, k_ref[0, :, cols], hi) - ck_ref[0, h:h + 1, :]
            if masked:
                t = jnp.where(visible, t, NEG_INF)
            cq = cq_ref[0, :, cols]
            m_prev = m_ref[h]
            m_new = jnp.maximum(m_prev, jnp.max(t, axis=-1, keepdims=True) + cq)
            p = jnp.exp2(t - _spread_lanes(m_new - cq, tk))
            alpha = jnp.exp2(m_prev - m_new)
            l_ref[h] = alpha * l_ref[h] + jnp.sum(p, axis=-1, keepdims=True)
            acc_ref[:, cols] = alpha * acc_ref[:, cols] + _dot(p, v_ref[0, :, cols], hi)
            m_ref[h] = m_new

    crosses_diagonal = kj * tk + (tk - 1) > past + qi * tq
    pl.when(crosses_diagonal)(functools.partial(update, True))
    pl.when(jnp.logical_not(crosses_diagonal))(functools.partial(update, False))

    @pl.when(last_ref[s_idx] == 1)
    def _():
        for h in range(N_HEADS):
            cols = slice(h * HEAD_DIM, (h + 1) * HEAD_DIM)
            o_ref[0, :, cols] = acc_ref[:, cols] / l_ref[h]


def _fox_schedule(n_q, tq, tk, past):
    qi, kj, last = [], [], []
    for i in range(n_q):
        hi = (past + (i + 1) * tq - 1) // tk
        for j in range(hi + 1):
            qi.append(i)
            kj.append(j)
            last.append(1 if j == hi else 0)
    return (jnp.asarray(np.array(qi, np.int32)), jnp.asarray(np.array(kj, np.int32)),
            jnp.asarray(np.array(last, np.int32)))


def _fox(q_arr, q_map, k_arr, k_map, v_arr, v_map, cq_arr, cq_map, ck_arr, ck_map,
         n_batch, n_q, tq, tk, past, hi=False):
    qi, kj, last = _fox_schedule(n_q, tq, tk, past)
    n_steps = int(qi.shape[0])
    spec = lambda shape, fn, tab: pl.BlockSpec(shape, lambda b, s, qi_r, kj_r, la_r: fn(b, (qi_r if tab == 'q' else kj_r)[s]))
    return pl.pallas_call(
        functools.partial(_fox_kernel, tq=tq, tk=tk, past=past, hi=hi),
        grid_spec=pltpu.PrefetchScalarGridSpec(
            num_scalar_prefetch=3,
            grid=(n_batch, n_steps),
            in_specs=[spec((1, tq, WIDTH), q_map, 'q'),
                      spec((1, tk, WIDTH), k_map, 'k'),
                      spec((1, tk, WIDTH), v_map, 'k'),
                      spec((1, tq, WIDTH), cq_map, 'q'),
                      spec((1, SUBLANES, tk), ck_map, 'k')],
            out_specs=spec((1, tq, WIDTH), lambda b, i: (b, i, 0), 'q'),
            scratch_shapes=[pltpu.VMEM((N_HEADS, tq, HEAD_DIM), F32),
                            pltpu.VMEM((N_HEADS, tq, HEAD_DIM), F32),
                            pltpu.VMEM((tq, WIDTH), F32)]),
        out_shape=jax.ShapeDtypeStruct((n_batch, n_q * tq, WIDTH), F32),
        compiler_params=_cparams(("parallel", "arbitrary")),
        name="fox_attention",
    )(qi, kj, last, q_arr, k_arr, v_arr, cq_arr, ck_arr)


def _gdn_kernel(first_ref, last_ref, seq_ref,
                pre_ref, z_ref, sm_ref, cpast_ref, convw_ref, s0_ref, alog_ref, dtb_ref, ng_ref,
                o_ref, sout_ref, stage_ref, s_ref, *, rows, chunk, hi):
    dot, dot_nt, dot_tn = (functools.partial(f, hi=hi) for f in (_dot, _dot_nt, _dot_tn))
    step = pl.program_id(0)
    halo = SUBLANES
    n_chunks = rows // chunk

    @pl.when(first_ref[step] == 1)
    def _():
        stage_ref[0:halo, :] = cpast_ref[0]
        s_ref[...] = s0_ref[0]

    stage_ref[halo:halo + rows, :] = pre_ref[0]
    conv = stage_ref[halo:halo + rows, :] * convw_ref[CONV_WIDTH - 1:CONV_WIDTH, :]
    for j in range(1, CONV_WIDTH):
        conv = conv + (stage_ref[halo - j:halo - j + rows, :]
                       * convw_ref[CONV_WIDTH - 1 - j:CONV_WIDTH - j, :])
    stage_ref[0:halo, :] = stage_ref[rows:rows + halo, :]
    act = _silu(conv)

    small = sm_ref[0]
    beta_all = _sigmoid(small)
    g_all = -jnp.exp(alog_ref[...]) * _softplus(small + dtb_ref[...])
    r = lax.broadcasted_iota(I32, (rows, rows), 0)
    c = lax.broadcasted_iota(I32, (rows, rows), 1)
    same_chunk = (r // chunk) == (c // chunk)
    incl = same_chunk & (c <= r)
    strict = same_chunk & (c < r)
    eye = (c == r).astype(F32)
    gc_all = jnp.dot(incl.astype(F32), g_all, precision=HIGHEST, preferred_element_type=F32)
    gc_rows = _lanes_to_rows(gc_all, LANE_A)

    heads = range(N_HEADS)
    head_cols = [slice(h * HEAD_DIM, (h + 1) * HEAD_DIM) for h in heads]
    q, k, gc, decay, kb, vb, low = [], [], [], [], [], [], []
    for h in heads:
        qh = act[:, h * HEAD_DIM:(h + 1) * HEAD_DIM]
        kh = act[:, WIDTH + h * HEAD_DIM:WIDTH + (h + 1) * HEAD_DIM]
        vh = act[:, 2 * WIDTH + h * HEAD_DIM:2 * WIDTH + (h + 1) * HEAD_DIM]
        q.append(qh * lax.rsqrt(jnp.sum(qh * qh, axis=-1, keepdims=True) + NORM_EPS) * (HEAD_DIM ** -0.5))
        k.append(kh * lax.rsqrt(jnp.sum(kh * kh, axis=-1, keepdims=True) + NORM_EPS))
        beta = beta_all[:, LANE_B + h:LANE_B + h + 1]
        gc.append(gc_all[:, LANE_A + h:LANE_A + h + 1])
        diff = gc[h] - gc_rows[h:h + 1, :]
        decay.append(jnp.where(incl, jnp.exp(jnp.where(incl, diff, 0.0)), 0.0))
        kb.append(k[h] * beta)
        vb.append(vh * beta)
    for h in heads:
        low.append(jnp.where(strict, dot_nt(kb[h], k[h]) * decay[h], 0.0))
    inv = [eye - low[h] for h in heads]
    pw = [dot(low[h], low[h]) for h in heads]
    n_sq = chunk.bit_length() - 2
    for it in range(n_sq):
        inv = [inv[h] + dot(inv[h], pw[h]) for h in heads]
        if it + 1 < n_sq:
            pw = [dot(pw[h], pw[h]) for h in heads]
    egc = [jnp.exp(gc[h]) for h in heads]
    uw = [dot(inv[h], jnp.concatenate([vb[h], kb[h] * egc[h]], axis=1)) for h in heads]
    intra = [jnp.where(incl, dot_nt(q[h], k[h]) * decay[h], 0.0) for h in heads]
    qd = [q[h] * egc[h] for h in heads]
    g_last = [[gc[h][(g + 1) * chunk - 1:(g + 1) * chunk, :] for g in range(n_chunks)] for h in heads]
    kd = [k[h] * jnp.exp(jnp.concatenate([jnp.broadcast_to(gl, (chunk, 1)) for gl in g_last[h]], axis=0)
                         - gc[h]) for h in heads]
    state = [s_ref[h] for h in heads]
    v_new = [[] for _ in heads]
    for g in range(n_chunks):
        rs = slice(g * chunk, (g + 1) * chunk)
        for h in heads:
            v_new[h].append(uw[h][rs, :HEAD_DIM] - dot(uw[h][rs, HEAD_DIM:], state[h]))
        for h in heads:
            v_rows = jnp.concatenate(
                v_new[h] + [jnp.zeros((rows - (g + 1) * chunk, HEAD_DIM), F32)] * (g + 1 < n_chunks), axis=0)
            o = dot(qd[h][rs, :], state[h]) + dot(intra[h][rs, :], v_rows)
            state[h] = state[h] * jnp.exp(g_last[h][g]) + dot_tn(kd[h][rs, :], v_new[h][g])
            o = (o * lax.rsqrt(jnp.mean(o * o, axis=-1, keepdims=True) + NORM_EPS)
                 * ng_ref[...] * _silu(z_ref[0, rs, head_cols[h]]))
            o_ref[0, rs, head_cols[h]] = o
    for h in heads:
        s_ref[h] = state[h]

    @pl.when(last_ref[step] == 1)
    def _():
        sout_ref[0] = s_ref[...]


def _gdn(u_view, hi, first, last, seq, conv_past, conv_w, s0, alog_row, dtb_row, ng_row):
    n_steps, rows, _ = u_view.shape
    n_seq = s0.shape[0]
    gw = 3 * WIDTH
    return pl.pallas_call(
        functools.partial(_gdn_kernel, rows=rows, chunk=min(rows, GDN_CHUNK), hi=hi),
        grid_spec=pltpu.PrefetchScalarGridSpec(
            num_scalar_prefetch=3,
            grid=(n_steps,),
            in_specs=[pl.BlockSpec((1, rows, gw), lambda s, f, l, q: (s, 0, COL_G // gw)),
                      pl.BlockSpec((1, rows, WIDTH), lambda s, f, l, q: (s, 0, COL_Z // WIDTH)),
                      pl.BlockSpec((1, rows, LANES), lambda s, f, l, q: (s, 0, COL_S // LANES)),
                      pl.BlockSpec((1, SUBLANES, gw), lambda s, f, l, q: (q[s], 0, 0)),
                      pl.BlockSpec((SUBLANES, gw), lambda s, f, l, q: (0, 0)),
                      pl.BlockSpec((1, N_HEADS, HEAD_DIM, HEAD_DIM), lambda s, f, l, q: (q[s], 0, 0, 0)),
                      pl.BlockSpec((1, LANES), lambda s, f, l, q: (0, 0)),
                      pl.BlockSpec((1, LANES), lambda s, f, l, q: (0, 0)),
                      pl.BlockSpec((1, LANES), lambda s, f, l, q: (0, 0))],
            out_specs=[pl.BlockSpec((1, rows, WIDTH), lambda s, f, l, q: (s, 0, 0)),
                       pl.BlockSpec((1, N_HEADS, HEAD_DIM, HEAD_DIM), lambda s, f, l, q: (q[s], 0, 0, 0))],
            scratch_shapes=[pltpu.VMEM((rows + SUBLANES, gw), F32),
                            pltpu.VMEM((N_HEADS, HEAD_DIM, HEAD_DIM), F32)]),
        out_shape=[jax.ShapeDtypeStruct((n_steps, rows, WIDTH), F32),
                   jax.ShapeDtypeStruct((n_seq, N_HEADS, HEAD_DIM, HEAD_DIM), F32)],
        compiler_params=_cparams(("arbitrary",)),
        name="gated_deltanet",
    )(first, last, seq, u_view, u_view, u_view, conv_past, conv_w, s0, alog_row, dtb_row, ng_row)


def _outproj_ln_kernel(xp_ref, ofp_ref, ogp_ref, xs_ref, ofs_ref, ogs_ref, w_ref, wf_ref, g_ref, b_ref,
                       h_ref, ht_ref, *, hi_from):
    def run(hi, x_ref, of_ref, og_ref):
        w = wf_ref if hi else w_ref
        mix = _dot(of_ref[...], w[0:WIDTH, :], hi) + _dot(og_ref[...], w[WIDTH:2 * WIDTH, :], hi)
        h = _layer_norm(DN_ALPHA * x_ref[...] + mix, g_ref[...], b_ref[...])
        h_ref[...] = h
        _store_token_tiles(ht_ref, (), h)

    pl.when(pl.program_id(0) < hi_from)(functools.partial(run, False, xp_ref, ofp_ref, ogp_ref))
    pl.when(pl.program_id(0) >= hi_from)(functools.partial(run, True, xs_ref, ofs_ref, ogs_ref))


def _outproj_ln(prompt, sample, w, g_row, b_row):
    n_p, d = prompt[0].shape
    n = n_p + sample[0].shape[0]
    hi_from = n_p // TM_TOK
    row = lambda i: (i, 0)
    row_p = lambda i: (jnp.minimum(i, hi_from - 1), 0)
    row_s = lambda i: (jnp.maximum(i - hi_from, 0), 0)
    fixed = lambda i: (0, 0)
    group = lambda rows: [pl.BlockSpec((TM_TOK, d), rows), pl.BlockSpec((TM_TOK, WIDTH), rows),
                          pl.BlockSpec((TM_TOK, WIDTH), rows)]
    return pl.pallas_call(
        functools.partial(_outproj_ln_kernel, hi_from=hi_from),
        grid=(n // TM_TOK,),
        in_specs=group(row_p) + group(row_s) + [pl.BlockSpec((2 * WIDTH, d), fixed),
                                                pl.BlockSpec((2 * WIDTH, d), fixed),
                                                pl.BlockSpec((1, d), fixed), pl.BlockSpec((1, d), fixed)],
        out_specs=[pl.BlockSpec((TM_TOK, d), row), pl.BlockSpec((TM_TOK * SUBLANES, LANES), row)],
        out_shape=[jax.ShapeDtypeStruct((n, d), F32), jax.ShapeDtypeStruct((n * SUBLANES, LANES), F32)],
        compiler_params=_cparams(("parallel",)),
        name="out_proj_ln",
    )(*prompt, *sample, w.astype(BF16), w, g_row, b_row)


def _pool_ln_kernel(x_ref, halo_ref, pw_ref, ps_ref, w_ref, g_ref, b_ref, h_ref, ht_ref, stage_ref,
                    *, tm, pos0, zero_first_halo):
    i = pl.program_id(0)
    stage_ref[0:POOL_HALO, :] = halo_ref[0]
    if zero_first_halo:
        @pl.when(i == 0)
        def _():
            stage_ref[0:POOL_HALO, :] = jnp.zeros((POOL_HALO, stage_ref.shape[1]), F32)
    x = x_ref[...]
    stage_ref[POOL_HALO:POOL_HALO + tm, :] = x
    gdim = x.shape[1] // len(POOL_WINDOWS)
    pos = pos0 + lax.broadcasted_iota(I32, (tm, 1), 0)
    if zero_first_halo:
        pos = pos + i * tm
    parts = []
    for gi, win in enumerate(POOL_WINDOWS):
        cols = slice(gi * gdim, (gi + 1) * gdim)
        s = stage_ref[POOL_HALO:POOL_HALO + tm, cols]
        for j in range(1, win):
            s = s + stage_ref[POOL_HALO - j:POOL_HALO - j + tm, cols]
        cnt = jnp.minimum(pos + 1, win).astype(F32)
        zg = s / cnt - x[:, cols]
        parts.append(_dot(zg, pw_ref[gi]))
    zg = jnp.concatenate(parts, axis=-1) * ps_ref[...]
    mix = _dot(zg, w_ref[...])
    h = _layer_norm(DN_ALPHA * x + mix, g_ref[...], b_ref[...])
    h_ref[...] = h
    _store_token_tiles(ht_ref, (), h)


def _pool_ln_into_kernel(*refs, **kw):
    _pool_ln_kernel(*refs[:7], *refs[9:], **kw)


def _pool_ln(x, x_map, halo_arr, halo_map, n_tiles, tm, pos0, zero_first_halo,
             pw_bf16, ps_row, w_bf16, g_row, b_row, into=None):
    n, d = x.shape
    gdim = d // len(POOL_WINDOWS)
    fixed = lambda i: (0, 0)
    in_specs = [pl.BlockSpec((tm, d), x_map),
                pl.BlockSpec((1, POOL_HALO, d), halo_map),
                pl.BlockSpec((len(POOL_WINDOWS), gdim, gdim), lambda i: (0, 0, 0)),
                pl.BlockSpec((1, d), fixed), pl.BlockSpec((d, d), fixed),
                pl.BlockSpec((1, d), fixed), pl.BlockSpec((1, d), fixed)]
    args = (x, halo_arr, pw_bf16, ps_row, w_bf16, g_row, b_row)
    kw = dict(tm=tm, pos0=pos0, zero_first_halo=zero_first_halo)
    return pl.pallas_call(
        functools.partial(_pool_ln_kernel if into is None else _pool_ln_into_kernel, **kw),
        grid=(n_tiles,),
        in_specs=in_specs + ([] if into is None else [pl.BlockSpec(memory_space=pl.ANY)] * 2),
        out_specs=[pl.BlockSpec((tm, d), x_map),
                   pl.BlockSpec((tm * SUBLANES, LANES), x_map)],
        out_shape=[jax.ShapeDtypeStruct((n, d), F32), jax.ShapeDtypeStruct((n * SUBLANES, LANES), F32)],
        input_output_aliases={} if into is None else {7: 0, 8: 1},
        scratch_shapes=[pltpu.VMEM((POOL_HALO + tm, d), F32)],
        compiler_params=_cparams(("arbitrary",)),
        name="pool_mixer_ln",
    )(*args, *(() if into is None else into))


def _router_kernel(h_ref, wr_ref, br_ref, idx_ref, gate_ref, rank_ref, pstart_ref, tab_ref, carry_ref):
    @pl.when(pl.program_id(0) == 0)
    def _():
        carry_ref[...] = jnp.zeros_like(carry_ref)

    tm = h_ref.shape[0]
    lane = lax.broadcasted_iota(I32, (tm, LANES), 1).astype(F32)
    logits = jnp.dot(h_ref[...], wr_ref[...], precision=HIGHEST, preferred_element_type=F32)
    work = jnp.where(lane < N_EXPERTS, logits + br_ref[...], -jnp.inf)
    vals, ids = [], []
    for _ in range(TOP_K):
        m = jnp.max(work, axis=-1, keepdims=True)
        ik = jnp.min(jnp.where(work == m, lane, float(LANES)), axis=-1, keepdims=True)
        vals.append(m)
        ids.append(ik)
        work = jnp.where(lane == ik, -jnp.inf, work)
    exps = [jnp.exp(v - vals[0]) for v in vals]
    denom = exps[0]
    for e in exps[1:]:
        denom = denom + e
    multihot = jnp.zeros((tm, LANES), F32)
    idx_out = jnp.zeros((tm, LANES), F32)
    gate_out = jnp.zeros((tm, LANES), F32)
    for k in range(TOP_K):
        multihot = multihot + (lane == ids[k]).astype(F32)
        idx_out = jnp.where(lane == k, ids[k], idx_out)
        gate_out = jnp.where(lane == k, exps[k] / denom, gate_out)
    r = lax.broadcasted_iota(I32, (tm, tm), 0)
    c = lax.broadcasted_iota(I32, (tm, tm), 1)
    before = _dot((c < r).astype(F32), multihot) + carry_ref[0:1, :]
    rank_out = jnp.zeros((tm, LANES), F32)
    for k in range(TOP_K):
        rk = jnp.sum(jnp.where(lane == ids[k], before, 0.0), axis=-1, keepdims=True)
        rank_out = jnp.where(lane == k, rk, rank_out)
    idx_ref[...] = _lanes_to_rows(idx_out, 0).astype(I32)
    rank_ref[...] = _lanes_to_rows(rank_out, 0).astype(I32)
    gate_ref[...] = gate_out
    total = carry_ref[0:1, :] + jnp.sum(multihot, axis=0, keepdims=True)
    carry_ref[...] = jnp.broadcast_to(total, carry_ref.shape)

    @pl.when(pl.program_id(0) == pl.num_programs(0) - 1)
    def _():
        n_rows = tab_ref.shape[0]
        padded = jnp.floor((total + (MOE_BLOCK - 1)) / MOE_BLOCK) * MOE_BLOCK
        rr = lax.broadcasted_iota(I32, (LANES, LANES), 0)
        cc = lax.broadcasted_iota(I32, (LANES, LANES), 1)
        pend = jnp.dot(jnp.broadcast_to(padded, (SUBLANES, LANES)), (rr <= cc).astype(F32),
                       precision=HIGHEST, preferred_element_type=F32)[0:1, :]
        pstart = pend - padded
        n_used = jnp.max(pend, axis=-1, keepdims=True) / MOE_BLOCK
        elane = lax.broadcasted_iota(I32, (n_rows, LANES), 1)
        blk = jnp.minimum(lax.broadcasted_iota(I32, (n_rows, 1), 0).astype(F32), n_used - 1.0) * MOE_BLOCK
        ends_before = jnp.where((elane < N_EXPERTS) & (pend <= blk), 1.0, 0.0)
        block_e = jnp.minimum(jnp.sum(ends_before, axis=-1, keepdims=True), N_EXPERTS - 1.0)
        mine = elane.astype(F32) == block_e
        filled = jnp.sum(jnp.where(mine, total - (blk - pstart), 0.0), axis=-1, keepdims=True)
        block_valid = jnp.clip(filled, 0.0, float(MOE_BLOCK))
        tab = jnp.where(elane == 0, block_e,
                        jnp.where(elane == 1, block_valid, jnp.where(elane == 2, n_used, 0.0)))
        tab_ref[...] = tab.astype(I32)
        pstart_ref[...] = jnp.broadcast_to(pstart, pstart_ref.shape).astype(I32)


def _router(h, wr_pad, br_row, n_blocks):
    n, d = h.shape
    row = lambda i: (i, 0)
    col = lambda i: (0, i)
    fixed = lambda i: (0, 0)
    tab_rows = -(-n_blocks // SUBLANES) * SUBLANES
    return pl.pallas_call(
        _router_kernel,
        grid=(n // TM_TOK,),
        in_specs=[pl.BlockSpec((TM_TOK, d), row), pl.BlockSpec((d, LANES), fixed),
                  pl.BlockSpec((1, LANES), fixed)],
        out_specs=[pl.BlockSpec((SUBLANES, TM_TOK), col), pl.BlockSpec((TM_TOK, LANES), row),
                   pl.BlockSpec((SUBLANES, TM_TOK), col), pl.BlockSpec((SUBLANES, LANES), fixed),
                   pl.BlockSpec((tab_rows, LANES), fixed)],
        out_shape=[jax.ShapeDtypeStruct((SUBLANES, n), I32), jax.ShapeDtypeStruct((n, LANES), F32),
                   jax.ShapeDtypeStruct((SUBLANES, n), I32), jax.ShapeDtypeStruct((SUBLANES, LANES), I32),
                   jax.ShapeDtypeStruct((tab_rows, LANES), I32)],
        scratch_shapes=[pltpu.VMEM((SUBLANES, LANES), F32)],
        compiler_params=_cparams(("arbitrary",)),
        name="moe_router",
    )(h, wr_pad, br_row)


def _expert_kernel(be_ref, nu_ref, valid_ref, xb_ref, wup_ref, bup_ref, wdn_ref, bdn_ref, yb_ref,
                   wup_bf_ref, wdn_bf_ref):
    b = pl.program_id(0)

    @pl.when(b < nu_ref[0])
    def _():
        @pl.when((b == 0) | (be_ref[b] != be_ref[jnp.maximum(b - 1, 0)]))
        def _():
            wup_bf_ref[...] = wup_ref[0, 0].astype(BF16)
            wdn_bf_ref[...] = wdn_ref[0, 0].astype(BF16)

        d_exp = wdn_ref.shape[2]
        x = _load_token_tiles(xb_ref, (), MOE_BLOCK)
        row = lax.broadcasted_iota(I32, (MOE_BLOCK, 1), 0)
        x = jnp.where(row < valid_ref[b], x, 0.0)
        hu = jnp.dot(x.astype(BF16), wup_bf_ref[...], preferred_element_type=F32) + bup_ref[0, 0]
        glu = jnp.minimum(hu[:, :d_exp], SWIGLU_LIMIT)
        lin = jnp.clip(hu[:, d_exp:], -SWIGLU_LIMIT, SWIGLU_LIMIT)
        a = glu * _sigmoid(SWIGLU_ALPHA * glu) * (lin + 1.0)
        y = jnp.dot(a.astype(BF16), wdn_bf_ref[...], preferred_element_type=F32) + bdn_ref[0, 0]
        _store_token_tiles(yb_ref, (), y)

    @pl.when(b >= nu_ref[0])
    def _():
        yb_ref[...] = jnp.zeros_like(yb_ref)


def _experts(xb, block_e, n_used, block_valid, layer, w_up, b_up, w_dn, b_dn):
    d = SUBLANES * LANES
    rows = MOE_BLOCK * SUBLANES
    n_blocks = xb.shape[0] // rows
    d_up = w_up.shape[3]
    d_exp = w_dn.shape[2]
    wsel = lambda b, be, nu, va: (layer, be[b], 0, 0)
    return pl.pallas_call(
        _expert_kernel,
        grid_spec=pltpu.PrefetchScalarGridSpec(
            num_scalar_prefetch=3,
            grid=(n_blocks,),
            in_specs=[pl.BlockSpec((rows, LANES), lambda b, be, nu, va: (jnp.minimum(b, nu[0] - 1), 0)),
                      pl.BlockSpec((1, 1, d, d_up), wsel),
                      pl.BlockSpec((1, 1, 1, d_up), wsel),
                      pl.BlockSpec((1, 1, d_exp, d), wsel),
                      pl.BlockSpec((1, 1, 1, d), wsel)],
            out_specs=pl.BlockSpec((rows, LANES), lambda b, be, nu, va: (b, 0)),
            scratch_shapes=[pltpu.VMEM((d, d_up), BF16), pltpu.VMEM((d_exp, d), BF16)]),
        out_shape=jax.ShapeDtypeStruct(xb.shape, F32),
        compiler_params=_cparams(("arbitrary",)),
        name="moe_experts",
    )(block_e, n_used, block_valid, xb, w_up, b_up, w_dn, b_dn)


def _sc_gather_tiles(table, idx):
    m = idx.shape[0]
    mesh = plsc.VectorSubcoreMesh(core_axis_name="core", subcore_axis_name="subcore")

    @functools.partial(pl.kernel, out_type=jax.ShapeDtypeStruct((m, SUBLANES, LANES), table.dtype), mesh=mesh)
    def gather(table_hbm, idx_hbm, out_hbm):
        def window(idx_vmem, out_vmem):
            pltpu.sync_copy(table_hbm.at[idx_vmem.at[0, pl.ds(0, SC_WINDOW)]], out_vmem)

        pltpu.emit_pipeline(
            window,
            grid=(m // SC_WINDOW,),
            in_specs=[pl.BlockSpec((1, LANES), lambda i: (i, 0))],
            out_specs=[pl.BlockSpec((SC_WINDOW, SUBLANES, LANES), lambda i: (i, 0, 0))],
            core_axis_name=("core", "subcore"),
            dimension_semantics=(pltpu.PARALLEL,),
        )(idx_hbm, out_hbm)

    idx_rows = jnp.pad(idx.reshape(m // SC_WINDOW, SC_WINDOW), ((0, 0), (0, LANES - SC_WINDOW)))
    return gather(table, idx_rows)


def _sc_scatter_tiles(tiles, idx_by_choice, n_out):
    n = tiles.shape[0]
    mesh = plsc.VectorSubcoreMesh(core_axis_name="core", subcore_axis_name="subcore")

    @functools.partial(pl.kernel, out_type=jax.ShapeDtypeStruct((n_out, SUBLANES, LANES), tiles.dtype),
                       mesh=mesh)
    def scatter(tiles_hbm, *refs):
        idx_hbm, out_hbm = refs[:TOP_K], refs[TOP_K]

        def window(tiles_vmem, *idx_vmem):
            for k in range(TOP_K):
                pltpu.sync_copy(tiles_vmem, out_hbm.at[idx_vmem[k].at[0, pl.ds(0, SC_WINDOW)]])

        pltpu.emit_pipeline(
            window,
            grid=(n // SC_WINDOW,),
            in_specs=[pl.BlockSpec((SC_WINDOW, SUBLANES, LANES), lambda i: (i, 0, 0))]
                     + [pl.BlockSpec((1, LANES), lambda i: (i, 0))] * TOP_K,
            out_specs=[],
            core_axis_name=("core", "subcore"),
            dimension_semantics=(pltpu.PARALLEL,),
        )(tiles_hbm, *idx_hbm)

    idx_rows = jnp.pad(idx_by_choice.reshape(TOP_K, n // SC_WINDOW, SC_WINDOW),
                       ((0, 0), (0, 0), (0, LANES - SC_WINDOW)))
    return scatter(tiles, *[idx_rows[k] for k in range(TOP_K)])


def _combine_kernel(h_ref, y0_ref, y1_ref, y2_ref, y3_ref, gate_ref, pp_ref, ps_ref, wpg_ref, wpp_ref,
                    g_ref, b_ref, *outs, n_ptiles):
    outp_ref, outs_ref = outs[0], outs[-1]
    tm = h_ref.shape[0]
    gate = gate_ref[...]
    moe = _load_token_tiles(y0_ref, (), tm) * gate[:, 0:1]
    for k, y_ref in enumerate((y1_ref, y2_ref, y3_ref), start=1):
        moe = moe + _load_token_tiles(y_ref, (), tm) * gate[:, k:k + 1]
    h2 = _layer_norm(DN_ALPHA * h_ref[...] + moe, g_ref[...], b_ref[...])
    embed_gate = _sigmoid(_dot(h2, wpg_ref[...]))

    def finish(p_ref, out_ref):
        out_ref[...] = h2 + embed_gate * _dot(p_ref[...], wpp_ref[...])

    is_prompt = pl.program_id(0) < n_ptiles
    pl.when(is_prompt)(functools.partial(finish, pp_ref, outp_ref))
    pl.when(jnp.logical_not(is_prompt))(functools.partial(finish, ps_ref, outs_ref))


def _combine(h, y, gate, p_prompt, p_sample, layer, n_p, split, wpg_bf16, wpp_bf16, g_row, b_row):
    n, d = h.shape
    e = p_prompt.shape[1]
    n_tiles = n // TM_TOK
    n_ptiles = n_p // TM_TOK
    n_stiles = (n - n_p) // TM_TOK
    row = lambda i: (i, 0)
    row_p = lambda i: (jnp.minimum(i, n_ptiles - 1), 0)
    row_s = lambda i: (jnp.maximum(i - n_ptiles, 0), 0)
    fixed = lambda i: (0, 0)
    choice = lambda k: pl.BlockSpec((TM_TOK * SUBLANES, LANES), lambda i: (k * n_tiles + i, 0))
    if split:
        out_specs = [pl.BlockSpec((TM_TOK, d), row_p), pl.BlockSpec((TM_TOK, d), row_s)]
        out_shape = [jax.ShapeDtypeStruct((n_p, d), F32), jax.ShapeDtypeStruct((n - n_p, d), F32)]
    else:
        out_specs = pl.BlockSpec((TM_TOK, d), row)
        out_shape = jax.ShapeDtypeStruct((n, d), F32)
    return pl.pallas_call(
        functools.partial(_combine_kernel, n_ptiles=n_ptiles),
        grid=(n_tiles,),
        in_specs=[pl.BlockSpec((TM_TOK, d), row)] + [choice(k) for k in range(TOP_K)]
                 + [pl.BlockSpec((TM_TOK, LANES), row),
                    pl.BlockSpec((TM_TOK, e), lambda i: (layer * n_ptiles + row_p(i)[0], 0)),
                    pl.BlockSpec((TM_TOK, e), lambda i: (layer * n_stiles + row_s(i)[0], 0)),
                    pl.BlockSpec((d, d), fixed), pl.BlockSpec((e, d), fixed), pl.BlockSpec((1, d), fixed),
                    pl.BlockSpec((1, d), fixed)],
        out_specs=out_specs,
        out_shape=out_shape,
        compiler_params=_cparams(("arbitrary",)),
        name="moe_combine_ln_embed",
    )(h, y, y, y, y, gate, p_prompt, p_sample, wpg_bf16, wpp_bf16, g_row, b_row)


def _layer_tail(h, ht, p_prompt, p_sample, layer, n_p, split, g2, b2, w_r, b_r, w_up, b_up, w_dn, b_dn,
                w_pg, w_pp):
    n, d = h.shape
    wr_pad = jnp.pad(w_r, ((0, 0), (0, LANES - N_EXPERTS)))
    br_row = jnp.pad(b_r, (0, LANES - N_EXPERTS))[None]
    n_asg = n * TOP_K
    n_blocks = n_asg // MOE_BLOCK + N_EXPERTS
    assert n_asg % MOE_BLOCK == 0 and n % SC_WINDOW == 0
    n_slots = n_blocks * MOE_BLOCK
    idx, gate, rank, pstart, tab = _router(h, wr_pad, br_row, n_blocks)
    slot_by_choice = pstart[0, :N_EXPERTS][idx[:TOP_K]] + rank[:TOP_K]
    xb = _sc_scatter_tiles(ht.reshape(n, SUBLANES, LANES), slot_by_choice, n_slots)
    yb = _experts(xb.reshape(n_slots * SUBLANES, LANES), tab[:n_blocks, 0], tab[0, 2:3], tab[:n_blocks, 1],
                  layer, w_up, b_up[:, :, None, :], w_dn, b_dn[:, :, None, :])
    y = _sc_gather_tiles(yb.reshape(n_slots, SUBLANES, LANES), slot_by_choice.reshape(-1))
    return _combine(h, y.reshape(n_asg * SUBLANES, LANES), gate, p_prompt, p_sample, layer, n_p, split,
                    w_pg.astype(BF16), w_pp.astype(BF16), g2[None], b2[None])


def _lane_row(v, lane0):
    return jnp.zeros((1, LANES), F32).at[0, lane0:lane0 + v.shape[0]].set(v.astype(F32))


def kernel(x_prompt, x_sample, cache_fox_k, cache_fox_v, cache_fox_logf, state_gdn, state_gdn_conv,
           cache_pool, p_prompt, p_sample, w_in_ab, b_fgate, gdn_a_log, gdn_dt_bias, gdn_conv_w,
           gdn_norm_g, w_out_ab, pool_w, pool_scale, w_out_pool, ln1_g, ln1_b, ln2_g, ln2_b,
           w_router, b_router, w_expert_up, b_expert_up, w_expert_down, b_expert_down,
           w_ple_gate, w_ple_proj):
    n_pb, seq, d = x_prompt.shape
    n_sb, dseq, _ = x_sample.shape
    past = cache_fox_k.shape[2]
    assert n_pb == 1 and dseq == CHUNK and past % dseq == 0 and seq % TQ == 0
    assert d == SUBLANES * LANES
    n_p = n_pb * seq
    n_s = n_sb * dseq
    n = n_p + n_s
    assert n_p % TM_TOK == 0 and n_s % TM_TOK == 0
    n_layers = p_prompt.shape[0]
    pp_all = p_prompt.reshape(n_layers * n_p, -1)
    ps_all = p_sample.reshape(n_layers * n_s, -1)

    def tail(h, ht, i, split):
        return _layer_tail(h, ht, pp_all, ps_all, i, n_p, split, ln2_g[i], ln2_b[i], w_router[i], b_router[i],
                           w_expert_up, b_expert_up, w_expert_down, b_expert_down, w_ple_gate[i],
                           w_ple_proj[i])

    w_in = w_in_ab[0]
    n_small = 3 * N_HEADS
    ff0 = 3 * WIDTH
    gq0 = ff0 + N_HEADS
    ga0 = gq0 + 4 * WIDTH
    w_small = jnp.concatenate([w_in[:, ff0:gq0], w_in[:, ga0:ga0 + 2 * N_HEADS],
                               jnp.zeros((d, LANES - n_small), F32)], axis=1)
    w_all = jnp.concatenate([w_in[:, :ff0], w_in[:, gq0:ga0], w_small], axis=1)
    bf_row = _lane_row(b_fgate[0], LANE_F)
    up = _proj(x_prompt.reshape(n_p, d), w_all.astype(BF16), bf_row, U_COLS, False)
    us = _proj(x_sample.reshape(n_s, d), w_all, bf_row, LANES, True)

    u3 = up[None]
    us3 = us.reshape(n_sb, dseq, U_COLS)
    cq_p, ck_p = _cumsum(u3, 1, n_p, TK, COL_S // LANES)
    lf_s = jnp.concatenate(
        [jnp.pad(cache_fox_logf[0].astype(F32), ((0, 0), (0, 0), (0, LANES - N_HEADS))),
         us3[:, :, COL_S:]], axis=1)
    cq_s, ck_s = _cumsum(lf_s, n_sb, past + dseq, past + dseq, 0)

    of_p = _fox(u3, lambda b, i: (0, i, COL_Q // WIDTH), u3, lambda b, j: (0, j, COL_K // WIDTH),
                u3, lambda b, j: (0, j, COL_V // WIDTH), cq_p, lambda b, i: (0, i, 0),
                ck_p, lambda b, j: (0, 0, j), 1, n_p // TQ, TQ, TK, 0)
    k_all = jnp.concatenate([cache_fox_k[0].reshape(n_sb, past, WIDTH), us3[:, :, COL_K:COL_K + WIDTH]], axis=1)
    v_all = jnp.concatenate([cache_fox_v[0].reshape(n_sb, past, WIDTH), us3[:, :, COL_V:COL_V + WIDTH]], axis=1)
    of_s = _fox(us3, lambda b, i: (b, 0, COL_Q // WIDTH), k_all, lambda b, j: (b, 0, 0),
                v_all, lambda b, j: (b, 0, 0), cq_s, lambda b, i: (b, past // dseq, 0),
                ck_s, lambda b, j: (b, 0, 0), n_sb, 1, dseq, past + dseq, past, hi=True)

    gw = 3 * WIDTH
    conv_w = jnp.pad(gdn_conv_w[0], ((0, SUBLANES - CONV_WIDTH), (0, 0)))
    gdn_args = (conv_w, _lane_row(gdn_a_log[0], LANE_A), _lane_row(gdn_dt_bias[0], LANE_A),
                gdn_norm_g[0][None])
    n_pstep = n_p // GDN_ROWS
    ends = lambda k: (jnp.asarray((np.arange(k) == 0).astype(np.int32)),
                      jnp.asarray((np.arange(k) == k - 1).astype(np.int32)))
    og_p, st_p = _gdn(up.reshape(n_pstep, GDN_ROWS, U_COLS), False, *ends(n_pstep),
                      jnp.zeros((n_pstep,), I32), jnp.zeros((1, SUBLANES, gw), F32), gdn_args[0],
                      jnp.zeros((1, N_HEADS, HEAD_DIM, HEAD_DIM), F32), *gdn_args[1:])
    conv_past = jnp.pad(state_gdn_conv[0].astype(F32), ((0, 0), (SUBLANES - (CONV_WIDTH - 1), 0), (0, 0)))
    ones = jnp.ones((n_sb,), I32)
    og_s, st_s = _gdn(us3, True, ones, ones, jnp.arange(n_sb, dtype=I32), conv_past, gdn_args[0],
                      state_gdn[0].astype(F32), *gdn_args[1:])
    h, ht = _outproj_ln((x_prompt.reshape(n_p, d), of_p.reshape(n_p, WIDTH), og_p.reshape(n_p, WIDTH)),
                        (x_sample.reshape(n_s, d), of_s.reshape(n_s, WIDTH), og_s.reshape(n_s, WIDTH)),
                        w_out_ab[0], ln1_g[0][None], ln1_b[0][None])
    x1 = tail(h, ht, 0, False)

    pool_args = (pool_w[0].astype(BF16), pool_scale[0][None], w_out_pool[0].astype(BF16),
                 ln1_g[1][None], ln1_b[1][None])
    ratio = TM_TOK // POOL_HALO
    x1_halo = x1.reshape(n // POOL_HALO, POOL_HALO, d)
    h_pool = _pool_ln(x1, lambda i: (i, 0), x1_halo, lambda i: (jnp.maximum(i * ratio - 1, 0), 0, 0),
                      n_p // TM_TOK, TM_TOK, 0, True, *pool_args)
    cache16 = jnp.pad(cache_pool[0].astype(F32), ((0, 0), (POOL_HALO - POOL_STATE, 0), (0, 0)))
    h, ht = _pool_ln(x1, lambda i: (n_p // dseq + i, 0), cache16, lambda i: (i, 0, 0),
                     n_sb, dseq, past, False, *pool_args, into=h_pool)
    x2_p, x2_s = tail(h, ht, 1, True)

    return (x2_p.reshape(n_pb, seq, d), x2_s.reshape(n_sb, dseq, d),
            up[:, COL_K:COL_K + WIDTH].reshape(1, n_pb, seq, N_HEADS, HEAD_DIM),
            up[:, COL_V:COL_V + WIDTH].reshape(1, n_pb, seq, N_HEADS, HEAD_DIM),
            up[:, COL_S:COL_S + N_HEADS].reshape(1, n_pb, seq, N_HEADS),
            st_p.reshape(1, n_pb, N_HEADS, HEAD_DIM, HEAD_DIM),
            up[seq - (CONV_WIDTH - 1):, COL_G:COL_G + gw].reshape(1, n_pb, CONV_WIDTH - 1, gw),
            x1[n_p - POOL_STATE:n_p].reshape(1, n_pb, POOL_STATE, d),
            us[:, COL_K:COL_K + WIDTH].reshape(1, n_sb, dseq, N_HEADS, HEAD_DIM),
            us[:, COL_V:COL_V + WIDTH].reshape(1, n_sb, dseq, N_HEADS, HEAD_DIM),
            us[:, COL_S:COL_S + N_HEADS].reshape(1, n_sb, dseq, N_HEADS),
            st_s.reshape(1, n_sb, N_HEADS, HEAD_DIM, HEAD_DIM),
            us[:, COL_G:COL_G + gw].reshape(n_sb, dseq, gw)[:, dseq - (CONV_WIDTH - 1):].reshape(
                1, n_sb, CONV_WIDTH - 1, gw),
            x1[n_p:].reshape(n_sb, dseq, d)[:, dseq - POOL_STATE:].reshape(1, n_sb, POOL_STATE, d))
```

```python
import functools

import numpy as np
import jax
import jax.numpy as jnp
from jax import lax
from jax.experimental import pallas as pl
from jax.experimental.pallas import tpu as pltpu
from jax.experimental.pallas import tpu_sc as plsc

F32 = jnp.float32
BF16 = jnp.bfloat16
I32 = jnp.int32
HIGHEST = lax.Precision.HIGHEST

LANES = 128
SUBLANES = 8
VMEM_LIMIT = 56 * 1024 * 1024

HEAD_DIM = 128
N_HEADS = 4
WIDTH = N_HEADS * HEAD_DIM
CHUNK = 64
CONV_WIDTH = 4
POOL_WINDOWS = (2, 4, 8, 16)
POOL_HALO = 16
POOL_STATE = 15
N_EXPERTS = 32
TOP_K = 4
SWIGLU_LIMIT = 7.0
SWIGLU_ALPHA = 1.702
DEPTH = 2
DN_ALPHA = (2 * DEPTH) ** 0.25
LN_EPS = 1e-5
NORM_EPS = 1e-6
NEG_INF = -1e30
LOG2E = 1.4426950408889634

COL_Q, COL_K, COL_V = 0, WIDTH, 2 * WIDTH
COL_G = 3 * WIDTH
COL_Z = 6 * WIDTH
COL_S = 7 * WIDTH
U_COLS = COL_S + LANES
LANE_F, LANE_A, LANE_B = 0, N_HEADS, 2 * N_HEADS

TM_PROJ = 256
TM_TOK = 512
SC_WINDOW = 32
MOE_BLOCK = 512
TQ = 1024
TK = 512
GDN_ROWS = 256
GDN_CHUNK = CHUNK


def _cparams(sem):
    return pltpu.CompilerParams(dimension_semantics=sem, vmem_limit_bytes=VMEM_LIMIT)


def _softplus(x):
    return jnp.maximum(x, 0.0) + jnp.log1p(jnp.exp(-jnp.abs(x)))


def _sigmoid(x):
    return 1.0 / (1.0 + jnp.exp(-x))


def _silu(x):
    return x * _sigmoid(x)


def _layer_norm(y, g, b):
    mu = jnp.mean(y, axis=-1, keepdims=True)
    yc = y - mu
    var = jnp.mean(yc * yc, axis=-1, keepdims=True)
    return yc * lax.rsqrt(var + LN_EPS) * g + b


def _dot_general(a, b, dims, hi):
    if hi:
        return lax.dot_general(a.astype(F32), b.astype(F32), (dims, ((), ())), precision=HIGHEST,
                               preferred_element_type=F32)
    return lax.dot_general(a.astype(BF16), b.astype(BF16), (dims, ((), ())), preferred_element_type=F32)


def _dot(a, b, hi=False):
    return _dot_general(a, b, ((1,), (0,)), hi)


def _dot_nt(a, b, hi=False):
    return _dot_general(a, b, ((1,), (1,)), hi)


def _dot_tn(a, b, hi=False):
    return _dot_general(a, b, ((0,), (0,)), hi)


def _spread_lanes(x, width):
    if width % LANES == 0:
        return jnp.concatenate([x] * (width // LANES), axis=1)
    return jnp.broadcast_to(x[:, 0:1], (x.shape[0], width))


def _load_token_tiles(ref, lead, n_tok):
    return jnp.concatenate([ref[(*lead, pl.ds(j, n_tok, stride=SUBLANES), slice(None))]
                            for j in range(SUBLANES)], axis=1)


def _store_token_tiles(ref, lead, x):
    for j in range(SUBLANES):
        ref[(*lead, pl.ds(j, x.shape[0], stride=SUBLANES), slice(None))] = x[:, j * LANES:(j + 1) * LANES]


def _lanes_to_rows(x, lane0):
    r = lax.broadcasted_iota(I32, (SUBLANES, LANES), 0)
    c = lax.broadcasted_iota(I32, (SUBLANES, LANES), 1)
    sel = (c == r + lane0).astype(F32)
    return lax.dot_general(sel, x, (((1,), (1,)), ((), ())), precision=HIGHEST,
                           preferred_element_type=F32)


def _proj_kernel(x_ref, w_ref, bf_ref, u_ref, *, hi):
    u = _dot(x_ref[...], w_ref[...], hi)
    u_ref[...] = u

    @pl.when(pl.program_id(1) == pl.num_programs(1) - 1)
    def _():
        small = u[:, u.shape[1] - LANES:]
        lane = lax.broadcasted_iota(I32, small.shape, 1)
        logf = -_softplus(-(small + bf_ref[...]))
        u_ref[:, u.shape[1] - LANES:] = jnp.where(lane < LANE_A, logf, small)


def _proj(x, w, bf_row, tn, hi):
    n, d = x.shape
    m = w.shape[1]
    return pl.pallas_call(
        functools.partial(_proj_kernel, hi=hi),
        grid=(n // TM_PROJ, m // tn),
        in_specs=[pl.BlockSpec((TM_PROJ, d), lambda i, j: (i, 0)),
                  pl.BlockSpec((d, tn), lambda i, j: (0, j)),
                  pl.BlockSpec((1, LANES), lambda i, j: (0, 0))],
        out_specs=pl.BlockSpec((TM_PROJ, tn), lambda i, j: (i, j)),
        out_shape=jax.ShapeDtypeStruct((n, m), F32),
        compiler_params=_cparams(("parallel", "parallel")),
        name="in_proj",
    )(x, w, bf_row)


def _cumsum_kernel(lf_ref, crep_ref, crow_ref, carry_ref):
    @pl.when(pl.program_id(1) == 0)
    def _():
        carry_ref[...] = jnp.zeros_like(carry_ref)

    lf = lf_ref[0]
    t = lf.shape[0]
    r = lax.broadcasted_iota(I32, (t, t), 0)
    c = lax.broadcasted_iota(I32, (t, t), 1)
    tril = (c <= r).astype(F32)
    cs = jnp.dot(tril, lf, precision=HIGHEST, preferred_element_type=F32) + carry_ref[0:1, :]
    carry_ref[...] = jnp.broadcast_to(cs[t - 1:t, :], carry_ref.shape)
    c2 = cs * LOG2E
    crow_ref[0] = _lanes_to_rows(c2, LANE_F)
    for h in range(N_HEADS):
        crep_ref[0, :, h * HEAD_DIM:(h + 1) * HEAD_DIM] = jnp.broadcast_to(
            c2[:, LANE_F + h:LANE_F + h + 1], (t, HEAD_DIM))


def _cumsum(arr, n_batch, length, tl, col_block):
    return pl.pallas_call(
        _cumsum_kernel,
        grid=(n_batch, length // tl),
        in_specs=[pl.BlockSpec((1, tl, LANES), lambda b, j: (b, j, col_block))],
        out_specs=[pl.BlockSpec((1, tl, WIDTH), lambda b, j: (b, j, 0)),
                   pl.BlockSpec((1, SUBLANES, tl), lambda b, j: (b, 0, j))],
        out_shape=[jax.ShapeDtypeStruct((n_batch, length, WIDTH), F32),
                   jax.ShapeDtypeStruct((n_batch, SUBLANES, length), F32)],
        scratch_shapes=[pltpu.VMEM((SUBLANES, LANES), F32)],
        compiler_params=_cparams(("parallel", "arbitrary")),
        name="logf_cumsum",
    )(arr)


def _fox_kernel(qi_ref, kj_ref, last_ref, q_ref, k_ref, v_ref, cq_ref, ck_ref, o_ref,
                m_ref, l_ref, acc_ref, *, tq, tk, past, hi):
    s_idx = pl.program_id(1)
    qi = qi_ref[s_idx]
    kj = kj_ref[s_idx]

    @pl.when(kj == 0)
    def _():
        m_ref[...] = jnp.full_like(m_ref, NEG_INF)
        l_ref[...] = jnp.zeros_like(l_ref)
        acc_ref[...] = jnp.zeros_like(acc_ref)

    def update(masked):
        if masked:
            q_pos = past + qi * tq + lax.broadcasted_iota(I32, (tq, tk), 0)
            k_pos = kj * tk + lax.broadcasted_iota(I32, (tq, tk), 1)
            visible = k_pos <= q_pos
        for h in range(N_HEADS):
            cols = slice(h * HEAD_DIM, (h + 1) * HEAD_DIM)
            q = q_ref[0, :, cols] * (HEAD_DIM ** -0.5 * LOG2E)
            t = _dot_nt(q, k_ref[0, :, cols], hi) - ck_ref[0, h:h + 1, :]
            if masked:
                t = jnp.where(visible, t, NEG_INF)
            cq = cq_ref[0, :, cols]
            m_prev = m_ref[h]
            m_new = jnp.maximum(m_prev, jnp.max(t, axis=-1, keepdims=True) + cq)
            p = jnp.exp2(t - _spread_lanes(m_new - cq, tk))
            alpha = jnp.exp2(m_prev - m_new)
            l_ref[h] = alpha * l_ref[h] + jnp.sum(p, axis=-1, keepdims=True)
            acc_ref[:, cols] = alpha * acc_ref[:, cols] + _dot(p, v_ref[0, :, cols], hi)
            m_ref[h] = m_new

    crosses_diagonal = kj * tk + (tk - 1) > past + qi * tq
    pl.when(crosses_diagonal)(functools.partial(update, True))
    pl.when(jnp.logical_not(crosses_diagonal))(functools.partial(update, False))

    @pl.when(last_ref[s_idx] == 1)
    def _():
        for h in range(N_HEADS):
            cols = slice(h * HEAD_DIM, (h + 1) * HEAD_DIM)
            o_ref[0, :, cols] = acc_ref[:, cols] / l_ref[h]


def _fox_schedule(n_q, tq, tk, past):
    qi, kj, last = [], [], []
    for i in range(n_q):
        hi = (past + (i + 1) * tq - 1) // tk
        for j in range(hi + 1):
            qi.append(i)
            kj.append(j)
            last.append(1 if j == hi else 0)
    return (jnp.asarray(np.array(qi, np.int32)), jnp.asarray(np.array(kj, np.int32)),
            jnp.asarray(np.array(last, np.int32)))


def _fox(q_arr, q_map, k_arr, k_map, v_arr, v_map, cq_arr, cq_map, ck_arr, ck_map,
         n_batch, n_q, tq, tk, past, hi=False):
    qi, kj, last = _fox_schedule(n_q, tq, tk, past)
    n_steps = int(qi.shape[0])
    spec = lambda shape, fn, tab: pl.BlockSpec(shape, lambda b, s, qi_r, kj_r, la_r: fn(b, (qi_r if tab == 'q' else kj_r)[s]))
    return pl.pallas_call(
        functools.partial(_fox_kernel, tq=tq, tk=tk, past=past, hi=hi),
        grid_spec=pltpu.PrefetchScalarGridSpec(
            num_scalar_prefetch=3,
            grid=(n_batch, n_steps),
            in_specs=[spec((1, tq, WIDTH), q_map, 'q'),
                      spec((1, tk, WIDTH), k_map, 'k'),
                      spec((1, tk, WIDTH), v_map, 'k'),
                      spec((1, tq, WIDTH), cq_map, 'q'),
                      spec((1, SUBLANES, tk), ck_map, 'k')],
            out_specs=spec((1, tq, WIDTH), lambda b, i: (b, i, 0), 'q'),
            scratch_shapes=[pltpu.VMEM((N_HEADS, tq, HEAD_DIM), F32),
                            pltpu.VMEM((N_HEADS, tq, HEAD_DIM), F32),
                            pltpu.VMEM((tq, WIDTH), F32)]),
        out_shape=jax.ShapeDtypeStruct((n_batch, n_q * tq, WIDTH), F32),
        compiler_params=_cparams(("parallel", "arbitrary")),
        name="fox_attention",
    )(qi, kj, last, q_arr, k_arr, v_arr, cq_arr, ck_arr)


def _gdn_kernel(first_ref, last_ref, seq_ref,
                pre_ref, z_ref, sm_ref, cpast_ref, convw_ref, s0_ref, alog_ref, dtb_ref, ng_ref,
                o_ref, sout_ref, stage_ref, s_ref, *, rows, chunk, hi):
    dot, dot_nt, dot_tn = (functools.partial(f, hi=hi) for f in (_dot, _dot_nt, _dot_tn))
    step = pl.program_id(0)
    halo = SUBLANES
    n_chunks = rows // chunk

    @pl.when(first_ref[step] == 1)
    def _():
        stage_ref[0:halo, :] = cpast_ref[0]
        s_ref[...] = s0_ref[0]

    stage_ref[halo:halo + rows, :] = pre_ref[0]
    conv = stage_ref[halo:halo + rows, :] * convw_ref[CONV_WIDTH - 1:CONV_WIDTH, :]
    for j in range(1, CONV_WIDTH):
        conv = conv + (stage_ref[halo - j:halo - j + rows, :]
                       * convw_ref[CONV_WIDTH - 1 - j:CONV_WIDTH - j, :])
    stage_ref[0:halo, :] = stage_ref[rows:rows + halo, :]
    act = _silu(conv)

    small = sm_ref[0]
    beta_all = _sigmoid(small)
    g_all = -jnp.exp(alog_ref[...]) * _softplus(small + dtb_ref[...])
    r = lax.broadcasted_iota(I32, (rows, rows), 0)
    c = lax.broadcasted_iota(I32, (rows, rows), 1)
    same_chunk = (r // chunk) == (c // chunk)
    incl = same_chunk & (c <= r)
    strict = same_chunk & (c < r)
    eye = (c == r).astype(F32)
    gc_all = jnp.dot(incl.astype(F32), g_all, precision=HIGHEST, preferred_element_type=F32)
    gc_rows = _lanes_to_rows(gc_all, LANE_A)

    heads = range(N_HEADS)
    head_cols = [slice(h * HEAD_DIM, (h + 1) * HEAD_DIM) for h in heads]
    q, k, gc, decay, kb, vb, low = [], [], [], [], [], [], []
    for h in heads:
        qh = act[:, h * HEAD_DIM:(h + 1) * HEAD_DIM]
        kh = act[:, WIDTH + h * HEAD_DIM:WIDTH + (h + 1) * HEAD_DIM]
        vh = act[:, 2 * WIDTH + h * HEAD_DIM:2 * WIDTH + (h + 1) * HEAD_DIM]
        q.append(qh * lax.rsqrt(jnp.sum(qh * qh, axis=-1, keepdims=True) + NORM_EPS) * (HEAD_DIM ** -0.5))
        k.append(kh * lax.rsqrt(jnp.sum(kh * kh, axis=-1, keepdims=True) + NORM_EPS))
        beta = beta_all[:, LANE_B + h:LANE_B + h + 1]
        gc.append(gc_all[:, LANE_A + h:LANE_A + h + 1])
        diff = gc[h] - gc_rows[h:h + 1, :]
        decay.append(jnp.where(incl, jnp.exp(jnp.where(incl, diff, 0.0)), 0.0))
        kb.append(k[h] * beta)
        vb.append(vh * beta)
    for h in heads:
        low.append(jnp.where(strict, dot_nt(kb[h], k[h]) * decay[h], 0.0))
    inv = [eye - low[h] for h in heads]
    pw = [dot(low[h], low[h]) for h in heads]
    n_sq = chunk.bit_length() - 2
    for it in range(n_sq):
        inv = [inv[h] + dot(inv[h], pw[h]) for h in heads]
        if it + 1 < n_sq:
            pw = [dot(pw[h], pw[h]) for h in heads]
    egc = [jnp.exp(gc[h]) for h in heads]
    uw = [dot(inv[h], jnp.concatenate([vb[h], kb[h] * egc[h]], axis=1)) for h in heads]
    intra = [jnp.where(incl, dot_nt(q[h], k[h]) * decay[h], 0.0) for h in heads]
    qd = [q[h] * egc[h] for h in heads]
    g_last = [[gc[h][(g + 1) * chunk - 1:(g + 1) * chunk, :] for g in range(n_chunks)] for h in heads]
    kd = [k[h] * jnp.exp(jnp.concatenate([jnp.broadcast_to(gl, (chunk, 1)) for gl in g_last[h]], axis=0)
                         - gc[h]) for h in heads]
    state = [s_ref[h] for h in heads]
    v_new = [[] for _ in heads]
    for g in range(n_chunks):
        rs = slice(g * chunk, (g + 1) * chunk)
        for h in heads:
            v_new[h].append(uw[h][rs, :HEAD_DIM] - dot(uw[h][rs, HEAD_DIM:], state[h]))
        for h in heads:
            v_rows = jnp.concatenate(
                v_new[h] + [jnp.zeros((rows - (g + 1) * chunk, HEAD_DIM), F32)] * (g + 1 < n_chunks), axis=0)
            o = dot(qd[h][rs, :], state[h]) + dot(intra[h][rs, :], v_rows)
            state[h] = state[h] * jnp.exp(g_last[h][g]) + dot_tn(kd[h][rs, :], v_new[h][g])
            o = (o * lax.rsqrt(jnp.mean(o * o, axis=-1, keepdims=True) + NORM_EPS)
                 * ng_ref[...] * _silu(z_ref[0, rs, head_cols[h]]))
            o_ref[0, rs, head_cols[h]] = o
    for h in heads:
        s_ref[h] = state[h]

    @pl.when(last_ref[step] == 1)
    def _():
        sout_ref[0] = s_ref[...]


def _gdn(u_view, hi, first, last, seq, conv_past, conv_w, s0, alog_row, dtb_row, ng_row):
    n_steps, rows, _ = u_view.shape
    n_seq = s0.shape[0]
    gw = 3 * WIDTH
    return pl.pallas_call(
        functools.partial(_gdn_kernel, rows=rows, chunk=min(rows, GDN_CHUNK), hi=hi),
        grid_spec=pltpu.PrefetchScalarGridSpec(
            num_scalar_prefetch=3,
            grid=(n_steps,),
            in_specs=[pl.BlockSpec((1, rows, gw), lambda s, f, l, q: (s, 0, COL_G // gw)),
                      pl.BlockSpec((1, rows, WIDTH), lambda s, f, l, q: (s, 0, COL_Z // WIDTH)),
                      pl.BlockSpec((1, rows, LANES), lambda s, f, l, q: (s, 0, COL_S // LANES)),
                      pl.BlockSpec((1, SUBLANES, gw), lambda s, f, l, q: (q[s], 0, 0)),
                      pl.BlockSpec((SUBLANES, gw), lambda s, f, l, q: (0, 0)),
                      pl.BlockSpec((1, N_HEADS, HEAD_DIM, HEAD_DIM), lambda s, f, l, q: (q[s], 0, 0, 0)),
                      pl.BlockSpec((1, LANES), lambda s, f, l, q: (0, 0)),
                      pl.BlockSpec((1, LANES), lambda s, f, l, q: (0, 0)),
                      pl.BlockSpec((1, LANES), lambda s, f, l, q: (0, 0))],
            out_specs=[pl.BlockSpec((1, rows, WIDTH), lambda s, f, l, q: (s, 0, 0)),
                       pl.BlockSpec((1, N_HEADS, HEAD_DIM, HEAD_DIM), lambda s, f, l, q: (q[s], 0, 0, 0))],
            scratch_shapes=[pltpu.VMEM((rows + SUBLANES, gw), F32),
                            pltpu.VMEM((N_HEADS, HEAD_DIM, HEAD_DIM), F32)]),
        out_shape=[jax.ShapeDtypeStruct((n_steps, rows, WIDTH), F32),
                   jax.ShapeDtypeStruct((n_seq, N_HEADS, HEAD_DIM, HEAD_DIM), F32)],
        compiler_params=_cparams(("arbitrary",)),
        name="gated_deltanet",
    )(first, last, seq, u_view, u_view, u_view, conv_past, conv_w, s0, alog_row, dtb_row, ng_row)


def _outproj_ln_kernel(xp_ref, ofp_ref, ogp_ref, xs_ref, ofs_ref, ogs_ref, w_ref, wf_ref, g_ref, b_ref,
                       h_ref, ht_ref, *, hi_from):
    def run(hi, x_ref, of_ref, og_ref):
        w = wf_ref if hi else w_ref
        mix = _dot(of_ref[...], w[0:WIDTH, :], hi) + _dot(og_ref[...], w[WIDTH:2 * WIDTH, :], hi)
        h = _layer_norm(DN_ALPHA * x_ref[...] + mix, g_ref[...], b_ref[...])
        h_ref[...] = h
        _store_token_tiles(ht_ref, (), h)

    pl.when(pl.program_id(0) < hi_from)(functools.partial(run, False, xp_ref, ofp_ref, ogp_ref))
    pl.when(pl.program_id(0) >= hi_from)(functools.partial(run, True, xs_ref, ofs_ref, ogs_ref))


def _outproj_ln(prompt, sample, w, g_row, b_row):
    n_p, d = prompt[0].shape
    n = n_p + sample[0].shape[0]
    hi_from = n_p // TM_TOK
    row = lambda i: (i, 0)
    row_p = lambda i: (jnp.minimum(i, hi_from - 1), 0)
    row_s = lambda i: (jnp.maximum(i - hi_from, 0), 0)
    fixed = lambda i: (0, 0)
    group = lambda rows: [pl.BlockSpec((TM_TOK, d), rows), pl.BlockSpec((TM_TOK, WIDTH), rows),
                          pl.BlockSpec((TM_TOK, WIDTH), rows)]
    return pl.pallas_call(
        functools.partial(_outproj_ln_kernel, hi_from=hi_from),
        grid=(n // TM_TOK,),
        in_specs=group(row_p) + group(row_s) + [pl.BlockSpec((2 * WIDTH, d), fixed),
                                                pl.BlockSpec((2 * WIDTH, d), fixed),
                                                pl.BlockSpec((1, d), fixed), pl.BlockSpec((1, d), fixed)],
        out_specs=[pl.BlockSpec((TM_TOK, d), row), pl.BlockSpec((TM_TOK * SUBLANES, LANES), row)],
        out_shape=[jax.ShapeDtypeStruct((n, d), F32), jax.ShapeDtypeStruct((n * SUBLANES, LANES), F32)],
        compiler_params=_cparams(("parallel",)),
        name="out_proj_ln",
    )(*prompt, *sample, w.astype(BF16), w, g_row, b_row)


def _pool_ln_kernel(x_ref, halo_ref, pw_ref, ps_ref, w_ref, g_ref, b_ref, h_ref, ht_ref, stage_ref,
                    *, tm, pos0, zero_first_halo):
    i = pl.program_id(0)
    stage_ref[0:POOL_HALO, :] = halo_ref[0]
    if zero_first_halo:
        @pl.when(i == 0)
        def _():
            stage_ref[0:POOL_HALO, :] = jnp.zeros((POOL_HALO, stage_ref.shape[1]), F32)
    x = x_ref[...]
    stage_ref[POOL_HALO:POOL_HALO + tm, :] = x
    gdim = x.shape[1] // len(POOL_WINDOWS)
    pos = pos0 + lax.broadcasted_iota(I32, (tm, 1), 0)
    if zero_first_halo:
        pos = pos + i * tm
    parts = []
    for gi, win in enumerate(POOL_WINDOWS):
        cols = slice(gi * gdim, (gi + 1) * gdim)
        s = stage_ref[POOL_HALO:POOL_HALO + tm, cols]
        for j in range(1, win):
            s = s + stage_ref[POOL_HALO - j:POOL_HALO - j + tm, cols]
        cnt = jnp.minimum(pos + 1, win).astype(F32)
        zg = s / cnt - x[:, cols]
        parts.append(_dot(zg, pw_ref[gi]))
    zg = jnp.concatenate(parts, axis=-1) * ps_ref[...]
    mix = _dot(zg, w_ref[...])
    h = _layer_norm(DN_ALPHA * x + mix, g_ref[...], b_ref[...])
    h_ref[...] = h
    _store_token_tiles(ht_ref, (), h)


def _pool_ln_into_kernel(*refs, **kw):
    _pool_ln_kernel(*refs[:7], *refs[9:], **kw)


def _pool_ln(x, x_map, halo_arr, halo_map, n_tiles, tm, pos0, zero_first_halo,
             pw_bf16, ps_row, w_bf16, g_row, b_row, into=None):
    n, d = x.shape
    gdim = d // len(POOL_WINDOWS)
    fixed = lambda i: (0, 0)
    in_specs = [pl.BlockSpec((tm, d), x_map),
                pl.BlockSpec((1, POOL_HALO, d), halo_map),
                pl.BlockSpec((len(POOL_WINDOWS), gdim, gdim), lambda i: (0, 0, 0)),
                pl.BlockSpec((1, d), fixed), pl.BlockSpec((d, d), fixed),
                pl.BlockSpec((1, d), fixed), pl.BlockSpec((1, d), fixed)]
    args = (x, halo_arr, pw_bf16, ps_row, w_bf16, g_row, b_row)
    kw = dict(tm=tm, pos0=pos0, zero_first_halo=zero_first_halo)
    return pl.pallas_call(
        functools.partial(_pool_ln_kernel if into is None else _pool_ln_into_kernel, **kw),
        grid=(n_tiles,),
        in_specs=in_specs + ([] if into is None else [pl.BlockSpec(memory_space=pl.ANY)] * 2),
        out_specs=[pl.BlockSpec((tm, d), x_map),
                   pl.BlockSpec((tm * SUBLANES, LANES), x_map)],
        out_shape=[jax.ShapeDtypeStruct((n, d), F32), jax.ShapeDtypeStruct((n * SUBLANES, LANES), F32)],
        input_output_aliases={} if into is None else {7: 0, 8: 1},
        scratch_shapes=[pltpu.VMEM((POOL_HALO + tm, d), F32)],
        compiler_params=_cparams(("arbitrary",)),
        name="pool_mixer_ln",
    )(*args, *(() if into is None else into))


def _router_kernel(h_ref, wr_ref, br_ref, idx_ref, gate_ref, rank_ref, pstart_ref, tab_ref, carry_ref):
    @pl.when(pl.program_id(0) == 0)
    def _():
        carry_ref[...] = jnp.zeros_like(carry_ref)

    tm = h_ref.shape[0]
    lane = lax.broadcasted_iota(I32, (tm, LANES), 1).astype(F32)
    logits = jnp.dot(h_ref[...], wr_ref[...], precision=HIGHEST, preferred_element_type=F32)
    work = jnp.where(lane < N_EXPERTS, logits + br_ref[...], -jnp.inf)
    vals, ids = [], []
    for _ in range(TOP_K):
        m = jnp.max(work, axis=-1, keepdims=True)
        ik = jnp.min(jnp.where(work == m, lane, float(LANES)), axis=-1, keepdims=True)
        vals.append(m)
        ids.append(ik)
        work = jnp.where(lane == ik, -jnp.inf, work)
    exps = [jnp.exp(v - vals[0]) for v in vals]
    denom = exps[0]
    for e in exps[1:]:
        denom = denom + e
    multihot = jnp.zeros((tm, LANES), F32)
    idx_out = jnp.zeros((tm, LANES), F32)
    gate_out = jnp.zeros((tm, LANES), F32)
    for k in range(TOP_K):
        multihot = multihot + (lane == ids[k]).astype(F32)
        idx_out = jnp.where(lane == k, ids[k], idx_out)
        gate_out = jnp.where(lane == k, exps[k] / denom, gate_out)
    r = lax.broadcasted_iota(I32, (tm, tm), 0)
    c = lax.broadcasted_iota(I32, (tm, tm), 1)
    before = _dot((c < r).astype(F32), multihot) + carry_ref[0:1, :]
    rank_out = jnp.zeros((tm, LANES), F32)
    for k in range(TOP_K):
        rk = jnp.sum(jnp.where(lane == ids[k], before, 0.0), axis=-1, keepdims=True)
        rank_out = jnp.where(lane == k, rk, rank_out)
    idx_ref[...] = jnp.transpose(idx_out)[:SUBLANES].astype(I32)
    rank_ref[...] = jnp.transpose(rank_out)[:SUBLANES].astype(I32)
    gate_ref[...] = gate_out
    total = carry_ref[0:1, :] + jnp.sum(multihot, axis=0, keepdims=True)
    carry_ref[...] = jnp.broadcast_to(total, carry_ref.shape)

    @pl.when(pl.program_id(0) == pl.num_programs(0) - 1)
    def _():
        n_rows = tab_ref.shape[0]
        padded = jnp.floor((total + (MOE_BLOCK - 1)) / MOE_BLOCK) * MOE_BLOCK
        rr = lax.broadcasted_iota(I32, (LANES, LANES), 0)
        cc = lax.broadcasted_iota(I32, (LANES, LANES), 1)
        pend = jnp.dot(jnp.broadcast_to(padded, (SUBLANES, LANES)), (rr <= cc).astype(F32),
                       precision=HIGHEST, preferred_element_type=F32)[0:1, :]
        pstart = pend - padded
        n_used = jnp.max(pend, axis=-1, keepdims=True) / MOE_BLOCK
        elane = lax.broadcasted_iota(I32, (n_rows, LANES), 1)
        blk = jnp.minimum(lax.broadcasted_iota(I32, (n_rows, 1), 0).astype(F32), n_used - 1.0) * MOE_BLOCK
        ends_before = jnp.where((elane < N_EXPERTS) & (pend <= blk), 1.0, 0.0)
        block_e = jnp.minimum(jnp.sum(ends_before, axis=-1, keepdims=True), N_EXPERTS - 1.0)
        mine = elane.astype(F32) == block_e
        filled = jnp.sum(jnp.where(mine, total - (blk - pstart), 0.0), axis=-1, keepdims=True)
        block_valid = jnp.clip(filled, 0.0, float(MOE_BLOCK))
        tab = jnp.where(elane == 0, block_e,
                        jnp.where(elane == 1, block_valid, jnp.where(elane == 2, n_used, 0.0)))
        tab_ref[...] = tab.astype(I32)
        pstart_ref[...] = jnp.broadcast_to(pstart, pstart_ref.shape).astype(I32)


def _router(h, wr_pad, br_row, n_blocks):
    n, d = h.shape
    row = lambda i: (i, 0)
    col = lambda i: (0, i)
    fixed = lambda i: (0, 0)
    tab_rows = -(-n_blocks // SUBLANES) * SUBLANES
    return pl.pallas_call(
        _router_kernel,
        grid=(n // TM_TOK,),
        in_specs=[pl.BlockSpec((TM_TOK, d), row), pl.BlockSpec((d, LANES), fixed),
                  pl.BlockSpec((1, LANES), fixed)],
        out_specs=[pl.BlockSpec((SUBLANES, TM_TOK), col), pl.BlockSpec((TM_TOK, LANES), row),
                   pl.BlockSpec((SUBLANES, TM_TOK), col), pl.BlockSpec((SUBLANES, LANES), fixed),
                   pl.BlockSpec((tab_rows, LANES), fixed)],
        out_shape=[jax.ShapeDtypeStruct((SUBLANES, n), I32), jax.ShapeDtypeStruct((n, LANES), F32),
                   jax.ShapeDtypeStruct((SUBLANES, n), I32), jax.ShapeDtypeStruct((SUBLANES, LANES), I32),
                   jax.ShapeDtypeStruct((tab_rows, LANES), I32)],
        scratch_shapes=[pltpu.VMEM((SUBLANES, LANES), F32)],
        compiler_params=_cparams(("arbitrary",)),
        name="moe_router",
    )(h, wr_pad, br_row)


def _expert_kernel(be_ref, nu_ref, valid_ref, xb_ref, wup_ref, bup_ref, wdn_ref, bdn_ref, yb_ref,
                   wup_bf_ref, wdn_bf_ref):
    b = pl.program_id(0)

    @pl.when(b < nu_ref[0])
    def _():
        @pl.when((b == 0) | (be_ref[b] != be_ref[jnp.maximum(b - 1, 0)]))
        def _():
            wup_bf_ref[...] = wup_ref[0, 0].astype(BF16)
            wdn_bf_ref[...] = wdn_ref[0, 0].astype(BF16)

        d_exp = wdn_ref.shape[2]
        x = _load_token_tiles(xb_ref, (), MOE_BLOCK)
        row = lax.broadcasted_iota(I32, (MOE_BLOCK, 1), 0)
        x = jnp.where(row < valid_ref[b], x, 0.0)
        hu = jnp.dot(x.astype(BF16), wup_bf_ref[...], preferred_element_type=F32) + bup_ref[0, 0]
        glu = jnp.minimum(hu[:, :d_exp], SWIGLU_LIMIT)
        lin = jnp.clip(hu[:, d_exp:], -SWIGLU_LIMIT, SWIGLU_LIMIT)
        a = glu * _sigmoid(SWIGLU_ALPHA * glu) * (lin + 1.0)
        y = jnp.dot(a.astype(BF16), wdn_bf_ref[...], preferred_element_type=F32) + bdn_ref[0, 0]
        _store_token_tiles(yb_ref, (), y)

    @pl.when(b >= nu_ref[0])
    def _():
        yb_ref[...] = jnp.zeros_like(yb_ref)


def _experts(xb, block_e, n_used, block_valid, layer, w_up, b_up, w_dn, b_dn):
    d = SUBLANES * LANES
    rows = MOE_BLOCK * SUBLANES
    n_blocks = xb.shape[0] // rows
    d_up = w_up.shape[3]
    d_exp = w_dn.shape[2]
    wsel = lambda b, be, nu, va: (layer, be[b], 0, 0)
    return pl.pallas_call(
        _expert_kernel,
        grid_spec=pltpu.PrefetchScalarGridSpec(
            num_scalar_prefetch=3,
            grid=(n_blocks,),
            in_specs=[pl.BlockSpec((rows, LANES), lambda b, be, nu, va: (jnp.minimum(b, nu[0] - 1), 0)),
                      pl.BlockSpec((1, 1, d, d_up), wsel),
                      pl.BlockSpec((1, 1, 1, d_up), wsel),
                      pl.BlockSpec((1, 1, d_exp, d), wsel),
                      pl.BlockSpec((1, 1, 1, d), wsel)],
            out_specs=pl.BlockSpec((rows, LANES), lambda b, be, nu, va: (b, 0)),
            scratch_shapes=[pltpu.VMEM((d, d_up), BF16), pltpu.VMEM((d_exp, d), BF16)]),
        out_shape=jax.ShapeDtypeStruct(xb.shape, F32),
        compiler_params=_cparams(("arbitrary",)),
        name="moe_experts",
    )(block_e, n_used, block_valid, xb, w_up, b_up, w_dn, b_dn)


def _sc_gather_tiles(table, idx):
    m = idx.shape[0]
    mesh = plsc.VectorSubcoreMesh(core_axis_name="core", subcore_axis_name="subcore")

    @functools.partial(pl.kernel, out_type=jax.ShapeDtypeStruct((m, SUBLANES, LANES), table.dtype), mesh=mesh)
    def gather(table_hbm, idx_hbm, out_hbm):
        def window(idx_vmem, out_vmem):
            pltpu.sync_copy(table_hbm.at[idx_vmem.at[0, pl.ds(0, SC_WINDOW)]], out_vmem)

        pltpu.emit_pipeline(
            window,
            grid=(m // SC_WINDOW,),
            in_specs=[pl.BlockSpec((1, LANES), lambda i: (i, 0))],
            out_specs=[pl.BlockSpec((SC_WINDOW, SUBLANES, LANES), lambda i: (i, 0, 0))],
            core_axis_name=("core", "subcore"),
            dimension_semantics=(pltpu.PARALLEL,),
        )(idx_hbm, out_hbm)

    idx_rows = jnp.pad(idx.reshape(m // SC_WINDOW, SC_WINDOW), ((0, 0), (0, LANES - SC_WINDOW)))
    return gather(table, idx_rows)


def _sc_scatter_tiles(tiles, idx_by_choice, n_out):
    n = tiles.shape[0]
    mesh = plsc.VectorSubcoreMesh(core_axis_name="core", subcore_axis_name="subcore")

    @functools.partial(pl.kernel, out_type=jax.ShapeDtypeStruct((n_out, SUBLANES, LANES), tiles.dtype),
                       mesh=mesh)
    def scatter(tiles_hbm, *refs):
        idx_hbm, out_hbm = refs[:TOP_K], refs[TOP_K]

        def window(tiles_vmem, *idx_vmem):
            for k in range(TOP_K):
                pltpu.sync_copy(tiles_vmem, out_hbm.at[idx_vmem[k].at[0, pl.ds(0, SC_WINDOW)]])

        pltpu.emit_pipeline(
            window,
            grid=(n // SC_WINDOW,),
            in_specs=[pl.BlockSpec((SC_WINDOW, SUBLANES, LANES), lambda i: (i, 0, 0))]
                     + [pl.BlockSpec((1, LANES), lambda i: (i, 0))] * TOP_K,
            out_specs=[],
            core_axis_name=("core", "subcore"),
            dimension_semantics=(pltpu.PARALLEL,),
        )(tiles_hbm, *idx_hbm)

    idx_rows = jnp.pad(idx_by_choice.reshape(TOP_K, n // SC_WINDOW, SC_WINDOW),
                       ((0, 0), (0, 0), (0, LANES - SC_WINDOW)))
    return scatter(tiles, *[idx_rows[k] for k in range(TOP_K)])


def _combine_kernel(h_ref, y0_ref, y1_ref, y2_ref, y3_ref, gate_ref, pp_ref, ps_ref, wpg_ref, wpp_ref,
                    g_ref, b_ref, *outs, n_ptiles):
    outp_ref, outs_ref = outs[0], outs[-1]
    tm = h_ref.shape[0]
    gate = gate_ref[...]
    moe = _load_token_tiles(y0_ref, (), tm) * gate[:, 0:1]
    for k, y_ref in enumerate((y1_ref, y2_ref, y3_ref), start=1):
        moe = moe + _load_token_tiles(y_ref, (), tm) * gate[:, k:k + 1]
    h2 = _layer_norm(DN_ALPHA * h_ref[...] + moe, g_ref[...], b_ref[...])
    embed_gate = _sigmoid(_dot(h2, wpg_ref[...]))

    def finish(p_ref, out_ref):
        out_ref[...] = h2 + embed_gate * _dot(p_ref[...], wpp_ref[...])

    is_prompt = pl.program_id(0) < n_ptiles
    pl.when(is_prompt)(functools.partial(finish, pp_ref, outp_ref))
    pl.when(jnp.logical_not(is_prompt))(functools.partial(finish, ps_ref, outs_ref))


def _combine(h, y, gate, p_prompt, p_sample, layer, n_p, split, wpg_bf16, wpp_bf16, g_row, b_row):
    n, d = h.shape
    e = p_prompt.shape[1]
    n_tiles = n // TM_TOK
    n_ptiles = n_p // TM_TOK
    n_stiles = (n - n_p) // TM_TOK
    row = lambda i: (i, 0)
    row_p = lambda i: (jnp.minimum(i, n_ptiles - 1), 0)
    row_s = lambda i: (jnp.maximum(i - n_ptiles, 0), 0)
    fixed = lambda i: (0, 0)
    choice = lambda k: pl.BlockSpec((TM_TOK * SUBLANES, LANES), lambda i: (k * n_tiles + i, 0))
    if split:
        out_specs = [pl.BlockSpec((TM_TOK, d), row_p), pl.BlockSpec((TM_TOK, d), row_s)]
        out_shape = [jax.ShapeDtypeStruct((n_p, d), F32), jax.ShapeDtypeStruct((n - n_p, d), F32)]
    else:
        out_specs = pl.BlockSpec((TM_TOK, d), row)
        out_shape = jax.ShapeDtypeStruct((n, d), F32)
    return pl.pallas_call(
        functools.partial(_combine_kernel, n_ptiles=n_ptiles),
        grid=(n_tiles,),
        in_specs=[pl.BlockSpec((TM_TOK, d), row)] + [choice(k) for k in range(TOP_K)]
                 + [pl.BlockSpec((TM_TOK, LANES), row),
                    pl.BlockSpec((TM_TOK, e), lambda i: (layer * n_ptiles + row_p(i)[0], 0)),
                    pl.BlockSpec((TM_TOK, e), lambda i: (layer * n_stiles + row_s(i)[0], 0)),
                    pl.BlockSpec((d, d), fixed), pl.BlockSpec((e, d), fixed), pl.BlockSpec((1, d), fixed),
                    pl.BlockSpec((1, d), fixed)],
        out_specs=out_specs,
        out_shape=out_shape,
        compiler_params=_cparams(("arbitrary",)),
        name="moe_combine_ln_embed",
    )(h, y, y, y, y, gate, p_prompt, p_sample, wpg_bf16, wpp_bf16, g_row, b_row)


def _layer_tail(h, ht, p_prompt, p_sample, layer, n_p, split, g2, b2, w_r, b_r, w_up, b_up, w_dn, b_dn,
                w_pg, w_pp):
    n, d = h.shape
    wr_pad = jnp.pad(w_r, ((0, 0), (0, LANES - N_EXPERTS)))
    br_row = jnp.pad(b_r, (0, LANES - N_EXPERTS))[None]
    n_asg = n * TOP_K
    n_blocks = n_asg // MOE_BLOCK + N_EXPERTS
    assert n_asg % MOE_BLOCK == 0 and n % SC_WINDOW == 0
    n_slots = n_blocks * MOE_BLOCK
    idx, gate, rank, pstart, tab = _router(h, wr_pad, br_row, n_blocks)
    experts = jnp.arange(N_EXPERTS, dtype=I32)[:, None, None]
    group_start = jnp.sum(jnp.where(idx[None, :TOP_K] == experts, pstart[0, :N_EXPERTS, None, None], 0), axis=0)
    slot_by_choice = group_start + rank[:TOP_K]
    xb = _sc_scatter_tiles(ht.reshape(n, SUBLANES, LANES), slot_by_choice, n_slots)
    yb = _experts(xb.reshape(n_slots * SUBLANES, LANES), tab[:n_blocks, 0], tab[0, 2:3], tab[:n_blocks, 1],
                  layer, w_up, b_up[:, :, None, :], w_dn, b_dn[:, :, None, :])
    y = _sc_gather_tiles(yb.reshape(n_slots, SUBLANES, LANES), slot_by_choice.reshape(-1))
    return _combine(h, y.reshape(n_asg * SUBLANES, LANES), gate, p_prompt, p_sample, layer, n_p, split,
                    w_pg.astype(BF16), w_pp.astype(BF16), g2[None], b2[None])


def _lane_row(v, lane0):
    return jnp.zeros((1, LANES), F32).at[0, lane0:lane0 + v.shape[0]].set(v.astype(F32))


def kernel(x_prompt, x_sample, cache_fox_k, cache_fox_v, cache_fox_logf, state_gdn, state_gdn_conv,
           cache_pool, p_prompt, p_sample, w_in_ab, b_fgate, gdn_a_log, gdn_dt_bias, gdn_conv_w,
           gdn_norm_g, w_out_ab, pool_w, pool_scale, w_out_pool, ln1_g, ln1_b, ln2_g, ln2_b,
           w_router, b_router, w_expert_up, b_expert_up, w_expert_down, b_expert_down,
           w_ple_gate, w_ple_proj):
    n_pb, seq, d = x_prompt.shape
    n_sb, dseq, _ = x_sample.shape
    past = cache_fox_k.shape[2]
    assert n_pb == 1 and dseq == CHUNK and past % dseq == 0 and seq % TQ == 0
    assert d == SUBLANES * LANES
    n_p = n_pb * seq
    n_s = n_sb * dseq
    n = n_p + n_s
    assert n_p % TM_TOK == 0 and n_s % TM_TOK == 0
    n_layers = p_prompt.shape[0]
    pp_all = p_prompt.reshape(n_layers * n_p, -1)
    ps_all = p_sample.reshape(n_layers * n_s, -1)

    def tail(h, ht, i, split):
        return _layer_tail(h, ht, pp_all, ps_all, i, n_p, split, ln2_g[i], ln2_b[i], w_router[i], b_router[i],
                           w_expert_up, b_expert_up, w_expert_down, b_expert_down, w_ple_gate[i],
                           w_ple_proj[i])

    w_in = w_in_ab[0]
    n_small = 3 * N_HEADS
    ff0 = 3 * WIDTH
    gq0 = ff0 + N_HEADS
    ga0 = gq0 + 4 * WIDTH
    w_small = jnp.concatenate([w_in[:, ff0:gq0], w_in[:, ga0:ga0 + 2 * N_HEADS],
                               jnp.zeros((d, LANES - n_small), F32)], axis=1)
    w_all = jnp.concatenate([w_in[:, :ff0], w_in[:, gq0:ga0], w_small], axis=1)
    bf_row = _lane_row(b_fgate[0], LANE_F)
    up = _proj(x_prompt.reshape(n_p, d), w_all.astype(BF16), bf_row, U_COLS, False)
    us = _proj(x_sample.reshape(n_s, d), w_all, bf_row, LANES, True)

    u3 = up[None]
    us3 = us.reshape(n_sb, dseq, U_COLS)
    cq_p, ck_p = _cumsum(u3, 1, n_p, TK, COL_S // LANES)
    lf_s = jnp.concatenate(
        [jnp.pad(cache_fox_logf[0].astype(F32), ((0, 0), (0, 0), (0, LANES - N_HEADS))),
         us3[:, :, COL_S:]], axis=1)
    cq_s, ck_s = _cumsum(lf_s, n_sb, past + dseq, past + dseq, 0)

    of_p = _fox(u3, lambda b, i: (0, i, COL_Q // WIDTH), u3, lambda b, j: (0, j, COL_K // WIDTH),
                u3, lambda b, j: (0, j, COL_V // WIDTH), cq_p, lambda b, i: (0, i, 0),
                ck_p, lambda b, j: (0, 0, j), 1, n_p // TQ, TQ, TK, 0)
    k_all = jnp.concatenate([cache_fox_k[0].reshape(n_sb, past, WIDTH), us3[:, :, COL_K:COL_K + WIDTH]], axis=1)
    v_all = jnp.concatenate([cache_fox_v[0].reshape(n_sb, past, WIDTH), us3[:, :, COL_V:COL_V + WIDTH]], axis=1)
    of_s = _fox(us3, lambda b, i: (b, 0, COL_Q // WIDTH), k_all, lambda b, j: (b, 0, 0),
                v_all, lambda b, j: (b, 0, 0), cq_s, lambda b, i: (b, past // dseq, 0),
                ck_s, lambda b, j: (b, 0, 0), n_sb, 1, dseq, past + dseq, past, hi=True)

    gw = 3 * WIDTH
    conv_w = jnp.pad(gdn_conv_w[0], ((0, SUBLANES - CONV_WIDTH), (0, 0)))
    gdn_args = (conv_w, _lane_row(gdn_a_log[0], LANE_A), _lane_row(gdn_dt_bias[0], LANE_A),
                gdn_norm_g[0][None])
    n_pstep = n_p // GDN_ROWS
    ends = lambda k: (jnp.asarray((np.arange(k) == 0).astype(np.int32)),
                      jnp.asarray((np.arange(k) == k - 1).astype(np.int32)))
    og_p, st_p = _gdn(up.reshape(n_pstep, GDN_ROWS, U_COLS), False, *ends(n_pstep),
                      jnp.zeros((n_pstep,), I32), jnp.zeros((1, SUBLANES, gw), F32), gdn_args[0],
                      jnp.zeros((1, N_HEADS, HEAD_DIM, HEAD_DIM), F32), *gdn_args[1:])
    conv_past = jnp.pad(state_gdn_conv[0].astype(F32), ((0, 0), (SUBLANES - (CONV_WIDTH - 1), 0), (0, 0)))
    ones = jnp.ones((n_sb,), I32)
    og_s, st_s = _gdn(us3, True, ones, ones, jnp.arange(n_sb, dtype=I32), conv_past, gdn_args[0],
                      state_gdn[0].astype(F32), *gdn_args[1:])
    h, ht = _outproj_ln((x_prompt.reshape(n_p, d), of_p.reshape(n_p, WIDTH), og_p.reshape(n_p, WIDTH)),
                        (x_sample.reshape(n_s, d), of_s.reshape(n_s, WIDTH), og_s.reshape(n_s, WIDTH)),
                        w_out_ab[0], ln1_g[0][None], ln1_b[0][None])
    x1 = tail(h, ht, 0, False)

    pool_args = (pool_w[0].astype(BF16), pool_scale[0][None], w_out_pool[0].astype(BF16),
                 ln1_g[1][None], ln1_b[1][None])
    ratio = TM_TOK // POOL_HALO
    x1_halo = x1.reshape(n // POOL_HALO, POOL_HALO, d)
    h_pool = _pool_ln(x1, lambda i: (i, 0), x1_halo, lambda i: (jnp.maximum(i * ratio - 1, 0), 0, 0),
                      n_p // TM_TOK, TM_TOK, 0, True, *pool_args)
    cache16 = jnp.pad(cache_pool[0].astype(F32), ((0, 0), (POOL_HALO - POOL_STATE, 0), (0, 0)))
    h, ht = _pool_ln(x1, lambda i: (n_p // dseq + i, 0), cache16, lambda i: (i, 0, 0),
                     n_sb, dseq, past, False, *pool_args, into=h_pool)
    x2_p, x2_s = tail(h, ht, 1, True)

    return (x2_p.reshape(n_pb, seq, d), x2_s.reshape(n_sb, dseq, d),
            up[:, COL_K:COL_K + WIDTH].reshape(1, n_pb, seq, N_HEADS, HEAD_DIM),
            up[:, COL_V:COL_V + WIDTH].reshape(1, n_pb, seq, N_HEADS, HEAD_DIM),
            up[:, COL_S:COL_S + N_HEADS].reshape(1, n_pb, seq, N_HEADS),
            st_p.reshape(1, n_pb, N_HEADS, HEAD_DIM, HEAD_DIM),
            up[seq - (CONV_WIDTH - 1):, COL_G:COL_G + gw].reshape(1, n_pb, CONV_WIDTH - 1, gw),
            x1[n_p - POOL_STATE:n_p].reshape(1, n_pb, POOL_STATE, d),
            us[:, COL_K:COL_K + WIDTH].reshape(1, n_sb, dseq, N_HEADS, HEAD_DIM),
            us[:, COL_V:COL_V + WIDTH].reshape(1, n_sb, dseq, N_HEADS, HEAD_DIM),
            us[:, COL_S:COL_S + N_HEADS].reshape(1, n_sb, dseq, N_HEADS),
            st_s.reshape(1, n_sb, N_HEADS, HEAD_DIM, HEAD_DIM),
            us[:, COL_G:COL_G + gw].reshape(n_sb, dseq, gw)[:, dseq - (CONV_WIDTH - 1):].reshape(
                1, n_sb, CONV_WIDTH - 1, gw),
            x1[n_p:].reshape(n_sb, dseq, d)[:, dseq - POOL_STATE:].reshape(1, n_sb, POOL_STATE, d))
```

```python
import functools

import numpy as np
import jax
import jax.numpy as jnp
from jax import lax
from jax.experimental import pallas as pl
from jax.experimental.pallas import tpu as pltpu
from jax.experimental.pallas import tpu_sc as plsc

F32 = jnp.float32
BF16 = jnp.bfloat16
I32 = jnp.int32
HIGHEST = lax.Precision.HIGHEST

LANES = 128
SUBLANES = 8
VMEM_LIMIT = 56 * 1024 * 1024

HEAD_DIM = 128
N_HEADS = 4
WIDTH = N_HEADS * HEAD_DIM
CHUNK = 64
CONV_WIDTH = 4
POOL_WINDOWS = (2, 4, 8, 16)
POOL_HALO = 16
POOL_STATE = 15
N_EXPERTS = 32
TOP_K = 4
SWIGLU_LIMIT = 7.0
SWIGLU_ALPHA = 1.702
DEPTH = 2
DN_ALPHA = (2 * DEPTH) ** 0.25
LN_EPS = 1e-5
NORM_EPS = 1e-6
NEG_INF = -1e30
LOG2E = 1.4426950408889634

COL_Q, COL_K, COL_V = 0, WIDTH, 2 * WIDTH
COL_G = 3 * WIDTH
COL_Z = 6 * WIDTH
COL_S = 7 * WIDTH
U_COLS = COL_S + LANES
LANE_F, LANE_A, LANE_B = 0, N_HEADS, 2 * N_HEADS

TM_PROJ = 256
TM_TOK = 512
SC_WINDOW = 32
MOE_BLOCK = 512
TQ = 1024
TK = 512
GDN_ROWS = 256
GDN_CHUNK = CHUNK


def _cparams(sem):
    return pltpu.CompilerParams(dimension_semantics=sem, vmem_limit_bytes=VMEM_LIMIT)


def _softplus(x):
    return jnp.maximum(x, 0.0) + jnp.log1p(jnp.exp(-jnp.abs(x)))


def _sigmoid(x):
    return 1.0 / (1.0 + jnp.exp(-x))


def _silu(x):
    return x * _sigmoid(x)


def _layer_norm(y, g, b):
    mu = jnp.mean(y, axis=-1, keepdims=True)
    yc = y - mu
    var = jnp.mean(yc * yc, axis=-1, keepdims=True)
    return yc * lax.rsqrt(var + LN_EPS) * g + b


def _dot_general(a, b, dims, hi):
    if hi:
        return lax.dot_general(a.astype(F32), b.astype(F32), (dims, ((), ())), precision=HIGHEST,
                               preferred_element_type=F32)
    return lax.dot_general(a.astype(BF16), b.astype(BF16), (dims, ((), ())), preferred_element_type=F32)


def _dot(a, b, hi=False):
    return _dot_general(a, b, ((1,), (0,)), hi)


def _dot_nt(a, b, hi=False):
    return _dot_general(a, b, ((1,), (1,)), hi)


def _dot_tn(a, b, hi=False):
    return _dot_general(a, b, ((0,), (0,)), hi)


def _spread_lanes(x, width):
    if width % LANES == 0:
        return jnp.concatenate([x] * (width // LANES), axis=1)
    return jnp.broadcast_to(x[:, 0:1], (x.shape[0], width))


def _load_token_tiles(ref, lead, n_tok):
    return jnp.concatenate([ref[(*lead, pl.ds(j, n_tok, stride=SUBLANES), slice(None))]
                            for j in range(SUBLANES)], axis=1)


def _store_token_tiles(ref, lead, x):
    for j in range(SUBLANES):
        ref[(*lead, pl.ds(j, x.shape[0], stride=SUBLANES), slice(None))] = x[:, j * LANES:(j + 1) * LANES]


def _lanes_to_rows(x, lane0):
    r = lax.broadcasted_iota(I32, (SUBLANES, LANES), 0)
    c = lax.broadcasted_iota(I32, (SUBLANES, LANES), 1)
    sel = (c == r + lane0).astype(F32)
    return lax.dot_general(sel, x, (((1,), (1,)), ((), ())), precision=HIGHEST,
                           preferred_element_type=F32)


def _proj_kernel(x_ref, w_ref, bf_ref, u_ref, *, hi):
    u = _dot(x_ref[...], w_ref[...], hi)
    u_ref[...] = u

    @pl.when(pl.program_id(1) == pl.num_programs(1) - 1)
    def _():
        small = u[:, u.shape[1] - LANES:]
        lane = lax.broadcasted_iota(I32, small.shape, 1)
        logf = -_softplus(-(small + bf_ref[...]))
        u_ref[:, u.shape[1] - LANES:] = jnp.where(lane < LANE_A, logf, small)


def _proj(x, w, bf_row, tn, hi):
    n, d = x.shape
    m = w.shape[1]
    return pl.pallas_call(
        functools.partial(_proj_kernel, hi=hi),
        grid=(n // TM_PROJ, m // tn),
        in_specs=[pl.BlockSpec((TM_PROJ, d), lambda i, j: (i, 0)),
                  pl.BlockSpec((d, tn), lambda i, j: (0, j)),
                  pl.BlockSpec((1, LANES), lambda i, j: (0, 0))],
        out_specs=pl.BlockSpec((TM_PROJ, tn), lambda i, j: (i, j)),
        out_shape=jax.ShapeDtypeStruct((n, m), F32),
        compiler_params=_cparams(("parallel", "parallel")),
        name="in_proj",
    )(x, w, bf_row)


def _cumsum_kernel(lf_ref, crep_ref, crow_ref, carry_ref):
    @pl.when(pl.program_id(1) == 0)
    def _():
        carry_ref[...] = jnp.zeros_like(carry_ref)

    lf = lf_ref[0]
    t = lf.shape[0]
    r = lax.broadcasted_iota(I32, (t, t), 0)
    c = lax.broadcasted_iota(I32, (t, t), 1)
    tril = (c <= r).astype(F32)
    cs = jnp.dot(tril, lf, precision=HIGHEST, preferred_element_type=F32) + carry_ref[0:1, :]
    carry_ref[...] = jnp.broadcast_to(cs[t - 1:t, :], carry_ref.shape)
    c2 = cs * LOG2E
    crow_ref[0] = _lanes_to_rows(c2, LANE_F)
    for h in range(N_HEADS):
        crep_ref[0, :, h * HEAD_DIM:(h + 1) * HEAD_DIM] = jnp.broadcast_to(
            c2[:, LANE_F + h:LANE_F + h + 1], (t, HEAD_DIM))


def _cumsum(arr, n_batch, length, tl, col_block):
    return pl.pallas_call(
        _cumsum_kernel,
        grid=(n_batch, length // tl),
        in_specs=[pl.BlockSpec((1, tl, LANES), lambda b, j: (b, j, col_block))],
        out_specs=[pl.BlockSpec((1, tl, WIDTH), lambda b, j: (b, j, 0)),
                   pl.BlockSpec((1, SUBLANES, tl), lambda b, j: (b, 0, j))],
        out_shape=[jax.ShapeDtypeStruct((n_batch, length, WIDTH), F32),
                   jax.ShapeDtypeStruct((n_batch, SUBLANES, length), F32)],
        scratch_shapes=[pltpu.VMEM((SUBLANES, LANES), F32)],
        compiler_params=_cparams(("parallel", "arbitrary")),
        name="logf_cumsum",
    )(arr)


def _fox_kernel(qi_ref, kj_ref, last_ref, q_ref, k_ref, v_ref, cq_ref, ck_ref, o_ref,
                m_ref, l_ref, acc_ref, *, tq, tk, past, hi):
    s_idx = pl.program_id(1)
    qi = qi_ref[s_idx]
    kj = kj_ref[s_idx]

    @pl.when(kj == 0)
    def _():
        m_ref[...] = jnp.full_like(m_ref, NEG_INF)
        l_ref[...] = jnp.zeros_like(l_ref)
        acc_ref[...] = jnp.zeros_like(acc_ref)

    def update(masked):
        if masked:
            q_pos = past + qi * tq + lax.broadcasted_iota(I32, (tq, tk), 0)
            k_pos = kj * tk + lax.broadcasted_iota(I32, (tq, tk), 1)
            visible = k_pos <= q_pos
        for h in range(N_HEADS):
            cols = slice(h * HEAD_DIM, (h + 1) * HEAD_DIM)
            q = q_ref[0, :, cols] * (HEAD_DIM ** -0.5 * LOG2E)
            t = _dot_nt(q, k_ref[0, :, cols], hi) - ck_ref[0, h:h + 1, :]
            if masked:
                t = jnp.where(visible, t, NEG_INF)
            cq = cq_ref[0, :, cols]
            m_prev = m_ref[h]
            m_new = jnp.maximum(m_prev, jnp.max(t, axis=-1, keepdims=True) + cq)
            p = jnp.exp2(t - _spread_lanes(m_new - cq, tk))
            alpha = jnp.exp2(m_prev - m_new)
            l_ref[h] = alpha * l_ref[h] + jnp.sum(p, axis=-1, keepdims=True)
            acc_ref[:, cols] = alpha * acc_ref[:, cols] + _dot(p, v_ref[0, :, cols], hi)
            m_ref[h] = m_new

    crosses_diagonal = kj * tk + (tk - 1) > past + qi * tq
    pl.when(crosses_diagonal)(functools.partial(update, True))
    pl.when(jnp.logical_not(crosses_diagonal))(functools.partial(update, False))

    @pl.when(last_ref[s_idx] == 1)
    def _():
        for h in range(N_HEADS):
            cols = slice(h * HEAD_DIM, (h + 1) * HEAD_DIM)
            o_ref[0, :, cols] = acc_ref[:, cols] / l_ref[h]


def _fox_schedule(n_q, tq, tk, past):
    qi, kj, last = [], [], []
    for i in range(n_q):
        hi = (past + (i + 1) * tq - 1) // tk
        for j in range(hi + 1):
            qi.append(i)
            kj.append(j)
            last.append(1 if j == hi else 0)
    return (jnp.asarray(np.array(qi, np.int32)), jnp.asarray(np.array(kj, np.int32)),
            jnp.asarray(np.array(last, np.int32)))


def _fox(q_arr, q_map, k_arr, k_map, v_arr, v_map, cq_arr, cq_map, ck_arr, ck_map,
         n_batch, n_q, tq, tk, past, hi=False):
    qi, kj, last = _fox_schedule(n_q, tq, tk, past)
    n_steps = int(qi.shape[0])
    spec = lambda shape, fn, tab: pl.BlockSpec(shape, lambda b, s, qi_r, kj_r, la_r: fn(b, (qi_r if tab == 'q' else kj_r)[s]))
    return pl.pallas_call(
        functools.partial(_fox_kernel, tq=tq, tk=tk, past=past, hi=hi),
        grid_spec=pltpu.PrefetchScalarGridSpec(
            num_scalar_prefetch=3,
            grid=(n_batch, n_steps),
            in_specs=[spec((1, tq, WIDTH), q_map, 'q'),
                      spec((1, tk, WIDTH), k_map, 'k'),
                      spec((1, tk, WIDTH), v_map, 'k'),
                      spec((1, tq, WIDTH), cq_map, 'q'),
                      spec((1, SUBLANES, tk), ck_map, 'k')],
            out_specs=spec((1, tq, WIDTH), lambda b, i: (b, i, 0), 'q'),
            scratch_shapes=[pltpu.VMEM((N_HEADS, tq, HEAD_DIM), F32),
                            pltpu.VMEM((N_HEADS, tq, HEAD_DIM), F32),
                            pltpu.VMEM((tq, WIDTH), F32)]),
        out_shape=jax.ShapeDtypeStruct((n_batch, n_q * tq, WIDTH), F32),
        compiler_params=_cparams(("parallel", "arbitrary")),
        name="fox_attention",
    )(qi, kj, last, q_arr, k_arr, v_arr, cq_arr, ck_arr)


def _gdn_kernel(first_ref, last_ref, seq_ref,
                pre_ref, z_ref, sm_ref, cpast_ref, convw_ref, s0_ref, alog_ref, dtb_ref, ng_ref,
                o_ref, sout_ref, stage_ref, s_ref, *, rows, chunk, hi):
    dot, dot_nt, dot_tn = (functools.partial(f, hi=hi) for f in (_dot, _dot_nt, _dot_tn))
    step = pl.program_id(0)
    halo = SUBLANES
    n_chunks = rows // chunk

    @pl.when(first_ref[step] == 1)
    def _():
        stage_ref[0:halo, :] = cpast_ref[0]
        s_ref[...] = s0_ref[0]

    stage_ref[halo:halo + rows, :] = pre_ref[0]
    conv = stage_ref[halo:halo + rows, :] * convw_ref[CONV_WIDTH - 1:CONV_WIDTH, :]
    for j in range(1, CONV_WIDTH):
        conv = conv + (stage_ref[halo - j:halo - j + rows, :]
                       * convw_ref[CONV_WIDTH - 1 - j:CONV_WIDTH - j, :])
    stage_ref[0:halo, :] = stage_ref[rows:rows + halo, :]
    act = _silu(conv)

    small = sm_ref[0]
    beta_all = _sigmoid(small)
    g_all = -jnp.exp(alog_ref[...]) * _softplus(small + dtb_ref[...])
    r = lax.broadcasted_iota(I32, (rows, rows), 0)
    c = lax.broadcasted_iota(I32, (rows, rows), 1)
    same_chunk = (r // chunk) == (c // chunk)
    incl = same_chunk & (c <= r)
    strict = same_chunk & (c < r)
    eye = (c == r).astype(F32)
    gc_all = jnp.dot(incl.astype(F32), g_all, precision=HIGHEST, preferred_element_type=F32)
    gc_rows = _lanes_to_rows(gc_all, LANE_A)

    heads = range(N_HEADS)
    head_cols = [slice(h * HEAD_DIM, (h + 1) * HEAD_DIM) for h in heads]
    q, k, gc, decay, kb, vb, low = [], [], [], [], [], [], []
    for h in heads:
        qh = act[:, h * HEAD_DIM:(h + 1) * HEAD_DIM]
        kh = act[:, WIDTH + h * HEAD_DIM:WIDTH + (h + 1) * HEAD_DIM]
        vh = act[:, 2 * WIDTH + h * HEAD_DIM:2 * WIDTH + (h + 1) * HEAD_DIM]
        q.append(qh * lax.rsqrt(jnp.sum(qh * qh, axis=-1, keepdims=True) + NORM_EPS) * (HEAD_DIM ** -0.5))
        k.append(kh * lax.rsqrt(jnp.sum(kh * kh, axis=-1, keepdims=True) + NORM_EPS))
        beta = beta_all[:, LANE_B + h:LANE_B + h + 1]
        gc.append(gc_all[:, LANE_A + h:LANE_A + h + 1])
        diff = gc[h] - gc_rows[h:h + 1, :]
        decay.append(jnp.where(incl, jnp.exp(jnp.where(incl, diff, 0.0)), 0.0))
        kb.append(k[h] * beta)
        vb.append(vh * beta)
    for h in heads:
        low.append(jnp.where(strict, dot_nt(kb[h], k[h]) * decay[h], 0.0))
    inv = [eye - low[h] for h in heads]
    pw = [dot(low[h], low[h]) for h in heads]
    n_sq = chunk.bit_length() - 2
    for it in range(n_sq):
        inv = [inv[h] + dot(inv[h], pw[h]) for h in heads]
        if it + 1 < n_sq:
            pw = [dot(pw[h], pw[h]) for h in heads]
    egc = [jnp.exp(gc[h]) for h in heads]
    uw = [dot(inv[h], jnp.concatenate([vb[h], kb[h] * egc[h]], axis=1)) for h in heads]
    intra = [jnp.where(incl, dot_nt(q[h], k[h]) * decay[h], 0.0) for h in heads]
    qd = [q[h] * egc[h] for h in heads]
    g_last = [[gc[h][(g + 1) * chunk - 1:(g + 1) * chunk, :] for g in range(n_chunks)] for h in heads]
    kd = [k[h] * jnp.exp(jnp.concatenate([jnp.broadcast_to(gl, (chunk, 1)) for gl in g_last[h]], axis=0)
                         - gc[h]) for h in heads]
    state = [s_ref[h] for h in heads]
    v_new = [[] for _ in heads]
    for g in range(n_chunks):
        rs = slice(g * chunk, (g + 1) * chunk)
        for h in heads:
            v_new[h].append(uw[h][rs, :HEAD_DIM] - dot(uw[h][rs, HEAD_DIM:], state[h]))
        for h in heads:
            v_rows = jnp.concatenate(
                v_new[h] + [jnp.zeros((rows - (g + 1) * chunk, HEAD_DIM), F32)] * (g + 1 < n_chunks), axis=0)
            o = dot(qd[h][rs, :], state[h]) + dot(intra[h][rs, :], v_rows)
            state[h] = state[h] * jnp.exp(g_last[h][g]) + dot_tn(kd[h][rs, :], v_new[h][g])
            o = (o * lax.rsqrt(jnp.mean(o * o, axis=-1, keepdims=True) + NORM_EPS)
                 * ng_ref[...] * _silu(z_ref[0, rs, head_cols[h]]))
            o_ref[0, rs, head_cols[h]] = o
    for h in heads:
        s_ref[h] = state[h]

    @pl.when(last_ref[step] == 1)
    def _():
        sout_ref[0] = s_ref[...]


def _gdn(u_view, hi, first, last, seq, conv_past, conv_w, s0, alog_row, dtb_row, ng_row):
    n_steps, rows, _ = u_view.shape
    n_seq = s0.shape[0]
    gw = 3 * WIDTH
    return pl.pallas_call(
        functools.partial(_gdn_kernel, rows=rows, chunk=min(rows, GDN_CHUNK), hi=hi),
        grid_spec=pltpu.PrefetchScalarGridSpec(
            num_scalar_prefetch=3,
            grid=(n_steps,),
            in_specs=[pl.BlockSpec((1, rows, gw), lambda s, f, l, q: (s, 0, COL_G // gw)),
                      pl.BlockSpec((1, rows, WIDTH), lambda s, f, l, q: (s, 0, COL_Z // WIDTH)),
                      pl.BlockSpec((1, rows, LANES), lambda s, f, l, q: (s, 0, COL_S // LANES)),
                      pl.BlockSpec((1, SUBLANES, gw), lambda s, f, l, q: (q[s], 0, 0)),
                      pl.BlockSpec((SUBLANES, gw), lambda s, f, l, q: (0, 0)),
                      pl.BlockSpec((1, N_HEADS, HEAD_DIM, HEAD_DIM), lambda s, f, l, q: (q[s], 0, 0, 0)),
                      pl.BlockSpec((1, LANES), lambda s, f, l, q: (0, 0)),
                      pl.BlockSpec((1, LANES), lambda s, f, l, q: (0, 0)),
                      pl.BlockSpec((1, LANES), lambda s, f, l, q: (0, 0))],
            out_specs=[pl.BlockSpec((1, rows, WIDTH), lambda s, f, l, q: (s, 0, 0)),
                       pl.BlockSpec((1, N_HEADS, HEAD_DIM, HEAD_DIM), lambda s, f, l, q: (q[s], 0, 0, 0))],
            scratch_shapes=[pltpu.VMEM((rows + SUBLANES, gw), F32),
                            pltpu.VMEM((N_HEADS, HEAD_DIM, HEAD_DIM), F32)]),
        out_shape=[jax.ShapeDtypeStruct((n_steps, rows, WIDTH), F32),
                   jax.ShapeDtypeStruct((n_seq, N_HEADS, HEAD_DIM, HEAD_DIM), F32)],
        compiler_params=_cparams(("arbitrary",)),
        name="gated_deltanet",
    )(first, last, seq, u_view, u_view, u_view, conv_past, conv_w, s0, alog_row, dtb_row, ng_row)


def _outproj_ln_kernel(xp_ref, ofp_ref, ogp_ref, xs_ref, ofs_ref, ogs_ref, w_ref, wf_ref, g_ref, b_ref,
                       h_ref, ht_ref, *, hi_from):
    def run(hi, x_ref, of_ref, og_ref):
        w = wf_ref if hi else w_ref
        mix = _dot(of_ref[...], w[0:WIDTH, :], hi) + _dot(og_ref[...], w[WIDTH:2 * WIDTH, :], hi)
        h = _layer_norm(DN_ALPHA * x_ref[...] + mix, g_ref[...], b_ref[...])
        h_ref[...] = h
        _store_token_tiles(ht_ref, (), h)

    pl.when(pl.program_id(0) < hi_from)(functools.partial(run, False, xp_ref, ofp_ref, ogp_ref))
    pl.when(pl.program_id(0) >= hi_from)(functools.partial(run, True, xs_ref, ofs_ref, ogs_ref))


def _outproj_ln(prompt, sample, w, g_row, b_row):
    n_p, d = prompt[0].shape
    n = n_p + sample[0].shape[0]
    hi_from = n_p // TM_TOK
    row = lambda i: (i, 0)
    row_p = lambda i: (jnp.minimum(i, hi_from - 1), 0)
    row_s = lambda i: (jnp.maximum(i - hi_from, 0), 0)
    fixed = lambda i: (0, 0)
    group = lambda rows: [pl.BlockSpec((TM_TOK, d), rows), pl.BlockSpec((TM_TOK, WIDTH), rows),
                          pl.BlockSpec((TM_TOK, WIDTH), rows)]
    return pl.pallas_call(
        functools.partial(_outproj_ln_kernel, hi_from=hi_from),
        grid=(n // TM_TOK,),
        in_specs=group(row_p) + group(row_s) + [pl.BlockSpec((2 * WIDTH, d), fixed),
                                                pl.BlockSpec((2 * WIDTH, d), fixed),
                                                pl.BlockSpec((1, d), fixed), pl.BlockSpec((1, d), fixed)],
        out_specs=[pl.BlockSpec((TM_TOK, d), row), pl.BlockSpec((TM_TOK * SUBLANES, LANES), row)],
        out_shape=[jax.ShapeDtypeStruct((n, d), F32), jax.ShapeDtypeStruct((n * SUBLANES, LANES), F32)],
        compiler_params=_cparams(("parallel",)),
        name="out_proj_ln",
    )(*prompt, *sample, w.astype(BF16), w, g_row, b_row)


def _pool_ln_kernel(x_ref, halo_ref, pw_ref, ps_ref, w_ref, g_ref, b_ref, h_ref, ht_ref, stage_ref,
                    *, tm, pos0, zero_first_halo):
    i = pl.program_id(0)
    stage_ref[0:POOL_HALO, :] = halo_ref[0]
    if zero_first_halo:
        @pl.when(i == 0)
        def _():
            stage_ref[0:POOL_HALO, :] = jnp.zeros((POOL_HALO, stage_ref.shape[1]), F32)
    x = x_ref[...]
    stage_ref[POOL_HALO:POOL_HALO + tm, :] = x
    gdim = x.shape[1] // len(POOL_WINDOWS)
    pos = pos0 + lax.broadcasted_iota(I32, (tm, 1), 0)
    if zero_first_halo:
        pos = pos + i * tm
    parts = []
    for gi, win in enumerate(POOL_WINDOWS):
        cols = slice(gi * gdim, (gi + 1) * gdim)
        s = stage_ref[POOL_HALO:POOL_HALO + tm, cols]
        for j in range(1, win):
            s = s + stage_ref[POOL_HALO - j:POOL_HALO - j + tm, cols]
        cnt = jnp.minimum(pos + 1, win).astype(F32)
        zg = s / cnt - x[:, cols]
        parts.append(_dot(zg, pw_ref[gi]))
    zg = jnp.concatenate(parts, axis=-1) * ps_ref[...]
    mix = _dot(zg, w_ref[...])
    h = _layer_norm(DN_ALPHA * x + mix, g_ref[...], b_ref[...])
    h_ref[...] = h
    _store_token_tiles(ht_ref, (), h)


def _pool_ln_into_kernel(*refs, **kw):
    _pool_ln_kernel(*refs[:7], *refs[9:], **kw)


def _pool_ln(x, x_map, halo_arr, halo_map, n_tiles, tm, pos0, zero_first_halo,
             pw_bf16, ps_row, w_bf16, g_row, b_row, into=None):
    n, d = x.shape
    gdim = d // len(POOL_WINDOWS)
    fixed = lambda i: (0, 0)
    in_specs = [pl.BlockSpec((tm, d), x_map),
                pl.BlockSpec((1, POOL_HALO, d), halo_map),
                pl.BlockSpec((len(POOL_WINDOWS), gdim, gdim), lambda i: (0, 0, 0)),
                pl.BlockSpec((1, d), fixed), pl.BlockSpec((d, d), fixed),
                pl.BlockSpec((1, d), fixed), pl.BlockSpec((1, d), fixed)]
    args = (x, halo_arr, pw_bf16, ps_row, w_bf16, g_row, b_row)
    kw = dict(tm=tm, pos0=pos0, zero_first_halo=zero_first_halo)
    return pl.pallas_call(
        functools.partial(_pool_ln_kernel if into is None else _pool_ln_into_kernel, **kw),
        grid=(n_tiles,),
        in_specs=in_specs + ([] if into is None else [pl.BlockSpec(memory_space=pl.ANY)] * 2),
        out_specs=[pl.BlockSpec((tm, d), x_map),
                   pl.BlockSpec((tm * SUBLANES, LANES), x_map)],
        out_shape=[jax.ShapeDtypeStruct((n, d), F32), jax.ShapeDtypeStruct((n * SUBLANES, LANES), F32)],
        input_output_aliases={} if into is None else {7: 0, 8: 1},
        scratch_shapes=[pltpu.VMEM((POOL_HALO + tm, d), F32)],
        compiler_params=_cparams(("arbitrary",)),
        name="pool_mixer_ln",
    )(*args, *(() if into is None else into))


def _router_kernel(h_ref, wr_ref, br_ref, idx_ref, gate_ref, rank_ref, pstart_ref, tab_ref, carry_ref):
    @pl.when(pl.program_id(0) == 0)
    def _():
        carry_ref[...] = jnp.zeros_like(carry_ref)

    tm = h_ref.shape[0]
    lane = lax.broadcasted_iota(I32, (tm, LANES), 1).astype(F32)
    logits = jnp.dot(h_ref[...], wr_ref[...], precision=HIGHEST, preferred_element_type=F32)
    work = jnp.where(lane < N_EXPERTS, logits + br_ref[...], -jnp.inf)
    vals, ids = [], []
    for _ in range(TOP_K):
        m = jnp.max(work, axis=-1, keepdims=True)
        ik = jnp.min(jnp.where(work == m, lane, float(LANES)), axis=-1, keepdims=True)
        vals.append(m)
        ids.append(ik)
        work = jnp.where(lane == ik, -jnp.inf, work)
    exps = [jnp.exp(v - vals[0]) for v in vals]
    denom = exps[0]
    for e in exps[1:]:
        denom = denom + e
    multihot = jnp.zeros((tm, LANES), F32)
    idx_out = jnp.zeros((tm, LANES), F32)
    gate_out = jnp.zeros((tm, LANES), F32)
    for k in range(TOP_K):
        multihot = multihot + (lane == ids[k]).astype(F32)
        idx_out = jnp.where(lane == k, ids[k], idx_out)
        gate_out = jnp.where(lane == k, exps[k] / denom, gate_out)
    r = lax.broadcasted_iota(I32, (tm, tm), 0)
    c = lax.broadcasted_iota(I32, (tm, tm), 1)
    before = _dot((c < r).astype(F32), multihot) + carry_ref[0:1, :]
    rank_out = jnp.zeros((tm, LANES), F32)
    for k in range(TOP_K):
        rk = jnp.sum(jnp.where(lane == ids[k], before, 0.0), axis=-1, keepdims=True)
        rank_out = jnp.where(lane == k, rk, rank_out)
    idx_ref[...] = jnp.transpose(idx_out)[:SUBLANES].astype(I32)
    rank_ref[...] = jnp.transpose(rank_out)[:SUBLANES].astype(I32)
    gate_ref[...] = gate_out
    total = carry_ref[0:1, :] + jnp.sum(multihot, axis=0, keepdims=True)
    carry_ref[...] = jnp.broadcast_to(total, carry_ref.shape)

    @pl.when(pl.program_id(0) == pl.num_programs(0) - 1)
    def _():
        n_rows = tab_ref.shape[0]
        padded = jnp.floor((total + (MOE_BLOCK - 1)) / MOE_BLOCK) * MOE_BLOCK
        rr = lax.broadcasted_iota(I32, (LANES, LANES), 0)
        cc = lax.broadcasted_iota(I32, (LANES, LANES), 1)
        pend = jnp.dot(jnp.broadcast_to(padded, (SUBLANES, LANES)), (rr <= cc).astype(F32),
                       precision=HIGHEST, preferred_element_type=F32)[0:1, :]
        pstart = pend - padded
        n_used = jnp.max(pend, axis=-1, keepdims=True) / MOE_BLOCK
        elane = lax.broadcasted_iota(I32, (n_rows, LANES), 1)
        blk = jnp.minimum(lax.broadcasted_iota(I32, (n_rows, 1), 0).astype(F32), n_used - 1.0) * MOE_BLOCK
        ends_before = jnp.where((elane < N_EXPERTS) & (pend <= blk), 1.0, 0.0)
        block_e = jnp.minimum(jnp.sum(ends_before, axis=-1, keepdims=True), N_EXPERTS - 1.0)
        mine = elane.astype(F32) == block_e
        filled = jnp.sum(jnp.where(mine, total - (blk - pstart), 0.0), axis=-1, keepdims=True)
        block_valid = jnp.clip(filled, 0.0, float(MOE_BLOCK))
        tab = jnp.where(elane == 0, block_e,
                        jnp.where(elane == 1, block_valid, jnp.where(elane == 2, n_used, 0.0)))
        tab_ref[...] = tab.astype(I32)
        pstart_ref[...] = jnp.broadcast_to(pstart, pstart_ref.shape).astype(I32)


def _router(h, wr_pad, br_row, n_blocks):
    n, d = h.shape
    row = lambda i: (i, 0)
    col = lambda i: (0, i)
    fixed = lambda i: (0, 0)
    tab_rows = -(-n_blocks // SUBLANES) * SUBLANES
    return pl.pallas_call(
        _router_kernel,
        grid=(n // TM_TOK,),
        in_specs=[pl.BlockSpec((TM_TOK, d), row), pl.BlockSpec((d, LANES), fixed),
                  pl.BlockSpec((1, LANES), fixed)],
        out_specs=[pl.BlockSpec((SUBLANES, TM_TOK), col), pl.BlockSpec((TM_TOK, LANES), row),
                   pl.BlockSpec((SUBLANES, TM_TOK), col), pl.BlockSpec((SUBLANES, LANES), fixed),
                   pl.BlockSpec((tab_rows, LANES), fixed)],
        out_shape=[jax.ShapeDtypeStruct((SUBLANES, n), I32), jax.ShapeDtypeStruct((n, LANES), F32),
                   jax.ShapeDtypeStruct((SUBLANES, n), I32), jax.ShapeDtypeStruct((SUBLANES, LANES), I32),
                   jax.ShapeDtypeStruct((tab_rows, LANES), I32)],
        scratch_shapes=[pltpu.VMEM((SUBLANES, LANES), F32)],
        compiler_params=_cparams(("arbitrary",)),
        name="moe_router",
    )(h, wr_pad, br_row)


def _expert_kernel(be_ref, nu_ref, valid_ref, xb_ref, wup_ref, bup_ref, wdn_ref, bdn_ref, yb_ref,
                   wup_bf_ref, wdn_bf_ref):
    b = pl.program_id(0)

    @pl.when(b < nu_ref[0])
    def _():
        @pl.when((b == 0) | (be_ref[b] != be_ref[jnp.maximum(b - 1, 0)]))
        def _():
            wup_bf_ref[...] = wup_ref[0, 0].astype(BF16)
            wdn_bf_ref[...] = wdn_ref[0, 0].astype(BF16)

        d_exp = wdn_ref.shape[2]

        def run(n_rows):
            x = _load_token_tiles(xb_ref, (), n_rows)
            row = lax.broadcasted_iota(I32, (n_rows, 1), 0)
            x = jnp.where(row < valid_ref[b], x, 0.0)
            hu = jnp.dot(x.astype(BF16), wup_bf_ref[...], preferred_element_type=F32) + bup_ref[0, 0]
            glu = jnp.minimum(hu[:, :d_exp], SWIGLU_LIMIT)
            lin = jnp.clip(hu[:, d_exp:], -SWIGLU_LIMIT, SWIGLU_LIMIT)
            a = glu * _sigmoid(SWIGLU_ALPHA * glu) * (lin + 1.0)
            y = jnp.dot(a.astype(BF16), wdn_bf_ref[...], preferred_element_type=F32) + bdn_ref[0, 0]
            if n_rows < MOE_BLOCK:
                y = jnp.concatenate([y, jnp.zeros((MOE_BLOCK - n_rows, y.shape[1]), F32)], axis=0)
            _store_token_tiles(yb_ref, (), y)

        half = MOE_BLOCK // 2
        pl.when(valid_ref[b] > half)(functools.partial(run, MOE_BLOCK))
        pl.when(valid_ref[b] <= half)(functools.partial(run, half))

    @pl.when(b >= nu_ref[0])
    def _():
        yb_ref[...] = jnp.zeros_like(yb_ref)


def _experts(xb, block_e, n_used, block_valid, layer, w_up, b_up, w_dn, b_dn):
    d = SUBLANES * LANES
    rows = MOE_BLOCK * SUBLANES
    n_blocks = xb.shape[0] // rows
    d_up = w_up.shape[3]
    d_exp = w_dn.shape[2]
    wsel = lambda b, be, nu, va: (layer, be[b], 0, 0)
    return pl.pallas_call(
        _expert_kernel,
        grid_spec=pltpu.PrefetchScalarGridSpec(
            num_scalar_prefetch=3,
            grid=(n_blocks,),
            in_specs=[pl.BlockSpec((rows, LANES), lambda b, be, nu, va: (jnp.minimum(b, nu[0] - 1), 0)),
                      pl.BlockSpec((1, 1, d, d_up), wsel),
                      pl.BlockSpec((1, 1, 1, d_up), wsel),
                      pl.BlockSpec((1, 1, d_exp, d), wsel),
                      pl.BlockSpec((1, 1, 1, d), wsel)],
            out_specs=pl.BlockSpec((rows, LANES), lambda b, be, nu, va: (b, 0)),
            scratch_shapes=[pltpu.VMEM((d, d_up), BF16), pltpu.VMEM((d_exp, d), BF16)]),
        out_shape=jax.ShapeDtypeStruct(xb.shape, F32),
        compiler_params=_cparams(("arbitrary",)),
        name="moe_experts",
    )(block_e, n_used, block_valid, xb, w_up, b_up, w_dn, b_dn)


def _sc_gather_tiles(table, idx):
    m = idx.shape[0]
    mesh = plsc.VectorSubcoreMesh(core_axis_name="core", subcore_axis_name="subcore")

    @functools.partial(pl.kernel, out_type=jax.ShapeDtypeStruct((m, SUBLANES, LANES), table.dtype), mesh=mesh)
    def gather(table_hbm, idx_hbm, out_hbm):
        def window(idx_vmem, out_vmem):
            pltpu.sync_copy(table_hbm.at[idx_vmem.at[0, pl.ds(0, SC_WINDOW)]], out_vmem)

        pltpu.emit_pipeline(
            window,
            grid=(m // SC_WINDOW,),
            in_specs=[pl.BlockSpec((1, LANES), lambda i: (i, 0))],
            out_specs=[pl.BlockSpec((SC_WINDOW, SUBLANES, LANES), lambda i: (i, 0, 0))],
            core_axis_name=("core", "subcore"),
            dimension_semantics=(pltpu.PARALLEL,),
        )(idx_hbm, out_hbm)

    idx_rows = jnp.pad(idx.reshape(m // SC_WINDOW, SC_WINDOW), ((0, 0), (0, LANES - SC_WINDOW)))
    return gather(table, idx_rows)


def _sc_scatter_tiles(tiles, idx_by_choice, n_out):
    n = tiles.shape[0]
    mesh = plsc.VectorSubcoreMesh(core_axis_name="core", subcore_axis_name="subcore")

    @functools.partial(pl.kernel, out_type=jax.ShapeDtypeStruct((n_out, SUBLANES, LANES), tiles.dtype),
                       mesh=mesh)
    def scatter(tiles_hbm, *refs):
        idx_hbm, out_hbm = refs[:TOP_K], refs[TOP_K]

        def window(tiles_vmem, *idx_vmem):
            for k in range(TOP_K):
                pltpu.sync_copy(tiles_vmem, out_hbm.at[idx_vmem[k].at[0, pl.ds(0, SC_WINDOW)]])

        pltpu.emit_pipeline(
            window,
            grid=(n // SC_WINDOW,),
            in_specs=[pl.BlockSpec((SC_WINDOW, SUBLANES, LANES), lambda i: (i, 0, 0))]
                     + [pl.BlockSpec((1, LANES), lambda i: (i, 0))] * TOP_K,
            out_specs=[],
            core_axis_name=("core", "subcore"),
            dimension_semantics=(pltpu.PARALLEL,),
        )(tiles_hbm, *idx_hbm)

    idx_rows = jnp.pad(idx_by_choice.reshape(TOP_K, n // SC_WINDOW, SC_WINDOW),
                       ((0, 0), (0, 0), (0, LANES - SC_WINDOW)))
    return scatter(tiles, *[idx_rows[k] for k in range(TOP_K)])


def _combine_kernel(h_ref, y0_ref, y1_ref, y2_ref, y3_ref, gate_ref, pp_ref, ps_ref, wpg_ref, wpp_ref,
                    g_ref, b_ref, *outs, n_ptiles):
    outp_ref, outs_ref = outs[0], outs[-1]
    tm = h_ref.shape[0]
    gate = gate_ref[...]
    moe = _load_token_tiles(y0_ref, (), tm) * gate[:, 0:1]
    for k, y_ref in enumerate((y1_ref, y2_ref, y3_ref), start=1):
        moe = moe + _load_token_tiles(y_ref, (), tm) * gate[:, k:k + 1]
    h2 = _layer_norm(DN_ALPHA * h_ref[...] + moe, g_ref[...], b_ref[...])
    embed_gate = _sigmoid(_dot(h2, wpg_ref[...]))

    def finish(p_ref, out_ref):
        out_ref[...] = h2 + embed_gate * _dot(p_ref[...], wpp_ref[...])

    is_prompt = pl.program_id(0) < n_ptiles
    pl.when(is_prompt)(functools.partial(finish, pp_ref, outp_ref))
    pl.when(jnp.logical_not(is_prompt))(functools.partial(finish, ps_ref, outs_ref))


def _combine(h, y, gate, p_prompt, p_sample, layer, n_p, split, wpg_bf16, wpp_bf16, g_row, b_row):
    n, d = h.shape
    e = p_prompt.shape[1]
    n_tiles = n // TM_TOK
    n_ptiles = n_p // TM_TOK
    n_stiles = (n - n_p) // TM_TOK
    row = lambda i: (i, 0)
    row_p = lambda i: (jnp.minimum(i, n_ptiles - 1), 0)
    row_s = lambda i: (jnp.maximum(i - n_ptiles, 0), 0)
    fixed = lambda i: (0, 0)
    choice = lambda k: pl.BlockSpec((TM_TOK * SUBLANES, LANES), lambda i: (k * n_tiles + i, 0))
    if split:
        out_specs = [pl.BlockSpec((TM_TOK, d), row_p), pl.BlockSpec((TM_TOK, d), row_s)]
        out_shape = [jax.ShapeDtypeStruct((n_p, d), F32), jax.ShapeDtypeStruct((n - n_p, d), F32)]
    else:
        out_specs = pl.BlockSpec((TM_TOK, d), row)
        out_shape = jax.ShapeDtypeStruct((n, d), F32)
    return pl.pallas_call(
        functools.partial(_combine_kernel, n_ptiles=n_ptiles),
        grid=(n_tiles,),
        in_specs=[pl.BlockSpec((TM_TOK, d), row)] + [choice(k) for k in range(TOP_K)]
                 + [pl.BlockSpec((TM_TOK, LANES), row),
                    pl.BlockSpec((TM_TOK, e), lambda i: (layer * n_ptiles + row_p(i)[0], 0)),
                    pl.BlockSpec((TM_TOK, e), lambda i: (layer * n_stiles + row_s(i)[0], 0)),
                    pl.BlockSpec((d, d), fixed), pl.BlockSpec((e, d), fixed), pl.BlockSpec((1, d), fixed),
                    pl.BlockSpec((1, d), fixed)],
        out_specs=out_specs,
        out_shape=out_shape,
        compiler_params=_cparams(("arbitrary",)),
        name="moe_combine_ln_embed",
    )(h, y, y, y, y, gate, p_prompt, p_sample, wpg_bf16, wpp_bf16, g_row, b_row)


def _layer_tail(h, ht, p_prompt, p_sample, layer, n_p, split, g2, b2, w_r, b_r, w_up, b_up, w_dn, b_dn,
                w_pg, w_pp):
    n, d = h.shape
    wr_pad = jnp.pad(w_r, ((0, 0), (0, LANES - N_EXPERTS)))
    br_row = jnp.pad(b_r, (0, LANES - N_EXPERTS))[None]
    n_asg = n * TOP_K
    n_blocks = n_asg // MOE_BLOCK + N_EXPERTS
    assert n_asg % MOE_BLOCK == 0 and n % SC_WINDOW == 0
    n_slots = n_blocks * MOE_BLOCK
    idx, gate, rank, pstart, tab = _router(h, wr_pad, br_row, n_blocks)
    experts = jnp.arange(N_EXPERTS, dtype=I32)[:, None, None]
    group_start = jnp.sum(jnp.where(idx[None, :TOP_K] == experts, pstart[0, :N_EXPERTS, None, None], 0), axis=0)
    slot_by_choice = group_start + rank[:TOP_K]
    xb = _sc_scatter_tiles(ht.reshape(n, SUBLANES, LANES), slot_by_choice, n_slots)
    yb = _experts(xb.reshape(n_slots * SUBLANES, LANES), tab[:n_blocks, 0], tab[0, 2:3], tab[:n_blocks, 1],
                  layer, w_up, b_up[:, :, None, :], w_dn, b_dn[:, :, None, :])
    y = _sc_gather_tiles(yb.reshape(n_slots, SUBLANES, LANES), slot_by_choice.reshape(-1))
    return _combine(h, y.reshape(n_asg * SUBLANES, LANES), gate, p_prompt, p_sample, layer, n_p, split,
                    w_pg.astype(BF16), w_pp.astype(BF16), g2[None], b2[None])


def _lane_row(v, lane0):
    return jnp.zeros((1, LANES), F32).at[0, lane0:lane0 + v.shape[0]].set(v.astype(F32))


def kernel(x_prompt, x_sample, cache_fox_k, cache_fox_v, cache_fox_logf, state_gdn, state_gdn_conv,
           cache_pool, p_prompt, p_sample, w_in_ab, b_fgate, gdn_a_log, gdn_dt_bias, gdn_conv_w,
           gdn_norm_g, w_out_ab, pool_w, pool_scale, w_out_pool, ln1_g, ln1_b, ln2_g, ln2_b,
           w_router, b_router, w_expert_up, b_expert_up, w_expert_down, b_expert_down,
           w_ple_gate, w_ple_proj):
    n_pb, seq, d = x_prompt.shape
    n_sb, dseq, _ = x_sample.shape
    past = cache_fox_k.shape[2]
    assert n_pb == 1 and dseq == CHUNK and past % dseq == 0 and seq % TQ == 0
    assert d == SUBLANES * LANES
    n_p = n_pb * seq
    n_s = n_sb * dseq
    n = n_p + n_s
    assert n_p % TM_TOK == 0 and n_s % TM_TOK == 0
    n_layers = p_prompt.shape[0]
    pp_all = p_prompt.reshape(n_layers * n_p, -1)
    ps_all = p_sample.reshape(n_layers * n_s, -1)

    def tail(h, ht, i, split):
        return _layer_tail(h, ht, pp_all, ps_all, i, n_p, split, ln2_g[i], ln2_b[i], w_router[i], b_router[i],
                           w_expert_up, b_expert_up, w_expert_down, b_expert_down, w_ple_gate[i],
                           w_ple_proj[i])

    w_in = w_in_ab[0]
    n_small = 3 * N_HEADS
    ff0 = 3 * WIDTH
    gq0 = ff0 + N_HEADS
    ga0 = gq0 + 4 * WIDTH
    w_small = jnp.concatenate([w_in[:, ff0:gq0], w_in[:, ga0:ga0 + 2 * N_HEADS],
                               jnp.zeros((d, LANES - n_small), F32)], axis=1)
    w_all = jnp.concatenate([w_in[:, :ff0], w_in[:, gq0:ga0], w_small], axis=1)
    bf_row = _lane_row(b_fgate[0], LANE_F)
    up = _proj(x_prompt.reshape(n_p, d), w_all.astype(BF16), bf_row, U_COLS, False)
    us = _proj(x_sample.reshape(n_s, d), w_all, bf_row, LANES, True)

    u3 = up[None]
    us3 = us.reshape(n_sb, dseq, U_COLS)
    cq_p, ck_p = _cumsum(u3, 1, n_p, TK, COL_S // LANES)
    lf_s = jnp.concatenate(
        [jnp.pad(cache_fox_logf[0].astype(F32), ((0, 0), (0, 0), (0, LANES - N_HEADS))),
         us3[:, :, COL_S:]], axis=1)
    cq_s, ck_s = _cumsum(lf_s, n_sb, past + dseq, past + dseq, 0)

    of_p = _fox(u3, lambda b, i: (0, i, COL_Q // WIDTH), u3, lambda b, j: (0, j, COL_K // WIDTH),
                u3, lambda b, j: (0, j, COL_V // WIDTH), cq_p, lambda b, i: (0, i, 0),
                ck_p, lambda b, j: (0, 0, j), 1, n_p // TQ, TQ, TK, 0)
    k_all = jnp.concatenate([cache_fox_k[0].reshape(n_sb, past, WIDTH), us3[:, :, COL_K:COL_K + WIDTH]], axis=1)
    v_all = jnp.concatenate([cache_fox_v[0].reshape(n_sb, past, WIDTH), us3[:, :, COL_V:COL_V + WIDTH]], axis=1)
    of_s = _fox(us3, lambda b, i: (b, 0, COL_Q // WIDTH), k_all, lambda b, j: (b, 0, 0),
                v_all, lambda b, j: (b, 0, 0), cq_s, lambda b, i: (b, past // dseq, 0),
                ck_s, lambda b, j: (b, 0, 0), n_sb, 1, dseq, past + dseq, past, hi=True)

    gw = 3 * WIDTH
    conv_w = jnp.pad(gdn_conv_w[0], ((0, SUBLANES - CONV_WIDTH), (0, 0)))
    gdn_args = (conv_w, _lane_row(gdn_a_log[0], LANE_A), _lane_row(gdn_dt_bias[0], LANE_A),
                gdn_norm_g[0][None])
    n_pstep = n_p // GDN_ROWS
    ends = lambda k: (jnp.asarray((np.arange(k) == 0).astype(np.int32)),
                      jnp.asarray((np.arange(k) == k - 1).astype(np.int32)))
    og_p, st_p = _gdn(up.reshape(n_pstep, GDN_ROWS, U_COLS), False, *ends(n_pstep),
                      jnp.zeros((n_pstep,), I32), jnp.zeros((1, SUBLANES, gw), F32), gdn_args[0],
                      jnp.zeros((1, N_HEADS, HEAD_DIM, HEAD_DIM), F32), *gdn_args[1:])
    conv_past = jnp.pad(state_gdn_conv[0].astype(F32), ((0, 0), (SUBLANES - (CONV_WIDTH - 1), 0), (0, 0)))
    ones = jnp.ones((n_sb,), I32)
    og_s, st_s = _gdn(us3, True, ones, ones, jnp.arange(n_sb, dtype=I32), conv_past, gdn_args[0],
                      state_gdn[0].astype(F32), *gdn_args[1:])
    h, ht = _outproj_ln((x_prompt.reshape(n_p, d), of_p.reshape(n_p, WIDTH), og_p.reshape(n_p, WIDTH)),
                        (x_sample.reshape(n_s, d), of_s.reshape(n_s, WIDTH), og_s.reshape(n_s, WIDTH)),
                        w_out_ab[0], ln1_g[0][None], ln1_b[0][None])
    x1 = tail(h, ht, 0, False)

    pool_args = (pool_w[0].astype(BF16), pool_scale[0][None], w_out_pool[0].astype(BF16),
                 ln1_g[1][None], ln1_b[1][None])
    ratio = TM_TOK // POOL_HALO
    x1_halo = x1.reshape(n // POOL_HALO, POOL_HALO, d)
    h_pool = _pool_ln(x1, lambda i: (i, 0), x1_halo, lambda i: (jnp.maximum(i * ratio - 1, 0), 0, 0),
                      n_p // TM_TOK, TM_TOK, 0, True, *pool_args)
    cache16 = jnp.pad(cache_pool[0].astype(F32), ((0, 0), (POOL_HALO - POOL_STATE, 0), (0, 0)))
    h, ht = _pool_ln(x1, lambda i: (n_p // dseq + i, 0), cache16, lambda i: (i, 0, 0),
                     n_sb, dseq, past, False, *pool_args, into=h_pool)
    x2_p, x2_s = tail(h, ht, 1, True)

    return (x2_p.reshape(n_pb, seq, d), x2_s.reshape(n_sb, dseq, d),
            up[:, COL_K:COL_K + WIDTH].reshape(1, n_pb, seq, N_HEADS, HEAD_DIM),
            up[:, COL_V:COL_V + WIDTH].reshape(1, n_pb, seq, N_HEADS, HEAD_DIM),
            up[:, COL_S:COL_S + N_HEADS].reshape(1, n_pb, seq, N_HEADS),
            st_p.reshape(1, n_pb, N_HEADS, HEAD_DIM, HEAD_DIM),
            up[seq - (CONV_WIDTH - 1):, COL_G:COL_G + gw].reshape(1, n_pb, CONV_WIDTH - 1, gw),
            x1[n_p - POOL_STATE:n_p].reshape(1, n_pb, POOL_STATE, d),
            us[:, COL_K:COL_K + WIDTH].reshape(1, n_sb, dseq, N_HEADS, HEAD_DIM),
            us[:, COL_V:COL_V + WIDTH].reshape(1, n_sb, dseq, N_HEADS, HEAD_DIM),
            us[:, COL_S:COL_S + N_HEADS].reshape(1, n_sb, dseq, N_HEADS),
            st_s.reshape(1, n_sb, N_HEADS, HEAD_DIM, HEAD_DIM),
            us[:, COL_G:COL_G + gw].reshape(n_sb, dseq, gw)[:, dseq - (CONV_WIDTH - 1):].reshape(
                1, n_sb, CONV_WIDTH - 1, gw),
            x1[n_p:].reshape(n_sb, dseq, d)[:, dseq - POOL_STATE:].reshape(1, n_sb, POOL_STATE, d))
```

```python
import functools

import numpy as np
import jax
import jax.numpy as jnp
from jax import lax
from jax.experimental import pallas as pl
from jax.experimental.pallas import tpu as pltpu
from jax.experimental.pallas import tpu_sc as plsc

F32 = jnp.float32
BF16 = jnp.bfloat16
I32 = jnp.int32
HIGHEST = lax.Precision.HIGHEST

LANES = 128
SUBLANES = 8
VMEM_LIMIT = 56 * 1024 * 1024

HEAD_DIM = 128
N_HEADS = 4
WIDTH = N_HEADS * HEAD_DIM
CHUNK = 64
CONV_WIDTH = 4
POOL_WINDOWS = (2, 4, 8, 16)
POOL_HALO = 16
POOL_STATE = 15
N_EXPERTS = 32
TOP_K = 4
SWIGLU_LIMIT = 7.0
SWIGLU_ALPHA = 1.702
DEPTH = 2
DN_ALPHA = (2 * DEPTH) ** 0.25
LN_EPS = 1e-5
NORM_EPS = 1e-6
NEG_INF = -1e30
LOG2E = 1.4426950408889634

COL_Q, COL_K, COL_V = 0, WIDTH, 2 * WIDTH
COL_G = 3 * WIDTH
COL_Z = 6 * WIDTH
COL_S = 7 * WIDTH
U_COLS = COL_S + LANES
LANE_F, LANE_A, LANE_B = 0, N_HEADS, 2 * N_HEADS

TM_PROJ = 256
TM_TOK = 512
SC_WINDOW = 32
MOE_BLOCK = 512
TQ = 1024
TK = 512
GDN_ROWS = 256
GDN_CHUNK = CHUNK


def _cparams(sem):
    return pltpu.CompilerParams(dimension_semantics=sem, vmem_limit_bytes=VMEM_LIMIT)


def _softplus(x):
    return jnp.maximum(x, 0.0) + jnp.log1p(jnp.exp(-jnp.abs(x)))


def _sigmoid(x):
    return 1.0 / (1.0 + jnp.exp(-x))


def _silu(x):
    return x * _sigmoid(x)


def _layer_norm(y, g, b):
    mu = jnp.mean(y, axis=-1, keepdims=True)
    yc = y - mu
    var = jnp.mean(yc * yc, axis=-1, keepdims=True)
    return yc * lax.rsqrt(var + LN_EPS) * g + b


def _dot_general(a, b, dims, hi):
    if hi:
        return lax.dot_general(a.astype(F32), b.astype(F32), (dims, ((), ())), precision=HIGHEST,
                               preferred_element_type=F32)
    return lax.dot_general(a.astype(BF16), b.astype(BF16), (dims, ((), ())), preferred_element_type=F32)


def _dot(a, b, hi=False):
    return _dot_general(a, b, ((1,), (0,)), hi)


def _dot_nt(a, b, hi=False):
    return _dot_general(a, b, ((1,), (1,)), hi)


def _dot_tn(a, b, hi=False):
    return _dot_general(a, b, ((0,), (0,)), hi)


def _spread_lanes(x, width):
    if width % LANES == 0:
        return jnp.concatenate([x] * (width // LANES), axis=1)
    return jnp.broadcast_to(x[:, 0:1], (x.shape[0], width))


def _load_token_tiles(ref, lead, n_tok):
    return jnp.concatenate([ref[(*lead, pl.ds(j, n_tok, stride=SUBLANES), slice(None))]
                            for j in range(SUBLANES)], axis=1)


def _store_token_tiles(ref, lead, x):
    for j in range(SUBLANES):
        ref[(*lead, pl.ds(j, x.shape[0], stride=SUBLANES), slice(None))] = x[:, j * LANES:(j + 1) * LANES]


def _lanes_to_rows(x, lane0):
    r = lax.broadcasted_iota(I32, (SUBLANES, LANES), 0)
    c = lax.broadcasted_iota(I32, (SUBLANES, LANES), 1)
    sel = (c == r + lane0).astype(F32)
    return lax.dot_general(sel, x, (((1,), (1,)), ((), ())), precision=HIGHEST,
                           preferred_element_type=F32)


def _proj_kernel(x_ref, w_ref, bf_ref, u_ref, *, hi):
    u = _dot(x_ref[...], w_ref[...], hi)
    u_ref[...] = u

    @pl.when(pl.program_id(1) == pl.num_programs(1) - 1)
    def _():
        small = u[:, u.shape[1] - LANES:]
        lane = lax.broadcasted_iota(I32, small.shape, 1)
        logf = -_softplus(-(small + bf_ref[...]))
        u_ref[:, u.shape[1] - LANES:] = jnp.where(lane < LANE_A, logf, small)


def _proj(x, w, bf_row, tn, hi):
    n, d = x.shape
    m = w.shape[1]
    return pl.pallas_call(
        functools.partial(_proj_kernel, hi=hi),
        grid=(n // TM_PROJ, m // tn),
        in_specs=[pl.BlockSpec((TM_PROJ, d), lambda i, j: (i, 0)),
                  pl.BlockSpec((d, tn), lambda i, j: (0, j)),
                  pl.BlockSpec((1, LANES), lambda i, j: (0, 0))],
        out_specs=pl.BlockSpec((TM_PROJ, tn), lambda i, j: (i, j)),
        out_shape=jax.ShapeDtypeStruct((n, m), F32),
        compiler_params=_cparams(("parallel", "parallel")),
        name="in_proj",
    )(x, w, bf_row)


def _cumsum_kernel(lf_ref, crep_ref, crow_ref, carry_ref):
    @pl.when(pl.program_id(1) == 0)
    def _():
        carry_ref[...] = jnp.zeros_like(carry_ref)

    lf = lf_ref[0]
    t = lf.shape[0]
    r = lax.broadcasted_iota(I32, (t, t), 0)
    c = lax.broadcasted_iota(I32, (t, t), 1)
    tril = (c <= r).astype(F32)
    cs = jnp.dot(tril, lf, precision=HIGHEST, preferred_element_type=F32) + carry_ref[0:1, :]
    carry_ref[...] = jnp.broadcast_to(cs[t - 1:t, :], carry_ref.shape)
    c2 = cs * LOG2E
    crow_ref[0] = _lanes_to_rows(c2, LANE_F)
    for h in range(N_HEADS):
        crep_ref[0, :, h * HEAD_DIM:(h + 1) * HEAD_DIM] = jnp.broadcast_to(
            c2[:, LANE_F + h:LANE_F + h + 1], (t, HEAD_DIM))


def _cumsum(arr, n_batch, length, tl, col_block):
    return pl.pallas_call(
        _cumsum_kernel,
        grid=(n_batch, length // tl),
        in_specs=[pl.BlockSpec((1, tl, LANES), lambda b, j: (b, j, col_block))],
        out_specs=[pl.BlockSpec((1, tl, WIDTH), lambda b, j: (b, j, 0)),
                   pl.BlockSpec((1, SUBLANES, tl), lambda b, j: (b, 0, j))],
        out_shape=[jax.ShapeDtypeStruct((n_batch, length, WIDTH), F32),
                   jax.ShapeDtypeStruct((n_batch, SUBLANES, length), F32)],
        scratch_shapes=[pltpu.VMEM((SUBLANES, LANES), F32)],
        compiler_params=_cparams(("parallel", "arbitrary")),
        name="logf_cumsum",
    )(arr)


def _fox_kernel(qi_ref, kj_ref, last_ref, q_ref, k_ref, v_ref, cq_ref, ck_ref, o_ref,
                m_ref, l_ref, acc_ref, *, tq, tk, past, hi):
    s_idx = pl.program_id(1)
    qi = qi_ref[s_idx]
    kj = kj_ref[s_idx]

    @pl.when(kj == 0)
    def _():
        m_ref[...] = jnp.full_like(m_ref, NEG_INF)
        l_ref[...] = jnp.zeros_like(l_ref)
        acc_ref[...] = jnp.zeros_like(acc_ref)

    def update(masked):
        if masked:
            q_pos = past + qi * tq + lax.broadcasted_iota(I32, (tq, tk), 0)
            k_pos = kj * tk + lax.broadcasted_iota(I32, (tq, tk), 1)
            visible = k_pos <= q_pos
        for h in range(N_HEADS):
            cols = slice(h * HEAD_DIM, (h + 1) * HEAD_DIM)
            q = q_ref[0, :, cols] * (HEAD_DIM ** -0.5 * LOG2E)
            t = _dot_nt(q, k_ref[0, :, cols], hi) - ck_ref[0, h:h + 1, :]
            if masked:
                t = jnp.where(visible, t, NEG_INF)
            cq = cq_ref[0, :, cols]
            m_prev = m_ref[h]
            m_new = jnp.maximum(m_prev, jnp.max(t, axis=-1, keepdims=True) + cq)
            p = jnp.exp2(t - _spread_lanes(m_new - cq, tk))
            alpha = jnp.exp2(m_prev - m_new)
            l_ref[h] = alpha * l_ref[h] + jnp.sum(p, axis=-1, keepdims=True)
            acc_ref[:, cols] = alpha * acc_ref[:, cols] + _dot(p, v_ref[0, :, cols], hi)
            m_ref[h] = m_new

    crosses_diagonal = kj * tk + (tk - 1) > past + qi * tq
    pl.when(crosses_diagonal)(functools.partial(update, True))
    pl.when(jnp.logical_not(crosses_diagonal))(functools.partial(update, False))

    @pl.when(last_ref[s_idx] == 1)
    def _():
        for h in range(N_HEADS):
            cols = slice(h * HEAD_DIM, (h + 1) * HEAD_DIM)
            o_ref[0, :, cols] = acc_ref[:, cols] / l_ref[h]


def _fox_schedule(n_q, tq, tk, past):
    qi, kj, last = [], [], []
    for i in range(n_q):
        hi = (past + (i + 1) * tq - 1) // tk
        for j in range(hi + 1):
            qi.append(i)
            kj.append(j)
            last.append(1 if j == hi else 0)
    return (jnp.asarray(np.array(qi, np.int32)), jnp.asarray(np.array(kj, np.int32)),
            jnp.asarray(np.array(last, np.int32)))


def _fox(q_arr, q_map, k_arr, k_map, v_arr, v_map, cq_arr, cq_map, ck_arr, ck_map,
         n_batch, n_q, tq, tk, past, hi=False):
    qi, kj, last = _fox_schedule(n_q, tq, tk, past)
    n_steps = int(qi.shape[0])
    spec = lambda shape, fn, tab: pl.BlockSpec(shape, lambda b, s, qi_r, kj_r, la_r: fn(b, (qi_r if tab == 'q' else kj_r)[s]))
    return pl.pallas_call(
        functools.partial(_fox_kernel, tq=tq, tk=tk, past=past, hi=hi),
        grid_spec=pltpu.PrefetchScalarGridSpec(
            num_scalar_prefetch=3,
            grid=(n_batch, n_steps),
            in_specs=[spec((1, tq, WIDTH), q_map, 'q'),
                      spec((1, tk, WIDTH), k_map, 'k'),
                      spec((1, tk, WIDTH), v_map, 'k'),
                      spec((1, tq, WIDTH), cq_map, 'q'),
                      spec((1, SUBLANES, tk), ck_map, 'k')],
            out_specs=spec((1, tq, WIDTH), lambda b, i: (b, i, 0), 'q'),
            scratch_shapes=[pltpu.VMEM((N_HEADS, tq, HEAD_DIM), F32),
                            pltpu.VMEM((N_HEADS, tq, HEAD_DIM), F32),
                            pltpu.VMEM((tq, WIDTH), F32)]),
        out_shape=jax.ShapeDtypeStruct((n_batch, n_q * tq, WIDTH), F32),
        compiler_params=_cparams(("parallel", "arbitrary")),
        name="fox_attention",
    )(qi, kj, last, q_arr, k_arr, v_arr, cq_arr, ck_arr)


def _gdn_kernel(first_ref, last_ref, seq_ref,
                pre_ref, z_ref, sm_ref, cpast_ref, convw_ref, s0_ref, alog_ref, dtb_ref, ng_ref,
                o_ref, sout_ref, stage_ref, s_ref, *, rows, chunk, hi):
    dot, dot_nt, dot_tn = (functools.partial(f, hi=hi) for f in (_dot, _dot_nt, _dot_tn))
    step = pl.program_id(0)
    halo = SUBLANES
    n_chunks = rows // chunk

    @pl.when(first_ref[step] == 1)
    def _():
        stage_ref[0:halo, :] = cpast_ref[0]
        s_ref[...] = s0_ref[0]

    stage_ref[halo:halo + rows, :] = pre_ref[0]
    conv = stage_ref[halo:halo + rows, :] * convw_ref[CONV_WIDTH - 1:CONV_WIDTH, :]
    for j in range(1, CONV_WIDTH):
        conv = conv + (stage_ref[halo - j:halo - j + rows, :]
                       * convw_ref[CONV_WIDTH - 1 - j:CONV_WIDTH - j, :])
    stage_ref[0:halo, :] = stage_ref[rows:rows + halo, :]
    act = _silu(conv)

    small = sm_ref[0]
    beta_all = _sigmoid(small)
    g_all = -jnp.exp(alog_ref[...]) * _softplus(small + dtb_ref[...])
    r = lax.broadcasted_iota(I32, (rows, rows), 0)
    c = lax.broadcasted_iota(I32, (rows, rows), 1)
    same_chunk = (r // chunk) == (c // chunk)
    incl = same_chunk & (c <= r)
    strict = same_chunk & (c < r)
    eye = (c == r).astype(F32)
    gc_all = jnp.dot(incl.astype(F32), g_all, precision=HIGHEST, preferred_element_type=F32)
    gc_rows = _lanes_to_rows(gc_all, LANE_A)

    heads = range(N_HEADS)
    head_cols = [slice(h * HEAD_DIM, (h + 1) * HEAD_DIM) for h in heads]
    q, k, gc, decay, kb, vb, low = [], [], [], [], [], [], []
    for h in heads:
        qh = act[:, h * HEAD_DIM:(h + 1) * HEAD_DIM]
        kh = act[:, WIDTH + h * HEAD_DIM:WIDTH + (h + 1) * HEAD_DIM]
        vh = act[:, 2 * WIDTH + h * HEAD_DIM:2 * WIDTH + (h + 1) * HEAD_DIM]
        q.append(qh * lax.rsqrt(jnp.sum(qh * qh, axis=-1, keepdims=True) + NORM_EPS) * (HEAD_DIM ** -0.5))
        k.append(kh * lax.rsqrt(jnp.sum(kh * kh, axis=-1, keepdims=True) + NORM_EPS))
        beta = beta_all[:, LANE_B + h:LANE_B + h + 1]
        gc.append(gc_all[:, LANE_A + h:LANE_A + h + 1])
        diff = gc[h] - gc_rows[h:h + 1, :]
        decay.append(jnp.where(incl, jnp.exp(jnp.where(incl, diff, 0.0)), 0.0))
        kb.append(k[h] * beta)
        vb.append(vh * beta)
    for h in heads:
        low.append(jnp.where(strict, dot_nt(kb[h], k[h]) * decay[h], 0.0))
    inv = [eye - low[h] for h in heads]
    pw = [dot(low[h], low[h]) for h in heads]
    n_sq = chunk.bit_length() - 2
    for it in range(n_sq):
        inv = [inv[h] + dot(inv[h], pw[h]) for h in heads]
        if it + 1 < n_sq:
            pw = [dot(pw[h], pw[h]) for h in heads]
    egc = [jnp.exp(gc[h]) for h in heads]
    uw = [dot(inv[h], jnp.concatenate([vb[h], kb[h] * egc[h]], axis=1)) for h in heads]
    intra = [jnp.where(incl, dot_nt(q[h], k[h]) * decay[h], 0.0) for h in heads]
    qd = [q[h] * egc[h] for h in heads]
    g_last = [[gc[h][(g + 1) * chunk - 1:(g + 1) * chunk, :] for g in range(n_chunks)] for h in heads]
    kd = [k[h] * jnp.exp(jnp.concatenate([jnp.broadcast_to(gl, (chunk, 1)) for gl in g_last[h]], axis=0)
                         - gc[h]) for h in heads]
    state = [s_ref[h] for h in heads]
    v_new = [[] for _ in heads]
    for g in range(n_chunks):
        rs = slice(g * chunk, (g + 1) * chunk)
        for h in heads:
            v_new[h].append(uw[h][rs, :HEAD_DIM] - dot(uw[h][rs, HEAD_DIM:], state[h]))
        for h in heads:
            v_rows = jnp.concatenate(
                v_new[h] + [jnp.zeros((rows - (g + 1) * chunk, HEAD_DIM), F32)] * (g + 1 < n_chunks), axis=0)
            o = dot(qd[h][rs, :], state[h]) + dot(intra[h][rs, :], v_rows)
            state[h] = state[h] * jnp.exp(g_last[h][g]) + dot_tn(kd[h][rs, :], v_new[h][g])
            o = (o * lax.rsqrt(jnp.mean(o * o, axis=-1, keepdims=True) + NORM_EPS)
                 * ng_ref[...] * _silu(z_ref[0, rs, head_cols[h]]))
            o_ref[0, rs, head_cols[h]] = o
    for h in heads:
        s_ref[h] = state[h]

    @pl.when(last_ref[step] == 1)
    def _():
        sout_ref[0] = s_ref[...]


def _gdn(u_view, hi, first, last, seq, conv_past, conv_w, s0, alog_row, dtb_row, ng_row):
    n_steps, rows, _ = u_view.shape
    n_seq = s0.shape[0]
    gw = 3 * WIDTH
    return pl.pallas_call(
        functools.partial(_gdn_kernel, rows=rows, chunk=min(rows, GDN_CHUNK), hi=hi),
        grid_spec=pltpu.PrefetchScalarGridSpec(
            num_scalar_prefetch=3,
            grid=(n_steps,),
            in_specs=[pl.BlockSpec((1, rows, gw), lambda s, f, l, q: (s, 0, COL_G // gw)),
                      pl.BlockSpec((1, rows, WIDTH), lambda s, f, l, q: (s, 0, COL_Z // WIDTH)),
                      pl.BlockSpec((1, rows, LANES), lambda s, f, l, q: (s, 0, COL_S // LANES)),
                      pl.BlockSpec((1, SUBLANES, gw), lambda s, f, l, q: (q[s], 0, 0)),
                      pl.BlockSpec((SUBLANES, gw), lambda s, f, l, q: (0, 0)),
                      pl.BlockSpec((1, N_HEADS, HEAD_DIM, HEAD_DIM), lambda s, f, l, q: (q[s], 0, 0, 0)),
                      pl.BlockSpec((1, LANES), lambda s, f, l, q: (0, 0)),
                      pl.BlockSpec((1, LANES), lambda s, f, l, q: (0, 0)),
                      pl.BlockSpec((1, LANES), lambda s, f, l, q: (0, 0))],
            out_specs=[pl.BlockSpec((1, rows, WIDTH), lambda s, f, l, q: (s, 0, 0)),
                       pl.BlockSpec((1, N_HEADS, HEAD_DIM, HEAD_DIM), lambda s, f, l, q: (q[s], 0, 0, 0))],
            scratch_shapes=[pltpu.VMEM((rows + SUBLANES, gw), F32),
                            pltpu.VMEM((N_HEADS, HEAD_DIM, HEAD_DIM), F32)]),
        out_shape=[jax.ShapeDtypeStruct((n_steps, rows, WIDTH), F32),
                   jax.ShapeDtypeStruct((n_seq, N_HEADS, HEAD_DIM, HEAD_DIM), F32)],
        compiler_params=_cparams(("arbitrary",)),
        name="gated_deltanet",
    )(first, last, seq, u_view, u_view, u_view, conv_past, conv_w, s0, alog_row, dtb_row, ng_row)


def _outproj_ln_kernel(xp_ref, ofp_ref, ogp_ref, xs_ref, ofs_ref, ogs_ref, w_ref, wf_ref, g_ref, b_ref,
                       h_ref, ht_ref, *, hi_from):
    def run(hi, x_ref, of_ref, og_ref):
        w = wf_ref if hi else w_ref
        mix = _dot(of_ref[...], w[0:WIDTH, :], hi) + _dot(og_ref[...], w[WIDTH:2 * WIDTH, :], hi)
        h = _layer_norm(DN_ALPHA * x_ref[...] + mix, g_ref[...], b_ref[...])
        h_ref[...] = h
        _store_token_tiles(ht_ref, (), h)

    pl.when(pl.program_id(0) < hi_from)(functools.partial(run, False, xp_ref, ofp_ref, ogp_ref))
    pl.when(pl.program_id(0) >= hi_from)(functools.partial(run, True, xs_ref, ofs_ref, ogs_ref))


def _outproj_ln(prompt, sample, w, g_row, b_row):
    n_p, d = prompt[0].shape
    n = n_p + sample[0].shape[0]
    hi_from = n_p // TM_TOK
    row = lambda i: (i, 0)
    row_p = lambda i: (jnp.minimum(i, hi_from - 1), 0)
    row_s = lambda i: (jnp.maximum(i - hi_from, 0), 0)
    fixed = lambda i: (0, 0)
    group = lambda rows: [pl.BlockSpec((TM_TOK, d), rows), pl.BlockSpec((TM_TOK, WIDTH), rows),
                          pl.BlockSpec((TM_TOK, WIDTH), rows)]
    return pl.pallas_call(
        functools.partial(_outproj_ln_kernel, hi_from=hi_from),
        grid=(n // TM_TOK,),
        in_specs=group(row_p) + group(row_s) + [pl.BlockSpec((2 * WIDTH, d), fixed),
                                                pl.BlockSpec((2 * WIDTH, d), fixed),
                                                pl.BlockSpec((1, d), fixed), pl.BlockSpec((1, d), fixed)],
        out_specs=[pl.BlockSpec((TM_TOK, d), row), pl.BlockSpec((TM_TOK * SUBLANES, LANES), row)],
        out_shape=[jax.ShapeDtypeStruct((n, d), F32), jax.ShapeDtypeStruct((n * SUBLANES, LANES), F32)],
        compiler_params=_cparams(("parallel",)),
        name="out_proj_ln",
    )(*prompt, *sample, w.astype(BF16), w, g_row, b_row)


def _pool_ln_kernel(x_ref, halo_ref, pw_ref, ps_ref, w_ref, g_ref, b_ref, h_ref, ht_ref, stage_ref,
                    *, tm, pos0, zero_first_halo):
    i = pl.program_id(0)
    stage_ref[0:POOL_HALO, :] = halo_ref[0]
    if zero_first_halo:
        @pl.when(i == 0)
        def _():
            stage_ref[0:POOL_HALO, :] = jnp.zeros((POOL_HALO, stage_ref.shape[1]), F32)
    x = x_ref[...]
    stage_ref[POOL_HALO:POOL_HALO + tm, :] = x
    gdim = x.shape[1] // len(POOL_WINDOWS)
    pos = pos0 + lax.broadcasted_iota(I32, (tm, 1), 0)
    if zero_first_halo:
        pos = pos + i * tm
    parts = []
    for gi, win in enumerate(POOL_WINDOWS):
        cols = slice(gi * gdim, (gi + 1) * gdim)
        s = stage_ref[POOL_HALO:POOL_HALO + tm, cols]
        for j in range(1, win):
            s = s + stage_ref[POOL_HALO - j:POOL_HALO - j + tm, cols]
        cnt = jnp.minimum(pos + 1, win).astype(F32)
        zg = s / cnt - x[:, cols]
        parts.append(_dot(zg, pw_ref[gi]))
    zg = jnp.concatenate(parts, axis=-1) * ps_ref[...]
    mix = _dot(zg, w_ref[...])
    h = _layer_norm(DN_ALPHA * x + mix, g_ref[...], b_ref[...])
    h_ref[...] = h
    _store_token_tiles(ht_ref, (), h)


def _pool_ln_into_kernel(*refs, **kw):
    _pool_ln_kernel(*refs[:7], *refs[9:], **kw)


def _pool_ln(x, x_map, halo_arr, halo_map, n_tiles, tm, pos0, zero_first_halo,
             pw_bf16, ps_row, w_bf16, g_row, b_row, into=None):
    n, d = x.shape
    gdim = d // len(POOL_WINDOWS)
    fixed = lambda i: (0, 0)
    in_specs = [pl.BlockSpec((tm, d), x_map),
                pl.BlockSpec((1, POOL_HALO, d), halo_map),
                pl.BlockSpec((len(POOL_WINDOWS), gdim, gdim), lambda i: (0, 0, 0)),
                pl.BlockSpec((1, d), fixed), pl.BlockSpec((d, d), fixed),
                pl.BlockSpec((1, d), fixed), pl.BlockSpec((1, d), fixed)]
    args = (x, halo_arr, pw_bf16, ps_row, w_bf16, g_row, b_row)
    kw = dict(tm=tm, pos0=pos0, zero_first_halo=zero_first_halo)
    return pl.pallas_call(
        functools.partial(_pool_ln_kernel if into is None else _pool_ln_into_kernel, **kw),
        grid=(n_tiles,),
        in_specs=in_specs + ([] if into is None else [pl.BlockSpec(memory_space=pl.ANY)] * 2),
        out_specs=[pl.BlockSpec((tm, d), x_map),
                   pl.BlockSpec((tm * SUBLANES, LANES), x_map)],
        out_shape=[jax.ShapeDtypeStruct((n, d), F32), jax.ShapeDtypeStruct((n * SUBLANES, LANES), F32)],
        input_output_aliases={} if into is None else {7: 0, 8: 1},
        scratch_shapes=[pltpu.VMEM((POOL_HALO + tm, d), F32)],
        compiler_params=_cparams(("arbitrary",)),
        name="pool_mixer_ln",
    )(*args, *(() if into is None else into))


def _router_kernel(h_ref, wr_ref, br_ref, idx_ref, gate_ref, rank_ref, pstart_ref, tab_ref, carry_ref):
    @pl.when(pl.program_id(0) == 0)
    def _():
        carry_ref[...] = jnp.zeros_like(carry_ref)

    tm = h_ref.shape[0]
    lane = lax.broadcasted_iota(I32, (tm, LANES), 1).astype(F32)
    logits = jnp.dot(h_ref[...], wr_ref[...], precision=HIGHEST, preferred_element_type=F32)
    work = jnp.where(lane < N_EXPERTS, logits + br_ref[...], -jnp.inf)
    vals, ids = [], []
    for _ in range(TOP_K):
        m = jnp.max(work, axis=-1, keepdims=True)
        ik = jnp.min(jnp.where(work == m, lane, float(LANES)), axis=-1, keepdims=True)
        vals.append(m)
        ids.append(ik)
        work = jnp.where(lane == ik, -jnp.inf, work)
    exps = [jnp.exp(v - vals[0]) for v in vals]
    denom = exps[0]
    for e in exps[1:]:
        denom = denom + e
    multihot = jnp.zeros((tm, LANES), F32)
    idx_out = jnp.zeros((tm, LANES), F32)
    gate_out = jnp.zeros((tm, LANES), F32)
    for k in range(TOP_K):
        multihot = multihot + (lane == ids[k]).astype(F32)
        idx_out = jnp.where(lane == k, ids[k], idx_out)
        gate_out = jnp.where(lane == k, exps[k] / denom, gate_out)
    r = lax.broadcasted_iota(I32, (tm, tm), 0)
    c = lax.broadcasted_iota(I32, (tm, tm), 1)
    before = _dot((c < r).astype(F32), multihot) + carry_ref[0:1, :]
    rank_out = jnp.zeros((tm, LANES), F32)
    for k in range(TOP_K):
        rk = jnp.sum(jnp.where(lane == ids[k], before, 0.0), axis=-1, keepdims=True)
        rank_out = jnp.where(lane == k, rk, rank_out)
    idx_ref[...] = jnp.transpose(idx_out)[:SUBLANES].astype(I32)
    rank_ref[...] = jnp.transpose(rank_out)[:SUBLANES].astype(I32)
    gate_ref[...] = gate_out
    total = carry_ref[0:1, :] + jnp.sum(multihot, axis=0, keepdims=True)
    carry_ref[...] = jnp.broadcast_to(total, carry_ref.shape)

    @pl.when(pl.program_id(0) == pl.num_programs(0) - 1)
    def _():
        n_rows = tab_ref.shape[0]
        padded = jnp.floor((total + (MOE_BLOCK - 1)) / MOE_BLOCK) * MOE_BLOCK
        rr = lax.broadcasted_iota(I32, (LANES, LANES), 0)
        cc = lax.broadcasted_iota(I32, (LANES, LANES), 1)
        pend = jnp.dot(jnp.broadcast_to(padded, (SUBLANES, LANES)), (rr <= cc).astype(F32),
                       precision=HIGHEST, preferred_element_type=F32)[0:1, :]
        pstart = pend - padded
        n_used = jnp.max(pend, axis=-1, keepdims=True) / MOE_BLOCK
        elane = lax.broadcasted_iota(I32, (n_rows, LANES), 1)
        blk = jnp.minimum(lax.broadcasted_iota(I32, (n_rows, 1), 0).astype(F32), n_used - 1.0) * MOE_BLOCK
        ends_before = jnp.where((elane < N_EXPERTS) & (pend <= blk), 1.0, 0.0)
        block_e = jnp.minimum(jnp.sum(ends_before, axis=-1, keepdims=True), N_EXPERTS - 1.0)
        mine = elane.astype(F32) == block_e
        filled = jnp.sum(jnp.where(mine, total - (blk - pstart), 0.0), axis=-1, keepdims=True)
        block_valid = jnp.clip(filled, 0.0, float(MOE_BLOCK))
        group_end = jnp.sum(jnp.where(mine, pend, 0.0), axis=-1, keepdims=True)
        next_e = jnp.sum(jnp.where((elane < N_EXPERTS) & (pend <= group_end), 1.0, 0.0), axis=-1, keepdims=True)
        tab = jnp.where(elane == 0, block_e, jnp.where(elane == 1, block_valid,
                                                       jnp.where(elane == 2, n_used,
                                                                 jnp.where(elane == 3, next_e, 0.0))))
        tab_ref[...] = tab.astype(I32)
        pstart_ref[...] = jnp.broadcast_to(pstart, pstart_ref.shape).astype(I32)


def _router(h, wr_pad, br_row, n_blocks):
    n, d = h.shape
    row = lambda i: (i, 0)
    col = lambda i: (0, i)
    fixed = lambda i: (0, 0)
    tab_rows = -(-n_blocks // SUBLANES) * SUBLANES
    return pl.pallas_call(
        _router_kernel,
        grid=(n // TM_TOK,),
        in_specs=[pl.BlockSpec((TM_TOK, d), row), pl.BlockSpec((d, LANES), fixed),
                  pl.BlockSpec((1, LANES), fixed)],
        out_specs=[pl.BlockSpec((SUBLANES, TM_TOK), col), pl.BlockSpec((TM_TOK, LANES), row),
                   pl.BlockSpec((SUBLANES, TM_TOK), col), pl.BlockSpec((SUBLANES, LANES), fixed),
                   pl.BlockSpec((tab_rows, LANES), fixed)],
        out_shape=[jax.ShapeDtypeStruct((SUBLANES, n), I32), jax.ShapeDtypeStruct((n, LANES), F32),
                   jax.ShapeDtypeStruct((SUBLANES, n), I32), jax.ShapeDtypeStruct((SUBLANES, LANES), I32),
                   jax.ShapeDtypeStruct((tab_rows, LANES), I32)],
        scratch_shapes=[pltpu.VMEM((SUBLANES, LANES), F32)],
        compiler_params=_cparams(("arbitrary",)),
        name="moe_router",
    )(h, wr_pad, br_row)


def _expert_kernel(be_ref, nu_ref, valid_ref, next_ref, xb_ref, wup_ref, bup_ref, wdn_ref, bdn_ref, yb_ref,
                   wup_f32_ref, wdn_f32_ref, wup_bf_ref, wdn_bf_ref, wsem, *, layer):
    b = pl.program_id(0)

    def fetch(e):
        return (pltpu.make_async_copy(wup_ref.at[layer, e], wup_f32_ref, wsem.at[0]),
                pltpu.make_async_copy(wdn_ref.at[layer, e], wdn_f32_ref, wsem.at[1]))

    @pl.when(b == 0)
    def _():
        for copy in fetch(be_ref[0]):
            copy.start()

    @pl.when(b < nu_ref[0])
    def _():
        @pl.when((b == 0) | (be_ref[b] != be_ref[jnp.maximum(b - 1, 0)]))
        def _():
            for copy in fetch(be_ref[b]):
                copy.wait()
            wup_bf_ref[...] = wup_f32_ref[...].astype(BF16)
            wdn_bf_ref[...] = wdn_f32_ref[...].astype(BF16)

            @pl.when(next_ref[b] < N_EXPERTS)
            def _():
                for copy in fetch(next_ref[b]):
                    copy.start()

        d_exp = wdn_f32_ref.shape[0]

        def run(n_rows):
            x = _load_token_tiles(xb_ref, (), n_rows)
            row = lax.broadcasted_iota(I32, (n_rows, 1), 0)
            x = jnp.where(row < valid_ref[b], x, 0.0)
            hu = jnp.dot(x.astype(BF16), wup_bf_ref[...], preferred_element_type=F32) + bup_ref[0, 0]
            glu = jnp.minimum(hu[:, :d_exp], SWIGLU_LIMIT)
            lin = jnp.clip(hu[:, d_exp:], -SWIGLU_LIMIT, SWIGLU_LIMIT)
            a = glu * _sigmoid(SWIGLU_ALPHA * glu) * (lin + 1.0)
            y = jnp.dot(a.astype(BF16), wdn_bf_ref[...], preferred_element_type=F32) + bdn_ref[0, 0]
            if n_rows < MOE_BLOCK:
                y = jnp.concatenate([y, jnp.zeros((MOE_BLOCK - n_rows, y.shape[1]), F32)], axis=0)
            _store_token_tiles(yb_ref, (), y)

        half = MOE_BLOCK // 2
        pl.when(valid_ref[b] > half)(functools.partial(run, MOE_BLOCK))
        pl.when(valid_ref[b] <= half)(functools.partial(run, half))

    @pl.when(b >= nu_ref[0])
    def _():
        yb_ref[...] = jnp.zeros_like(yb_ref)


def _experts(xb, block_e, n_used, block_valid, next_e, layer, w_up, b_up, w_dn, b_dn):
    d = SUBLANES * LANES
    rows = MOE_BLOCK * SUBLANES
    n_blocks = xb.shape[0] // rows
    d_up = w_up.shape[3]
    d_exp = w_dn.shape[2]
    bsel = lambda b, be, nu, va, ne: (layer, be[b], 0, 0)
    return pl.pallas_call(
        functools.partial(_expert_kernel, layer=layer),
        grid_spec=pltpu.PrefetchScalarGridSpec(
            num_scalar_prefetch=4,
            grid=(n_blocks,),
            in_specs=[pl.BlockSpec((rows, LANES), lambda b, be, nu, va, ne: (jnp.minimum(b, nu[0] - 1), 0)),
                      pl.BlockSpec(memory_space=pl.ANY),
                      pl.BlockSpec((1, 1, 1, d_up), bsel),
                      pl.BlockSpec(memory_space=pl.ANY),
                      pl.BlockSpec((1, 1, 1, d), bsel)],
            out_specs=pl.BlockSpec((rows, LANES), lambda b, be, nu, va, ne: (b, 0)),
            scratch_shapes=[pltpu.VMEM((d, d_up), F32), pltpu.VMEM((d_exp, d), F32),
                            pltpu.VMEM((d, d_up), BF16), pltpu.VMEM((d_exp, d), BF16),
                            pltpu.SemaphoreType.DMA((2,))]),
        out_shape=jax.ShapeDtypeStruct(xb.shape, F32),
        compiler_params=_cparams(("arbitrary",)),
        name="moe_experts",
    )(block_e, n_used, block_valid, next_e, xb, w_up, b_up, w_dn, b_dn)


def _sc_gather_tiles(table, idx):
    m = idx.shape[0]
    mesh = plsc.VectorSubcoreMesh(core_axis_name="core", subcore_axis_name="subcore")

    @functools.partial(pl.kernel, out_type=jax.ShapeDtypeStruct((m, SUBLANES, LANES), table.dtype), mesh=mesh)
    def gather(table_hbm, idx_hbm, out_hbm):
        def window(idx_vmem, out_vmem):
            pltpu.sync_copy(table_hbm.at[idx_vmem.at[0, pl.ds(0, SC_WINDOW)]], out_vmem)

        pltpu.emit_pipeline(
            window,
            grid=(m // SC_WINDOW,),
            in_specs=[pl.BlockSpec((1, LANES), lambda i: (i, 0))],
            out_specs=[pl.BlockSpec((SC_WINDOW, SUBLANES, LANES), lambda i: (i, 0, 0))],
            core_axis_name=("core", "subcore"),
            dimension_semantics=(pltpu.PARALLEL,),
        )(idx_hbm, out_hbm)

    idx_rows = jnp.pad(idx.reshape(m // SC_WINDOW, SC_WINDOW), ((0, 0), (0, LANES - SC_WINDOW)))
    return gather(table, idx_rows)


def _sc_scatter_tiles(tiles, idx_by_choice, n_out):
    n = tiles.shape[0]
    mesh = plsc.VectorSubcoreMesh(core_axis_name="core", subcore_axis_name="subcore")

    @functools.partial(pl.kernel, out_type=jax.ShapeDtypeStruct((n_out, SUBLANES, LANES), tiles.dtype),
                       mesh=mesh)
    def scatter(tiles_hbm, *refs):
        idx_hbm, out_hbm = refs[:TOP_K], refs[TOP_K]

        def window(tiles_vmem, *idx_vmem):
            for k in range(TOP_K):
                pltpu.sync_copy(tiles_vmem, out_hbm.at[idx_vmem[k].at[0, pl.ds(0, SC_WINDOW)]])

        pltpu.emit_pipeline(
            window,
            grid=(n // SC_WINDOW,),
            in_specs=[pl.BlockSpec((SC_WINDOW, SUBLANES, LANES), lambda i: (i, 0, 0))]
                     + [pl.BlockSpec((1, LANES), lambda i: (i, 0))] * TOP_K,
            out_specs=[],
            core_axis_name=("core", "subcore"),
            dimension_semantics=(pltpu.PARALLEL,),
        )(tiles_hbm, *idx_hbm)

    idx_rows = jnp.pad(idx_by_choice.reshape(TOP_K, n // SC_WINDOW, SC_WINDOW),
                       ((0, 0), (0, 0), (0, LANES - SC_WINDOW)))
    return scatter(tiles, *[idx_rows[k] for k in range(TOP_K)])


def _combine_kernel(h_ref, y0_ref, y1_ref, y2_ref, y3_ref, gate_ref, pp_ref, ps_ref, wpg_ref, wpp_ref,
                    g_ref, b_ref, *outs, n_ptiles):
    outp_ref, outs_ref = outs[0], outs[-1]
    tm = h_ref.shape[0]
    gate = gate_ref[...]
    moe = _load_token_tiles(y0_ref, (), tm) * gate[:, 0:1]
    for k, y_ref in enumerate((y1_ref, y2_ref, y3_ref), start=1):
        moe = moe + _load_token_tiles(y_ref, (), tm) * gate[:, k:k + 1]
    h2 = _layer_norm(DN_ALPHA * h_ref[...] + moe, g_ref[...], b_ref[...])
    embed_gate = _sigmoid(_dot(h2, wpg_ref[...]))

    def finish(p_ref, out_ref):
        out_ref[...] = h2 + embed_gate * _dot(p_ref[...], wpp_ref[...])

    is_prompt = pl.program_id(0) < n_ptiles
    pl.when(is_prompt)(functools.partial(finish, pp_ref, outp_ref))
    pl.when(jnp.logical_not(is_prompt))(functools.partial(finish, ps_ref, outs_ref))


def _combine(h, y, gate, p_prompt, p_sample, layer, n_p, split, wpg_bf16, wpp_bf16, g_row, b_row):
    n, d = h.shape
    e = p_prompt.shape[1]
    n_tiles = n // TM_TOK
    n_ptiles = n_p // TM_TOK
    n_stiles = (n - n_p) // TM_TOK
    row = lambda i: (i, 0)
    row_p = lambda i: (jnp.minimum(i, n_ptiles - 1), 0)
    row_s = lambda i: (jnp.maximum(i - n_ptiles, 0), 0)
    fixed = lambda i: (0, 0)
    choice = lambda k: pl.BlockSpec((TM_TOK * SUBLANES, LANES), lambda i: (k * n_tiles + i, 0))
    if split:
        out_specs = [pl.BlockSpec((TM_TOK, d), row_p), pl.BlockSpec((TM_TOK, d), row_s)]
        out_shape = [jax.ShapeDtypeStruct((n_p, d), F32), jax.ShapeDtypeStruct((n - n_p, d), F32)]
    else:
        out_specs = pl.BlockSpec((TM_TOK, d), row)
        out_shape = jax.ShapeDtypeStruct((n, d), F32)
    return pl.pallas_call(
        functools.partial(_combine_kernel, n_ptiles=n_ptiles),
        grid=(n_tiles,),
        in_specs=[pl.BlockSpec((TM_TOK, d), row)] + [choice(k) for k in range(TOP_K)]
                 + [pl.BlockSpec((TM_TOK, LANES), row),
                    pl.BlockSpec((TM_TOK, e), lambda i: (layer * n_ptiles + row_p(i)[0], 0)),
                    pl.BlockSpec((TM_TOK, e), lambda i: (layer * n_stiles + row_s(i)[0], 0)),
                    pl.BlockSpec((d, d), fixed), pl.BlockSpec((e, d), fixed), pl.BlockSpec((1, d), fixed),
                    pl.BlockSpec((1, d), fixed)],
        out_specs=out_specs,
        out_shape=out_shape,
        compiler_params=_cparams(("arbitrary",)),
        name="moe_combine_ln_embed",
    )(h, y, y, y, y, gate, p_prompt, p_sample, wpg_bf16, wpp_bf16, g_row, b_row)


def _layer_tail(h, ht, p_prompt, p_sample, layer, n_p, split, g2, b2, w_r, b_r, w_up, b_up, w_dn, b_dn,
                w_pg, w_pp):
    n, d = h.shape
    wr_pad = jnp.pad(w_r, ((0, 0), (0, LANES - N_EXPERTS)))
    br_row = jnp.pad(b_r, (0, LANES - N_EXPERTS))[None]
    n_asg = n * TOP_K
    n_blocks = n_asg // MOE_BLOCK + N_EXPERTS
    assert n_asg % MOE_BLOCK == 0 and n % SC_WINDOW == 0
    n_slots = n_blocks * MOE_BLOCK
    idx, gate, rank, pstart, tab = _router(h, wr_pad, br_row, n_blocks)
    experts = jnp.arange(N_EXPERTS, dtype=I32)[:, None, None]
    group_start = jnp.sum(jnp.where(idx[None, :TOP_K] == experts, pstart[0, :N_EXPERTS, None, None], 0), axis=0)
    slot_by_choice = group_start + rank[:TOP_K]
    xb = _sc_scatter_tiles(ht.reshape(n, SUBLANES, LANES), slot_by_choice, n_slots)
    yb = _experts(xb.reshape(n_slots * SUBLANES, LANES), tab[:n_blocks, 0], tab[0, 2:3], tab[:n_blocks, 1],
                  tab[:n_blocks, 3], layer, w_up, b_up[:, :, None, :], w_dn, b_dn[:, :, None, :])
    y = _sc_gather_tiles(yb.reshape(n_slots, SUBLANES, LANES), slot_by_choice.reshape(-1))
    return _combine(h, y.reshape(n_asg * SUBLANES, LANES), gate, p_prompt, p_sample, layer, n_p, split,
                    w_pg.astype(BF16), w_pp.astype(BF16), g2[None], b2[None])


def _lane_row(v, lane0):
    return jnp.zeros((1, LANES), F32).at[0, lane0:lane0 + v.shape[0]].set(v.astype(F32))


def kernel(x_prompt, x_sample, cache_fox_k, cache_fox_v, cache_fox_logf, state_gdn, state_gdn_conv,
           cache_pool, p_prompt, p_sample, w_in_ab, b_fgate, gdn_a_log, gdn_dt_bias, gdn_conv_w,
           gdn_norm_g, w_out_ab, pool_w, pool_scale, w_out_pool, ln1_g, ln1_b, ln2_g, ln2_b,
           w_router, b_router, w_expert_up, b_expert_up, w_expert_down, b_expert_down,
           w_ple_gate, w_ple_proj):
    n_pb, seq, d = x_prompt.shape
    n_sb, dseq, _ = x_sample.shape
    past = cache_fox_k.shape[2]
    assert n_pb == 1 and dseq == CHUNK and past % dseq == 0 and seq % TQ == 0
    assert d == SUBLANES * LANES
    n_p = n_pb * seq
    n_s = n_sb * dseq
    n = n_p + n_s
    assert n_p % TM_TOK == 0 and n_s % TM_TOK == 0
    n_layers = p_prompt.shape[0]
    pp_all = p_prompt.reshape(n_layers * n_p, -1)
    ps_all = p_sample.reshape(n_layers * n_s, -1)

    def tail(h, ht, i, split):
        return _layer_tail(h, ht, pp_all, ps_all, i, n_p, split, ln2_g[i], ln2_b[i], w_router[i], b_router[i],
                           w_expert_up, b_expert_up, w_expert_down, b_expert_down, w_ple_gate[i],
                           w_ple_proj[i])

    w_in = w_in_ab[0]
    n_small = 3 * N_HEADS
    ff0 = 3 * WIDTH
    gq0 = ff0 + N_HEADS
    ga0 = gq0 + 4 * WIDTH
    w_small = jnp.concatenate([w_in[:, ff0:gq0], w_in[:, ga0:ga0 + 2 * N_HEADS],
                               jnp.zeros((d, LANES - n_small), F32)], axis=1)
    w_all = jnp.concatenate([w_in[:, :ff0], w_in[:, gq0:ga0], w_small], axis=1)
    bf_row = _lane_row(b_fgate[0], LANE_F)
    up = _proj(x_prompt.reshape(n_p, d), w_all.astype(BF16), bf_row, U_COLS, False)
    us = _proj(x_sample.reshape(n_s, d), w_all, bf_row, LANES, True)

    u3 = up[None]
    us3 = us.reshape(n_sb, dseq, U_COLS)
    cq_p, ck_p = _cumsum(u3, 1, n_p, TK, COL_S // LANES)
    lf_s = jnp.concatenate(
        [jnp.pad(cache_fox_logf[0].astype(F32), ((0, 0), (0, 0), (0, LANES - N_HEADS))),
         us3[:, :, COL_S:]], axis=1)
    cq_s, ck_s = _cumsum(lf_s, n_sb, past + dseq, past + dseq, 0)

    of_p = _fox(u3, lambda b, i: (0, i, COL_Q // WIDTH), u3, lambda b, j: (0, j, COL_K // WIDTH),
                u3, lambda b, j: (0, j, COL_V // WIDTH), cq_p, lambda b, i: (0, i, 0),
                ck_p, lambda b, j: (0, 0, j), 1, n_p // TQ, TQ, TK, 0)
    k_all = jnp.concatenate([cache_fox_k[0].reshape(n_sb, past, WIDTH), us3[:, :, COL_K:COL_K + WIDTH]], axis=1)
    v_all = jnp.concatenate([cache_fox_v[0].reshape(n_sb, past, WIDTH), us3[:, :, COL_V:COL_V + WIDTH]], axis=1)
    of_s = _fox(us3, lambda b, i: (b, 0, COL_Q // WIDTH), k_all, lambda b, j: (b, 0, 0),
                v_all, lambda b, j: (b, 0, 0), cq_s, lambda b, i: (b, past // dseq, 0),
                ck_s, lambda b, j: (b, 0, 0), n_sb, 1, dseq, past + dseq, past, hi=True)

    gw = 3 * WIDTH
    conv_w = jnp.pad(gdn_conv_w[0], ((0, SUBLANES - CONV_WIDTH), (0, 0)))
    gdn_args = (conv_w, _lane_row(gdn_a_log[0], LANE_A), _lane_row(gdn_dt_bias[0], LANE_A),
                gdn_norm_g[0][None])
    n_pstep = n_p // GDN_ROWS
    ends = lambda k: (jnp.asarray((np.arange(k) == 0).astype(np.int32)),
                      jnp.asarray((np.arange(k) == k - 1).astype(np.int32)))
    og_p, st_p = _gdn(up.reshape(n_pstep, GDN_ROWS, U_COLS), False, *ends(n_pstep),
                      jnp.zeros((n_pstep,), I32), jnp.zeros((1, SUBLANES, gw), F32), gdn_args[0],
                      jnp.zeros((1, N_HEADS, HEAD_DIM, HEAD_DIM), F32), *gdn_args[1:])
    conv_past = jnp.pad(state_gdn_conv[0].astype(F32), ((0, 0), (SUBLANES - (CONV_WIDTH - 1), 0), (0, 0)))
    ones = jnp.ones((n_sb,), I32)
    og_s, st_s = _gdn(us3, True, ones, ones, jnp.arange(n_sb, dtype=I32), conv_past, gdn_args[0],
                      state_gdn[0].astype(F32), *gdn_args[1:])
    h, ht = _outproj_ln((x_prompt.reshape(n_p, d), of_p.reshape(n_p, WIDTH), og_p.reshape(n_p, WIDTH)),
                        (x_sample.reshape(n_s, d), of_s.reshape(n_s, WIDTH), og_s.reshape(n_s, WIDTH)),
                        w_out_ab[0], ln1_g[0][None], ln1_b[0][None])
    x1 = tail(h, ht, 0, False)

    pool_args = (pool_w[0].astype(BF16), pool_scale[0][None], w_out_pool[0].astype(BF16),
                 ln1_g[1][None], ln1_b[1][None])
    ratio = TM_TOK // POOL_HALO
    x1_halo = x1.reshape(n // POOL_HALO, POOL_HALO, d)
    h_pool = _pool_ln(x1, lambda i: (i, 0), x1_halo, lambda i: (jnp.maximum(i * ratio - 1, 0), 0, 0),
                      n_p // TM_TOK, TM_TOK, 0, True, *pool_args)
    cache16 = jnp.pad(cache_pool[0].astype(F32), ((0, 0), (POOL_HALO - POOL_STATE, 0), (0, 0)))
    h, ht = _pool_ln(x1, lambda i: (n_p // dseq + i, 0), cache16, lambda i: (i, 0, 0),
                     n_sb, dseq, past, False, *pool_args, into=h_pool)
    x2_p, x2_s = tail(h, ht, 1, True)

    return (x2_p.reshape(n_pb, seq, d), x2_s.reshape(n_sb, dseq, d),
            up[:, COL_K:COL_K + WIDTH].reshape(1, n_pb, seq, N_HEADS, HEAD_DIM),
            up[:, COL_V:COL_V + WIDTH].reshape(1, n_pb, seq, N_HEADS, HEAD_DIM),
            up[:, COL_S:COL_S + N_HEADS].reshape(1, n_pb, seq, N_HEADS),
            st_p.reshape(1, n_pb, N_HEADS, HEAD_DIM, HEAD_DIM),
            up[seq - (CONV_WIDTH - 1):, COL_G:COL_G + gw].reshape(1, n_pb, CONV_WIDTH - 1, gw),
            x1[n_p - POOL_STATE:n_p].reshape(1, n_pb, POOL_STATE, d),
            us[:, COL_K:COL_K + WIDTH].reshape(1, n_sb, dseq, N_HEADS, HEAD_DIM),
            us[:, COL_V:COL_V + WIDTH].reshape(1, n_sb, dseq, N_HEADS, HEAD_DIM),
            us[:, COL_S:COL_S + N_HEADS].reshape(1, n_sb, dseq, N_HEADS),
            st_s.reshape(1, n_sb, N_HEADS, HEAD_DIM, HEAD_DIM),
            us[:, COL_G:COL_G + gw].reshape(n_sb, dseq, gw)[:, dseq - (CONV_WIDTH - 1):].reshape(
                1, n_sb, CONV_WIDTH - 1, gw),
            x1[n_p:].reshape(n_sb, dseq, d)[:, dseq - POOL_STATE:].reshape(1, n_sb, POOL_STATE, d))
```

```python
import functools

import numpy as np
import jax
import jax.numpy as jnp
from jax import lax
from jax.experimental import pallas as pl
from jax.experimental.pallas import tpu as pltpu
from jax.experimental.pallas import tpu_sc as plsc

F32 = jnp.float32
BF16 = jnp.bfloat16
I32 = jnp.int32
HIGHEST = lax.Precision.HIGHEST

LANES = 128
SUBLANES = 8
VMEM_LIMIT = 56 * 1024 * 1024

HEAD_DIM = 128
N_HEADS = 4
WIDTH = N_HEADS * HEAD_DIM
CHUNK = 64
CONV_WIDTH = 4
POOL_WINDOWS = (2, 4, 8, 16)
POOL_HALO = 16
POOL_STATE = 15
N_EXPERTS = 32
TOP_K = 4
SWIGLU_LIMIT = 7.0
SWIGLU_ALPHA = 1.702
DEPTH = 2
DN_ALPHA = (2 * DEPTH) ** 0.25
LN_EPS = 1e-5
NORM_EPS = 1e-6
NEG_INF = -1e30
LOG2E = 1.4426950408889634

COL_Q, COL_K, COL_V = 0, WIDTH, 2 * WIDTH
COL_G = 3 * WIDTH
COL_Z = 6 * WIDTH
COL_S = 7 * WIDTH
U_COLS = COL_S + LANES
LANE_F, LANE_A, LANE_B = 0, N_HEADS, 2 * N_HEADS

TM_PROJ = 256
TM_TOK = 512
SC_WINDOW = 32
MOE_BLOCK = 512
TQ = 1024
TK = 512
GDN_ROWS = 256
GDN_CHUNK = CHUNK


def _cparams(sem):
    return pltpu.CompilerParams(dimension_semantics=sem, vmem_limit_bytes=VMEM_LIMIT)


def _softplus(x):
    return jnp.maximum(x, 0.0) + jnp.log1p(jnp.exp(-jnp.abs(x)))


def _sigmoid(x):
    return 1.0 / (1.0 + jnp.exp(-x))


def _silu(x):
    return x * _sigmoid(x)


def _layer_norm(y, g, b):
    mu = jnp.mean(y, axis=-1, keepdims=True)
    yc = y - mu
    var = jnp.mean(yc * yc, axis=-1, keepdims=True)
    return yc * lax.rsqrt(var + LN_EPS) * g + b


def _dot_general(a, b, dims, hi):
    if hi:
        return lax.dot_general(a.astype(F32), b.astype(F32), (dims, ((), ())), precision=HIGHEST,
                               preferred_element_type=F32)
    return lax.dot_general(a.astype(BF16), b.astype(BF16), (dims, ((), ())), preferred_element_type=F32)


def _dot(a, b, hi=False):
    return _dot_general(a, b, ((1,), (0,)), hi)


def _dot_nt(a, b, hi=False):
    return _dot_general(a, b, ((1,), (1,)), hi)


def _dot_tn(a, b, hi=False):
    return _dot_general(a, b, ((0,), (0,)), hi)


def _spread_lanes(x, width):
    if width % LANES == 0:
        return jnp.concatenate([x] * (width // LANES), axis=1)
    return jnp.broadcast_to(x[:, 0:1], (x.shape[0], width))


def _load_token_tiles(ref, lead, n_tok):
    return jnp.concatenate([ref[(*lead, pl.ds(j, n_tok, stride=SUBLANES), slice(None))]
                            for j in range(SUBLANES)], axis=1)


def _store_token_tiles(ref, lead, x):
    for j in range(SUBLANES):
        ref[(*lead, pl.ds(j, x.shape[0], stride=SUBLANES), slice(None))] = x[:, j * LANES:(j + 1) * LANES]


def _lanes_to_rows(x, lane0):
    r = lax.broadcasted_iota(I32, (SUBLANES, LANES), 0)
    c = lax.broadcasted_iota(I32, (SUBLANES, LANES), 1)
    sel = (c == r + lane0).astype(F32)
    return lax.dot_general(sel, x, (((1,), (1,)), ((), ())), precision=HIGHEST,
                           preferred_element_type=F32)


def _proj_kernel(x_ref, w_ref, bf_ref, u_ref, *, hi):
    u = _dot(x_ref[...], w_ref[...], hi)
    u_ref[...] = u

    @pl.when(pl.program_id(1) == pl.num_programs(1) - 1)
    def _():
        small = u[:, u.shape[1] - LANES:]
        lane = lax.broadcasted_iota(I32, small.shape, 1)
        logf = -_softplus(-(small + bf_ref[...]))
        u_ref[:, u.shape[1] - LANES:] = jnp.where(lane < LANE_A, logf, small)


def _proj(x, w, bf_row, tn, hi):
    n, d = x.shape
    m = w.shape[1]
    return pl.pallas_call(
        functools.partial(_proj_kernel, hi=hi),
        grid=(n // TM_PROJ, m // tn),
        in_specs=[pl.BlockSpec((TM_PROJ, d), lambda i, j: (i, 0)),
                  pl.BlockSpec((d, tn), lambda i, j: (0, j)),
                  pl.BlockSpec((1, LANES), lambda i, j: (0, 0))],
        out_specs=pl.BlockSpec((TM_PROJ, tn), lambda i, j: (i, j)),
        out_shape=jax.ShapeDtypeStruct((n, m), F32),
        compiler_params=_cparams(("parallel", "parallel")),
        name="in_proj",
    )(x, w, bf_row)


def _cumsum_kernel(lf_ref, crep_ref, crow_ref, carry_ref):
    @pl.when(pl.program_id(1) == 0)
    def _():
        carry_ref[...] = jnp.zeros_like(carry_ref)

    lf = lf_ref[0]
    t = lf.shape[0]
    r = lax.broadcasted_iota(I32, (t, t), 0)
    c = lax.broadcasted_iota(I32, (t, t), 1)
    tril = (c <= r).astype(F32)
    cs = jnp.dot(tril, lf, precision=HIGHEST, preferred_element_type=F32) + carry_ref[0:1, :]
    carry_ref[...] = jnp.broadcast_to(cs[t - 1:t, :], carry_ref.shape)
    c2 = cs * LOG2E
    crow_ref[0] = _lanes_to_rows(c2, LANE_F)
    for h in range(N_HEADS):
        crep_ref[0, :, h * HEAD_DIM:(h + 1) * HEAD_DIM] = jnp.broadcast_to(
            c2[:, LANE_F + h:LANE_F + h + 1], (t, HEAD_DIM))


def _cumsum(arr, n_batch, length, tl, col_block):
    return pl.pallas_call(
        _cumsum_kernel,
        grid=(n_batch, length // tl),
        in_specs=[pl.BlockSpec((1, tl, LANES), lambda b, j: (b, j, col_block))],
        out_specs=[pl.BlockSpec((1, tl, WIDTH), lambda b, j: (b, j, 0)),
                   pl.BlockSpec((1, SUBLANES, tl), lambda b, j: (b, 0, j))],
        out_shape=[jax.ShapeDtypeStruct((n_batch, length, WIDTH), F32),
                   jax.ShapeDtypeStruct((n_batch, SUBLANES, length), F32)],
        scratch_shapes=[pltpu.VMEM((SUBLANES, LANES), F32)],
        compiler_params=_cparams(("parallel", "arbitrary")),
        name="logf_cumsum",
    )(arr)


def _fox_kernel(qi_ref, kj_ref, last_ref, q_ref, k_ref, v_ref, cq_ref, ck_ref, o_ref,
                m_ref, l_ref, acc_ref, *, tq, tk, past, hi):
    s_idx = pl.program_id(1)
    qi = qi_ref[s_idx]
    kj = kj_ref[s_idx]

    @pl.when(kj == 0)
    def _():
        m_ref[...] = jnp.full_like(m_ref, NEG_INF)
        l_ref[...] = jnp.zeros_like(l_ref)
        acc_ref[...] = jnp.zeros_like(acc_ref)

    def update(masked):
        if masked:
            q_pos = past + qi * tq + lax.broadcasted_iota(I32, (tq, tk), 0)
            k_pos = kj * tk + lax.broadcasted_iota(I32, (tq, tk), 1)
            visible = k_pos <= q_pos
        for h in range(N_HEADS):
            cols = slice(h * HEAD_DIM, (h + 1) * HEAD_DIM)
            q = q_ref[0, :, cols] * (HEAD_DIM ** -0.5 * LOG2E)
            t = _dot_nt(q, k_ref[0, :, cols], hi) - ck_ref[0, h:h + 1, :]
            if masked:
                t = jnp.where(visible, t, NEG_INF)
            cq = cq_ref[0, :, cols]
            m_prev = m_ref[h]
            m_new = jnp.maximum(m_prev, jnp.max(t, axis=-1, keepdims=True) + cq)
            p = jnp.exp2(t - _spread_lanes(m_new - cq, tk))
            alpha = jnp.exp2(m_prev - m_new)
            l_ref[h] = alpha * l_ref[h] + jnp.sum(p, axis=-1, keepdims=True)
            acc_ref[:, cols] = alpha * acc_ref[:, cols] + _dot(p, v_ref[0, :, cols], hi)
            m_ref[h] = m_new

    crosses_diagonal = kj * tk + (tk - 1) > past + qi * tq
    pl.when(crosses_diagonal)(functools.partial(update, True))
    pl.when(jnp.logical_not(crosses_diagonal))(functools.partial(update, False))

    @pl.when(last_ref[s_idx] == 1)
    def _():
        for h in range(N_HEADS):
            cols = slice(h * HEAD_DIM, (h + 1) * HEAD_DIM)
            o_ref[0, :, cols] = acc_ref[:, cols] / l_ref[h]


def _fox_schedule(n_q, tq, tk, past):
    qi, kj, last = [], [], []
    for i in range(n_q):
        hi = (past + (i + 1) * tq - 1) // tk
        for j in range(hi + 1):
            qi.append(i)
            kj.append(j)
            last.append(1 if j == hi else 0)
    return (jnp.asarray(np.array(qi, np.int32)), jnp.asarray(np.array(kj, np.int32)),
            jnp.asarray(np.array(last, np.int32)))


def _fox(q_arr, q_map, k_arr, k_map, v_arr, v_map, cq_arr, cq_map, ck_arr, ck_map,
         n_batch, n_q, tq, tk, past, hi=False):
    qi, kj, last = _fox_schedule(n_q, tq, tk, past)
    n_steps = int(qi.shape[0])
    spec = lambda shape, fn, tab: pl.BlockSpec(shape, lambda b, s, qi_r, kj_r, la_r: fn(b, (qi_r if tab == 'q' else kj_r)[s]))
    return pl.pallas_call(
        functools.partial(_fox_kernel, tq=tq, tk=tk, past=past, hi=hi),
        grid_spec=pltpu.PrefetchScalarGridSpec(
            num_scalar_prefetch=3,
            grid=(n_batch, n_steps),
            in_specs=[spec((1, tq, WIDTH), q_map, 'q'),
                      spec((1, tk, WIDTH), k_map, 'k'),
                      spec((1, tk, WIDTH), v_map, 'k'),
                      spec((1, tq, WIDTH), cq_map, 'q'),
                      spec((1, SUBLANES, tk), ck_map, 'k')],
            out_specs=spec((1, tq, WIDTH), lambda b, i: (b, i, 0), 'q'),
            scratch_shapes=[pltpu.VMEM((N_HEADS, tq, HEAD_DIM), F32),
                            pltpu.VMEM((N_HEADS, tq, HEAD_DIM), F32),
                            pltpu.VMEM((tq, WIDTH), F32)]),
        out_shape=jax.ShapeDtypeStruct((n_batch, n_q * tq, WIDTH), F32),
        compiler_params=_cparams(("parallel", "arbitrary")),
        name="fox_attention",
    )(qi, kj, last, q_arr, k_arr, v_arr, cq_arr, ck_arr)


def _gdn_kernel(first_ref, last_ref, seq_ref,
                pre_ref, z_ref, sm_ref, cpast_ref, convw_ref, s0_ref, alog_ref, dtb_ref, ng_ref,
                o_ref, sout_ref, stage_ref, s_ref, *, rows, chunk, hi):
    dot, dot_nt, dot_tn = (functools.partial(f, hi=hi) for f in (_dot, _dot_nt, _dot_tn))
    step = pl.program_id(0)
    halo = SUBLANES
    n_chunks = rows // chunk

    @pl.when(first_ref[step] == 1)
    def _():
        stage_ref[0:halo, :] = cpast_ref[0]
        s_ref[...] = s0_ref[0]

    stage_ref[halo:halo + rows, :] = pre_ref[0]
    conv = stage_ref[halo:halo + rows, :] * convw_ref[CONV_WIDTH - 1:CONV_WIDTH, :]
    for j in range(1, CONV_WIDTH):
        conv = conv + (stage_ref[halo - j:halo - j + rows, :]
                       * convw_ref[CONV_WIDTH - 1 - j:CONV_WIDTH - j, :])
    stage_ref[0:halo, :] = stage_ref[rows:rows + halo, :]
    act = _silu(conv)

    small = sm_ref[0]
    beta_all = _sigmoid(small)
    g_all = -jnp.exp(alog_ref[...]) * _softplus(small + dtb_ref[...])
    r = lax.broadcasted_iota(I32, (rows, rows), 0)
    c = lax.broadcasted_iota(I32, (rows, rows), 1)
    same_chunk = (r // chunk) == (c // chunk)
    incl = same_chunk & (c <= r)
    strict = same_chunk & (c < r)
    eye = (c == r).astype(F32)
    gc_all = jnp.dot(incl.astype(F32), g_all, precision=HIGHEST, preferred_element_type=F32)
    gc_rows = _lanes_to_rows(gc_all, LANE_A)

    heads = range(N_HEADS)
    head_cols = [slice(h * HEAD_DIM, (h + 1) * HEAD_DIM) for h in heads]
    q, k, gc, decay, kb, vb, low = [], [], [], [], [], [], []
    for h in heads:
        qh = act[:, h * HEAD_DIM:(h + 1) * HEAD_DIM]
        kh = act[:, WIDTH + h * HEAD_DIM:WIDTH + (h + 1) * HEAD_DIM]
        vh = act[:, 2 * WIDTH + h * HEAD_DIM:2 * WIDTH + (h + 1) * HEAD_DIM]
        q.append(qh * lax.rsqrt(jnp.sum(qh * qh, axis=-1, keepdims=True) + NORM_EPS) * (HEAD_DIM ** -0.5))
        k.append(kh * lax.rsqrt(jnp.sum(kh * kh, axis=-1, keepdims=True) + NORM_EPS))
        beta = beta_all[:, LANE_B + h:LANE_B + h + 1]
        gc.append(gc_all[:, LANE_A + h:LANE_A + h + 1])
        diff = gc[h] - gc_rows[h:h + 1, :]
        decay.append(jnp.where(incl, jnp.exp(jnp.where(incl, diff, 0.0)), 0.0))
        kb.append(k[h] * beta)
        vb.append(vh * beta)
    for h in heads:
        low.append(jnp.where(strict, dot_nt(kb[h], k[h]) * decay[h], 0.0))
    inv = [eye - low[h] for h in heads]
    pw = [dot(low[h], low[h]) for h in heads]
    n_sq = chunk.bit_length() - 2
    for it in range(n_sq):
        inv = [inv[h] + dot(inv[h], pw[h]) for h in heads]
        if it + 1 < n_sq:
            pw = [dot(pw[h], pw[h]) for h in heads]
    egc = [jnp.exp(gc[h]) for h in heads]
    uw = [dot(inv[h], jnp.concatenate([vb[h], kb[h] * egc[h]], axis=1)) for h in heads]
    intra = [jnp.where(incl, dot_nt(q[h], k[h]) * decay[h], 0.0) for h in heads]
    qd = [q[h] * egc[h] for h in heads]
    g_last = [[gc[h][(g + 1) * chunk - 1:(g + 1) * chunk, :] for g in range(n_chunks)] for h in heads]
    kd = [k[h] * jnp.exp(jnp.concatenate([jnp.broadcast_to(gl, (chunk, 1)) for gl in g_last[h]], axis=0)
                         - gc[h]) for h in heads]
    state = [s_ref[h] for h in heads]
    v_new = [[] for _ in heads]
    for g in range(n_chunks):
        rs = slice(g * chunk, (g + 1) * chunk)
        for h in heads:
            v_new[h].append(uw[h][rs, :HEAD_DIM] - dot(uw[h][rs, HEAD_DIM:], state[h]))
        for h in heads:
            v_rows = jnp.concatenate(
                v_new[h] + [jnp.zeros((rows - (g + 1) * chunk, HEAD_DIM), F32)] * (g + 1 < n_chunks), axis=0)
            o = dot(qd[h][rs, :], state[h]) + dot(intra[h][rs, :], v_rows)
            state[h] = state[h] * jnp.exp(g_last[h][g]) + dot_tn(kd[h][rs, :], v_new[h][g])
            o = (o * lax.rsqrt(jnp.mean(o * o, axis=-1, keepdims=True) + NORM_EPS)
                 * ng_ref[...] * _silu(z_ref[0, rs, head_cols[h]]))
            o_ref[0, rs, head_cols[h]] = o
    for h in heads:
        s_ref[h] = state[h]

    @pl.when(last_ref[step] == 1)
    def _():
        sout_ref[0] = s_ref[...]


def _gdn(u_view, hi, first, last, seq, conv_past, conv_w, s0, alog_row, dtb_row, ng_row):
    n_steps, rows, _ = u_view.shape
    n_seq = s0.shape[0]
    gw = 3 * WIDTH
    return pl.pallas_call(
        functools.partial(_gdn_kernel, rows=rows, chunk=min(rows, GDN_CHUNK), hi=hi),
        grid_spec=pltpu.PrefetchScalarGridSpec(
            num_scalar_prefetch=3,
            grid=(n_steps,),
            in_specs=[pl.BlockSpec((1, rows, gw), lambda s, f, l, q: (s, 0, COL_G // gw)),
                      pl.BlockSpec((1, rows, WIDTH), lambda s, f, l, q: (s, 0, COL_Z // WIDTH)),
                      pl.BlockSpec((1, rows, LANES), lambda s, f, l, q: (s, 0, COL_S // LANES)),
                      pl.BlockSpec((1, SUBLANES, gw), lambda s, f, l, q: (q[s], 0, 0)),
                      pl.BlockSpec((SUBLANES, gw), lambda s, f, l, q: (0, 0)),
                      pl.BlockSpec((1, N_HEADS, HEAD_DIM, HEAD_DIM), lambda s, f, l, q: (q[s], 0, 0, 0)),
                      pl.BlockSpec((1, LANES), lambda s, f, l, q: (0, 0)),
                      pl.BlockSpec((1, LANES), lambda s, f, l, q: (0, 0)),
                      pl.BlockSpec((1, LANES), lambda s, f, l, q: (0, 0))],
            out_specs=[pl.BlockSpec((1, rows, WIDTH), lambda s, f, l, q: (s, 0, 0)),
                       pl.BlockSpec((1, N_HEADS, HEAD_DIM, HEAD_DIM), lambda s, f, l, q: (q[s], 0, 0, 0))],
            scratch_shapes=[pltpu.VMEM((rows + SUBLANES, gw), F32),
                            pltpu.VMEM((N_HEADS, HEAD_DIM, HEAD_DIM), F32)]),
        out_shape=[jax.ShapeDtypeStruct((n_steps, rows, WIDTH), F32),
                   jax.ShapeDtypeStruct((n_seq, N_HEADS, HEAD_DIM, HEAD_DIM), F32)],
        compiler_params=_cparams(("arbitrary",)),
        name="gated_deltanet",
    )(first, last, seq, u_view, u_view, u_view, conv_past, conv_w, s0, alog_row, dtb_row, ng_row)


def _outproj_ln_kernel(xp_ref, ofp_ref, ogp_ref, xs_ref, ofs_ref, ogs_ref, w_ref, wf_ref, g_ref, b_ref,
                       h_ref, ht_ref, *, hi_from):
    def run(hi, x_ref, of_ref, og_ref):
        w = wf_ref if hi else w_ref
        mix = _dot(of_ref[...], w[0:WIDTH, :], hi) + _dot(og_ref[...], w[WIDTH:2 * WIDTH, :], hi)
        h = _layer_norm(DN_ALPHA * x_ref[...] + mix, g_ref[...], b_ref[...])
        h_ref[...] = h
        _store_token_tiles(ht_ref, (), h)

    pl.when(pl.program_id(0) < hi_from)(functools.partial(run, False, xp_ref, ofp_ref, ogp_ref))
    pl.when(pl.program_id(0) >= hi_from)(functools.partial(run, True, xs_ref, ofs_ref, ogs_ref))


def _outproj_ln(prompt, sample, w, g_row, b_row):
    n_p, d = prompt[0].shape
    n = n_p + sample[0].shape[0]
    hi_from = n_p // TM_TOK
    row = lambda i: (i, 0)
    row_p = lambda i: (jnp.minimum(i, hi_from - 1), 0)
    row_s = lambda i: (jnp.maximum(i - hi_from, 0), 0)
    fixed = lambda i: (0, 0)
    group = lambda rows: [pl.BlockSpec((TM_TOK, d), rows), pl.BlockSpec((TM_TOK, WIDTH), rows),
                          pl.BlockSpec((TM_TOK, WIDTH), rows)]
    return pl.pallas_call(
        functools.partial(_outproj_ln_kernel, hi_from=hi_from),
        grid=(n // TM_TOK,),
        in_specs=group(row_p) + group(row_s) + [pl.BlockSpec((2 * WIDTH, d), fixed),
                                                pl.BlockSpec((2 * WIDTH, d), fixed),
                                                pl.BlockSpec((1, d), fixed), pl.BlockSpec((1, d), fixed)],
        out_specs=[pl.BlockSpec((TM_TOK, d), row), pl.BlockSpec((TM_TOK * SUBLANES, LANES), row)],
        out_shape=[jax.ShapeDtypeStruct((n, d), F32), jax.ShapeDtypeStruct((n * SUBLANES, LANES), F32)],
        compiler_params=_cparams(("parallel",)),
        name="out_proj_ln",
    )(*prompt, *sample, w.astype(BF16), w, g_row, b_row)


def _pool_ln_kernel(x_ref, halo_ref, pw_ref, ps_ref, w_ref, g_ref, b_ref, h_ref, ht_ref, stage_ref,
                    *, tm, pos0, zero_first_halo):
    i = pl.program_id(0)
    stage_ref[0:POOL_HALO, :] = halo_ref[0]
    if zero_first_halo:
        @pl.when(i == 0)
        def _():
            stage_ref[0:POOL_HALO, :] = jnp.zeros((POOL_HALO, stage_ref.shape[1]), F32)
    x = x_ref[...]
    stage_ref[POOL_HALO:POOL_HALO + tm, :] = x
    gdim = x.shape[1] // len(POOL_WINDOWS)
    pos = pos0 + lax.broadcasted_iota(I32, (tm, 1), 0)
    if zero_first_halo:
        pos = pos + i * tm
    parts = []
    for gi, win in enumerate(POOL_WINDOWS):
        cols = slice(gi * gdim, (gi + 1) * gdim)
        s = stage_ref[POOL_HALO:POOL_HALO + tm, cols]
        for j in range(1, win):
            s = s + stage_ref[POOL_HALO - j:POOL_HALO - j + tm, cols]
        cnt = jnp.minimum(pos + 1, win).astype(F32)
        zg = s / cnt - x[:, cols]
        parts.append(_dot(zg, pw_ref[gi]))
    zg = jnp.concatenate(parts, axis=-1) * ps_ref[...]
    mix = _dot(zg, w_ref[...])
    h = _layer_norm(DN_ALPHA * x + mix, g_ref[...], b_ref[...])
    h_ref[...] = h
    _store_token_tiles(ht_ref, (), h)


def _pool_ln_into_kernel(*refs, **kw):
    _pool_ln_kernel(*refs[:7], *refs[9:], **kw)


def _pool_ln(x, x_map, halo_arr, halo_map, n_tiles, tm, pos0, zero_first_halo,
             pw_bf16, ps_row, w_bf16, g_row, b_row, into=None):
    n, d = x.shape
    gdim = d // len(POOL_WINDOWS)
    fixed = lambda i: (0, 0)
    in_specs = [pl.BlockSpec((tm, d), x_map),
                pl.BlockSpec((1, POOL_HALO, d), halo_map),
                pl.BlockSpec((len(POOL_WINDOWS), gdim, gdim), lambda i: (0, 0, 0)),
                pl.BlockSpec((1, d), fixed), pl.BlockSpec((d, d), fixed),
                pl.BlockSpec((1, d), fixed), pl.BlockSpec((1, d), fixed)]
    args = (x, halo_arr, pw_bf16, ps_row, w_bf16, g_row, b_row)
    kw = dict(tm=tm, pos0=pos0, zero_first_halo=zero_first_halo)
    return pl.pallas_call(
        functools.partial(_pool_ln_kernel if into is None else _pool_ln_into_kernel, **kw),
        grid=(n_tiles,),
        in_specs=in_specs + ([] if into is None else [pl.BlockSpec(memory_space=pl.ANY)] * 2),
        out_specs=[pl.BlockSpec((tm, d), x_map),
                   pl.BlockSpec((tm * SUBLANES, LANES), x_map)],
        out_shape=[jax.ShapeDtypeStruct((n, d), F32), jax.ShapeDtypeStruct((n * SUBLANES, LANES), F32)],
        input_output_aliases={} if into is None else {7: 0, 8: 1},
        scratch_shapes=[pltpu.VMEM((POOL_HALO + tm, d), F32)],
        compiler_params=_cparams(("arbitrary",)),
        name="pool_mixer_ln",
    )(*args, *(() if into is None else into))


def _router_kernel(h_ref, wr_ref, br_ref, idx_ref, gate_ref, rank_ref, pstart_ref, tab_ref, carry_ref):
    @pl.when(pl.program_id(0) == 0)
    def _():
        carry_ref[...] = jnp.zeros_like(carry_ref)

    tm = h_ref.shape[0]
    lane = lax.broadcasted_iota(I32, (tm, LANES), 1).astype(F32)
    logits = jnp.dot(h_ref[...], wr_ref[...], precision=HIGHEST, preferred_element_type=F32)
    work = jnp.where(lane < N_EXPERTS, logits + br_ref[...], -jnp.inf)
    vals, ids = [], []
    for _ in range(TOP_K):
        m = jnp.max(work, axis=-1, keepdims=True)
        ik = jnp.min(jnp.where(work == m, lane, float(LANES)), axis=-1, keepdims=True)
        vals.append(m)
        ids.append(ik)
        work = jnp.where(lane == ik, -jnp.inf, work)
    exps = [jnp.exp(v - vals[0]) for v in vals]
    denom = exps[0]
    for e in exps[1:]:
        denom = denom + e
    multihot = jnp.zeros((tm, LANES), F32)
    idx_out = jnp.zeros((tm, LANES), F32)
    gate_out = jnp.zeros((tm, LANES), F32)
    for k in range(TOP_K):
        multihot = multihot + (lane == ids[k]).astype(F32)
        idx_out = jnp.where(lane == k, ids[k], idx_out)
        gate_out = jnp.where(lane == k, exps[k] / denom, gate_out)
    r = lax.broadcasted_iota(I32, (tm, tm), 0)
    c = lax.broadcasted_iota(I32, (tm, tm), 1)
    before = _dot((c < r).astype(F32), multihot) + carry_ref[0:1, :]
    rank_out = jnp.zeros((tm, LANES), F32)
    for k in range(TOP_K):
        rk = jnp.sum(jnp.where(lane == ids[k], before, 0.0), axis=-1, keepdims=True)
        rank_out = jnp.where(lane == k, rk, rank_out)
    idx_ref[...] = jnp.transpose(idx_out)[:SUBLANES].astype(I32)
    rank_ref[...] = jnp.transpose(rank_out)[:SUBLANES].astype(I32)
    gate_ref[...] = gate_out
    total = carry_ref[0:1, :] + jnp.sum(multihot, axis=0, keepdims=True)
    carry_ref[...] = jnp.broadcast_to(total, carry_ref.shape)

    @pl.when(pl.program_id(0) == pl.num_programs(0) - 1)
    def _():
        n_rows = tab_ref.shape[0]
        padded = jnp.floor((total + (MOE_BLOCK - 1)) / MOE_BLOCK) * MOE_BLOCK
        rr = lax.broadcasted_iota(I32, (LANES, LANES), 0)
        cc = lax.broadcasted_iota(I32, (LANES, LANES), 1)
        pend = jnp.dot(jnp.broadcast_to(padded, (SUBLANES, LANES)), (rr <= cc).astype(F32),
                       precision=HIGHEST, preferred_element_type=F32)[0:1, :]
        pstart = pend - padded
        n_used = jnp.max(pend, axis=-1, keepdims=True) / MOE_BLOCK
        elane = lax.broadcasted_iota(I32, (n_rows, LANES), 1)
        blk = jnp.minimum(lax.broadcasted_iota(I32, (n_rows, 1), 0).astype(F32), n_used - 1.0) * MOE_BLOCK
        ends_before = jnp.where((elane < N_EXPERTS) & (pend <= blk), 1.0, 0.0)
        block_e = jnp.minimum(jnp.sum(ends_before, axis=-1, keepdims=True), N_EXPERTS - 1.0)
        mine = elane.astype(F32) == block_e
        filled = jnp.sum(jnp.where(mine, total - (blk - pstart), 0.0), axis=-1, keepdims=True)
        block_valid = jnp.clip(filled, 0.0, float(MOE_BLOCK))
        group_end = jnp.sum(jnp.where(mine, pend, 0.0), axis=-1, keepdims=True)
        next_e = jnp.sum(jnp.where((elane < N_EXPERTS) & (pend <= group_end), 1.0, 0.0), axis=-1, keepdims=True)
        tab = jnp.where(elane == 0, block_e, jnp.where(elane == 1, block_valid,
                                                       jnp.where(elane == 2, n_used,
                                                                 jnp.where(elane == 3, next_e, 0.0))))
        tab_ref[...] = tab.astype(I32)
        pstart_ref[...] = jnp.broadcast_to(pstart, pstart_ref.shape).astype(I32)


def _router(h, wr_pad, br_row, n_blocks):
    n, d = h.shape
    row = lambda i: (i, 0)
    col = lambda i: (0, i)
    fixed = lambda i: (0, 0)
    tab_rows = -(-n_blocks // SUBLANES) * SUBLANES
    return pl.pallas_call(
        _router_kernel,
        grid=(n // TM_TOK,),
        in_specs=[pl.BlockSpec((TM_TOK, d), row), pl.BlockSpec((d, LANES), fixed),
                  pl.BlockSpec((1, LANES), fixed)],
        out_specs=[pl.BlockSpec((SUBLANES, TM_TOK), col), pl.BlockSpec((TM_TOK, LANES), row),
                   pl.BlockSpec((SUBLANES, TM_TOK), col), pl.BlockSpec((SUBLANES, LANES), fixed),
                   pl.BlockSpec((tab_rows, LANES), fixed)],
        out_shape=[jax.ShapeDtypeStruct((SUBLANES, n), I32), jax.ShapeDtypeStruct((n, LANES), F32),
                   jax.ShapeDtypeStruct((SUBLANES, n), I32), jax.ShapeDtypeStruct((SUBLANES, LANES), I32),
                   jax.ShapeDtypeStruct((tab_rows, LANES), I32)],
        scratch_shapes=[pltpu.VMEM((SUBLANES, LANES), F32)],
        compiler_params=_cparams(("arbitrary",)),
        name="moe_router",
    )(h, wr_pad, br_row)


def _expert_kernel(be_ref, nu_ref, valid_ref, next_ref, xb_ref, wup_ref, bup_ref, wdn_ref, bdn_ref, yb_ref,
                   wup_f32_ref, wdn_f32_ref, wup_bf_ref, wdn_bf_ref, wsem, *, layer):
    b = pl.program_id(0)

    def fetch(e):
        return (pltpu.make_async_copy(wup_ref.at[layer, e], wup_f32_ref, wsem.at[0]),
                pltpu.make_async_copy(wdn_ref.at[layer, e], wdn_f32_ref, wsem.at[1]))

    @pl.when(b == 0)
    def _():
        for copy in fetch(be_ref[0]):
            copy.start()

    @pl.when(b < nu_ref[0])
    def _():
        @pl.when((b == 0) | (be_ref[b] != be_ref[jnp.maximum(b - 1, 0)]))
        def _():
            for copy in fetch(be_ref[b]):
                copy.wait()
            wup_bf_ref[...] = wup_f32_ref[...].astype(BF16)
            wdn_bf_ref[...] = wdn_f32_ref[...].astype(BF16)

            @pl.when(next_ref[b] < N_EXPERTS)
            def _():
                for copy in fetch(next_ref[b]):
                    copy.start()

        d_exp = wdn_f32_ref.shape[0]

        def run(n_rows):
            x = _load_token_tiles(xb_ref, (), n_rows)
            row = lax.broadcasted_iota(I32, (n_rows, 1), 0)
            x = jnp.where(row < valid_ref[b], x, 0.0)
            hu = jnp.dot(x.astype(BF16), wup_bf_ref[...], preferred_element_type=F32) + bup_ref[0, 0]
            glu = jnp.minimum(hu[:, :d_exp], SWIGLU_LIMIT)
            lin = jnp.clip(hu[:, d_exp:], -SWIGLU_LIMIT, SWIGLU_LIMIT)
            a = glu * _sigmoid(SWIGLU_ALPHA * glu) * (lin + 1.0)
            y = jnp.dot(a.astype(BF16), wdn_bf_ref[...], preferred_element_type=F32) + bdn_ref[0, 0]
            if n_rows < MOE_BLOCK:
                y = jnp.concatenate([y, jnp.zeros((MOE_BLOCK - n_rows, y.shape[1]), F32)], axis=0)
            _store_token_tiles(yb_ref, (), y)

        half = MOE_BLOCK // 2
        pl.when(valid_ref[b] > half)(functools.partial(run, MOE_BLOCK))
        pl.when(valid_ref[b] <= half)(functools.partial(run, half))

    @pl.when(b >= nu_ref[0])
    def _():
        yb_ref[...] = jnp.zeros_like(yb_ref)


def _experts(xb, block_e, n_used, block_valid, next_e, layer, w_up, b_up, w_dn, b_dn):
    d = SUBLANES * LANES
    rows = MOE_BLOCK * SUBLANES
    n_blocks = xb.shape[0] // rows
    d_up = w_up.shape[3]
    d_exp = w_dn.shape[2]
    bsel = lambda b, be, nu, va, ne: (layer, be[b], 0, 0)
    return pl.pallas_call(
        functools.partial(_expert_kernel, layer=layer),
        grid_spec=pltpu.PrefetchScalarGridSpec(
            num_scalar_prefetch=4,
            grid=(n_blocks,),
            in_specs=[pl.BlockSpec((rows, LANES), lambda b, be, nu, va, ne: (jnp.minimum(b, nu[0] - 1), 0)),
                      pl.BlockSpec(memory_space=pl.ANY),
                      pl.BlockSpec((1, 1, 1, d_up), bsel),
                      pl.BlockSpec(memory_space=pl.ANY),
                      pl.BlockSpec((1, 1, 1, d), bsel)],
            out_specs=pl.BlockSpec((rows, LANES), lambda b, be, nu, va, ne: (b, 0)),
            scratch_shapes=[pltpu.VMEM((d, d_up), F32), pltpu.VMEM((d_exp, d), F32),
                            pltpu.VMEM((d, d_up), BF16), pltpu.VMEM((d_exp, d), BF16),
                            pltpu.SemaphoreType.DMA((2,))]),
        out_shape=jax.ShapeDtypeStruct(xb.shape, F32),
        compiler_params=_cparams(("arbitrary",)),
        name="moe_experts",
    )(block_e, n_used, block_valid, next_e, xb, w_up, b_up, w_dn, b_dn)


def _sc_gather_tiles(table, idx):
    m = idx.shape[0]
    mesh = plsc.VectorSubcoreMesh(core_axis_name="core", subcore_axis_name="subcore")

    @functools.partial(pl.kernel, out_type=jax.ShapeDtypeStruct((m, SUBLANES, LANES), table.dtype), mesh=mesh)
    def gather(table_hbm, idx_hbm, out_hbm):
        def window(idx_vmem, out_vmem):
            pltpu.sync_copy(table_hbm.at[idx_vmem.at[0, pl.ds(0, SC_WINDOW)]], out_vmem)

        pltpu.emit_pipeline(
            window,
            grid=(m // SC_WINDOW,),
            in_specs=[pl.BlockSpec((1, LANES), lambda i: (i, 0))],
            out_specs=[pl.BlockSpec((SC_WINDOW, SUBLANES, LANES), lambda i: (i, 0, 0))],
            core_axis_name=("core", "subcore"),
            dimension_semantics=(pltpu.PARALLEL,),
        )(idx_hbm, out_hbm)

    idx_rows = jnp.pad(idx.reshape(m // SC_WINDOW, SC_WINDOW), ((0, 0), (0, LANES - SC_WINDOW)))
    return gather(table, idx_rows)


def _sc_scatter_tiles(tiles, idx_by_choice, n_out):
    n = tiles.shape[0]
    mesh = plsc.VectorSubcoreMesh(core_axis_name="core", subcore_axis_name="subcore")

    @functools.partial(pl.kernel, out_type=jax.ShapeDtypeStruct((n_out, SUBLANES, LANES), tiles.dtype),
                       mesh=mesh)
    def scatter(tiles_hbm, *refs):
        idx_hbm, out_hbm = refs[:TOP_K], refs[TOP_K]

        def window(tiles_vmem, *idx_vmem):
            for k in range(TOP_K):
                pltpu.sync_copy(tiles_vmem, out_hbm.at[idx_vmem[k].at[0, pl.ds(0, SC_WINDOW)]])

        pltpu.emit_pipeline(
            window,
            grid=(n // SC_WINDOW,),
            in_specs=[pl.BlockSpec((SC_WINDOW, SUBLANES, LANES), lambda i: (i, 0, 0))]
                     + [pl.BlockSpec((1, LANES), lambda i: (i, 0))] * TOP_K,
            out_specs=[],
            core_axis_name=("core", "subcore"),
            dimension_semantics=(pltpu.PARALLEL,),
        )(tiles_hbm, *idx_hbm)

    idx_rows = jnp.pad(idx_by_choice.reshape(TOP_K, n // SC_WINDOW, SC_WINDOW),
                       ((0, 0), (0, 0), (0, LANES - SC_WINDOW)))
    return scatter(tiles, *[idx_rows[k] for k in range(TOP_K)])


def _combine_kernel(h_ref, y0_ref, y1_ref, y2_ref, y3_ref, gate_ref, pp_ref, ps_ref, wpg_ref, wpp_ref,
                    g_ref, b_ref, *refs, n_ptiles, tile0, n_out):
    outs = refs[-n_out:]
    outp_ref, outs_ref = outs[0], outs[-1]
    tm = h_ref.shape[0]
    gate = gate_ref[...]
    moe = _load_token_tiles(y0_ref, (), tm) * gate[:, 0:1]
    for k, y_ref in enumerate((y1_ref, y2_ref, y3_ref), start=1):
        moe = moe + _load_token_tiles(y_ref, (), tm) * gate[:, k:k + 1]
    h2 = _layer_norm(DN_ALPHA * h_ref[...] + moe, g_ref[...], b_ref[...])
    embed_gate = _sigmoid(_dot(h2, wpg_ref[...]))

    def finish(p_ref, out_ref):
        out_ref[...] = h2 + embed_gate * _dot(p_ref[...], wpp_ref[...])

    is_prompt = tile0 + pl.program_id(0) < n_ptiles
    pl.when(is_prompt)(functools.partial(finish, pp_ref, outp_ref))
    pl.when(jnp.logical_not(is_prompt))(functools.partial(finish, ps_ref, outs_ref))


def _combine(h, y, tile0, gate, p_prompt, p_sample, layer, n_p, split, wpg_bf16, wpp_bf16, g_row, b_row,
             into=None):
    n, d = h.shape
    e = p_prompt.shape[1]
    m = y.shape[0] // (TOP_K * TM_TOK * SUBLANES)
    n_ptiles = n_p // TM_TOK
    n_stiles = (n - n_p) // TM_TOK
    row = lambda i: (tile0 + i, 0)
    row_p = lambda i: (jnp.minimum(tile0 + i, n_ptiles - 1), 0)
    row_s = lambda i: (jnp.maximum(tile0 + i - n_ptiles, 0), 0)
    fixed = lambda i: (0, 0)
    choice = lambda k: pl.BlockSpec((TM_TOK * SUBLANES, LANES), lambda i: (k * m + i, 0))
    if split:
        out_specs = [pl.BlockSpec((TM_TOK, d), row_p), pl.BlockSpec((TM_TOK, d), row_s)]
        out_shape = [jax.ShapeDtypeStruct((n_p, d), F32), jax.ShapeDtypeStruct((n - n_p, d), F32)]
    else:
        out_specs = [pl.BlockSpec((TM_TOK, d), row)]
        out_shape = [jax.ShapeDtypeStruct((n, d), F32)]
    into = () if into is None else tuple(into)
    n_in = 12
    return pl.pallas_call(
        functools.partial(_combine_kernel, n_ptiles=n_ptiles, tile0=tile0, n_out=len(out_specs)),
        grid=(m,),
        in_specs=[pl.BlockSpec((TM_TOK, d), row)] + [choice(k) for k in range(TOP_K)]
                 + [pl.BlockSpec((TM_TOK, LANES), row),
                    pl.BlockSpec((TM_TOK, e), lambda i: (layer * n_ptiles + row_p(i)[0], 0)),
                    pl.BlockSpec((TM_TOK, e), lambda i: (layer * n_stiles + row_s(i)[0], 0)),
                    pl.BlockSpec((d, d), fixed), pl.BlockSpec((e, d), fixed), pl.BlockSpec((1, d), fixed),
                    pl.BlockSpec((1, d), fixed)] + [pl.BlockSpec(memory_space=pl.ANY)] * len(into),
        out_specs=out_specs,
        out_shape=out_shape,
        input_output_aliases={n_in + j: j for j in range(len(into))},
        compiler_params=_cparams(("arbitrary",)),
        name="moe_combine_ln_embed",
    )(h, y, y, y, y, gate, p_prompt, p_sample, wpg_bf16, wpp_bf16, g_row, b_row, *into)


def _layer_tail(h, ht, p_prompt, p_sample, layer, n_p, split, g2, b2, w_r, b_r, w_up, b_up, w_dn, b_dn,
                w_pg, w_pp):
    n, d = h.shape
    wr_pad = jnp.pad(w_r, ((0, 0), (0, LANES - N_EXPERTS)))
    br_row = jnp.pad(b_r, (0, LANES - N_EXPERTS))[None]
    n_asg = n * TOP_K
    n_blocks = n_asg // MOE_BLOCK + N_EXPERTS
    assert n_asg % MOE_BLOCK == 0 and n % SC_WINDOW == 0
    n_slots = n_blocks * MOE_BLOCK
    idx, gate, rank, pstart, tab = _router(h, wr_pad, br_row, n_blocks)
    experts = jnp.arange(N_EXPERTS, dtype=I32)[:, None, None]
    group_start = jnp.sum(jnp.where(idx[None, :TOP_K] == experts, pstart[0, :N_EXPERTS, None, None], 0), axis=0)
    slot_by_choice = group_start + rank[:TOP_K]
    xb = _sc_scatter_tiles(ht.reshape(n, SUBLANES, LANES), slot_by_choice, n_slots)
    yb = _experts(xb.reshape(n_slots * SUBLANES, LANES), tab[:n_blocks, 0], tab[0, 2:3], tab[:n_blocks, 1],
                  tab[:n_blocks, 3], layer, w_up, b_up[:, :, None, :], w_dn, b_dn[:, :, None, :])
    n_tiles = n // TM_TOK
    out = None
    for tile0, tile1 in ((0, n_tiles // 2), (n_tiles // 2, n_tiles)):
        idx_range = slot_by_choice[:, tile0 * TM_TOK:tile1 * TM_TOK].reshape(-1)
        y = _sc_gather_tiles(yb.reshape(n_slots, SUBLANES, LANES), idx_range)
        out = _combine(h, y.reshape(-1, LANES), tile0, gate, p_prompt, p_sample, layer, n_p, split,
                       w_pg.astype(BF16), w_pp.astype(BF16), g2[None], b2[None], into=out)
    return out if split else out[0]


def _lane_row(v, lane0):
    return jnp.zeros((1, LANES), F32).at[0, lane0:lane0 + v.shape[0]].set(v.astype(F32))


def kernel(x_prompt, x_sample, cache_fox_k, cache_fox_v, cache_fox_logf, state_gdn, state_gdn_conv,
           cache_pool, p_prompt, p_sample, w_in_ab, b_fgate, gdn_a_log, gdn_dt_bias, gdn_conv_w,
           gdn_norm_g, w_out_ab, pool_w, pool_scale, w_out_pool, ln1_g, ln1_b, ln2_g, ln2_b,
           w_router, b_router, w_expert_up, b_expert_up, w_expert_down, b_expert_down,
           w_ple_gate, w_ple_proj):
    n_pb, seq, d = x_prompt.shape
    n_sb, dseq, _ = x_sample.shape
    past = cache_fox_k.shape[2]
    assert n_pb == 1 and dseq == CHUNK and past % dseq == 0 and seq % TQ == 0
    assert d == SUBLANES * LANES
    n_p = n_pb * seq
    n_s = n_sb * dseq
    n = n_p + n_s
    assert n_p % TM_TOK == 0 and n_s % TM_TOK == 0
    n_layers = p_prompt.shape[0]
    pp_all = p_prompt.reshape(n_layers * n_p, -1)
    ps_all = p_sample.reshape(n_layers * n_s, -1)

    def tail(h, ht, i, split):
        return _layer_tail(h, ht, pp_all, ps_all, i, n_p, split, ln2_g[i], ln2_b[i], w_router[i], b_router[i],
                           w_expert_up, b_expert_up, w_expert_down, b_expert_down, w_ple_gate[i],
                           w_ple_proj[i])

    w_in = w_in_ab[0]
    n_small = 3 * N_HEADS
    ff0 = 3 * WIDTH
    gq0 = ff0 + N_HEADS
    ga0 = gq0 + 4 * WIDTH
    w_small = jnp.concatenate([w_in[:, ff0:gq0], w_in[:, ga0:ga0 + 2 * N_HEADS],
                               jnp.zeros((d, LANES - n_small), F32)], axis=1)
    w_all = jnp.concatenate([w_in[:, :ff0], w_in[:, gq0:ga0], w_small], axis=1)
    bf_row = _lane_row(b_fgate[0], LANE_F)
    up = _proj(x_prompt.reshape(n_p, d), w_all.astype(BF16), bf_row, U_COLS, False)
    us = _proj(x_sample.reshape(n_s, d), w_all, bf_row, LANES, True)

    u3 = up[None]
    us3 = us.reshape(n_sb, dseq, U_COLS)
    cq_p, ck_p = _cumsum(u3, 1, n_p, TK, COL_S // LANES)
    lf_s = jnp.concatenate(
        [jnp.pad(cache_fox_logf[0].astype(F32), ((0, 0), (0, 0), (0, LANES - N_HEADS))),
         us3[:, :, COL_S:]], axis=1)
    cq_s, ck_s = _cumsum(lf_s, n_sb, past + dseq, past + dseq, 0)
    assert n // TM_TOK >= 2

    of_p = _fox(u3, lambda b, i: (0, i, COL_Q // WIDTH), u3, lambda b, j: (0, j, COL_K // WIDTH),
                u3, lambda b, j: (0, j, COL_V // WIDTH), cq_p, lambda b, i: (0, i, 0),
                ck_p, lambda b, j: (0, 0, j), 1, n_p // TQ, TQ, TK, 0)
    k_all = jnp.concatenate([cache_fox_k[0].reshape(n_sb, past, WIDTH), us3[:, :, COL_K:COL_K + WIDTH]], axis=1)
    v_all = jnp.concatenate([cache_fox_v[0].reshape(n_sb, past, WIDTH), us3[:, :, COL_V:COL_V + WIDTH]], axis=1)
    of_s = _fox(us3, lambda b, i: (b, 0, COL_Q // WIDTH), k_all, lambda b, j: (b, 0, 0),
                v_all, lambda b, j: (b, 0, 0), cq_s, lambda b, i: (b, past // dseq, 0),
                ck_s, lambda b, j: (b, 0, 0), n_sb, 1, dseq, past + dseq, past, hi=True)

    gw = 3 * WIDTH
    conv_w = jnp.pad(gdn_conv_w[0], ((0, SUBLANES - CONV_WIDTH), (0, 0)))
    gdn_args = (conv_w, _lane_row(gdn_a_log[0], LANE_A), _lane_row(gdn_dt_bias[0], LANE_A),
                gdn_norm_g[0][None])
    n_pstep = n_p // GDN_ROWS
    ends = lambda k: (jnp.asarray((np.arange(k) == 0).astype(np.int32)),
                      jnp.asarray((np.arange(k) == k - 1).astype(np.int32)))
    og_p, st_p = _gdn(up.reshape(n_pstep, GDN_ROWS, U_COLS), False, *ends(n_pstep),
                      jnp.zeros((n_pstep,), I32), jnp.zeros((1, SUBLANES, gw), F32), gdn_args[0],
                      jnp.zeros((1, N_HEADS, HEAD_DIM, HEAD_DIM), F32), *gdn_args[1:])
    conv_past = jnp.pad(state_gdn_conv[0].astype(F32), ((0, 0), (SUBLANES - (CONV_WIDTH - 1), 0), (0, 0)))
    ones = jnp.ones((n_sb,), I32)
    og_s, st_s = _gdn(us3, True, ones, ones, jnp.arange(n_sb, dtype=I32), conv_past, gdn_args[0],
                      state_gdn[0].astype(F32), *gdn_args[1:])
    h, ht = _outproj_ln((x_prompt.reshape(n_p, d), of_p.reshape(n_p, WIDTH), og_p.reshape(n_p, WIDTH)),
                        (x_sample.reshape(n_s, d), of_s.reshape(n_s, WIDTH), og_s.reshape(n_s, WIDTH)),
                        w_out_ab[0], ln1_g[0][None], ln1_b[0][None])
    x1 = tail(h, ht, 0, False)

    pool_args = (pool_w[0].astype(BF16), pool_scale[0][None], w_out_pool[0].astype(BF16),
                 ln1_g[1][None], ln1_b[1][None])
    ratio = TM_TOK // POOL_HALO
    x1_halo = x1.reshape(n // POOL_HALO, POOL_HALO, d)
    h_pool = _pool_ln(x1, lambda i: (i, 0), x1_halo, lambda i: (jnp.maximum(i * ratio - 1, 0), 0, 0),
                      n_p // TM_TOK, TM_TOK, 0, True, *pool_args)
    cache16 = jnp.pad(cache_pool[0].astype(F32), ((0, 0), (POOL_HALO - POOL_STATE, 0), (0, 0)))
    h, ht = _pool_ln(x1, lambda i: (n_p // dseq + i, 0), cache16, lambda i: (i, 0, 0),
                     n_sb, dseq, past, False, *pool_args, into=h_pool)
    x2_p, x2_s = tail(h, ht, 1, True)

    return (x2_p.reshape(n_pb, seq, d), x2_s.reshape(n_sb, dseq, d),
            up[:, COL_K:COL_K + WIDTH].reshape(1, n_pb, seq, N_HEADS, HEAD_DIM),
            up[:, COL_V:COL_V + WIDTH].reshape(1, n_pb, seq, N_HEADS, HEAD_DIM),
            up[:, COL_S:COL_S + N_HEADS].reshape(1, n_pb, seq, N_HEADS),
            st_p.reshape(1, n_pb, N_HEADS, HEAD_DIM, HEAD_DIM),
            up[seq - (CONV_WIDTH - 1):, COL_G:COL_G + gw].reshape(1, n_pb, CONV_WIDTH - 1, gw),
            x1[n_p - POOL_STATE:n_p].reshape(1, n_pb, POOL_STATE, d),
            us[:, COL_K:COL_K + WIDTH].reshape(1, n_sb, dseq, N_HEADS, HEAD_DIM),
            us[:, COL_V:COL_V + WIDTH].reshape(1, n_sb, dseq, N_HEADS, HEAD_DIM),
            us[:, COL_S:COL_S + N_HEADS].reshape(1, n_sb, dseq, N_HEADS),
            st_s.reshape(1, n_sb, N_HEADS, HEAD_DIM, HEAD_DIM),
            us[:, COL_G:COL_G + gw].reshape(n_sb, dseq, gw)[:, dseq - (CONV_WIDTH - 1):].reshape(
                1, n_sb, CONV_WIDTH - 1, gw),
            x1[n_p:].reshape(n_sb, dseq, d)[:, dseq - POOL_STATE:].reshape(1, n_sb, POOL_STATE, d))
```

```python
import functools

import numpy as np
import jax
import jax.numpy as jnp
from jax import lax
from jax.experimental import pallas as pl
from jax.experimental.pallas import tpu as pltpu
from jax.experimental.pallas import tpu_sc as plsc

F32 = jnp.float32
BF16 = jnp.bfloat16
I32 = jnp.int32
HIGHEST = lax.Precision.HIGHEST

LANES = 128
SUBLANES = 8
VMEM_LIMIT = 56 * 1024 * 1024

HEAD_DIM = 128
N_HEADS = 4
WIDTH = N_HEADS * HEAD_DIM
CHUNK = 64
CONV_WIDTH = 4
POOL_WINDOWS = (2, 4, 8, 16)
POOL_HALO = 16
POOL_STATE = 15
N_EXPERTS = 32
TOP_K = 4
SWIGLU_LIMIT = 7.0
SWIGLU_ALPHA = 1.702
DEPTH = 2
DN_ALPHA = (2 * DEPTH) ** 0.25
LN_EPS = 1e-5
NORM_EPS = 1e-6
NEG_INF = -1e30
LOG2E = 1.4426950408889634

COL_Q, COL_K, COL_V = 0, WIDTH, 2 * WIDTH
COL_G = 3 * WIDTH
COL_Z = 6 * WIDTH
COL_S = 7 * WIDTH
U_COLS = COL_S + LANES
LANE_F, LANE_A, LANE_B = 0, N_HEADS, 2 * N_HEADS

TM_PROJ = 256
TM_TOK = 512
SC_WINDOW = 32
MOE_BLOCK = 512
TQ = 1024
TK = 512
GDN_ROWS = 256
GDN_CHUNK = CHUNK


def _cparams(sem):
    return pltpu.CompilerParams(dimension_semantics=sem, vmem_limit_bytes=VMEM_LIMIT)


def _softplus(x):
    return jnp.maximum(x, 0.0) + jnp.log1p(jnp.exp(-jnp.abs(x)))


def _sigmoid(x):
    return 1.0 / (1.0 + jnp.exp(-x))


def _silu(x):
    return x * _sigmoid(x)


def _layer_norm(y, g, b):
    mu = jnp.mean(y, axis=-1, keepdims=True)
    yc = y - mu
    var = jnp.mean(yc * yc, axis=-1, keepdims=True)
    return yc * lax.rsqrt(var + LN_EPS) * g + b


def _dot_general(a, b, dims, hi):
    if hi:
        return lax.dot_general(a.astype(F32), b.astype(F32), (dims, ((), ())), precision=HIGHEST,
                               preferred_element_type=F32)
    return lax.dot_general(a.astype(BF16), b.astype(BF16), (dims, ((), ())), preferred_element_type=F32)


def _dot(a, b, hi=False):
    return _dot_general(a, b, ((1,), (0,)), hi)


def _dot_nt(a, b, hi=False):
    return _dot_general(a, b, ((1,), (1,)), hi)


def _dot_tn(a, b, hi=False):
    return _dot_general(a, b, ((0,), (0,)), hi)


def _spread_lanes(x, width):
    if width % LANES == 0:
        return jnp.concatenate([x] * (width // LANES), axis=1)
    return jnp.broadcast_to(x[:, 0:1], (x.shape[0], width))


def _load_token_tiles(ref, lead, n_tok):
    return jnp.concatenate([ref[(*lead, pl.ds(j, n_tok, stride=SUBLANES), slice(None))]
                            for j in range(SUBLANES)], axis=1)


def _store_token_tiles(ref, lead, x):
    for j in range(SUBLANES):
        ref[(*lead, pl.ds(j, x.shape[0], stride=SUBLANES), slice(None))] = x[:, j * LANES:(j + 1) * LANES]


def _lanes_to_rows(x, lane0):
    r = lax.broadcasted_iota(I32, (SUBLANES, LANES), 0)
    c = lax.broadcasted_iota(I32, (SUBLANES, LANES), 1)
    sel = (c == r + lane0).astype(F32)
    return lax.dot_general(sel, x, (((1,), (1,)), ((), ())), precision=HIGHEST,
                           preferred_element_type=F32)


def _proj_kernel(x_ref, w_ref, bf_ref, u_ref, *, hi):
    u = _dot(x_ref[...], w_ref[...], hi)
    u_ref[...] = u

    @pl.when(pl.program_id(1) == pl.num_programs(1) - 1)
    def _():
        small = u[:, u.shape[1] - LANES:]
        lane = lax.broadcasted_iota(I32, small.shape, 1)
        logf = -_softplus(-(small + bf_ref[...]))
        u_ref[:, u.shape[1] - LANES:] = jnp.where(lane < LANE_A, logf, small)


def _proj(x, w, bf_row, tm, tn, hi):
    n, d = x.shape
    m = w.shape[1]
    return pl.pallas_call(
        functools.partial(_proj_kernel, hi=hi),
        grid=(n // tm, m // tn),
        in_specs=[pl.BlockSpec((tm, d), lambda i, j: (i, 0)),
                  pl.BlockSpec((d, tn), lambda i, j: (0, j)),
                  pl.BlockSpec((1, LANES), lambda i, j: (0, 0))],
        out_specs=pl.BlockSpec((tm, tn), lambda i, j: (i, j)),
        out_shape=jax.ShapeDtypeStruct((n, m), F32),
        compiler_params=_cparams(("parallel", "parallel")),
        name="in_proj",
    )(x, w, bf_row)


def _cumsum_kernel(lf_ref, crep_ref, crow_ref, carry_ref):
    @pl.when(pl.program_id(1) == 0)
    def _():
        carry_ref[...] = jnp.zeros_like(carry_ref)

    lf = lf_ref[0]
    t = lf.shape[0]
    r = lax.broadcasted_iota(I32, (t, t), 0)
    c = lax.broadcasted_iota(I32, (t, t), 1)
    tril = (c <= r).astype(F32)
    cs = jnp.dot(tril, lf, precision=HIGHEST, preferred_element_type=F32) + carry_ref[0:1, :]
    carry_ref[...] = jnp.broadcast_to(cs[t - 1:t, :], carry_ref.shape)
    c2 = cs * LOG2E
    crow_ref[0] = _lanes_to_rows(c2, LANE_F)
    for h in range(N_HEADS):
        crep_ref[0, :, h * HEAD_DIM:(h + 1) * HEAD_DIM] = jnp.broadcast_to(
            c2[:, LANE_F + h:LANE_F + h + 1], (t, HEAD_DIM))


def _cumsum(arr, n_batch, length, tl, col_block):
    return pl.pallas_call(
        _cumsum_kernel,
        grid=(n_batch, length // tl),
        in_specs=[pl.BlockSpec((1, tl, LANES), lambda b, j: (b, j, col_block))],
        out_specs=[pl.BlockSpec((1, tl, WIDTH), lambda b, j: (b, j, 0)),
                   pl.BlockSpec((1, SUBLANES, tl), lambda b, j: (b, 0, j))],
        out_shape=[jax.ShapeDtypeStruct((n_batch, length, WIDTH), F32),
                   jax.ShapeDtypeStruct((n_batch, SUBLANES, length), F32)],
        scratch_shapes=[pltpu.VMEM((SUBLANES, LANES), F32)],
        compiler_params=_cparams(("parallel", "arbitrary")),
        name="logf_cumsum",
    )(arr)


def _fox_kernel(qi_ref, kj_ref, last_ref, q_ref, k_ref, v_ref, cq_ref, ck_ref, o_ref,
                m_ref, l_ref, acc_ref, *, tq, tk, past, hi):
    s_idx = pl.program_id(1)
    qi = qi_ref[s_idx]
    kj = kj_ref[s_idx]

    @pl.when(kj == 0)
    def _():
        m_ref[...] = jnp.full_like(m_ref, NEG_INF)
        l_ref[...] = jnp.zeros_like(l_ref)
        acc_ref[...] = jnp.zeros_like(acc_ref)

    def update(masked):
        if masked:
            q_pos = past + qi * tq + lax.broadcasted_iota(I32, (tq, tk), 0)
            k_pos = kj * tk + lax.broadcasted_iota(I32, (tq, tk), 1)
            visible = k_pos <= q_pos
        for h in range(N_HEADS):
            cols = slice(h * HEAD_DIM, (h + 1) * HEAD_DIM)
            q = q_ref[0, :, cols] * (HEAD_DIM ** -0.5 * LOG2E)
            t = _dot_nt(q, k_ref[0, :, cols], hi) - ck_ref[0, h:h + 1, :]
            if masked:
                t = jnp.where(visible, t, NEG_INF)
            cq = cq_ref[0, :, cols]
            m_prev = m_ref[h]
            m_new = jnp.maximum(m_prev, jnp.max(t, axis=-1, keepdims=True) + cq)
            p = jnp.exp2(t - _spread_lanes(m_new - cq, tk))
            alpha = jnp.exp2(m_prev - m_new)
            l_ref[h] = alpha * l_ref[h] + jnp.sum(p, axis=-1, keepdims=True)
            acc_ref[:, cols] = alpha * acc_ref[:, cols] + _dot(p, v_ref[0, :, cols], hi)
            m_ref[h] = m_new

    crosses_diagonal = kj * tk + (tk - 1) > past + qi * tq
    pl.when(crosses_diagonal)(functools.partial(update, True))
    pl.when(jnp.logical_not(crosses_diagonal))(functools.partial(update, False))

    @pl.when(last_ref[s_idx] == 1)
    def _():
        for h in range(N_HEADS):
            cols = slice(h * HEAD_DIM, (h + 1) * HEAD_DIM)
            o_ref[0, :, cols] = acc_ref[:, cols] / l_ref[h]


def _fox_schedule(n_q, tq, tk, past):
    qi, kj, last = [], [], []
    for i in range(n_q):
        hi = (past + (i + 1) * tq - 1) // tk
        for j in range(hi + 1):
            qi.append(i)
            kj.append(j)
            last.append(1 if j == hi else 0)
    return (jnp.asarray(np.array(qi, np.int32)), jnp.asarray(np.array(kj, np.int32)),
            jnp.asarray(np.array(last, np.int32)))


def _fox(q_arr, q_map, k_arr, k_map, v_arr, v_map, cq_arr, cq_map, ck_arr, ck_map,
         n_batch, n_q, tq, tk, past, hi=False):
    qi, kj, last = _fox_schedule(n_q, tq, tk, past)
    n_steps = int(qi.shape[0])
    spec = lambda shape, fn, tab: pl.BlockSpec(shape, lambda b, s, qi_r, kj_r, la_r: fn(b, (qi_r if tab == 'q' else kj_r)[s]))
    return pl.pallas_call(
        functools.partial(_fox_kernel, tq=tq, tk=tk, past=past, hi=hi),
        grid_spec=pltpu.PrefetchScalarGridSpec(
            num_scalar_prefetch=3,
            grid=(n_batch, n_steps),
            in_specs=[spec((1, tq, WIDTH), q_map, 'q'),
                      spec((1, tk, WIDTH), k_map, 'k'),
                      spec((1, tk, WIDTH), v_map, 'k'),
                      spec((1, tq, WIDTH), cq_map, 'q'),
                      spec((1, SUBLANES, tk), ck_map, 'k')],
            out_specs=spec((1, tq, WIDTH), lambda b, i: (b, i, 0), 'q'),
            scratch_shapes=[pltpu.VMEM((N_HEADS, tq, HEAD_DIM), F32),
                            pltpu.VMEM((N_HEADS, tq, HEAD_DIM), F32),
                            pltpu.VMEM((tq, WIDTH), F32)]),
        out_shape=jax.ShapeDtypeStruct((n_batch, n_q * tq, WIDTH), F32),
        compiler_params=_cparams(("parallel", "arbitrary")),
        name="fox_attention",
    )(qi, kj, last, q_arr, k_arr, v_arr, cq_arr, ck_arr)


def _gdn_kernel(first_ref, last_ref, seq_ref,
                pre_ref, z_ref, sm_ref, cpast_ref, convw_ref, s0_ref, alog_ref, dtb_ref, ng_ref,
                o_ref, sout_ref, stage_ref, s_ref, *, rows, chunk, hi):
    dot, dot_nt, dot_tn = (functools.partial(f, hi=hi) for f in (_dot, _dot_nt, _dot_tn))
    step = pl.program_id(0)
    halo = SUBLANES
    n_chunks = rows // chunk

    @pl.when(first_ref[step] == 1)
    def _():
        stage_ref[0:halo, :] = cpast_ref[0]
        s_ref[...] = s0_ref[0]

    stage_ref[halo:halo + rows, :] = pre_ref[0]
    conv = stage_ref[halo:halo + rows, :] * convw_ref[CONV_WIDTH - 1:CONV_WIDTH, :]
    for j in range(1, CONV_WIDTH):
        conv = conv + (stage_ref[halo - j:halo - j + rows, :]
                       * convw_ref[CONV_WIDTH - 1 - j:CONV_WIDTH - j, :])
    stage_ref[0:halo, :] = stage_ref[rows:rows + halo, :]
    act = _silu(conv)

    small = sm_ref[0]
    beta_all = _sigmoid(small)
    g_all = -jnp.exp(alog_ref[...]) * _softplus(small + dtb_ref[...])
    r = lax.broadcasted_iota(I32, (rows, rows), 0)
    c = lax.broadcasted_iota(I32, (rows, rows), 1)
    same_chunk = (r // chunk) == (c // chunk)
    incl = same_chunk & (c <= r)
    strict = same_chunk & (c < r)
    eye = (c == r).astype(F32)
    gc_all = jnp.dot(incl.astype(F32), g_all, precision=HIGHEST, preferred_element_type=F32)
    gc_rows = _lanes_to_rows(gc_all, LANE_A)

    heads = range(N_HEADS)
    head_cols = [slice(h * HEAD_DIM, (h + 1) * HEAD_DIM) for h in heads]
    q, k, gc, decay, kb, vb, low = [], [], [], [], [], [], []
    for h in heads:
        qh = act[:, h * HEAD_DIM:(h + 1) * HEAD_DIM]
        kh = act[:, WIDTH + h * HEAD_DIM:WIDTH + (h + 1) * HEAD_DIM]
        vh = act[:, 2 * WIDTH + h * HEAD_DIM:2 * WIDTH + (h + 1) * HEAD_DIM]
        q.append(qh * lax.rsqrt(jnp.sum(qh * qh, axis=-1, keepdims=True) + NORM_EPS) * (HEAD_DIM ** -0.5))
        k.append(kh * lax.rsqrt(jnp.sum(kh * kh, axis=-1, keepdims=True) + NORM_EPS))
        beta = beta_all[:, LANE_B + h:LANE_B + h + 1]
        gc.append(gc_all[:, LANE_A + h:LANE_A + h + 1])
        diff = gc[h] - gc_rows[h:h + 1, :]
        decay.append(jnp.where(incl, jnp.exp(jnp.where(incl, diff, 0.0)), 0.0))
        kb.append(k[h] * beta)
        vb.append(vh * beta)
    for h in heads:
        low.append(jnp.where(strict, dot_nt(kb[h], k[h]) * decay[h], 0.0))
    inv = [eye - low[h] for h in heads]
    pw = [dot(low[h], low[h]) for h in heads]
    n_sq = chunk.bit_length() - 2
    for it in range(n_sq):
        inv = [inv[h] + dot(inv[h], pw[h]) for h in heads]
        if it + 1 < n_sq:
            pw = [dot(pw[h], pw[h]) for h in heads]
    egc = [jnp.exp(gc[h]) for h in heads]
    uw = [dot(inv[h], jnp.concatenate([vb[h], kb[h] * egc[h]], axis=1)) for h in heads]
    intra = [jnp.where(incl, dot_nt(q[h], k[h]) * decay[h], 0.0) for h in heads]
    qd = [q[h] * egc[h] for h in heads]
    g_last = [[gc[h][(g + 1) * chunk - 1:(g + 1) * chunk, :] for g in range(n_chunks)] for h in heads]
    kd = [k[h] * jnp.exp(jnp.concatenate([jnp.broadcast_to(gl, (chunk, 1)) for gl in g_last[h]], axis=0)
                         - gc[h]) for h in heads]
    state = [s_ref[h] for h in heads]
    v_new = [[] for _ in heads]
    for g in range(n_chunks):
        rs = slice(g * chunk, (g + 1) * chunk)
        for h in heads:
            v_new[h].append(uw[h][rs, :HEAD_DIM] - dot(uw[h][rs, HEAD_DIM:], state[h]))
        for h in heads:
            v_rows = jnp.concatenate(
                v_new[h] + [jnp.zeros((rows - (g + 1) * chunk, HEAD_DIM), F32)] * (g + 1 < n_chunks), axis=0)
            o = dot(qd[h][rs, :], state[h]) + dot(intra[h][rs, :], v_rows)
            state[h] = state[h] * jnp.exp(g_last[h][g]) + dot_tn(kd[h][rs, :], v_new[h][g])
            o = (o * lax.rsqrt(jnp.mean(o * o, axis=-1, keepdims=True) + NORM_EPS)
                 * ng_ref[...] * _silu(z_ref[0, rs, head_cols[h]]))
            o_ref[0, rs, head_cols[h]] = o
    for h in heads:
        s_ref[h] = state[h]

    @pl.when(last_ref[step] == 1)
    def _():
        sout_ref[0] = s_ref[...]


def _gdn(u_view, hi, first, last, seq, conv_past, conv_w, s0, alog_row, dtb_row, ng_row):
    n_steps, rows, _ = u_view.shape
    n_seq = s0.shape[0]
    gw = 3 * WIDTH
    return pl.pallas_call(
        functools.partial(_gdn_kernel, rows=rows, chunk=min(rows, GDN_CHUNK), hi=hi),
        grid_spec=pltpu.PrefetchScalarGridSpec(
            num_scalar_prefetch=3,
            grid=(n_steps,),
            in_specs=[pl.BlockSpec((1, rows, gw), lambda s, f, l, q: (s, 0, COL_G // gw)),
                      pl.BlockSpec((1, rows, WIDTH), lambda s, f, l, q: (s, 0, COL_Z // WIDTH)),
                      pl.BlockSpec((1, rows, LANES), lambda s, f, l, q: (s, 0, COL_S // LANES)),
                      pl.BlockSpec((1, SUBLANES, gw), lambda s, f, l, q: (q[s], 0, 0)),
                      pl.BlockSpec((SUBLANES, gw), lambda s, f, l, q: (0, 0)),
                      pl.BlockSpec((1, N_HEADS, HEAD_DIM, HEAD_DIM), lambda s, f, l, q: (q[s], 0, 0, 0)),
                      pl.BlockSpec((1, LANES), lambda s, f, l, q: (0, 0)),
                      pl.BlockSpec((1, LANES), lambda s, f, l, q: (0, 0)),
                      pl.BlockSpec((1, LANES), lambda s, f, l, q: (0, 0))],
            out_specs=[pl.BlockSpec((1, rows, WIDTH), lambda s, f, l, q: (s, 0, 0)),
                       pl.BlockSpec((1, N_HEADS, HEAD_DIM, HEAD_DIM), lambda s, f, l, q: (q[s], 0, 0, 0))],
            scratch_shapes=[pltpu.VMEM((rows + SUBLANES, gw), F32),
                            pltpu.VMEM((N_HEADS, HEAD_DIM, HEAD_DIM), F32)]),
        out_shape=[jax.ShapeDtypeStruct((n_steps, rows, WIDTH), F32),
                   jax.ShapeDtypeStruct((n_seq, N_HEADS, HEAD_DIM, HEAD_DIM), F32)],
        compiler_params=_cparams(("arbitrary",)),
        name="gated_deltanet",
    )(first, last, seq, u_view, u_view, u_view, conv_past, conv_w, s0, alog_row, dtb_row, ng_row)


def _outproj_ln_kernel(xp_ref, ofp_ref, ogp_ref, xs_ref, ofs_ref, ogs_ref, w_ref, wf_ref, g_ref, b_ref,
                       h_ref, ht_ref, *, hi_from):
    def run(hi, x_ref, of_ref, og_ref):
        w = wf_ref if hi else w_ref
        mix = _dot(of_ref[...], w[0:WIDTH, :], hi) + _dot(og_ref[...], w[WIDTH:2 * WIDTH, :], hi)
        h = _layer_norm(DN_ALPHA * x_ref[...] + mix, g_ref[...], b_ref[...])
        h_ref[...] = h
        _store_token_tiles(ht_ref, (), h)

    pl.when(pl.program_id(0) < hi_from)(functools.partial(run, False, xp_ref, ofp_ref, ogp_ref))
    pl.when(pl.program_id(0) >= hi_from)(functools.partial(run, True, xs_ref, ofs_ref, ogs_ref))


def _outproj_ln(prompt, sample, w, g_row, b_row):
    n_p, d = prompt[0].shape
    n = n_p + sample[0].shape[0]
    hi_from = n_p // TM_TOK
    row = lambda i: (i, 0)
    row_p = lambda i: (jnp.minimum(i, hi_from - 1), 0)
    row_s = lambda i: (jnp.maximum(i - hi_from, 0), 0)
    fixed = lambda i: (0, 0)
    group = lambda rows: [pl.BlockSpec((TM_TOK, d), rows), pl.BlockSpec((TM_TOK, WIDTH), rows),
                          pl.BlockSpec((TM_TOK, WIDTH), rows)]
    return pl.pallas_call(
        functools.partial(_outproj_ln_kernel, hi_from=hi_from),
        grid=(n // TM_TOK,),
        in_specs=group(row_p) + group(row_s) + [pl.BlockSpec((2 * WIDTH, d), fixed),
                                                pl.BlockSpec((2 * WIDTH, d), fixed),
                                                pl.BlockSpec((1, d), fixed), pl.BlockSpec((1, d), fixed)],
        out_specs=[pl.BlockSpec((TM_TOK, d), row), pl.BlockSpec((TM_TOK * SUBLANES, LANES), row)],
        out_shape=[jax.ShapeDtypeStruct((n, d), F32), jax.ShapeDtypeStruct((n * SUBLANES, LANES), F32)],
        compiler_params=_cparams(("parallel",)),
        name="out_proj_ln",
    )(*prompt, *sample, w.astype(BF16), w, g_row, b_row)


def _pool_ln_kernel(x_ref, halo_ref, pw_ref, ps_ref, w_ref, g_ref, b_ref, h_ref, ht_ref, stage_ref,
                    lvl_a_ref, lvl_b_ref, *, tm, pos0, zero_first_halo):
    i = pl.program_id(0)
    pad = SUBLANES
    n_rows = POOL_HALO + tm
    for ref in (stage_ref, lvl_a_ref, lvl_b_ref):
        ref[0:pad, :] = jnp.zeros((pad, ref.shape[1]), F32)
    stage_ref[pad:pad + POOL_HALO, :] = halo_ref[0]
    if zero_first_halo:
        @pl.when(i == 0)
        def _():
            stage_ref[pad:pad + POOL_HALO, :] = jnp.zeros((POOL_HALO, stage_ref.shape[1]), F32)
    x = x_ref[...]
    stage_ref[pad + POOL_HALO:pad + n_rows, :] = x
    gdim = x.shape[1] // len(POOL_WINDOWS)
    pos = pos0 + lax.broadcasted_iota(I32, (tm, 1), 0)
    if zero_first_halo:
        pos = pos + i * tm

    def window_sum(cols, win):
        assert win & (win - 1) == 0 and win <= POOL_HALO
        src, src_cols, span, level = stage_ref, cols, 1, 0
        while span < win:
            dst = (lvl_a_ref, lvl_b_ref)[level % 2]
            dst[pad:pad + n_rows, :] = (src[pad:pad + n_rows, src_cols]
                                        + src[pad - span:pad - span + n_rows, src_cols])
            src, src_cols, span, level = dst, slice(None), 2 * span, level + 1
        return src[pad + POOL_HALO:pad + n_rows, src_cols]

    parts = []
    for gi, win in enumerate(POOL_WINDOWS):
        cols = slice(gi * gdim, (gi + 1) * gdim)
        s = window_sum(cols, win)
        cnt = jnp.minimum(pos + 1, win).astype(F32)
        zg = s / cnt - x[:, cols]
        parts.append(_dot(zg, pw_ref[gi]))
    zg = jnp.concatenate(parts, axis=-1) * ps_ref[...]
    mix = _dot(zg, w_ref[...])
    h = _layer_norm(DN_ALPHA * x + mix, g_ref[...], b_ref[...])
    h_ref[...] = h
    _store_token_tiles(ht_ref, (), h)


def _pool_ln_into_kernel(*refs, **kw):
    _pool_ln_kernel(*refs[:7], *refs[9:], **kw)


def _pool_ln(x, x_map, halo_arr, halo_map, n_tiles, tm, pos0, zero_first_halo,
             pw_bf16, ps_row, w_bf16, g_row, b_row, into=None):
    n, d = x.shape
    gdim = d // len(POOL_WINDOWS)
    fixed = lambda i: (0, 0)
    in_specs = [pl.BlockSpec((tm, d), x_map),
                pl.BlockSpec((1, POOL_HALO, d), halo_map),
                pl.BlockSpec((len(POOL_WINDOWS), gdim, gdim), lambda i: (0, 0, 0)),
                pl.BlockSpec((1, d), fixed), pl.BlockSpec((d, d), fixed),
                pl.BlockSpec((1, d), fixed), pl.BlockSpec((1, d), fixed)]
    args = (x, halo_arr, pw_bf16, ps_row, w_bf16, g_row, b_row)
    kw = dict(tm=tm, pos0=pos0, zero_first_halo=zero_first_halo)
    return pl.pallas_call(
        functools.partial(_pool_ln_kernel if into is None else _pool_ln_into_kernel, **kw),
        grid=(n_tiles,),
        in_specs=in_specs + ([] if into is None else [pl.BlockSpec(memory_space=pl.ANY)] * 2),
        out_specs=[pl.BlockSpec((tm, d), x_map),
                   pl.BlockSpec((tm * SUBLANES, LANES), x_map)],
        out_shape=[jax.ShapeDtypeStruct((n, d), F32), jax.ShapeDtypeStruct((n * SUBLANES, LANES), F32)],
        input_output_aliases={} if into is None else {7: 0, 8: 1},
        scratch_shapes=[pltpu.VMEM((SUBLANES + POOL_HALO + tm, d), F32),
                        pltpu.VMEM((SUBLANES + POOL_HALO + tm, gdim), F32),
                        pltpu.VMEM((SUBLANES + POOL_HALO + tm, gdim), F32)],
        compiler_params=_cparams(("arbitrary",)),
        name="pool_mixer_ln",
    )(*args, *(() if into is None else into))


def _router_kernel(h_ref, wr_ref, br_ref, idx_ref, gate_ref, rank_ref, pstart_ref, tab_ref, carry_ref):
    @pl.when(pl.program_id(0) == 0)
    def _():
        carry_ref[...] = jnp.zeros_like(carry_ref)

    tm = h_ref.shape[0]
    lane = lax.broadcasted_iota(I32, (tm, LANES), 1).astype(F32)
    logits = jnp.dot(h_ref[...], wr_ref[...], precision=HIGHEST, preferred_element_type=F32)
    work = jnp.where(lane < N_EXPERTS, logits + br_ref[...], -jnp.inf)
    vals, ids = [], []
    for _ in range(TOP_K):
        m = jnp.max(work, axis=-1, keepdims=True)
        ik = jnp.min(jnp.where(work == m, lane, float(LANES)), axis=-1, keepdims=True)
        vals.append(m)
        ids.append(ik)
        work = jnp.where(lane == ik, -jnp.inf, work)
    exps = [jnp.exp(v - vals[0]) for v in vals]
    denom = exps[0]
    for e in exps[1:]:
        denom = denom + e
    multihot = jnp.zeros((tm, LANES), F32)
    idx_out = jnp.zeros((tm, LANES), F32)
    gate_out = jnp.zeros((tm, LANES), F32)
    for k in range(TOP_K):
        multihot = multihot + (lane == ids[k]).astype(F32)
        idx_out = jnp.where(lane == k, ids[k], idx_out)
        gate_out = jnp.where(lane == k, exps[k] / denom, gate_out)
    r = lax.broadcasted_iota(I32, (tm, tm), 0)
    c = lax.broadcasted_iota(I32, (tm, tm), 1)
    before = _dot((c < r).astype(F32), multihot) + carry_ref[0:1, :]
    rank_out = jnp.zeros((tm, LANES), F32)
    for k in range(TOP_K):
        rk = jnp.sum(jnp.where(lane == ids[k], before, 0.0), axis=-1, keepdims=True)
        rank_out = jnp.where(lane == k, rk, rank_out)
    idx_ref[...] = jnp.transpose(idx_out)[:SUBLANES].astype(I32)
    rank_ref[...] = jnp.transpose(rank_out)[:SUBLANES].astype(I32)
    gate_ref[...] = gate_out
    total = carry_ref[0:1, :] + jnp.sum(multihot, axis=0, keepdims=True)
    carry_ref[...] = jnp.broadcast_to(total, carry_ref.shape)

    @pl.when(pl.program_id(0) == pl.num_programs(0) - 1)
    def _():
        n_rows = tab_ref.shape[0]
        padded = jnp.floor((total + (MOE_BLOCK - 1)) / MOE_BLOCK) * MOE_BLOCK
        rr = lax.broadcasted_iota(I32, (LANES, LANES), 0)
        cc = lax.broadcasted_iota(I32, (LANES, LANES), 1)
        pend = jnp.dot(jnp.broadcast_to(padded, (SUBLANES, LANES)), (rr <= cc).astype(F32),
                       precision=HIGHEST, preferred_element_type=F32)[0:1, :]
        pstart = pend - padded
        n_used = jnp.max(pend, axis=-1, keepdims=True) / MOE_BLOCK
        elane = lax.broadcasted_iota(I32, (n_rows, LANES), 1)
        blk = jnp.minimum(lax.broadcasted_iota(I32, (n_rows, 1), 0).astype(F32), n_used - 1.0) * MOE_BLOCK
        ends_before = jnp.where((elane < N_EXPERTS) & (pend <= blk), 1.0, 0.0)
        block_e = jnp.minimum(jnp.sum(ends_before, axis=-1, keepdims=True), N_EXPERTS - 1.0)
        mine = elane.astype(F32) == block_e
        filled = jnp.sum(jnp.where(mine, total - (blk - pstart), 0.0), axis=-1, keepdims=True)
        block_valid = jnp.clip(filled, 0.0, float(MOE_BLOCK))
        group_end = jnp.sum(jnp.where(mine, pend, 0.0), axis=-1, keepdims=True)
        next_e = jnp.sum(jnp.where((elane < N_EXPERTS) & (pend <= group_end), 1.0, 0.0), axis=-1, keepdims=True)
        tab = jnp.where(elane == 0, block_e, jnp.where(elane == 1, block_valid,
                                                       jnp.where(elane == 2, n_used,
                                                                 jnp.where(elane == 3, next_e, 0.0))))
        tab_ref[...] = tab.astype(I32)
        pstart_ref[...] = jnp.broadcast_to(pstart, pstart_ref.shape).astype(I32)


def _router(h, wr_pad, br_row, n_blocks):
    n, d = h.shape
    row = lambda i: (i, 0)
    col = lambda i: (0, i)
    fixed = lambda i: (0, 0)
    tab_rows = -(-n_blocks // SUBLANES) * SUBLANES
    return pl.pallas_call(
        _router_kernel,
        grid=(n // TM_TOK,),
        in_specs=[pl.BlockSpec((TM_TOK, d), row), pl.BlockSpec((d, LANES), fixed),
                  pl.BlockSpec((1, LANES), fixed)],
        out_specs=[pl.BlockSpec((SUBLANES, TM_TOK), col), pl.BlockSpec((TM_TOK, LANES), row),
                   pl.BlockSpec((SUBLANES, TM_TOK), col), pl.BlockSpec((SUBLANES, LANES), fixed),
                   pl.BlockSpec((tab_rows, LANES), fixed)],
        out_shape=[jax.ShapeDtypeStruct((SUBLANES, n), I32), jax.ShapeDtypeStruct((n, LANES), F32),
                   jax.ShapeDtypeStruct((SUBLANES, n), I32), jax.ShapeDtypeStruct((SUBLANES, LANES), I32),
                   jax.ShapeDtypeStruct((tab_rows, LANES), I32)],
        scratch_shapes=[pltpu.VMEM((SUBLANES, LANES), F32)],
        compiler_params=_cparams(("arbitrary",)),
        name="moe_router",
    )(h, wr_pad, br_row)


def _expert_kernel(be_ref, nu_ref, valid_ref, next_ref, xb_ref, wup_ref, bup_ref, wdn_ref, bdn_ref, yb_ref,
                   wup_f32_ref, wdn_f32_ref, wup_bf_ref, wdn_bf_ref, wsem, *, layer):
    b = pl.program_id(0)

    def fetch(e):
        return (pltpu.make_async_copy(wup_ref.at[layer, e], wup_f32_ref, wsem.at[0]),
                pltpu.make_async_copy(wdn_ref.at[layer, e], wdn_f32_ref, wsem.at[1]))

    @pl.when(b == 0)
    def _():
        for copy in fetch(be_ref[0]):
            copy.start()

    @pl.when(b < nu_ref[0])
    def _():
        @pl.when((b == 0) | (be_ref[b] != be_ref[jnp.maximum(b - 1, 0)]))
        def _():
            for copy in fetch(be_ref[b]):
                copy.wait()
            wup_bf_ref[...] = wup_f32_ref[...].astype(BF16)
            wdn_bf_ref[...] = wdn_f32_ref[...].astype(BF16)

            @pl.when(next_ref[b] < N_EXPERTS)
            def _():
                for copy in fetch(next_ref[b]):
                    copy.start()

        d_exp = wdn_f32_ref.shape[0]

        def run(n_rows):
            x = _load_token_tiles(xb_ref, (), n_rows)
            row = lax.broadcasted_iota(I32, (n_rows, 1), 0)
            x = jnp.where(row < valid_ref[b], x, 0.0)
            hu = jnp.dot(x.astype(BF16), wup_bf_ref[...], preferred_element_type=F32) + bup_ref[0, 0]
            glu = jnp.minimum(hu[:, :d_exp], SWIGLU_LIMIT)
            lin = jnp.clip(hu[:, d_exp:], -SWIGLU_LIMIT, SWIGLU_LIMIT)
            a = glu * _sigmoid(SWIGLU_ALPHA * glu) * (lin + 1.0)
            y = jnp.dot(a.astype(BF16), wdn_bf_ref[...], preferred_element_type=F32) + bdn_ref[0, 0]
            if n_rows < MOE_BLOCK:
                y = jnp.concatenate([y, jnp.zeros((MOE_BLOCK - n_rows, y.shape[1]), F32)], axis=0)
            _store_token_tiles(yb_ref, (), y)

        half = MOE_BLOCK // 2
        pl.when(valid_ref[b] > half)(functools.partial(run, MOE_BLOCK))
        pl.when(valid_ref[b] <= half)(functools.partial(run, half))

    @pl.when(b >= nu_ref[0])
    def _():
        yb_ref[...] = jnp.zeros_like(yb_ref)


def _experts(xb, block_e, n_used, block_valid, next_e, layer, w_up, b_up, w_dn, b_dn):
    d = SUBLANES * LANES
    rows = MOE_BLOCK * SUBLANES
    n_blocks = xb.shape[0] // rows
    d_up = w_up.shape[3]
    d_exp = w_dn.shape[2]
    bsel = lambda b, be, nu, va, ne: (layer, be[b], 0, 0)
    return pl.pallas_call(
        functools.partial(_expert_kernel, layer=layer),
        grid_spec=pltpu.PrefetchScalarGridSpec(
            num_scalar_prefetch=4,
            grid=(n_blocks,),
            in_specs=[pl.BlockSpec((rows, LANES), lambda b, be, nu, va, ne: (jnp.minimum(b, nu[0] - 1), 0)),
                      pl.BlockSpec(memory_space=pl.ANY),
                      pl.BlockSpec((1, 1, 1, d_up), bsel),
                      pl.BlockSpec(memory_space=pl.ANY),
                      pl.BlockSpec((1, 1, 1, d), bsel)],
            out_specs=pl.BlockSpec((rows, LANES), lambda b, be, nu, va, ne: (b, 0)),
            scratch_shapes=[pltpu.VMEM((d, d_up), F32), pltpu.VMEM((d_exp, d), F32),
                            pltpu.VMEM((d, d_up), BF16), pltpu.VMEM((d_exp, d), BF16),
                            pltpu.SemaphoreType.DMA((2,))]),
        out_shape=jax.ShapeDtypeStruct(xb.shape, F32),
        compiler_params=_cparams(("arbitrary",)),
        name="moe_experts",
    )(block_e, n_used, block_valid, next_e, xb, w_up, b_up, w_dn, b_dn)


def _sc_gather_tiles(table, idx):
    m = idx.shape[0]
    mesh = plsc.VectorSubcoreMesh(core_axis_name="core", subcore_axis_name="subcore")

    @functools.partial(pl.kernel, out_type=jax.ShapeDtypeStruct((m, SUBLANES, LANES), table.dtype), mesh=mesh)
    def gather(table_hbm, idx_hbm, out_hbm):
        def window(idx_vmem, out_vmem):
            pltpu.sync_copy(table_hbm.at[idx_vmem.at[0, pl.ds(0, SC_WINDOW)]], out_vmem)

        pltpu.emit_pipeline(
            window,
            grid=(m // SC_WINDOW,),
            in_specs=[pl.BlockSpec((1, LANES), lambda i: (i, 0))],
            out_specs=[pl.BlockSpec((SC_WINDOW, SUBLANES, LANES), lambda i: (i, 0, 0))],
            core_axis_name=("core", "subcore"),
            dimension_semantics=(pltpu.PARALLEL,),
        )(idx_hbm, out_hbm)

    idx_rows = jnp.pad(idx.reshape(m // SC_WINDOW, SC_WINDOW), ((0, 0), (0, LANES - SC_WINDOW)))
    return gather(table, idx_rows)


def _sc_scatter_tiles(tiles, idx_by_choice, n_out):
    n = tiles.shape[0]
    mesh = plsc.VectorSubcoreMesh(core_axis_name="core", subcore_axis_name="subcore")

    @functools.partial(pl.kernel, out_type=jax.ShapeDtypeStruct((n_out, SUBLANES, LANES), tiles.dtype),
                       mesh=mesh)
    def scatter(tiles_hbm, *refs):
        idx_hbm, out_hbm = refs[:TOP_K], refs[TOP_K]

        def window(tiles_vmem, *idx_vmem):
            for k in range(TOP_K):
                pltpu.sync_copy(tiles_vmem, out_hbm.at[idx_vmem[k].at[0, pl.ds(0, SC_WINDOW)]])

        pltpu.emit_pipeline(
            window,
            grid=(n // SC_WINDOW,),
            in_specs=[pl.BlockSpec((SC_WINDOW, SUBLANES, LANES), lambda i: (i, 0, 0))]
                     + [pl.BlockSpec((1, LANES), lambda i: (i, 0))] * TOP_K,
            out_specs=[],
            core_axis_name=("core", "subcore"),
            dimension_semantics=(pltpu.PARALLEL,),
        )(tiles_hbm, *idx_hbm)

    idx_rows = jnp.pad(idx_by_choice.reshape(TOP_K, n // SC_WINDOW, SC_WINDOW),
                       ((0, 0), (0, 0), (0, LANES - SC_WINDOW)))
    return scatter(tiles, *[idx_rows[k] for k in range(TOP_K)])


def _combine_kernel(h_ref, y0_ref, y1_ref, y2_ref, y3_ref, gate_ref, pp_ref, ps_ref, wpg_ref, wpp_ref,
                    g_ref, b_ref, *outs, n_ptiles):
    outp_ref, outs_ref = outs[0], outs[-1]
    tm = h_ref.shape[0]
    gate = gate_ref[...]
    moe = _load_token_tiles(y0_ref, (), tm) * gate[:, 0:1]
    for k, y_ref in enumerate((y1_ref, y2_ref, y3_ref), start=1):
        moe = moe + _load_token_tiles(y_ref, (), tm) * gate[:, k:k + 1]
    h2 = _layer_norm(DN_ALPHA * h_ref[...] + moe, g_ref[...], b_ref[...])
    embed_gate = _sigmoid(_dot(h2, wpg_ref[...]))

    def finish(p_ref, out_ref):
        out_ref[...] = h2 + embed_gate * _dot(p_ref[...], wpp_ref[...])

    is_prompt = pl.program_id(0) < n_ptiles
    pl.when(is_prompt)(functools.partial(finish, pp_ref, outp_ref))
    pl.when(jnp.logical_not(is_prompt))(functools.partial(finish, ps_ref, outs_ref))


def _combine(h, y, gate, p_prompt, p_sample, layer, n_p, split, wpg_bf16, wpp_bf16, g_row, b_row):
    n, d = h.shape
    e = p_prompt.shape[1]
    n_tiles = n // TM_TOK
    n_ptiles = n_p // TM_TOK
    n_stiles = (n - n_p) // TM_TOK
    row = lambda i: (i, 0)
    row_p = lambda i: (jnp.minimum(i, n_ptiles - 1), 0)
    row_s = lambda i: (jnp.maximum(i - n_ptiles, 0), 0)
    fixed = lambda i: (0, 0)
    choice = lambda k: pl.BlockSpec((TM_TOK * SUBLANES, LANES), lambda i: (k * n_tiles + i, 0))
    if split:
        out_specs = [pl.BlockSpec((TM_TOK, d), row_p), pl.BlockSpec((TM_TOK, d), row_s)]
        out_shape = [jax.ShapeDtypeStruct((n_p, d), F32), jax.ShapeDtypeStruct((n - n_p, d), F32)]
    else:
        out_specs = pl.BlockSpec((TM_TOK, d), row)
        out_shape = jax.ShapeDtypeStruct((n, d), F32)
    return pl.pallas_call(
        functools.partial(_combine_kernel, n_ptiles=n_ptiles),
        grid=(n_tiles,),
        in_specs=[pl.BlockSpec((TM_TOK, d), row)] + [choice(k) for k in range(TOP_K)]
                 + [pl.BlockSpec((TM_TOK, LANES), row),
                    pl.BlockSpec((TM_TOK, e), lambda i: (layer * n_ptiles + row_p(i)[0], 0)),
                    pl.BlockSpec((TM_TOK, e), lambda i: (layer * n_stiles + row_s(i)[0], 0)),
                    pl.BlockSpec((d, d), fixed), pl.BlockSpec((e, d), fixed), pl.BlockSpec((1, d), fixed),
                    pl.BlockSpec((1, d), fixed)],
        out_specs=out_specs,
        out_shape=out_shape,
        compiler_params=_cparams(("arbitrary",)),
        name="moe_combine_ln_embed",
    )(h, y, y, y, y, gate, p_prompt, p_sample, wpg_bf16, wpp_bf16, g_row, b_row)


def _layer_tail(h, ht, p_prompt, p_sample, layer, n_p, split, g2, b2, w_r, b_r, w_up, b_up, w_dn, b_dn,
                w_pg, w_pp):
    n, d = h.shape
    wr_pad = jnp.pad(w_r, ((0, 0), (0, LANES - N_EXPERTS)))
    br_row = jnp.pad(b_r, (0, LANES - N_EXPERTS))[None]
    n_asg = n * TOP_K
    n_blocks = n_asg // MOE_BLOCK + N_EXPERTS
    assert n_asg % MOE_BLOCK == 0 and n % SC_WINDOW == 0
    n_slots = n_blocks * MOE_BLOCK
    idx, gate, rank, pstart, tab = _router(h, wr_pad, br_row, n_blocks)
    experts = jnp.arange(N_EXPERTS, dtype=I32)[:, None, None]
    group_start = jnp.sum(jnp.where(idx[None, :TOP_K] == experts, pstart[0, :N_EXPERTS, None, None], 0), axis=0)
    slot_by_choice = group_start + rank[:TOP_K]
    xb = _sc_scatter_tiles(ht.reshape(n, SUBLANES, LANES), slot_by_choice, n_slots)
    yb = _experts(xb.reshape(n_slots * SUBLANES, LANES), tab[:n_blocks, 0], tab[0, 2:3], tab[:n_blocks, 1],
                  tab[:n_blocks, 3], layer, w_up, b_up[:, :, None, :], w_dn, b_dn[:, :, None, :])
    y = _sc_gather_tiles(yb.reshape(n_slots, SUBLANES, LANES), slot_by_choice.reshape(-1))
    return _combine(h, y.reshape(n_asg * SUBLANES, LANES), gate, p_prompt, p_sample, layer, n_p, split,
                    w_pg.astype(BF16), w_pp.astype(BF16), g2[None], b2[None])


def _lane_row(v, lane0):
    return jnp.zeros((1, LANES), F32).at[0, lane0:lane0 + v.shape[0]].set(v.astype(F32))


def kernel(x_prompt, x_sample, cache_fox_k, cache_fox_v, cache_fox_logf, state_gdn, state_gdn_conv,
           cache_pool, p_prompt, p_sample, w_in_ab, b_fgate, gdn_a_log, gdn_dt_bias, gdn_conv_w,
           gdn_norm_g, w_out_ab, pool_w, pool_scale, w_out_pool, ln1_g, ln1_b, ln2_g, ln2_b,
           w_router, b_router, w_expert_up, b_expert_up, w_expert_down, b_expert_down,
           w_ple_gate, w_ple_proj):
    n_pb, seq, d = x_prompt.shape
    n_sb, dseq, _ = x_sample.shape
    past = cache_fox_k.shape[2]
    assert n_pb == 1 and dseq == CHUNK and past % dseq == 0 and seq % TQ == 0
    assert d == SUBLANES * LANES
    n_p = n_pb * seq
    n_s = n_sb * dseq
    n = n_p + n_s
    assert n_p % TM_TOK == 0 and n_s % TM_TOK == 0
    n_layers = p_prompt.shape[0]
    pp_all = p_prompt.reshape(n_layers * n_p, -1)
    ps_all = p_sample.reshape(n_layers * n_s, -1)

    def tail(h, ht, i, split):
        return _layer_tail(h, ht, pp_all, ps_all, i, n_p, split, ln2_g[i], ln2_b[i], w_router[i], b_router[i],
                           w_expert_up, b_expert_up, w_expert_down, b_expert_down, w_ple_gate[i],
                           w_ple_proj[i])

    w_in = w_in_ab[0]
    n_small = 3 * N_HEADS
    ff0 = 3 * WIDTH
    gq0 = ff0 + N_HEADS
    ga0 = gq0 + 4 * WIDTH
    w_small = jnp.concatenate([w_in[:, ff0:gq0], w_in[:, ga0:ga0 + 2 * N_HEADS],
                               jnp.zeros((d, LANES - n_small), F32)], axis=1)
    w_all = jnp.concatenate([w_in[:, :ff0], w_in[:, gq0:ga0], w_small], axis=1)
    bf_row = _lane_row(b_fgate[0], LANE_F)
    up = _proj(x_prompt.reshape(n_p, d), w_all.astype(BF16), bf_row, TM_PROJ, U_COLS, False)
    us = _proj(x_sample.reshape(n_s, d), w_all, bf_row, TM_TOK, LANES, True)

    u3 = up[None]
    us3 = us.reshape(n_sb, dseq, U_COLS)
    cq_p, ck_p = _cumsum(u3, 1, n_p, TK, COL_S // LANES)
    lf_s = jnp.concatenate(
        [jnp.pad(cache_fox_logf[0].astype(F32), ((0, 0), (0, 0), (0, LANES - N_HEADS))),
         us3[:, :, COL_S:]], axis=1)
    cq_s, ck_s = _cumsum(lf_s, n_sb, past + dseq, past + dseq, 0)

    of_p = _fox(u3, lambda b, i: (0, i, COL_Q // WIDTH), u3, lambda b, j: (0, j, COL_K // WIDTH),
                u3, lambda b, j: (0, j, COL_V // WIDTH), cq_p, lambda b, i: (0, i, 0),
                ck_p, lambda b, j: (0, 0, j), 1, n_p // TQ, TQ, TK, 0)
    k_all = jnp.concatenate([cache_fox_k[0].reshape(n_sb, past, WIDTH), us3[:, :, COL_K:COL_K + WIDTH]], axis=1)
    v_all = jnp.concatenate([cache_fox_v[0].reshape(n_sb, past, WIDTH), us3[:, :, COL_V:COL_V + WIDTH]], axis=1)
    of_s = _fox(us3, lambda b, i: (b, 0, COL_Q // WIDTH), k_all, lambda b, j: (b, 0, 0),
                v_all, lambda b, j: (b, 0, 0), cq_s, lambda b, i: (b, past // dseq, 0),
                ck_s, lambda b, j: (b, 0, 0), n_sb, 1, dseq, past + dseq, past, hi=True)

    gw = 3 * WIDTH
    conv_w = jnp.pad(gdn_conv_w[0], ((0, SUBLANES - CONV_WIDTH), (0, 0)))
    gdn_args = (conv_w, _lane_row(gdn_a_log[0], LANE_A), _lane_row(gdn_dt_bias[0], LANE_A),
                gdn_norm_g[0][None])
    n_pstep = n_p // GDN_ROWS
    ends = lambda k: (jnp.asarray((np.arange(k) == 0).astype(np.int32)),
                      jnp.asarray((np.arange(k) == k - 1).astype(np.int32)))
    og_p, st_p = _gdn(up.reshape(n_pstep, GDN_ROWS, U_COLS), False, *ends(n_pstep),
                      jnp.zeros((n_pstep,), I32), jnp.zeros((1, SUBLANES, gw), F32), gdn_args[0],
                      jnp.zeros((1, N_HEADS, HEAD_DIM, HEAD_DIM), F32), *gdn_args[1:])
    conv_past = jnp.pad(state_gdn_conv[0].astype(F32), ((0, 0), (SUBLANES - (CONV_WIDTH - 1), 0), (0, 0)))
    ones = jnp.ones((n_sb,), I32)
    og_s, st_s = _gdn(us3, True, ones, ones, jnp.arange(n_sb, dtype=I32), conv_past, gdn_args[0],
                      state_gdn[0].astype(F32), *gdn_args[1:])
    h, ht = _outproj_ln((x_prompt.reshape(n_p, d), of_p.reshape(n_p, WIDTH), og_p.reshape(n_p, WIDTH)),
                        (x_sample.reshape(n_s, d), of_s.reshape(n_s, WIDTH), og_s.reshape(n_s, WIDTH)),
                        w_out_ab[0], ln1_g[0][None], ln1_b[0][None])
    x1 = tail(h, ht, 0, False)

    pool_args = (pool_w[0].astype(BF16), pool_scale[0][None], w_out_pool[0].astype(BF16),
                 ln1_g[1][None], ln1_b[1][None])
    ratio = TM_TOK // POOL_HALO
    x1_halo = x1.reshape(n // POOL_HALO, POOL_HALO, d)
    h_pool = _pool_ln(x1, lambda i: (i, 0), x1_halo, lambda i: (jnp.maximum(i * ratio - 1, 0), 0, 0),
                      n_p // TM_TOK, TM_TOK, 0, True, *pool_args)
    cache16 = jnp.pad(cache_pool[0].astype(F32), ((0, 0), (POOL_HALO - POOL_STATE, 0), (0, 0)))
    h, ht = _pool_ln(x1, lambda i: (n_p // dseq + i, 0), cache16, lambda i: (i, 0, 0),
                     n_sb, dseq, past, False, *pool_args, into=h_pool)
    x2_p, x2_s = tail(h, ht, 1, True)

    return (x2_p.reshape(n_pb, seq, d), x2_s.reshape(n_sb, dseq, d),
            up[:, COL_K:COL_K + WIDTH].reshape(1, n_pb, seq, N_HEADS, HEAD_DIM),
            up[:, COL_V:COL_V + WIDTH].reshape(1, n_pb, seq, N_HEADS, HEAD_DIM),
            up[:, COL_S:COL_S + N_HEADS].reshape(1, n_pb, seq, N_HEADS),
            st_p.reshape(1, n_pb, N_HEADS, HEAD_DIM, HEAD_DIM),
            up[seq - (CONV_WIDTH - 1):, COL_G:COL_G + gw].reshape(1, n_pb, CONV_WIDTH - 1, gw),
            x1[n_p - POOL_STATE:n_p].reshape(1, n_pb, POOL_STATE, d),
            us[:, COL_K:COL_K + WIDTH].reshape(1, n_sb, dseq, N_HEADS, HEAD_DIM),
            us[:, COL_V:COL_V + WIDTH].reshape(1, n_sb, dseq, N_HEADS, HEAD_DIM),
            us[:, COL_S:COL_S + N_HEADS].reshape(1, n_sb, dseq, N_HEADS),
            st_s.reshape(1, n_sb, N_HEADS, HEAD_DIM, HEAD_DIM),
            us[:, COL_G:COL_G + gw].reshape(n_sb, dseq, gw)[:, dseq - (CONV_WIDTH - 1):].reshape(
                1, n_sb, CONV_WIDTH - 1, gw),
            x1[n_p:].reshape(n_sb, dseq, d)[:, dseq - POOL_STATE:].reshape(1, n_sb, POOL_STATE, d))
```

```python
import functools

import numpy as np
import jax
import jax.numpy as jnp
from jax import lax
from jax.experimental import pallas as pl
from jax.experimental.pallas import tpu as pltpu
from jax.experimental.pallas import tpu_sc as plsc

F32 = jnp.float32
BF16 = jnp.bfloat16
I32 = jnp.int32
HIGHEST = lax.Precision.HIGHEST

LANES = 128
SUBLANES = 8
VMEM_LIMIT = 56 * 1024 * 1024

HEAD_DIM = 128
N_HEADS = 4
WIDTH = N_HEADS * HEAD_DIM
CHUNK = 64
CONV_WIDTH = 4
POOL_WINDOWS = (2, 4, 8, 16)
POOL_HALO = 16
POOL_STATE = 15
N_EXPERTS = 32
TOP_K = 4
SWIGLU_LIMIT = 7.0
SWIGLU_ALPHA = 1.702
DEPTH = 2
DN_ALPHA = (2 * DEPTH) ** 0.25
LN_EPS = 1e-5
NORM_EPS = 1e-6
NEG_INF = -1e30
LOG2E = 1.4426950408889634

COL_Q, COL_K, COL_V = 0, WIDTH, 2 * WIDTH
COL_G = 3 * WIDTH
COL_Z = 6 * WIDTH
COL_S = 7 * WIDTH
U_COLS = COL_S + LANES
LANE_F, LANE_A, LANE_B = 0, N_HEADS, 2 * N_HEADS

TM_PROJ = 256
TM_TOK = 512
SC_WINDOW = 32
MOE_BLOCK = 512
TQ = 1024
TK = 1024
CUMSUM_TILE = 512
GDN_ROWS = 256
GDN_CHUNK = CHUNK


def _cparams(sem):
    return pltpu.CompilerParams(dimension_semantics=sem, vmem_limit_bytes=VMEM_LIMIT)


def _softplus(x):
    return jnp.maximum(x, 0.0) + jnp.log1p(jnp.exp(-jnp.abs(x)))


def _sigmoid(x):
    return 1.0 / (1.0 + jnp.exp(-x))


def _silu(x):
    return x * _sigmoid(x)


def _layer_norm(y, g, b):
    mu = jnp.mean(y, axis=-1, keepdims=True)
    yc = y - mu
    var = jnp.mean(yc * yc, axis=-1, keepdims=True)
    return yc * lax.rsqrt(var + LN_EPS) * g + b


def _dot_general(a, b, dims, hi):
    if hi:
        return lax.dot_general(a.astype(F32), b.astype(F32), (dims, ((), ())), precision=HIGHEST,
                               preferred_element_type=F32)
    return lax.dot_general(a.astype(BF16), b.astype(BF16), (dims, ((), ())), preferred_element_type=F32)


def _dot(a, b, hi=False):
    return _dot_general(a, b, ((1,), (0,)), hi)


def _dot_nt(a, b, hi=False):
    return _dot_general(a, b, ((1,), (1,)), hi)


def _dot_tn(a, b, hi=False):
    return _dot_general(a, b, ((0,), (0,)), hi)


def _spread_lanes(x, width):
    if width % LANES == 0:
        return jnp.concatenate([x] * (width // LANES), axis=1)
    return jnp.broadcast_to(x[:, 0:1], (x.shape[0], width))


def _load_token_tiles(ref, lead, n_tok):
    return jnp.concatenate([ref[(*lead, pl.ds(j, n_tok, stride=SUBLANES), slice(None))]
                            for j in range(SUBLANES)], axis=1)


def _store_token_tiles(ref, lead, x):
    for j in range(SUBLANES):
        ref[(*lead, pl.ds(j, x.shape[0], stride=SUBLANES), slice(None))] = x[:, j * LANES:(j + 1) * LANES]


def _lanes_to_rows(x, lane0):
    r = lax.broadcasted_iota(I32, (SUBLANES, LANES), 0)
    c = lax.broadcasted_iota(I32, (SUBLANES, LANES), 1)
    sel = (c == r + lane0).astype(F32)
    return lax.dot_general(sel, x, (((1,), (1,)), ((), ())), precision=HIGHEST,
                           preferred_element_type=F32)


def _proj_kernel(x_ref, w_ref, bf_ref, u_ref, *, hi):
    u = _dot(x_ref[...], w_ref[...], hi)
    u_ref[...] = u

    @pl.when(pl.program_id(1) == pl.num_programs(1) - 1)
    def _():
        small = u[:, u.shape[1] - LANES:]
        lane = lax.broadcasted_iota(I32, small.shape, 1)
        logf = -_softplus(-(small + bf_ref[...]))
        u_ref[:, u.shape[1] - LANES:] = jnp.where(lane < LANE_A, logf, small)


def _proj(x, w, bf_row, tm, tn, hi):
    n, d = x.shape
    m = w.shape[1]
    return pl.pallas_call(
        functools.partial(_proj_kernel, hi=hi),
        grid=(n // tm, m // tn),
        in_specs=[pl.BlockSpec((tm, d), lambda i, j: (i, 0)),
                  pl.BlockSpec((d, tn), lambda i, j: (0, j)),
                  pl.BlockSpec((1, LANES), lambda i, j: (0, 0))],
        out_specs=pl.BlockSpec((tm, tn), lambda i, j: (i, j)),
        out_shape=jax.ShapeDtypeStruct((n, m), F32),
        compiler_params=_cparams(("parallel", "parallel")),
        name="in_proj",
    )(x, w, bf_row)


def _cumsum_kernel(lf_ref, crep_ref, crow_ref, carry_ref):
    @pl.when(pl.program_id(1) == 0)
    def _():
        carry_ref[...] = jnp.zeros_like(carry_ref)

    lf = lf_ref[0]
    t = lf.shape[0]
    r = lax.broadcasted_iota(I32, (t, t), 0)
    c = lax.broadcasted_iota(I32, (t, t), 1)
    tril = (c <= r).astype(F32)
    cs = jnp.dot(tril, lf, precision=HIGHEST, preferred_element_type=F32) + carry_ref[0:1, :]
    carry_ref[...] = jnp.broadcast_to(cs[t - 1:t, :], carry_ref.shape)
    c2 = cs * LOG2E
    crow_ref[0] = _lanes_to_rows(c2, LANE_F)
    for h in range(N_HEADS):
        crep_ref[0, :, h * HEAD_DIM:(h + 1) * HEAD_DIM] = jnp.broadcast_to(
            c2[:, LANE_F + h:LANE_F + h + 1], (t, HEAD_DIM))


def _cumsum(arr, n_batch, length, tl, col_block):
    return pl.pallas_call(
        _cumsum_kernel,
        grid=(n_batch, length // tl),
        in_specs=[pl.BlockSpec((1, tl, LANES), lambda b, j: (b, j, col_block))],
        out_specs=[pl.BlockSpec((1, tl, WIDTH), lambda b, j: (b, j, 0)),
                   pl.BlockSpec((1, SUBLANES, tl), lambda b, j: (b, 0, j))],
        out_shape=[jax.ShapeDtypeStruct((n_batch, length, WIDTH), F32),
                   jax.ShapeDtypeStruct((n_batch, SUBLANES, length), F32)],
        scratch_shapes=[pltpu.VMEM((SUBLANES, LANES), F32)],
        compiler_params=_cparams(("parallel", "arbitrary")),
        name="logf_cumsum",
    )(arr)


def _fox_kernel(qi_ref, kj_ref, last_ref, q_ref, k_ref, v_ref, cq_ref, ck_ref, o_ref,
                m_ref, l_ref, acc_ref, *, tq, tk, past, hi):
    s_idx = pl.program_id(1)
    qi = qi_ref[s_idx]
    kj = kj_ref[s_idx]

    @pl.when(kj == 0)
    def _():
        m_ref[...] = jnp.full_like(m_ref, NEG_INF)
        l_ref[...] = jnp.zeros_like(l_ref)
        acc_ref[...] = jnp.zeros_like(acc_ref)

    def update(masked):
        if masked:
            q_pos = past + qi * tq + lax.broadcasted_iota(I32, (tq, tk), 0)
            k_pos = kj * tk + lax.broadcasted_iota(I32, (tq, tk), 1)
            visible = k_pos <= q_pos
        for h in range(N_HEADS):
            cols = slice(h * HEAD_DIM, (h + 1) * HEAD_DIM)
            q = q_ref[0, :, cols] * (HEAD_DIM ** -0.5 * LOG2E)
            t = _dot_nt(q, k_ref[0, :, cols], hi) - ck_ref[0, h:h + 1, :]
            if masked:
                t = jnp.where(visible, t, NEG_INF)
            cq = cq_ref[0, :, cols]
            m_prev = m_ref[h]
            m_new = jnp.maximum(m_prev, jnp.max(t, axis=-1, keepdims=True) + cq)
            p = jnp.exp2(t - _spread_lanes(m_new - cq, tk))
            alpha = jnp.exp2(m_prev - m_new)
            l_ref[h] = alpha * l_ref[h] + jnp.sum(p, axis=-1, keepdims=True)
            acc_ref[:, cols] = alpha * acc_ref[:, cols] + _dot(p, v_ref[0, :, cols], hi)
            m_ref[h] = m_new

    crosses_diagonal = kj * tk + (tk - 1) > past + qi * tq
    pl.when(crosses_diagonal)(functools.partial(update, True))
    pl.when(jnp.logical_not(crosses_diagonal))(functools.partial(update, False))

    @pl.when(last_ref[s_idx] == 1)
    def _():
        for h in range(N_HEADS):
            cols = slice(h * HEAD_DIM, (h + 1) * HEAD_DIM)
            o_ref[0, :, cols] = acc_ref[:, cols] / l_ref[h]


def _fox_schedule(n_q, tq, tk, past):
    qi, kj, last = [], [], []
    for i in range(n_q):
        hi = (past + (i + 1) * tq - 1) // tk
        for j in range(hi + 1):
            qi.append(i)
            kj.append(j)
            last.append(1 if j == hi else 0)
    return (jnp.asarray(np.array(qi, np.int32)), jnp.asarray(np.array(kj, np.int32)),
            jnp.asarray(np.array(last, np.int32)))


def _fox(q_arr, q_map, k_arr, k_map, v_arr, v_map, cq_arr, cq_map, ck_arr, ck_map,
         n_batch, n_q, tq, tk, past, hi=False):
    qi, kj, last = _fox_schedule(n_q, tq, tk, past)
    n_steps = int(qi.shape[0])
    spec = lambda shape, fn, tab: pl.BlockSpec(shape, lambda b, s, qi_r, kj_r, la_r: fn(b, (qi_r if tab == 'q' else kj_r)[s]))
    return pl.pallas_call(
        functools.partial(_fox_kernel, tq=tq, tk=tk, past=past, hi=hi),
        grid_spec=pltpu.PrefetchScalarGridSpec(
            num_scalar_prefetch=3,
            grid=(n_batch, n_steps),
            in_specs=[spec((1, tq, WIDTH), q_map, 'q'),
                      spec((1, tk, WIDTH), k_map, 'k'),
                      spec((1, tk, WIDTH), v_map, 'k'),
                      spec((1, tq, WIDTH), cq_map, 'q'),
                      spec((1, SUBLANES, tk), ck_map, 'k')],
            out_specs=spec((1, tq, WIDTH), lambda b, i: (b, i, 0), 'q'),
            scratch_shapes=[pltpu.VMEM((N_HEADS, tq, HEAD_DIM), F32),
                            pltpu.VMEM((N_HEADS, tq, HEAD_DIM), F32),
                            pltpu.VMEM((tq, WIDTH), F32)]),
        out_shape=jax.ShapeDtypeStruct((n_batch, n_q * tq, WIDTH), F32),
        compiler_params=_cparams(("parallel", "arbitrary")),
        name="fox_attention",
    )(qi, kj, last, q_arr, k_arr, v_arr, cq_arr, ck_arr)


def _gdn_kernel(first_ref, last_ref, seq_ref,
                pre_ref, z_ref, sm_ref, cpast_ref, convw_ref, s0_ref, alog_ref, dtb_ref, ng_ref,
                o_ref, sout_ref, stage_ref, s_ref, *, rows, chunk, hi):
    dot, dot_nt, dot_tn = (functools.partial(f, hi=hi) for f in (_dot, _dot_nt, _dot_tn))
    step = pl.program_id(0)
    halo = SUBLANES
    n_chunks = rows // chunk

    @pl.when(first_ref[step] == 1)
    def _():
        stage_ref[0:halo, :] = cpast_ref[0]
        s_ref[...] = s0_ref[0]

    stage_ref[halo:halo + rows, :] = pre_ref[0]
    conv = stage_ref[halo:halo + rows, :] * convw_ref[CONV_WIDTH - 1:CONV_WIDTH, :]
    for j in range(1, CONV_WIDTH):
        conv = conv + (stage_ref[halo - j:halo - j + rows, :]
                       * convw_ref[CONV_WIDTH - 1 - j:CONV_WIDTH - j, :])
    stage_ref[0:halo, :] = stage_ref[rows:rows + halo, :]
    act = _silu(conv)

    small = sm_ref[0]
    beta_all = _sigmoid(small)
    g_all = -jnp.exp(alog_ref[...]) * _softplus(small + dtb_ref[...])
    r = lax.broadcasted_iota(I32, (rows, rows), 0)
    c = lax.broadcasted_iota(I32, (rows, rows), 1)
    same_chunk = (r // chunk) == (c // chunk)
    incl = same_chunk & (c <= r)
    strict = same_chunk & (c < r)
    eye = (c == r).astype(F32)
    gc_all = jnp.dot(incl.astype(F32), g_all, precision=HIGHEST, preferred_element_type=F32)
    gc_rows = _lanes_to_rows(gc_all, LANE_A)

    heads = range(N_HEADS)
    head_cols = [slice(h * HEAD_DIM, (h + 1) * HEAD_DIM) for h in heads]
    q, k, gc, decay, kb, vb, low = [], [], [], [], [], [], []
    for h in heads:
        qh = act[:, h * HEAD_DIM:(h + 1) * HEAD_DIM]
        kh = act[:, WIDTH + h * HEAD_DIM:WIDTH + (h + 1) * HEAD_DIM]
        vh = act[:, 2 * WIDTH + h * HEAD_DIM:2 * WIDTH + (h + 1) * HEAD_DIM]
        q.append(qh * lax.rsqrt(jnp.sum(qh * qh, axis=-1, keepdims=True) + NORM_EPS) * (HEAD_DIM ** -0.5))
        k.append(kh * lax.rsqrt(jnp.sum(kh * kh, axis=-1, keepdims=True) + NORM_EPS))
        beta = beta_all[:, LANE_B + h:LANE_B + h + 1]
        gc.append(gc_all[:, LANE_A + h:LANE_A + h + 1])
        diff = gc[h] - gc_rows[h:h + 1, :]
        decay.append(jnp.where(incl, jnp.exp(jnp.where(incl, diff, 0.0)), 0.0))
        kb.append(k[h] * beta)
        vb.append(vh * beta)
    for h in heads:
        low.append(jnp.where(strict, dot_nt(kb[h], k[h]) * decay[h], 0.0))
    inv = [eye - low[h] for h in heads]
    pw = [dot(low[h], low[h]) for h in heads]
    n_sq = chunk.bit_length() - 2
    for it in range(n_sq):
        inv = [inv[h] + dot(inv[h], pw[h]) for h in heads]
        if it + 1 < n_sq:
            pw = [dot(pw[h], pw[h]) for h in heads]
    egc = [jnp.exp(gc[h]) for h in heads]
    uw = [dot(inv[h], jnp.concatenate([vb[h], kb[h] * egc[h]], axis=1)) for h in heads]
    intra = [jnp.where(incl, dot_nt(q[h], k[h]) * decay[h], 0.0) for h in heads]
    qd = [q[h] * egc[h] for h in heads]
    g_last = [[gc[h][(g + 1) * chunk - 1:(g + 1) * chunk, :] for g in range(n_chunks)] for h in heads]
    kd = [k[h] * jnp.exp(jnp.concatenate([jnp.broadcast_to(gl, (chunk, 1)) for gl in g_last[h]], axis=0)
                         - gc[h]) for h in heads]
    state = [s_ref[h] for h in heads]
    v_new = [[] for _ in heads]
    for g in range(n_chunks):
        rs = slice(g * chunk, (g + 1) * chunk)
        for h in heads:
            v_new[h].append(uw[h][rs, :HEAD_DIM] - dot(uw[h][rs, HEAD_DIM:], state[h]))
        for h in heads:
            v_rows = jnp.concatenate(
                v_new[h] + [jnp.zeros((rows - (g + 1) * chunk, HEAD_DIM), F32)] * (g + 1 < n_chunks), axis=0)
            o = dot(qd[h][rs, :], state[h]) + dot(intra[h][rs, :], v_rows)
            state[h] = state[h] * jnp.exp(g_last[h][g]) + dot_tn(kd[h][rs, :], v_new[h][g])
            o = (o * lax.rsqrt(jnp.mean(o * o, axis=-1, keepdims=True) + NORM_EPS)
                 * ng_ref[...] * _silu(z_ref[0, rs, head_cols[h]]))
            o_ref[0, rs, head_cols[h]] = o
    for h in heads:
        s_ref[h] = state[h]

    @pl.when(last_ref[step] == 1)
    def _():
        sout_ref[0] = s_ref[...]


def _gdn(u_view, hi, first, last, seq, conv_past, conv_w, s0, alog_row, dtb_row, ng_row):
    n_steps, rows, _ = u_view.shape
    n_seq = s0.shape[0]
    gw = 3 * WIDTH
    return pl.pallas_call(
        functools.partial(_gdn_kernel, rows=rows, chunk=min(rows, GDN_CHUNK), hi=hi),
        grid_spec=pltpu.PrefetchScalarGridSpec(
            num_scalar_prefetch=3,
            grid=(n_steps,),
            in_specs=[pl.BlockSpec((1, rows, gw), lambda s, f, l, q: (s, 0, COL_G // gw)),
                      pl.BlockSpec((1, rows, WIDTH), lambda s, f, l, q: (s, 0, COL_Z // WIDTH)),
                      pl.BlockSpec((1, rows, LANES), lambda s, f, l, q: (s, 0, COL_S // LANES)),
                      pl.BlockSpec((1, SUBLANES, gw), lambda s, f, l, q: (q[s], 0, 0)),
                      pl.BlockSpec((SUBLANES, gw), lambda s, f, l, q: (0, 0)),
                      pl.BlockSpec((1, N_HEADS, HEAD_DIM, HEAD_DIM), lambda s, f, l, q: (q[s], 0, 0, 0)),
                      pl.BlockSpec((1, LANES), lambda s, f, l, q: (0, 0)),
                      pl.BlockSpec((1, LANES), lambda s, f, l, q: (0, 0)),
                      pl.BlockSpec((1, LANES), lambda s, f, l, q: (0, 0))],
            out_specs=[pl.BlockSpec((1, rows, WIDTH), lambda s, f, l, q: (s, 0, 0)),
                       pl.BlockSpec((1, N_HEADS, HEAD_DIM, HEAD_DIM), lambda s, f, l, q: (q[s], 0, 0, 0))],
            scratch_shapes=[pltpu.VMEM((rows + SUBLANES, gw), F32),
                            pltpu.VMEM((N_HEADS, HEAD_DIM, HEAD_DIM), F32)]),
        out_shape=[jax.ShapeDtypeStruct((n_steps, rows, WIDTH), F32),
                   jax.ShapeDtypeStruct((n_seq, N_HEADS, HEAD_DIM, HEAD_DIM), F32)],
        compiler_params=_cparams(("arbitrary",)),
        name="gated_deltanet",
    )(first, last, seq, u_view, u_view, u_view, conv_past, conv_w, s0, alog_row, dtb_row, ng_row)


def _outproj_ln_kernel(xp_ref, ofp_ref, ogp_ref, xs_ref, ofs_ref, ogs_ref, w_ref, wf_ref, g_ref, b_ref,
                       h_ref, ht_ref, *, hi_from):
    def run(hi, x_ref, of_ref, og_ref):
        w = wf_ref if hi else w_ref
        mix = _dot(of_ref[...], w[0:WIDTH, :], hi) + _dot(og_ref[...], w[WIDTH:2 * WIDTH, :], hi)
        h = _layer_norm(DN_ALPHA * x_ref[...] + mix, g_ref[...], b_ref[...])
        h_ref[...] = h
        _store_token_tiles(ht_ref, (), h)

    pl.when(pl.program_id(0) < hi_from)(functools.partial(run, False, xp_ref, ofp_ref, ogp_ref))
    pl.when(pl.program_id(0) >= hi_from)(functools.partial(run, True, xs_ref, ofs_ref, ogs_ref))


def _outproj_ln(prompt, sample, w, g_row, b_row):
    n_p, d = prompt[0].shape
    n = n_p + sample[0].shape[0]
    hi_from = n_p // TM_TOK
    row = lambda i: (i, 0)
    row_p = lambda i: (jnp.minimum(i, hi_from - 1), 0)
    row_s = lambda i: (jnp.maximum(i - hi_from, 0), 0)
    fixed = lambda i: (0, 0)
    group = lambda rows: [pl.BlockSpec((TM_TOK, d), rows), pl.BlockSpec((TM_TOK, WIDTH), rows),
                          pl.BlockSpec((TM_TOK, WIDTH), rows)]
    return pl.pallas_call(
        functools.partial(_outproj_ln_kernel, hi_from=hi_from),
        grid=(n // TM_TOK,),
        in_specs=group(row_p) + group(row_s) + [pl.BlockSpec((2 * WIDTH, d), fixed),
                                                pl.BlockSpec((2 * WIDTH, d), fixed),
                                                pl.BlockSpec((1, d), fixed), pl.BlockSpec((1, d), fixed)],
        out_specs=[pl.BlockSpec((TM_TOK, d), row), pl.BlockSpec((TM_TOK * SUBLANES, LANES), row)],
        out_shape=[jax.ShapeDtypeStruct((n, d), F32), jax.ShapeDtypeStruct((n * SUBLANES, LANES), F32)],
        compiler_params=_cparams(("parallel",)),
        name="out_proj_ln",
    )(*prompt, *sample, w.astype(BF16), w, g_row, b_row)


def _pool_ln_kernel(x_ref, halo_ref, pw_ref, ps_ref, w_ref, g_ref, b_ref, h_ref, ht_ref, stage_ref,
                    lvl_a_ref, lvl_b_ref, *, tm, pos0, zero_first_halo):
    i = pl.program_id(0)
    pad = SUBLANES
    n_rows = POOL_HALO + tm
    for ref in (stage_ref, lvl_a_ref, lvl_b_ref):
        ref[0:pad, :] = jnp.zeros((pad, ref.shape[1]), F32)
    stage_ref[pad:pad + POOL_HALO, :] = halo_ref[0]
    if zero_first_halo:
        @pl.when(i == 0)
        def _():
            stage_ref[pad:pad + POOL_HALO, :] = jnp.zeros((POOL_HALO, stage_ref.shape[1]), F32)
    x = x_ref[...]
    stage_ref[pad + POOL_HALO:pad + n_rows, :] = x
    gdim = x.shape[1] // len(POOL_WINDOWS)
    pos = pos0 + lax.broadcasted_iota(I32, (tm, 1), 0)
    if zero_first_halo:
        pos = pos + i * tm

    def window_sum(cols, win):
        assert win & (win - 1) == 0 and win <= POOL_HALO
        src, src_cols, span, level = stage_ref, cols, 1, 0
        while span < win:
            dst = (lvl_a_ref, lvl_b_ref)[level % 2]
            dst[pad:pad + n_rows, :] = (src[pad:pad + n_rows, src_cols]
                                        + src[pad - span:pad - span + n_rows, src_cols])
            src, src_cols, span, level = dst, slice(None), 2 * span, level + 1
        return src[pad + POOL_HALO:pad + n_rows, src_cols]

    parts = []
    for gi, win in enumerate(POOL_WINDOWS):
        cols = slice(gi * gdim, (gi + 1) * gdim)
        s = window_sum(cols, win)
        cnt = jnp.minimum(pos + 1, win).astype(F32)
        zg = s / cnt - x[:, cols]
        parts.append(_dot(zg, pw_ref[gi]))
    zg = jnp.concatenate(parts, axis=-1) * ps_ref[...]
    mix = _dot(zg, w_ref[...])
    h = _layer_norm(DN_ALPHA * x + mix, g_ref[...], b_ref[...])
    h_ref[...] = h
    _store_token_tiles(ht_ref, (), h)


def _pool_ln_into_kernel(*refs, **kw):
    _pool_ln_kernel(*refs[:7], *refs[9:], **kw)


def _pool_ln(x, x_map, halo_arr, halo_map, n_tiles, tm, pos0, zero_first_halo,
             pw_bf16, ps_row, w_bf16, g_row, b_row, into=None):
    n, d = x.shape
    gdim = d // len(POOL_WINDOWS)
    fixed = lambda i: (0, 0)
    in_specs = [pl.BlockSpec((tm, d), x_map),
                pl.BlockSpec((1, POOL_HALO, d), halo_map),
                pl.BlockSpec((len(POOL_WINDOWS), gdim, gdim), lambda i: (0, 0, 0)),
                pl.BlockSpec((1, d), fixed), pl.BlockSpec((d, d), fixed),
                pl.BlockSpec((1, d), fixed), pl.BlockSpec((1, d), fixed)]
    args = (x, halo_arr, pw_bf16, ps_row, w_bf16, g_row, b_row)
    kw = dict(tm=tm, pos0=pos0, zero_first_halo=zero_first_halo)
    return pl.pallas_call(
        functools.partial(_pool_ln_kernel if into is None else _pool_ln_into_kernel, **kw),
        grid=(n_tiles,),
        in_specs=in_specs + ([] if into is None else [pl.BlockSpec(memory_space=pl.ANY)] * 2),
        out_specs=[pl.BlockSpec((tm, d), x_map),
                   pl.BlockSpec((tm * SUBLANES, LANES), x_map)],
        out_shape=[jax.ShapeDtypeStruct((n, d), F32), jax.ShapeDtypeStruct((n * SUBLANES, LANES), F32)],
        input_output_aliases={} if into is None else {7: 0, 8: 1},
        scratch_shapes=[pltpu.VMEM((SUBLANES + POOL_HALO + tm, d), F32),
                        pltpu.VMEM((SUBLANES + POOL_HALO + tm, gdim), F32),
                        pltpu.VMEM((SUBLANES + POOL_HALO + tm, gdim), F32)],
        compiler_params=_cparams(("arbitrary",)),
        name="pool_mixer_ln",
    )(*args, *(() if into is None else into))


def _router_kernel(h_ref, wr_ref, br_ref, idx_ref, gate_ref, rank_ref, pstart_ref, tab_ref, carry_ref):
    @pl.when(pl.program_id(0) == 0)
    def _():
        carry_ref[...] = jnp.zeros_like(carry_ref)

    tm = h_ref.shape[0]
    lane = lax.broadcasted_iota(I32, (tm, LANES), 1).astype(F32)
    logits = jnp.dot(h_ref[...], wr_ref[...], precision=HIGHEST, preferred_element_type=F32)
    work = jnp.where(lane < N_EXPERTS, logits + br_ref[...], -jnp.inf)
    vals, ids = [], []
    for _ in range(TOP_K):
        m = jnp.max(work, axis=-1, keepdims=True)
        ik = jnp.min(jnp.where(work == m, lane, float(LANES)), axis=-1, keepdims=True)
        vals.append(m)
        ids.append(ik)
        work = jnp.where(lane == ik, -jnp.inf, work)
    exps = [jnp.exp(v - vals[0]) for v in vals]
    denom = exps[0]
    for e in exps[1:]:
        denom = denom + e
    multihot = jnp.zeros((tm, LANES), F32)
    idx_out = jnp.zeros((tm, LANES), F32)
    gate_out = jnp.zeros((tm, LANES), F32)
    for k in range(TOP_K):
        multihot = multihot + (lane == ids[k]).astype(F32)
        idx_out = jnp.where(lane == k, ids[k], idx_out)
        gate_out = jnp.where(lane == k, exps[k] / denom, gate_out)
    r = lax.broadcasted_iota(I32, (tm, tm), 0)
    c = lax.broadcasted_iota(I32, (tm, tm), 1)
    before = _dot((c < r).astype(F32), multihot) + carry_ref[0:1, :]
    rank_out = jnp.zeros((tm, LANES), F32)
    for k in range(TOP_K):
        rk = jnp.sum(jnp.where(lane == ids[k], before, 0.0), axis=-1, keepdims=True)
        rank_out = jnp.where(lane == k, rk, rank_out)
    idx_ref[...] = jnp.transpose(idx_out)[:SUBLANES].astype(I32)
    rank_ref[...] = jnp.transpose(rank_out)[:SUBLANES].astype(I32)
    gate_ref[...] = gate_out
    total = carry_ref[0:1, :] + jnp.sum(multihot, axis=0, keepdims=True)
    carry_ref[...] = jnp.broadcast_to(total, carry_ref.shape)

    @pl.when(pl.program_id(0) == pl.num_programs(0) - 1)
    def _():
        n_rows = tab_ref.shape[0]
        padded = jnp.floor((total + (MOE_BLOCK - 1)) / MOE_BLOCK) * MOE_BLOCK
        rr = lax.broadcasted_iota(I32, (LANES, LANES), 0)
        cc = lax.broadcasted_iota(I32, (LANES, LANES), 1)
        pend = jnp.dot(jnp.broadcast_to(padded, (SUBLANES, LANES)), (rr <= cc).astype(F32),
                       precision=HIGHEST, preferred_element_type=F32)[0:1, :]
        pstart = pend - padded
        n_used = jnp.max(pend, axis=-1, keepdims=True) / MOE_BLOCK
        elane = lax.broadcasted_iota(I32, (n_rows, LANES), 1)
        blk = jnp.minimum(lax.broadcasted_iota(I32, (n_rows, 1), 0).astype(F32), n_used - 1.0) * MOE_BLOCK
        ends_before = jnp.where((elane < N_EXPERTS) & (pend <= blk), 1.0, 0.0)
        block_e = jnp.minimum(jnp.sum(ends_before, axis=-1, keepdims=True), N_EXPERTS - 1.0)
        mine = elane.astype(F32) == block_e
        filled = jnp.sum(jnp.where(mine, total - (blk - pstart), 0.0), axis=-1, keepdims=True)
        block_valid = jnp.clip(filled, 0.0, float(MOE_BLOCK))
        group_end = jnp.sum(jnp.where(mine, pend, 0.0), axis=-1, keepdims=True)
        next_e = jnp.sum(jnp.where((elane < N_EXPERTS) & (pend <= group_end), 1.0, 0.0), axis=-1, keepdims=True)
        tab = jnp.where(elane == 0, block_e, jnp.where(elane == 1, block_valid,
                                                       jnp.where(elane == 2, n_used,
                                                                 jnp.where(elane == 3, next_e, 0.0))))
        tab_ref[...] = tab.astype(I32)
        pstart_ref[...] = jnp.broadcast_to(pstart, pstart_ref.shape).astype(I32)


def _router(h, wr_pad, br_row, n_blocks):
    n, d = h.shape
    row = lambda i: (i, 0)
    col = lambda i: (0, i)
    fixed = lambda i: (0, 0)
    tab_rows = -(-n_blocks // SUBLANES) * SUBLANES
    return pl.pallas_call(
        _router_kernel,
        grid=(n // TM_TOK,),
        in_specs=[pl.BlockSpec((TM_TOK, d), row), pl.BlockSpec((d, LANES), fixed),
                  pl.BlockSpec((1, LANES), fixed)],
        out_specs=[pl.BlockSpec((SUBLANES, TM_TOK), col), pl.BlockSpec((TM_TOK, LANES), row),
                   pl.BlockSpec((SUBLANES, TM_TOK), col), pl.BlockSpec((SUBLANES, LANES), fixed),
                   pl.BlockSpec((tab_rows, LANES), fixed)],
        out_shape=[jax.ShapeDtypeStruct((SUBLANES, n), I32), jax.ShapeDtypeStruct((n, LANES), F32),
                   jax.ShapeDtypeStruct((SUBLANES, n), I32), jax.ShapeDtypeStruct((SUBLANES, LANES), I32),
                   jax.ShapeDtypeStruct((tab_rows, LANES), I32)],
        scratch_shapes=[pltpu.VMEM((SUBLANES, LANES), F32)],
        compiler_params=_cparams(("arbitrary",)),
        name="moe_router",
    )(h, wr_pad, br_row)


def _expert_kernel(be_ref, nu_ref, valid_ref, next_ref, xb_ref, wup_ref, bup_ref, wdn_ref, bdn_ref, yb_ref,
                   wup_f32_ref, wdn_f32_ref, wup_bf_ref, wdn_bf_ref, wsem, *, layer):
    b = pl.program_id(0)

    def fetch(e):
        return (pltpu.make_async_copy(wup_ref.at[layer, e], wup_f32_ref, wsem.at[0]),
                pltpu.make_async_copy(wdn_ref.at[layer, e], wdn_f32_ref, wsem.at[1]))

    @pl.when(b == 0)
    def _():
        for copy in fetch(be_ref[0]):
            copy.start()

    @pl.when(b < nu_ref[0])
    def _():
        @pl.when((b == 0) | (be_ref[b] != be_ref[jnp.maximum(b - 1, 0)]))
        def _():
            for copy in fetch(be_ref[b]):
                copy.wait()
            wup_bf_ref[...] = wup_f32_ref[...].astype(BF16)
            wdn_bf_ref[...] = wdn_f32_ref[...].astype(BF16)

            @pl.when(next_ref[b] < N_EXPERTS)
            def _():
                for copy in fetch(next_ref[b]):
                    copy.start()

        d_exp = wdn_f32_ref.shape[0]

        def run(n_rows):
            x = _load_token_tiles(xb_ref, (), n_rows)
            row = lax.broadcasted_iota(I32, (n_rows, 1), 0)
            x = jnp.where(row < valid_ref[b], x, 0.0)
            hu = jnp.dot(x.astype(BF16), wup_bf_ref[...], preferred_element_type=F32) + bup_ref[0, 0]
            glu = jnp.minimum(hu[:, :d_exp], SWIGLU_LIMIT)
            lin = jnp.clip(hu[:, d_exp:], -SWIGLU_LIMIT, SWIGLU_LIMIT)
            a = glu * _sigmoid(SWIGLU_ALPHA * glu) * (lin + 1.0)
            y = jnp.dot(a.astype(BF16), wdn_bf_ref[...], preferred_element_type=F32) + bdn_ref[0, 0]
            if n_rows < MOE_BLOCK:
                y = jnp.concatenate([y, jnp.zeros((MOE_BLOCK - n_rows, y.shape[1]), F32)], axis=0)
            _store_token_tiles(yb_ref, (), y)

        half = MOE_BLOCK // 2
        pl.when(valid_ref[b] > half)(functools.partial(run, MOE_BLOCK))
        pl.when(valid_ref[b] <= half)(functools.partial(run, half))

    @pl.when(b >= nu_ref[0])
    def _():
        yb_ref[...] = jnp.zeros_like(yb_ref)


def _experts(xb, block_e, n_used, block_valid, next_e, layer, w_up, b_up, w_dn, b_dn):
    d = SUBLANES * LANES
    rows = MOE_BLOCK * SUBLANES
    n_blocks = xb.shape[0] // rows
    d_up = w_up.shape[3]
    d_exp = w_dn.shape[2]
    bsel = lambda b, be, nu, va, ne: (layer, be[b], 0, 0)
    return pl.pallas_call(
        functools.partial(_expert_kernel, layer=layer),
        grid_spec=pltpu.PrefetchScalarGridSpec(
            num_scalar_prefetch=4,
            grid=(n_blocks,),
            in_specs=[pl.BlockSpec((rows, LANES), lambda b, be, nu, va, ne: (jnp.minimum(b, nu[0] - 1), 0)),
                      pl.BlockSpec(memory_space=pl.ANY),
                      pl.BlockSpec((1, 1, 1, d_up), bsel),
                      pl.BlockSpec(memory_space=pl.ANY),
                      pl.BlockSpec((1, 1, 1, d), bsel)],
            out_specs=pl.BlockSpec((rows, LANES), lambda b, be, nu, va, ne: (b, 0)),
            scratch_shapes=[pltpu.VMEM((d, d_up), F32), pltpu.VMEM((d_exp, d), F32),
                            pltpu.VMEM((d, d_up), BF16), pltpu.VMEM((d_exp, d), BF16),
                            pltpu.SemaphoreType.DMA((2,))]),
        out_shape=jax.ShapeDtypeStruct(xb.shape, F32),
        compiler_params=_cparams(("arbitrary",)),
        name="moe_experts",
    )(block_e, n_used, block_valid, next_e, xb, w_up, b_up, w_dn, b_dn)


def _sc_gather_tiles(table, idx):
    m = idx.shape[0]
    mesh = plsc.VectorSubcoreMesh(core_axis_name="core", subcore_axis_name="subcore")

    @functools.partial(pl.kernel, out_type=jax.ShapeDtypeStruct((m, SUBLANES, LANES), table.dtype), mesh=mesh)
    def gather(table_hbm, idx_hbm, out_hbm):
        def window(idx_vmem, out_vmem):
            pltpu.sync_copy(table_hbm.at[idx_vmem.at[0, pl.ds(0, SC_WINDOW)]], out_vmem)

        pltpu.emit_pipeline(
            window,
            grid=(m // SC_WINDOW,),
            in_specs=[pl.BlockSpec((1, LANES), lambda i: (i, 0))],
            out_specs=[pl.BlockSpec((SC_WINDOW, SUBLANES, LANES), lambda i: (i, 0, 0))],
            core_axis_name=("core", "subcore"),
            dimension_semantics=(pltpu.PARALLEL,),
        )(idx_hbm, out_hbm)

    idx_rows = jnp.pad(idx.reshape(m // SC_WINDOW, SC_WINDOW), ((0, 0), (0, LANES - SC_WINDOW)))
    return gather(table, idx_rows)


def _sc_scatter_tiles(tiles, idx_by_choice, n_out):
    n = tiles.shape[0]
    mesh = plsc.VectorSubcoreMesh(core_axis_name="core", subcore_axis_name="subcore")

    @functools.partial(pl.kernel, out_type=jax.ShapeDtypeStruct((n_out, SUBLANES, LANES), tiles.dtype),
                       mesh=mesh)
    def scatter(tiles_hbm, *refs):
        idx_hbm, out_hbm = refs[:TOP_K], refs[TOP_K]

        def window(tiles_vmem, *idx_vmem):
            for k in range(TOP_K):
                pltpu.sync_copy(tiles_vmem, out_hbm.at[idx_vmem[k].at[0, pl.ds(0, SC_WINDOW)]])

        pltpu.emit_pipeline(
            window,
            grid=(n // SC_WINDOW,),
            in_specs=[pl.BlockSpec((SC_WINDOW, SUBLANES, LANES), lambda i: (i, 0, 0))]
                     + [pl.BlockSpec((1, LANES), lambda i: (i, 0))] * TOP_K,
            out_specs=[],
            core_axis_name=("core", "subcore"),
            dimension_semantics=(pltpu.PARALLEL,),
        )(tiles_hbm, *idx_hbm)

    idx_rows = jnp.pad(idx_by_choice.reshape(TOP_K, n // SC_WINDOW, SC_WINDOW),
                       ((0, 0), (0, 0), (0, LANES - SC_WINDOW)))
    return scatter(tiles, *[idx_rows[k] for k in range(TOP_K)])


def _combine_kernel(h_ref, y0_ref, y1_ref, y2_ref, y3_ref, gate_ref, pp_ref, ps_ref, wpg_ref, wpp_ref,
                    g_ref, b_ref, *outs, n_ptiles):
    outp_ref, outs_ref = outs[0], outs[-1]
    tm = h_ref.shape[0]
    gate = gate_ref[...]
    moe = _load_token_tiles(y0_ref, (), tm) * gate[:, 0:1]
    for k, y_ref in enumerate((y1_ref, y2_ref, y3_ref), start=1):
        moe = moe + _load_token_tiles(y_ref, (), tm) * gate[:, k:k + 1]
    h2 = _layer_norm(DN_ALPHA * h_ref[...] + moe, g_ref[...], b_ref[...])
    embed_gate = _sigmoid(_dot(h2, wpg_ref[...]))

    def finish(p_ref, out_ref):
        out_ref[...] = h2 + embed_gate * _dot(p_ref[...], wpp_ref[...])

    is_prompt = pl.program_id(0) < n_ptiles
    pl.when(is_prompt)(functools.partial(finish, pp_ref, outp_ref))
    pl.when(jnp.logical_not(is_prompt))(functools.partial(finish, ps_ref, outs_ref))


def _combine(h, y, gate, p_prompt, p_sample, layer, n_p, split, wpg_bf16, wpp_bf16, g_row, b_row):
    n, d = h.shape
    e = p_prompt.shape[1]
    n_tiles = n // TM_TOK
    n_ptiles = n_p // TM_TOK
    n_stiles = (n - n_p) // TM_TOK
    row = lambda i: (i, 0)
    row_p = lambda i: (jnp.minimum(i, n_ptiles - 1), 0)
    row_s = lambda i: (jnp.maximum(i - n_ptiles, 0), 0)
    fixed = lambda i: (0, 0)
    choice = lambda k: pl.BlockSpec((TM_TOK * SUBLANES, LANES), lambda i: (k * n_tiles + i, 0))
    if split:
        out_specs = [pl.BlockSpec((TM_TOK, d), row_p), pl.BlockSpec((TM_TOK, d), row_s)]
        out_shape = [jax.ShapeDtypeStruct((n_p, d), F32), jax.ShapeDtypeStruct((n - n_p, d), F32)]
    else:
        out_specs = pl.BlockSpec((TM_TOK, d), row)
        out_shape = jax.ShapeDtypeStruct((n, d), F32)
    return pl.pallas_call(
        functools.partial(_combine_kernel, n_ptiles=n_ptiles),
        grid=(n_tiles,),
        in_specs=[pl.BlockSpec((TM_TOK, d), row)] + [choice(k) for k in range(TOP_K)]
                 + [pl.BlockSpec((TM_TOK, LANES), row),
                    pl.BlockSpec((TM_TOK, e), lambda i: (layer * n_ptiles + row_p(i)[0], 0)),
                    pl.BlockSpec((TM_TOK, e), lambda i: (layer * n_stiles + row_s(i)[0], 0)),
                    pl.BlockSpec((d, d), fixed), pl.BlockSpec((e, d), fixed), pl.BlockSpec((1, d), fixed),
                    pl.BlockSpec((1, d), fixed)],
        out_specs=out_specs,
        out_shape=out_shape,
        compiler_params=_cparams(("arbitrary",)),
        name="moe_combine_ln_embed",
    )(h, y, y, y, y, gate, p_prompt, p_sample, wpg_bf16, wpp_bf16, g_row, b_row)


def _layer_tail(h, ht, p_prompt, p_sample, layer, n_p, split, g2, b2, w_r, b_r, w_up, b_up, w_dn, b_dn,
                w_pg, w_pp):
    n, d = h.shape
    wr_pad = jnp.pad(w_r, ((0, 0), (0, LANES - N_EXPERTS)))
    br_row = jnp.pad(b_r, (0, LANES - N_EXPERTS))[None]
    n_asg = n * TOP_K
    n_blocks = n_asg // MOE_BLOCK + N_EXPERTS
    assert n_asg % MOE_BLOCK == 0 and n % SC_WINDOW == 0
    n_slots = n_blocks * MOE_BLOCK
    idx, gate, rank, pstart, tab = _router(h, wr_pad, br_row, n_blocks)
    experts = jnp.arange(N_EXPERTS, dtype=I32)[:, None, None]
    group_start = jnp.sum(jnp.where(idx[None, :TOP_K] == experts, pstart[0, :N_EXPERTS, None, None], 0), axis=0)
    slot_by_choice = group_start + rank[:TOP_K]
    xb = _sc_scatter_tiles(ht.reshape(n, SUBLANES, LANES), slot_by_choice, n_slots)
    yb = _experts(xb.reshape(n_slots * SUBLANES, LANES), tab[:n_blocks, 0], tab[0, 2:3], tab[:n_blocks, 1],
                  tab[:n_blocks, 3], layer, w_up, b_up[:, :, None, :], w_dn, b_dn[:, :, None, :])
    y = _sc_gather_tiles(yb.reshape(n_slots, SUBLANES, LANES), slot_by_choice.reshape(-1))
    return _combine(h, y.reshape(n_asg * SUBLANES, LANES), gate, p_prompt, p_sample, layer, n_p, split,
                    w_pg.astype(BF16), w_pp.astype(BF16), g2[None], b2[None])


def _lane_row(v, lane0):
    return jnp.zeros((1, LANES), F32).at[0, lane0:lane0 + v.shape[0]].set(v.astype(F32))


def kernel(x_prompt, x_sample, cache_fox_k, cache_fox_v, cache_fox_logf, state_gdn, state_gdn_conv,
           cache_pool, p_prompt, p_sample, w_in_ab, b_fgate, gdn_a_log, gdn_dt_bias, gdn_conv_w,
           gdn_norm_g, w_out_ab, pool_w, pool_scale, w_out_pool, ln1_g, ln1_b, ln2_g, ln2_b,
           w_router, b_router, w_expert_up, b_expert_up, w_expert_down, b_expert_down,
           w_ple_gate, w_ple_proj):
    n_pb, seq, d = x_prompt.shape
    n_sb, dseq, _ = x_sample.shape
    past = cache_fox_k.shape[2]
    assert n_pb == 1 and dseq == CHUNK and past % dseq == 0 and seq % TQ == 0
    assert d == SUBLANES * LANES
    n_p = n_pb * seq
    n_s = n_sb * dseq
    n = n_p + n_s
    assert n_p % TM_TOK == 0 and n_s % TM_TOK == 0
    n_layers = p_prompt.shape[0]
    pp_all = p_prompt.reshape(n_layers * n_p, -1)
    ps_all = p_sample.reshape(n_layers * n_s, -1)

    def tail(h, ht, i, split):
        return _layer_tail(h, ht, pp_all, ps_all, i, n_p, split, ln2_g[i], ln2_b[i], w_router[i], b_router[i],
                           w_expert_up, b_expert_up, w_expert_down, b_expert_down, w_ple_gate[i],
                           w_ple_proj[i])

    w_in = w_in_ab[0]
    n_small = 3 * N_HEADS
    ff0 = 3 * WIDTH
    gq0 = ff0 + N_HEADS
    ga0 = gq0 + 4 * WIDTH
    w_small = jnp.concatenate([w_in[:, ff0:gq0], w_in[:, ga0:ga0 + 2 * N_HEADS],
                               jnp.zeros((d, LANES - n_small), F32)], axis=1)
    w_all = jnp.concatenate([w_in[:, :ff0], w_in[:, gq0:ga0], w_small], axis=1)
    bf_row = _lane_row(b_fgate[0], LANE_F)
    up = _proj(x_prompt.reshape(n_p, d), w_all.astype(BF16), bf_row, TM_PROJ, U_COLS, False)
    us = _proj(x_sample.reshape(n_s, d), w_all, bf_row, TM_TOK, LANES, True)

    u3 = up[None]
    us3 = us.reshape(n_sb, dseq, U_COLS)
    cq_p, ck_p = _cumsum(u3, 1, n_p, CUMSUM_TILE, COL_S // LANES)
    lf_s = jnp.concatenate(
        [jnp.pad(cache_fox_logf[0].astype(F32), ((0, 0), (0, 0), (0, LANES - N_HEADS))),
         us3[:, :, COL_S:]], axis=1)
    cq_s, ck_s = _cumsum(lf_s, n_sb, past + dseq, past + dseq, 0)

    of_p = _fox(u3, lambda b, i: (0, i, COL_Q // WIDTH), u3, lambda b, j: (0, j, COL_K // WIDTH),
                u3, lambda b, j: (0, j, COL_V // WIDTH), cq_p, lambda b, i: (0, i, 0),
                ck_p, lambda b, j: (0, 0, j), 1, n_p // TQ, TQ, TK, 0)
    k_all = jnp.concatenate([cache_fox_k[0].reshape(n_sb, past, WIDTH), us3[:, :, COL_K:COL_K + WIDTH]], axis=1)
    v_all = jnp.concatenate([cache_fox_v[0].reshape(n_sb, past, WIDTH), us3[:, :, COL_V:COL_V + WIDTH]], axis=1)
    of_s = _fox(us3, lambda b, i: (b, 0, COL_Q // WIDTH), k_all, lambda b, j: (b, 0, 0),
                v_all, lambda b, j: (b, 0, 0), cq_s, lambda b, i: (b, past // dseq, 0),
                ck_s, lambda b, j: (b, 0, 0), n_sb, 1, dseq, past + dseq, past, hi=True)

    gw = 3 * WIDTH
    conv_w = jnp.pad(gdn_conv_w[0], ((0, SUBLANES - CONV_WIDTH), (0, 0)))
    gdn_args = (conv_w, _lane_row(gdn_a_log[0], LANE_A), _lane_row(gdn_dt_bias[0], LANE_A),
                gdn_norm_g[0][None])
    n_pstep = n_p // GDN_ROWS
    ends = lambda k: (jnp.asarray((np.arange(k) == 0).astype(np.int32)),
                      jnp.asarray((np.arange(k) == k - 1).astype(np.int32)))
    og_p, st_p = _gdn(up.reshape(n_pstep, GDN_ROWS, U_COLS), False, *ends(n_pstep),
                      jnp.zeros((n_pstep,), I32), jnp.zeros((1, SUBLANES, gw), F32), gdn_args[0],
                      jnp.zeros((1, N_HEADS, HEAD_DIM, HEAD_DIM), F32), *gdn_args[1:])
    conv_past = jnp.pad(state_gdn_conv[0].astype(F32), ((0, 0), (SUBLANES - (CONV_WIDTH - 1), 0), (0, 0)))
    ones = jnp.ones((n_sb,), I32)
    og_s, st_s = _gdn(us3, True, ones, ones, jnp.arange(n_sb, dtype=I32), conv_past, gdn_args[0],
                      state_gdn[0].astype(F32), *gdn_args[1:])
    h, ht = _outproj_ln((x_prompt.reshape(n_p, d), of_p.reshape(n_p, WIDTH), og_p.reshape(n_p, WIDTH)),
                        (x_sample.reshape(n_s, d), of_s.reshape(n_s, WIDTH), og_s.reshape(n_s, WIDTH)),
                        w_out_ab[0], ln1_g[0][None], ln1_b[0][None])
    x1 = tail(h, ht, 0, False)

    pool_args = (pool_w[0].astype(BF16), pool_scale[0][None], w_out_pool[0].astype(BF16),
                 ln1_g[1][None], ln1_b[1][None])
    ratio = TM_TOK // POOL_HALO
    x1_halo = x1.reshape(n // POOL_HALO, POOL_HALO, d)
    h_pool = _pool_ln(x1, lambda i: (i, 0), x1_halo, lambda i: (jnp.maximum(i * ratio - 1, 0), 0, 0),
                      n_p // TM_TOK, TM_TOK, 0, True, *pool_args)
    cache16 = jnp.pad(cache_pool[0].astype(F32), ((0, 0), (POOL_HALO - POOL_STATE, 0), (0, 0)))
    h, ht = _pool_ln(x1, lambda i: (n_p // dseq + i, 0), cache16, lambda i: (i, 0, 0),
                     n_sb, dseq, past, False, *pool_args, into=h_pool)
    x2_p, x2_s = tail(h, ht, 1, True)

    return (x2_p.reshape(n_pb, seq, d), x2_s.reshape(n_sb, dseq, d),
            up[:, COL_K:COL_K + WIDTH].reshape(1, n_pb, seq, N_HEADS, HEAD_DIM),
            up[:, COL_V:COL_V + WIDTH].reshape(1, n_pb, seq, N_HEADS, HEAD_DIM),
            up[:, COL_S:COL_S + N_HEADS].reshape(1, n_pb, seq, N_HEADS),
            st_p.reshape(1, n_pb, N_HEADS, HEAD_DIM, HEAD_DIM),
            up[seq - (CONV_WIDTH - 1):, COL_G:COL_G + gw].reshape(1, n_pb, CONV_WIDTH - 1, gw),
            x1[n_p - POOL_STATE:n_p].reshape(1, n_pb, POOL_STATE, d),
            us[:, COL_K:COL_K + WIDTH].reshape(1, n_sb, dseq, N_HEADS, HEAD_DIM),
            us[:, COL_V:COL_V + WIDTH].reshape(1, n_sb, dseq, N_HEADS, HEAD_DIM),
            us[:, COL_S:COL_S + N_HEADS].reshape(1, n_sb, dseq, N_HEADS),
            st_s.reshape(1, n_sb, N_HEADS, HEAD_DIM, HEAD_DIM),
            us[:, COL_G:COL_G + gw].reshape(n_sb, dseq, gw)[:, dseq - (CONV_WIDTH - 1):].reshape(
                1, n_sb, CONV_WIDTH - 1, gw),
            x1[n_p:].reshape(n_sb, dseq, d)[:, dseq - POOL_STATE:].reshape(1, n_sb, POOL_STATE, d))
```

```python
import functools

import numpy as np
import jax
import jax.numpy as jnp
from jax import lax
from jax.experimental import pallas as pl
from jax.experimental.pallas import tpu as pltpu
from jax.experimental.pallas import tpu_sc as plsc

F32 = jnp.float32
BF16 = jnp.bfloat16
I32 = jnp.int32
HIGHEST = lax.Precision.HIGHEST

LANES = 128
SUBLANES = 8
VMEM_LIMIT = 56 * 1024 * 1024

HEAD_DIM = 128
N_HEADS = 4
WIDTH = N_HEADS * HEAD_DIM
CHUNK = 64
CONV_WIDTH = 4
POOL_WINDOWS = (2, 4, 8, 16)
POOL_HALO = 16
POOL_STATE = 15
N_EXPERTS = 32
TOP_K = 4
SWIGLU_LIMIT = 7.0
SWIGLU_ALPHA = 1.702
DEPTH = 2
DN_ALPHA = (2 * DEPTH) ** 0.25
LN_EPS = 1e-5
NORM_EPS = 1e-6
NEG_INF = -1e30
LOG2E = 1.4426950408889634

COL_Q, COL_K, COL_V = 0, WIDTH, 2 * WIDTH
COL_G = 3 * WIDTH
COL_Z = 6 * WIDTH
COL_S = 7 * WIDTH
U_COLS = COL_S + LANES
LANE_F, LANE_A, LANE_B = 0, N_HEADS, 2 * N_HEADS

TM_PROJ = 512
TM_TOK = 512
SC_WINDOW = 32
MOE_BLOCK = 512
TQ = 1024
TK = 1024
CUMSUM_TILE = 512
GDN_ROWS = 256
GDN_CHUNK = CHUNK


def _cparams(sem):
    return pltpu.CompilerParams(dimension_semantics=sem, vmem_limit_bytes=VMEM_LIMIT)


def _softplus(x):
    return jnp.maximum(x, 0.0) + jnp.log1p(jnp.exp(-jnp.abs(x)))


def _sigmoid(x):
    return 1.0 / (1.0 + jnp.exp(-x))


def _silu(x):
    return x * _sigmoid(x)


def _layer_norm(y, g, b):
    mu = jnp.mean(y, axis=-1, keepdims=True)
    yc = y - mu
    var = jnp.mean(yc * yc, axis=-1, keepdims=True)
    return yc * lax.rsqrt(var + LN_EPS) * g + b


def _dot_general(a, b, dims, hi):
    if hi:
        return lax.dot_general(a.astype(F32), b.astype(F32), (dims, ((), ())), precision=HIGHEST,
                               preferred_element_type=F32)
    return lax.dot_general(a.astype(BF16), b.astype(BF16), (dims, ((), ())), preferred_element_type=F32)


def _dot(a, b, hi=False):
    return _dot_general(a, b, ((1,), (0,)), hi)


def _dot_nt(a, b, hi=False):
    return _dot_general(a, b, ((1,), (1,)), hi)


def _dot_tn(a, b, hi=False):
    return _dot_general(a, b, ((0,), (0,)), hi)


def _spread_lanes(x, width):
    if width % LANES == 0:
        return jnp.concatenate([x] * (width // LANES), axis=1)
    return jnp.broadcast_to(x[:, 0:1], (x.shape[0], width))


def _load_token_tiles(ref, lead, n_tok):
    return jnp.concatenate([ref[(*lead, pl.ds(j, n_tok, stride=SUBLANES), slice(None))]
                            for j in range(SUBLANES)], axis=1)


def _store_token_tiles(ref, lead, x):
    for j in range(SUBLANES):
        ref[(*lead, pl.ds(j, x.shape[0], stride=SUBLANES), slice(None))] = x[:, j * LANES:(j + 1) * LANES]


def _lanes_to_rows(x, lane0):
    r = lax.broadcasted_iota(I32, (SUBLANES, LANES), 0)
    c = lax.broadcasted_iota(I32, (SUBLANES, LANES), 1)
    sel = (c == r + lane0).astype(F32)
    return lax.dot_general(sel, x, (((1,), (1,)), ((), ())), precision=HIGHEST,
                           preferred_element_type=F32)


def _proj_kernel(x_ref, w_ref, bf_ref, u_ref, *, hi):
    u = _dot(x_ref[...], w_ref[...], hi)
    u_ref[...] = u

    @pl.when(pl.program_id(1) == pl.num_programs(1) - 1)
    def _():
        small = u[:, u.shape[1] - LANES:]
        lane = lax.broadcasted_iota(I32, small.shape, 1)
        logf = -_softplus(-(small + bf_ref[...]))
        u_ref[:, u.shape[1] - LANES:] = jnp.where(lane < LANE_A, logf, small)


def _proj(x, w, bf_row, tm, tn, hi):
    n, d = x.shape
    m = w.shape[1]
    return pl.pallas_call(
        functools.partial(_proj_kernel, hi=hi),
        grid=(n // tm, m // tn),
        in_specs=[pl.BlockSpec((tm, d), lambda i, j: (i, 0)),
                  pl.BlockSpec((d, tn), lambda i, j: (0, j)),
                  pl.BlockSpec((1, LANES), lambda i, j: (0, 0))],
        out_specs=pl.BlockSpec((tm, tn), lambda i, j: (i, j)),
        out_shape=jax.ShapeDtypeStruct((n, m), F32),
        compiler_params=_cparams(("parallel", "parallel")),
        name="in_proj",
    )(x, w, bf_row)


def _cumsum_kernel(lf_ref, crep_ref, crow_ref, carry_ref):
    @pl.when(pl.program_id(1) == 0)
    def _():
        carry_ref[...] = jnp.zeros_like(carry_ref)

    lf = lf_ref[0]
    t = lf.shape[0]
    r = lax.broadcasted_iota(I32, (t, t), 0)
    c = lax.broadcasted_iota(I32, (t, t), 1)
    tril = (c <= r).astype(F32)
    cs = jnp.dot(tril, lf, precision=HIGHEST, preferred_element_type=F32) + carry_ref[0:1, :]
    carry_ref[...] = jnp.broadcast_to(cs[t - 1:t, :], carry_ref.shape)
    c2 = cs * LOG2E
    crow_ref[0] = _lanes_to_rows(c2, LANE_F)
    for h in range(N_HEADS):
        crep_ref[0, :, h * HEAD_DIM:(h + 1) * HEAD_DIM] = jnp.broadcast_to(
            c2[:, LANE_F + h:LANE_F + h + 1], (t, HEAD_DIM))


def _cumsum(arr, n_batch, length, tl, col_block):
    return pl.pallas_call(
        _cumsum_kernel,
        grid=(n_batch, length // tl),
        in_specs=[pl.BlockSpec((1, tl, LANES), lambda b, j: (b, j, col_block))],
        out_specs=[pl.BlockSpec((1, tl, WIDTH), lambda b, j: (b, j, 0)),
                   pl.BlockSpec((1, SUBLANES, tl), lambda b, j: (b, 0, j))],
        out_shape=[jax.ShapeDtypeStruct((n_batch, length, WIDTH), F32),
                   jax.ShapeDtypeStruct((n_batch, SUBLANES, length), F32)],
        scratch_shapes=[pltpu.VMEM((SUBLANES, LANES), F32)],
        compiler_params=_cparams(("parallel", "arbitrary")),
        name="logf_cumsum",
    )(arr)


def _fox_kernel(qi_ref, kj_ref, last_ref, q_ref, k_ref, v_ref, cq_ref, ck_ref, o_ref,
                m_ref, l_ref, acc_ref, *, tq, tk, past, hi):
    s_idx = pl.program_id(1)
    qi = qi_ref[s_idx]
    kj = kj_ref[s_idx]

    @pl.when(kj == 0)
    def _():
        m_ref[...] = jnp.full_like(m_ref, NEG_INF)
        l_ref[...] = jnp.zeros_like(l_ref)
        acc_ref[...] = jnp.zeros_like(acc_ref)

    def update(masked):
        if masked:
            q_pos = past + qi * tq + lax.broadcasted_iota(I32, (tq, tk), 0)
            k_pos = kj * tk + lax.broadcasted_iota(I32, (tq, tk), 1)
            visible = k_pos <= q_pos
        for h in range(N_HEADS):
            cols = slice(h * HEAD_DIM, (h + 1) * HEAD_DIM)
            q = q_ref[0, :, cols] * (HEAD_DIM ** -0.5 * LOG2E)
            t = _dot_nt(q, k_ref[0, :, cols], hi) - ck_ref[0, h:h + 1, :]
            if masked:
                t = jnp.where(visible, t, NEG_INF)
            cq = cq_ref[0, :, cols]
            m_prev = m_ref[h]
            m_new = jnp.maximum(m_prev, jnp.max(t, axis=-1, keepdims=True) + cq)
            p = jnp.exp2(t - _spread_lanes(m_new - cq, tk))
            alpha = jnp.exp2(m_prev - m_new)
            l_ref[h] = alpha * l_ref[h] + jnp.sum(p, axis=-1, keepdims=True)
            acc_ref[:, cols] = alpha * acc_ref[:, cols] + _dot(p, v_ref[0, :, cols], hi)
            m_ref[h] = m_new

    crosses_diagonal = kj * tk + (tk - 1) > past + qi * tq
    pl.when(crosses_diagonal)(functools.partial(update, True))
    pl.when(jnp.logical_not(crosses_diagonal))(functools.partial(update, False))

    @pl.when(last_ref[s_idx] == 1)
    def _():
        for h in range(N_HEADS):
            cols = slice(h * HEAD_DIM, (h + 1) * HEAD_DIM)
            o_ref[0, :, cols] = acc_ref[:, cols] / l_ref[h]


def _fox_schedule(n_q, tq, tk, past):
    qi, kj, last = [], [], []
    for i in range(n_q):
        hi = (past + (i + 1) * tq - 1) // tk
        for j in range(hi + 1):
            qi.append(i)
            kj.append(j)
            last.append(1 if j == hi else 0)
    return (jnp.asarray(np.array(qi, np.int32)), jnp.asarray(np.array(kj, np.int32)),
            jnp.asarray(np.array(last, np.int32)))


def _fox(q_arr, q_map, k_arr, k_map, v_arr, v_map, cq_arr, cq_map, ck_arr, ck_map,
         n_batch, n_q, tq, tk, past, hi=False):
    qi, kj, last = _fox_schedule(n_q, tq, tk, past)
    n_steps = int(qi.shape[0])
    spec = lambda shape, fn, tab: pl.BlockSpec(shape, lambda b, s, qi_r, kj_r, la_r: fn(b, (qi_r if tab == 'q' else kj_r)[s]))
    return pl.pallas_call(
        functools.partial(_fox_kernel, tq=tq, tk=tk, past=past, hi=hi),
        grid_spec=pltpu.PrefetchScalarGridSpec(
            num_scalar_prefetch=3,
            grid=(n_batch, n_steps),
            in_specs=[spec((1, tq, WIDTH), q_map, 'q'),
                      spec((1, tk, WIDTH), k_map, 'k'),
                      spec((1, tk, WIDTH), v_map, 'k'),
                      spec((1, tq, WIDTH), cq_map, 'q'),
                      spec((1, SUBLANES, tk), ck_map, 'k')],
            out_specs=spec((1, tq, WIDTH), lambda b, i: (b, i, 0), 'q'),
            scratch_shapes=[pltpu.VMEM((N_HEADS, tq, HEAD_DIM), F32),
                            pltpu.VMEM((N_HEADS, tq, HEAD_DIM), F32),
                            pltpu.VMEM((tq, WIDTH), F32)]),
        out_shape=jax.ShapeDtypeStruct((n_batch, n_q * tq, WIDTH), F32),
        compiler_params=_cparams(("parallel", "arbitrary")),
        name="fox_attention",
    )(qi, kj, last, q_arr, k_arr, v_arr, cq_arr, ck_arr)


def _gdn_kernel(first_ref, last_ref, seq_ref,
                pre_ref, z_ref, sm_ref, cpast_ref, convw_ref, s0_ref, alog_ref, dtb_ref, ng_ref,
                o_ref, sout_ref, stage_ref, s_ref, *, rows, chunk, hi):
    dot, dot_nt, dot_tn = (functools.partial(f, hi=hi) for f in (_dot, _dot_nt, _dot_tn))
    step = pl.program_id(0)
    halo = SUBLANES
    n_chunks = rows // chunk

    @pl.when(first_ref[step] == 1)
    def _():
        stage_ref[0:halo, :] = cpast_ref[0]
        s_ref[...] = s0_ref[0]

    stage_ref[halo:halo + rows, :] = pre_ref[0]
    conv = stage_ref[halo:halo + rows, :] * convw_ref[CONV_WIDTH - 1:CONV_WIDTH, :]
    for j in range(1, CONV_WIDTH):
        conv = conv + (stage_ref[halo - j:halo - j + rows, :]
                       * convw_ref[CONV_WIDTH - 1 - j:CONV_WIDTH - j, :])
    stage_ref[0:halo, :] = stage_ref[rows:rows + halo, :]
    act = _silu(conv)

    small = sm_ref[0]
    beta_all = _sigmoid(small)
    g_all = -jnp.exp(alog_ref[...]) * _softplus(small + dtb_ref[...])
    r = lax.broadcasted_iota(I32, (rows, rows), 0)
    c = lax.broadcasted_iota(I32, (rows, rows), 1)
    same_chunk = (r // chunk) == (c // chunk)
    incl = same_chunk & (c <= r)
    strict = same_chunk & (c < r)
    eye = (c == r).astype(F32)
    gc_all = jnp.dot(incl.astype(F32), g_all, precision=HIGHEST, preferred_element_type=F32)
    gc_rows = _lanes_to_rows(gc_all, LANE_A)

    heads = range(N_HEADS)
    head_cols = [slice(h * HEAD_DIM, (h + 1) * HEAD_DIM) for h in heads]
    q, k, gc, decay, kb, vb, low = [], [], [], [], [], [], []
    for h in heads:
        qh = act[:, h * HEAD_DIM:(h + 1) * HEAD_DIM]
        kh = act[:, WIDTH + h * HEAD_DIM:WIDTH + (h + 1) * HEAD_DIM]
        vh = act[:, 2 * WIDTH + h * HEAD_DIM:2 * WIDTH + (h + 1) * HEAD_DIM]
        q.append(qh * lax.rsqrt(jnp.sum(qh * qh, axis=-1, keepdims=True) + NORM_EPS) * (HEAD_DIM ** -0.5))
        k.append(kh * lax.rsqrt(jnp.sum(kh * kh, axis=-1, keepdims=True) + NORM_EPS))
        beta = beta_all[:, LANE_B + h:LANE_B + h + 1]
        gc.append(gc_all[:, LANE_A + h:LANE_A + h + 1])
        diff = gc[h] - gc_rows[h:h + 1, :]
        decay.append(jnp.where(incl, jnp.exp(jnp.where(incl, diff, 0.0)), 0.0))
        kb.append(k[h] * beta)
        vb.append(vh * beta)
    for h in heads:
        low.append(jnp.where(strict, dot_nt(kb[h], k[h]) * decay[h], 0.0))
    inv = [eye - low[h] for h in heads]
    pw = [dot(low[h], low[h]) for h in heads]
    n_sq = chunk.bit_length() - 2
    for it in range(n_sq):
        inv = [inv[h] + dot(inv[h], pw[h]) for h in heads]
        if it + 1 < n_sq:
            pw = [dot(pw[h], pw[h]) for h in heads]
    egc = [jnp.exp(gc[h]) for h in heads]
    uw = [dot(inv[h], jnp.concatenate([vb[h], kb[h] * egc[h]], axis=1)) for h in heads]
    intra = [jnp.where(incl, dot_nt(q[h], k[h]) * decay[h], 0.0) for h in heads]
    qd = [q[h] * egc[h] for h in heads]
    g_last = [[gc[h][(g + 1) * chunk - 1:(g + 1) * chunk, :] for g in range(n_chunks)] for h in heads]
    kd = [k[h] * jnp.exp(jnp.concatenate([jnp.broadcast_to(gl, (chunk, 1)) for gl in g_last[h]], axis=0)
                         - gc[h]) for h in heads]
    state = [s_ref[h] for h in heads]
    v_new = [[] for _ in heads]
    for g in range(n_chunks):
        rs = slice(g * chunk, (g + 1) * chunk)
        for h in heads:
            v_new[h].append(uw[h][rs, :HEAD_DIM] - dot(uw[h][rs, HEAD_DIM:], state[h]))
        for h in heads:
            v_rows = jnp.concatenate(
                v_new[h] + [jnp.zeros((rows - (g + 1) * chunk, HEAD_DIM), F32)] * (g + 1 < n_chunks), axis=0)
            o = dot(qd[h][rs, :], state[h]) + dot(intra[h][rs, :], v_rows)
            state[h] = state[h] * jnp.exp(g_last[h][g]) + dot_tn(kd[h][rs, :], v_new[h][g])
            o = (o * lax.rsqrt(jnp.mean(o * o, axis=-1, keepdims=True) + NORM_EPS)
                 * ng_ref[...] * _silu(z_ref[0, rs, head_cols[h]]))
            o_ref[0, rs, head_cols[h]] = o
    for h in heads:
        s_ref[h] = state[h]

    @pl.when(last_ref[step] == 1)
    def _():
        sout_ref[0] = s_ref[...]


def _gdn(u_view, hi, first, last, seq, conv_past, conv_w, s0, alog_row, dtb_row, ng_row):
    n_steps, rows, _ = u_view.shape
    n_seq = s0.shape[0]
    gw = 3 * WIDTH
    return pl.pallas_call(
        functools.partial(_gdn_kernel, rows=rows, chunk=min(rows, GDN_CHUNK), hi=hi),
        grid_spec=pltpu.PrefetchScalarGridSpec(
            num_scalar_prefetch=3,
            grid=(n_steps,),
            in_specs=[pl.BlockSpec((1, rows, gw), lambda s, f, l, q: (s, 0, COL_G // gw)),
                      pl.BlockSpec((1, rows, WIDTH), lambda s, f, l, q: (s, 0, COL_Z // WIDTH)),
                      pl.BlockSpec((1, rows, LANES), lambda s, f, l, q: (s, 0, COL_S // LANES)),
                      pl.BlockSpec((1, SUBLANES, gw), lambda s, f, l, q: (q[s], 0, 0)),
                      pl.BlockSpec((SUBLANES, gw), lambda s, f, l, q: (0, 0)),
                      pl.BlockSpec((1, N_HEADS, HEAD_DIM, HEAD_DIM), lambda s, f, l, q: (q[s], 0, 0, 0)),
                      pl.BlockSpec((1, LANES), lambda s, f, l, q: (0, 0)),
                      pl.BlockSpec((1, LANES), lambda s, f, l, q: (0, 0)),
                      pl.BlockSpec((1, LANES), lambda s, f, l, q: (0, 0))],
            out_specs=[pl.BlockSpec((1, rows, WIDTH), lambda s, f, l, q: (s, 0, 0)),
                       pl.BlockSpec((1, N_HEADS, HEAD_DIM, HEAD_DIM), lambda s, f, l, q: (q[s], 0, 0, 0))],
            scratch_shapes=[pltpu.VMEM((rows + SUBLANES, gw), F32),
                            pltpu.VMEM((N_HEADS, HEAD_DIM, HEAD_DIM), F32)]),
        out_shape=[jax.ShapeDtypeStruct((n_steps, rows, WIDTH), F32),
                   jax.ShapeDtypeStruct((n_seq, N_HEADS, HEAD_DIM, HEAD_DIM), F32)],
        compiler_params=_cparams(("arbitrary",)),
        name="gated_deltanet",
    )(first, last, seq, u_view, u_view, u_view, conv_past, conv_w, s0, alog_row, dtb_row, ng_row)


def _outproj_ln_kernel(xp_ref, ofp_ref, ogp_ref, xs_ref, ofs_ref, ogs_ref, w_ref, wf_ref, g_ref, b_ref,
                       h_ref, ht_ref, *, hi_from):
    def run(hi, x_ref, of_ref, og_ref):
        w = wf_ref if hi else w_ref
        mix = _dot(of_ref[...], w[0:WIDTH, :], hi) + _dot(og_ref[...], w[WIDTH:2 * WIDTH, :], hi)
        h = _layer_norm(DN_ALPHA * x_ref[...] + mix, g_ref[...], b_ref[...])
        h_ref[...] = h
        _store_token_tiles(ht_ref, (), h)

    pl.when(pl.program_id(0) < hi_from)(functools.partial(run, False, xp_ref, ofp_ref, ogp_ref))
    pl.when(pl.program_id(0) >= hi_from)(functools.partial(run, True, xs_ref, ofs_ref, ogs_ref))


def _outproj_ln(prompt, sample, w, g_row, b_row):
    n_p, d = prompt[0].shape
    n = n_p + sample[0].shape[0]
    hi_from = n_p // TM_TOK
    row = lambda i: (i, 0)
    row_p = lambda i: (jnp.minimum(i, hi_from - 1), 0)
    row_s = lambda i: (jnp.maximum(i - hi_from, 0), 0)
    fixed = lambda i: (0, 0)
    group = lambda rows: [pl.BlockSpec((TM_TOK, d), rows), pl.BlockSpec((TM_TOK, WIDTH), rows),
                          pl.BlockSpec((TM_TOK, WIDTH), rows)]
    return pl.pallas_call(
        functools.partial(_outproj_ln_kernel, hi_from=hi_from),
        grid=(n // TM_TOK,),
        in_specs=group(row_p) + group(row_s) + [pl.BlockSpec((2 * WIDTH, d), fixed),
                                                pl.BlockSpec((2 * WIDTH, d), fixed),
                                                pl.BlockSpec((1, d), fixed), pl.BlockSpec((1, d), fixed)],
        out_specs=[pl.BlockSpec((TM_TOK, d), row), pl.BlockSpec((TM_TOK * SUBLANES, LANES), row)],
        out_shape=[jax.ShapeDtypeStruct((n, d), F32), jax.ShapeDtypeStruct((n * SUBLANES, LANES), F32)],
        compiler_params=_cparams(("parallel",)),
        name="out_proj_ln",
    )(*prompt, *sample, w.astype(BF16), w, g_row, b_row)


def _pool_ln_kernel(x_ref, halo_ref, pw_ref, ps_ref, w_ref, g_ref, b_ref, h_ref, ht_ref, stage_ref,
                    lvl_a_ref, lvl_b_ref, *, tm, pos0, zero_first_halo):
    i = pl.program_id(0)
    pad = SUBLANES
    n_rows = POOL_HALO + tm
    for ref in (stage_ref, lvl_a_ref, lvl_b_ref):
        ref[0:pad, :] = jnp.zeros((pad, ref.shape[1]), F32)
    stage_ref[pad:pad + POOL_HALO, :] = halo_ref[0]
    if zero_first_halo:
        @pl.when(i == 0)
        def _():
            stage_ref[pad:pad + POOL_HALO, :] = jnp.zeros((POOL_HALO, stage_ref.shape[1]), F32)
    x = x_ref[...]
    stage_ref[pad + POOL_HALO:pad + n_rows, :] = x
    gdim = x.shape[1] // len(POOL_WINDOWS)
    pos = pos0 + lax.broadcasted_iota(I32, (tm, 1), 0)
    if zero_first_halo:
        pos = pos + i * tm

    def window_sum(cols, win):
        assert win & (win - 1) == 0 and win <= POOL_HALO
        src, src_cols, span, level = stage_ref, cols, 1, 0
        while span < win:
            dst = (lvl_a_ref, lvl_b_ref)[level % 2]
            dst[pad:pad + n_rows, :] = (src[pad:pad + n_rows, src_cols]
                                        + src[pad - span:pad - span + n_rows, src_cols])
            src, src_cols, span, level = dst, slice(None), 2 * span, level + 1
        return src[pad + POOL_HALO:pad + n_rows, src_cols]

    parts = []
    for gi, win in enumerate(POOL_WINDOWS):
        cols = slice(gi * gdim, (gi + 1) * gdim)
        s = window_sum(cols, win)
        cnt = jnp.minimum(pos + 1, win).astype(F32)
        zg = s / cnt - x[:, cols]
        parts.append(_dot(zg, pw_ref[gi]))
    zg = jnp.concatenate(parts, axis=-1) * ps_ref[...]
    mix = _dot(zg, w_ref[...])
    h = _layer_norm(DN_ALPHA * x + mix, g_ref[...], b_ref[...])
    h_ref[...] = h
    _store_token_tiles(ht_ref, (), h)


def _pool_ln_into_kernel(*refs, **kw):
    _pool_ln_kernel(*refs[:7], *refs[9:], **kw)


def _pool_ln(x, x_map, halo_arr, halo_map, n_tiles, tm, pos0, zero_first_halo,
             pw_bf16, ps_row, w_bf16, g_row, b_row, into=None):
    n, d = x.shape
    gdim = d // len(POOL_WINDOWS)
    fixed = lambda i: (0, 0)
    in_specs = [pl.BlockSpec((tm, d), x_map),
                pl.BlockSpec((1, POOL_HALO, d), halo_map),
                pl.BlockSpec((len(POOL_WINDOWS), gdim, gdim), lambda i: (0, 0, 0)),
                pl.BlockSpec((1, d), fixed), pl.BlockSpec((d, d), fixed),
                pl.BlockSpec((1, d), fixed), pl.BlockSpec((1, d), fixed)]
    args = (x, halo_arr, pw_bf16, ps_row, w_bf16, g_row, b_row)
    kw = dict(tm=tm, pos0=pos0, zero_first_halo=zero_first_halo)
    return pl.pallas_call(
        functools.partial(_pool_ln_kernel if into is None else _pool_ln_into_kernel, **kw),
        grid=(n_tiles,),
        in_specs=in_specs + ([] if into is None else [pl.BlockSpec(memory_space=pl.ANY)] * 2),
        out_specs=[pl.BlockSpec((tm, d), x_map),
                   pl.BlockSpec((tm * SUBLANES, LANES), x_map)],
        out_shape=[jax.ShapeDtypeStruct((n, d), F32), jax.ShapeDtypeStruct((n * SUBLANES, LANES), F32)],
        input_output_aliases={} if into is None else {7: 0, 8: 1},
        scratch_shapes=[pltpu.VMEM((SUBLANES + POOL_HALO + tm, d), F32),
                        pltpu.VMEM((SUBLANES + POOL_HALO + tm, gdim), F32),
                        pltpu.VMEM((SUBLANES + POOL_HALO + tm, gdim), F32)],
        compiler_params=_cparams(("arbitrary",)),
        name="pool_mixer_ln",
    )(*args, *(() if into is None else into))


def _router_kernel(h_ref, wr_ref, br_ref, idx_ref, gate_ref, rank_ref, pstart_ref, tab_ref, carry_ref):
    @pl.when(pl.program_id(0) == 0)
    def _():
        carry_ref[...] = jnp.zeros_like(carry_ref)

    tm = h_ref.shape[0]
    lane = lax.broadcasted_iota(I32, (tm, LANES), 1).astype(F32)
    logits = jnp.dot(h_ref[...], wr_ref[...], precision=HIGHEST, preferred_element_type=F32)
    work = jnp.where(lane < N_EXPERTS, logits + br_ref[...], -jnp.inf)
    vals, ids = [], []
    for _ in range(TOP_K):
        m = jnp.max(work, axis=-1, keepdims=True)
        ik = jnp.min(jnp.where(work == m, lane, float(LANES)), axis=-1, keepdims=True)
        vals.append(m)
        ids.append(ik)
        work = jnp.where(lane == ik, -jnp.inf, work)
    exps = [jnp.exp(v - vals[0]) for v in vals]
    denom = exps[0]
    for e in exps[1:]:
        denom = denom + e
    multihot = jnp.zeros((tm, LANES), F32)
    idx_out = jnp.zeros((tm, LANES), F32)
    gate_out = jnp.zeros((tm, LANES), F32)
    for k in range(TOP_K):
        multihot = multihot + (lane == ids[k]).astype(F32)
        idx_out = jnp.where(lane == k, ids[k], idx_out)
        gate_out = jnp.where(lane == k, exps[k] / denom, gate_out)
    r = lax.broadcasted_iota(I32, (tm, tm), 0)
    c = lax.broadcasted_iota(I32, (tm, tm), 1)
    before = _dot((c < r).astype(F32), multihot) + carry_ref[0:1, :]
    rank_out = jnp.zeros((tm, LANES), F32)
    for k in range(TOP_K):
        rk = jnp.sum(jnp.where(lane == ids[k], before, 0.0), axis=-1, keepdims=True)
        rank_out = jnp.where(lane == k, rk, rank_out)
    idx_ref[...] = jnp.transpose(idx_out)[:SUBLANES].astype(I32)
    rank_ref[...] = jnp.transpose(rank_out)[:SUBLANES].astype(I32)
    gate_ref[...] = gate_out
    total = carry_ref[0:1, :] + jnp.sum(multihot, axis=0, keepdims=True)
    carry_ref[...] = jnp.broadcast_to(total, carry_ref.shape)

    @pl.when(pl.program_id(0) == pl.num_programs(0) - 1)
    def _():
        n_rows = tab_ref.shape[0]
        padded = jnp.floor((total + (MOE_BLOCK - 1)) / MOE_BLOCK) * MOE_BLOCK
        rr = lax.broadcasted_iota(I32, (LANES, LANES), 0)
        cc = lax.broadcasted_iota(I32, (LANES, LANES), 1)
        pend = jnp.dot(jnp.broadcast_to(padded, (SUBLANES, LANES)), (rr <= cc).astype(F32),
                       precision=HIGHEST, preferred_element_type=F32)[0:1, :]
        pstart = pend - padded
        n_used = jnp.max(pend, axis=-1, keepdims=True) / MOE_BLOCK
        elane = lax.broadcasted_iota(I32, (n_rows, LANES), 1)
        blk = jnp.minimum(lax.broadcasted_iota(I32, (n_rows, 1), 0).astype(F32), n_used - 1.0) * MOE_BLOCK
        ends_before = jnp.where((elane < N_EXPERTS) & (pend <= blk), 1.0, 0.0)
        block_e = jnp.minimum(jnp.sum(ends_before, axis=-1, keepdims=True), N_EXPERTS - 1.0)
        mine = elane.astype(F32) == block_e
        filled = jnp.sum(jnp.where(mine, total - (blk - pstart), 0.0), axis=-1, keepdims=True)
        block_valid = jnp.clip(filled, 0.0, float(MOE_BLOCK))
        group_end = jnp.sum(jnp.where(mine, pend, 0.0), axis=-1, keepdims=True)
        next_e = jnp.sum(jnp.where((elane < N_EXPERTS) & (pend <= group_end), 1.0, 0.0), axis=-1, keepdims=True)
        tab = jnp.where(elane == 0, block_e, jnp.where(elane == 1, block_valid,
                                                       jnp.where(elane == 2, n_used,
                                                                 jnp.where(elane == 3, next_e, 0.0))))
        tab_ref[...] = tab.astype(I32)
        pstart_ref[...] = jnp.broadcast_to(pstart, pstart_ref.shape).astype(I32)


def _router(h, wr_pad, br_row, n_blocks):
    n, d = h.shape
    row = lambda i: (i, 0)
    col = lambda i: (0, i)
    fixed = lambda i: (0, 0)
    tab_rows = -(-n_blocks // SUBLANES) * SUBLANES
    return pl.pallas_call(
        _router_kernel,
        grid=(n // TM_TOK,),
        in_specs=[pl.BlockSpec((TM_TOK, d), row), pl.BlockSpec((d, LANES), fixed),
                  pl.BlockSpec((1, LANES), fixed)],
        out_specs=[pl.BlockSpec((SUBLANES, TM_TOK), col), pl.BlockSpec((TM_TOK, LANES), row),
                   pl.BlockSpec((SUBLANES, TM_TOK), col), pl.BlockSpec((SUBLANES, LANES), fixed),
                   pl.BlockSpec((tab_rows, LANES), fixed)],
        out_shape=[jax.ShapeDtypeStruct((SUBLANES, n), I32), jax.ShapeDtypeStruct((n, LANES), F32),
                   jax.ShapeDtypeStruct((SUBLANES, n), I32), jax.ShapeDtypeStruct((SUBLANES, LANES), I32),
                   jax.ShapeDtypeStruct((tab_rows, LANES), I32)],
        scratch_shapes=[pltpu.VMEM((SUBLANES, LANES), F32)],
        compiler_params=_cparams(("arbitrary",)),
        name="moe_router",
    )(h, wr_pad, br_row)


def _expert_kernel(be_ref, nu_ref, valid_ref, next_ref, xb_ref, wup_ref, bup_ref, wdn_ref, bdn_ref, yb_ref,
                   wup_f32_ref, wdn_f32_ref, wup_bf_ref, wdn_bf_ref, wsem, *, layer):
    b = pl.program_id(0)

    def fetch(e):
        return (pltpu.make_async_copy(wup_ref.at[layer, e], wup_f32_ref, wsem.at[0]),
                pltpu.make_async_copy(wdn_ref.at[layer, e], wdn_f32_ref, wsem.at[1]))

    @pl.when(b == 0)
    def _():
        for copy in fetch(be_ref[0]):
            copy.start()

    @pl.when(b < nu_ref[0])
    def _():
        @pl.when((b == 0) | (be_ref[b] != be_ref[jnp.maximum(b - 1, 0)]))
        def _():
            for copy in fetch(be_ref[b]):
                copy.wait()
            wup_bf_ref[...] = wup_f32_ref[...].astype(BF16)
            wdn_bf_ref[...] = wdn_f32_ref[...].astype(BF16)

            @pl.when(next_ref[b] < N_EXPERTS)
            def _():
                for copy in fetch(next_ref[b]):
                    copy.start()

        d_exp = wdn_f32_ref.shape[0]

        def run(n_rows):
            x = _load_token_tiles(xb_ref, (), n_rows)
            row = lax.broadcasted_iota(I32, (n_rows, 1), 0)
            x = jnp.where(row < valid_ref[b], x, 0.0)
            hu = jnp.dot(x.astype(BF16), wup_bf_ref[...], preferred_element_type=F32) + bup_ref[0, 0]
            glu = jnp.minimum(hu[:, :d_exp], SWIGLU_LIMIT)
            lin = jnp.clip(hu[:, d_exp:], -SWIGLU_LIMIT, SWIGLU_LIMIT)
            a = glu * _sigmoid(SWIGLU_ALPHA * glu) * (lin + 1.0)
            y = jnp.dot(a.astype(BF16), wdn_bf_ref[...], preferred_element_type=F32) + bdn_ref[0, 0]
            if n_rows < MOE_BLOCK:
                y = jnp.concatenate([y, jnp.zeros((MOE_BLOCK - n_rows, y.shape[1]), F32)], axis=0)
            _store_token_tiles(yb_ref, (), y)

        half = MOE_BLOCK // 2
        pl.when(valid_ref[b] > half)(functools.partial(run, MOE_BLOCK))
        pl.when(valid_ref[b] <= half)(functools.partial(run, half))

    @pl.when(b >= nu_ref[0])
    def _():
        yb_ref[...] = jnp.zeros_like(yb_ref)


def _experts(xb, block_e, n_used, block_valid, next_e, layer, w_up, b_up, w_dn, b_dn):
    d = SUBLANES * LANES
    rows = MOE_BLOCK * SUBLANES
    n_blocks = xb.shape[0] // rows
    d_up = w_up.shape[3]
    d_exp = w_dn.shape[2]
    bsel = lambda b, be, nu, va, ne: (layer, be[b], 0, 0)
    return pl.pallas_call(
        functools.partial(_expert_kernel, layer=layer),
        grid_spec=pltpu.PrefetchScalarGridSpec(
            num_scalar_prefetch=4,
            grid=(n_blocks,),
            in_specs=[pl.BlockSpec((rows, LANES), lambda b, be, nu, va, ne: (jnp.minimum(b, nu[0] - 1), 0)),
                      pl.BlockSpec(memory_space=pl.ANY),
                      pl.BlockSpec((1, 1, 1, d_up), bsel),
                      pl.BlockSpec(memory_space=pl.ANY),
                      pl.BlockSpec((1, 1, 1, d), bsel)],
            out_specs=pl.BlockSpec((rows, LANES), lambda b, be, nu, va, ne: (b, 0)),
            scratch_shapes=[pltpu.VMEM((d, d_up), F32), pltpu.VMEM((d_exp, d), F32),
                            pltpu.VMEM((d, d_up), BF16), pltpu.VMEM((d_exp, d), BF16),
                            pltpu.SemaphoreType.DMA((2,))]),
        out_shape=jax.ShapeDtypeStruct(xb.shape, F32),
        compiler_params=_cparams(("arbitrary",)),
        name="moe_experts",
    )(block_e, n_used, block_valid, next_e, xb, w_up, b_up, w_dn, b_dn)


def _sc_gather_tiles(table, idx):
    m = idx.shape[0]
    mesh = plsc.VectorSubcoreMesh(core_axis_name="core", subcore_axis_name="subcore")

    @functools.partial(pl.kernel, out_type=jax.ShapeDtypeStruct((m, SUBLANES, LANES), table.dtype), mesh=mesh)
    def gather(table_hbm, idx_hbm, out_hbm):
        def window(idx_vmem, out_vmem):
            pltpu.sync_copy(table_hbm.at[idx_vmem.at[0, pl.ds(0, SC_WINDOW)]], out_vmem)

        pltpu.emit_pipeline(
            window,
            grid=(m // SC_WINDOW,),
            in_specs=[pl.BlockSpec((1, LANES), lambda i: (i, 0))],
            out_specs=[pl.BlockSpec((SC_WINDOW, SUBLANES, LANES), lambda i: (i, 0, 0))],
            core_axis_name=("core", "subcore"),
            dimension_semantics=(pltpu.PARALLEL,),
        )(idx_hbm, out_hbm)

    idx_rows = jnp.pad(idx.reshape(m // SC_WINDOW, SC_WINDOW), ((0, 0), (0, LANES - SC_WINDOW)))
    return gather(table, idx_rows)


def _sc_scatter_tiles(tiles, idx_by_choice, n_out):
    n = tiles.shape[0]
    mesh = plsc.VectorSubcoreMesh(core_axis_name="core", subcore_axis_name="subcore")

    @functools.partial(pl.kernel, out_type=jax.ShapeDtypeStruct((n_out, SUBLANES, LANES), tiles.dtype),
                       mesh=mesh)
    def scatter(tiles_hbm, *refs):
        idx_hbm, out_hbm = refs[:TOP_K], refs[TOP_K]

        def window(tiles_vmem, *idx_vmem):
            for k in range(TOP_K):
                pltpu.sync_copy(tiles_vmem, out_hbm.at[idx_vmem[k].at[0, pl.ds(0, SC_WINDOW)]])

        pltpu.emit_pipeline(
            window,
            grid=(n // SC_WINDOW,),
            in_specs=[pl.BlockSpec((SC_WINDOW, SUBLANES, LANES), lambda i: (i, 0, 0))]
                     + [pl.BlockSpec((1, LANES), lambda i: (i, 0))] * TOP_K,
            out_specs=[],
            core_axis_name=("core", "subcore"),
            dimension_semantics=(pltpu.PARALLEL,),
        )(tiles_hbm, *idx_hbm)

    idx_rows = jnp.pad(idx_by_choice.reshape(TOP_K, n // SC_WINDOW, SC_WINDOW),
                       ((0, 0), (0, 0), (0, LANES - SC_WINDOW)))
    return scatter(tiles, *[idx_rows[k] for k in range(TOP_K)])


def _combine_kernel(h_ref, y0_ref, y1_ref, y2_ref, y3_ref, gate_ref, pp_ref, ps_ref, wpg_ref, wpp_ref,
                    g_ref, b_ref, *outs, n_ptiles):
    outp_ref, outs_ref = outs[0], outs[-1]
    tm = h_ref.shape[0]
    gate = gate_ref[...]
    moe = _load_token_tiles(y0_ref, (), tm) * gate[:, 0:1]
    for k, y_ref in enumerate((y1_ref, y2_ref, y3_ref), start=1):
        moe = moe + _load_token_tiles(y_ref, (), tm) * gate[:, k:k + 1]
    h2 = _layer_norm(DN_ALPHA * h_ref[...] + moe, g_ref[...], b_ref[...])
    embed_gate = _sigmoid(_dot(h2, wpg_ref[...]))

    def finish(p_ref, out_ref):
        out_ref[...] = h2 + embed_gate * _dot(p_ref[...], wpp_ref[...])

    is_prompt = pl.program_id(0) < n_ptiles
    pl.when(is_prompt)(functools.partial(finish, pp_ref, outp_ref))
    pl.when(jnp.logical_not(is_prompt))(functools.partial(finish, ps_ref, outs_ref))


def _combine(h, y, gate, p_prompt, p_sample, layer, n_p, split, wpg_bf16, wpp_bf16, g_row, b_row):
    n, d = h.shape
    e = p_prompt.shape[1]
    n_tiles = n // TM_TOK
    n_ptiles = n_p // TM_TOK
    n_stiles = (n - n_p) // TM_TOK
    row = lambda i: (i, 0)
    row_p = lambda i: (jnp.minimum(i, n_ptiles - 1), 0)
    row_s = lambda i: (jnp.maximum(i - n_ptiles, 0), 0)
    fixed = lambda i: (0, 0)
    choice = lambda k: pl.BlockSpec((TM_TOK * SUBLANES, LANES), lambda i: (k * n_tiles + i, 0))
    if split:
        out_specs = [pl.BlockSpec((TM_TOK, d), row_p), pl.BlockSpec((TM_TOK, d), row_s)]
        out_shape = [jax.ShapeDtypeStruct((n_p, d), F32), jax.ShapeDtypeStruct((n - n_p, d), F32)]
    else:
        out_specs = pl.BlockSpec((TM_TOK, d), row)
        out_shape = jax.ShapeDtypeStruct((n, d), F32)
    return pl.pallas_call(
        functools.partial(_combine_kernel, n_ptiles=n_ptiles),
        grid=(n_tiles,),
        in_specs=[pl.BlockSpec((TM_TOK, d), row)] + [choice(k) for k in range(TOP_K)]
                 + [pl.BlockSpec((TM_TOK, LANES), row),
                    pl.BlockSpec((TM_TOK, e), lambda i: (layer * n_ptiles + row_p(i)[0], 0)),
                    pl.BlockSpec((TM_TOK, e), lambda i: (layer * n_stiles + row_s(i)[0], 0)),
                    pl.BlockSpec((d, d), fixed), pl.BlockSpec((e, d), fixed), pl.BlockSpec((1, d), fixed),
                    pl.BlockSpec((1, d), fixed)],
        out_specs=out_specs,
        out_shape=out_shape,
        compiler_params=_cparams(("arbitrary",)),
        name="moe_combine_ln_embed",
    )(h, y, y, y, y, gate, p_prompt, p_sample, wpg_bf16, wpp_bf16, g_row, b_row)


def _layer_tail(h, ht, p_prompt, p_sample, layer, n_p, split, g2, b2, w_r, b_r, w_up, b_up, w_dn, b_dn,
                w_pg, w_pp):
    n, d = h.shape
    wr_pad = jnp.pad(w_r, ((0, 0), (0, LANES - N_EXPERTS)))
    br_row = jnp.pad(b_r, (0, LANES - N_EXPERTS))[None]
    n_asg = n * TOP_K
    n_blocks = n_asg // MOE_BLOCK + N_EXPERTS
    assert n_asg % MOE_BLOCK == 0 and n % SC_WINDOW == 0
    n_slots = n_blocks * MOE_BLOCK
    idx, gate, rank, pstart, tab = _router(h, wr_pad, br_row, n_blocks)
    experts = jnp.arange(N_EXPERTS, dtype=I32)[:, None, None]
    group_start = jnp.sum(jnp.where(idx[None, :TOP_K] == experts, pstart[0, :N_EXPERTS, None, None], 0), axis=0)
    slot_by_choice = group_start + rank[:TOP_K]
    xb = _sc_scatter_tiles(ht.reshape(n, SUBLANES, LANES), slot_by_choice, n_slots)
    yb = _experts(xb.reshape(n_slots * SUBLANES, LANES), tab[:n_blocks, 0], tab[0, 2:3], tab[:n_blocks, 1],
                  tab[:n_blocks, 3], layer, w_up, b_up[:, :, None, :], w_dn, b_dn[:, :, None, :])
    y = _sc_gather_tiles(yb.reshape(n_slots, SUBLANES, LANES), slot_by_choice.reshape(-1))
    return _combine(h, y.reshape(n_asg * SUBLANES, LANES), gate, p_prompt, p_sample, layer, n_p, split,
                    w_pg.astype(BF16), w_pp.astype(BF16), g2[None], b2[None])


def _lane_row(v, lane0):
    return jnp.zeros((1, LANES), F32).at[0, lane0:lane0 + v.shape[0]].set(v.astype(F32))


def kernel(x_prompt, x_sample, cache_fox_k, cache_fox_v, cache_fox_logf, state_gdn, state_gdn_conv,
           cache_pool, p_prompt, p_sample, w_in_ab, b_fgate, gdn_a_log, gdn_dt_bias, gdn_conv_w,
           gdn_norm_g, w_out_ab, pool_w, pool_scale, w_out_pool, ln1_g, ln1_b, ln2_g, ln2_b,
           w_router, b_router, w_expert_up, b_expert_up, w_expert_down, b_expert_down,
           w_ple_gate, w_ple_proj):
    n_pb, seq, d = x_prompt.shape
    n_sb, dseq, _ = x_sample.shape
    past = cache_fox_k.shape[2]
    assert n_pb == 1 and dseq == CHUNK and past % dseq == 0 and seq % TQ == 0
    assert d == SUBLANES * LANES
    n_p = n_pb * seq
    n_s = n_sb * dseq
    n = n_p + n_s
    assert n_p % TM_TOK == 0 and n_s % TM_TOK == 0
    n_layers = p_prompt.shape[0]
    pp_all = p_prompt.reshape(n_layers * n_p, -1)
    ps_all = p_sample.reshape(n_layers * n_s, -1)

    def tail(h, ht, i, split):
        return _layer_tail(h, ht, pp_all, ps_all, i, n_p, split, ln2_g[i], ln2_b[i], w_router[i], b_router[i],
                           w_expert_up, b_expert_up, w_expert_down, b_expert_down, w_ple_gate[i],
                           w_ple_proj[i])

    w_in = w_in_ab[0]
    n_small = 3 * N_HEADS
    ff0 = 3 * WIDTH
    gq0 = ff0 + N_HEADS
    ga0 = gq0 + 4 * WIDTH
    w_small = jnp.concatenate([w_in[:, ff0:gq0], w_in[:, ga0:ga0 + 2 * N_HEADS],
                               jnp.zeros((d, LANES - n_small), F32)], axis=1)
    w_all = jnp.concatenate([w_in[:, :ff0], w_in[:, gq0:ga0], w_small], axis=1)
    bf_row = _lane_row(b_fgate[0], LANE_F)
    up = _proj(x_prompt.reshape(n_p, d), w_all.astype(BF16), bf_row, TM_PROJ, U_COLS, False)
    us = _proj(x_sample.reshape(n_s, d), w_all, bf_row, TM_TOK, LANES, True)

    u3 = up[None]
    us3 = us.reshape(n_sb, dseq, U_COLS)
    cq_p, ck_p = _cumsum(u3, 1, n_p, CUMSUM_TILE, COL_S // LANES)
    lf_s = jnp.concatenate(
        [jnp.pad(cache_fox_logf[0].astype(F32), ((0, 0), (0, 0), (0, LANES - N_HEADS))),
         us3[:, :, COL_S:]], axis=1)
    cq_s, ck_s = _cumsum(lf_s, n_sb, past + dseq, past + dseq, 0)

    of_p = _fox(u3, lambda b, i: (0, i, COL_Q // WIDTH), u3, lambda b, j: (0, j, COL_K // WIDTH),
                u3, lambda b, j: (0, j, COL_V // WIDTH), cq_p, lambda b, i: (0, i, 0),
                ck_p, lambda b, j: (0, 0, j), 1, n_p // TQ, TQ, TK, 0)
    k_all = jnp.concatenate([cache_fox_k[0].reshape(n_sb, past, WIDTH), us3[:, :, COL_K:COL_K + WIDTH]], axis=1)
    v_all = jnp.concatenate([cache_fox_v[0].reshape(n_sb, past, WIDTH), us3[:, :, COL_V:COL_V + WIDTH]], axis=1)
    of_s = _fox(us3, lambda b, i: (b, 0, COL_Q // WIDTH), k_all, lambda b, j: (b, 0, 0),
                v_all, lambda b, j: (b, 0, 0), cq_s, lambda b, i: (b, past // dseq, 0),
                ck_s, lambda b, j: (b, 0, 0), n_sb, 1, dseq, past + dseq, past, hi=True)

    gw = 3 * WIDTH
    conv_w = jnp.pad(gdn_conv_w[0], ((0, SUBLANES - CONV_WIDTH), (0, 0)))
    gdn_args = (conv_w, _lane_row(gdn_a_log[0], LANE_A), _lane_row(gdn_dt_bias[0], LANE_A),
                gdn_norm_g[0][None])
    n_pstep = n_p // GDN_ROWS
    ends = lambda k: (jnp.asarray((np.arange(k) == 0).astype(np.int32)),
                      jnp.asarray((np.arange(k) == k - 1).astype(np.int32)))
    og_p, st_p = _gdn(up.reshape(n_pstep, GDN_ROWS, U_COLS), False, *ends(n_pstep),
                      jnp.zeros((n_pstep,), I32), jnp.zeros((1, SUBLANES, gw), F32), gdn_args[0],
                      jnp.zeros((1, N_HEADS, HEAD_DIM, HEAD_DIM), F32), *gdn_args[1:])
    conv_past = jnp.pad(state_gdn_conv[0].astype(F32), ((0, 0), (SUBLANES - (CONV_WIDTH - 1), 0), (0, 0)))
    ones = jnp.ones((n_sb,), I32)
    og_s, st_s = _gdn(us3, True, ones, ones, jnp.arange(n_sb, dtype=I32), conv_past, gdn_args[0],
                      state_gdn[0].astype(F32), *gdn_args[1:])
    h, ht = _outproj_ln((x_prompt.reshape(n_p, d), of_p.reshape(n_p, WIDTH), og_p.reshape(n_p, WIDTH)),
                        (x_sample.reshape(n_s, d), of_s.reshape(n_s, WIDTH), og_s.reshape(n_s, WIDTH)),
                        w_out_ab[0], ln1_g[0][None], ln1_b[0][None])
    x1 = tail(h, ht, 0, False)

    pool_args = (pool_w[0].astype(BF16), pool_scale[0][None], w_out_pool[0].astype(BF16),
                 ln1_g[1][None], ln1_b[1][None])
    ratio = TM_TOK // POOL_HALO
    x1_halo = x1.reshape(n // POOL_HALO, POOL_HALO, d)
    h_pool = _pool_ln(x1, lambda i: (i, 0), x1_halo, lambda i: (jnp.maximum(i * ratio - 1, 0), 0, 0),
                      n_p // TM_TOK, TM_TOK, 0, True, *pool_args)
    cache16 = jnp.pad(cache_pool[0].astype(F32), ((0, 0), (POOL_HALO - POOL_STATE, 0), (0, 0)))
    h, ht = _pool_ln(x1, lambda i: (n_p // dseq + i, 0), cache16, lambda i: (i, 0, 0),
                     n_sb, dseq, past, False, *pool_args, into=h_pool)
    x2_p, x2_s = tail(h, ht, 1, True)

    return (x2_p.reshape(n_pb, seq, d), x2_s.reshape(n_sb, dseq, d),
            up[:, COL_K:COL_K + WIDTH].reshape(1, n_pb, seq, N_HEADS, HEAD_DIM),
            up[:, COL_V:COL_V + WIDTH].reshape(1, n_pb, seq, N_HEADS, HEAD_DIM),
            up[:, COL_S:COL_S + N_HEADS].reshape(1, n_pb, seq, N_HEADS),
            st_p.reshape(1, n_pb, N_HEADS, HEAD_DIM, HEAD_DIM),
            up[seq - (CONV_WIDTH - 1):, COL_G:COL_G + gw].reshape(1, n_pb, CONV_WIDTH - 1, gw),
            x1[n_p - POOL_STATE:n_p].reshape(1, n_pb, POOL_STATE, d),
            us[:, COL_K:COL_K + WIDTH].reshape(1, n_sb, dseq, N_HEADS, HEAD_DIM),
            us[:, COL_V:COL_V + WIDTH].reshape(1, n_sb, dseq, N_HEADS, HEAD_DIM),
            us[:, COL_S:COL_S + N_HEADS].reshape(1, n_sb, dseq, N_HEADS),
            st_s.reshape(1, n_sb, N_HEADS, HEAD_DIM, HEAD_DIM),
            us[:, COL_G:COL_G + gw].reshape(n_sb, dseq, gw)[:, dseq - (CONV_WIDTH - 1):].reshape(
                1, n_sb, CONV_WIDTH - 1, gw),
            x1[n_p:].reshape(n_sb, dseq, d)[:, dseq - POOL_STATE:].reshape(1, n_sb, POOL_STATE, d))
```

```python
import functools

import numpy as np
import jax
import jax.numpy as jnp
from jax import lax
from jax.experimental import pallas as pl
from jax.experimental.pallas import tpu as pltpu
from jax.experimental.pallas import tpu_sc as plsc

F32 = jnp.float32
BF16 = jnp.bfloat16
I32 = jnp.int32
HIGHEST = lax.Precision.HIGHEST

LANES = 128
SUBLANES = 8
VMEM_LIMIT = 56 * 1024 * 1024

HEAD_DIM = 128
N_HEADS = 4
WIDTH = N_HEADS * HEAD_DIM
CHUNK = 64
CONV_WIDTH = 4
POOL_WINDOWS = (2, 4, 8, 16)
POOL_HALO = 16
POOL_STATE = 15
N_EXPERTS = 32
TOP_K = 4
SWIGLU_LIMIT = 7.0
SWIGLU_ALPHA = 1.702
DEPTH = 2
DN_ALPHA = (2 * DEPTH) ** 0.25
LN_EPS = 1e-5
NORM_EPS = 1e-6
NEG_INF = -1e30
LOG2E = 1.4426950408889634

COL_Q, COL_K, COL_V = 0, WIDTH, 2 * WIDTH
COL_G = 3 * WIDTH
COL_Z = 6 * WIDTH
COL_S = 7 * WIDTH
U_COLS = COL_S + LANES
LANE_F, LANE_A, LANE_B = 0, N_HEADS, 2 * N_HEADS

TM_PROJ = 512
TM_TOK = 512
SC_WINDOW = 32
MOE_BLOCK = 512
TQ = 1024
TK = 1024
CUMSUM_TILE = 512
GDN_ROWS = 256
GDN_CHUNK = CHUNK


def _cparams(sem):
    return pltpu.CompilerParams(dimension_semantics=sem, vmem_limit_bytes=VMEM_LIMIT)


def _softplus(x):
    return jnp.maximum(x, 0.0) + jnp.log1p(jnp.exp(-jnp.abs(x)))


def _sigmoid(x):
    return 1.0 / (1.0 + jnp.exp(-x))


def _silu(x):
    return x * _sigmoid(x)


def _layer_norm(y, g, b):
    mu = jnp.mean(y, axis=-1, keepdims=True)
    yc = y - mu
    var = jnp.mean(yc * yc, axis=-1, keepdims=True)
    return yc * lax.rsqrt(var + LN_EPS) * g + b


def _dot_general(a, b, dims, hi):
    if hi:
        return lax.dot_general(a.astype(F32), b.astype(F32), (dims, ((), ())), precision=HIGHEST,
                               preferred_element_type=F32)
    return lax.dot_general(a.astype(BF16), b.astype(BF16), (dims, ((), ())), preferred_element_type=F32)


def _dot(a, b, hi=False):
    return _dot_general(a, b, ((1,), (0,)), hi)


def _dot_nt(a, b, hi=False):
    return _dot_general(a, b, ((1,), (1,)), hi)


def _dot_tn(a, b, hi=False):
    return _dot_general(a, b, ((0,), (0,)), hi)


def _spread_lanes(x, width):
    if width % LANES == 0:
        return jnp.concatenate([x] * (width // LANES), axis=1)
    return jnp.broadcast_to(x[:, 0:1], (x.shape[0], width))


def _load_token_tiles(ref, lead, n_tok):
    return jnp.concatenate([ref[(*lead, pl.ds(j, n_tok, stride=SUBLANES), slice(None))]
                            for j in range(SUBLANES)], axis=1)


def _store_token_tiles(ref, lead, x):
    for j in range(SUBLANES):
        ref[(*lead, pl.ds(j, x.shape[0], stride=SUBLANES), slice(None))] = x[:, j * LANES:(j + 1) * LANES]


def _lanes_to_rows(x, lane0):
    r = lax.broadcasted_iota(I32, (SUBLANES, LANES), 0)
    c = lax.broadcasted_iota(I32, (SUBLANES, LANES), 1)
    sel = (c == r + lane0).astype(F32)
    return lax.dot_general(sel, x, (((1,), (1,)), ((), ())), precision=HIGHEST,
                           preferred_element_type=F32)


def _proj_kernel(x_ref, w_ref, bf_ref, u_ref, *, hi):
    u = _dot(x_ref[...], w_ref[...], hi)
    u_ref[...] = u

    @pl.when(pl.program_id(1) == pl.num_programs(1) - 1)
    def _():
        small = u[:, u.shape[1] - LANES:]
        lane = lax.broadcasted_iota(I32, small.shape, 1)
        logf = -_softplus(-(small + bf_ref[...]))
        u_ref[:, u.shape[1] - LANES:] = jnp.where(lane < LANE_A, logf, small)


def _proj(x, w, bf_row, tm, tn, hi):
    n, d = x.shape
    m = w.shape[1]
    return pl.pallas_call(
        functools.partial(_proj_kernel, hi=hi),
        grid=(n // tm, m // tn),
        in_specs=[pl.BlockSpec((tm, d), lambda i, j: (i, 0)),
                  pl.BlockSpec((d, tn), lambda i, j: (0, j)),
                  pl.BlockSpec((1, LANES), lambda i, j: (0, 0))],
        out_specs=pl.BlockSpec((tm, tn), lambda i, j: (i, j)),
        out_shape=jax.ShapeDtypeStruct((n, m), F32),
        compiler_params=_cparams(("parallel", "parallel")),
        name="in_proj",
    )(x, w, bf_row)


def _cumsum_kernel(lf_ref, crep_ref, crow_ref, carry_ref):
    @pl.when(pl.program_id(1) == 0)
    def _():
        carry_ref[...] = jnp.zeros_like(carry_ref)

    lf = lf_ref[0]
    t = lf.shape[0]
    r = lax.broadcasted_iota(I32, (t, t), 0)
    c = lax.broadcasted_iota(I32, (t, t), 1)
    tril = (c <= r).astype(F32)
    cs = jnp.dot(tril, lf, precision=HIGHEST, preferred_element_type=F32) + carry_ref[0:1, :]
    carry_ref[...] = jnp.broadcast_to(cs[t - 1:t, :], carry_ref.shape)
    c2 = cs * LOG2E
    crow_ref[0] = _lanes_to_rows(c2, LANE_F)
    for h in range(N_HEADS):
        crep_ref[0, :, h * HEAD_DIM:(h + 1) * HEAD_DIM] = jnp.broadcast_to(
            c2[:, LANE_F + h:LANE_F + h + 1], (t, HEAD_DIM))


def _cumsum(arr, n_batch, length, tl, col_block):
    return pl.pallas_call(
        _cumsum_kernel,
        grid=(n_batch, length // tl),
        in_specs=[pl.BlockSpec((1, tl, LANES), lambda b, j: (b, j, col_block))],
        out_specs=[pl.BlockSpec((1, tl, WIDTH), lambda b, j: (b, j, 0)),
                   pl.BlockSpec((1, SUBLANES, tl), lambda b, j: (b, 0, j))],
        out_shape=[jax.ShapeDtypeStruct((n_batch, length, WIDTH), F32),
                   jax.ShapeDtypeStruct((n_batch, SUBLANES, length), F32)],
        scratch_shapes=[pltpu.VMEM((SUBLANES, LANES), F32)],
        compiler_params=_cparams(("parallel", "arbitrary")),
        name="logf_cumsum",
    )(arr)


def _fox_kernel(qi_ref, kj_ref, last_ref, q_ref, k_ref, v_ref, cq_ref, ck_ref, o_ref,
                m_ref, l_ref, acc_ref, *, tq, tk, past, hi):
    s_idx = pl.program_id(1)
    qi = qi_ref[s_idx]
    kj = kj_ref[s_idx]

    @pl.when(kj == 0)
    def _():
        m_ref[...] = jnp.full_like(m_ref, NEG_INF)
        l_ref[...] = jnp.zeros_like(l_ref)
        acc_ref[...] = jnp.zeros_like(acc_ref)

    def update(masked):
        if masked:
            q_pos = past + qi * tq + lax.broadcasted_iota(I32, (tq, tk), 0)
            k_pos = kj * tk + lax.broadcasted_iota(I32, (tq, tk), 1)
            visible = k_pos <= q_pos
        for h in range(N_HEADS):
            cols = slice(h * HEAD_DIM, (h + 1) * HEAD_DIM)
            q = q_ref[0, :, cols] * (HEAD_DIM ** -0.5 * LOG2E)
            t = _dot_nt(q, k_ref[0, :, cols], hi) - ck_ref[0, h:h + 1, :]
            if masked:
                t = jnp.where(visible, t, NEG_INF)
            cq = cq_ref[0, :, cols]
            m_prev = m_ref[h]
            m_new = jnp.maximum(m_prev, jnp.max(t, axis=-1, keepdims=True) + cq)
            p = jnp.exp2(t - _spread_lanes(m_new - cq, tk))
            alpha = jnp.exp2(m_prev - m_new)
            l_ref[h] = alpha * l_ref[h] + jnp.sum(p, axis=-1, keepdims=True)
            acc_ref[:, cols] = alpha * acc_ref[:, cols] + _dot(p, v_ref[0, :, cols], hi)
            m_ref[h] = m_new

    crosses_diagonal = kj * tk + (tk - 1) > past + qi * tq
    pl.when(crosses_diagonal)(functools.partial(update, True))
    pl.when(jnp.logical_not(crosses_diagonal))(functools.partial(update, False))

    @pl.when(last_ref[s_idx] == 1)
    def _():
        for h in range(N_HEADS):
            cols = slice(h * HEAD_DIM, (h + 1) * HEAD_DIM)
            o_ref[0, :, cols] = acc_ref[:, cols] / l_ref[h]


def _fox_schedule(n_q, tq, tk, past):
    qi, kj, last = [], [], []
    for i in range(n_q):
        hi = (past + (i + 1) * tq - 1) // tk
        for j in range(hi + 1):
            qi.append(i)
            kj.append(j)
            last.append(1 if j == hi else 0)
    return (jnp.asarray(np.array(qi, np.int32)), jnp.asarray(np.array(kj, np.int32)),
            jnp.asarray(np.array(last, np.int32)))


def _fox(q_arr, q_map, k_arr, k_map, v_arr, v_map, cq_arr, cq_map, ck_arr, ck_map,
         n_batch, n_q, tq, tk, past, hi=False):
    qi, kj, last = _fox_schedule(n_q, tq, tk, past)
    n_steps = int(qi.shape[0])
    spec = lambda shape, fn, tab: pl.BlockSpec(shape, lambda b, s, qi_r, kj_r, la_r: fn(b, (qi_r if tab == 'q' else kj_r)[s]))
    return pl.pallas_call(
        functools.partial(_fox_kernel, tq=tq, tk=tk, past=past, hi=hi),
        grid_spec=pltpu.PrefetchScalarGridSpec(
            num_scalar_prefetch=3,
            grid=(n_batch, n_steps),
            in_specs=[spec((1, tq, WIDTH), q_map, 'q'),
                      spec((1, tk, WIDTH), k_map, 'k'),
                      spec((1, tk, WIDTH), v_map, 'k'),
                      spec((1, tq, WIDTH), cq_map, 'q'),
                      spec((1, SUBLANES, tk), ck_map, 'k')],
            out_specs=spec((1, tq, WIDTH), lambda b, i: (b, i, 0), 'q'),
            scratch_shapes=[pltpu.VMEM((N_HEADS, tq, HEAD_DIM), F32),
                            pltpu.VMEM((N_HEADS, tq, HEAD_DIM), F32),
                            pltpu.VMEM((tq, WIDTH), F32)]),
        out_shape=jax.ShapeDtypeStruct((n_batch, n_q * tq, WIDTH), F32),
        compiler_params=_cparams(("parallel", "arbitrary")),
        name="fox_attention",
    )(qi, kj, last, q_arr, k_arr, v_arr, cq_arr, ck_arr)


def _gdn_kernel(first_ref, last_ref, seq_ref,
                pre_ref, z_ref, sm_ref, cpast_ref, convw_ref, s0_ref, alog_ref, dtb_ref, ng_ref,
                o_ref, sout_ref, stage_ref, s_ref, *, rows, chunk, hi):
    dot, dot_nt, dot_tn = (functools.partial(f, hi=hi) for f in (_dot, _dot_nt, _dot_tn))
    step = pl.program_id(0)
    halo = SUBLANES
    n_chunks = rows // chunk

    @pl.when(first_ref[step] == 1)
    def _():
        stage_ref[0:halo, :] = cpast_ref[0]
        s_ref[...] = s0_ref[0]

    stage_ref[halo:halo + rows, :] = pre_ref[0]
    conv = stage_ref[halo:halo + rows, :] * convw_ref[CONV_WIDTH - 1:CONV_WIDTH, :]
    for j in range(1, CONV_WIDTH):
        conv = conv + (stage_ref[halo - j:halo - j + rows, :]
                       * convw_ref[CONV_WIDTH - 1 - j:CONV_WIDTH - j, :])
    stage_ref[0:halo, :] = stage_ref[rows:rows + halo, :]
    act = _silu(conv)

    small = sm_ref[0]
    beta_all = _sigmoid(small)
    g_all = -jnp.exp(alog_ref[...]) * _softplus(small + dtb_ref[...])
    r = lax.broadcasted_iota(I32, (rows, rows), 0)
    c = lax.broadcasted_iota(I32, (rows, rows), 1)
    same_chunk = (r // chunk) == (c // chunk)
    incl = same_chunk & (c <= r)
    strict = same_chunk & (c < r)
    eye = (c == r).astype(F32)
    gc_all = jnp.dot(incl.astype(F32), g_all, precision=HIGHEST, preferred_element_type=F32)
    gc_rows = _lanes_to_rows(gc_all, LANE_A)

    heads = range(N_HEADS)
    head_cols = [slice(h * HEAD_DIM, (h + 1) * HEAD_DIM) for h in heads]
    q, k, gc, decay, kb, vb, low = [], [], [], [], [], [], []
    for h in heads:
        qh = act[:, h * HEAD_DIM:(h + 1) * HEAD_DIM]
        kh = act[:, WIDTH + h * HEAD_DIM:WIDTH + (h + 1) * HEAD_DIM]
        vh = act[:, 2 * WIDTH + h * HEAD_DIM:2 * WIDTH + (h + 1) * HEAD_DIM]
        q.append(qh * lax.rsqrt(jnp.sum(qh * qh, axis=-1, keepdims=True) + NORM_EPS) * (HEAD_DIM ** -0.5))
        k.append(kh * lax.rsqrt(jnp.sum(kh * kh, axis=-1, keepdims=True) + NORM_EPS))
        beta = beta_all[:, LANE_B + h:LANE_B + h + 1]
        gc.append(gc_all[:, LANE_A + h:LANE_A + h + 1])
        diff = gc[h] - gc_rows[h:h + 1, :]
        decay.append(jnp.where(incl, jnp.exp(jnp.where(incl, diff, 0.0)), 0.0))
        kb.append(k[h] * beta)
        vb.append(vh * beta)
    for h in heads:
        low.append(jnp.where(strict, dot_nt(kb[h], k[h]) * decay[h], 0.0))
    inv = [eye - low[h] for h in heads]
    pw = [dot(low[h], low[h]) for h in heads]
    n_sq = chunk.bit_length() - 2
    for it in range(n_sq):
        inv = [inv[h] + dot(inv[h], pw[h]) for h in heads]
        if it + 1 < n_sq:
            pw = [dot(pw[h], pw[h]) for h in heads]
    egc = [jnp.exp(gc[h]) for h in heads]
    uw = [dot(inv[h], jnp.concatenate([vb[h], kb[h] * egc[h]], axis=1)) for h in heads]
    intra = [jnp.where(incl, dot_nt(q[h], k[h]) * decay[h], 0.0) for h in heads]
    qd = [q[h] * egc[h] for h in heads]
    g_last = [[gc[h][(g + 1) * chunk - 1:(g + 1) * chunk, :] for g in range(n_chunks)] for h in heads]
    kd = [k[h] * jnp.exp(jnp.concatenate([jnp.broadcast_to(gl, (chunk, 1)) for gl in g_last[h]], axis=0)
                         - gc[h]) for h in heads]
    state = [s_ref[h] for h in heads]
    v_new = [[] for _ in heads]
    for g in range(n_chunks):
        rs = slice(g * chunk, (g + 1) * chunk)
        for h in heads:
            v_new[h].append(uw[h][rs, :HEAD_DIM] - dot(uw[h][rs, HEAD_DIM:], state[h]))
        for h in heads:
            v_rows = jnp.concatenate(
                v_new[h] + [jnp.zeros((rows - (g + 1) * chunk, HEAD_DIM), F32)] * (g + 1 < n_chunks), axis=0)
            o = dot(qd[h][rs, :], state[h]) + dot(intra[h][rs, :], v_rows)
            state[h] = state[h] * jnp.exp(g_last[h][g]) + dot_tn(kd[h][rs, :], v_new[h][g])
            o = (o * lax.rsqrt(jnp.mean(o * o, axis=-1, keepdims=True) + NORM_EPS)
                 * ng_ref[...] * _silu(z_ref[0, rs, head_cols[h]]))
            o_ref[0, rs, head_cols[h]] = o
    for h in heads:
        s_ref[h] = state[h]

    @pl.when(last_ref[step] == 1)
    def _():
        sout_ref[0] = s_ref[...]


def _gdn(u_view, hi, first, last, seq, conv_past, conv_w, s0, alog_row, dtb_row, ng_row):
    n_steps, rows, _ = u_view.shape
    n_seq = s0.shape[0]
    gw = 3 * WIDTH
    return pl.pallas_call(
        functools.partial(_gdn_kernel, rows=rows, chunk=min(rows, GDN_CHUNK), hi=hi),
        grid_spec=pltpu.PrefetchScalarGridSpec(
            num_scalar_prefetch=3,
            grid=(n_steps,),
            in_specs=[pl.BlockSpec((1, rows, gw), lambda s, f, l, q: (s, 0, COL_G // gw)),
                      pl.BlockSpec((1, rows, WIDTH), lambda s, f, l, q: (s, 0, COL_Z // WIDTH)),
                      pl.BlockSpec((1, rows, LANES), lambda s, f, l, q: (s, 0, COL_S // LANES)),
                      pl.BlockSpec((1, SUBLANES, gw), lambda s, f, l, q: (q[s], 0, 0)),
                      pl.BlockSpec((SUBLANES, gw), lambda s, f, l, q: (0, 0)),
                      pl.BlockSpec((1, N_HEADS, HEAD_DIM, HEAD_DIM), lambda s, f, l, q: (q[s], 0, 0, 0)),
                      pl.BlockSpec((1, LANES), lambda s, f, l, q: (0, 0)),
                      pl.BlockSpec((1, LANES), lambda s, f, l, q: (0, 0)),
                      pl.BlockSpec((1, LANES), lambda s, f, l, q: (0, 0))],
            out_specs=[pl.BlockSpec((1, rows, WIDTH), lambda s, f, l, q: (s, 0, 0)),
                       pl.BlockSpec((1, N_HEADS, HEAD_DIM, HEAD_DIM), lambda s, f, l, q: (q[s], 0, 0, 0))],
            scratch_shapes=[pltpu.VMEM((rows + SUBLANES, gw), F32),
                            pltpu.VMEM((N_HEADS, HEAD_DIM, HEAD_DIM), F32)]),
        out_shape=[jax.ShapeDtypeStruct((n_steps, rows, WIDTH), F32),
                   jax.ShapeDtypeStruct((n_seq, N_HEADS, HEAD_DIM, HEAD_DIM), F32)],
        compiler_params=_cparams(("arbitrary",)),
        name="gated_deltanet",
    )(first, last, seq, u_view, u_view, u_view, conv_past, conv_w, s0, alog_row, dtb_row, ng_row)


def _outproj_ln_kernel(xp_ref, ofp_ref, ogp_ref, xs_ref, ofs_ref, ogs_ref, w_ref, wf_ref, g_ref, b_ref,
                       h_ref, ht_ref, *, hi_from):
    def run(hi, x_ref, of_ref, og_ref):
        w = wf_ref if hi else w_ref
        mix = _dot(of_ref[...], w[0:WIDTH, :], hi) + _dot(og_ref[...], w[WIDTH:2 * WIDTH, :], hi)
        h = _layer_norm(DN_ALPHA * x_ref[...] + mix, g_ref[...], b_ref[...])
        h_ref[...] = h
        _store_token_tiles(ht_ref, (), h)

    pl.when(pl.program_id(0) < hi_from)(functools.partial(run, False, xp_ref, ofp_ref, ogp_ref))
    pl.when(pl.program_id(0) >= hi_from)(functools.partial(run, True, xs_ref, ofs_ref, ogs_ref))


def _outproj_ln(prompt, sample, w, g_row, b_row):
    n_p, d = prompt[0].shape
    n = n_p + sample[0].shape[0]
    hi_from = n_p // TM_TOK
    row = lambda i: (i, 0)
    row_p = lambda i: (jnp.minimum(i, hi_from - 1), 0)
    row_s = lambda i: (jnp.maximum(i - hi_from, 0), 0)
    fixed = lambda i: (0, 0)
    group = lambda rows: [pl.BlockSpec((TM_TOK, d), rows), pl.BlockSpec((TM_TOK, WIDTH), rows),
                          pl.BlockSpec((TM_TOK, WIDTH), rows)]
    return pl.pallas_call(
        functools.partial(_outproj_ln_kernel, hi_from=hi_from),
        grid=(n // TM_TOK,),
        in_specs=group(row_p) + group(row_s) + [pl.BlockSpec((2 * WIDTH, d), fixed),
                                                pl.BlockSpec((2 * WIDTH, d), fixed),
                                                pl.BlockSpec((1, d), fixed), pl.BlockSpec((1, d), fixed)],
        out_specs=[pl.BlockSpec((TM_TOK, d), row), pl.BlockSpec((TM_TOK * SUBLANES, LANES), row)],
        out_shape=[jax.ShapeDtypeStruct((n, d), F32), jax.ShapeDtypeStruct((n * SUBLANES, LANES), F32)],
        compiler_params=_cparams(("parallel",)),
        name="out_proj_ln",
    )(*prompt, *sample, w.astype(BF16), w, g_row, b_row)


def _pool_ln_kernel(x_ref, halo_ref, pw_ref, ps_ref, w_ref, g_ref, b_ref, h_ref, ht_ref, stage_ref,
                    lvl_a_ref, lvl_b_ref, *, tm, pos0, zero_first_halo):
    i = pl.program_id(0)
    pad = SUBLANES
    n_rows = POOL_HALO + tm
    for ref in (stage_ref, lvl_a_ref, lvl_b_ref):
        ref[0:pad, :] = jnp.zeros((pad, ref.shape[1]), F32)
    stage_ref[pad:pad + POOL_HALO, :] = halo_ref[0]
    if zero_first_halo:
        @pl.when(i == 0)
        def _():
            stage_ref[pad:pad + POOL_HALO, :] = jnp.zeros((POOL_HALO, stage_ref.shape[1]), F32)
    x = x_ref[...]
    stage_ref[pad + POOL_HALO:pad + n_rows, :] = x
    gdim = x.shape[1] // len(POOL_WINDOWS)
    pos = pos0 + lax.broadcasted_iota(I32, (tm, 1), 0)
    if zero_first_halo:
        pos = pos + i * tm

    def window_sum(cols, win):
        assert win & (win - 1) == 0 and win <= POOL_HALO
        src, src_cols, span, level = stage_ref, cols, 1, 0
        while span < win:
            dst = (lvl_a_ref, lvl_b_ref)[level % 2]
            dst[pad:pad + n_rows, :] = (src[pad:pad + n_rows, src_cols]
                                        + src[pad - span:pad - span + n_rows, src_cols])
            src, src_cols, span, level = dst, slice(None), 2 * span, level + 1
        return src[pad + POOL_HALO:pad + n_rows, src_cols]

    parts = []
    for gi, win in enumerate(POOL_WINDOWS):
        cols = slice(gi * gdim, (gi + 1) * gdim)
        s = window_sum(cols, win)
        cnt = jnp.minimum(pos + 1, win).astype(F32)
        zg = s / cnt - x[:, cols]
        parts.append(_dot(zg, pw_ref[gi]))
    zg = jnp.concatenate(parts, axis=-1) * ps_ref[...]
    mix = _dot(zg, w_ref[...])
    h = _layer_norm(DN_ALPHA * x + mix, g_ref[...], b_ref[...])
    h_ref[...] = h
    _store_token_tiles(ht_ref, (), h)


def _pool_ln_into_kernel(*refs, **kw):
    _pool_ln_kernel(*refs[:7], *refs[9:], **kw)


def _pool_ln(x, x_map, halo_arr, halo_map, n_tiles, tm, pos0, zero_first_halo,
             pw_bf16, ps_row, w_bf16, g_row, b_row, into=None):
    n, d = x.shape
    gdim = d // len(POOL_WINDOWS)
    fixed = lambda i: (0, 0)
    in_specs = [pl.BlockSpec((tm, d), x_map),
                pl.BlockSpec((1, POOL_HALO, d), halo_map),
                pl.BlockSpec((len(POOL_WINDOWS), gdim, gdim), lambda i: (0, 0, 0)),
                pl.BlockSpec((1, d), fixed), pl.BlockSpec((d, d), fixed),
                pl.BlockSpec((1, d), fixed), pl.BlockSpec((1, d), fixed)]
    args = (x, halo_arr, pw_bf16, ps_row, w_bf16, g_row, b_row)
    kw = dict(tm=tm, pos0=pos0, zero_first_halo=zero_first_halo)
    return pl.pallas_call(
        functools.partial(_pool_ln_kernel if into is None else _pool_ln_into_kernel, **kw),
        grid=(n_tiles,),
        in_specs=in_specs + ([] if into is None else [pl.BlockSpec(memory_space=pl.ANY)] * 2),
        out_specs=[pl.BlockSpec((tm, d), x_map),
                   pl.BlockSpec((tm * SUBLANES, LANES), x_map)],
        out_shape=[jax.ShapeDtypeStruct((n, d), F32), jax.ShapeDtypeStruct((n * SUBLANES, LANES), F32)],
        input_output_aliases={} if into is None else {7: 0, 8: 1},
        scratch_shapes=[pltpu.VMEM((SUBLANES + POOL_HALO + tm, d), F32),
                        pltpu.VMEM((SUBLANES + POOL_HALO + tm, gdim), F32),
                        pltpu.VMEM((SUBLANES + POOL_HALO + tm, gdim), F32)],
        compiler_params=_cparams(("arbitrary",)),
        name="pool_mixer_ln",
    )(*args, *(() if into is None else into))


def _router_kernel(h_ref, wr_ref, br_ref, idx_ref, gate_ref, rank_ref, pstart_ref, tab_ref, carry_ref):
    @pl.when(pl.program_id(0) == 0)
    def _():
        carry_ref[...] = jnp.zeros_like(carry_ref)

    tm = h_ref.shape[0]
    lane = lax.broadcasted_iota(I32, (tm, LANES), 1).astype(F32)
    logits = jnp.dot(h_ref[...], wr_ref[...], precision=HIGHEST, preferred_element_type=F32)
    work = jnp.where(lane < N_EXPERTS, logits + br_ref[...], -jnp.inf)
    vals, ids = [], []
    for _ in range(TOP_K):
        m = jnp.max(work, axis=-1, keepdims=True)
        ik = jnp.min(jnp.where(work == m, lane, float(LANES)), axis=-1, keepdims=True)
        vals.append(m)
        ids.append(ik)
        work = jnp.where(lane == ik, -jnp.inf, work)
    exps = [jnp.exp(v - vals[0]) for v in vals]
    denom = exps[0]
    for e in exps[1:]:
        denom = denom + e
    multihot = jnp.zeros((tm, LANES), F32)
    idx_out = jnp.zeros((tm, LANES), F32)
    gate_out = jnp.zeros((tm, LANES), F32)
    for k in range(TOP_K):
        multihot = multihot + (lane == ids[k]).astype(F32)
        idx_out = jnp.where(lane == k, ids[k], idx_out)
        gate_out = jnp.where(lane == k, exps[k] / denom, gate_out)
    r = lax.broadcasted_iota(I32, (tm, tm), 0)
    c = lax.broadcasted_iota(I32, (tm, tm), 1)
    before = _dot((c < r).astype(F32), multihot) + carry_ref[0:1, :]
    rank_out = jnp.zeros((tm, LANES), F32)
    for k in range(TOP_K):
        rk = jnp.sum(jnp.where(lane == ids[k], before, 0.0), axis=-1, keepdims=True)
        rank_out = jnp.where(lane == k, rk, rank_out)
    idx_ref[...] = jnp.transpose(idx_out)[:SUBLANES].astype(I32)
    rank_ref[...] = jnp.transpose(rank_out)[:SUBLANES].astype(I32)
    gate_ref[...] = gate_out
    total = carry_ref[0:1, :] + jnp.sum(multihot, axis=0, keepdims=True)
    carry_ref[...] = jnp.broadcast_to(total, carry_ref.shape)

    @pl.when(pl.program_id(0) == pl.num_programs(0) - 1)
    def _():
        n_rows = tab_ref.shape[0]
        padded = jnp.floor((total + (MOE_BLOCK - 1)) / MOE_BLOCK) * MOE_BLOCK
        rr = lax.broadcasted_iota(I32, (LANES, LANES), 0)
        cc = lax.broadcasted_iota(I32, (LANES, LANES), 1)
        pend = jnp.dot(jnp.broadcast_to(padded, (SUBLANES, LANES)), (rr <= cc).astype(F32),
                       precision=HIGHEST, preferred_element_type=F32)[0:1, :]
        pstart = pend - padded
        n_used = jnp.max(pend, axis=-1, keepdims=True) / MOE_BLOCK
        elane = lax.broadcasted_iota(I32, (n_rows, LANES), 1)
        blk = jnp.minimum(lax.broadcasted_iota(I32, (n_rows, 1), 0).astype(F32), n_used - 1.0) * MOE_BLOCK
        ends_before = jnp.where((elane < N_EXPERTS) & (pend <= blk), 1.0, 0.0)
        block_e = jnp.minimum(jnp.sum(ends_before, axis=-1, keepdims=True), N_EXPERTS - 1.0)
        mine = elane.astype(F32) == block_e
        filled = jnp.sum(jnp.where(mine, total - (blk - pstart), 0.0), axis=-1, keepdims=True)
        block_valid = jnp.clip(filled, 0.0, float(MOE_BLOCK))
        group_end = jnp.sum(jnp.where(mine, pend, 0.0), axis=-1, keepdims=True)
        next_e = jnp.sum(jnp.where((elane < N_EXPERTS) & (pend <= group_end), 1.0, 0.0), axis=-1, keepdims=True)
        tab = jnp.where(elane == 0, block_e, jnp.where(elane == 1, block_valid,
                                                       jnp.where(elane == 2, n_used,
                                                                 jnp.where(elane == 3, next_e, 0.0))))
        tab_ref[...] = tab.astype(I32)
        pstart_ref[...] = jnp.broadcast_to(pstart, pstart_ref.shape).astype(I32)


def _router(h, wr_pad, br_row, n_blocks):
    n, d = h.shape
    row = lambda i: (i, 0)
    col = lambda i: (0, i)
    fixed = lambda i: (0, 0)
    tab_rows = -(-n_blocks // SUBLANES) * SUBLANES
    return pl.pallas_call(
        _router_kernel,
        grid=(n // TM_TOK,),
        in_specs=[pl.BlockSpec((TM_TOK, d), row), pl.BlockSpec((d, LANES), fixed),
                  pl.BlockSpec((1, LANES), fixed)],
        out_specs=[pl.BlockSpec((SUBLANES, TM_TOK), col), pl.BlockSpec((TM_TOK, LANES), row),
                   pl.BlockSpec((SUBLANES, TM_TOK), col), pl.BlockSpec((SUBLANES, LANES), fixed),
                   pl.BlockSpec((tab_rows, LANES), fixed)],
        out_shape=[jax.ShapeDtypeStruct((SUBLANES, n), I32), jax.ShapeDtypeStruct((n, LANES), F32),
                   jax.ShapeDtypeStruct((SUBLANES, n), I32), jax.ShapeDtypeStruct((SUBLANES, LANES), I32),
                   jax.ShapeDtypeStruct((tab_rows, LANES), I32)],
        scratch_shapes=[pltpu.VMEM((SUBLANES, LANES), F32)],
        compiler_params=_cparams(("arbitrary",)),
        name="moe_router",
    )(h, wr_pad, br_row)


def _expert_kernel(be_ref, nu_ref, valid_ref, next_ref, xb_ref, wup_ref, bup_ref, wdn_ref, bdn_ref, yb_ref,
                   wup_f32_ref, wdn_f32_ref, wup_bf_ref, wdn_bf_ref, wsem, *, layer):
    b = pl.program_id(0)

    def fetch(e):
        return (pltpu.make_async_copy(wup_ref.at[layer, e], wup_f32_ref, wsem.at[0]),
                pltpu.make_async_copy(wdn_ref.at[layer, e], wdn_f32_ref, wsem.at[1]))

    @pl.when(b == 0)
    def _():
        for copy in fetch(be_ref[0]):
            copy.start()

    @pl.when(b < nu_ref[0])
    def _():
        @pl.when((b == 0) | (be_ref[b] != be_ref[jnp.maximum(b - 1, 0)]))
        def _():
            for copy in fetch(be_ref[b]):
                copy.wait()
            wup_bf_ref[...] = wup_f32_ref[...].astype(BF16)
            wdn_bf_ref[...] = wdn_f32_ref[...].astype(BF16)

            @pl.when(next_ref[b] < N_EXPERTS)
            def _():
                for copy in fetch(next_ref[b]):
                    copy.start()

        d_exp = wdn_f32_ref.shape[0]

        def run(n_rows):
            x = _load_token_tiles(xb_ref, (), n_rows)
            row = lax.broadcasted_iota(I32, (n_rows, 1), 0)
            x = jnp.where(row < valid_ref[b], x, 0.0)
            hu = jnp.dot(x.astype(BF16), wup_bf_ref[...], preferred_element_type=F32) + bup_ref[0, 0]
            glu = jnp.minimum(hu[:, :d_exp], SWIGLU_LIMIT)
            lin = jnp.clip(hu[:, d_exp:], -SWIGLU_LIMIT, SWIGLU_LIMIT)
            a = glu * _sigmoid(SWIGLU_ALPHA * glu) * (lin + 1.0)
            y = jnp.dot(a.astype(BF16), wdn_bf_ref[...], preferred_element_type=F32) + bdn_ref[0, 0]
            if n_rows < MOE_BLOCK:
                y = jnp.concatenate([y, jnp.zeros((MOE_BLOCK - n_rows, y.shape[1]), F32)], axis=0)
            _store_token_tiles(yb_ref, (), y)

        half = MOE_BLOCK // 2
        pl.when(valid_ref[b] > half)(functools.partial(run, MOE_BLOCK))
        pl.when(valid_ref[b] <= half)(functools.partial(run, half))

    @pl.when(b >= nu_ref[0])
    def _():
        yb_ref[...] = jnp.zeros_like(yb_ref)


def _experts(xb, block_e, n_used, block_valid, next_e, layer, w_up, b_up, w_dn, b_dn):
    d = SUBLANES * LANES
    rows = MOE_BLOCK * SUBLANES
    n_blocks = xb.shape[0] // rows
    d_up = w_up.shape[3]
    d_exp = w_dn.shape[2]
    bsel = lambda b, be, nu, va, ne: (layer, be[b], 0, 0)
    return pl.pallas_call(
        functools.partial(_expert_kernel, layer=layer),
        grid_spec=pltpu.PrefetchScalarGridSpec(
            num_scalar_prefetch=4,
            grid=(n_blocks,),
            in_specs=[pl.BlockSpec((rows, LANES), lambda b, be, nu, va, ne: (jnp.minimum(b, nu[0] - 1), 0)),
                      pl.BlockSpec(memory_space=pl.ANY),
                      pl.BlockSpec((1, 1, 1, d_up), bsel),
                      pl.BlockSpec(memory_space=pl.ANY),
                      pl.BlockSpec((1, 1, 1, d), bsel)],
            out_specs=pl.BlockSpec((rows, LANES), lambda b, be, nu, va, ne: (b, 0)),
            scratch_shapes=[pltpu.VMEM((d, d_up), F32), pltpu.VMEM((d_exp, d), F32),
                            pltpu.VMEM((d, d_up), BF16), pltpu.VMEM((d_exp, d), BF16),
                            pltpu.SemaphoreType.DMA((2,))]),
        out_shape=jax.ShapeDtypeStruct(xb.shape, F32),
        compiler_params=_cparams(("arbitrary",)),
        name="moe_experts",
    )(block_e, n_used, block_valid, next_e, xb, w_up, b_up, w_dn, b_dn)


def _sc_gather_tiles(table, idx):
    m = idx.shape[0]
    mesh = plsc.VectorSubcoreMesh(core_axis_name="core", subcore_axis_name="subcore")

    @functools.partial(pl.kernel, out_type=jax.ShapeDtypeStruct((m, SUBLANES, LANES), table.dtype), mesh=mesh)
    def gather(table_hbm, idx_hbm, out_hbm):
        def window(idx_vmem, out_vmem):
            pltpu.sync_copy(table_hbm.at[idx_vmem.at[0, pl.ds(0, SC_WINDOW)]], out_vmem)

        pltpu.emit_pipeline(
            window,
            grid=(m // SC_WINDOW,),
            in_specs=[pl.BlockSpec((1, LANES), lambda i: (i, 0))],
            out_specs=[pl.BlockSpec((SC_WINDOW, SUBLANES, LANES), lambda i: (i, 0, 0))],
            core_axis_name=("core", "subcore"),
            dimension_semantics=(pltpu.PARALLEL,),
        )(idx_hbm, out_hbm)

    idx_rows = jnp.pad(idx.reshape(m // SC_WINDOW, SC_WINDOW), ((0, 0), (0, LANES - SC_WINDOW)))
    return gather(table, idx_rows)


def _sc_scatter_tiles(tiles, idx_by_choice, n_out):
    n = tiles.shape[0]
    mesh = plsc.VectorSubcoreMesh(core_axis_name="core", subcore_axis_name="subcore")

    @functools.partial(pl.kernel, out_type=jax.ShapeDtypeStruct((n_out, SUBLANES, LANES), tiles.dtype),
                       mesh=mesh)
    def scatter(tiles_hbm, *refs):
        idx_hbm, out_hbm = refs[:TOP_K], refs[TOP_K]

        def window(tiles_vmem, *idx_vmem):
            for k in range(TOP_K):
                pltpu.sync_copy(tiles_vmem, out_hbm.at[idx_vmem[k].at[0, pl.ds(0, SC_WINDOW)]])

        pltpu.emit_pipeline(
            window,
            grid=(n // SC_WINDOW,),
            in_specs=[pl.BlockSpec((SC_WINDOW, SUBLANES, LANES), lambda i: (i, 0, 0))]
                     + [pl.BlockSpec((1, LANES), lambda i: (i, 0))] * TOP_K,
            out_specs=[],
            core_axis_name=("core", "subcore"),
            dimension_semantics=(pltpu.PARALLEL,),
        )(tiles_hbm, *idx_hbm)

    idx_rows = jnp.pad(idx_by_choice.reshape(TOP_K, n // SC_WINDOW, SC_WINDOW),
                       ((0, 0), (0, 0), (0, LANES - SC_WINDOW)))
    return scatter(tiles, *[idx_rows[k] for k in range(TOP_K)])


def _combine_kernel(h_ref, y0_ref, y1_ref, y2_ref, y3_ref, gate_ref, pp_ref, ps_ref, wpg_ref, wpp_ref,
                    g_ref, b_ref, *outs, n_ptiles):
    outp_ref, outs_ref = outs[0], outs[-1]
    tm = h_ref.shape[0]
    gate = gate_ref[...]
    moe = _load_token_tiles(y0_ref, (), tm) * gate[:, 0:1]
    for k, y_ref in enumerate((y1_ref, y2_ref, y3_ref), start=1):
        moe = moe + _load_token_tiles(y_ref, (), tm) * gate[:, k:k + 1]
    h2 = _layer_norm(DN_ALPHA * h_ref[...] + moe, g_ref[...], b_ref[...])
    embed_gate = _sigmoid(_dot(h2, wpg_ref[...]))

    def finish(p_ref, out_ref):
        out_ref[...] = h2 + embed_gate * _dot(p_ref[...], wpp_ref[...])

    is_prompt = pl.program_id(0) < n_ptiles
    pl.when(is_prompt)(functools.partial(finish, pp_ref, outp_ref))
    pl.when(jnp.logical_not(is_prompt))(functools.partial(finish, ps_ref, outs_ref))


def _combine(h, y, gate, p_prompt, p_sample, layer, n_p, split, wpg_bf16, wpp_bf16, g_row, b_row):
    n, d = h.shape
    e = p_prompt.shape[1]
    n_tiles = n // TM_TOK
    n_ptiles = n_p // TM_TOK
    n_stiles = (n - n_p) // TM_TOK
    row = lambda i: (i, 0)
    row_p = lambda i: (jnp.minimum(i, n_ptiles - 1), 0)
    row_s = lambda i: (jnp.maximum(i - n_ptiles, 0), 0)
    fixed = lambda i: (0, 0)
    choice = lambda k: pl.BlockSpec((TM_TOK * SUBLANES, LANES), lambda i: (k * n_tiles + i, 0))
    if split:
        out_specs = [pl.BlockSpec((TM_TOK, d), row_p), pl.BlockSpec((TM_TOK, d), row_s)]
        out_shape = [jax.ShapeDtypeStruct((n_p, d), F32), jax.ShapeDtypeStruct((n - n_p, d), F32)]
    else:
        out_specs = pl.BlockSpec((TM_TOK, d), row)
        out_shape = jax.ShapeDtypeStruct((n, d), F32)
    return pl.pallas_call(
        functools.partial(_combine_kernel, n_ptiles=n_ptiles),
        grid=(n_tiles,),
        in_specs=[pl.BlockSpec((TM_TOK, d), row)] + [choice(k) for k in range(TOP_K)]
                 + [pl.BlockSpec((TM_TOK, LANES), row),
                    pl.BlockSpec((TM_TOK, e), lambda i: (layer * n_ptiles + row_p(i)[0], 0)),
                    pl.BlockSpec((TM_TOK, e), lambda i: (layer * n_stiles + row_s(i)[0], 0)),
                    pl.BlockSpec((d, d), fixed), pl.BlockSpec((e, d), fixed), pl.BlockSpec((1, d), fixed),
                    pl.BlockSpec((1, d), fixed)],
        out_specs=out_specs,
        out_shape=out_shape,
        compiler_params=_cparams(("arbitrary",)),
        name="moe_combine_ln_embed",
    )(h, y, y, y, y, gate, p_prompt, p_sample, wpg_bf16, wpp_bf16, g_row, b_row)


def _layer_tail(h, ht, p_prompt, p_sample, layer, n_p, split, g2, b2, w_r, b_r, w_up, b_up, w_dn, b_dn,
                w_pg, w_pp, beside_gather):
    n, d = h.shape
    wr_pad = jnp.pad(w_r, ((0, 0), (0, LANES - N_EXPERTS)))
    br_row = jnp.pad(b_r, (0, LANES - N_EXPERTS))[None]
    n_asg = n * TOP_K
    n_blocks = n_asg // MOE_BLOCK + N_EXPERTS
    assert n_asg % MOE_BLOCK == 0 and n % SC_WINDOW == 0
    n_slots = n_blocks * MOE_BLOCK
    idx, gate, rank, pstart, tab = _router(h, wr_pad, br_row, n_blocks)
    experts = jnp.arange(N_EXPERTS, dtype=I32)[:, None, None]
    group_start = jnp.sum(jnp.where(idx[None, :TOP_K] == experts, pstart[0, :N_EXPERTS, None, None], 0), axis=0)
    slot_by_choice = group_start + rank[:TOP_K]
    xb = _sc_scatter_tiles(ht.reshape(n, SUBLANES, LANES), slot_by_choice, n_slots)
    yb = _experts(xb.reshape(n_slots * SUBLANES, LANES), tab[:n_blocks, 0], tab[0, 2:3], tab[:n_blocks, 1],
                  tab[:n_blocks, 3], layer, w_up, b_up[:, :, None, :], w_dn, b_dn[:, :, None, :])
    side_in, side_fn = beside_gather
    side_in, yb = lax.optimization_barrier((side_in, yb))
    side_out = side_fn(side_in)
    y = _sc_gather_tiles(yb.reshape(n_slots, SUBLANES, LANES), slot_by_choice.reshape(-1))
    side_out, y = lax.optimization_barrier((side_out, y))
    out = _combine(h, y.reshape(n_asg * SUBLANES, LANES), gate, p_prompt, p_sample, layer, n_p, split,
                   w_pg.astype(BF16), w_pp.astype(BF16), g2[None], b2[None])
    return out, side_out


def _lane_row(v, lane0):
    return jnp.zeros((1, LANES), F32).at[0, lane0:lane0 + v.shape[0]].set(v.astype(F32))


def kernel(x_prompt, x_sample, cache_fox_k, cache_fox_v, cache_fox_logf, state_gdn, state_gdn_conv,
           cache_pool, p_prompt, p_sample, w_in_ab, b_fgate, gdn_a_log, gdn_dt_bias, gdn_conv_w,
           gdn_norm_g, w_out_ab, pool_w, pool_scale, w_out_pool, ln1_g, ln1_b, ln2_g, ln2_b,
           w_router, b_router, w_expert_up, b_expert_up, w_expert_down, b_expert_down,
           w_ple_gate, w_ple_proj):
    n_pb, seq, d = x_prompt.shape
    n_sb, dseq, _ = x_sample.shape
    past = cache_fox_k.shape[2]
    assert n_pb == 1 and dseq == CHUNK and past % dseq == 0 and seq % TQ == 0
    assert d == SUBLANES * LANES
    n_p = n_pb * seq
    n_s = n_sb * dseq
    n = n_p + n_s
    assert n_p % TM_TOK == 0 and n_s % TM_TOK == 0
    n_layers = p_prompt.shape[0]
    pp_all = p_prompt.reshape(n_layers * n_p, -1)
    ps_all = p_sample.reshape(n_layers * n_s, -1)

    def tail(h, ht, i, split, beside_gather):
        return _layer_tail(h, ht, pp_all, ps_all, i, n_p, split, ln2_g[i], ln2_b[i], w_router[i], b_router[i],
                           w_expert_up, b_expert_up, w_expert_down, b_expert_down, w_ple_gate[i],
                           w_ple_proj[i], beside_gather)

    w_in = w_in_ab[0]
    n_small = 3 * N_HEADS
    ff0 = 3 * WIDTH
    gq0 = ff0 + N_HEADS
    ga0 = gq0 + 4 * WIDTH
    w_small = jnp.concatenate([w_in[:, ff0:gq0], w_in[:, ga0:ga0 + 2 * N_HEADS],
                               jnp.zeros((d, LANES - n_small), F32)], axis=1)
    w_all = jnp.concatenate([w_in[:, :ff0], w_in[:, gq0:ga0], w_small], axis=1)
    bf_row = _lane_row(b_fgate[0], LANE_F)
    up = _proj(x_prompt.reshape(n_p, d), w_all.astype(BF16), bf_row, TM_PROJ, U_COLS, False)
    us = _proj(x_sample.reshape(n_s, d), w_all, bf_row, TM_TOK, LANES, True)

    u3 = up[None]
    us3 = us.reshape(n_sb, dseq, U_COLS)
    cq_p, ck_p = _cumsum(u3, 1, n_p, CUMSUM_TILE, COL_S // LANES)
    lf_s = jnp.concatenate(
        [jnp.pad(cache_fox_logf[0].astype(F32), ((0, 0), (0, 0), (0, LANES - N_HEADS))),
         us3[:, :, COL_S:]], axis=1)
    cq_s, ck_s = _cumsum(lf_s, n_sb, past + dseq, past + dseq, 0)

    of_p = _fox(u3, lambda b, i: (0, i, COL_Q // WIDTH), u3, lambda b, j: (0, j, COL_K // WIDTH),
                u3, lambda b, j: (0, j, COL_V // WIDTH), cq_p, lambda b, i: (0, i, 0),
                ck_p, lambda b, j: (0, 0, j), 1, n_p // TQ, TQ, TK, 0)
    k_all = jnp.concatenate([cache_fox_k[0].reshape(n_sb, past, WIDTH), us3[:, :, COL_K:COL_K + WIDTH]], axis=1)
    v_all = jnp.concatenate([cache_fox_v[0].reshape(n_sb, past, WIDTH), us3[:, :, COL_V:COL_V + WIDTH]], axis=1)
    of_s = _fox(us3, lambda b, i: (b, 0, COL_Q // WIDTH), k_all, lambda b, j: (b, 0, 0),
                v_all, lambda b, j: (b, 0, 0), cq_s, lambda b, i: (b, past // dseq, 0),
                ck_s, lambda b, j: (b, 0, 0), n_sb, 1, dseq, past + dseq, past, hi=True)

    gw = 3 * WIDTH
    conv_w = jnp.pad(gdn_conv_w[0], ((0, SUBLANES - CONV_WIDTH), (0, 0)))
    gdn_args = (conv_w, _lane_row(gdn_a_log[0], LANE_A), _lane_row(gdn_dt_bias[0], LANE_A),
                gdn_norm_g[0][None])
    n_pstep = n_p // GDN_ROWS
    ends = lambda k: (jnp.asarray((np.arange(k) == 0).astype(np.int32)),
                      jnp.asarray((np.arange(k) == k - 1).astype(np.int32)))
    og_p, st_p = _gdn(up.reshape(n_pstep, GDN_ROWS, U_COLS), False, *ends(n_pstep),
                      jnp.zeros((n_pstep,), I32), jnp.zeros((1, SUBLANES, gw), F32), gdn_args[0],
                      jnp.zeros((1, N_HEADS, HEAD_DIM, HEAD_DIM), F32), *gdn_args[1:])
    conv_past = jnp.pad(state_gdn_conv[0].astype(F32), ((0, 0), (SUBLANES - (CONV_WIDTH - 1), 0), (0, 0)))
    ones = jnp.ones((n_sb,), I32)
    og_s, st_s = _gdn(us3, True, ones, ones, jnp.arange(n_sb, dtype=I32), conv_past, gdn_args[0],
                      state_gdn[0].astype(F32), *gdn_args[1:])
    h, ht = _outproj_ln((x_prompt.reshape(n_p, d), of_p.reshape(n_p, WIDTH), og_p.reshape(n_p, WIDTH)),
                        (x_sample.reshape(n_s, d), of_s.reshape(n_s, WIDTH), og_s.reshape(n_s, WIDTH)),
                        w_out_ab[0], ln1_g[0][None], ln1_b[0][None])
    head_major = lambda col: lambda u_: u_[:, col:col + WIDTH].reshape(1, n_pb, seq, N_HEADS, HEAD_DIM)
    x1, fk_prompt = tail(h, ht, 0, False, (up, head_major(COL_K)))

    pool_args = (pool_w[0].astype(BF16), pool_scale[0][None], w_out_pool[0].astype(BF16),
                 ln1_g[1][None], ln1_b[1][None])
    ratio = TM_TOK // POOL_HALO
    x1_halo = x1.reshape(n // POOL_HALO, POOL_HALO, d)
    h_pool = _pool_ln(x1, lambda i: (i, 0), x1_halo, lambda i: (jnp.maximum(i * ratio - 1, 0), 0, 0),
                      n_p // TM_TOK, TM_TOK, 0, True, *pool_args)
    cache16 = jnp.pad(cache_pool[0].astype(F32), ((0, 0), (POOL_HALO - POOL_STATE, 0), (0, 0)))
    h, ht = _pool_ln(x1, lambda i: (n_p // dseq + i, 0), cache16, lambda i: (i, 0, 0),
                     n_sb, dseq, past, False, *pool_args, into=h_pool)
    (x2_p, x2_s), fv_prompt = tail(h, ht, 1, True, (up, head_major(COL_V)))

    return (x2_p.reshape(n_pb, seq, d), x2_s.reshape(n_sb, dseq, d),
            fk_prompt,
            fv_prompt,
            up[:, COL_S:COL_S + N_HEADS].reshape(1, n_pb, seq, N_HEADS),
            st_p.reshape(1, n_pb, N_HEADS, HEAD_DIM, HEAD_DIM),
            up[seq - (CONV_WIDTH - 1):, COL_G:COL_G + gw].reshape(1, n_pb, CONV_WIDTH - 1, gw),
            x1[n_p - POOL_STATE:n_p].reshape(1, n_pb, POOL_STATE, d),
            us[:, COL_K:COL_K + WIDTH].reshape(1, n_sb, dseq, N_HEADS, HEAD_DIM),
            us[:, COL_V:COL_V + WIDTH].reshape(1, n_sb, dseq, N_HEADS, HEAD_DIM),
            us[:, COL_S:COL_S + N_HEADS].reshape(1, n_sb, dseq, N_HEADS),
            st_s.reshape(1, n_sb, N_HEADS, HEAD_DIM, HEAD_DIM),
            us[:, COL_G:COL_G + gw].reshape(n_sb, dseq, gw)[:, dseq - (CONV_WIDTH - 1):].reshape(
                1, n_sb, CONV_WIDTH - 1, gw),
            x1[n_p:].reshape(n_sb, dseq, d)[:, dseq - POOL_STATE:].reshape(1, n_sb, POOL_STATE, d))
```

```python
import functools

import numpy as np
import jax
import jax.numpy as jnp
from jax import lax
from jax.experimental import pallas as pl
from jax.experimental.pallas import tpu as pltpu
from jax.experimental.pallas import tpu_sc as plsc

F32 = jnp.float32
BF16 = jnp.bfloat16
I32 = jnp.int32
HIGHEST = lax.Precision.HIGHEST

LANES = 128
SUBLANES = 8
VMEM_LIMIT = 56 * 1024 * 1024

HEAD_DIM = 128
N_HEADS = 4
WIDTH = N_HEADS * HEAD_DIM
CHUNK = 64
CONV_WIDTH = 4
POOL_WINDOWS = (2, 4, 8, 16)
POOL_HALO = 16
POOL_STATE = 15
N_EXPERTS = 32
TOP_K = 4
SWIGLU_LIMIT = 7.0
SWIGLU_ALPHA = 1.702
DEPTH = 2
DN_ALPHA = (2 * DEPTH) ** 0.25
LN_EPS = 1e-5
NORM_EPS = 1e-6
NEG_INF = -1e30
LOG2E = 1.4426950408889634

COL_Q, COL_K, COL_V = 0, WIDTH, 2 * WIDTH
COL_G = 3 * WIDTH
COL_Z = 6 * WIDTH
COL_S = 7 * WIDTH
U_COLS = COL_S + LANES
LANE_F, LANE_A, LANE_B = 0, N_HEADS, 2 * N_HEADS

TM_PROJ = 512
TM_TOK = 512
SC_WINDOW = 32
MOE_BLOCK = 1024
TQ = 1024
TK = 1024
CUMSUM_TILE = 512
GDN_ROWS = 256
GDN_CHUNK = CHUNK


def _cparams(sem):
    return pltpu.CompilerParams(dimension_semantics=sem, vmem_limit_bytes=VMEM_LIMIT)


def _softplus(x):
    return jnp.maximum(x, 0.0) + jnp.log1p(jnp.exp(-jnp.abs(x)))


def _sigmoid(x):
    return 1.0 / (1.0 + jnp.exp(-x))


def _silu(x):
    return x * _sigmoid(x)


def _layer_norm(y, g, b):
    mu = jnp.mean(y, axis=-1, keepdims=True)
    yc = y - mu
    var = jnp.mean(yc * yc, axis=-1, keepdims=True)
    return yc * lax.rsqrt(var + LN_EPS) * g + b


def _dot_general(a, b, dims, hi):
    if hi:
        return lax.dot_general(a.astype(F32), b.astype(F32), (dims, ((), ())), precision=HIGHEST,
                               preferred_element_type=F32)
    return lax.dot_general(a.astype(BF16), b.astype(BF16), (dims, ((), ())), preferred_element_type=F32)


def _dot(a, b, hi=False):
    return _dot_general(a, b, ((1,), (0,)), hi)


def _dot_nt(a, b, hi=False):
    return _dot_general(a, b, ((1,), (1,)), hi)


def _dot_tn(a, b, hi=False):
    return _dot_general(a, b, ((0,), (0,)), hi)


def _spread_lanes(x, width):
    if width % LANES == 0:
        return jnp.concatenate([x] * (width // LANES), axis=1)
    return jnp.broadcast_to(x[:, 0:1], (x.shape[0], width))


def _load_token_tiles(ref, lead, n_tok):
    return jnp.concatenate([ref[(*lead, pl.ds(j, n_tok, stride=SUBLANES), slice(None))]
                            for j in range(SUBLANES)], axis=1)


def _store_token_tiles(ref, lead, x):
    for j in range(SUBLANES):
        ref[(*lead, pl.ds(j, x.shape[0], stride=SUBLANES), slice(None))] = x[:, j * LANES:(j + 1) * LANES]


def _lanes_to_rows(x, lane0):
    r = lax.broadcasted_iota(I32, (SUBLANES, LANES), 0)
    c = lax.broadcasted_iota(I32, (SUBLANES, LANES), 1)
    sel = (c == r + lane0).astype(F32)
    return lax.dot_general(sel, x, (((1,), (1,)), ((), ())), precision=HIGHEST,
                           preferred_element_type=F32)


def _proj_kernel(x_ref, w_ref, bf_ref, u_ref, *, hi):
    u = _dot(x_ref[...], w_ref[...], hi)
    u_ref[...] = u

    @pl.when(pl.program_id(1) == pl.num_programs(1) - 1)
    def _():
        small = u[:, u.shape[1] - LANES:]
        lane = lax.broadcasted_iota(I32, small.shape, 1)
        logf = -_softplus(-(small + bf_ref[...]))
        u_ref[:, u.shape[1] - LANES:] = jnp.where(lane < LANE_A, logf, small)


def _proj(x, w, bf_row, tm, tn, hi):
    n, d = x.shape
    m = w.shape[1]
    return pl.pallas_call(
        functools.partial(_proj_kernel, hi=hi),
        grid=(n // tm, m // tn),
        in_specs=[pl.BlockSpec((tm, d), lambda i, j: (i, 0)),
                  pl.BlockSpec((d, tn), lambda i, j: (0, j)),
                  pl.BlockSpec((1, LANES), lambda i, j: (0, 0))],
        out_specs=pl.BlockSpec((tm, tn), lambda i, j: (i, j)),
        out_shape=jax.ShapeDtypeStruct((n, m), F32),
        compiler_params=_cparams(("parallel", "parallel")),
        name="in_proj",
    )(x, w, bf_row)


def _cumsum_kernel(lf_ref, crep_ref, crow_ref, carry_ref):
    @pl.when(pl.program_id(1) == 0)
    def _():
        carry_ref[...] = jnp.zeros_like(carry_ref)

    lf = lf_ref[0]
    t = lf.shape[0]
    r = lax.broadcasted_iota(I32, (t, t), 0)
    c = lax.broadcasted_iota(I32, (t, t), 1)
    tril = (c <= r).astype(F32)
    cs = jnp.dot(tril, lf, precision=HIGHEST, preferred_element_type=F32) + carry_ref[0:1, :]
    carry_ref[...] = jnp.broadcast_to(cs[t - 1:t, :], carry_ref.shape)
    c2 = cs * LOG2E
    crow_ref[0] = _lanes_to_rows(c2, LANE_F)
    for h in range(N_HEADS):
        crep_ref[0, :, h * HEAD_DIM:(h + 1) * HEAD_DIM] = jnp.broadcast_to(
            c2[:, LANE_F + h:LANE_F + h + 1], (t, HEAD_DIM))


def _cumsum(arr, n_batch, length, tl, col_block):
    return pl.pallas_call(
        _cumsum_kernel,
        grid=(n_batch, length // tl),
        in_specs=[pl.BlockSpec((1, tl, LANES), lambda b, j: (b, j, col_block))],
        out_specs=[pl.BlockSpec((1, tl, WIDTH), lambda b, j: (b, j, 0)),
                   pl.BlockSpec((1, SUBLANES, tl), lambda b, j: (b, 0, j))],
        out_shape=[jax.ShapeDtypeStruct((n_batch, length, WIDTH), F32),
                   jax.ShapeDtypeStruct((n_batch, SUBLANES, length), F32)],
        scratch_shapes=[pltpu.VMEM((SUBLANES, LANES), F32)],
        compiler_params=_cparams(("parallel", "arbitrary")),
        name="logf_cumsum",
    )(arr)


def _fox_kernel(qi_ref, kj_ref, last_ref, q_ref, k_ref, v_ref, cq_ref, ck_ref, o_ref,
                m_ref, l_ref, acc_ref, *, tq, tk, past, hi):
    s_idx = pl.program_id(1)
    qi = qi_ref[s_idx]
    kj = kj_ref[s_idx]

    @pl.when(kj == 0)
    def _():
        m_ref[...] = jnp.full_like(m_ref, NEG_INF)
        l_ref[...] = jnp.zeros_like(l_ref)
        acc_ref[...] = jnp.zeros_like(acc_ref)

    def update(masked):
        if masked:
            q_pos = past + qi * tq + lax.broadcasted_iota(I32, (tq, tk), 0)
            k_pos = kj * tk + lax.broadcasted_iota(I32, (tq, tk), 1)
            visible = k_pos <= q_pos
        for h in range(N_HEADS):
            cols = slice(h * HEAD_DIM, (h + 1) * HEAD_DIM)
            q = q_ref[0, :, cols] * (HEAD_DIM ** -0.5 * LOG2E)
            t = _dot_nt(q, k_ref[0, :, cols], hi) - ck_ref[0, h:h + 1, :]
            if masked:
                t = jnp.where(visible, t, NEG_INF)
            cq = cq_ref[0, :, cols]
            m_prev = m_ref[h]
            m_new = jnp.maximum(m_prev, jnp.max(t, axis=-1, keepdims=True) + cq)
            p = jnp.exp2(t - _spread_lanes(m_new - cq, tk))
            alpha = jnp.exp2(m_prev - m_new)
            l_ref[h] = alpha * l_ref[h] + jnp.sum(p, axis=-1, keepdims=True)
            acc_ref[:, cols] = alpha * acc_ref[:, cols] + _dot(p, v_ref[0, :, cols], hi)
            m_ref[h] = m_new

    crosses_diagonal = kj * tk + (tk - 1) > past + qi * tq
    pl.when(crosses_diagonal)(functools.partial(update, True))
    pl.when(jnp.logical_not(crosses_diagonal))(functools.partial(update, False))

    @pl.when(last_ref[s_idx] == 1)
    def _():
        for h in range(N_HEADS):
            cols = slice(h * HEAD_DIM, (h + 1) * HEAD_DIM)
            o_ref[0, :, cols] = acc_ref[:, cols] / l_ref[h]


def _fox_schedule(n_q, tq, tk, past):
    qi, kj, last = [], [], []
    for i in range(n_q):
        hi = (past + (i + 1) * tq - 1) // tk
        for j in range(hi + 1):
            qi.append(i)
            kj.append(j)
            last.append(1 if j == hi else 0)
    return (jnp.asarray(np.array(qi, np.int32)), jnp.asarray(np.array(kj, np.int32)),
            jnp.asarray(np.array(last, np.int32)))


def _fox(q_arr, q_map, k_arr, k_map, v_arr, v_map, cq_arr, cq_map, ck_arr, ck_map,
         n_batch, n_q, tq, tk, past, hi=False):
    qi, kj, last = _fox_schedule(n_q, tq, tk, past)
    n_steps = int(qi.shape[0])
    spec = lambda shape, fn, tab: pl.BlockSpec(shape, lambda b, s, qi_r, kj_r, la_r: fn(b, (qi_r if tab == 'q' else kj_r)[s]))
    return pl.pallas_call(
        functools.partial(_fox_kernel, tq=tq, tk=tk, past=past, hi=hi),
        grid_spec=pltpu.PrefetchScalarGridSpec(
            num_scalar_prefetch=3,
            grid=(n_batch, n_steps),
            in_specs=[spec((1, tq, WIDTH), q_map, 'q'),
                      spec((1, tk, WIDTH), k_map, 'k'),
                      spec((1, tk, WIDTH), v_map, 'k'),
                      spec((1, tq, WIDTH), cq_map, 'q'),
                      spec((1, SUBLANES, tk), ck_map, 'k')],
            out_specs=spec((1, tq, WIDTH), lambda b, i: (b, i, 0), 'q'),
            scratch_shapes=[pltpu.VMEM((N_HEADS, tq, HEAD_DIM), F32),
                            pltpu.VMEM((N_HEADS, tq, HEAD_DIM), F32),
                            pltpu.VMEM((tq, WIDTH), F32)]),
        out_shape=jax.ShapeDtypeStruct((n_batch, n_q * tq, WIDTH), F32),
        compiler_params=_cparams(("parallel", "arbitrary")),
        name="fox_attention",
    )(qi, kj, last, q_arr, k_arr, v_arr, cq_arr, ck_arr)


def _gdn_kernel(first_ref, last_ref, seq_ref,
                pre_ref, z_ref, sm_ref, cpast_ref, convw_ref, s0_ref, alog_ref, dtb_ref, ng_ref,
                o_ref, sout_ref, stage_ref, s_ref, *, rows, chunk, hi):
    dot, dot_nt, dot_tn = (functools.partial(f, hi=hi) for f in (_dot, _dot_nt, _dot_tn))
    step = pl.program_id(0)
    halo = SUBLANES
    n_chunks = rows // chunk

    @pl.when(first_ref[step] == 1)
    def _():
        stage_ref[0:halo, :] = cpast_ref[0]
        s_ref[...] = s0_ref[0]

    stage_ref[halo:halo + rows, :] = pre_ref[0]
    conv = stage_ref[halo:halo + rows, :] * convw_ref[CONV_WIDTH - 1:CONV_WIDTH, :]
    for j in range(1, CONV_WIDTH):
        conv = conv + (stage_ref[halo - j:halo - j + rows, :]
                       * convw_ref[CONV_WIDTH - 1 - j:CONV_WIDTH - j, :])
    stage_ref[0:halo, :] = stage_ref[rows:rows + halo, :]
    act = _silu(conv)

    small = sm_ref[0]
    beta_all = _sigmoid(small)
    g_all = -jnp.exp(alog_ref[...]) * _softplus(small + dtb_ref[...])
    r = lax.broadcasted_iota(I32, (rows, rows), 0)
    c = lax.broadcasted_iota(I32, (rows, rows), 1)
    same_chunk = (r // chunk) == (c // chunk)
    incl = same_chunk & (c <= r)
    strict = same_chunk & (c < r)
    eye = (c == r).astype(F32)
    gc_all = jnp.dot(incl.astype(F32), g_all, precision=HIGHEST, preferred_element_type=F32)
    gc_rows = _lanes_to_rows(gc_all, LANE_A)

    heads = range(N_HEADS)
    head_cols = [slice(h * HEAD_DIM, (h + 1) * HEAD_DIM) for h in heads]
    q, k, gc, decay, kb, vb, low = [], [], [], [], [], [], []
    for h in heads:
        qh = act[:, h * HEAD_DIM:(h + 1) * HEAD_DIM]
        kh = act[:, WIDTH + h * HEAD_DIM:WIDTH + (h + 1) * HEAD_DIM]
        vh = act[:, 2 * WIDTH + h * HEAD_DIM:2 * WIDTH + (h + 1) * HEAD_DIM]
        q.append(qh * lax.rsqrt(jnp.sum(qh * qh, axis=-1, keepdims=True) + NORM_EPS) * (HEAD_DIM ** -0.5))
        k.append(kh * lax.rsqrt(jnp.sum(kh * kh, axis=-1, keepdims=True) + NORM_EPS))
        beta = beta_all[:, LANE_B + h:LANE_B + h + 1]
        gc.append(gc_all[:, LANE_A + h:LANE_A + h + 1])
        diff = gc[h] - gc_rows[h:h + 1, :]
        decay.append(jnp.where(incl, jnp.exp(jnp.where(incl, diff, 0.0)), 0.0))
        kb.append(k[h] * beta)
        vb.append(vh * beta)
    for h in heads:
        low.append(jnp.where(strict, dot_nt(kb[h], k[h]) * decay[h], 0.0))
    inv = [eye - low[h] for h in heads]
    pw = [dot(low[h], low[h]) for h in heads]
    n_sq = chunk.bit_length() - 2
    for it in range(n_sq):
        inv = [inv[h] + dot(inv[h], pw[h]) for h in heads]
        if it + 1 < n_sq:
            pw = [dot(pw[h], pw[h]) for h in heads]
    egc = [jnp.exp(gc[h]) for h in heads]
    uw = [dot(inv[h], jnp.concatenate([vb[h], kb[h] * egc[h]], axis=1)) for h in heads]
    intra = [jnp.where(incl, dot_nt(q[h], k[h]) * decay[h], 0.0) for h in heads]
    qd = [q[h] * egc[h] for h in heads]
    g_last = [[gc[h][(g + 1) * chunk - 1:(g + 1) * chunk, :] for g in range(n_chunks)] for h in heads]
    kd = [k[h] * jnp.exp(jnp.concatenate([jnp.broadcast_to(gl, (chunk, 1)) for gl in g_last[h]], axis=0)
                         - gc[h]) for h in heads]
    state = [s_ref[h] for h in heads]
    v_new = [[] for _ in heads]
    for g in range(n_chunks):
        rs = slice(g * chunk, (g + 1) * chunk)
        for h in heads:
            v_new[h].append(uw[h][rs, :HEAD_DIM] - dot(uw[h][rs, HEAD_DIM:], state[h]))
        for h in heads:
            v_rows = jnp.concatenate(
                v_new[h] + [jnp.zeros((rows - (g + 1) * chunk, HEAD_DIM), F32)] * (g + 1 < n_chunks), axis=0)
            o = dot(qd[h][rs, :], state[h]) + dot(intra[h][rs, :], v_rows)
            state[h] = state[h] * jnp.exp(g_last[h][g]) + dot_tn(kd[h][rs, :], v_new[h][g])
            o = (o * lax.rsqrt(jnp.mean(o * o, axis=-1, keepdims=True) + NORM_EPS)
                 * ng_ref[...] * _silu(z_ref[0, rs, head_cols[h]]))
            o_ref[0, rs, head_cols[h]] = o
    for h in heads:
        s_ref[h] = state[h]

    @pl.when(last_ref[step] == 1)
    def _():
        sout_ref[0] = s_ref[...]


def _gdn(u_view, hi, first, last, seq, conv_past, conv_w, s0, alog_row, dtb_row, ng_row):
    n_steps, rows, _ = u_view.shape
    n_seq = s0.shape[0]
    gw = 3 * WIDTH
    return pl.pallas_call(
        functools.partial(_gdn_kernel, rows=rows, chunk=min(rows, GDN_CHUNK), hi=hi),
        grid_spec=pltpu.PrefetchScalarGridSpec(
            num_scalar_prefetch=3,
            grid=(n_steps,),
            in_specs=[pl.BlockSpec((1, rows, gw), lambda s, f, l, q: (s, 0, COL_G // gw)),
                      pl.BlockSpec((1, rows, WIDTH), lambda s, f, l, q: (s, 0, COL_Z // WIDTH)),
                      pl.BlockSpec((1, rows, LANES), lambda s, f, l, q: (s, 0, COL_S // LANES)),
                      pl.BlockSpec((1, SUBLANES, gw), lambda s, f, l, q: (q[s], 0, 0)),
                      pl.BlockSpec((SUBLANES, gw), lambda s, f, l, q: (0, 0)),
                      pl.BlockSpec((1, N_HEADS, HEAD_DIM, HEAD_DIM), lambda s, f, l, q: (q[s], 0, 0, 0)),
                      pl.BlockSpec((1, LANES), lambda s, f, l, q: (0, 0)),
                      pl.BlockSpec((1, LANES), lambda s, f, l, q: (0, 0)),
                      pl.BlockSpec((1, LANES), lambda s, f, l, q: (0, 0))],
            out_specs=[pl.BlockSpec((1, rows, WIDTH), lambda s, f, l, q: (s, 0, 0)),
                       pl.BlockSpec((1, N_HEADS, HEAD_DIM, HEAD_DIM), lambda s, f, l, q: (q[s], 0, 0, 0))],
            scratch_shapes=[pltpu.VMEM((rows + SUBLANES, gw), F32),
                            pltpu.VMEM((N_HEADS, HEAD_DIM, HEAD_DIM), F32)]),
        out_shape=[jax.ShapeDtypeStruct((n_steps, rows, WIDTH), F32),
                   jax.ShapeDtypeStruct((n_seq, N_HEADS, HEAD_DIM, HEAD_DIM), F32)],
        compiler_params=_cparams(("arbitrary",)),
        name="gated_deltanet",
    )(first, last, seq, u_view, u_view, u_view, conv_past, conv_w, s0, alog_row, dtb_row, ng_row)


def _outproj_ln_kernel(xp_ref, ofp_ref, ogp_ref, xs_ref, ofs_ref, ogs_ref, w_ref, wf_ref, g_ref, b_ref,
                       h_ref, ht_ref, *, hi_from):
    def run(hi, x_ref, of_ref, og_ref):
        w = wf_ref if hi else w_ref
        mix = _dot(of_ref[...], w[0:WIDTH, :], hi) + _dot(og_ref[...], w[WIDTH:2 * WIDTH, :], hi)
        h = _layer_norm(DN_ALPHA * x_ref[...] + mix, g_ref[...], b_ref[...])
        h_ref[...] = h
        _store_token_tiles(ht_ref, (), h)

    pl.when(pl.program_id(0) < hi_from)(functools.partial(run, False, xp_ref, ofp_ref, ogp_ref))
    pl.when(pl.program_id(0) >= hi_from)(functools.partial(run, True, xs_ref, ofs_ref, ogs_ref))


def _outproj_ln(prompt, sample, w, g_row, b_row):
    n_p, d = prompt[0].shape
    n = n_p + sample[0].shape[0]
    hi_from = n_p // TM_TOK
    row = lambda i: (i, 0)
    row_p = lambda i: (jnp.minimum(i, hi_from - 1), 0)
    row_s = lambda i: (jnp.maximum(i - hi_from, 0), 0)
    fixed = lambda i: (0, 0)
    group = lambda rows: [pl.BlockSpec((TM_TOK, d), rows), pl.BlockSpec((TM_TOK, WIDTH), rows),
                          pl.BlockSpec((TM_TOK, WIDTH), rows)]
    return pl.pallas_call(
        functools.partial(_outproj_ln_kernel, hi_from=hi_from),
        grid=(n // TM_TOK,),
        in_specs=group(row_p) + group(row_s) + [pl.BlockSpec((2 * WIDTH, d), fixed),
                                                pl.BlockSpec((2 * WIDTH, d), fixed),
                                                pl.BlockSpec((1, d), fixed), pl.BlockSpec((1, d), fixed)],
        out_specs=[pl.BlockSpec((TM_TOK, d), row), pl.BlockSpec((TM_TOK * SUBLANES, LANES), row)],
        out_shape=[jax.ShapeDtypeStruct((n, d), F32), jax.ShapeDtypeStruct((n * SUBLANES, LANES), F32)],
        compiler_params=_cparams(("parallel",)),
        name="out_proj_ln",
    )(*prompt, *sample, w.astype(BF16), w, g_row, b_row)


def _pool_ln_kernel(x_ref, halo_ref, pw_ref, ps_ref, w_ref, g_ref, b_ref, h_ref, ht_ref, stage_ref,
                    lvl_a_ref, lvl_b_ref, *, tm, pos0, zero_first_halo):
    i = pl.program_id(0)
    pad = SUBLANES
    n_rows = POOL_HALO + tm
    for ref in (stage_ref, lvl_a_ref, lvl_b_ref):
        ref[0:pad, :] = jnp.zeros((pad, ref.shape[1]), F32)
    stage_ref[pad:pad + POOL_HALO, :] = halo_ref[0]
    if zero_first_halo:
        @pl.when(i == 0)
        def _():
            stage_ref[pad:pad + POOL_HALO, :] = jnp.zeros((POOL_HALO, stage_ref.shape[1]), F32)
    x = x_ref[...]
    stage_ref[pad + POOL_HALO:pad + n_rows, :] = x
    gdim = x.shape[1] // len(POOL_WINDOWS)
    pos = pos0 + lax.broadcasted_iota(I32, (tm, 1), 0)
    if zero_first_halo:
        pos = pos + i * tm

    def window_sum(cols, win):
        assert win & (win - 1) == 0 and win <= POOL_HALO
        src, src_cols, span, level = stage_ref, cols, 1, 0
        while span < win:
            dst = (lvl_a_ref, lvl_b_ref)[level % 2]
            dst[pad:pad + n_rows, :] = (src[pad:pad + n_rows, src_cols]
                                        + src[pad - span:pad - span + n_rows, src_cols])
            src, src_cols, span, level = dst, slice(None), 2 * span, level + 1
        return src[pad + POOL_HALO:pad + n_rows, src_cols]

    parts = []
    for gi, win in enumerate(POOL_WINDOWS):
        cols = slice(gi * gdim, (gi + 1) * gdim)
        s = window_sum(cols, win)
        cnt = jnp.minimum(pos + 1, win).astype(F32)
        zg = s / cnt - x[:, cols]
        parts.append(_dot(zg, pw_ref[gi]))
    zg = jnp.concatenate(parts, axis=-1) * ps_ref[...]
    mix = _dot(zg, w_ref[...])
    h = _layer_norm(DN_ALPHA * x + mix, g_ref[...], b_ref[...])
    h_ref[...] = h
    _store_token_tiles(ht_ref, (), h)


def _pool_ln_into_kernel(*refs, **kw):
    _pool_ln_kernel(*refs[:7], *refs[9:], **kw)


def _pool_ln(x, x_map, halo_arr, halo_map, n_tiles, tm, pos0, zero_first_halo,
             pw_bf16, ps_row, w_bf16, g_row, b_row, into=None):
    n, d = x.shape
    gdim = d // len(POOL_WINDOWS)
    fixed = lambda i: (0, 0)
    in_specs = [pl.BlockSpec((tm, d), x_map),
                pl.BlockSpec((1, POOL_HALO, d), halo_map),
                pl.BlockSpec((len(POOL_WINDOWS), gdim, gdim), lambda i: (0, 0, 0)),
                pl.BlockSpec((1, d), fixed), pl.BlockSpec((d, d), fixed),
                pl.BlockSpec((1, d), fixed), pl.BlockSpec((1, d), fixed)]
    args = (x, halo_arr, pw_bf16, ps_row, w_bf16, g_row, b_row)
    kw = dict(tm=tm, pos0=pos0, zero_first_halo=zero_first_halo)
    return pl.pallas_call(
        functools.partial(_pool_ln_kernel if into is None else _pool_ln_into_kernel, **kw),
        grid=(n_tiles,),
        in_specs=in_specs + ([] if into is None else [pl.BlockSpec(memory_space=pl.ANY)] * 2),
        out_specs=[pl.BlockSpec((tm, d), x_map),
                   pl.BlockSpec((tm * SUBLANES, LANES), x_map)],
        out_shape=[jax.ShapeDtypeStruct((n, d), F32), jax.ShapeDtypeStruct((n * SUBLANES, LANES), F32)],
        input_output_aliases={} if into is None else {7: 0, 8: 1},
        scratch_shapes=[pltpu.VMEM((SUBLANES + POOL_HALO + tm, d), F32),
                        pltpu.VMEM((SUBLANES + POOL_HALO + tm, gdim), F32),
                        pltpu.VMEM((SUBLANES + POOL_HALO + tm, gdim), F32)],
        compiler_params=_cparams(("arbitrary",)),
        name="pool_mixer_ln",
    )(*args, *(() if into is None else into))


def _router_kernel(h_ref, wr_ref, br_ref, idx_ref, gate_ref, rank_ref, pstart_ref, tab_ref, carry_ref):
    @pl.when(pl.program_id(0) == 0)
    def _():
        carry_ref[...] = jnp.zeros_like(carry_ref)

    tm = h_ref.shape[0]
    lane = lax.broadcasted_iota(I32, (tm, LANES), 1).astype(F32)
    logits = jnp.dot(h_ref[...], wr_ref[...], precision=HIGHEST, preferred_element_type=F32)
    work = jnp.where(lane < N_EXPERTS, logits + br_ref[...], -jnp.inf)
    vals, ids = [], []
    for _ in range(TOP_K):
        m = jnp.max(work, axis=-1, keepdims=True)
        ik = jnp.min(jnp.where(work == m, lane, float(LANES)), axis=-1, keepdims=True)
        vals.append(m)
        ids.append(ik)
        work = jnp.where(lane == ik, -jnp.inf, work)
    exps = [jnp.exp(v - vals[0]) for v in vals]
    denom = exps[0]
    for e in exps[1:]:
        denom = denom + e
    multihot = jnp.zeros((tm, LANES), F32)
    idx_out = jnp.zeros((tm, LANES), F32)
    gate_out = jnp.zeros((tm, LANES), F32)
    for k in range(TOP_K):
        multihot = multihot + (lane == ids[k]).astype(F32)
        idx_out = jnp.where(lane == k, ids[k], idx_out)
        gate_out = jnp.where(lane == k, exps[k] / denom, gate_out)
    r = lax.broadcasted_iota(I32, (tm, tm), 0)
    c = lax.broadcasted_iota(I32, (tm, tm), 1)
    before = _dot((c < r).astype(F32), multihot) + carry_ref[0:1, :]
    rank_out = jnp.zeros((tm, LANES), F32)
    for k in range(TOP_K):
        rk = jnp.sum(jnp.where(lane == ids[k], before, 0.0), axis=-1, keepdims=True)
        rank_out = jnp.where(lane == k, rk, rank_out)
    idx_ref[...] = jnp.transpose(idx_out)[:SUBLANES].astype(I32)
    rank_ref[...] = jnp.transpose(rank_out)[:SUBLANES].astype(I32)
    gate_ref[...] = gate_out
    total = carry_ref[0:1, :] + jnp.sum(multihot, axis=0, keepdims=True)
    carry_ref[...] = jnp.broadcast_to(total, carry_ref.shape)

    @pl.when(pl.program_id(0) == pl.num_programs(0) - 1)
    def _():
        n_rows = tab_ref.shape[0]
        padded = jnp.floor((total + (MOE_BLOCK - 1)) / MOE_BLOCK) * MOE_BLOCK
        rr = lax.broadcasted_iota(I32, (LANES, LANES), 0)
        cc = lax.broadcasted_iota(I32, (LANES, LANES), 1)
        pend = jnp.dot(jnp.broadcast_to(padded, (SUBLANES, LANES)), (rr <= cc).astype(F32),
                       precision=HIGHEST, preferred_element_type=F32)[0:1, :]
        pstart = pend - padded
        n_used = jnp.max(pend, axis=-1, keepdims=True) / MOE_BLOCK
        elane = lax.broadcasted_iota(I32, (n_rows, LANES), 1)
        blk = jnp.minimum(lax.broadcasted_iota(I32, (n_rows, 1), 0).astype(F32), n_used - 1.0) * MOE_BLOCK
        ends_before = jnp.where((elane < N_EXPERTS) & (pend <= blk), 1.0, 0.0)
        block_e = jnp.minimum(jnp.sum(ends_before, axis=-1, keepdims=True), N_EXPERTS - 1.0)
        mine = elane.astype(F32) == block_e
        filled = jnp.sum(jnp.where(mine, total - (blk - pstart), 0.0), axis=-1, keepdims=True)
        block_valid = jnp.clip(filled, 0.0, float(MOE_BLOCK))
        group_end = jnp.sum(jnp.where(mine, pend, 0.0), axis=-1, keepdims=True)
        next_e = jnp.sum(jnp.where((elane < N_EXPERTS) & (pend <= group_end), 1.0, 0.0), axis=-1, keepdims=True)
        tab = jnp.where(elane == 0, block_e, jnp.where(elane == 1, block_valid,
                                                       jnp.where(elane == 2, n_used,
                                                                 jnp.where(elane == 3, next_e, 0.0))))
        tab_ref[...] = tab.astype(I32)
        pstart_ref[...] = jnp.broadcast_to(pstart, pstart_ref.shape).astype(I32)


def _router(h, wr_pad, br_row, n_blocks):
    n, d = h.shape
    row = lambda i: (i, 0)
    col = lambda i: (0, i)
    fixed = lambda i: (0, 0)
    tab_rows = -(-n_blocks // SUBLANES) * SUBLANES
    return pl.pallas_call(
        _router_kernel,
        grid=(n // TM_TOK,),
        in_specs=[pl.BlockSpec((TM_TOK, d), row), pl.BlockSpec((d, LANES), fixed),
                  pl.BlockSpec((1, LANES), fixed)],
        out_specs=[pl.BlockSpec((SUBLANES, TM_TOK), col), pl.BlockSpec((TM_TOK, LANES), row),
                   pl.BlockSpec((SUBLANES, TM_TOK), col), pl.BlockSpec((SUBLANES, LANES), fixed),
                   pl.BlockSpec((tab_rows, LANES), fixed)],
        out_shape=[jax.ShapeDtypeStruct((SUBLANES, n), I32), jax.ShapeDtypeStruct((n, LANES), F32),
                   jax.ShapeDtypeStruct((SUBLANES, n), I32), jax.ShapeDtypeStruct((SUBLANES, LANES), I32),
                   jax.ShapeDtypeStruct((tab_rows, LANES), I32)],
        scratch_shapes=[pltpu.VMEM((SUBLANES, LANES), F32)],
        compiler_params=_cparams(("arbitrary",)),
        name="moe_router",
    )(h, wr_pad, br_row)


def _expert_kernel(be_ref, nu_ref, valid_ref, next_ref, xb_ref, wup_ref, bup_ref, wdn_ref, bdn_ref, yb_ref,
                   wup_f32_ref, wdn_f32_ref, wup_bf_ref, wdn_bf_ref, wsem, *, layer):
    b = pl.program_id(0)

    def fetch(e):
        return (pltpu.make_async_copy(wup_ref.at[layer, e], wup_f32_ref, wsem.at[0]),
                pltpu.make_async_copy(wdn_ref.at[layer, e], wdn_f32_ref, wsem.at[1]))

    @pl.when(b == 0)
    def _():
        for copy in fetch(be_ref[0]):
            copy.start()

    @pl.when(b < nu_ref[0])
    def _():
        @pl.when((b == 0) | (be_ref[b] != be_ref[jnp.maximum(b - 1, 0)]))
        def _():
            for copy in fetch(be_ref[b]):
                copy.wait()
            wup_bf_ref[...] = wup_f32_ref[...].astype(BF16)
            wdn_bf_ref[...] = wdn_f32_ref[...].astype(BF16)

            @pl.when(next_ref[b] < N_EXPERTS)
            def _():
                for copy in fetch(next_ref[b]):
                    copy.start()

        d_exp = wdn_f32_ref.shape[0]

        def run(n_rows):
            x = _load_token_tiles(xb_ref, (), n_rows)
            row = lax.broadcasted_iota(I32, (n_rows, 1), 0)
            x = jnp.where(row < valid_ref[b], x, 0.0)
            hu = jnp.dot(x.astype(BF16), wup_bf_ref[...], preferred_element_type=F32) + bup_ref[0, 0]
            glu = jnp.minimum(hu[:, :d_exp], SWIGLU_LIMIT)
            lin = jnp.clip(hu[:, d_exp:], -SWIGLU_LIMIT, SWIGLU_LIMIT)
            a = glu * _sigmoid(SWIGLU_ALPHA * glu) * (lin + 1.0)
            y = jnp.dot(a.astype(BF16), wdn_bf_ref[...], preferred_element_type=F32) + bdn_ref[0, 0]
            if n_rows < MOE_BLOCK:
                y = jnp.concatenate([y, jnp.zeros((MOE_BLOCK - n_rows, y.shape[1]), F32)], axis=0)
            _store_token_tiles(yb_ref, (), y)

        half = MOE_BLOCK // 2
        pl.when(valid_ref[b] > half)(functools.partial(run, MOE_BLOCK))
        pl.when(valid_ref[b] <= half)(functools.partial(run, half))

    @pl.when(b >= nu_ref[0])
    def _():
        yb_ref[...] = jnp.zeros_like(yb_ref)


def _experts(xb, block_e, n_used, block_valid, next_e, layer, w_up, b_up, w_dn, b_dn):
    d = SUBLANES * LANES
    rows = MOE_BLOCK * SUBLANES
    n_blocks = xb.shape[0] // rows
    d_up = w_up.shape[3]
    d_exp = w_dn.shape[2]
    bsel = lambda b, be, nu, va, ne: (layer, be[b], 0, 0)
    return pl.pallas_call(
        functools.partial(_expert_kernel, layer=layer),
        grid_spec=pltpu.PrefetchScalarGridSpec(
            num_scalar_prefetch=4,
            grid=(n_blocks,),
            in_specs=[pl.BlockSpec((rows, LANES), lambda b, be, nu, va, ne: (jnp.minimum(b, nu[0] - 1), 0)),
                      pl.BlockSpec(memory_space=pl.ANY),
                      pl.BlockSpec((1, 1, 1, d_up), bsel),
                      pl.BlockSpec(memory_space=pl.ANY),
                      pl.BlockSpec((1, 1, 1, d), bsel)],
            out_specs=pl.BlockSpec((rows, LANES), lambda b, be, nu, va, ne: (b, 0)),
            scratch_shapes=[pltpu.VMEM((d, d_up), F32), pltpu.VMEM((d_exp, d), F32),
                            pltpu.VMEM((d, d_up), BF16), pltpu.VMEM((d_exp, d), BF16),
                            pltpu.SemaphoreType.DMA((2,))]),
        out_shape=jax.ShapeDtypeStruct(xb.shape, F32),
        compiler_params=_cparams(("arbitrary",)),
        name="moe_experts",
    )(block_e, n_used, block_valid, next_e, xb, w_up, b_up, w_dn, b_dn)


def _sc_gather_tiles(table, idx):
    m = idx.shape[0]
    mesh = plsc.VectorSubcoreMesh(core_axis_name="core", subcore_axis_name="subcore")

    @functools.partial(pl.kernel, out_type=jax.ShapeDtypeStruct((m, SUBLANES, LANES), table.dtype), mesh=mesh)
    def gather(table_hbm, idx_hbm, out_hbm):
        def window(idx_vmem, out_vmem):
            pltpu.sync_copy(table_hbm.at[idx_vmem.at[0, pl.ds(0, SC_WINDOW)]], out_vmem)

        pltpu.emit_pipeline(
            window,
            grid=(m // SC_WINDOW,),
            in_specs=[pl.BlockSpec((1, LANES), lambda i: (i, 0))],
            out_specs=[pl.BlockSpec((SC_WINDOW, SUBLANES, LANES), lambda i: (i, 0, 0))],
            core_axis_name=("core", "subcore"),
            dimension_semantics=(pltpu.PARALLEL,),
        )(idx_hbm, out_hbm)

    idx_rows = jnp.pad(idx.reshape(m // SC_WINDOW, SC_WINDOW), ((0, 0), (0, LANES - SC_WINDOW)))
    return gather(table, idx_rows)


def _sc_scatter_tiles(tiles, idx_by_choice, n_out):
    n = tiles.shape[0]
    mesh = plsc.VectorSubcoreMesh(core_axis_name="core", subcore_axis_name="subcore")

    @functools.partial(pl.kernel, out_type=jax.ShapeDtypeStruct((n_out, SUBLANES, LANES), tiles.dtype),
                       mesh=mesh)
    def scatter(tiles_hbm, *refs):
        idx_hbm, out_hbm = refs[:TOP_K], refs[TOP_K]

        def window(tiles_vmem, *idx_vmem):
            for k in range(TOP_K):
                pltpu.sync_copy(tiles_vmem, out_hbm.at[idx_vmem[k].at[0, pl.ds(0, SC_WINDOW)]])

        pltpu.emit_pipeline(
            window,
            grid=(n // SC_WINDOW,),
            in_specs=[pl.BlockSpec((SC_WINDOW, SUBLANES, LANES), lambda i: (i, 0, 0))]
                     + [pl.BlockSpec((1, LANES), lambda i: (i, 0))] * TOP_K,
            out_specs=[],
            core_axis_name=("core", "subcore"),
            dimension_semantics=(pltpu.PARALLEL,),
        )(tiles_hbm, *idx_hbm)

    idx_rows = jnp.pad(idx_by_choice.reshape(TOP_K, n // SC_WINDOW, SC_WINDOW),
                       ((0, 0), (0, 0), (0, LANES - SC_WINDOW)))
    return scatter(tiles, *[idx_rows[k] for k in range(TOP_K)])


def _combine_kernel(h_ref, y0_ref, y1_ref, y2_ref, y3_ref, gate_ref, pp_ref, ps_ref, wpg_ref, wpp_ref,
                    g_ref, b_ref, *outs, n_ptiles):
    outp_ref, outs_ref = outs[0], outs[-1]
    tm = h_ref.shape[0]
    gate = gate_ref[...]
    moe = _load_token_tiles(y0_ref, (), tm) * gate[:, 0:1]
    for k, y_ref in enumerate((y1_ref, y2_ref, y3_ref), start=1):
        moe = moe + _load_token_tiles(y_ref, (), tm) * gate[:, k:k + 1]
    h2 = _layer_norm(DN_ALPHA * h_ref[...] + moe, g_ref[...], b_ref[...])
    embed_gate = _sigmoid(_dot(h2, wpg_ref[...]))

    def finish(p_ref, out_ref):
        out_ref[...] = h2 + embed_gate * _dot(p_ref[...], wpp_ref[...])

    is_prompt = pl.program_id(0) < n_ptiles
    pl.when(is_prompt)(functools.partial(finish, pp_ref, outp_ref))
    pl.when(jnp.logical_not(is_prompt))(functools.partial(finish, ps_ref, outs_ref))


def _combine(h, y, gate, p_prompt, p_sample, layer, n_p, split, wpg_bf16, wpp_bf16, g_row, b_row):
    n, d = h.shape
    e = p_prompt.shape[1]
    n_tiles = n // TM_TOK
    n_ptiles = n_p // TM_TOK
    n_stiles = (n - n_p) // TM_TOK
    row = lambda i: (i, 0)
    row_p = lambda i: (jnp.minimum(i, n_ptiles - 1), 0)
    row_s = lambda i: (jnp.maximum(i - n_ptiles, 0), 0)
    fixed = lambda i: (0, 0)
    choice = lambda k: pl.BlockSpec((TM_TOK * SUBLANES, LANES), lambda i: (k * n_tiles + i, 0))
    if split:
        out_specs = [pl.BlockSpec((TM_TOK, d), row_p), pl.BlockSpec((TM_TOK, d), row_s)]
        out_shape = [jax.ShapeDtypeStruct((n_p, d), F32), jax.ShapeDtypeStruct((n - n_p, d), F32)]
    else:
        out_specs = pl.BlockSpec((TM_TOK, d), row)
        out_shape = jax.ShapeDtypeStruct((n, d), F32)
    return pl.pallas_call(
        functools.partial(_combine_kernel, n_ptiles=n_ptiles),
        grid=(n_tiles,),
        in_specs=[pl.BlockSpec((TM_TOK, d), row)] + [choice(k) for k in range(TOP_K)]
                 + [pl.BlockSpec((TM_TOK, LANES), row),
                    pl.BlockSpec((TM_TOK, e), lambda i: (layer * n_ptiles + row_p(i)[0], 0)),
                    pl.BlockSpec((TM_TOK, e), lambda i: (layer * n_stiles + row_s(i)[0], 0)),
                    pl.BlockSpec((d, d), fixed), pl.BlockSpec((e, d), fixed), pl.BlockSpec((1, d), fixed),
                    pl.BlockSpec((1, d), fixed)],
        out_specs=out_specs,
        out_shape=out_shape,
        compiler_params=_cparams(("arbitrary",)),
        name="moe_combine_ln_embed",
    )(h, y, y, y, y, gate, p_prompt, p_sample, wpg_bf16, wpp_bf16, g_row, b_row)


def _layer_tail(h, ht, p_prompt, p_sample, layer, n_p, split, g2, b2, w_r, b_r, w_up, b_up, w_dn, b_dn,
                w_pg, w_pp):
    n, d = h.shape
    wr_pad = jnp.pad(w_r, ((0, 0), (0, LANES - N_EXPERTS)))
    br_row = jnp.pad(b_r, (0, LANES - N_EXPERTS))[None]
    n_asg = n * TOP_K
    n_blocks = n_asg // MOE_BLOCK + N_EXPERTS
    assert n_asg % MOE_BLOCK == 0 and n % SC_WINDOW == 0
    n_slots = n_blocks * MOE_BLOCK
    idx, gate, rank, pstart, tab = _router(h, wr_pad, br_row, n_blocks)
    experts = jnp.arange(N_EXPERTS, dtype=I32)[:, None, None]
    group_start = jnp.sum(jnp.where(idx[None, :TOP_K] == experts, pstart[0, :N_EXPERTS, None, None], 0), axis=0)
    slot_by_choice = group_start + rank[:TOP_K]
    xb = _sc_scatter_tiles(ht.reshape(n, SUBLANES, LANES), slot_by_choice, n_slots)
    yb = _experts(xb.reshape(n_slots * SUBLANES, LANES), tab[:n_blocks, 0], tab[0, 2:3], tab[:n_blocks, 1],
                  tab[:n_blocks, 3], layer, w_up, b_up[:, :, None, :], w_dn, b_dn[:, :, None, :])
    y = _sc_gather_tiles(yb.reshape(n_slots, SUBLANES, LANES), slot_by_choice.reshape(-1))
    return _combine(h, y.reshape(n_asg * SUBLANES, LANES), gate, p_prompt, p_sample, layer, n_p, split,
                    w_pg.astype(BF16), w_pp.astype(BF16), g2[None], b2[None])


def _lane_row(v, lane0):
    return jnp.zeros((1, LANES), F32).at[0, lane0:lane0 + v.shape[0]].set(v.astype(F32))


def kernel(x_prompt, x_sample, cache_fox_k, cache_fox_v, cache_fox_logf, state_gdn, state_gdn_conv,
           cache_pool, p_prompt, p_sample, w_in_ab, b_fgate, gdn_a_log, gdn_dt_bias, gdn_conv_w,
           gdn_norm_g, w_out_ab, pool_w, pool_scale, w_out_pool, ln1_g, ln1_b, ln2_g, ln2_b,
           w_router, b_router, w_expert_up, b_expert_up, w_expert_down, b_expert_down,
           w_ple_gate, w_ple_proj):
    n_pb, seq, d = x_prompt.shape
    n_sb, dseq, _ = x_sample.shape
    past = cache_fox_k.shape[2]
    assert n_pb == 1 and dseq == CHUNK and past % dseq == 0 and seq % TQ == 0
    assert d == SUBLANES * LANES
    n_p = n_pb * seq
    n_s = n_sb * dseq
    n = n_p + n_s
    assert n_p % TM_TOK == 0 and n_s % TM_TOK == 0
    n_layers = p_prompt.shape[0]
    pp_all = p_prompt.reshape(n_layers * n_p, -1)
    ps_all = p_sample.reshape(n_layers * n_s, -1)

    def tail(h, ht, i, split):
        return _layer_tail(h, ht, pp_all, ps_all, i, n_p, split, ln2_g[i], ln2_b[i], w_router[i], b_router[i],
                           w_expert_up, b_expert_up, w_expert_down, b_expert_down, w_ple_gate[i],
                           w_ple_proj[i])

    w_in = w_in_ab[0]
    n_small = 3 * N_HEADS
    ff0 = 3 * WIDTH
    gq0 = ff0 + N_HEADS
    ga0 = gq0 + 4 * WIDTH
    w_small = jnp.concatenate([w_in[:, ff0:gq0], w_in[:, ga0:ga0 + 2 * N_HEADS],
                               jnp.zeros((d, LANES - n_small), F32)], axis=1)
    w_all = jnp.concatenate([w_in[:, :ff0], w_in[:, gq0:ga0], w_small], axis=1)
    bf_row = _lane_row(b_fgate[0], LANE_F)
    up = _proj(x_prompt.reshape(n_p, d), w_all.astype(BF16), bf_row, TM_PROJ, U_COLS, False)
    us = _proj(x_sample.reshape(n_s, d), w_all, bf_row, TM_TOK, LANES, True)

    u3 = up[None]
    us3 = us.reshape(n_sb, dseq, U_COLS)
    cq_p, ck_p = _cumsum(u3, 1, n_p, CUMSUM_TILE, COL_S // LANES)
    lf_s = jnp.concatenate(
        [jnp.pad(cache_fox_logf[0].astype(F32), ((0, 0), (0, 0), (0, LANES - N_HEADS))),
         us3[:, :, COL_S:]], axis=1)
    cq_s, ck_s = _cumsum(lf_s, n_sb, past + dseq, past + dseq, 0)

    of_p = _fox(u3, lambda b, i: (0, i, COL_Q // WIDTH), u3, lambda b, j: (0, j, COL_K // WIDTH),
                u3, lambda b, j: (0, j, COL_V // WIDTH), cq_p, lambda b, i: (0, i, 0),
                ck_p, lambda b, j: (0, 0, j), 1, n_p // TQ, TQ, TK, 0)
    k_all = jnp.concatenate([cache_fox_k[0].reshape(n_sb, past, WIDTH), us3[:, :, COL_K:COL_K + WIDTH]], axis=1)
    v_all = jnp.concatenate([cache_fox_v[0].reshape(n_sb, past, WIDTH), us3[:, :, COL_V:COL_V + WIDTH]], axis=1)
    of_s = _fox(us3, lambda b, i: (b, 0, COL_Q // WIDTH), k_all, lambda b, j: (b, 0, 0),
                v_all, lambda b, j: (b, 0, 0), cq_s, lambda b, i: (b, past // dseq, 0),
                ck_s, lambda b, j: (b, 0, 0), n_sb, 1, dseq, past + dseq, past, hi=True)

    gw = 3 * WIDTH
    conv_w = jnp.pad(gdn_conv_w[0], ((0, SUBLANES - CONV_WIDTH), (0, 0)))
    gdn_args = (conv_w, _lane_row(gdn_a_log[0], LANE_A), _lane_row(gdn_dt_bias[0], LANE_A),
                gdn_norm_g[0][None])
    n_pstep = n_p // GDN_ROWS
    ends = lambda k: (jnp.asarray((np.arange(k) == 0).astype(np.int32)),
                      jnp.asarray((np.arange(k) == k - 1).astype(np.int32)))
    og_p, st_p = _gdn(up.reshape(n_pstep, GDN_ROWS, U_COLS), False, *ends(n_pstep),
                      jnp.zeros((n_pstep,), I32), jnp.zeros((1, SUBLANES, gw), F32), gdn_args[0],
                      jnp.zeros((1, N_HEADS, HEAD_DIM, HEAD_DIM), F32), *gdn_args[1:])
    conv_past = jnp.pad(state_gdn_conv[0].astype(F32), ((0, 0), (SUBLANES - (CONV_WIDTH - 1), 0), (0, 0)))
    ones = jnp.ones((n_sb,), I32)
    og_s, st_s = _gdn(us3, True, ones, ones, jnp.arange(n_sb, dtype=I32), conv_past, gdn_args[0],
                      state_gdn[0].astype(F32), *gdn_args[1:])
    h, ht = _outproj_ln((x_prompt.reshape(n_p, d), of_p.reshape(n_p, WIDTH), og_p.reshape(n_p, WIDTH)),
                        (x_sample.reshape(n_s, d), of_s.reshape(n_s, WIDTH), og_s.reshape(n_s, WIDTH)),
                        w_out_ab[0], ln1_g[0][None], ln1_b[0][None])
    x1 = tail(h, ht, 0, False)

    pool_args = (pool_w[0].astype(BF16), pool_scale[0][None], w_out_pool[0].astype(BF16),
                 ln1_g[1][None], ln1_b[1][None])
    ratio = TM_TOK // POOL_HALO
    x1_halo = x1.reshape(n // POOL_HALO, POOL_HALO, d)
    h_pool = _pool_ln(x1, lambda i: (i, 0), x1_halo, lambda i: (jnp.maximum(i * ratio - 1, 0), 0, 0),
                      n_p // TM_TOK, TM_TOK, 0, True, *pool_args)
    cache16 = jnp.pad(cache_pool[0].astype(F32), ((0, 0), (POOL_HALO - POOL_STATE, 0), (0, 0)))
    h, ht = _pool_ln(x1, lambda i: (n_p // dseq + i, 0), cache16, lambda i: (i, 0, 0),
                     n_sb, dseq, past, False, *pool_args, into=h_pool)
    x2_p, x2_s = tail(h, ht, 1, True)

    return (x2_p.reshape(n_pb, seq, d), x2_s.reshape(n_sb, dseq, d),
            up[:, COL_K:COL_K + WIDTH].reshape(1, n_pb, seq, N_HEADS, HEAD_DIM),
            up[:, COL_V:COL_V + WIDTH].reshape(1, n_pb, seq, N_HEADS, HEAD_DIM),
            up[:, COL_S:COL_S + N_HEADS].reshape(1, n_pb, seq, N_HEADS),
            st_p.reshape(1, n_pb, N_HEADS, HEAD_DIM, HEAD_DIM),
            up[seq - (CONV_WIDTH - 1):, COL_G:COL_G + gw].reshape(1, n_pb, CONV_WIDTH - 1, gw),
            x1[n_p - POOL_STATE:n_p].reshape(1, n_pb, POOL_STATE, d),
            us[:, COL_K:COL_K + WIDTH].reshape(1, n_sb, dseq, N_HEADS, HEAD_DIM),
            us[:, COL_V:COL_V + WIDTH].reshape(1, n_sb, dseq, N_HEADS, HEAD_DIM),
            us[:, COL_S:COL_S + N_HEADS].reshape(1, n_sb, dseq, N_HEADS),
            st_s.reshape(1, n_sb, N_HEADS, HEAD_DIM, HEAD_DIM),
            us[:, COL_G:COL_G + gw].reshape(n_sb, dseq, gw)[:, dseq - (CONV_WIDTH - 1):].reshape(
                1, n_sb, CONV_WIDTH - 1, gw),
            x1[n_p:].reshape(n_sb, dseq, d)[:, dseq - POOL_STATE:].reshape(1, n_sb, POOL_STATE, d))
```

```python
import functools

import numpy as np
import jax
import jax.numpy as jnp
from jax import lax
from jax.experimental import pallas as pl
from jax.experimental.pallas import tpu as pltpu
from jax.experimental.pallas import tpu_sc as plsc

F32 = jnp.float32
BF16 = jnp.bfloat16
I32 = jnp.int32
HIGHEST = lax.Precision.HIGHEST

LANES = 128
SUBLANES = 8
VMEM_LIMIT = 56 * 1024 * 1024

HEAD_DIM = 128
N_HEADS = 4
WIDTH = N_HEADS * HEAD_DIM
CHUNK = 64
CONV_WIDTH = 4
POOL_WINDOWS = (2, 4, 8, 16)
POOL_HALO = 16
POOL_STATE = 15
N_EXPERTS = 32
TOP_K = 4
SWIGLU_LIMIT = 7.0
SWIGLU_ALPHA = 1.702
DEPTH = 2
DN_ALPHA = (2 * DEPTH) ** 0.25
LN_EPS = 1e-5
NORM_EPS = 1e-6
NEG_INF = -1e30
LOG2E = 1.4426950408889634
FOX_Q_SCALE = HEAD_DIM ** -0.5 * LOG2E

COL_Q, COL_K, COL_V = 0, WIDTH, 2 * WIDTH
COL_G = 3 * WIDTH
COL_Z = 6 * WIDTH
COL_S = 7 * WIDTH
U_COLS = COL_S + LANES
LANE_F, LANE_A, LANE_B = 0, N_HEADS, 2 * N_HEADS

TM_PROJ = 512
TM_TOK = 512
SC_WINDOW = 32
MOE_BLOCK = 512
TQ = 1024
TK = 1024
CUMSUM_TILE = 512
GDN_ROWS = 256
GDN_CHUNK = CHUNK


def _cparams(sem):
    return pltpu.CompilerParams(dimension_semantics=sem, vmem_limit_bytes=VMEM_LIMIT)


def _softplus(x):
    return jnp.maximum(x, 0.0) + jnp.log1p(jnp.exp(-jnp.abs(x)))


def _sigmoid(x):
    return 1.0 / (1.0 + jnp.exp(-x))


def _silu(x):
    return x * _sigmoid(x)


def _layer_norm(y, g, b):
    mu = jnp.mean(y, axis=-1, keepdims=True)
    yc = y - mu
    var = jnp.mean(yc * yc, axis=-1, keepdims=True)
    return yc * lax.rsqrt(var + LN_EPS) * g + b


def _dot_general(a, b, dims, hi):
    if hi:
        return lax.dot_general(a.astype(F32), b.astype(F32), (dims, ((), ())), precision=HIGHEST,
                               preferred_element_type=F32)
    return lax.dot_general(a.astype(BF16), b.astype(BF16), (dims, ((), ())), preferred_element_type=F32)


def _dot(a, b, hi=False):
    return _dot_general(a, b, ((1,), (0,)), hi)


def _dot_nt(a, b, hi=False):
    return _dot_general(a, b, ((1,), (1,)), hi)


def _dot_tn(a, b, hi=False):
    return _dot_general(a, b, ((0,), (0,)), hi)


def _spread_lanes(x, width):
    if width % LANES == 0:
        return jnp.concatenate([x] * (width // LANES), axis=1)
    return jnp.broadcast_to(x[:, 0:1], (x.shape[0], width))


def _load_token_tiles(ref, lead, n_tok):
    return jnp.concatenate([ref[(*lead, pl.ds(j, n_tok, stride=SUBLANES), slice(None))]
                            for j in range(SUBLANES)], axis=1)


def _store_token_tiles(ref, lead, x):
    for j in range(SUBLANES):
        ref[(*lead, pl.ds(j, x.shape[0], stride=SUBLANES), slice(None))] = x[:, j * LANES:(j + 1) * LANES]


def _lanes_to_rows(x, lane0):
    r = lax.broadcasted_iota(I32, (SUBLANES, LANES), 0)
    c = lax.broadcasted_iota(I32, (SUBLANES, LANES), 1)
    sel = (c == r + lane0).astype(F32)
    return lax.dot_general(sel, x, (((1,), (1,)), ((), ())), precision=HIGHEST,
                           preferred_element_type=F32)


def _proj_kernel(x_ref, w_ref, bf_ref, u_ref, *qkv_ref, hi):
    u = _dot(x_ref[...], w_ref[...], hi)
    u_ref[...] = u
    if qkv_ref:
        qkv_ref[0][:, COL_Q:COL_K] = (u[:, COL_Q:COL_K] * FOX_Q_SCALE).astype(BF16)
        qkv_ref[0][:, COL_K:COL_G] = u[:, COL_K:COL_G].astype(BF16)

    @pl.when(pl.program_id(1) == pl.num_programs(1) - 1)
    def _():
        small = u[:, u.shape[1] - LANES:]
        lane = lax.broadcasted_iota(I32, small.shape, 1)
        logf = -_softplus(-(small + bf_ref[...]))
        u_ref[:, u.shape[1] - LANES:] = jnp.where(lane < LANE_A, logf, small)


def _proj(x, w, bf_row, tm, tn, hi, with_qkv=False):
    n, d = x.shape
    m = w.shape[1]
    assert not with_qkv or tn == m
    out_specs = [pl.BlockSpec((tm, tn), lambda i, j: (i, j))]
    out_shape = [jax.ShapeDtypeStruct((n, m), F32)]
    if with_qkv:
        out_specs.append(pl.BlockSpec((tm, COL_G), lambda i, j: (i, 0)))
        out_shape.append(jax.ShapeDtypeStruct((n, COL_G), BF16))
    return pl.pallas_call(
        functools.partial(_proj_kernel, hi=hi),
        grid=(n // tm, m // tn),
        in_specs=[pl.BlockSpec((tm, d), lambda i, j: (i, 0)),
                  pl.BlockSpec((d, tn), lambda i, j: (0, j)),
                  pl.BlockSpec((1, LANES), lambda i, j: (0, 0))],
        out_specs=out_specs,
        out_shape=out_shape,
        compiler_params=_cparams(("parallel", "parallel")),
        name="in_proj",
    )(x, w, bf_row)


def _cumsum_kernel(lf_ref, crep_ref, crow_ref, carry_ref):
    @pl.when(pl.program_id(1) == 0)
    def _():
        carry_ref[...] = jnp.zeros_like(carry_ref)

    lf = lf_ref[0]
    t = lf.shape[0]
    r = lax.broadcasted_iota(I32, (t, t), 0)
    c = lax.broadcasted_iota(I32, (t, t), 1)
    tril = (c <= r).astype(F32)
    cs = jnp.dot(tril, lf, precision=HIGHEST, preferred_element_type=F32) + carry_ref[0:1, :]
    carry_ref[...] = jnp.broadcast_to(cs[t - 1:t, :], carry_ref.shape)
    c2 = cs * LOG2E
    crow_ref[0] = _lanes_to_rows(c2, LANE_F)
    for h in range(N_HEADS):
        crep_ref[0, :, h * HEAD_DIM:(h + 1) * HEAD_DIM] = jnp.broadcast_to(
            c2[:, LANE_F + h:LANE_F + h + 1], (t, HEAD_DIM))


def _cumsum(arr, n_batch, length, tl, col_block):
    return pl.pallas_call(
        _cumsum_kernel,
        grid=(n_batch, length // tl),
        in_specs=[pl.BlockSpec((1, tl, LANES), lambda b, j: (b, j, col_block))],
        out_specs=[pl.BlockSpec((1, tl, WIDTH), lambda b, j: (b, j, 0)),
                   pl.BlockSpec((1, SUBLANES, tl), lambda b, j: (b, 0, j))],
        out_shape=[jax.ShapeDtypeStruct((n_batch, length, WIDTH), F32),
                   jax.ShapeDtypeStruct((n_batch, SUBLANES, length), F32)],
        scratch_shapes=[pltpu.VMEM((SUBLANES, LANES), F32)],
        compiler_params=_cparams(("parallel", "arbitrary")),
        name="logf_cumsum",
    )(arr)


def _fox_kernel(qi_ref, kj_ref, last_ref, q_ref, k_ref, v_ref, cq_ref, ck_ref, o_ref,
                m_ref, l_ref, acc_ref, *, tq, tk, past, hi):
    s_idx = pl.program_id(1)
    qi = qi_ref[s_idx]
    kj = kj_ref[s_idx]

    @pl.when(kj == 0)
    def _():
        m_ref[...] = jnp.full_like(m_ref, NEG_INF)
        l_ref[...] = jnp.zeros_like(l_ref)
        acc_ref[...] = jnp.zeros_like(acc_ref)

    def update(masked):
        if masked:
            q_pos = past + qi * tq + lax.broadcasted_iota(I32, (tq, tk), 0)
            k_pos = kj * tk + lax.broadcasted_iota(I32, (tq, tk), 1)
            visible = k_pos <= q_pos
        for h in range(N_HEADS):
            cols = slice(h * HEAD_DIM, (h + 1) * HEAD_DIM)
            q = q_ref[0, :, cols]
            if q.dtype != BF16:
                q = q * FOX_Q_SCALE
            t = _dot_nt(q, k_ref[0, :, cols], hi) - ck_ref[0, h:h + 1, :]
            if masked:
                t = jnp.where(visible, t, NEG_INF)
            cq = cq_ref[0, :, cols]
            m_prev = m_ref[h]
            m_new = jnp.maximum(m_prev, jnp.max(t, axis=-1, keepdims=True) + cq)
            p = jnp.exp2(t - _spread_lanes(m_new - cq, tk))
            alpha = jnp.exp2(m_prev - m_new)
            l_ref[h] = alpha * l_ref[h] + jnp.sum(p, axis=-1, keepdims=True)
            acc_ref[:, cols] = alpha * acc_ref[:, cols] + _dot(p, v_ref[0, :, cols], hi)
            m_ref[h] = m_new

    crosses_diagonal = kj * tk + (tk - 1) > past + qi * tq
    pl.when(crosses_diagonal)(functools.partial(update, True))
    pl.when(jnp.logical_not(crosses_diagonal))(functools.partial(update, False))

    @pl.when(last_ref[s_idx] == 1)
    def _():
        for h in range(N_HEADS):
            cols = slice(h * HEAD_DIM, (h + 1) * HEAD_DIM)
            o_ref[0, :, cols] = acc_ref[:, cols] / l_ref[h]


def _fox_schedule(n_q, tq, tk, past):
    qi, kj, last = [], [], []
    for i in range(n_q):
        hi = (past + (i + 1) * tq - 1) // tk
        for j in range(hi + 1):
            qi.append(i)
            kj.append(j)
            last.append(1 if j == hi else 0)
    return (jnp.asarray(np.array(qi, np.int32)), jnp.asarray(np.array(kj, np.int32)),
            jnp.asarray(np.array(last, np.int32)))


def _fox(q_arr, q_map, k_arr, k_map, v_arr, v_map, cq_arr, cq_map, ck_arr, ck_map,
         n_batch, n_q, tq, tk, past, hi=False):
    qi, kj, last = _fox_schedule(n_q, tq, tk, past)
    n_steps = int(qi.shape[0])
    spec = lambda shape, fn, tab: pl.BlockSpec(shape, lambda b, s, qi_r, kj_r, la_r: fn(b, (qi_r if tab == 'q' else kj_r)[s]))
    return pl.pallas_call(
        functools.partial(_fox_kernel, tq=tq, tk=tk, past=past, hi=hi),
        grid_spec=pltpu.PrefetchScalarGridSpec(
            num_scalar_prefetch=3,
            grid=(n_batch, n_steps),
            in_specs=[spec((1, tq, WIDTH), q_map, 'q'),
                      spec((1, tk, WIDTH), k_map, 'k'),
                      spec((1, tk, WIDTH), v_map, 'k'),
                      spec((1, tq, WIDTH), cq_map, 'q'),
                      spec((1, SUBLANES, tk), ck_map, 'k')],
            out_specs=spec((1, tq, WIDTH), lambda b, i: (b, i, 0), 'q'),
            scratch_shapes=[pltpu.VMEM((N_HEADS, tq, HEAD_DIM), F32),
                            pltpu.VMEM((N_HEADS, tq, HEAD_DIM), F32),
                            pltpu.VMEM((tq, WIDTH), F32)]),
        out_shape=jax.ShapeDtypeStruct((n_batch, n_q * tq, WIDTH), F32),
        compiler_params=_cparams(("parallel", "arbitrary")),
        name="fox_attention",
    )(qi, kj, last, q_arr, k_arr, v_arr, cq_arr, ck_arr)


def _gdn_kernel(first_ref, last_ref, seq_ref,
                pre_ref, z_ref, sm_ref, cpast_ref, convw_ref, s0_ref, alog_ref, dtb_ref, ng_ref,
                o_ref, sout_ref, stage_ref, s_ref, *, rows, chunk, hi):
    dot, dot_nt, dot_tn = (functools.partial(f, hi=hi) for f in (_dot, _dot_nt, _dot_tn))
    step = pl.program_id(0)
    halo = SUBLANES
    n_chunks = rows // chunk

    @pl.when(first_ref[step] == 1)
    def _():
        stage_ref[0:halo, :] = cpast_ref[0]
        s_ref[...] = s0_ref[0]

    stage_ref[halo:halo + rows, :] = pre_ref[0]
    conv = stage_ref[halo:halo + rows, :] * convw_ref[CONV_WIDTH - 1:CONV_WIDTH, :]
    for j in range(1, CONV_WIDTH):
        conv = conv + (stage_ref[halo - j:halo - j + rows, :]
                       * convw_ref[CONV_WIDTH - 1 - j:CONV_WIDTH - j, :])
    stage_ref[0:halo, :] = stage_ref[rows:rows + halo, :]
    act = _silu(conv)

    small = sm_ref[0]
    beta_all = _sigmoid(small)
    g_all = -jnp.exp(alog_ref[...]) * _softplus(small + dtb_ref[...])
    r = lax.broadcasted_iota(I32, (rows, rows), 0)
    c = lax.broadcasted_iota(I32, (rows, rows), 1)
    same_chunk = (r // chunk) == (c // chunk)
    incl = same_chunk & (c <= r)
    strict = same_chunk & (c < r)
    eye = (c == r).astype(F32)
    gc_all = jnp.dot(incl.astype(F32), g_all, precision=HIGHEST, preferred_element_type=F32)
    gc_rows = _lanes_to_rows(gc_all, LANE_A)

    heads = range(N_HEADS)
    head_cols = [slice(h * HEAD_DIM, (h + 1) * HEAD_DIM) for h in heads]
    q, k, gc, decay, kb, vb, low = [], [], [], [], [], [], []
    for h in heads:
        qh = act[:, h * HEAD_DIM:(h + 1) * HEAD_DIM]
        kh = act[:, WIDTH + h * HEAD_DIM:WIDTH + (h + 1) * HEAD_DIM]
        vh = act[:, 2 * WIDTH + h * HEAD_DIM:2 * WIDTH + (h + 1) * HEAD_DIM]
        q.append(qh * lax.rsqrt(jnp.sum(qh * qh, axis=-1, keepdims=True) + NORM_EPS) * (HEAD_DIM ** -0.5))
        k.append(kh * lax.rsqrt(jnp.sum(kh * kh, axis=-1, keepdims=True) + NORM_EPS))
        beta = beta_all[:, LANE_B + h:LANE_B + h + 1]
        gc.append(gc_all[:, LANE_A + h:LANE_A + h + 1])
        diff = gc[h] - gc_rows[h:h + 1, :]
        decay.append(jnp.where(incl, jnp.exp(jnp.where(incl, diff, 0.0)), 0.0))
        kb.append(k[h] * beta)
        vb.append(vh * beta)
    for h in heads:
        low.append(jnp.where(strict, dot_nt(kb[h], k[h]) * decay[h], 0.0))
    inv = [eye - low[h] for h in heads]
    pw = [dot(low[h], low[h]) for h in heads]
    n_sq = chunk.bit_length() - 2
    for it in range(n_sq):
        inv = [inv[h] + dot(inv[h], pw[h]) for h in heads]
        if it + 1 < n_sq:
            pw = [dot(pw[h], pw[h]) for h in heads]
    egc = [jnp.exp(gc[h]) for h in heads]
    uw = [dot(inv[h], jnp.concatenate([vb[h], kb[h] * egc[h]], axis=1)) for h in heads]
    intra = [jnp.where(incl, dot_nt(q[h], k[h]) * decay[h], 0.0) for h in heads]
    qd = [q[h] * egc[h] for h in heads]
    g_last = [[gc[h][(g + 1) * chunk - 1:(g + 1) * chunk, :] for g in range(n_chunks)] for h in heads]
    kd = [k[h] * jnp.exp(jnp.concatenate([jnp.broadcast_to(gl, (chunk, 1)) for gl in g_last[h]], axis=0)
                         - gc[h]) for h in heads]
    state = [s_ref[h] for h in heads]
    v_new = [[] for _ in heads]
    for g in range(n_chunks):
        rs = slice(g * chunk, (g + 1) * chunk)
        for h in heads:
            v_new[h].append(uw[h][rs, :HEAD_DIM] - dot(uw[h][rs, HEAD_DIM:], state[h]))
        for h in heads:
            v_rows = jnp.concatenate(
                v_new[h] + [jnp.zeros((rows - (g + 1) * chunk, HEAD_DIM), F32)] * (g + 1 < n_chunks), axis=0)
            o = dot(qd[h][rs, :], state[h]) + dot(intra[h][rs, :], v_rows)
            state[h] = state[h] * jnp.exp(g_last[h][g]) + dot_tn(kd[h][rs, :], v_new[h][g])
            o = (o * lax.rsqrt(jnp.mean(o * o, axis=-1, keepdims=True) + NORM_EPS)
                 * ng_ref[...] * _silu(z_ref[0, rs, head_cols[h]]))
            o_ref[0, rs, head_cols[h]] = o
    for h in heads:
        s_ref[h] = state[h]

    @pl.when(last_ref[step] == 1)
    def _():
        sout_ref[0] = s_ref[...]


def _gdn(u_view, hi, first, last, seq, conv_past, conv_w, s0, alog_row, dtb_row, ng_row):
    n_steps, rows, _ = u_view.shape
    n_seq = s0.shape[0]
    gw = 3 * WIDTH
    return pl.pallas_call(
        functools.partial(_gdn_kernel, rows=rows, chunk=min(rows, GDN_CHUNK), hi=hi),
        grid_spec=pltpu.PrefetchScalarGridSpec(
            num_scalar_prefetch=3,
            grid=(n_steps,),
            in_specs=[pl.BlockSpec((1, rows, gw), lambda s, f, l, q: (s, 0, COL_G // gw)),
                      pl.BlockSpec((1, rows, WIDTH), lambda s, f, l, q: (s, 0, COL_Z // WIDTH)),
                      pl.BlockSpec((1, rows, LANES), lambda s, f, l, q: (s, 0, COL_S // LANES)),
                      pl.BlockSpec((1, SUBLANES, gw), lambda s, f, l, q: (q[s], 0, 0)),
                      pl.BlockSpec((SUBLANES, gw), lambda s, f, l, q: (0, 0)),
                      pl.BlockSpec((1, N_HEADS, HEAD_DIM, HEAD_DIM), lambda s, f, l, q: (q[s], 0, 0, 0)),
                      pl.BlockSpec((1, LANES), lambda s, f, l, q: (0, 0)),
                      pl.BlockSpec((1, LANES), lambda s, f, l, q: (0, 0)),
                      pl.BlockSpec((1, LANES), lambda s, f, l, q: (0, 0))],
            out_specs=[pl.BlockSpec((1, rows, WIDTH), lambda s, f, l, q: (s, 0, 0)),
                       pl.BlockSpec((1, N_HEADS, HEAD_DIM, HEAD_DIM), lambda s, f, l, q: (q[s], 0, 0, 0))],
            scratch_shapes=[pltpu.VMEM((rows + SUBLANES, gw), F32),
                            pltpu.VMEM((N_HEADS, HEAD_DIM, HEAD_DIM), F32)]),
        out_shape=[jax.ShapeDtypeStruct((n_steps, rows, WIDTH), F32),
                   jax.ShapeDtypeStruct((n_seq, N_HEADS, HEAD_DIM, HEAD_DIM), F32)],
        compiler_params=_cparams(("arbitrary",)),
        name="gated_deltanet",
    )(first, last, seq, u_view, u_view, u_view, conv_past, conv_w, s0, alog_row, dtb_row, ng_row)


def _outproj_ln_kernel(xp_ref, ofp_ref, ogp_ref, xs_ref, ofs_ref, ogs_ref, w_ref, wf_ref, g_ref, b_ref,
                       h_ref, ht_ref, *, hi_from):
    def run(hi, x_ref, of_ref, og_ref):
        w = wf_ref if hi else w_ref
        mix = _dot(of_ref[...], w[0:WIDTH, :], hi) + _dot(og_ref[...], w[WIDTH:2 * WIDTH, :], hi)
        h = _layer_norm(DN_ALPHA * x_ref[...] + mix, g_ref[...], b_ref[...])
        h_ref[...] = h
        _store_token_tiles(ht_ref, (), h)

    pl.when(pl.program_id(0) < hi_from)(functools.partial(run, False, xp_ref, ofp_ref, ogp_ref))
    pl.when(pl.program_id(0) >= hi_from)(functools.partial(run, True, xs_ref, ofs_ref, ogs_ref))


def _outproj_ln(prompt, sample, w, g_row, b_row):
    n_p, d = prompt[0].shape
    n = n_p + sample[0].shape[0]
    hi_from = n_p // TM_TOK
    row = lambda i: (i, 0)
    row_p = lambda i: (jnp.minimum(i, hi_from - 1), 0)
    row_s = lambda i: (jnp.maximum(i - hi_from, 0), 0)
    fixed = lambda i: (0, 0)
    group = lambda rows: [pl.BlockSpec((TM_TOK, d), rows), pl.BlockSpec((TM_TOK, WIDTH), rows),
                          pl.BlockSpec((TM_TOK, WIDTH), rows)]
    return pl.pallas_call(
        functools.partial(_outproj_ln_kernel, hi_from=hi_from),
        grid=(n // TM_TOK,),
        in_specs=group(row_p) + group(row_s) + [pl.BlockSpec((2 * WIDTH, d), fixed),
                                                pl.BlockSpec((2 * WIDTH, d), fixed),
                                                pl.BlockSpec((1, d), fixed), pl.BlockSpec((1, d), fixed)],
        out_specs=[pl.BlockSpec((TM_TOK, d), row), pl.BlockSpec((TM_TOK * SUBLANES, LANES), row)],
        out_shape=[jax.ShapeDtypeStruct((n, d), F32), jax.ShapeDtypeStruct((n * SUBLANES, LANES), F32)],
        compiler_params=_cparams(("parallel",)),
        name="out_proj_ln",
    )(*prompt, *sample, w.astype(BF16), w, g_row, b_row)


def _pool_ln_kernel(x_ref, halo_ref, pw_ref, ps_ref, w_ref, g_ref, b_ref, h_ref, ht_ref, stage_ref,
                    lvl_a_ref, lvl_b_ref, *, tm, pos0, zero_first_halo):
    i = pl.program_id(0)
    pad = SUBLANES
    n_rows = POOL_HALO + tm
    for ref in (stage_ref, lvl_a_ref, lvl_b_ref):
        ref[0:pad, :] = jnp.zeros((pad, ref.shape[1]), F32)
    stage_ref[pad:pad + POOL_HALO, :] = halo_ref[0]
    if zero_first_halo:
        @pl.when(i == 0)
        def _():
            stage_ref[pad:pad + POOL_HALO, :] = jnp.zeros((POOL_HALO, stage_ref.shape[1]), F32)
    x = x_ref[...]
    stage_ref[pad + POOL_HALO:pad + n_rows, :] = x
    gdim = x.shape[1] // len(POOL_WINDOWS)
    pos = pos0 + lax.broadcasted_iota(I32, (tm, 1), 0)
    if zero_first_halo:
        pos = pos + i * tm

    def window_sum(cols, win):
        assert win & (win - 1) == 0 and win <= POOL_HALO
        src, src_cols, span, level = stage_ref, cols, 1, 0
        while span < win:
            dst = (lvl_a_ref, lvl_b_ref)[level % 2]
            dst[pad:pad + n_rows, :] = (src[pad:pad + n_rows, src_cols]
                                        + src[pad - span:pad - span + n_rows, src_cols])
            src, src_cols, span, level = dst, slice(None), 2 * span, level + 1
        return src[pad + POOL_HALO:pad + n_rows, src_cols]

    parts = []
    for gi, win in enumerate(POOL_WINDOWS):
        cols = slice(gi * gdim, (gi + 1) * gdim)
        s = window_sum(cols, win)
        cnt = jnp.minimum(pos + 1, win).astype(F32)
        zg = s / cnt - x[:, cols]
        parts.append(_dot(zg, pw_ref[gi]))
    zg = jnp.concatenate(parts, axis=-1) * ps_ref[...]
    mix = _dot(zg, w_ref[...])
    h = _layer_norm(DN_ALPHA * x + mix, g_ref[...], b_ref[...])
    h_ref[...] = h
    _store_token_tiles(ht_ref, (), h)


def _pool_ln_into_kernel(*refs, **kw):
    _pool_ln_kernel(*refs[:7], *refs[9:], **kw)


def _pool_ln(x, x_map, halo_arr, halo_map, n_tiles, tm, pos0, zero_first_halo,
             pw_bf16, ps_row, w_bf16, g_row, b_row, into=None):
    n, d = x.shape
    gdim = d // len(POOL_WINDOWS)
    fixed = lambda i: (0, 0)
    in_specs = [pl.BlockSpec((tm, d), x_map),
                pl.BlockSpec((1, POOL_HALO, d), halo_map),
                pl.BlockSpec((len(POOL_WINDOWS), gdim, gdim), lambda i: (0, 0, 0)),
                pl.BlockSpec((1, d), fixed), pl.BlockSpec((d, d), fixed),
                pl.BlockSpec((1, d), fixed), pl.BlockSpec((1, d), fixed)]
    args = (x, halo_arr, pw_bf16, ps_row, w_bf16, g_row, b_row)
    kw = dict(tm=tm, pos0=pos0, zero_first_halo=zero_first_halo)
    return pl.pallas_call(
        functools.partial(_pool_ln_kernel if into is None else _pool_ln_into_kernel, **kw),
        grid=(n_tiles,),
        in_specs=in_specs + ([] if into is None else [pl.BlockSpec(memory_space=pl.ANY)] * 2),
        out_specs=[pl.BlockSpec((tm, d), x_map),
                   pl.BlockSpec((tm * SUBLANES, LANES), x_map)],
        out_shape=[jax.ShapeDtypeStruct((n, d), F32), jax.ShapeDtypeStruct((n * SUBLANES, LANES), F32)],
        input_output_aliases={} if into is None else {7: 0, 8: 1},
        scratch_shapes=[pltpu.VMEM((SUBLANES + POOL_HALO + tm, d), F32),
                        pltpu.VMEM((SUBLANES + POOL_HALO + tm, gdim), F32),
                        pltpu.VMEM((SUBLANES + POOL_HALO + tm, gdim), F32)],
        compiler_params=_cparams(("arbitrary",)),
        name="pool_mixer_ln",
    )(*args, *(() if into is None else into))


def _router_kernel(h_ref, wr_ref, br_ref, idx_ref, gate_ref, rank_ref, pstart_ref, tab_ref, carry_ref):
    @pl.when(pl.program_id(0) == 0)
    def _():
        carry_ref[...] = jnp.zeros_like(carry_ref)

    tm = h_ref.shape[0]
    lane = lax.broadcasted_iota(I32, (tm, LANES), 1).astype(F32)
    logits = jnp.dot(h_ref[...], wr_ref[...], precision=HIGHEST, preferred_element_type=F32)
    work = jnp.where(lane < N_EXPERTS, logits + br_ref[...], -jnp.inf)
    vals, ids = [], []
    for _ in range(TOP_K):
        m = jnp.max(work, axis=-1, keepdims=True)
        ik = jnp.min(jnp.where(work == m, lane, float(LANES)), axis=-1, keepdims=True)
        vals.append(m)
        ids.append(ik)
        work = jnp.where(lane == ik, -jnp.inf, work)
    exps = [jnp.exp(v - vals[0]) for v in vals]
    denom = exps[0]
    for e in exps[1:]:
        denom = denom + e
    multihot = jnp.zeros((tm, LANES), F32)
    idx_out = jnp.zeros((tm, LANES), F32)
    gate_out = jnp.zeros((tm, LANES), F32)
    for k in range(TOP_K):
        multihot = multihot + (lane == ids[k]).astype(F32)
        idx_out = jnp.where(lane == k, ids[k], idx_out)
        gate_out = jnp.where(lane == k, exps[k] / denom, gate_out)
    r = lax.broadcasted_iota(I32, (tm, tm), 0)
    c = lax.broadcasted_iota(I32, (tm, tm), 1)
    before = _dot((c < r).astype(F32), multihot) + carry_ref[0:1, :]
    rank_out = jnp.zeros((tm, LANES), F32)
    for k in range(TOP_K):
        rk = jnp.sum(jnp.where(lane == ids[k], before, 0.0), axis=-1, keepdims=True)
        rank_out = jnp.where(lane == k, rk, rank_out)
    idx_ref[...] = jnp.transpose(idx_out)[:SUBLANES].astype(I32)
    rank_ref[...] = jnp.transpose(rank_out)[:SUBLANES].astype(I32)
    gate_ref[...] = gate_out
    total = carry_ref[0:1, :] + jnp.sum(multihot, axis=0, keepdims=True)
    carry_ref[...] = jnp.broadcast_to(total, carry_ref.shape)

    @pl.when(pl.program_id(0) == pl.num_programs(0) - 1)
    def _():
        n_rows = tab_ref.shape[0]
        padded = jnp.floor((total + (MOE_BLOCK - 1)) / MOE_BLOCK) * MOE_BLOCK
        rr = lax.broadcasted_iota(I32, (LANES, LANES), 0)
        cc = lax.broadcasted_iota(I32, (LANES, LANES), 1)
        pend = jnp.dot(jnp.broadcast_to(padded, (SUBLANES, LANES)), (rr <= cc).astype(F32),
                       precision=HIGHEST, preferred_element_type=F32)[0:1, :]
        pstart = pend - padded
        n_used = jnp.max(pend, axis=-1, keepdims=True) / MOE_BLOCK
        elane = lax.broadcasted_iota(I32, (n_rows, LANES), 1)
        blk = jnp.minimum(lax.broadcasted_iota(I32, (n_rows, 1), 0).astype(F32), n_used - 1.0) * MOE_BLOCK
        ends_before = jnp.where((elane < N_EXPERTS) & (pend <= blk), 1.0, 0.0)
        block_e = jnp.minimum(jnp.sum(ends_before, axis=-1, keepdims=True), N_EXPERTS - 1.0)
        mine = elane.astype(F32) == block_e
        filled = jnp.sum(jnp.where(mine, total - (blk - pstart), 0.0), axis=-1, keepdims=True)
        block_valid = jnp.clip(filled, 0.0, float(MOE_BLOCK))
        group_end = jnp.sum(jnp.where(mine, pend, 0.0), axis=-1, keepdims=True)
        next_e = jnp.sum(jnp.where((elane < N_EXPERTS) & (pend <= group_end), 1.0, 0.0), axis=-1, keepdims=True)
        tab = jnp.where(elane == 0, block_e, jnp.where(elane == 1, block_valid,
                                                       jnp.where(elane == 2, n_used,
                                                                 jnp.where(elane == 3, next_e, 0.0))))
        tab_ref[...] = tab.astype(I32)
        pstart_ref[...] = jnp.broadcast_to(pstart, pstart_ref.shape).astype(I32)


def _router(h, wr_pad, br_row, n_blocks):
    n, d = h.shape
    row = lambda i: (i, 0)
    col = lambda i: (0, i)
    fixed = lambda i: (0, 0)
    tab_rows = -(-n_blocks // SUBLANES) * SUBLANES
    return pl.pallas_call(
        _router_kernel,
        grid=(n // TM_TOK,),
        in_specs=[pl.BlockSpec((TM_TOK, d), row), pl.BlockSpec((d, LANES), fixed),
                  pl.BlockSpec((1, LANES), fixed)],
        out_specs=[pl.BlockSpec((SUBLANES, TM_TOK), col), pl.BlockSpec((TM_TOK, LANES), row),
                   pl.BlockSpec((SUBLANES, TM_TOK), col), pl.BlockSpec((SUBLANES, LANES), fixed),
                   pl.BlockSpec((tab_rows, LANES), fixed)],
        out_shape=[jax.ShapeDtypeStruct((SUBLANES, n), I32), jax.ShapeDtypeStruct((n, LANES), F32),
                   jax.ShapeDtypeStruct((SUBLANES, n), I32), jax.ShapeDtypeStruct((SUBLANES, LANES), I32),
                   jax.ShapeDtypeStruct((tab_rows, LANES), I32)],
        scratch_shapes=[pltpu.VMEM((SUBLANES, LANES), F32)],
        compiler_params=_cparams(("arbitrary",)),
        name="moe_router",
    )(h, wr_pad, br_row)


def _expert_kernel(be_ref, nu_ref, valid_ref, next_ref, xb_ref, wup_ref, bup_ref, wdn_ref, bdn_ref, yb_ref,
                   wup_f32_ref, wdn_f32_ref, wup_bf_ref, wdn_bf_ref, wsem, *, layer):
    b = pl.program_id(0)

    def fetch(e):
        return (pltpu.make_async_copy(wup_ref.at[layer, e], wup_f32_ref, wsem.at[0]),
                pltpu.make_async_copy(wdn_ref.at[layer, e], wdn_f32_ref, wsem.at[1]))

    @pl.when(b == 0)
    def _():
        for copy in fetch(be_ref[0]):
            copy.start()

    @pl.when(b < nu_ref[0])
    def _():
        @pl.when((b == 0) | (be_ref[b] != be_ref[jnp.maximum(b - 1, 0)]))
        def _():
            for copy in fetch(be_ref[b]):
                copy.wait()
            wup_bf_ref[...] = wup_f32_ref[...].astype(BF16)
            wdn_bf_ref[...] = wdn_f32_ref[...].astype(BF16)

            @pl.when(next_ref[b] < N_EXPERTS)
            def _():
                for copy in fetch(next_ref[b]):
                    copy.start()

        d_exp = wdn_f32_ref.shape[0]

        def run(n_rows):
            x = _load_token_tiles(xb_ref, (), n_rows)
            row = lax.broadcasted_iota(I32, (n_rows, 1), 0)
            x = jnp.where(row < valid_ref[b], x, 0.0)
            hu = jnp.dot(x.astype(BF16), wup_bf_ref[...], preferred_element_type=F32) + bup_ref[0, 0]
            glu = jnp.minimum(hu[:, :d_exp], SWIGLU_LIMIT)
            lin = jnp.clip(hu[:, d_exp:], -SWIGLU_LIMIT, SWIGLU_LIMIT)
            a = glu * _sigmoid(SWIGLU_ALPHA * glu) * (lin + 1.0)
            y = jnp.dot(a.astype(BF16), wdn_bf_ref[...], preferred_element_type=F32) + bdn_ref[0, 0]
            if n_rows < MOE_BLOCK:
                y = jnp.concatenate([y, jnp.zeros((MOE_BLOCK - n_rows, y.shape[1]), F32)], axis=0)
            _store_token_tiles(yb_ref, (), y)

        half = MOE_BLOCK // 2
        pl.when(valid_ref[b] > half)(functools.partial(run, MOE_BLOCK))
        pl.when(valid_ref[b] <= half)(functools.partial(run, half))

    @pl.when(b >= nu_ref[0])
    def _():
        yb_ref[...] = jnp.zeros_like(yb_ref)


def _experts(xb, block_e, n_used, block_valid, next_e, layer, w_up, b_up, w_dn, b_dn):
    d = SUBLANES * LANES
    rows = MOE_BLOCK * SUBLANES
    n_blocks = xb.shape[0] // rows
    d_up = w_up.shape[3]
    d_exp = w_dn.shape[2]
    bsel = lambda b, be, nu, va, ne: (layer, be[b], 0, 0)
    return pl.pallas_call(
        functools.partial(_expert_kernel, layer=layer),
        grid_spec=pltpu.PrefetchScalarGridSpec(
            num_scalar_prefetch=4,
            grid=(n_blocks,),
            in_specs=[pl.BlockSpec((rows, LANES), lambda b, be, nu, va, ne: (jnp.minimum(b, nu[0] - 1), 0)),
                      pl.BlockSpec(memory_space=pl.ANY),
                      pl.BlockSpec((1, 1, 1, d_up), bsel),
                      pl.BlockSpec(memory_space=pl.ANY),
                      pl.BlockSpec((1, 1, 1, d), bsel)],
            out_specs=pl.BlockSpec((rows, LANES), lambda b, be, nu, va, ne: (b, 0)),
            scratch_shapes=[pltpu.VMEM((d, d_up), F32), pltpu.VMEM((d_exp, d), F32),
                            pltpu.VMEM((d, d_up), BF16), pltpu.VMEM((d_exp, d), BF16),
                            pltpu.SemaphoreType.DMA((2,))]),
        out_shape=jax.ShapeDtypeStruct(xb.shape, F32),
        compiler_params=_cparams(("arbitrary",)),
        name="moe_experts",
    )(block_e, n_used, block_valid, next_e, xb, w_up, b_up, w_dn, b_dn)


def _sc_gather_tiles(table, idx):
    m = idx.shape[0]
    mesh = plsc.VectorSubcoreMesh(core_axis_name="core", subcore_axis_name="subcore")

    @functools.partial(pl.kernel, out_type=jax.ShapeDtypeStruct((m, SUBLANES, LANES), table.dtype), mesh=mesh)
    def gather(table_hbm, idx_hbm, out_hbm):
        def window(idx_vmem, out_vmem):
            pltpu.sync_copy(table_hbm.at[idx_vmem.at[0, pl.ds(0, SC_WINDOW)]], out_vmem)

        pltpu.emit_pipeline(
            window,
            grid=(m // SC_WINDOW,),
            in_specs=[pl.BlockSpec((1, LANES), lambda i: (i, 0))],
            out_specs=[pl.BlockSpec((SC_WINDOW, SUBLANES, LANES), lambda i: (i, 0, 0))],
            core_axis_name=("core", "subcore"),
            dimension_semantics=(pltpu.PARALLEL,),
        )(idx_hbm, out_hbm)

    idx_rows = jnp.pad(idx.reshape(m // SC_WINDOW, SC_WINDOW), ((0, 0), (0, LANES - SC_WINDOW)))
    return gather(table, idx_rows)


def _sc_scatter_tiles(tiles, idx_by_choice, n_out):
    n = tiles.shape[0]
    mesh = plsc.VectorSubcoreMesh(core_axis_name="core", subcore_axis_name="subcore")

    @functools.partial(pl.kernel, out_type=jax.ShapeDtypeStruct((n_out, SUBLANES, LANES), tiles.dtype),
                       mesh=mesh)
    def scatter(tiles_hbm, *refs):
        idx_hbm, out_hbm = refs[:TOP_K], refs[TOP_K]

        def window(tiles_vmem, *idx_vmem):
            for k in range(TOP_K):
                pltpu.sync_copy(tiles_vmem, out_hbm.at[idx_vmem[k].at[0, pl.ds(0, SC_WINDOW)]])

        pltpu.emit_pipeline(
            window,
            grid=(n // SC_WINDOW,),
            in_specs=[pl.BlockSpec((SC_WINDOW, SUBLANES, LANES), lambda i: (i, 0, 0))]
                     + [pl.BlockSpec((1, LANES), lambda i: (i, 0))] * TOP_K,
            out_specs=[],
            core_axis_name=("core", "subcore"),
            dimension_semantics=(pltpu.PARALLEL,),
        )(tiles_hbm, *idx_hbm)

    idx_rows = jnp.pad(idx_by_choice.reshape(TOP_K, n // SC_WINDOW, SC_WINDOW),
                       ((0, 0), (0, 0), (0, LANES - SC_WINDOW)))
    return scatter(tiles, *[idx_rows[k] for k in range(TOP_K)])


def _combine_kernel(h_ref, y0_ref, y1_ref, y2_ref, y3_ref, gate_ref, pp_ref, ps_ref, wpg_ref, wpp_ref,
                    g_ref, b_ref, *outs, n_ptiles):
    outp_ref, outs_ref = outs[0], outs[-1]
    tm = h_ref.shape[0]
    gate = gate_ref[...]
    moe = _load_token_tiles(y0_ref, (), tm) * gate[:, 0:1]
    for k, y_ref in enumerate((y1_ref, y2_ref, y3_ref), start=1):
        moe = moe + _load_token_tiles(y_ref, (), tm) * gate[:, k:k + 1]
    h2 = _layer_norm(DN_ALPHA * h_ref[...] + moe, g_ref[...], b_ref[...])
    embed_gate = _sigmoid(_dot(h2, wpg_ref[...]))

    def finish(p_ref, out_ref):
        out_ref[...] = h2 + embed_gate * _dot(p_ref[...], wpp_ref[...])

    is_prompt = pl.program_id(0) < n_ptiles
    pl.when(is_prompt)(functools.partial(finish, pp_ref, outp_ref))
    pl.when(jnp.logical_not(is_prompt))(functools.partial(finish, ps_ref, outs_ref))


def _combine(h, y, gate, p_prompt, p_sample, layer, n_p, split, wpg_bf16, wpp_bf16, g_row, b_row):
    n, d = h.shape
    e = p_prompt.shape[1]
    n_tiles = n // TM_TOK
    n_ptiles = n_p // TM_TOK
    n_stiles = (n - n_p) // TM_TOK
    row = lambda i: (i, 0)
    row_p = lambda i: (jnp.minimum(i, n_ptiles - 1), 0)
    row_s = lambda i: (jnp.maximum(i - n_ptiles, 0), 0)
    fixed = lambda i: (0, 0)
    choice = lambda k: pl.BlockSpec((TM_TOK * SUBLANES, LANES), lambda i: (k * n_tiles + i, 0))
    if split:
        out_specs = [pl.BlockSpec((TM_TOK, d), row_p), pl.BlockSpec((TM_TOK, d), row_s)]
        out_shape = [jax.ShapeDtypeStruct((n_p, d), F32), jax.ShapeDtypeStruct((n - n_p, d), F32)]
    else:
        out_specs = pl.BlockSpec((TM_TOK, d), row)
        out_shape = jax.ShapeDtypeStruct((n, d), F32)
    return pl.pallas_call(
        functools.partial(_combine_kernel, n_ptiles=n_ptiles),
        grid=(n_tiles,),
        in_specs=[pl.BlockSpec((TM_TOK, d), row)] + [choice(k) for k in range(TOP_K)]
                 + [pl.BlockSpec((TM_TOK, LANES), row),
                    pl.BlockSpec((TM_TOK, e), lambda i: (layer * n_ptiles + row_p(i)[0], 0)),
                    pl.BlockSpec((TM_TOK, e), lambda i: (layer * n_stiles + row_s(i)[0], 0)),
                    pl.BlockSpec((d, d), fixed), pl.BlockSpec((e, d), fixed), pl.BlockSpec((1, d), fixed),
                    pl.BlockSpec((1, d), fixed)],
        out_specs=out_specs,
        out_shape=out_shape,
        compiler_params=_cparams(("arbitrary",)),
        name="moe_combine_ln_embed",
    )(h, y, y, y, y, gate, p_prompt, p_sample, wpg_bf16, wpp_bf16, g_row, b_row)


def _layer_tail(h, ht, p_prompt, p_sample, layer, n_p, split, g2, b2, w_r, b_r, w_up, b_up, w_dn, b_dn,
                w_pg, w_pp):
    n, d = h.shape
    wr_pad = jnp.pad(w_r, ((0, 0), (0, LANES - N_EXPERTS)))
    br_row = jnp.pad(b_r, (0, LANES - N_EXPERTS))[None]
    n_asg = n * TOP_K
    n_blocks = n_asg // MOE_BLOCK + N_EXPERTS
    assert n_asg % MOE_BLOCK == 0 and n % SC_WINDOW == 0
    n_slots = n_blocks * MOE_BLOCK
    idx, gate, rank, pstart, tab = _router(h, wr_pad, br_row, n_blocks)
    experts = jnp.arange(N_EXPERTS, dtype=I32)[:, None, None]
    group_start = jnp.sum(jnp.where(idx[None, :TOP_K] == experts, pstart[0, :N_EXPERTS, None, None], 0), axis=0)
    slot_by_choice = group_start + rank[:TOP_K]
    xb = _sc_scatter_tiles(ht.reshape(n, SUBLANES, LANES), slot_by_choice, n_slots)
    yb = _experts(xb.reshape(n_slots * SUBLANES, LANES), tab[:n_blocks, 0], tab[0, 2:3], tab[:n_blocks, 1],
                  tab[:n_blocks, 3], layer, w_up, b_up[:, :, None, :], w_dn, b_dn[:, :, None, :])
    y = _sc_gather_tiles(yb.reshape(n_slots, SUBLANES, LANES), slot_by_choice.reshape(-1))
    return _combine(h, y.reshape(n_asg * SUBLANES, LANES), gate, p_prompt, p_sample, layer, n_p, split,
                    w_pg.astype(BF16), w_pp.astype(BF16), g2[None], b2[None])


def _lane_row(v, lane0):
    return jnp.zeros((1, LANES), F32).at[0, lane0:lane0 + v.shape[0]].set(v.astype(F32))


def kernel(x_prompt, x_sample, cache_fox_k, cache_fox_v, cache_fox_logf, state_gdn, state_gdn_conv,
           cache_pool, p_prompt, p_sample, w_in_ab, b_fgate, gdn_a_log, gdn_dt_bias, gdn_conv_w,
           gdn_norm_g, w_out_ab, pool_w, pool_scale, w_out_pool, ln1_g, ln1_b, ln2_g, ln2_b,
           w_router, b_router, w_expert_up, b_expert_up, w_expert_down, b_expert_down,
           w_ple_gate, w_ple_proj):
    n_pb, seq, d = x_prompt.shape
    n_sb, dseq, _ = x_sample.shape
    past = cache_fox_k.shape[2]
    assert n_pb == 1 and dseq == CHUNK and past % dseq == 0 and seq % TQ == 0
    assert d == SUBLANES * LANES
    n_p = n_pb * seq
    n_s = n_sb * dseq
    n = n_p + n_s
    assert n_p % TM_TOK == 0 and n_s % TM_TOK == 0
    n_layers = p_prompt.shape[0]
    pp_all = p_prompt.reshape(n_layers * n_p, -1)
    ps_all = p_sample.reshape(n_layers * n_s, -1)

    def tail(h, ht, i, split):
        return _layer_tail(h, ht, pp_all, ps_all, i, n_p, split, ln2_g[i], ln2_b[i], w_router[i], b_router[i],
                           w_expert_up, b_expert_up, w_expert_down, b_expert_down, w_ple_gate[i],
                           w_ple_proj[i])

    w_in = w_in_ab[0]
    n_small = 3 * N_HEADS
    ff0 = 3 * WIDTH
    gq0 = ff0 + N_HEADS
    ga0 = gq0 + 4 * WIDTH
    w_small = jnp.concatenate([w_in[:, ff0:gq0], w_in[:, ga0:ga0 + 2 * N_HEADS],
                               jnp.zeros((d, LANES - n_small), F32)], axis=1)
    w_all = jnp.concatenate([w_in[:, :ff0], w_in[:, gq0:ga0], w_small], axis=1)
    bf_row = _lane_row(b_fgate[0], LANE_F)
    up, qkv_p = _proj(x_prompt.reshape(n_p, d), w_all.astype(BF16), bf_row, TM_PROJ, U_COLS, False,
                      with_qkv=True)
    (us,) = _proj(x_sample.reshape(n_s, d), w_all, bf_row, TM_TOK, LANES, True)
    qkv3 = qkv_p[None]

    u3 = up[None]
    us3 = us.reshape(n_sb, dseq, U_COLS)
    cq_p, ck_p = _cumsum(u3, 1, n_p, CUMSUM_TILE, COL_S // LANES)
    lf_s = jnp.concatenate(
        [jnp.pad(cache_fox_logf[0].astype(F32), ((0, 0), (0, 0), (0, LANES - N_HEADS))),
         us3[:, :, COL_S:]], axis=1)
    cq_s, ck_s = _cumsum(lf_s, n_sb, past + dseq, past + dseq, 0)

    of_p = _fox(qkv3, lambda b, i: (0, i, COL_Q // WIDTH), qkv3, lambda b, j: (0, j, COL_K // WIDTH),
                qkv3, lambda b, j: (0, j, COL_V // WIDTH), cq_p, lambda b, i: (0, i, 0),
                ck_p, lambda b, j: (0, 0, j), 1, n_p // TQ, TQ, TK, 0)
    k_all = jnp.concatenate([cache_fox_k[0].reshape(n_sb, past, WIDTH), us3[:, :, COL_K:COL_K + WIDTH]], axis=1)
    v_all = jnp.concatenate([cache_fox_v[0].reshape(n_sb, past, WIDTH), us3[:, :, COL_V:COL_V + WIDTH]], axis=1)
    of_s = _fox(us3, lambda b, i: (b, 0, COL_Q // WIDTH), k_all, lambda b, j: (b, 0, 0),
                v_all, lambda b, j: (b, 0, 0), cq_s, lambda b, i: (b, past // dseq, 0),
                ck_s, lambda b, j: (b, 0, 0), n_sb, 1, dseq, past + dseq, past, hi=True)

    gw = 3 * WIDTH
    conv_w = jnp.pad(gdn_conv_w[0], ((0, SUBLANES - CONV_WIDTH), (0, 0)))
    gdn_args = (conv_w, _lane_row(gdn_a_log[0], LANE_A), _lane_row(gdn_dt_bias[0], LANE_A),
                gdn_norm_g[0][None])
    n_pstep = n_p // GDN_ROWS
    ends = lambda k: (jnp.asarray((np.arange(k) == 0).astype(np.int32)),
                      jnp.asarray((np.arange(k) == k - 1).astype(np.int32)))
    og_p, st_p = _gdn(up.reshape(n_pstep, GDN_ROWS, U_COLS), False, *ends(n_pstep),
                      jnp.zeros((n_pstep,), I32), jnp.zeros((1, SUBLANES, gw), F32), gdn_args[0],
                      jnp.zeros((1, N_HEADS, HEAD_DIM, HEAD_DIM), F32), *gdn_args[1:])
    conv_past = jnp.pad(state_gdn_conv[0].astype(F32), ((0, 0), (SUBLANES - (CONV_WIDTH - 1), 0), (0, 0)))
    ones = jnp.ones((n_sb,), I32)
    og_s, st_s = _gdn(us3, True, ones, ones, jnp.arange(n_sb, dtype=I32), conv_past, gdn_args[0],
                      state_gdn[0].astype(F32), *gdn_args[1:])
    h, ht = _outproj_ln((x_prompt.reshape(n_p, d), of_p.reshape(n_p, WIDTH), og_p.reshape(n_p, WIDTH)),
                        (x_sample.reshape(n_s, d), of_s.reshape(n_s, WIDTH), og_s.reshape(n_s, WIDTH)),
                        w_out_ab[0], ln1_g[0][None], ln1_b[0][None])
    x1 = tail(h, ht, 0, False)

    pool_args = (pool_w[0].astype(BF16), pool_scale[0][None], w_out_pool[0].astype(BF16),
                 ln1_g[1][None], ln1_b[1][None])
    ratio = TM_TOK // POOL_HALO
    x1_halo = x1.reshape(n // POOL_HALO, POOL_HALO, d)
    h_pool = _pool_ln(x1, lambda i: (i, 0), x1_halo, lambda i: (jnp.maximum(i * ratio - 1, 0), 0, 0),
                      n_p // TM_TOK, TM_TOK, 0, True, *pool_args)
    cache16 = jnp.pad(cache_pool[0].astype(F32), ((0, 0), (POOL_HALO - POOL_STATE, 0), (0, 0)))
    h, ht = _pool_ln(x1, lambda i: (n_p // dseq + i, 0), cache16, lambda i: (i, 0, 0),
                     n_sb, dseq, past, False, *pool_args, into=h_pool)
    x2_p, x2_s = tail(h, ht, 1, True)

    return (x2_p.reshape(n_pb, seq, d), x2_s.reshape(n_sb, dseq, d),
            up[:, COL_K:COL_K + WIDTH].reshape(1, n_pb, seq, N_HEADS, HEAD_DIM),
            up[:, COL_V:COL_V + WIDTH].reshape(1, n_pb, seq, N_HEADS, HEAD_DIM),
            up[:, COL_S:COL_S + N_HEADS].reshape(1, n_pb, seq, N_HEADS),
            st_p.reshape(1, n_pb, N_HEADS, HEAD_DIM, HEAD_DIM),
            up[seq - (CONV_WIDTH - 1):, COL_G:COL_G + gw].reshape(1, n_pb, CONV_WIDTH - 1, gw),
            x1[n_p - POOL_STATE:n_p].reshape(1, n_pb, POOL_STATE, d),
            us[:, COL_K:COL_K + WIDTH].reshape(1, n_sb, dseq, N_HEADS, HEAD_DIM),
            us[:, COL_V:COL_V + WIDTH].reshape(1, n_sb, dseq, N_HEADS, HEAD_DIM),
            us[:, COL_S:COL_S + N_HEADS].reshape(1, n_sb, dseq, N_HEADS),
            st_s.reshape(1, n_sb, N_HEADS, HEAD_DIM, HEAD_DIM),
            us[:, COL_G:COL_G + gw].reshape(n_sb, dseq, gw)[:, dseq - (CONV_WIDTH - 1):].reshape(
                1, n_sb, CONV_WIDTH - 1, gw),
            x1[n_p:].reshape(n_sb, dseq, d)[:, dseq - POOL_STATE:].reshape(1, n_sb, POOL_STATE, d))
```

```python
import functools

import numpy as np
import jax
import jax.numpy as jnp
from jax import lax
from jax.experimental import pallas as pl
from jax.experimental.pallas import tpu as pltpu
from jax.experimental.pallas import tpu_sc as plsc

F32 = jnp.float32
BF16 = jnp.bfloat16
I32 = jnp.int32
HIGHEST = lax.Precision.HIGHEST

LANES = 128
SUBLANES = 8
VMEM_LIMIT = 56 * 1024 * 1024

HEAD_DIM = 128
N_HEADS = 4
WIDTH = N_HEADS * HEAD_DIM
CHUNK = 64
CONV_WIDTH = 4
POOL_WINDOWS = (2, 4, 8, 16)
POOL_HALO = 16
POOL_STATE = 15
N_EXPERTS = 32
TOP_K = 4
SWIGLU_LIMIT = 7.0
SWIGLU_ALPHA = 1.702
DEPTH = 2
DN_ALPHA = (2 * DEPTH) ** 0.25
LN_EPS = 1e-5
NORM_EPS = 1e-6
NEG_INF = -1e30
LOG2E = 1.4426950408889634

COL_Q, COL_K, COL_V = 0, WIDTH, 2 * WIDTH
COL_G = 3 * WIDTH
COL_Z = 6 * WIDTH
COL_S = 7 * WIDTH
U_COLS = COL_S + LANES
LANE_F, LANE_A, LANE_B = 0, N_HEADS, 2 * N_HEADS

TM_PROJ = 512
TM_TOK = 512
SC_WINDOW = 32
MOE_BLOCK = 512
TQ = 1024
TK = 1024
CUMSUM_TILE = 512
GDN_ROWS = 256
GDN_CHUNK = CHUNK


def _cparams(sem):
    return pltpu.CompilerParams(dimension_semantics=sem, vmem_limit_bytes=VMEM_LIMIT)


def _softplus(x):
    return jnp.maximum(x, 0.0) + jnp.log1p(jnp.exp(-jnp.abs(x)))


def _sigmoid(x):
    return 1.0 / (1.0 + jnp.exp(-x))


def _silu(x):
    return x * _sigmoid(x)


def _layer_norm(y, g, b):
    mu = jnp.mean(y, axis=-1, keepdims=True)
    yc = y - mu
    var = jnp.mean(yc * yc, axis=-1, keepdims=True)
    return yc * lax.rsqrt(var + LN_EPS) * g + b


def _dot_general(a, b, dims, hi):
    if hi:
        return lax.dot_general(a.astype(F32), b.astype(F32), (dims, ((), ())), precision=HIGHEST,
                               preferred_element_type=F32)
    return lax.dot_general(a.astype(BF16), b.astype(BF16), (dims, ((), ())), preferred_element_type=F32)


def _dot(a, b, hi=False):
    return _dot_general(a, b, ((1,), (0,)), hi)


def _dot_nt(a, b, hi=False):
    return _dot_general(a, b, ((1,), (1,)), hi)


def _dot_tn(a, b, hi=False):
    return _dot_general(a, b, ((0,), (0,)), hi)


def _spread_lanes(x, width):
    if width % LANES == 0:
        return jnp.concatenate([x] * (width // LANES), axis=1)
    return jnp.broadcast_to(x[:, 0:1], (x.shape[0], width))


def _load_token_tiles(ref, lead, n_tok):
    return jnp.concatenate([ref[(*lead, pl.ds(j, n_tok, stride=SUBLANES), slice(None))]
                            for j in range(SUBLANES)], axis=1)


def _store_token_tiles(ref, lead, x):
    for j in range(SUBLANES):
        ref[(*lead, pl.ds(j, x.shape[0], stride=SUBLANES), slice(None))] = x[:, j * LANES:(j + 1) * LANES]


def _lanes_to_rows(x, lane0):
    r = lax.broadcasted_iota(I32, (SUBLANES, LANES), 0)
    c = lax.broadcasted_iota(I32, (SUBLANES, LANES), 1)
    sel = (c == r + lane0).astype(F32)
    return lax.dot_general(sel, x, (((1,), (1,)), ((), ())), precision=HIGHEST,
                           preferred_element_type=F32)


def _proj_kernel(x_ref, w_ref, bf_ref, u_ref, *, hi):
    u = _dot(x_ref[...], w_ref[...], hi)
    u_ref[...] = u

    @pl.when(pl.program_id(1) == pl.num_programs(1) - 1)
    def _():
        small = u[:, u.shape[1] - LANES:]
        lane = lax.broadcasted_iota(I32, small.shape, 1)
        logf = -_softplus(-(small + bf_ref[...]))
        u_ref[:, u.shape[1] - LANES:] = jnp.where(lane < LANE_A, logf, small)


def _proj(x, w, bf_row, tm, tn, hi):
    n, d = x.shape
    m = w.shape[1]
    return pl.pallas_call(
        functools.partial(_proj_kernel, hi=hi),
        grid=(n // tm, m // tn),
        in_specs=[pl.BlockSpec((tm, d), lambda i, j: (i, 0)),
                  pl.BlockSpec((d, tn), lambda i, j: (0, j)),
                  pl.BlockSpec((1, LANES), lambda i, j: (0, 0))],
        out_specs=pl.BlockSpec((tm, tn), lambda i, j: (i, j)),
        out_shape=jax.ShapeDtypeStruct((n, m), F32),
        compiler_params=_cparams(("parallel", "parallel")),
        name="in_proj",
    )(x, w, bf_row)


def _cumsum_kernel(lf_ref, crep_ref, crow_ref, carry_ref):
    @pl.when(pl.program_id(1) == 0)
    def _():
        carry_ref[...] = jnp.zeros_like(carry_ref)

    lf = lf_ref[0]
    t = lf.shape[0]
    r = lax.broadcasted_iota(I32, (t, t), 0)
    c = lax.broadcasted_iota(I32, (t, t), 1)
    tril = (c <= r).astype(F32)
    cs = jnp.dot(tril, lf, precision=HIGHEST, preferred_element_type=F32) + carry_ref[0:1, :]
    carry_ref[...] = jnp.broadcast_to(cs[t - 1:t, :], carry_ref.shape)
    c2 = cs * LOG2E
    crow_ref[0] = _lanes_to_rows(c2, LANE_F)
    for h in range(N_HEADS):
        crep_ref[0, :, h * HEAD_DIM:(h + 1) * HEAD_DIM] = jnp.broadcast_to(
            c2[:, LANE_F + h:LANE_F + h + 1], (t, HEAD_DIM))


def _cumsum(arr, n_batch, length, tl, col_block):
    return pl.pallas_call(
        _cumsum_kernel,
        grid=(n_batch, length // tl),
        in_specs=[pl.BlockSpec((1, tl, LANES), lambda b, j: (b, j, col_block))],
        out_specs=[pl.BlockSpec((1, tl, WIDTH), lambda b, j: (b, j, 0)),
                   pl.BlockSpec((1, SUBLANES, tl), lambda b, j: (b, 0, j))],
        out_shape=[jax.ShapeDtypeStruct((n_batch, length, WIDTH), F32),
                   jax.ShapeDtypeStruct((n_batch, SUBLANES, length), F32)],
        scratch_shapes=[pltpu.VMEM((SUBLANES, LANES), F32)],
        compiler_params=_cparams(("parallel", "arbitrary")),
        name="logf_cumsum",
    )(arr)


def _fox_kernel(qi_ref, kj_ref, last_ref, q_ref, k_ref, v_ref, cq_ref, ck_ref, o_ref,
                m_ref, l_ref, acc_ref, *, tq, tk, past, hi):
    s_idx = pl.program_id(1)
    qi = qi_ref[s_idx]
    kj = kj_ref[s_idx]

    @pl.when(kj == 0)
    def _():
        m_ref[...] = jnp.full_like(m_ref, NEG_INF)
        l_ref[...] = jnp.zeros_like(l_ref)
        acc_ref[...] = jnp.zeros_like(acc_ref)

    def update(masked):
        if masked:
            q_pos = past + qi * tq + lax.broadcasted_iota(I32, (tq, tk), 0)
            k_pos = kj * tk + lax.broadcasted_iota(I32, (tq, tk), 1)
            visible = k_pos <= q_pos
        for h in range(N_HEADS):
            cols = slice(h * HEAD_DIM, (h + 1) * HEAD_DIM)
            q = q_ref[0, :, cols] * (HEAD_DIM ** -0.5 * LOG2E)
            t = _dot_nt(q, k_ref[0, :, cols], hi) - ck_ref[0, h:h + 1, :]
            if masked:
                t = jnp.where(visible, t, NEG_INF)
            cq = cq_ref[0, :, cols]
            m_prev = m_ref[h]
            m_new = jnp.maximum(m_prev, jnp.max(t, axis=-1, keepdims=True) + cq)
            p = jnp.exp2(t - _spread_lanes(m_new - cq, tk))
            alpha = jnp.exp2(m_prev - m_new)
            l_ref[h] = alpha * l_ref[h] + jnp.sum(p, axis=-1, keepdims=True)
            acc_ref[:, cols] = alpha * acc_ref[:, cols] + _dot(p, v_ref[0, :, cols], hi)
            m_ref[h] = m_new

    crosses_diagonal = kj * tk + (tk - 1) > past + qi * tq
    pl.when(crosses_diagonal)(functools.partial(update, True))
    pl.when(jnp.logical_not(crosses_diagonal))(functools.partial(update, False))

    @pl.when(last_ref[s_idx] == 1)
    def _():
        for h in range(N_HEADS):
            cols = slice(h * HEAD_DIM, (h + 1) * HEAD_DIM)
            o_ref[0, :, cols] = acc_ref[:, cols] / l_ref[h]


def _fox_schedule(n_q, tq, tk, past):
    qi, kj, last = [], [], []
    for i in range(n_q):
        hi = (past + (i + 1) * tq - 1) // tk
        for j in range(hi + 1):
            qi.append(i)
            kj.append(j)
            last.append(1 if j == hi else 0)
    return (jnp.asarray(np.array(qi, np.int32)), jnp.asarray(np.array(kj, np.int32)),
            jnp.asarray(np.array(last, np.int32)))


def _fox(q_arr, q_map, k_arr, k_map, v_arr, v_map, cq_arr, cq_map, ck_arr, ck_map,
         n_batch, n_q, tq, tk, past, hi=False):
    qi, kj, last = _fox_schedule(n_q, tq, tk, past)
    n_steps = int(qi.shape[0])
    spec = lambda shape, fn, tab: pl.BlockSpec(shape, lambda b, s, qi_r, kj_r, la_r: fn(b, (qi_r if tab == 'q' else kj_r)[s]))
    return pl.pallas_call(
        functools.partial(_fox_kernel, tq=tq, tk=tk, past=past, hi=hi),
        grid_spec=pltpu.PrefetchScalarGridSpec(
            num_scalar_prefetch=3,
            grid=(n_batch, n_steps),
            in_specs=[spec((1, tq, WIDTH), q_map, 'q'),
                      spec((1, tk, WIDTH), k_map, 'k'),
                      spec((1, tk, WIDTH), v_map, 'k'),
                      spec((1, tq, WIDTH), cq_map, 'q'),
                      spec((1, SUBLANES, tk), ck_map, 'k')],
            out_specs=spec((1, tq, WIDTH), lambda b, i: (b, i, 0), 'q'),
            scratch_shapes=[pltpu.VMEM((N_HEADS, tq, HEAD_DIM), F32),
                            pltpu.VMEM((N_HEADS, tq, HEAD_DIM), F32),
                            pltpu.VMEM((tq, WIDTH), F32)]),
        out_shape=jax.ShapeDtypeStruct((n_batch, n_q * tq, WIDTH), F32),
        compiler_params=_cparams(("parallel", "arbitrary")),
        name="fox_attention",
    )(qi, kj, last, q_arr, k_arr, v_arr, cq_arr, ck_arr)


def _gdn_kernel(first_ref, last_ref, seq_ref,
                pre_ref, z_ref, sm_ref, cpast_ref, convw_ref, s0_ref, alog_ref, dtb_ref, ng_ref,
                o_ref, sout_ref, stage_ref, s_ref, *, rows, chunk, hi):
    dot, dot_nt, dot_tn = (functools.partial(f, hi=hi) for f in (_dot, _dot_nt, _dot_tn))
    step = pl.program_id(0)
    halo = SUBLANES
    n_chunks = rows // chunk

    @pl.when(first_ref[step] == 1)
    def _():
        stage_ref[0:halo, :] = cpast_ref[0]
        s_ref[...] = s0_ref[0]

    stage_ref[halo:halo + rows, :] = pre_ref[0]
    conv = stage_ref[halo:halo + rows, :] * convw_ref[CONV_WIDTH - 1:CONV_WIDTH, :]
    for j in range(1, CONV_WIDTH):
        conv = conv + (stage_ref[halo - j:halo - j + rows, :]
                       * convw_ref[CONV_WIDTH - 1 - j:CONV_WIDTH - j, :])
    stage_ref[0:halo, :] = stage_ref[rows:rows + halo, :]
    act = _silu(conv)

    small = sm_ref[0]
    beta_all = _sigmoid(small)
    g_all = -jnp.exp(alog_ref[...]) * _softplus(small + dtb_ref[...])
    r = lax.broadcasted_iota(I32, (rows, rows), 0)
    c = lax.broadcasted_iota(I32, (rows, rows), 1)
    same_chunk = (r // chunk) == (c // chunk)
    incl = same_chunk & (c <= r)
    strict = same_chunk & (c < r)
    eye = (c == r).astype(F32)
    gc_all = jnp.dot(incl.astype(F32), g_all, precision=HIGHEST, preferred_element_type=F32)
    gc_rows = _lanes_to_rows(gc_all, LANE_A)

    heads = range(N_HEADS)
    head_cols = [slice(h * HEAD_DIM, (h + 1) * HEAD_DIM) for h in heads]
    q, k, gc, decay, kb, vb, low = [], [], [], [], [], [], []
    for h in heads:
        qh = act[:, h * HEAD_DIM:(h + 1) * HEAD_DIM]
        kh = act[:, WIDTH + h * HEAD_DIM:WIDTH + (h + 1) * HEAD_DIM]
        vh = act[:, 2 * WIDTH + h * HEAD_DIM:2 * WIDTH + (h + 1) * HEAD_DIM]
        q.append(qh * lax.rsqrt(jnp.sum(qh * qh, axis=-1, keepdims=True) + NORM_EPS) * (HEAD_DIM ** -0.5))
        k.append(kh * lax.rsqrt(jnp.sum(kh * kh, axis=-1, keepdims=True) + NORM_EPS))
        beta = beta_all[:, LANE_B + h:LANE_B + h + 1]
        gc.append(gc_all[:, LANE_A + h:LANE_A + h + 1])
        diff = gc[h] - gc_rows[h:h + 1, :]
        decay.append(jnp.where(incl, jnp.exp(jnp.where(incl, diff, 0.0)), 0.0))
        kb.append(k[h] * beta)
        vb.append(vh * beta)
    for h in heads:
        low.append(jnp.where(strict, dot_nt(kb[h], k[h]) * decay[h], 0.0))
    inv = [eye - low[h] for h in heads]
    pw = [dot(low[h], low[h]) for h in heads]
    n_sq = chunk.bit_length() - 2
    for it in range(n_sq):
        inv = [inv[h] + dot(inv[h], pw[h]) for h in heads]
        if it + 1 < n_sq:
            pw = [dot(pw[h], pw[h]) for h in heads]
    egc = [jnp.exp(gc[h]) for h in heads]
    uw = [dot(inv[h], jnp.concatenate([vb[h], kb[h] * egc[h]], axis=1)) for h in heads]
    intra = [jnp.where(incl, dot_nt(q[h], k[h]) * decay[h], 0.0) for h in heads]
    qd = [q[h] * egc[h] for h in heads]
    g_last = [[gc[h][(g + 1) * chunk - 1:(g + 1) * chunk, :] for g in range(n_chunks)] for h in heads]
    kd = [k[h] * jnp.exp(jnp.concatenate([jnp.broadcast_to(gl, (chunk, 1)) for gl in g_last[h]], axis=0)
                         - gc[h]) for h in heads]
    state = [s_ref[h] for h in heads]
    v_new = [[] for _ in heads]
    for g in range(n_chunks):
        rs = slice(g * chunk, (g + 1) * chunk)
        for h in heads:
            v_new[h].append(uw[h][rs, :HEAD_DIM] - dot(uw[h][rs, HEAD_DIM:], state[h]))
        for h in heads:
            v_rows = jnp.concatenate(
                v_new[h] + [jnp.zeros((rows - (g + 1) * chunk, HEAD_DIM), F32)] * (g + 1 < n_chunks), axis=0)
            o = dot(qd[h][rs, :], state[h]) + dot(intra[h][rs, :], v_rows)
            state[h] = state[h] * jnp.exp(g_last[h][g]) + dot_tn(kd[h][rs, :], v_new[h][g])
            o = (o * lax.rsqrt(jnp.mean(o * o, axis=-1, keepdims=True) + NORM_EPS)
                 * ng_ref[...] * _silu(z_ref[0, rs, head_cols[h]]))
            o_ref[0, rs, head_cols[h]] = o
    for h in heads:
        s_ref[h] = state[h]

    @pl.when(last_ref[step] == 1)
    def _():
        sout_ref[0] = s_ref[...]


def _gdn(u_view, hi, first, last, seq, conv_past, conv_w, s0, alog_row, dtb_row, ng_row):
    n_steps, rows, _ = u_view.shape
    n_seq = s0.shape[0]
    gw = 3 * WIDTH
    return pl.pallas_call(
        functools.partial(_gdn_kernel, rows=rows, chunk=min(rows, GDN_CHUNK), hi=hi),
        grid_spec=pltpu.PrefetchScalarGridSpec(
            num_scalar_prefetch=3,
            grid=(n_steps,),
            in_specs=[pl.BlockSpec((1, rows, gw), lambda s, f, l, q: (s, 0, COL_G // gw)),
                      pl.BlockSpec((1, rows, WIDTH), lambda s, f, l, q: (s, 0, COL_Z // WIDTH)),
                      pl.BlockSpec((1, rows, LANES), lambda s, f, l, q: (s, 0, COL_S // LANES)),
                      pl.BlockSpec((1, SUBLANES, gw), lambda s, f, l, q: (q[s], 0, 0)),
                      pl.BlockSpec((SUBLANES, gw), lambda s, f, l, q: (0, 0)),
                      pl.BlockSpec((1, N_HEADS, HEAD_DIM, HEAD_DIM), lambda s, f, l, q: (q[s], 0, 0, 0)),
                      pl.BlockSpec((1, LANES), lambda s, f, l, q: (0, 0)),
                      pl.BlockSpec((1, LANES), lambda s, f, l, q: (0, 0)),
                      pl.BlockSpec((1, LANES), lambda s, f, l, q: (0, 0))],
            out_specs=[pl.BlockSpec((1, rows, WIDTH), lambda s, f, l, q: (s, 0, 0)),
                       pl.BlockSpec((1, N_HEADS, HEAD_DIM, HEAD_DIM), lambda s, f, l, q: (q[s], 0, 0, 0))],
            scratch_shapes=[pltpu.VMEM((rows + SUBLANES, gw), F32),
                            pltpu.VMEM((N_HEADS, HEAD_DIM, HEAD_DIM), F32)]),
        out_shape=[jax.ShapeDtypeStruct((n_steps, rows, WIDTH), F32),
                   jax.ShapeDtypeStruct((n_seq, N_HEADS, HEAD_DIM, HEAD_DIM), F32)],
        compiler_params=_cparams(("arbitrary",)),
        name="gated_deltanet",
    )(first, last, seq, u_view, u_view, u_view, conv_past, conv_w, s0, alog_row, dtb_row, ng_row)


def _outproj_ln_kernel(xp_ref, ofp_ref, ogp_ref, xs_ref, ofs_ref, ogs_ref, w_ref, wf_ref, g_ref, b_ref,
                       h_ref, ht_ref, *, hi_from):
    def run(hi, x_ref, of_ref, og_ref):
        w = wf_ref if hi else w_ref
        mix = _dot(of_ref[...], w[0:WIDTH, :], hi) + _dot(og_ref[...], w[WIDTH:2 * WIDTH, :], hi)
        h = _layer_norm(DN_ALPHA * x_ref[...] + mix, g_ref[...], b_ref[...])
        h_ref[...] = h
        _store_token_tiles(ht_ref, (), h)

    pl.when(pl.program_id(0) < hi_from)(functools.partial(run, False, xp_ref, ofp_ref, ogp_ref))
    pl.when(pl.program_id(0) >= hi_from)(functools.partial(run, True, xs_ref, ofs_ref, ogs_ref))


def _outproj_ln(prompt, sample, w, g_row, b_row):
    n_p, d = prompt[0].shape
    n = n_p + sample[0].shape[0]
    hi_from = n_p // TM_TOK
    row = lambda i: (i, 0)
    row_p = lambda i: (jnp.minimum(i, hi_from - 1), 0)
    row_s = lambda i: (jnp.maximum(i - hi_from, 0), 0)
    fixed = lambda i: (0, 0)
    group = lambda rows: [pl.BlockSpec((TM_TOK, d), rows), pl.BlockSpec((TM_TOK, WIDTH), rows),
                          pl.BlockSpec((TM_TOK, WIDTH), rows)]
    return pl.pallas_call(
        functools.partial(_outproj_ln_kernel, hi_from=hi_from),
        grid=(n // TM_TOK,),
        in_specs=group(row_p) + group(row_s) + [pl.BlockSpec((2 * WIDTH, d), fixed),
                                                pl.BlockSpec((2 * WIDTH, d), fixed),
                                                pl.BlockSpec((1, d), fixed), pl.BlockSpec((1, d), fixed)],
        out_specs=[pl.BlockSpec((TM_TOK, d), row), pl.BlockSpec((TM_TOK * SUBLANES, LANES), row)],
        out_shape=[jax.ShapeDtypeStruct((n, d), F32), jax.ShapeDtypeStruct((n * SUBLANES, LANES), F32)],
        compiler_params=_cparams(("parallel",)),
        name="out_proj_ln",
    )(*prompt, *sample, w.astype(BF16), w, g_row, b_row)


def _pool_ln_kernel(x_ref, halo_ref, pw_ref, ps_ref, w_ref, g_ref, b_ref, h_ref, ht_ref, stage_ref,
                    lvl_a_ref, lvl_b_ref, *, tm, pos0, zero_first_halo):
    i = pl.program_id(0)
    pad = SUBLANES
    n_rows = POOL_HALO + tm
    for ref in (stage_ref, lvl_a_ref, lvl_b_ref):
        ref[0:pad, :] = jnp.zeros((pad, ref.shape[1]), F32)
    stage_ref[pad:pad + POOL_HALO, :] = halo_ref[0]
    if zero_first_halo:
        @pl.when(i == 0)
        def _():
            stage_ref[pad:pad + POOL_HALO, :] = jnp.zeros((POOL_HALO, stage_ref.shape[1]), F32)
    x = x_ref[...]
    stage_ref[pad + POOL_HALO:pad + n_rows, :] = x
    gdim = x.shape[1] // len(POOL_WINDOWS)
    pos = pos0 + lax.broadcasted_iota(I32, (tm, 1), 0)
    if zero_first_halo:
        pos = pos + i * tm

    def window_sum(cols, win):
        assert win & (win - 1) == 0 and win <= POOL_HALO
        src, src_cols, span, level = stage_ref, cols, 1, 0
        while span < win:
            dst = (lvl_a_ref, lvl_b_ref)[level % 2]
            dst[pad:pad + n_rows, :] = (src[pad:pad + n_rows, src_cols]
                                        + src[pad - span:pad - span + n_rows, src_cols])
            src, src_cols, span, level = dst, slice(None), 2 * span, level + 1
        return src[pad + POOL_HALO:pad + n_rows, src_cols]

    parts = []
    for gi, win in enumerate(POOL_WINDOWS):
        cols = slice(gi * gdim, (gi + 1) * gdim)
        s = window_sum(cols, win)
        cnt = jnp.minimum(pos + 1, win).astype(F32)
        zg = s / cnt - x[:, cols]
        parts.append(_dot(zg, pw_ref[gi]))
    zg = jnp.concatenate(parts, axis=-1) * ps_ref[...]
    mix = _dot(zg, w_ref[...])
    h = _layer_norm(DN_ALPHA * x + mix, g_ref[...], b_ref[...])
    h_ref[...] = h
    _store_token_tiles(ht_ref, (), h)


def _pool_ln_into_kernel(*refs, **kw):
    _pool_ln_kernel(*refs[:7], *refs[9:], **kw)


def _pool_ln(x, x_map, halo_arr, halo_map, n_tiles, tm, pos0, zero_first_halo,
             pw_bf16, ps_row, w_bf16, g_row, b_row, into=None):
    n, d = x.shape
    gdim = d // len(POOL_WINDOWS)
    fixed = lambda i: (0, 0)
    in_specs = [pl.BlockSpec((tm, d), x_map),
                pl.BlockSpec((1, POOL_HALO, d), halo_map),
                pl.BlockSpec((len(POOL_WINDOWS), gdim, gdim), lambda i: (0, 0, 0)),
                pl.BlockSpec((1, d), fixed), pl.BlockSpec((d, d), fixed),
                pl.BlockSpec((1, d), fixed), pl.BlockSpec((1, d), fixed)]
    args = (x, halo_arr, pw_bf16, ps_row, w_bf16, g_row, b_row)
    kw = dict(tm=tm, pos0=pos0, zero_first_halo=zero_first_halo)
    return pl.pallas_call(
        functools.partial(_pool_ln_kernel if into is None else _pool_ln_into_kernel, **kw),
        grid=(n_tiles,),
        in_specs=in_specs + ([] if into is None else [pl.BlockSpec(memory_space=pl.ANY)] * 2),
        out_specs=[pl.BlockSpec((tm, d), x_map),
                   pl.BlockSpec((tm * SUBLANES, LANES), x_map)],
        out_shape=[jax.ShapeDtypeStruct((n, d), F32), jax.ShapeDtypeStruct((n * SUBLANES, LANES), F32)],
        input_output_aliases={} if into is None else {7: 0, 8: 1},
        scratch_shapes=[pltpu.VMEM((SUBLANES + POOL_HALO + tm, d), F32),
                        pltpu.VMEM((SUBLANES + POOL_HALO + tm, gdim), F32),
                        pltpu.VMEM((SUBLANES + POOL_HALO + tm, gdim), F32)],
        compiler_params=_cparams(("arbitrary",)),
        name="pool_mixer_ln",
    )(*args, *(() if into is None else into))


def _router_kernel(h_ref, wr_ref, br_ref, idx_ref, gate_ref, rank_ref, pstart_ref, tab_ref, carry_ref,
                   earlier_ref):
    @pl.when(pl.program_id(0) == 0)
    def _():
        carry_ref[...] = jnp.zeros_like(carry_ref)
        rows = lax.broadcasted_iota(I32, earlier_ref.shape, 0)
        cols = lax.broadcasted_iota(I32, earlier_ref.shape, 1)
        earlier_ref[...] = (cols < rows).astype(BF16)

    tm = h_ref.shape[0]
    lane = lax.broadcasted_iota(I32, (tm, LANES), 1).astype(F32)
    logits = jnp.dot(h_ref[...], wr_ref[...], precision=HIGHEST, preferred_element_type=F32)
    work = jnp.where(lane < N_EXPERTS, logits + br_ref[...], -jnp.inf)
    vals, ids = [], []
    for _ in range(TOP_K):
        m = jnp.max(work, axis=-1, keepdims=True)
        ik = jnp.min(jnp.where(work == m, lane, float(LANES)), axis=-1, keepdims=True)
        vals.append(m)
        ids.append(ik)
        work = jnp.where(lane == ik, -jnp.inf, work)
    exps = [jnp.exp(v - vals[0]) for v in vals]
    denom = exps[0]
    for e in exps[1:]:
        denom = denom + e
    multihot = jnp.zeros((tm, LANES), F32)
    idx_out = jnp.zeros((tm, LANES), F32)
    gate_out = jnp.zeros((tm, LANES), F32)
    for k in range(TOP_K):
        multihot = multihot + (lane == ids[k]).astype(F32)
        idx_out = jnp.where(lane == k, ids[k], idx_out)
        gate_out = jnp.where(lane == k, exps[k] / denom, gate_out)
    before = _dot(earlier_ref[...], multihot) + carry_ref[0:1, :]
    rank_out = jnp.zeros((tm, LANES), F32)
    for k in range(TOP_K):
        rk = jnp.sum(jnp.where(lane == ids[k], before, 0.0), axis=-1, keepdims=True)
        rank_out = jnp.where(lane == k, rk, rank_out)
    idx_ref[...] = jnp.transpose(idx_out)[:SUBLANES].astype(I32)
    rank_ref[...] = jnp.transpose(rank_out)[:SUBLANES].astype(I32)
    gate_ref[...] = gate_out
    total = carry_ref[0:1, :] + jnp.sum(multihot, axis=0, keepdims=True)
    carry_ref[...] = jnp.broadcast_to(total, carry_ref.shape)

    @pl.when(pl.program_id(0) == pl.num_programs(0) - 1)
    def _():
        n_rows = tab_ref.shape[0]
        padded = jnp.floor((total + (MOE_BLOCK - 1)) / MOE_BLOCK) * MOE_BLOCK
        rr = lax.broadcasted_iota(I32, (LANES, LANES), 0)
        cc = lax.broadcasted_iota(I32, (LANES, LANES), 1)
        pend = jnp.dot(jnp.broadcast_to(padded, (SUBLANES, LANES)), (rr <= cc).astype(F32),
                       precision=HIGHEST, preferred_element_type=F32)[0:1, :]
        pstart = pend - padded
        n_used = jnp.max(pend, axis=-1, keepdims=True) / MOE_BLOCK
        elane = lax.broadcasted_iota(I32, (n_rows, LANES), 1)
        blk = jnp.minimum(lax.broadcasted_iota(I32, (n_rows, 1), 0).astype(F32), n_used - 1.0) * MOE_BLOCK
        ends_before = jnp.where((elane < N_EXPERTS) & (pend <= blk), 1.0, 0.0)
        block_e = jnp.minimum(jnp.sum(ends_before, axis=-1, keepdims=True), N_EXPERTS - 1.0)
        mine = elane.astype(F32) == block_e
        filled = jnp.sum(jnp.where(mine, total - (blk - pstart), 0.0), axis=-1, keepdims=True)
        block_valid = jnp.clip(filled, 0.0, float(MOE_BLOCK))
        group_end = jnp.sum(jnp.where(mine, pend, 0.0), axis=-1, keepdims=True)
        next_e = jnp.sum(jnp.where((elane < N_EXPERTS) & (pend <= group_end), 1.0, 0.0), axis=-1, keepdims=True)
        tab = jnp.where(elane == 0, block_e, jnp.where(elane == 1, block_valid,
                                                       jnp.where(elane == 2, n_used,
                                                                 jnp.where(elane == 3, next_e, 0.0))))
        tab_ref[...] = tab.astype(I32)
        pstart_ref[...] = jnp.broadcast_to(pstart, pstart_ref.shape).astype(I32)


def _router(h, wr_pad, br_row, n_blocks):
    n, d = h.shape
    row = lambda i: (i, 0)
    col = lambda i: (0, i)
    fixed = lambda i: (0, 0)
    tab_rows = -(-n_blocks // SUBLANES) * SUBLANES
    return pl.pallas_call(
        _router_kernel,
        grid=(n // TM_TOK,),
        in_specs=[pl.BlockSpec((TM_TOK, d), row), pl.BlockSpec((d, LANES), fixed),
                  pl.BlockSpec((1, LANES), fixed)],
        out_specs=[pl.BlockSpec((SUBLANES, TM_TOK), col), pl.BlockSpec((TM_TOK, LANES), row),
                   pl.BlockSpec((SUBLANES, TM_TOK), col), pl.BlockSpec((SUBLANES, LANES), fixed),
                   pl.BlockSpec((tab_rows, LANES), fixed)],
        out_shape=[jax.ShapeDtypeStruct((SUBLANES, n), I32), jax.ShapeDtypeStruct((n, LANES), F32),
                   jax.ShapeDtypeStruct((SUBLANES, n), I32), jax.ShapeDtypeStruct((SUBLANES, LANES), I32),
                   jax.ShapeDtypeStruct((tab_rows, LANES), I32)],
        scratch_shapes=[pltpu.VMEM((SUBLANES, LANES), F32), pltpu.VMEM((TM_TOK, TM_TOK), BF16)],
        compiler_params=_cparams(("arbitrary",)),
        name="moe_router",
    )(h, wr_pad, br_row)


def _expert_kernel(be_ref, nu_ref, valid_ref, next_ref, xb_ref, wup_ref, bup_ref, wdn_ref, bdn_ref, yb_ref,
                   wup_f32_ref, wdn_f32_ref, wup_bf_ref, wdn_bf_ref, wsem, *, layer):
    b = pl.program_id(0)

    def fetch(e):
        return (pltpu.make_async_copy(wup_ref.at[layer, e], wup_f32_ref, wsem.at[0]),
                pltpu.make_async_copy(wdn_ref.at[layer, e], wdn_f32_ref, wsem.at[1]))

    @pl.when(b == 0)
    def _():
        for copy in fetch(be_ref[0]):
            copy.start()

    @pl.when(b < nu_ref[0])
    def _():
        @pl.when((b == 0) | (be_ref[b] != be_ref[jnp.maximum(b - 1, 0)]))
        def _():
            for copy in fetch(be_ref[b]):
                copy.wait()
            wup_bf_ref[...] = wup_f32_ref[...].astype(BF16)
            wdn_bf_ref[...] = wdn_f32_ref[...].astype(BF16)

            @pl.when(next_ref[b] < N_EXPERTS)
            def _():
                for copy in fetch(next_ref[b]):
                    copy.start()

        d_exp = wdn_f32_ref.shape[0]

        def run(n_rows):
            x = _load_token_tiles(xb_ref, (), n_rows)
            row = lax.broadcasted_iota(I32, (n_rows, 1), 0)
            x = jnp.where(row < valid_ref[b], x, 0.0)
            hu = jnp.dot(x.astype(BF16), wup_bf_ref[...], preferred_element_type=F32) + bup_ref[0, 0]
            glu = jnp.minimum(hu[:, :d_exp], SWIGLU_LIMIT)
            lin = jnp.clip(hu[:, d_exp:], -SWIGLU_LIMIT, SWIGLU_LIMIT)
            a = glu * _sigmoid(SWIGLU_ALPHA * glu) * (lin + 1.0)
            y = jnp.dot(a.astype(BF16), wdn_bf_ref[...], preferred_element_type=F32) + bdn_ref[0, 0]
            if n_rows < MOE_BLOCK:
                y = jnp.concatenate([y, jnp.zeros((MOE_BLOCK - n_rows, y.shape[1]), F32)], axis=0)
            _store_token_tiles(yb_ref, (), y)

        half = MOE_BLOCK // 2
        pl.when(valid_ref[b] > half)(functools.partial(run, MOE_BLOCK))
        pl.when(valid_ref[b] <= half)(functools.partial(run, half))

    @pl.when(b >= nu_ref[0])
    def _():
        yb_ref[...] = jnp.zeros_like(yb_ref)


def _experts(xb, block_e, n_used, block_valid, next_e, layer, w_up, b_up, w_dn, b_dn):
    d = SUBLANES * LANES
    rows = MOE_BLOCK * SUBLANES
    n_blocks = xb.shape[0] // rows
    d_up = w_up.shape[3]
    d_exp = w_dn.shape[2]
    bsel = lambda b, be, nu, va, ne: (layer, be[b], 0, 0)
    return pl.pallas_call(
        functools.partial(_expert_kernel, layer=layer),
        grid_spec=pltpu.PrefetchScalarGridSpec(
            num_scalar_prefetch=4,
            grid=(n_blocks,),
            in_specs=[pl.BlockSpec((rows, LANES), lambda b, be, nu, va, ne: (jnp.minimum(b, nu[0] - 1), 0)),
                      pl.BlockSpec(memory_space=pl.ANY),
                      pl.BlockSpec((1, 1, 1, d_up), bsel),
                      pl.BlockSpec(memory_space=pl.ANY),
                      pl.BlockSpec((1, 1, 1, d), bsel)],
            out_specs=pl.BlockSpec((rows, LANES), lambda b, be, nu, va, ne: (b, 0)),
            scratch_shapes=[pltpu.VMEM((d, d_up), F32), pltpu.VMEM((d_exp, d), F32),
                            pltpu.VMEM((d, d_up), BF16), pltpu.VMEM((d_exp, d), BF16),
                            pltpu.SemaphoreType.DMA((2,))]),
        out_shape=jax.ShapeDtypeStruct(xb.shape, F32),
        compiler_params=_cparams(("arbitrary",)),
        name="moe_experts",
    )(block_e, n_used, block_valid, next_e, xb, w_up, b_up, w_dn, b_dn)


def _sc_gather_tiles(table, idx):
    m = idx.shape[0]
    mesh = plsc.VectorSubcoreMesh(core_axis_name="core", subcore_axis_name="subcore")

    @functools.partial(pl.kernel, out_type=jax.ShapeDtypeStruct((m, SUBLANES, LANES), table.dtype), mesh=mesh)
    def gather(table_hbm, idx_hbm, out_hbm):
        def window(idx_vmem, out_vmem):
            pltpu.sync_copy(table_hbm.at[idx_vmem.at[0, pl.ds(0, SC_WINDOW)]], out_vmem)

        pltpu.emit_pipeline(
            window,
            grid=(m // SC_WINDOW,),
            in_specs=[pl.BlockSpec((1, LANES), lambda i: (i, 0))],
            out_specs=[pl.BlockSpec((SC_WINDOW, SUBLANES, LANES), lambda i: (i, 0, 0))],
            core_axis_name=("core", "subcore"),
            dimension_semantics=(pltpu.PARALLEL,),
        )(idx_hbm, out_hbm)

    idx_rows = jnp.pad(idx.reshape(m // SC_WINDOW, SC_WINDOW), ((0, 0), (0, LANES - SC_WINDOW)))
    return gather(table, idx_rows)


def _sc_scatter_tiles(tiles, idx_by_choice, n_out):
    n = tiles.shape[0]
    mesh = plsc.VectorSubcoreMesh(core_axis_name="core", subcore_axis_name="subcore")

    @functools.partial(pl.kernel, out_type=jax.ShapeDtypeStruct((n_out, SUBLANES, LANES), tiles.dtype),
                       mesh=mesh)
    def scatter(tiles_hbm, *refs):
        idx_hbm, out_hbm = refs[:TOP_K], refs[TOP_K]

        def window(tiles_vmem, *idx_vmem):
            for k in range(TOP_K):
                pltpu.sync_copy(tiles_vmem, out_hbm.at[idx_vmem[k].at[0, pl.ds(0, SC_WINDOW)]])

        pltpu.emit_pipeline(
            window,
            grid=(n // SC_WINDOW,),
            in_specs=[pl.BlockSpec((SC_WINDOW, SUBLANES, LANES), lambda i: (i, 0, 0))]
                     + [pl.BlockSpec((1, LANES), lambda i: (i, 0))] * TOP_K,
            out_specs=[],
            core_axis_name=("core", "subcore"),
            dimension_semantics=(pltpu.PARALLEL,),
        )(tiles_hbm, *idx_hbm)

    idx_rows = jnp.pad(idx_by_choice.reshape(TOP_K, n // SC_WINDOW, SC_WINDOW),
                       ((0, 0), (0, 0), (0, LANES - SC_WINDOW)))
    return scatter(tiles, *[idx_rows[k] for k in range(TOP_K)])


def _combine_kernel(h_ref, y0_ref, y1_ref, y2_ref, y3_ref, gate_ref, pp_ref, ps_ref, wpg_ref, wpp_ref,
                    g_ref, b_ref, *outs, n_ptiles):
    outp_ref, outs_ref = outs[0], outs[-1]
    tm = h_ref.shape[0]
    gate = gate_ref[...]
    moe = _load_token_tiles(y0_ref, (), tm) * gate[:, 0:1]
    for k, y_ref in enumerate((y1_ref, y2_ref, y3_ref), start=1):
        moe = moe + _load_token_tiles(y_ref, (), tm) * gate[:, k:k + 1]
    h2 = _layer_norm(DN_ALPHA * h_ref[...] + moe, g_ref[...], b_ref[...])
    embed_gate = _sigmoid(_dot(h2, wpg_ref[...]))

    def finish(p_ref, out_ref):
        out_ref[...] = h2 + embed_gate * _dot(p_ref[...], wpp_ref[...])

    is_prompt = pl.program_id(0) < n_ptiles
    pl.when(is_prompt)(functools.partial(finish, pp_ref, outp_ref))
    pl.when(jnp.logical_not(is_prompt))(functools.partial(finish, ps_ref, outs_ref))


def _combine(h, y, gate, p_prompt, p_sample, layer, n_p, split, wpg_bf16, wpp_bf16, g_row, b_row):
    n, d = h.shape
    e = p_prompt.shape[1]
    n_tiles = n // TM_TOK
    n_ptiles = n_p // TM_TOK
    n_stiles = (n - n_p) // TM_TOK
    row = lambda i: (i, 0)
    row_p = lambda i: (jnp.minimum(i, n_ptiles - 1), 0)
    row_s = lambda i: (jnp.maximum(i - n_ptiles, 0), 0)
    fixed = lambda i: (0, 0)
    choice = lambda k: pl.BlockSpec((TM_TOK * SUBLANES, LANES), lambda i: (k * n_tiles + i, 0))
    if split:
        out_specs = [pl.BlockSpec((TM_TOK, d), row_p), pl.BlockSpec((TM_TOK, d), row_s)]
        out_shape = [jax.ShapeDtypeStruct((n_p, d), F32), jax.ShapeDtypeStruct((n - n_p, d), F32)]
    else:
        out_specs = pl.BlockSpec((TM_TOK, d), row)
        out_shape = jax.ShapeDtypeStruct((n, d), F32)
    return pl.pallas_call(
        functools.partial(_combine_kernel, n_ptiles=n_ptiles),
        grid=(n_tiles,),
        in_specs=[pl.BlockSpec((TM_TOK, d), row)] + [choice(k) for k in range(TOP_K)]
                 + [pl.BlockSpec((TM_TOK, LANES), row),
                    pl.BlockSpec((TM_TOK, e), lambda i: (layer * n_ptiles + row_p(i)[0], 0)),
                    pl.BlockSpec((TM_TOK, e), lambda i: (layer * n_stiles + row_s(i)[0], 0)),
                    pl.BlockSpec((d, d), fixed), pl.BlockSpec((e, d), fixed), pl.BlockSpec((1, d), fixed),
                    pl.BlockSpec((1, d), fixed)],
        out_specs=out_specs,
        out_shape=out_shape,
        compiler_params=_cparams(("arbitrary",)),
        name="moe_combine_ln_embed",
    )(h, y, y, y, y, gate, p_prompt, p_sample, wpg_bf16, wpp_bf16, g_row, b_row)


def _layer_tail(h, ht, p_prompt, p_sample, layer, n_p, split, g2, b2, w_r, b_r, w_up, b_up, w_dn, b_dn,
                w_pg, w_pp):
    n, d = h.shape
    wr_pad = jnp.pad(w_r, ((0, 0), (0, LANES - N_EXPERTS)))
    br_row = jnp.pad(b_r, (0, LANES - N_EXPERTS))[None]
    n_asg = n * TOP_K
    n_blocks = n_asg // MOE_BLOCK + N_EXPERTS
    assert n_asg % MOE_BLOCK == 0 and n % SC_WINDOW == 0
    n_slots = n_blocks * MOE_BLOCK
    idx, gate, rank, pstart, tab = _router(h, wr_pad, br_row, n_blocks)
    experts = jnp.arange(N_EXPERTS, dtype=I32)[:, None, None]
    group_start = jnp.sum(jnp.where(idx[None, :TOP_K] == experts, pstart[0, :N_EXPERTS, None, None], 0), axis=0)
    slot_by_choice = group_start + rank[:TOP_K]
    xb = _sc_scatter_tiles(ht.reshape(n, SUBLANES, LANES), slot_by_choice, n_slots)
    yb = _experts(xb.reshape(n_slots * SUBLANES, LANES), tab[:n_blocks, 0], tab[0, 2:3], tab[:n_blocks, 1],
                  tab[:n_blocks, 3], layer, w_up, b_up[:, :, None, :], w_dn, b_dn[:, :, None, :])
    y = _sc_gather_tiles(yb.reshape(n_slots, SUBLANES, LANES), slot_by_choice.reshape(-1))
    return _combine(h, y.reshape(n_asg * SUBLANES, LANES), gate, p_prompt, p_sample, layer, n_p, split,
                    w_pg.astype(BF16), w_pp.astype(BF16), g2[None], b2[None])


def _lane_row(v, lane0):
    return jnp.zeros((1, LANES), F32).at[0, lane0:lane0 + v.shape[0]].set(v.astype(F32))


def kernel(x_prompt, x_sample, cache_fox_k, cache_fox_v, cache_fox_logf, state_gdn, state_gdn_conv,
           cache_pool, p_prompt, p_sample, w_in_ab, b_fgate, gdn_a_log, gdn_dt_bias, gdn_conv_w,
           gdn_norm_g, w_out_ab, pool_w, pool_scale, w_out_pool, ln1_g, ln1_b, ln2_g, ln2_b,
           w_router, b_router, w_expert_up, b_expert_up, w_expert_down, b_expert_down,
           w_ple_gate, w_ple_proj):
    n_pb, seq, d = x_prompt.shape
    n_sb, dseq, _ = x_sample.shape
    past = cache_fox_k.shape[2]
    assert n_pb == 1 and dseq == CHUNK and past % dseq == 0 and seq % TQ == 0
    assert d == SUBLANES * LANES
    n_p = n_pb * seq
    n_s = n_sb * dseq
    n = n_p + n_s
    assert n_p % TM_TOK == 0 and n_s % TM_TOK == 0
    n_layers = p_prompt.shape[0]
    pp_all = p_prompt.reshape(n_layers * n_p, -1)
    ps_all = p_sample.reshape(n_layers * n_s, -1)

    def tail(h, ht, i, split):
        return _layer_tail(h, ht, pp_all, ps_all, i, n_p, split, ln2_g[i], ln2_b[i], w_router[i], b_router[i],
                           w_expert_up, b_expert_up, w_expert_down, b_expert_down, w_ple_gate[i],
                           w_ple_proj[i])

    w_in = w_in_ab[0]
    n_small = 3 * N_HEADS
    ff0 = 3 * WIDTH
    gq0 = ff0 + N_HEADS
    ga0 = gq0 + 4 * WIDTH
    w_small = jnp.concatenate([w_in[:, ff0:gq0], w_in[:, ga0:ga0 + 2 * N_HEADS],
                               jnp.zeros((d, LANES - n_small), F32)], axis=1)
    w_all = jnp.concatenate([w_in[:, :ff0], w_in[:, gq0:ga0], w_small], axis=1)
    bf_row = _lane_row(b_fgate[0], LANE_F)
    up = _proj(x_prompt.reshape(n_p, d), w_all.astype(BF16), bf_row, TM_PROJ, U_COLS, False)
    us = _proj(x_sample.reshape(n_s, d), w_all, bf_row, TM_TOK, LANES, True)

    u3 = up[None]
    us3 = us.reshape(n_sb, dseq, U_COLS)
    cq_p, ck_p = _cumsum(u3, 1, n_p, CUMSUM_TILE, COL_S // LANES)
    lf_s = jnp.concatenate(
        [jnp.pad(cache_fox_logf[0].astype(F32), ((0, 0), (0, 0), (0, LANES - N_HEADS))),
         us3[:, :, COL_S:]], axis=1)
    cq_s, ck_s = _cumsum(lf_s, n_sb, past + dseq, past + dseq, 0)

    of_p = _fox(u3, lambda b, i: (0, i, COL_Q // WIDTH), u3, lambda b, j: (0, j, COL_K // WIDTH),
                u3, lambda b, j: (0, j, COL_V // WIDTH), cq_p, lambda b, i: (0, i, 0),
                ck_p, lambda b, j: (0, 0, j), 1, n_p // TQ, TQ, TK, 0)
    k_all = jnp.concatenate([cache_fox_k[0].reshape(n_sb, past, WIDTH), us3[:, :, COL_K:COL_K + WIDTH]], axis=1)
    v_all = jnp.concatenate([cache_fox_v[0].reshape(n_sb, past, WIDTH), us3[:, :, COL_V:COL_V + WIDTH]], axis=1)
    of_s = _fox(us3, lambda b, i: (b, 0, COL_Q // WIDTH), k_all, lambda b, j: (b, 0, 0),
                v_all, lambda b, j: (b, 0, 0), cq_s, lambda b, i: (b, past // dseq, 0),
                ck_s, lambda b, j: (b, 0, 0), n_sb, 1, dseq, past + dseq, past, hi=True)

    gw = 3 * WIDTH
    conv_w = jnp.pad(gdn_conv_w[0], ((0, SUBLANES - CONV_WIDTH), (0, 0)))
    gdn_args = (conv_w, _lane_row(gdn_a_log[0], LANE_A), _lane_row(gdn_dt_bias[0], LANE_A),
                gdn_norm_g[0][None])
    n_pstep = n_p // GDN_ROWS
    ends = lambda k: (jnp.asarray((np.arange(k) == 0).astype(np.int32)),
                      jnp.asarray((np.arange(k) == k - 1).astype(np.int32)))
    og_p, st_p = _gdn(up.reshape(n_pstep, GDN_ROWS, U_COLS), False, *ends(n_pstep),
                      jnp.zeros((n_pstep,), I32), jnp.zeros((1, SUBLANES, gw), F32), gdn_args[0],
                      jnp.zeros((1, N_HEADS, HEAD_DIM, HEAD_DIM), F32), *gdn_args[1:])
    conv_past = jnp.pad(state_gdn_conv[0].astype(F32), ((0, 0), (SUBLANES - (CONV_WIDTH - 1), 0), (0, 0)))
    ones = jnp.ones((n_sb,), I32)
    og_s, st_s = _gdn(us3, True, ones, ones, jnp.arange(n_sb, dtype=I32), conv_past, gdn_args[0],
                      state_gdn[0].astype(F32), *gdn_args[1:])
    h, ht = _outproj_ln((x_prompt.reshape(n_p, d), of_p.reshape(n_p, WIDTH), og_p.reshape(n_p, WIDTH)),
                        (x_sample.reshape(n_s, d), of_s.reshape(n_s, WIDTH), og_s.reshape(n_s, WIDTH)),
                        w_out_ab[0], ln1_g[0][None], ln1_b[0][None])
    x1 = tail(h, ht, 0, False)

    pool_args = (pool_w[0].astype(BF16), pool_scale[0][None], w_out_pool[0].astype(BF16),
                 ln1_g[1][None], ln1_b[1][None])
    ratio = TM_TOK // POOL_HALO
    x1_halo = x1.reshape(n // POOL_HALO, POOL_HALO, d)
    h_pool = _pool_ln(x1, lambda i: (i, 0), x1_halo, lambda i: (jnp.maximum(i * ratio - 1, 0), 0, 0),
                      n_p // TM_TOK, TM_TOK, 0, True, *pool_args)
    cache16 = jnp.pad(cache_pool[0].astype(F32), ((0, 0), (POOL_HALO - POOL_STATE, 0), (0, 0)))
    h, ht = _pool_ln(x1, lambda i: (n_p // dseq + i, 0), cache16, lambda i: (i, 0, 0),
                     n_sb, dseq, past, False, *pool_args, into=h_pool)
    x2_p, x2_s = tail(h, ht, 1, True)

    return (x2_p.reshape(n_pb, seq, d), x2_s.reshape(n_sb, dseq, d),
            up[:, COL_K:COL_K + WIDTH].reshape(1, n_pb, seq, N_HEADS, HEAD_DIM),
            up[:, COL_V:COL_V + WIDTH].reshape(1, n_pb, seq, N_HEADS, HEAD_DIM),
            up[:, COL_S:COL_S + N_HEADS].reshape(1, n_pb, seq, N_HEADS),
            st_p.reshape(1, n_pb, N_HEADS, HEAD_DIM, HEAD_DIM),
            up[seq - (CONV_WIDTH - 1):, COL_G:COL_G + gw].reshape(1, n_pb, CONV_WIDTH - 1, gw),
            x1[n_p - POOL_STATE:n_p].reshape(1, n_pb, POOL_STATE, d),
            us[:, COL_K:COL_K + WIDTH].reshape(1, n_sb, dseq, N_HEADS, HEAD_DIM),
            us[:, COL_V:COL_V + WIDTH].reshape(1, n_sb, dseq, N_HEADS, HEAD_DIM),
            us[:, COL_S:COL_S + N_HEADS].reshape(1, n_sb, dseq, N_HEADS),
            st_s.reshape(1, n_sb, N_HEADS, HEAD_DIM, HEAD_DIM),
            us[:, COL_G:COL_G + gw].reshape(n_sb, dseq, gw)[:, dseq - (CONV_WIDTH - 1):].reshape(
                1, n_sb, CONV_WIDTH - 1, gw),
            x1[n_p:].reshape(n_sb, dseq, d)[:, dseq - POOL_STATE:].reshape(1, n_sb, POOL_STATE, d))
```
